```python
import math
import jax, jax.numpy as jnp
from jax import lax
import numpy as np

D_MODEL = 1024
BATCH = 8
SEQ = 8192
DEPTH = 4

N_MEM = 256
EPS = 1e-6

MLA_HEADS = 8
MLA_NOPE = 64
MLA_ROPE = 32
MLA_V = 64
MLA_Q_RANK = 256
MLA_KV_RANK = 128
ROPE_THETA = 10000.0
Q_BLOCK = 128

SSM_HEADS = 4
SSM_HEADDIM = 64
SSM_INNER = SSM_HEADS * SSM_HEADDIM
SSM_GROUPS = 2
SSM_STATE = 64
SSM_CONV = 4
SSM_CHUNK = 128
SSM_XBC = SSM_INNER + 2 * SSM_GROUPS * SSM_STATE

LRU_WIDTH = 256
LRU_BLOCKS = 4
LRU_BLOCK = LRU_WIDTH // LRU_BLOCKS
LRU_CONV = 4
LRU_C = 8.0

MEM_HEADS = 4
MEM_HEAD_DIM = D_MODEL // MEM_HEADS
D_FF = 4 * D_MODEL

MLA_OUT = MLA_HEADS * MLA_V
MIX_WIDTH = MLA_OUT + SSM_INNER + LRU_WIDTH
IN_SPLITS = (
    MLA_Q_RANK,
    MLA_KV_RANK,
    MLA_ROPE,
    SSM_INNER,
    SSM_INNER,
    SSM_GROUPS * SSM_STATE,
    SSM_GROUPS * SSM_STATE,
    SSM_HEADS,
    LRU_WIDTH,
    LRU_WIDTH,
)
D_IN_PROJ = sum(IN_SPLITS)

kernel_name = "hymba_mla_ssd_rglru_trunk"


def rms_norm(x, g):
    xf = x.astype(jnp.float32)
    y = xf * lax.rsqrt(jnp.mean(xf * xf, axis=-1, keepdims=True) + EPS)
    return (y * g.astype(jnp.float32)).astype(x.dtype)


def split_cols(u, sizes):
    offsets = []
    acc = 0
    for s in sizes[:-1]:
        acc += s
        offsets.append(acc)
    return jnp.split(u, offsets, axis=-1)


def causal_depthwise_conv(x, w, b):
    k_width = w.shape[0]
    s = x.shape[1]
    xp = jnp.pad(x, ((0, 0), (k_width - 1, 0), (0, 0)))
    y = b
    for k in range(k_width):
        y = y + xp[:, k:k + s, :] * w[k]
    return y


def rope_tables(positions):
    half = MLA_ROPE // 2
    inv_freq = ROPE_THETA ** (-jnp.arange(half, dtype=jnp.float32) * 2.0 / MLA_ROPE)
    ang = positions.astype(jnp.float32)[..., None] * inv_freq
    return jnp.cos(ang)[:, :, None, :], jnp.sin(ang)[:, :, None, :]


def apply_rope(x, cos, sin):
    xf = x.astype(jnp.float32)
    half = MLA_ROPE // 2
    x1, x2 = xf[..., :half], xf[..., half:]
    out = jnp.concatenate([x1 * cos - x2 * sin, x2 * cos + x1 * sin], axis=-1)
    return out.astype(x.dtype)


def causal_block_attention(q, k, v, scale):
    b, s, h, dk = q.shape
    dv = v.shape[-1]
    nb = s // Q_BLOCK
    qb = q.reshape(b, nb, Q_BLOCK, h, dk).transpose(1, 0, 2, 3, 4)
    kpos = jnp.arange(s)

    def one_block(args):
        qi, i = args
        sc = jnp.einsum('bqhd,bkhd->bhqk', qi, k).astype(jnp.float32) * scale
        qpos = i * Q_BLOCK + jnp.arange(Q_BLOCK)
        mask = kpos[None, :] <= qpos[:, None]
        sc = jnp.where(mask[None, None], sc, -jnp.inf)
        p = jax.nn.softmax(sc, axis=-1).astype(v.dtype)
        return jnp.einsum('bhqk,bkhd->bqhd', p, v)

    out = lax.map(one_block, (qb, jnp.arange(nb)))
    return out.transpose(1, 0, 2, 3, 4).reshape(b, s, h * dv)


def mla_group(c_q, c_kv, k_rope_raw, cos, sin, q_norm_g, kv_norm_g, w_uq, w_ukv):
    b, s, _ = c_q.shape
    q = (rms_norm(c_q, q_norm_g) @ w_uq).reshape(b, s, MLA_HEADS, MLA_NOPE + MLA_ROPE)
    q_nope, q_rope = q[..., :MLA_NOPE], q[..., MLA_NOPE:]
    q = jnp.concatenate([q_nope, apply_rope(q_rope, cos, sin)], axis=-1)
    kv = (rms_norm(c_kv, kv_norm_g) @ w_ukv).reshape(b, s, MLA_HEADS, MLA_NOPE + MLA_V)
    k_nope, v = kv[..., :MLA_NOPE], kv[..., MLA_NOPE:]
    k_rope = apply_rope(k_rope_raw[:, :, None, :], cos, sin)
    k = jnp.concatenate([k_nope, jnp.broadcast_to(k_rope, (b, s, MLA_HEADS, MLA_ROPE))], axis=-1)
    scale = 1.0 / math.sqrt(MLA_NOPE + MLA_ROPE)
    return causal_block_attention(q, k, v, scale)


def segsum(a):
    t = a.shape[-1]
    a_rep = jnp.broadcast_to(a[..., :, None], a.shape + (t,))
    strict = jnp.tril(jnp.ones((t, t), dtype=bool), -1)
    cs = jnp.cumsum(jnp.where(strict, a_rep, 0.0), axis=-2)
    incl = jnp.tril(jnp.ones((t, t), dtype=bool), 0)
    return jnp.where(incl, cs, -jnp.inf)


def ssd_chunked(x, dt, a_neg, bm, cm):
    b, s, h, p = x.shape
    n = bm.shape[-1]
    nc = s // SSM_CHUNK
    xdt = (x.astype(jnp.float32) * dt[..., None]).reshape(b, nc, SSM_CHUNK, h, p)
    bc = bm.astype(jnp.float32).reshape(b, nc, SSM_CHUNK, h, n)
    cc = cm.astype(jnp.float32).reshape(b, nc, SSM_CHUNK, h, n)
    a = (dt * a_neg).reshape(b, nc, SSM_CHUNK, h).transpose(0, 3, 1, 2)
    a_cum = jnp.cumsum(a, axis=-1)
    lmat = jnp.exp(segsum(a))
    scores = jnp.einsum('bclhn,bcshn->bhcls', cc, bc) * lmat
    y_diag = jnp.einsum('bhcls,bcshp->bclhp', scores, xdt)
    decay_states = jnp.exp(a_cum[..., -1:] - a_cum)
    states = jnp.einsum('bclhn,bhcl,bclhp->bchpn', bc, decay_states, xdt)
    chunk_tot = jnp.pad(a_cum[..., -1], ((0, 0), (0, 0), (1, 0)))
    decay_chunk = jnp.exp(segsum(chunk_tot))
    states = jnp.concatenate([jnp.zeros_like(states[:, :1]), states], axis=1)
    states = jnp.einsum('bhzc,bchpn->bzhpn', decay_chunk, states)[:, :-1]
    y_off = jnp.einsum('bclhn,bchpn,bhcl->bclhp', cc, states, jnp.exp(a_cum))
    return (y_diag + y_off).reshape(b, s, h, p)


def mamba2_group(z, xs, bs, cs, dt_raw, conv_w, conv_b, dt_bias, a_log, d_skip, norm_g):
    b, s, _ = xs.shape
    xbc = jax.nn.silu(causal_depthwise_conv(jnp.concatenate([xs, bs, cs], axis=-1), conv_w, conv_b))
    xs, bs, cs = split_cols(xbc, (SSM_INNER, SSM_GROUPS * SSM_STATE, SSM_GROUPS * SSM_STATE))
    dt = jax.nn.softplus(dt_raw.astype(jnp.float32) + dt_bias.astype(jnp.float32))
    a_neg = -jnp.exp(a_log.astype(jnp.float32))
    xh = xs.reshape(b, s, SSM_HEADS, SSM_HEADDIM)
    rep = SSM_HEADS // SSM_GROUPS
    bm = jnp.repeat(bs.reshape(b, s, SSM_GROUPS, SSM_STATE), rep, axis=2)
    cm = jnp.repeat(cs.reshape(b, s, SSM_GROUPS, SSM_STATE), rep, axis=2)
    y = ssd_chunked(xh, dt, a_neg, bm, cm) + d_skip.astype(jnp.float32)[:, None] * xh.astype(jnp.float32)
    y = y.reshape(b, s, SSM_INNER) * jax.nn.silu(z.astype(jnp.float32))
    yg = y.reshape(b, s, SSM_GROUPS, SSM_INNER // SSM_GROUPS)
    yg = yg * lax.rsqrt(jnp.mean(yg * yg, axis=-1, keepdims=True) + EPS)
    return (yg.reshape(b, s, SSM_INNER) * norm_g.astype(jnp.float32)).astype(xs.dtype)


def lru_combine(c1, c2):
    a1, b1 = c1
    a2, b2 = c2
    return a1 * a2, a2 * b1 + b2


def rglru_group(xr, gate, conv_w, conv_b, w_a, b_a, w_i, b_i, lam):
    b, s, _ = xr.shape
    xr = causal_depthwise_conv(xr, conv_w, conv_b)
    xb = xr.reshape(b, s, LRU_BLOCKS, LRU_BLOCK)
    r = jax.nn.sigmoid(jnp.einsum('bsnd,nde->bsne', xb, w_a) + b_a).reshape(b, s, LRU_WIDTH)
    i = jax.nn.sigmoid(jnp.einsum('bsnd,nde->bsne', xb, w_i) + b_i).reshape(b, s, LRU_WIDTH)
    log_a = -LRU_C * r.astype(jnp.float32) * jax.nn.softplus(-lam.astype(jnp.float32))
    a = jnp.exp(log_a)
    u = jnp.sqrt(-jnp.expm1(2.0 * log_a)) * (i * xr).astype(jnp.float32)
    _, h = lax.associative_scan(lru_combine, (a, u), axis=1)
    return (h * jax.nn.gelu(gate.astype(jnp.float32), approximate=True)).astype(xr.dtype)


def memory_cross_attention(h, m, w_mq, w_mk, w_mv, w_mo):
    b, s, _ = h.shape
    q = (h @ w_mq).reshape(b, s, MEM_HEADS, MEM_HEAD_DIM)
    k = (m @ w_mk).reshape(b, N_MEM, MEM_HEADS, MEM_HEAD_DIM)
    v = (m @ w_mv).reshape(b, N_MEM, MEM_HEADS, MEM_HEAD_DIM)
    sc = jnp.einsum('bshd,bmhd->bhsm', q, k).astype(jnp.float32) / math.sqrt(MEM_HEAD_DIM)
    p = jax.nn.softmax(sc, axis=-1).astype(v.dtype)
    o = jnp.einsum('bhsm,bmhd->bshd', p, v).reshape(b, s, D_MODEL)
    return o @ w_mo


def _fwd_setup_inputs(seed: int = 0) -> dict:
    key = jax.random.key(seed)
    ks = iter(jax.random.split(key, 64))
    L = DEPTH

    def nrm(shape, scale):
        return jax.random.normal(next(ks), shape, jnp.float32) * scale

    def gain(shape):
        return 1.0 + nrm(shape, 0.02)

    x = nrm((BATCH, SEQ, D_MODEL), 1.0)
    mem = nrm((BATCH, N_MEM, D_MODEL), 1.0)
    offset = jax.random.randint(next(ks), (BATCH, 1), 0, 1024, dtype=jnp.int32)
    positions = offset + jnp.arange(SEQ, dtype=jnp.int32)[None, :]

    dt0 = jnp.exp(jax.random.uniform(next(ks), (L, SSM_HEADS), jnp.float32, math.log(1e-3), math.log(1e-1)))
    ssm_dt_bias = dt0 + jnp.log(-jnp.expm1(-dt0))
    ssm_a_log = jnp.log(jax.random.uniform(next(ks), (L, SSM_HEADS), jnp.float32, 1.0, 16.0))
    a_c = jax.random.uniform(next(ks), (L, LRU_WIDTH), jnp.float32, 0.9, 0.999)
    sig = a_c ** (1.0 / LRU_C)
    lru_lambda = jnp.log(sig) - jnp.log1p(-sig)

    return {
        "x": x,
        "mem": mem,
        "positions": positions,
        "mix_norm_g": gain((L, D_MODEL)),
        "w_in": nrm((L, D_MODEL, D_IN_PROJ), D_MODEL ** -0.5),
        "mla_q_norm_g": gain((L, MLA_Q_RANK)),
        "mla_kv_norm_g": gain((L, MLA_KV_RANK)),
        "mla_w_uq": nrm((L, MLA_Q_RANK, MLA_HEADS * (MLA_NOPE + MLA_ROPE)), MLA_Q_RANK ** -0.5),
        "mla_w_ukv": nrm((L, MLA_KV_RANK, MLA_HEADS * (MLA_NOPE + MLA_V)), MLA_KV_RANK ** -0.5),
        "mla_out_g": gain((L, MLA_OUT)),
        "ssm_conv_w": nrm((L, SSM_CONV, SSM_XBC), SSM_CONV ** -0.5),
        "ssm_conv_b": nrm((L, SSM_XBC), 0.01),
        "ssm_dt_bias": ssm_dt_bias,
        "ssm_a_log": ssm_a_log,
        "ssm_d": gain((L, SSM_HEADS)),
        "ssm_norm_g": gain((L, SSM_INNER)),
        "lru_conv_w": nrm((L, LRU_CONV, LRU_WIDTH), LRU_CONV ** -0.5),
        "lru_conv_b": nrm((L, LRU_WIDTH), 0.01),
        "lru_w_a": nrm((L, LRU_BLOCKS, LRU_BLOCK, LRU_BLOCK), LRU_BLOCK ** -0.5),
        "lru_b_a": nrm((L, LRU_BLOCKS, LRU_BLOCK), 0.01),
        "lru_w_i": nrm((L, LRU_BLOCKS, LRU_BLOCK, LRU_BLOCK), LRU_BLOCK ** -0.5),
        "lru_b_i": nrm((L, LRU_BLOCKS, LRU_BLOCK), 0.01),
        "lru_lambda": lru_lambda,
        "lru_out_g": gain((L, LRU_WIDTH)),
        "w_out": nrm((L, MIX_WIDTH, D_MODEL), MIX_WIDTH ** -0.5),
        "xattn_norm_g": gain((L, D_MODEL)),
        "mem_norm_g": gain((L, D_MODEL)),
        "w_mq": nrm((L, D_MODEL, D_MODEL), D_MODEL ** -0.5),
        "w_mk": nrm((L, D_MODEL, D_MODEL), D_MODEL ** -0.5),
        "w_mv": nrm((L, D_MODEL, D_MODEL), D_MODEL ** -0.5),
        "w_mo": nrm((L, D_MODEL, D_MODEL), D_MODEL ** -0.5),
        "mlp_norm_g": gain((L, D_MODEL)),
        "w_mlp1": nrm((L, D_MODEL, D_FF), D_MODEL ** -0.5),
        "w_mlp2": nrm((L, D_FF, D_MODEL), D_FF ** -0.5),
        "final_norm_g": gain((D_MODEL,)),
    }


def _fwd_reference(x, mem, positions, mix_norm_g, w_in, mla_q_norm_g, mla_kv_norm_g, mla_w_uq, mla_w_ukv,
              mla_out_g, ssm_conv_w, ssm_conv_b, ssm_dt_bias, ssm_a_log, ssm_d, ssm_norm_g,
              lru_conv_w, lru_conv_b, lru_w_a, lru_b_a, lru_w_i, lru_b_i, lru_lambda, lru_out_g,
              w_out, xattn_norm_g, mem_norm_g, w_mq, w_mk, w_mv, w_mo, mlp_norm_g, w_mlp1, w_mlp2,
              final_norm_g):
    cos, sin = rope_tables(positions)
    for l in range(DEPTH):
        h = rms_norm(x, mix_norm_g[l])
        u = h @ w_in[l]
        c_q, c_kv, k_rope, z, xs, bs, cs, dt_raw, xr, gate = split_cols(u, IN_SPLITS)
        y_mla = mla_group(c_q, c_kv, k_rope, cos, sin, mla_q_norm_g[l], mla_kv_norm_g[l],
                          mla_w_uq[l], mla_w_ukv[l])
        y_ssm = mamba2_group(z, xs, bs, cs, dt_raw, ssm_conv_w[l], ssm_conv_b[l], ssm_dt_bias[l],
                             ssm_a_log[l], ssm_d[l], ssm_norm_g[l])
        y_lru = rglru_group(xr, gate, lru_conv_w[l], lru_conv_b[l], lru_w_a[l], lru_b_a[l],
                            lru_w_i[l], lru_b_i[l], lru_lambda[l])
        y_mix = jnp.concatenate([rms_norm(y_mla, mla_out_g[l]), y_ssm, rms_norm(y_lru, lru_out_g[l])], axis=-1)
        x = x + y_mix @ w_out[l]
        x = x + memory_cross_attention(rms_norm(x, xattn_norm_g[l]), rms_norm(mem, mem_norm_g[l]),
                                       w_mq[l], w_mk[l], w_mv[l], w_mo[l])
        hm = rms_norm(x, mlp_norm_g[l]) @ w_mlp1[l]
        x = x + jnp.square(jax.nn.relu(hm)) @ w_mlp2[l]
    return rms_norm(x, final_norm_g)


import jax as _jax
import jax.numpy as _jnp

TWIN_FORMAT = 'train_step'
FWD_PARAMS = ['x', 'mem', 'positions', 'mix_norm_g', 'w_in', 'mla_q_norm_g', 'mla_kv_norm_g', 'mla_w_uq', 'mla_w_ukv', 'mla_out_g', 'ssm_conv_w', 'ssm_conv_b', 'ssm_dt_bias', 'ssm_a_log', 'ssm_d', 'ssm_norm_g', 'lru_conv_w', 'lru_conv_b', 'lru_w_a', 'lru_b_a', 'lru_w_i', 'lru_b_i', 'lru_lambda', 'lru_out_g', 'w_out', 'xattn_norm_g', 'mem_norm_g', 'w_mq', 'w_mk', 'w_mv', 'w_mo', 'mlp_norm_g', 'w_mlp1', 'w_mlp2', 'final_norm_g']
TWIN_WEIGHTS = ['mix_norm_g', 'w_in', 'mla_q_norm_g', 'mla_kv_norm_g', 'mla_w_uq', 'mla_w_ukv', 'mla_out_g', 'ssm_conv_w', 'ssm_conv_b', 'ssm_dt_bias', 'ssm_a_log', 'ssm_d', 'ssm_norm_g', 'lru_conv_w', 'lru_conv_b', 'lru_w_a', 'lru_b_a', 'lru_w_i', 'lru_b_i', 'lru_lambda', 'lru_out_g', 'w_out', 'xattn_norm_g', 'mem_norm_g', 'w_mq', 'w_mk', 'w_mv', 'w_mo', 'mlp_norm_g', 'w_mlp1', 'w_mlp2', 'final_norm_g']
TWIN_DIFF_INPUT = 'x'
TWIN_INPUTS = ['x', 'mem', 'positions', 'mix_norm_g', 'w_in', 'mla_q_norm_g', 'mla_kv_norm_g', 'mla_w_uq', 'mla_w_ukv', 'mla_out_g', 'ssm_conv_w', 'ssm_conv_b', 'ssm_dt_bias', 'ssm_a_log', 'ssm_d', 'ssm_norm_g', 'lru_conv_w', 'lru_conv_b', 'lru_w_a', 'lru_b_a', 'lru_w_i', 'lru_b_i', 'lru_lambda', 'lru_out_g', 'w_out', 'xattn_norm_g', 'mem_norm_g', 'w_mq', 'w_mk', 'w_mv', 'w_mo', 'mlp_norm_g', 'w_mlp1', 'w_mlp2', 'final_norm_g', 'loss_target', 'm_mix_norm_g', 'm_w_in', 'm_mla_q_norm_g', 'm_mla_kv_norm_g', 'm_mla_w_uq', 'm_mla_w_ukv', 'm_mla_out_g', 'm_ssm_conv_w', 'm_ssm_conv_b', 'm_ssm_dt_bias', 'm_ssm_a_log', 'm_ssm_d', 'm_ssm_norm_g', 'm_lru_conv_w', 'm_lru_conv_b', 'm_lru_w_a', 'm_lru_b_a', 'm_lru_w_i', 'm_lru_b_i', 'm_lru_lambda', 'm_lru_out_g', 'm_w_out', 'm_xattn_norm_g', 'm_mem_norm_g', 'm_w_mq', 'm_w_mk', 'm_w_mv', 'm_w_mo', 'm_mlp_norm_g', 'm_w_mlp1', 'm_w_mlp2', 'm_final_norm_g', 'v_mix_norm_g', 'v_w_in', 'v_mla_q_norm_g', 'v_mla_kv_norm_g', 'v_mla_w_uq', 'v_mla_w_ukv', 'v_mla_out_g', 'v_ssm_conv_w', 'v_ssm_conv_b', 'v_ssm_dt_bias', 'v_ssm_a_log', 'v_ssm_d', 'v_ssm_norm_g', 'v_lru_conv_w', 'v_lru_conv_b', 'v_lru_w_a', 'v_lru_b_a', 'v_lru_w_i', 'v_lru_b_i', 'v_lru_lambda', 'v_lru_out_g', 'v_w_out', 'v_xattn_norm_g', 'v_mem_norm_g', 'v_w_mq', 'v_w_mk', 'v_w_mv', 'v_w_mo', 'v_mlp_norm_g', 'v_w_mlp1', 'v_w_mlp2', 'v_final_norm_g']
TWIN_OUTPUTS = ['loss', 'grad_x', 'grad_mix_norm_g', 'grad_w_in', 'grad_mla_q_norm_g', 'grad_mla_kv_norm_g', 'grad_mla_w_uq', 'grad_mla_w_ukv', 'grad_mla_out_g', 'grad_ssm_conv_w', 'grad_ssm_conv_b', 'grad_ssm_dt_bias', 'grad_ssm_a_log', 'grad_ssm_d', 'grad_ssm_norm_g', 'grad_lru_conv_w', 'grad_lru_conv_b', 'grad_lru_w_a', 'grad_lru_b_a', 'grad_lru_w_i', 'grad_lru_b_i', 'grad_lru_lambda', 'grad_lru_out_g', 'grad_w_out', 'grad_xattn_norm_g', 'grad_mem_norm_g', 'grad_w_mq', 'grad_w_mk', 'grad_w_mv', 'grad_w_mo', 'grad_mlp_norm_g', 'grad_w_mlp1', 'grad_w_mlp2', 'grad_final_norm_g', 'delta_mix_norm_g', 'delta_w_in', 'delta_mla_q_norm_g', 'delta_mla_kv_norm_g', 'delta_mla_w_uq', 'delta_mla_w_ukv', 'delta_mla_out_g', 'delta_ssm_conv_w', 'delta_ssm_conv_b', 'delta_ssm_dt_bias', 'delta_ssm_a_log', 'delta_ssm_d', 'delta_ssm_norm_g', 'delta_lru_conv_w', 'delta_lru_conv_b', 'delta_lru_w_a', 'delta_lru_b_a', 'delta_lru_w_i', 'delta_lru_b_i', 'delta_lru_lambda', 'delta_lru_out_g', 'delta_w_out', 'delta_xattn_norm_g', 'delta_mem_norm_g', 'delta_w_mq', 'delta_w_mk', 'delta_w_mv', 'delta_w_mo', 'delta_mlp_norm_g', 'delta_w_mlp1', 'delta_w_mlp2', 'delta_final_norm_g', 'new_m_mix_norm_g', 'new_m_w_in', 'new_m_mla_q_norm_g', 'new_m_mla_kv_norm_g', 'new_m_mla_w_uq', 'new_m_mla_w_ukv', 'new_m_mla_out_g', 'new_m_ssm_conv_w', 'new_m_ssm_conv_b', 'new_m_ssm_dt_bias', 'new_m_ssm_a_log', 'new_m_ssm_d', 'new_m_ssm_norm_g', 'new_m_lru_conv_w', 'new_m_lru_conv_b', 'new_m_lru_w_a', 'new_m_lru_b_a', 'new_m_lru_w_i', 'new_m_lru_b_i', 'new_m_lru_lambda', 'new_m_lru_out_g', 'new_m_w_out', 'new_m_xattn_norm_g', 'new_m_mem_norm_g', 'new_m_w_mq', 'new_m_w_mk', 'new_m_w_mv', 'new_m_w_mo', 'new_m_mlp_norm_g', 'new_m_w_mlp1', 'new_m_w_mlp2', 'new_m_final_norm_g', 'new_v_mix_norm_g', 'new_v_w_in', 'new_v_mla_q_norm_g', 'new_v_mla_kv_norm_g', 'new_v_mla_w_uq', 'new_v_mla_w_ukv', 'new_v_mla_out_g', 'new_v_ssm_conv_w', 'new_v_ssm_conv_b', 'new_v_ssm_dt_bias', 'new_v_ssm_a_log', 'new_v_ssm_d', 'new_v_ssm_norm_g', 'new_v_lru_conv_w', 'new_v_lru_conv_b', 'new_v_lru_w_a', 'new_v_lru_b_a', 'new_v_lru_w_i', 'new_v_lru_b_i', 'new_v_lru_lambda', 'new_v_lru_out_g', 'new_v_w_out', 'new_v_xattn_norm_g', 'new_v_mem_norm_g', 'new_v_w_mq', 'new_v_w_mk', 'new_v_w_mv', 'new_v_w_mo', 'new_v_mlp_norm_g', 'new_v_w_mlp1', 'new_v_w_mlp2', 'new_v_final_norm_g']
TWIN_LEAF_KINDS = {'loss': 'loss', 'grad_x': 'grad_x', 'grad_mix_norm_g': 'grad_w', 'grad_w_in': 'grad_w', 'grad_mla_q_norm_g': 'grad_w', 'grad_mla_kv_norm_g': 'grad_w', 'grad_mla_w_uq': 'grad_w', 'grad_mla_w_ukv': 'grad_w', 'grad_mla_out_g': 'grad_w', 'grad_ssm_conv_w': 'grad_w', 'grad_ssm_conv_b': 'grad_w', 'grad_ssm_dt_bias': 'grad_w', 'grad_ssm_a_log': 'grad_w', 'grad_ssm_d': 'grad_w', 'grad_ssm_norm_g': 'grad_w', 'grad_lru_conv_w': 'grad_w', 'grad_lru_conv_b': 'grad_w', 'grad_lru_w_a': 'grad_w', 'grad_lru_b_a': 'grad_w', 'grad_lru_w_i': 'grad_w', 'grad_lru_b_i': 'grad_w', 'grad_lru_lambda': 'grad_w', 'grad_lru_out_g': 'grad_w', 'grad_w_out': 'grad_w', 'grad_xattn_norm_g': 'grad_w', 'grad_mem_norm_g': 'grad_w', 'grad_w_mq': 'grad_w', 'grad_w_mk': 'grad_w', 'grad_w_mv': 'grad_w', 'grad_w_mo': 'grad_w', 'grad_mlp_norm_g': 'grad_w', 'grad_w_mlp1': 'grad_w', 'grad_w_mlp2': 'grad_w', 'grad_final_norm_g': 'grad_w', 'delta_mix_norm_g': 'delta_w', 'delta_w_in': 'delta_w', 'delta_mla_q_norm_g': 'delta_w', 'delta_mla_kv_norm_g': 'delta_w', 'delta_mla_w_uq': 'delta_w', 'delta_mla_w_ukv': 'delta_w', 'delta_mla_out_g': 'delta_w', 'delta_ssm_conv_w': 'delta_w', 'delta_ssm_conv_b': 'delta_w', 'delta_ssm_dt_bias': 'delta_w', 'delta_ssm_a_log': 'delta_w', 'delta_ssm_d': 'delta_w', 'delta_ssm_norm_g': 'delta_w', 'delta_lru_conv_w': 'delta_w', 'delta_lru_conv_b': 'delta_w', 'delta_lru_w_a': 'delta_w', 'delta_lru_b_a': 'delta_w', 'delta_lru_w_i': 'delta_w', 'delta_lru_b_i': 'delta_w', 'delta_lru_lambda': 'delta_w', 'delta_lru_out_g': 'delta_w', 'delta_w_out': 'delta_w', 'delta_xattn_norm_g': 'delta_w', 'delta_mem_norm_g': 'delta_w', 'delta_w_mq': 'delta_w', 'delta_w_mk': 'delta_w', 'delta_w_mv': 'delta_w', 'delta_w_mo': 'delta_w', 'delta_mlp_norm_g': 'delta_w', 'delta_w_mlp1': 'delta_w', 'delta_w_mlp2': 'delta_w', 'delta_final_norm_g': 'delta_w', 'new_m_mix_norm_g': 'new_m', 'new_m_w_in': 'new_m', 'new_m_mla_q_norm_g': 'new_m', 'new_m_mla_kv_norm_g': 'new_m', 'new_m_mla_w_uq': 'new_m', 'new_m_mla_w_ukv': 'new_m', 'new_m_mla_out_g': 'new_m', 'new_m_ssm_conv_w': 'new_m', 'new_m_ssm_conv_b': 'new_m', 'new_m_ssm_dt_bias': 'new_m', 'new_m_ssm_a_log': 'new_m', 'new_m_ssm_d': 'new_m', 'new_m_ssm_norm_g': 'new_m', 'new_m_lru_conv_w': 'new_m', 'new_m_lru_conv_b': 'new_m', 'new_m_lru_w_a': 'new_m', 'new_m_lru_b_a': 'new_m', 'new_m_lru_w_i': 'new_m', 'new_m_lru_b_i': 'new_m', 'new_m_lru_lambda': 'new_m', 'new_m_lru_out_g': 'new_m', 'new_m_w_out': 'new_m', 'new_m_xattn_norm_g': 'new_m', 'new_m_mem_norm_g': 'new_m', 'new_m_w_mq': 'new_m', 'new_m_w_mk': 'new_m', 'new_m_w_mv': 'new_m', 'new_m_w_mo': 'new_m', 'new_m_mlp_norm_g': 'new_m', 'new_m_w_mlp1': 'new_m', 'new_m_w_mlp2': 'new_m', 'new_m_final_norm_g': 'new_m', 'new_v_mix_norm_g': 'new_v', 'new_v_w_in': 'new_v', 'new_v_mla_q_norm_g': 'new_v', 'new_v_mla_kv_norm_g': 'new_v', 'new_v_mla_w_uq': 'new_v', 'new_v_mla_w_ukv': 'new_v', 'new_v_mla_out_g': 'new_v', 'new_v_ssm_conv_w': 'new_v', 'new_v_ssm_conv_b': 'new_v', 'new_v_ssm_dt_bias': 'new_v', 'new_v_ssm_a_log': 'new_v', 'new_v_ssm_d': 'new_v', 'new_v_ssm_norm_g': 'new_v', 'new_v_lru_conv_w': 'new_v', 'new_v_lru_conv_b': 'new_v', 'new_v_lru_w_a': 'new_v', 'new_v_lru_b_a': 'new_v', 'new_v_lru_w_i': 'new_v', 'new_v_lru_b_i': 'new_v', 'new_v_lru_lambda': 'new_v', 'new_v_lru_out_g': 'new_v', 'new_v_w_out': 'new_v', 'new_v_xattn_norm_g': 'new_v', 'new_v_mem_norm_g': 'new_v', 'new_v_w_mq': 'new_v', 'new_v_w_mk': 'new_v', 'new_v_w_mv': 'new_v', 'new_v_w_mo': 'new_v', 'new_v_mlp_norm_g': 'new_v', 'new_v_w_mlp1': 'new_v', 'new_v_w_mlp2': 'new_v', 'new_v_final_norm_g': 'new_v'}


def _forward(args):
    return _fwd_reference(*[args[k] for k in FWD_PARAMS])


def _output_shape():
    def fwd():
        inp = _fwd_setup_inputs(0)
        return _fwd_reference(*[inp[k] for k in FWD_PARAMS])
    out = _jax.eval_shape(fwd)
    return out.shape, out.dtype

N_MICROBATCH = 1
ADAM_LR = 0.001
ADAM_B1 = 0.9
ADAM_B2 = 0.999
ADAM_EPS = 1e-08
ADAM_WD = 0.01
ADAM_STEP = 10
PER_EXAMPLE_BATCH_AXIS = {'x': 0, 'mem': 0, 'positions': 0, 'loss_target': 0}
SHARED_INPUTS = []
_WEIGHT_DTYPES = {'mix_norm_g': _jnp.float32, 'w_in': _jnp.float32, 'mla_q_norm_g': _jnp.float32, 'mla_kv_norm_g': _jnp.float32, 'mla_w_uq': _jnp.float32, 'mla_w_ukv': _jnp.float32, 'mla_out_g': _jnp.float32, 'ssm_conv_w': _jnp.float32, 'ssm_conv_b': _jnp.float32, 'ssm_dt_bias': _jnp.float32, 'ssm_a_log': _jnp.float32, 'ssm_d': _jnp.float32, 'ssm_norm_g': _jnp.float32, 'lru_conv_w': _jnp.float32, 'lru_conv_b': _jnp.float32, 'lru_w_a': _jnp.float32, 'lru_b_a': _jnp.float32, 'lru_w_i': _jnp.float32, 'lru_b_i': _jnp.float32, 'lru_lambda': _jnp.float32, 'lru_out_g': _jnp.float32, 'w_out': _jnp.float32, 'xattn_norm_g': _jnp.float32, 'mem_norm_g': _jnp.float32, 'w_mq': _jnp.float32, 'w_mk': _jnp.float32, 'w_mv': _jnp.float32, 'w_mo': _jnp.float32, 'mlp_norm_g': _jnp.float32, 'w_mlp1': _jnp.float32, 'w_mlp2': _jnp.float32, 'final_norm_g': _jnp.float32}
MOMENT_SCALE = {'mix_norm_g': 2.787625e-01, 'w_in': 2.014519e-01, 'mla_q_norm_g': 1.513791e-01, 'mla_kv_norm_g': 7.764237e-01, 'mla_w_uq': 8.570545e-02, 'mla_w_ukv': 2.058755e-01, 'mla_out_g': 2.621300e-01, 'ssm_conv_w': 1.124196e-01, 'ssm_conv_b': 1.783393e-01, 'ssm_dt_bias': 6.311708e-01, 'ssm_a_log': 1.923874e-01, 'ssm_d': 2.090909e+00, 'ssm_norm_g': 1.669967e-01, 'lru_conv_w': 2.090139e-01, 'lru_conv_b': 1.973670e+00, 'lru_w_a': 6.576940e-02, 'lru_b_a': 4.818691e-02, 'lru_w_i': 1.252677e-01, 'lru_b_i': 6.893259e-02, 'lru_lambda': 1.028726e-01, 'lru_out_g': 1.948597e-01, 'w_out': 2.358937e-01, 'xattn_norm_g': 1.522244e-02, 'mem_norm_g': 2.734043e-02, 'w_mq': 1.526136e-02, 'w_mk': 1.520944e-02, 'w_mv': 2.123856e-02, 'w_mo': 2.128980e-02, 'mlp_norm_g': 1.841898e-01, 'w_mlp1': 8.978573e-02, 'w_mlp2': 2.823936e-01, 'final_norm_g': 6.675053e+01}


def _to_microbatches(a, axis):
    t = _jnp.moveaxis(a, axis, 0)
    t = t.reshape((N_MICROBATCH, t.shape[0] // N_MICROBATCH) + t.shape[1:])
    return _jnp.moveaxis(t, 1, axis + 1)


def setup_inputs(seed: int = 0) -> dict:
    inp = _fwd_setup_inputs(seed)
    key = _jax.random.fold_in(_jax.random.key(seed), 7919)
    shape, _ = _output_shape()
    out = dict(inp)
    out["loss_target"] = _jax.random.normal(_jax.random.fold_in(key, 0), shape, _jnp.float32)
    for i, name in enumerate(TWIN_WEIGHTS):
        w = inp[name].astype(_jnp.float32)
        if MOMENT_SCALE is None:
            s = _jnp.sqrt(_jnp.mean(_jnp.square(w)) + 1e-30)
        else:
            s = MOMENT_SCALE[name]
        km, kv = _jax.random.split(_jax.random.fold_in(key, i + 1))
        out[name] = w
        out["m_" + name] = s * _jax.random.normal(km, w.shape, _jnp.float32)
        out["v_" + name] = (s * s) * _jax.random.uniform(kv, w.shape, _jnp.float32, 0.5, 1.5)
    if N_MICROBATCH > 1:
        for name, axis in PER_EXAMPLE_BATCH_AXIS.items():
            out[name] = _to_microbatches(out[name], axis)
    return {'x': out['x'], 'mem': out['mem'], 'positions': out['positions'], 'mix_norm_g': out['mix_norm_g'], 'w_in': out['w_in'], 'mla_q_norm_g': out['mla_q_norm_g'], 'mla_kv_norm_g': out['mla_kv_norm_g'], 'mla_w_uq': out['mla_w_uq'], 'mla_w_ukv': out['mla_w_ukv'], 'mla_out_g': out['mla_out_g'], 'ssm_conv_w': out['ssm_conv_w'], 'ssm_conv_b': out['ssm_conv_b'], 'ssm_dt_bias': out['ssm_dt_bias'], 'ssm_a_log': out['ssm_a_log'], 'ssm_d': out['ssm_d'], 'ssm_norm_g': out['ssm_norm_g'], 'lru_conv_w': out['lru_conv_w'], 'lru_conv_b': out['lru_conv_b'], 'lru_w_a': out['lru_w_a'], 'lru_b_a': out['lru_b_a'], 'lru_w_i': out['lru_w_i'], 'lru_b_i': out['lru_b_i'], 'lru_lambda': out['lru_lambda'], 'lru_out_g': out['lru_out_g'], 'w_out': out['w_out'], 'xattn_norm_g': out['xattn_norm_g'], 'mem_norm_g': out['mem_norm_g'], 'w_mq': out['w_mq'], 'w_mk': out['w_mk'], 'w_mv': out['w_mv'], 'w_mo': out['w_mo'], 'mlp_norm_g': out['mlp_norm_g'], 'w_mlp1': out['w_mlp1'], 'w_mlp2': out['w_mlp2'], 'final_norm_g': out['final_norm_g'], 'loss_target': out['loss_target'], 'm_mix_norm_g': out['m_mix_norm_g'], 'm_w_in': out['m_w_in'], 'm_mla_q_norm_g': out['m_mla_q_norm_g'], 'm_mla_kv_norm_g': out['m_mla_kv_norm_g'], 'm_mla_w_uq': out['m_mla_w_uq'], 'm_mla_w_ukv': out['m_mla_w_ukv'], 'm_mla_out_g': out['m_mla_out_g'], 'm_ssm_conv_w': out['m_ssm_conv_w'], 'm_ssm_conv_b': out['m_ssm_conv_b'], 'm_ssm_dt_bias': out['m_ssm_dt_bias'], 'm_ssm_a_log': out['m_ssm_a_log'], 'm_ssm_d': out['m_ssm_d'], 'm_ssm_norm_g': out['m_ssm_norm_g'], 'm_lru_conv_w': out['m_lru_conv_w'], 'm_lru_conv_b': out['m_lru_conv_b'], 'm_lru_w_a': out['m_lru_w_a'], 'm_lru_b_a': out['m_lru_b_a'], 'm_lru_w_i': out['m_lru_w_i'], 'm_lru_b_i': out['m_lru_b_i'], 'm_lru_lambda': out['m_lru_lambda'], 'm_lru_out_g': out['m_lru_out_g'], 'm_w_out': out['m_w_out'], 'm_xattn_norm_g': out['m_xattn_norm_g'], 'm_mem_norm_g': out['m_mem_norm_g'], 'm_w_mq': out['m_w_mq'], 'm_w_mk': out['m_w_mk'], 'm_w_mv': out['m_w_mv'], 'm_w_mo': out['m_w_mo'], 'm_mlp_norm_g': out['m_mlp_norm_g'], 'm_w_mlp1': out['m_w_mlp1'], 'm_w_mlp2': out['m_w_mlp2'], 'm_final_norm_g': out['m_final_norm_g'], 'v_mix_norm_g': out['v_mix_norm_g'], 'v_w_in': out['v_w_in'], 'v_mla_q_norm_g': out['v_mla_q_norm_g'], 'v_mla_kv_norm_g': out['v_mla_kv_norm_g'], 'v_mla_w_uq': out['v_mla_w_uq'], 'v_mla_w_ukv': out['v_mla_w_ukv'], 'v_mla_out_g': out['v_mla_out_g'], 'v_ssm_conv_w': out['v_ssm_conv_w'], 'v_ssm_conv_b': out['v_ssm_conv_b'], 'v_ssm_dt_bias': out['v_ssm_dt_bias'], 'v_ssm_a_log': out['v_ssm_a_log'], 'v_ssm_d': out['v_ssm_d'], 'v_ssm_norm_g': out['v_ssm_norm_g'], 'v_lru_conv_w': out['v_lru_conv_w'], 'v_lru_conv_b': out['v_lru_conv_b'], 'v_lru_w_a': out['v_lru_w_a'], 'v_lru_b_a': out['v_lru_b_a'], 'v_lru_w_i': out['v_lru_w_i'], 'v_lru_b_i': out['v_lru_b_i'], 'v_lru_lambda': out['v_lru_lambda'], 'v_lru_out_g': out['v_lru_out_g'], 'v_w_out': out['v_w_out'], 'v_xattn_norm_g': out['v_xattn_norm_g'], 'v_mem_norm_g': out['v_mem_norm_g'], 'v_w_mq': out['v_w_mq'], 'v_w_mk': out['v_w_mk'], 'v_w_mv': out['v_w_mv'], 'v_w_mo': out['v_w_mo'], 'v_mlp_norm_g': out['v_mlp_norm_g'], 'v_w_mlp1': out['v_w_mlp1'], 'v_w_mlp2': out['v_w_mlp2'], 'v_final_norm_g': out['v_final_norm_g']}


def _loss(weights, diff, rest, loss_target):
    with _jax.named_scope("forward"):
        args = {**rest, TWIN_DIFF_INPUT: diff, **{k: w.astype(_WEIGHT_DTYPES[k]) for k, w in weights.items()}}
        y = _forward(args)
    with _jax.named_scope("loss_head"):
        err = _jnp.square(y.astype(_jnp.float32) - loss_target)
        return 0.5 * _jnp.sum(_jnp.mean(err, axis=-1)) if err.ndim else 0.5 * err


def _adamw(w, g, m, v):
    m = ADAM_B1 * m + (1.0 - ADAM_B1) * g
    v = ADAM_B2 * v + (1.0 - ADAM_B2) * _jnp.square(g)
    m_hat = m / (1.0 - ADAM_B1 ** ADAM_STEP)
    v_hat = v / (1.0 - ADAM_B2 ** ADAM_STEP)
    delta = -ADAM_LR * (m_hat / (_jnp.sqrt(v_hat) + ADAM_EPS) + ADAM_WD * w)
    return delta, m, v


def reference(x, mem, positions, mix_norm_g, w_in, mla_q_norm_g, mla_kv_norm_g, mla_w_uq, mla_w_ukv, mla_out_g, ssm_conv_w, ssm_conv_b, ssm_dt_bias, ssm_a_log, ssm_d, ssm_norm_g, lru_conv_w, lru_conv_b, lru_w_a, lru_b_a, lru_w_i, lru_b_i, lru_lambda, lru_out_g, w_out, xattn_norm_g, mem_norm_g, w_mq, w_mk, w_mv, w_mo, mlp_norm_g, w_mlp1, w_mlp2, final_norm_g, loss_target, m_mix_norm_g, m_w_in, m_mla_q_norm_g, m_mla_kv_norm_g, m_mla_w_uq, m_mla_w_ukv, m_mla_out_g, m_ssm_conv_w, m_ssm_conv_b, m_ssm_dt_bias, m_ssm_a_log, m_ssm_d, m_ssm_norm_g, m_lru_conv_w, m_lru_conv_b, m_lru_w_a, m_lru_b_a, m_lru_w_i, m_lru_b_i, m_lru_lambda, m_lru_out_g, m_w_out, m_xattn_norm_g, m_mem_norm_g, m_w_mq, m_w_mk, m_w_mv, m_w_mo, m_mlp_norm_g, m_w_mlp1, m_w_mlp2, m_final_norm_g, v_mix_norm_g, v_w_in, v_mla_q_norm_g, v_mla_kv_norm_g, v_mla_w_uq, v_mla_w_ukv, v_mla_out_g, v_ssm_conv_w, v_ssm_conv_b, v_ssm_dt_bias, v_ssm_a_log, v_ssm_d, v_ssm_norm_g, v_lru_conv_w, v_lru_conv_b, v_lru_w_a, v_lru_b_a, v_lru_w_i, v_lru_b_i, v_lru_lambda, v_lru_out_g, v_w_out, v_xattn_norm_g, v_mem_norm_g, v_w_mq, v_w_mk, v_w_mv, v_w_mo, v_mlp_norm_g, v_w_mlp1, v_w_mlp2, v_final_norm_g):
    given = dict(x=x, mem=mem, positions=positions, mix_norm_g=mix_norm_g, w_in=w_in, mla_q_norm_g=mla_q_norm_g, mla_kv_norm_g=mla_kv_norm_g, mla_w_uq=mla_w_uq, mla_w_ukv=mla_w_ukv, mla_out_g=mla_out_g, ssm_conv_w=ssm_conv_w, ssm_conv_b=ssm_conv_b, ssm_dt_bias=ssm_dt_bias, ssm_a_log=ssm_a_log, ssm_d=ssm_d, ssm_norm_g=ssm_norm_g, lru_conv_w=lru_conv_w, lru_conv_b=lru_conv_b, lru_w_a=lru_w_a, lru_b_a=lru_b_a, lru_w_i=lru_w_i, lru_b_i=lru_b_i, lru_lambda=lru_lambda, lru_out_g=lru_out_g, w_out=w_out, xattn_norm_g=xattn_norm_g, mem_norm_g=mem_norm_g, w_mq=w_mq, w_mk=w_mk, w_mv=w_mv, w_mo=w_mo, mlp_norm_g=mlp_norm_g, w_mlp1=w_mlp1, w_mlp2=w_mlp2, final_norm_g=final_norm_g, loss_target=loss_target, m_mix_norm_g=m_mix_norm_g, m_w_in=m_w_in, m_mla_q_norm_g=m_mla_q_norm_g, m_mla_kv_norm_g=m_mla_kv_norm_g, m_mla_w_uq=m_mla_w_uq, m_mla_w_ukv=m_mla_w_ukv, m_mla_out_g=m_mla_out_g, m_ssm_conv_w=m_ssm_conv_w, m_ssm_conv_b=m_ssm_conv_b, m_ssm_dt_bias=m_ssm_dt_bias, m_ssm_a_log=m_ssm_a_log, m_ssm_d=m_ssm_d, m_ssm_norm_g=m_ssm_norm_g, m_lru_conv_w=m_lru_conv_w, m_lru_conv_b=m_lru_conv_b, m_lru_w_a=m_lru_w_a, m_lru_b_a=m_lru_b_a, m_lru_w_i=m_lru_w_i, m_lru_b_i=m_lru_b_i, m_lru_lambda=m_lru_lambda, m_lru_out_g=m_lru_out_g, m_w_out=m_w_out, m_xattn_norm_g=m_xattn_norm_g, m_mem_norm_g=m_mem_norm_g, m_w_mq=m_w_mq, m_w_mk=m_w_mk, m_w_mv=m_w_mv, m_w_mo=m_w_mo, m_mlp_norm_g=m_mlp_norm_g, m_w_mlp1=m_w_mlp1, m_w_mlp2=m_w_mlp2, m_final_norm_g=m_final_norm_g, v_mix_norm_g=v_mix_norm_g, v_w_in=v_w_in, v_mla_q_norm_g=v_mla_q_norm_g, v_mla_kv_norm_g=v_mla_kv_norm_g, v_mla_w_uq=v_mla_w_uq, v_mla_w_ukv=v_mla_w_ukv, v_mla_out_g=v_mla_out_g, v_ssm_conv_w=v_ssm_conv_w, v_ssm_conv_b=v_ssm_conv_b, v_ssm_dt_bias=v_ssm_dt_bias, v_ssm_a_log=v_ssm_a_log, v_ssm_d=v_ssm_d, v_ssm_norm_g=v_ssm_norm_g, v_lru_conv_w=v_lru_conv_w, v_lru_conv_b=v_lru_conv_b, v_lru_w_a=v_lru_w_a, v_lru_b_a=v_lru_b_a, v_lru_w_i=v_lru_w_i, v_lru_b_i=v_lru_b_i, v_lru_lambda=v_lru_lambda, v_lru_out_g=v_lru_out_g, v_w_out=v_w_out, v_xattn_norm_g=v_xattn_norm_g, v_mem_norm_g=v_mem_norm_g, v_w_mq=v_w_mq, v_w_mk=v_w_mk, v_w_mv=v_w_mv, v_w_mo=v_w_mo, v_mlp_norm_g=v_mlp_norm_g, v_w_mlp1=v_w_mlp1, v_w_mlp2=v_w_mlp2, v_final_norm_g=v_final_norm_g)
    weights = {n: given[n] for n in TWIN_WEIGHTS}
    shared = {n: given[n] for n in SHARED_INPUTS}
    per_example = {n: given[n] for n in ['x', 'mem', 'positions']}
    grad_fn = _jax.value_and_grad(_loss, argnums=(0, 1))

    def one_microbatch(ex, loss_target):
        ex = dict(ex)
        diff = ex.pop(TWIN_DIFF_INPUT)
        return grad_fn(weights, diff, {**shared, **ex}, loss_target)

    if N_MICROBATCH == 1:
        loss, (grad_w, grad_x) = one_microbatch(per_example, given["loss_target"])
    else:
        def body(carry, xs):
            loss_sum, grad_sum = carry
            l_k, (gw_k, gx_k) = one_microbatch(xs[0], xs[1])
            with _jax.named_scope("update"):
                return (loss_sum + l_k, _jax.tree.map(_jnp.add, grad_sum, gw_k)), gx_k

        init = (_jnp.zeros((), _jnp.float32), _jax.tree.map(_jnp.zeros_like, weights))
        (loss, grad_w), grad_x = _jax.lax.scan(body, init, (per_example, given["loss_target"]))
    with _jax.named_scope("update"):
        delta_w, new_m, new_v = {}, {}, {}
        for n in TWIN_WEIGHTS:
            delta_w[n], new_m[n], new_v[n] = _adamw(weights[n], grad_w[n], given["m_" + n], given["v_" + n])
    return (loss, grad_x, *[grad_w[n] for n in TWIN_WEIGHTS], *[delta_w[n] for n in TWIN_WEIGHTS],
            *[new_m[n] for n in TWIN_WEIGHTS], *[new_v[n] for n in TWIN_WEIGHTS])
```

```python
import functools
import math

import jax
import jax.numpy as jnp
from jax import lax
from jax.experimental import pallas as pl
from jax.experimental.pallas import tpu as pltpu

F32, BF16 = jnp.float32, jnp.bfloat16

D = 1024
DEPTH = 4
N_MEM = 256
EPS = 1e-6
HEADS = 8
NOPE, ROPE, VDIM = 64, 32, 64
Q_RANK, KV_RANK = 256, 128
ROPE_THETA = 10000.0
SSM_CHUNK = 128
LRU_C = 8.0
MEM_HEADS = 4
D_FF = 4 * D
SLAB = 128
LR, B1, B2, AEPS, WD, STEP = 0.001, 0.9, 0.999, 1e-08, 0.01, 10

N_DEV = 8
MESH = pl.DeviceIdType.MESH

U_MLA = 640
U_GATE = 640
U_CONV = 768

_DN = {"nn": (((1,), (0,)), ((), ())), "nt": (((1,), (1,)), ((), ())), "tn": (((0,), (0,)), ((), ()))}


def _dot(a, b, kind):
    return lax.dot_general(a.astype(BF16), b.astype(BF16), _DN[kind], preferred_element_type=F32)


@functools.partial(jax.custom_vjp, nondiff_argnums=(2,))
def _bdot(a, b, kind):
    return _dot(a, b, kind)


def _bdot_fwd(a, b, kind):
    return _dot(a, b, kind), (a, b)


def _bdot_bwd(kind, res, g):
    a, b = res
    if kind == "nn":
        da, db = _dot(g, b, "nt"), _dot(a, g, "tn")
    elif kind == "nt":
        da, db = _dot(g, b, "nn"), _dot(g, a, "tn")
    else:
        da, db = _dot(b, g, "nt"), _dot(a, g, "nn")
    return da.astype(a.dtype), db.astype(b.dtype)


_bdot.defvjp(_bdot_fwd, _bdot_bwd)


def _tile(n, pref):
    if n <= pref:
        return n
    t = pref
    while n % t:
        t -= SLAB
    return t


def _cparams(sem, vmem_mb=48):
    return pltpu.CompilerParams(dimension_semantics=sem, vmem_limit_bytes=vmem_mb * 1024 * 1024)


def _mm(a, b, kind="nn", *, name, out_dtypes=(F32,), epi=None, extras=(), tm=512, tn=512, tk=1024):
    if kind == "tn":
        K, M = a.shape
    else:
        M, K = a.shape
    N = b.shape[0] if kind == "nt" else b.shape[1]
    tm, tn, tk = _tile(M, tm), _tile(N, tn), _tile(K, tk)
    nk = K // tk
    a_spec = pl.BlockSpec((tk, tm), lambda i, j, k: (k, i)) if kind == "tn" else pl.BlockSpec((tm, tk), lambda i, j, k: (i, k))
    b_spec = pl.BlockSpec((tn, tk), lambda i, j, k: (j, k)) if kind == "nt" else pl.BlockSpec((tk, tn), lambda i, j, k: (k, j))
    o_spec = pl.BlockSpec((tm, tn), lambda i, j, k: (i, j))
    n_ex, n_out = len(extras), len(out_dtypes)

    def body(*refs):
        a_ref, b_ref = refs[:2]
        ex = refs[2:2 + n_ex]
        outs = refs[2 + n_ex:2 + n_ex + n_out]
        acc = refs[-1]
        k = pl.program_id(2)

        @pl.when(k == 0)
        def _():
            acc[...] = jnp.zeros_like(acc)

        acc[...] += _dot(a_ref[...], b_ref[...], kind)

        @pl.when(k == nk - 1)
        def _():
            r = acc[...]
            res = epi(r, *[e[...] for e in ex]) if epi is not None else (r,)
            for o, v in zip(outs, res):
                o[...] = v.astype(o.dtype)

    res = pl.pallas_call(
        body,
        name=name,
        grid=(M // tm, N // tn, nk),
        in_specs=[a_spec, b_spec] + [o_spec] * n_ex,
        out_specs=[o_spec] * n_out,
        out_shape=[jax.ShapeDtypeStruct((M, N), dt) for dt in out_dtypes],
        scratch_shapes=[pltpu.VMEM((tm, tn), F32)],
        compiler_params=_cparams(("parallel", "parallel", "arbitrary")),
    )(a, b, *extras)
    return res[0] if n_out == 1 else res


def _row_spec(arr, tm):
    return pl.BlockSpec((tm, arr.shape[1]), lambda i: (i, 0))


def _full_spec(arr):
    nd = arr.ndim
    return pl.BlockSpec(arr.shape, lambda i: (0,) * nd)


def _rows_fwd(fn, rows, params, outs, *, name, tm=256):
    T = rows[0].shape[0]
    tm = min(tm, T)
    nr, npar = len(rows), len(params)

    def body(*refs):
        ins = [r[...] for r in refs[:nr + npar]]
        res = fn(*ins)
        for o, v in zip(refs[nr + npar:], res):
            o[...] = v.astype(o.dtype)

    res = pl.pallas_call(
        body,
        name=name,
        grid=(T // tm,),
        in_specs=[_row_spec(r, tm) for r in rows] + [_full_spec(p) for p in params],
        out_specs=[pl.BlockSpec((tm, c), lambda i: (i, 0)) for c, _ in outs],
        out_shape=[jax.ShapeDtypeStruct((T, c), dt) for c, dt in outs],
        compiler_params=_cparams(("parallel",)),
    )(*rows, *params)
    return res


def _rows_vjp(fn, rows, params, cts, *, name, drows, dparams, drow_dtypes, add=None, tm=256):
    T = rows[0].shape[0]
    tm = min(tm, T)
    nr, npar, nct = len(rows), len(params), len(cts)
    n_add = 0 if add is None else 1
    n_dr, n_dp = len(drows), len(dparams)

    def body(*refs):
        row_t = [r[...] for r in refs[:nr]]
        par_t = [r[...] for r in refs[nr:nr + npar]]
        ct_t = [r[...].astype(F32) for r in refs[nr + npar:nr + npar + nct]]
        pos = nr + npar + nct
        add_t = refs[pos][...] if n_add else None
        pos += n_add
        drow_refs = refs[pos:pos + n_dr]
        dpar_refs = refs[pos + n_dr:pos + n_dr + n_dp]

        def g(*dargs):
            rr, pp = list(row_t), list(par_t)
            for idx, v in zip(drows, dargs[:n_dr]):
                rr[idx] = v
            for idx, v in zip(dparams, dargs[n_dr:]):
                pp[idx] = v
            return tuple(fn(*rr, *pp))

        prim = [row_t[i].astype(F32) for i in drows] + [par_t[i].astype(F32) for i in dparams]
        _, vjp = jax.vjp(g, *prim)
        grads = vjp(tuple(ct_t))
        for n, (o, v) in enumerate(zip(drow_refs, grads[:n_dr])):
            if n == 0 and n_add:
                v = v + add_t.astype(F32)
            o[...] = v.astype(o.dtype)

        @pl.when(pl.program_id(0) == 0)
        def _():
            for o in dpar_refs:
                o[...] = jnp.zeros_like(o)

        for o, v in zip(dpar_refs, grads[n_dr:]):
            o[...] += v

    res = pl.pallas_call(
        body,
        name=name,
        grid=(T // tm,),
        in_specs=[_row_spec(r, tm) for r in rows] + [_full_spec(p) for p in params] + [_row_spec(c, tm) for c in cts]
        + ([_row_spec(add, tm)] if n_add else []),
        out_specs=[_row_spec(rows[i], tm) for i in drows] + [_full_spec(params[i]) for i in dparams],
        out_shape=[jax.ShapeDtypeStruct(rows[i].shape, dt) for i, dt in zip(drows, drow_dtypes)]
        + [jax.ShapeDtypeStruct(params[i].shape, F32) for i in dparams],
        compiler_params=_cparams(("arbitrary",)),
    )(*rows, *params, *cts, *([add] if n_add else []))
    return list(res[:n_dr]), list(res[n_dr:])


def _rms(x, g, n):
    return x * lax.rsqrt(jnp.sum(x * x, axis=-1, keepdims=True) * (1.0 / n) + EPS) * g


def _sigmoid(x):
    return 1.0 / (1.0 + jnp.exp(-x))


def _silu(x):
    return x * _sigmoid(x)


def _softplus(x):
    return jnp.maximum(x, 0.0) + jnp.log(1.0 + jnp.exp(-jnp.abs(x)))


def _gelu_tanh(x):
    return 0.5 * x * (1.0 + jnp.tanh(math.sqrt(2.0 / math.pi) * (x + 0.044715 * x * x * x)))


def _lane(shape):
    return lax.broadcasted_iota(jnp.int32, shape, len(shape) - 1)


def _col(x, h):
    return jnp.sum(jnp.where(_lane(x.shape) == h, x, 0.0), axis=-1, keepdims=True)


def _f_norm(x, g):
    return (_rms(x.astype(F32), g, x.shape[-1]),)


def _f_mla_prep(u, ck, sk, gq, gkv):
    u = u.astype(F32)
    cq = _rms(u[:, 0:256], gq, Q_RANK)
    ckv = _rms(u[:, 256:384], gkv, KV_RANK)
    kr = u[:, 384:512] * ck + u[:, 512:640] * sk
    return cq, jnp.concatenate([ckv, kr], axis=1)


def _f_qrope(y, cq, sq):
    y = y.astype(F32)
    c8, s8 = jnp.tile(cq, (1, HEADS)), jnp.tile(sq, (1, HEADS))
    return (y[:, :HEADS * SLAB] * c8 + y[:, HEADS * SLAB:] * s8,)


def _f_lru_gates(xc, wa, ba, wi, bi, lam):
    xc = xc.astype(F32)
    r = _sigmoid(_bdot(xc, wa, "nn") + ba)
    i = _sigmoid(_bdot(xc, wi, "nn") + bi)
    log_a = -LRU_C * r * _softplus(-lam)
    a = jnp.exp(log_a)
    x2 = 2.0 * log_a
    m1 = jnp.where(x2 > -0.02, -x2 * (1.0 + x2 * (0.5 + x2 * (1.0 / 6.0 + x2 * (1.0 / 24.0)))), 1.0 - jnp.exp(x2))
    return a, jnp.sqrt(m1) * (i * xc)


def _f_mix(o, ys, h, ug, g_mla, g_lru):
    o = o.astype(F32)
    y_mla = _rms(o, g_mla, HEADS * VDIM)
    y_lru = _rms(h.astype(F32) * _gelu_tanh(ug[:, 256:512].astype(F32)), g_lru, 256)
    return (jnp.concatenate([y_mla, ys.astype(F32), y_lru], axis=1),)


def _f_xattn(q, k, v):
    hd = D // MEM_HEADS
    outs = []
    for h in range(MEM_HEADS):
        sl = slice(h * hd, (h + 1) * hd)
        s = _bdot(q[:, sl], k[:, sl], "nt") * (1.0 / math.sqrt(hd))
        s = s - jnp.max(s, axis=-1, keepdims=True)
        p = jnp.exp(s)
        p = p / jnp.sum(p, axis=-1, keepdims=True)
        outs.append(_bdot(p, v[:, sl], "nn"))
    return (jnp.concatenate(outs, axis=1),)


def _split_dot(tri, a, kind):
    a_hi = a.astype(BF16)
    r1 = a - a_hi.astype(F32)
    a_mid = r1.astype(BF16)
    a_lo = (r1 - a_mid.astype(F32)).astype(BF16)
    return _dot(tri, a_hi, kind) + _dot(tri, a_mid, kind) + _dot(tri, a_lo, kind)


@jax.custom_vjp
def _tri_cumsum(tri, a):
    return _split_dot(tri, a, "nn")


def _tri_cumsum_fwd(tri, a):
    return _split_dot(tri, a, "nn"), tri


def _tri_cumsum_bwd(tri, g):
    return jnp.zeros_like(tri), _split_dot(tri, g, "tn")


_tri_cumsum.defvjp(_tri_cumsum_fwd, _tri_cumsum_bwd)


def _f_ssd_chunk(c, ug, s0, s1, dtb, alog, dsk, ng):
    L = c.shape[0]
    c = c.astype(F32)
    xbc = _silu(c)
    xs, bm, cm = xbc[:, 0:256], xbc[:, 256:384], xbc[:, 384:512]
    z = ug[:, 0:256].astype(F32)
    dt = _softplus(ug[:, 512:640].astype(F32) + dtb)
    a = dt * (-jnp.exp(alog))
    rowi = lax.broadcasted_iota(jnp.int32, (L, L), 0)
    coli = lax.broadcasted_iota(jnp.int32, (L, L), 1)
    tril = rowi >= coli
    acum = _tri_cumsum(tril.astype(BF16), a)
    acum_t = acum.T
    lane = _lane((1, SLAB))
    lo = lane < 64
    ys, new_s = [], []
    for g in range(2):
        gm = (lane >= 64 * g) & (lane < 64 * g + 64)
        bg, cg = jnp.where(gm, bm, 0.0), jnp.where(gm, cm, 0.0)
        cb = _bdot(cg, bg, "nt")
        x = xs[:, SLAB * g:SLAB * (g + 1)]
        h0, h1 = 2 * g, 2 * g + 1
        ac0, ac1 = _col(acum, h0), _col(acum, h1)
        xdt = x * jnp.where(lo, _col(dt, h0), _col(dt, h1))
        ac_l = jnp.where(lo, ac0, ac1)
        tot = acum[L - 1:L, :]
        tot_l = jnp.where(lo, _col(tot, h0), _col(tot, h1))
        yd = jnp.zeros((L, SLAB), F32)
        for hh, acc, hm in ((h0, ac0, lo), (h1, ac1, jnp.logical_not(lo))):
            seg = acc - acum_t[hh:hh + 1, :]
            lm = jnp.where(tril, jnp.exp(jnp.where(tril, seg, 0.0)), 0.0)
            yd = yd + _bdot(cb * lm, jnp.where(hm, xdt, 0.0), "nn")
        sg = (s0, s1)[g]
        y_off = _bdot(cg, sg, "nn") * jnp.exp(ac_l)
        st = _bdot(bg, xdt * jnp.exp(tot_l - ac_l), "tn")
        new_s.append(jnp.exp(tot_l) * sg + st)
        y = yd + y_off + jnp.where(lo, _col(dsk, h0), _col(dsk, h1)) * x
        y = y * _silu(z[:, SLAB * g:SLAB * (g + 1)])
        ys.append(_rms(y, ng[:, SLAB * g:SLAB * (g + 1)], SLAB))
    return jnp.concatenate(ys, axis=1), new_s[0], new_s[1]


_SSD_TILE = 512


def _ssd_fwd(c, ug, dtb, alog, dsk, ng, *, name):
    T = c.shape[0]
    tm = min(_SSD_TILE, T)
    ncs = tm // SSM_CHUNK
    nc = T // SSM_CHUNK

    def body(c_ref, ug_ref, dtb_ref, alog_ref, dsk_ref, ng_ref, y_ref, sall_ref, s_scr):
        @pl.when(pl.program_id(0) == 0)
        def _():
            s_scr[...] = jnp.zeros_like(s_scr)

        s0, s1 = s_scr[0], s_scr[1]
        for k in range(ncs):
            rows = slice(k * SSM_CHUNK, (k + 1) * SSM_CHUNK)
            sall_ref[k, 0] = s0
            sall_ref[k, 1] = s1
            y, s0, s1 = _f_ssd_chunk(c_ref[rows, :], ug_ref[rows, :], s0, s1, dtb_ref[...], alog_ref[...],
                                     dsk_ref[...], ng_ref[...])
            y_ref[rows, :] = y
        s_scr[0] = s0
        s_scr[1] = s1

    y, sall = pl.pallas_call(
        body,
        name=name,
        grid=(T // tm,),
        in_specs=[_row_spec(c, tm), _row_spec(ug, tm)] + [_full_spec(p) for p in (dtb, alog, dsk, ng)],
        out_specs=[pl.BlockSpec((tm, 256), lambda i: (i, 0)), pl.BlockSpec((ncs, 2, SLAB, SLAB), lambda i: (i, 0, 0, 0))],
        out_shape=[jax.ShapeDtypeStruct((T, 256), F32), jax.ShapeDtypeStruct((nc, 2, SLAB, SLAB), F32)],
        scratch_shapes=[pltpu.VMEM((2, SLAB, SLAB), F32)],
        compiler_params=_cparams(("arbitrary",)),
    )(c, ug, dtb, alog, dsk, ng)
    return y, sall


def _ssd_bwd(c, ug, sall, dy, dtb, alog, dsk, ng, *, name):
    T = c.shape[0]
    tm = min(_SSD_TILE, T)
    ncs = tm // SSM_CHUNK
    nt = T // tm

    def body(c_ref, ug_ref, sall_ref, dy_ref, dtb_ref, alog_ref, dsk_ref, ng_ref,
             dc_ref, dug_ref, ddtb_ref, dalog_ref, ddsk_ref, dng_ref, ds_scr):
        @pl.when(pl.program_id(0) == 0)
        def _():
            ds_scr[...] = jnp.zeros_like(ds_scr)
            for o in (ddtb_ref, dalog_ref, ddsk_ref, dng_ref):
                o[...] = jnp.zeros_like(o)

        ds0, ds1 = ds_scr[0], ds_scr[1]
        for k in reversed(range(ncs)):
            rows = slice(k * SSM_CHUNK, (k + 1) * SSM_CHUNK)
            prim = (c_ref[rows, :].astype(F32), ug_ref[rows, :].astype(F32), sall_ref[k, 0], sall_ref[k, 1],
                    dtb_ref[...], alog_ref[...], dsk_ref[...], ng_ref[...])
            _, vjp = jax.vjp(_f_ssd_chunk, *prim)
            dc, dug, ds0, ds1, g_dtb, g_alog, g_dsk, g_ng = vjp((dy_ref[rows, :].astype(F32), ds0, ds1))
            dc_ref[rows, :] = dc
            dug_ref[rows, :] = dug
            ddtb_ref[...] += g_dtb
            dalog_ref[...] += g_alog
            ddsk_ref[...] += g_dsk
            dng_ref[...] += g_ng
        ds_scr[0] = ds0
        ds_scr[1] = ds1

    rev = lambda i: (nt - 1 - i, 0)
    params = (dtb, alog, dsk, ng)
    res = pl.pallas_call(
        body,
        name=name,
        grid=(nt,),
        in_specs=[pl.BlockSpec((tm, c.shape[1]), rev), pl.BlockSpec((tm, ug.shape[1]), rev),
                  pl.BlockSpec((ncs, 2, SLAB, SLAB), lambda i: (nt - 1 - i, 0, 0, 0)), pl.BlockSpec((tm, 256), rev)]
        + [_full_spec(p) for p in params],
        out_specs=[pl.BlockSpec((tm, 512), rev), pl.BlockSpec((tm, U_GATE), rev)] + [_full_spec(p) for p in params],
        out_shape=[jax.ShapeDtypeStruct((T, 512), F32), jax.ShapeDtypeStruct((T, U_GATE), F32)]
        + [jax.ShapeDtypeStruct(p.shape, F32) for p in params],
        scratch_shapes=[pltpu.VMEM((2, SLAB, SLAB), F32)],
        compiler_params=_cparams(("arbitrary",)),
    )(c, ug, sall, dy, *params)
    return res


_CONV_TILE = 512
_HALO = 8
_CONV_W = 4


def _conv_fwd(u, w, b, *, name):
    T, C = u.shape
    tm = min(_CONV_TILE, T)
    hb = tm // _HALO

    def body(u_ref, prev_ref, w_ref, b_ref, y1_ref, y2_ref, ext):
        i = pl.program_id(0)
        ext[0:_HALO, :] = jnp.where(i > 0, prev_ref[...], 0.0)
        ext[_HALO:, :] = u_ref[...]
        y = jnp.broadcast_to(b_ref[...], (tm, C))
        for k in range(_CONV_W):
            y = y + ext[_HALO - (_CONV_W - 1) + k:_HALO - (_CONV_W - 1) + k + tm, :] * w_ref[k:k + 1, :]
        y1_ref[...] = y[:, 0:512]
        y2_ref[...] = y[:, 512:768]

    return pl.pallas_call(
        body,
        name=name,
        grid=(T // tm,),
        in_specs=[_row_spec(u, tm), pl.BlockSpec((_HALO, C), lambda i: (jnp.maximum(i * hb - 1, 0), 0)),
                  _full_spec(w), _full_spec(b)],
        out_specs=[pl.BlockSpec((tm, 512), lambda i: (i, 0)), pl.BlockSpec((tm, 256), lambda i: (i, 0))],
        out_shape=[jax.ShapeDtypeStruct((T, 512), F32), jax.ShapeDtypeStruct((T, 256), F32)],
        scratch_shapes=[pltpu.VMEM((tm + _HALO, C), F32)],
        compiler_params=_cparams(("parallel",)),
    )(u, u, w, b)


def _conv_bwd(u, dy1, dy2, w, *, name):
    T, C = u.shape
    tm = min(_CONV_TILE, T)
    hb = tm // _HALO
    nt = T // tm

    def body(u_ref, prev_ref, dy1_ref, next1_ref, dy2_ref, next2_ref, w_ref, du_ref, dw_ref, db_ref, ext, dext):
        i = pl.program_id(0)
        ext[0:_HALO, :] = jnp.where(i > 0, prev_ref[...], 0.0)
        ext[_HALO:, :] = u_ref[...]
        dext[0:tm, 0:512] = dy1_ref[...]
        dext[0:tm, 512:768] = dy2_ref[...]
        dext[tm:, 0:512] = jnp.where(i < nt - 1, next1_ref[...], 0.0)
        dext[tm:, 512:768] = jnp.where(i < nt - 1, next2_ref[...], 0.0)

        @pl.when(i == 0)
        def _():
            dw_ref[...] = jnp.zeros_like(dw_ref)
            db_ref[...] = jnp.zeros_like(db_ref)

        dy = dext[0:tm, :]
        du = jnp.zeros((tm, C), F32)
        for k in range(_CONV_W):
            du = du + dext[_CONV_W - 1 - k:_CONV_W - 1 - k + tm, :] * w_ref[k:k + 1, :]
            xk = ext[_HALO - (_CONV_W - 1) + k:_HALO - (_CONV_W - 1) + k + tm, :]
            dw_ref[k:k + 1, :] += jnp.sum(dy * xk, axis=0, keepdims=True)
        du_ref[...] = du
        db_ref[...] += jnp.sum(dy, axis=0, keepdims=True)

    nxt = lambda i: (jnp.minimum((i + 1) * hb, T // _HALO - 1), 0)
    return pl.pallas_call(
        body,
        name=name,
        grid=(nt,),
        in_specs=[_row_spec(u, tm), pl.BlockSpec((_HALO, C), lambda i: (jnp.maximum(i * hb - 1, 0), 0)),
                  _row_spec(dy1, tm), pl.BlockSpec((_HALO, 512), nxt), _row_spec(dy2, tm), pl.BlockSpec((_HALO, 256), nxt),
                  _full_spec(w)],
        out_specs=[pl.BlockSpec((tm, C), lambda i: (i, 0)), _full_spec(w), pl.BlockSpec((1, C), lambda i: (0, 0))],
        out_shape=[jax.ShapeDtypeStruct((T, C), F32), jax.ShapeDtypeStruct(w.shape, F32), jax.ShapeDtypeStruct((1, C), F32)],
        scratch_shapes=[pltpu.VMEM((tm + _HALO, C), F32), pltpu.VMEM((tm + _HALO, C), F32)],
        compiler_params=_cparams(("arbitrary",)),
    )(u, u, dy1, dy1, dy2, dy2, w)


_SCAN_TILE = 1024
_SUB = 8


def _shift_rows(x, d, fill, up):
    r = lax.broadcasted_iota(jnp.int32, x.shape, 0)
    if up:
        return jnp.where(r < _SUB - d, pltpu.roll(x, _SUB - d, 0), fill)
    return jnp.where(r >= d, pltpu.roll(x, d, 0), fill)


def _lru_scan_fwd(a, b, *, name):
    T, W = a.shape
    tr = min(_SCAN_TILE, T)

    def body(a_ref, b_ref, h_ref, hp_ref, carry):
        @pl.when(pl.program_id(0) == 0)
        def _():
            carry[...] = jnp.zeros_like(carry)

        def step(t, cr):
            rows = pl.ds(pl.multiple_of(t * _SUB, _SUB), _SUB)
            aa, bb = a_ref[rows, :], b_ref[rows, :]
            for d in (1, 2, 4):
                bb = bb + aa * _shift_rows(bb, d, 0.0, False)
                aa = aa * _shift_rows(aa, d, 1.0, False)
            h = bb + aa * cr
            h_ref[rows, :] = h
            r = lax.broadcasted_iota(jnp.int32, h.shape, 0)
            hp_ref[rows, :] = jnp.where(r >= 1, pltpu.roll(h, 1, 0), cr)
            return jnp.broadcast_to(h[_SUB - 1:_SUB, :], (_SUB, W))

        carry[...] = lax.fori_loop(0, tr // _SUB, step, carry[...])

    return pl.pallas_call(
        body,
        name=name,
        grid=(T // tr,),
        in_specs=[_row_spec(a, tr), _row_spec(b, tr)],
        out_specs=[pl.BlockSpec((tr, W), lambda i: (i, 0))] * 2,
        out_shape=[jax.ShapeDtypeStruct((T, W), F32)] * 2,
        scratch_shapes=[pltpu.VMEM((_SUB, W), F32)],
        compiler_params=_cparams(("arbitrary",)),
    )(a, b)


def _lru_scan_bwd(a, dh, hprev, *, name):
    T, W = a.shape
    tr = min(_SCAN_TILE, T)
    nt = T // tr

    def body(a_ref, dh_ref, hp_ref, g_ref, da_ref, carry):
        @pl.when(pl.program_id(0) == 0)
        def _():
            carry[...] = jnp.zeros_like(carry)

        nsub = tr // _SUB

        def step(s, cr):
            t = nsub - 1 - s
            rows = pl.ds(pl.multiple_of(t * _SUB, _SUB), _SUB)
            a_t = a_ref[rows, :]
            aa = _shift_rows(a_t, 1, 1.0, True)
            bb = dh_ref[rows, :]
            for d in (1, 2, 4):
                bb = bb + aa * _shift_rows(bb, d, 0.0, True)
                aa = aa * _shift_rows(aa, d, 1.0, True)
            g = bb + aa * cr
            g_ref[rows, :] = g
            da_ref[rows, :] = g * hp_ref[rows, :]
            return jnp.broadcast_to(a_t[0:1, :] * g[0:1, :], (_SUB, W))

        carry[...] = lax.fori_loop(0, nsub, step, carry[...])

    rev = lambda i: (nt - 1 - i, 0)
    return pl.pallas_call(
        body,
        name=name,
        grid=(nt,),
        in_specs=[pl.BlockSpec((tr, W), rev)] * 3,
        out_specs=[pl.BlockSpec((tr, W), rev)] * 2,
        out_shape=[jax.ShapeDtypeStruct((T, W), F32)] * 2,
        scratch_shapes=[pltpu.VMEM((_SUB, W), F32)],
        compiler_params=_cparams(("arbitrary",)),
    )(a, dh, hprev)


_ATT_BLK = 512
_ATT_SCALE = 1.0 / math.sqrt(NOPE + ROPE)


def _attn_fwd(q, kv, *, name):
    T = q.shape[0]
    blk = min(_ATT_BLK, T)
    nq = T // blk

    def body(q_ref, kv_ref, o_ref, lse_ref):
        i = pl.program_id(1)
        qb = q_ref[...]
        qpos = i * blk + lax.broadcasted_iota(jnp.int32, (blk, blk), 0)

        def step(j, carry, masked):
            m, l, acc = carry
            rows = pl.ds(pl.multiple_of(j * blk, blk), blk)
            kb, vb = kv_ref[rows, 0:SLAB], kv_ref[rows, SLAB:2 * SLAB]
            s = _dot(qb, kb, "nt") * _ATT_SCALE
            if masked:
                kpos = j * blk + lax.broadcasted_iota(jnp.int32, (blk, blk), 1)
                s = jnp.where(kpos <= qpos, s, -jnp.inf)
            m_new = jnp.maximum(m, jnp.max(s, axis=-1, keepdims=True))
            alpha = jnp.exp(m - m_new)
            p = jnp.exp(s - m_new)
            l = alpha * l + jnp.sum(p, axis=-1, keepdims=True)
            acc = alpha * acc + _dot(p, vb, "nn")
            return m_new, l, acc

        init = (jnp.full((blk, 1), -jnp.inf, F32), jnp.zeros((blk, 1), F32), jnp.zeros((blk, SLAB), F32))
        carry = lax.fori_loop(0, i, functools.partial(step, masked=False), init)
        m, l, acc = step(i, carry, True)
        o_ref[...] = acc / l
        lse = jnp.broadcast_to(m + jnp.log(l), (blk, SLAB))
        lse_ref[...] = lse.T[0:_SUB, :]

    return pl.pallas_call(
        body,
        name=name,
        grid=(HEADS, nq),
        in_specs=[pl.BlockSpec((blk, SLAB), lambda h, i: (i, h)), pl.BlockSpec((T, 2 * SLAB), lambda h, i: (0, h))],
        out_specs=[pl.BlockSpec((blk, SLAB), lambda h, i: (i, h)),
                   pl.BlockSpec((None, None, _SUB, blk), lambda h, i: (h, i, 0, 0))],
        out_shape=[jax.ShapeDtypeStruct((T, HEADS * SLAB), F32), jax.ShapeDtypeStruct((HEADS, nq, _SUB, blk), F32)],
        compiler_params=_cparams(("parallel", "arbitrary")),
    )(q, kv)


def _attn_delta(do, o, *, name):
    T = o.shape[0]
    blk = min(_ATT_BLK, T)
    nq = T // blk

    def body(do_ref, o_ref, d_ref):
        dl = jnp.sum(do_ref[...].astype(F32) * o_ref[...], axis=-1, keepdims=True)
        d_ref[...] = jnp.broadcast_to(dl, (blk, SLAB)).T[0:_SUB, :]

    return pl.pallas_call(
        body,
        name=name,
        grid=(HEADS, nq),
        in_specs=[pl.BlockSpec((blk, SLAB), lambda h, i: (i, h))] * 2,
        out_specs=pl.BlockSpec((None, None, _SUB, blk), lambda h, i: (h, i, 0, 0)),
        out_shape=jax.ShapeDtypeStruct((HEADS, nq, _SUB, blk), F32),
        compiler_params=_cparams(("parallel", "parallel")),
    )(do, o)


def _attn_bwd(q, kv, do, lse, delta, *, name):
    T = q.shape[0]
    blk = min(_ATT_BLK, T)
    nq = T // blk

    def body(q_ref, kv_ref, do_ref, lse_ref, dl_ref, dqt_ref, dkv_ref):
        j = pl.program_id(1)

        @pl.when(j == 0)
        def _():
            dqt_ref[...] = jnp.zeros_like(dqt_ref)

        kb, vb = kv_ref[:, 0:SLAB], kv_ref[:, SLAB:2 * SLAB]
        kbt = kb.astype(F32).T.astype(BF16)
        kpos = j * blk + lax.broadcasted_iota(jnp.int32, (blk, blk), 0)

        def step(i, carry, masked):
            dk, dv = carry
            rows = pl.ds(pl.multiple_of(i * blk, blk), blk)
            qb, dob = q_ref[rows, :], do_ref[rows, :]
            st = _dot(kb, qb, "nt") * _ATT_SCALE
            if masked:
                qpos = i * blk + lax.broadcasted_iota(jnp.int32, (blk, blk), 1)
                st = jnp.where(kpos <= qpos, st, -jnp.inf)
            pt = jnp.exp(st - lse_ref[i, 0:1, :])
            dpt = _dot(vb, dob, "nt")
            dst = pt * (dpt - dl_ref[i, 0:1, :]) * _ATT_SCALE
            dv = dv + _dot(pt, dob, "nn")
            dk = dk + _dot(dst, qb, "nn")
            dqt_ref[i] += _dot(kbt, dst, "nn")
            return dk, dv

        zero = jnp.zeros((blk, SLAB), F32)
        carry = step(j, (zero, zero), True)
        dk, dv = lax.fori_loop(j + 1, nq, functools.partial(step, masked=False), carry)
        dkv_ref[:, 0:SLAB] = dk.astype(BF16)
        dkv_ref[:, SLAB:2 * SLAB] = dv.astype(BF16)

    stat_spec = pl.BlockSpec((None, nq, _SUB, blk), lambda h, j: (h, 0, 0, 0))
    return pl.pallas_call(
        body,
        name=name,
        grid=(HEADS, nq),
        in_specs=[pl.BlockSpec((T, SLAB), lambda h, j: (0, h)), pl.BlockSpec((blk, 2 * SLAB), lambda h, j: (j, h)),
                  pl.BlockSpec((T, SLAB), lambda h, j: (0, h)), stat_spec, stat_spec],
        out_specs=[pl.BlockSpec((None, nq, SLAB, blk), lambda h, j: (h, 0, 0, 0)),
                   pl.BlockSpec((blk, 2 * SLAB), lambda h, j: (j, h))],
        out_shape=[jax.ShapeDtypeStruct((HEADS, nq, SLAB, blk), F32), jax.ShapeDtypeStruct((T, HEADS * 2 * SLAB), BF16)],
        compiler_params=_cparams(("parallel", "arbitrary")),
    )(q, kv, do, lse, delta)


def _qrope_bwd(dqt, cq, sq, *, name):
    _, nq, _, blk = dqt.shape
    T = nq * blk

    def body(dqt_ref, c_ref, s_ref, dy_ref):
        c, s = c_ref[...], s_ref[...]
        for h in range(HEADS):
            dq = dqt_ref[h].T
            dy_ref[:, h * SLAB:(h + 1) * SLAB] = (dq * c).astype(BF16)
            dy_ref[:, (HEADS + h) * SLAB:(HEADS + h + 1) * SLAB] = (dq * s).astype(BF16)

    return pl.pallas_call(
        body,
        name=name,
        grid=(nq,),
        in_specs=[pl.BlockSpec((HEADS, None, SLAB, blk), lambda i: (0, i, 0, 0)), _row_spec(cq, blk), _row_spec(sq, blk)],
        out_specs=pl.BlockSpec((blk, 2 * HEADS * SLAB), lambda i: (i, 0)),
        out_shape=jax.ShapeDtypeStruct((T, 2 * HEADS * SLAB), BF16),
        compiler_params=_cparams(("parallel",)),
    )(dqt, cq, sq)


def _loss_head(x, tgt, g, *, name, tm=256):
    T = x.shape[0]
    tm = min(tm, T)

    def body(x_ref, t_ref, g_ref, loss_ref, dx_ref, dg_ref):
        def f(xv, gv):
            e = _rms(xv, gv, D) - t_ref[...]
            row = jnp.sum(e * e, axis=1, keepdims=True)
            return jnp.sum(row, axis=0, keepdims=True) * (0.5 / D)

        val, vjp = jax.vjp(f, x_ref[...], g_ref[...])
        dxv, dgv = vjp(jnp.ones((1, 1), F32))

        @pl.when(pl.program_id(0) == 0)
        def _():
            loss_ref[...] = jnp.zeros_like(loss_ref)
            dg_ref[...] = jnp.zeros_like(dg_ref)

        dx_ref[...] = dxv
        dg_ref[...] += dgv
        loss_ref[...] += jnp.broadcast_to(val, loss_ref.shape)

    return pl.pallas_call(
        body,
        name=name,
        grid=(T // tm,),
        in_specs=[_row_spec(x, tm), _row_spec(tgt, tm), _full_spec(g)],
        out_specs=[pl.BlockSpec((1, SLAB), lambda i: (0, 0)), _row_spec(x, tm), _full_spec(g)],
        out_shape=[jax.ShapeDtypeStruct((1, SLAB), F32), jax.ShapeDtypeStruct(x.shape, F32), jax.ShapeDtypeStruct(g.shape, F32)],
        compiler_params=_cparams(("arbitrary",)),
    )(x, tgt, g)


def _row_tile(rows, cols, budget=256 * 1024):
    best = None
    for t in range(16, rows + 1, 16):
        if rows % t == 0 and t * cols <= budget:
            best = t
    return best or rows


def _sum_fixed(x, out_dtype, *, name):
    n, R, C = x.shape
    tr = _row_tile(R, C)

    def body(x_ref, o_ref):
        acc = x_ref[0].astype(F32)
        for k in range(1, n):
            acc = acc + x_ref[k].astype(F32)
        o_ref[...] = acc.astype(o_ref.dtype)

    return pl.pallas_call(
        body,
        name=name,
        grid=(R // tr,),
        in_specs=[pl.BlockSpec((n, tr, C), lambda i: (0, i, 0))],
        out_specs=pl.BlockSpec((tr, C), lambda i: (i, 0)),
        out_shape=jax.ShapeDtypeStruct((R, C), out_dtype),
        compiler_params=_cparams(("parallel",)),
    )(x)


def _add_pair(a, b, out_dtype, *, name):
    R, C = a.shape
    tr = _row_tile(R, C)

    def body(a_ref, b_ref, o_ref):
        o_ref[...] = (a_ref[...] + b_ref[...]).astype(o_ref.dtype)

    spec = pl.BlockSpec((tr, C), lambda i: (i, 0))
    return pl.pallas_call(
        body, name=name, grid=(R // tr,), in_specs=[spec, spec], out_specs=spec,
        out_shape=jax.ShapeDtypeStruct((R, C), out_dtype), compiler_params=_cparams(("parallel",)),
    )(a, b)


def _adamw(w, g, m, v, *, name):
    R, C = w.shape
    tr = _row_tile(R, C, 128 * 1024)

    def body(w_ref, g_ref, m_ref, v_ref, d_ref, nm_ref, nv_ref):
        gv = g_ref[...]
        mv = B1 * m_ref[...] + (1.0 - B1) * gv
        vv = B2 * v_ref[...] + (1.0 - B2) * (gv * gv)
        m_hat = mv / (1.0 - B1 ** STEP)
        v_hat = vv / (1.0 - B2 ** STEP)
        d_ref[...] = -LR * (m_hat / (jnp.sqrt(v_hat) + AEPS) + WD * w_ref[...])
        nm_ref[...] = mv
        nv_ref[...] = vv

    spec = pl.BlockSpec((tr, C), lambda i: (i, 0))
    return pl.pallas_call(
        body, name=name, grid=(R // tr,), in_specs=[spec] * 4, out_specs=[spec] * 3,
        out_shape=[jax.ShapeDtypeStruct((R, C), F32)] * 3, compiler_params=_cparams(("parallel",)),
    )(w, g, m, v)


_FLIPS = ((1, 0), (0, 1), (1, 1))
_ANY = pl.BlockSpec(memory_space=pl.ANY)


def _me():
    return lax.axis_index("x"), lax.axis_index("y"), lax.axis_index("c")


def _flip(mx, my, f):
    return (1 - mx if f[0] else mx), (1 - my if f[1] else my)


def _chip_exchange(x, gather, *, name):
    shape = x.shape if not gather else (4,) + x.shape

    def body(x_ref, out_ref, send_sems, recv_sems, local_sem):
        mx, my, mc = _me()
        mine = 2 * mx + my
        src_own = x_ref if gather else x_ref.at[mine]
        local = pltpu.make_async_copy(src_own, out_ref.at[mine], local_sem)
        local.start()
        sends = []
        for k, f in enumerate(_FLIPS):
            px, py = _flip(mx, my, f)
            src = x_ref if gather else x_ref.at[2 * px + py]
            cp = pltpu.make_async_remote_copy(src_ref=src, dst_ref=out_ref.at[mine], send_sem=send_sems.at[k],
                                              recv_sem=recv_sems.at[k], device_id=(px, py, mc), device_id_type=MESH)
            cp.start()
            sends.append(cp)
        for k, f in enumerate(_FLIPS):
            px, py = _flip(mx, my, f)
            pltpu.make_async_remote_copy(src_ref=src_own, dst_ref=out_ref.at[2 * px + py], send_sem=send_sems.at[k],
                                         recv_sem=recv_sems.at[k], device_id=(px, py, mc), device_id_type=MESH).wait_recv()
        for cp in sends:
            cp.wait_send()
        local.wait()

    return pl.pallas_call(
        body, name=name, in_specs=[_ANY], out_specs=_ANY, out_shape=jax.ShapeDtypeStruct(shape, x.dtype),
        scratch_shapes=[pltpu.SemaphoreType.DMA((3,)), pltpu.SemaphoreType.DMA((3,)), pltpu.SemaphoreType.DMA],
    )(x)


def _core_exchange(x, gather, *, name):
    shape = x.shape if not gather else (4, 2) + x.shape[1:]

    def body(x_ref, out_ref, send_sem, recv_sem, local_sem):
        mx, my, mc = _me()
        dst_mine = out_ref.at[:, mc] if gather else out_ref
        dst_other = out_ref.at[:, 1 - mc] if gather else out_ref
        if gather:
            local = pltpu.make_async_copy(x_ref, dst_mine, local_sem)
            local.start()
        cp = pltpu.make_async_remote_copy(src_ref=x_ref, dst_ref=dst_mine, send_sem=send_sem, recv_sem=recv_sem,
                                          device_id=(mx, my, 1 - mc), device_id_type=MESH)
        cp.start()
        pltpu.make_async_remote_copy(src_ref=x_ref, dst_ref=dst_other, send_sem=send_sem, recv_sem=recv_sem,
                                     device_id=(mx, my, 1 - mc), device_id_type=MESH).wait_recv()
        cp.wait_send()
        if gather:
            local.wait()

    return pl.pallas_call(
        body, name=name, in_specs=[_ANY], out_specs=_ANY, out_shape=jax.ShapeDtypeStruct(shape, x.dtype),
        scratch_shapes=[pltpu.SemaphoreType.DMA, pltpu.SemaphoreType.DMA, pltpu.SemaphoreType.DMA],
    )(x)


def _all_gather(x, *, name):
    g4 = _chip_exchange(x, True, name=name + "_chips")
    g8 = _core_exchange(g4, True, name=name + "_cores")
    return g8.reshape((N_DEV,) + x.shape)


def _reduce_scatter(x, *, name):
    _, R, C = x.shape
    mc = lax.axis_index("c")
    x = x.reshape(4, 2, R, C)
    keep = lax.dynamic_index_in_dim(x, mc, axis=1, keepdims=False)
    give = lax.dynamic_index_in_dim(x, 1 - mc, axis=1, keepdims=False)
    got = _core_exchange(give, False, name=name + "_cores")
    t = _add_pair(keep.reshape(4 * R, C), got.reshape(4 * R, C), BF16, name=name + "_add").reshape(4, R, C)
    parts = _chip_exchange(t, False, name=name + "_chips")
    return _sum_fixed(parts, F32, name=name + "_sum")


def _all_reduce(x, *, name):
    return _sum_fixed(_all_gather(x, name=name), F32, name=name + "_sum")


WEIGHTS = ['mix_norm_g', 'w_in', 'mla_q_norm_g', 'mla_kv_norm_g', 'mla_w_uq', 'mla_w_ukv', 'mla_out_g', 'ssm_conv_w',
           'ssm_conv_b', 'ssm_dt_bias', 'ssm_a_log', 'ssm_d', 'ssm_norm_g', 'lru_conv_w', 'lru_conv_b', 'lru_w_a',
           'lru_b_a', 'lru_w_i', 'lru_b_i', 'lru_lambda', 'lru_out_g', 'w_out', 'xattn_norm_g', 'mem_norm_g', 'w_mq',
           'w_mk', 'w_mv', 'w_mo', 'mlp_norm_g', 'w_mlp1', 'w_mlp2', 'final_norm_g']
ROW_SHARDED = ('w_in', 'w_out', 'w_mq', 'w_mk', 'w_mv', 'w_mo', 'w_mlp2')
COL_SHARDED = ('mla_w_uq', 'mla_w_ukv', 'w_mlp1')
BIG = tuple(n for n in WEIGHTS if n in ROW_SHARDED + COL_SHARDED)
CONV_SHARDED = ('ssm_conv_w', 'lru_conv_w')
SMALL = tuple(n for n in WEIGHTS if n not in BIG)
PACK_C = 1024


def _pack(arrs, dtype, lead=()):
    flat = jnp.concatenate([a.reshape(lead + (-1,)).astype(dtype) for a in arrs], axis=-1)
    n = flat.shape[-1]
    rows = -(-n // (16 * PACK_C)) * 16
    flat = jnp.pad(flat, [(0, 0)] * len(lead) + [(0, rows * PACK_C - n)])
    return flat.reshape(lead + (rows, PACK_C))


def _unpack(packed, shapes, lead=()):
    flat = packed.reshape(lead + (-1,))
    out, off = [], 0
    for s in shapes:
        n = math.prod(s)
        out.append(flat[..., off:off + n].reshape(lead + tuple(s)))
        off += n
    return out


def _to_full(name, g):
    _, L, a, b = g.shape
    if name in ROW_SHARDED:
        return g.transpose(1, 0, 2, 3).reshape(L, N_DEV * a, b)
    return g.transpose(1, 2, 0, 3).reshape(L, a, N_DEV * b)


def _to_shards(name, full):
    L, K, N = full.shape
    if name in ROW_SHARDED:
        return full.reshape(L, N_DEV, K // N_DEV, N).transpose(1, 0, 2, 3)
    return full.reshape(L, K, N_DEV, N // N_DEV).transpose(2, 0, 1, 3)


def _pad_lanes(v, n=SLAB):
    return jnp.pad(v.astype(F32), (0, n - v.shape[0])).reshape(1, n)


def _prep_layer(W, l):
    P = {}
    w_in = W['w_in'][l]
    kr = w_in[:, 384:416]
    zc = lambda k: jnp.zeros((D, k), w_in.dtype)
    P['win'] = jnp.concatenate(
        [w_in[:, 0:384], kr, zc(96), kr[:, 16:32], kr[:, 0:16], zc(96),
         w_in[:, 416:672], w_in[:, 1444:1700], w_in[:, 1184:1188], zc(124),
         w_in[:, 672:1184], w_in[:, 1188:1444]], axis=1)
    wuq = W['mla_w_uq'][l].reshape(Q_RANK, HEADS, NOPE + ROPE)
    nope, r1, r2 = wuq[..., :NOPE], wuq[..., NOPE:NOPE + 16], wuq[..., NOPE + 16:]
    zq = lambda k: jnp.zeros((Q_RANK, HEADS, k), wuq.dtype)
    P['wq'] = jnp.concatenate([jnp.concatenate([nope, r1, r2, zq(32)], -1).reshape(Q_RANK, HEADS * SLAB),
                               jnp.concatenate([zq(64), r2, r1, zq(32)], -1).reshape(Q_RANK, HEADS * SLAB)], axis=1)
    wukv = W['mla_w_ukv'][l].reshape(KV_RANK, HEADS, NOPE + VDIM)
    zk = jnp.zeros((KV_RANK, HEADS, 64), wukv.dtype)
    top = jnp.concatenate([wukv[..., :NOPE], zk, wukv[..., NOPE:], zk], -1)
    ri, ci = jnp.arange(SLAB)[:, None], jnp.arange(2 * SLAB)[None, :]
    sel = ((ri < ROPE) & (ci == ri + NOPE)).astype(wukv.dtype)
    bot = jnp.broadcast_to(sel[:, None, :], (SLAB, HEADS, 2 * SLAB))
    P['wkv'] = jnp.concatenate([top, bot], axis=0).reshape(2 * SLAB, HEADS * 2 * SLAB)
    wout = W['w_out'][l]
    mla_rows = jnp.pad(wout[:HEADS * VDIM].reshape(HEADS, VDIM, D), ((0, 0), (0, SLAB - VDIM), (0, 0)))
    P['wout'] = jnp.concatenate([mla_rows.reshape(HEADS * SLAB, D), wout[HEADS * VDIM:]], axis=0)
    for n in ('w_mq', 'w_mk', 'w_mv', 'w_mo', 'w_mlp1', 'w_mlp2'):
        P[n] = W[n][l]
    row = lambda n: W[n][l].astype(F32).reshape(1, -1)
    for n in ('mix_norm_g', 'mla_q_norm_g', 'mla_kv_norm_g', 'ssm_norm_g', 'lru_lambda', 'lru_out_g', 'xattn_norm_g',
              'mem_norm_g', 'mlp_norm_g'):
        P[n] = row(n)
    P['mla_out_g'] = jnp.pad(W['mla_out_g'][l].astype(F32).reshape(HEADS, VDIM), ((0, 0), (0, SLAB - VDIM))).reshape(1, -1)
    for n in ('ssm_dt_bias', 'ssm_a_log', 'ssm_d'):
        P[n] = _pad_lanes(W[n][l])
    P['conv_w'] = jnp.pad(jnp.concatenate([W['ssm_conv_w'][l], W['lru_conv_w'][l]], axis=1).astype(F32), ((0, 4), (0, 0)))
    P['conv_b'] = jnp.concatenate([W['ssm_conv_b'][l], W['lru_conv_b'][l]]).astype(F32).reshape(1, -1)
    for n in ('lru_w_a', 'lru_w_i'):
        P[n] = jnp.concatenate([jnp.pad(W[n][l, k].astype(F32), ((0, 0), (64 * k, 192 - 64 * k))) for k in range(4)], axis=0)
    for n in ('lru_b_a', 'lru_b_i'):
        P[n] = W[n][l].astype(F32).reshape(1, -1)
    return P


def _unprep_layer(G):
    o = {}
    m = G['win']
    gk = m[:, 384:416] + jnp.concatenate([m[:, 528:544], m[:, 512:528]], axis=1)
    o['w_in'] = jnp.concatenate([m[:, 0:384], gk, m[:, 640:896], m[:, 1280:1792], m[:, 1152:1156], m[:, 1792:2048],
                                 m[:, 896:1152]], axis=1)
    q0 = G['wq'][:, :HEADS * SLAB].reshape(Q_RANK, HEADS, SLAB)
    q1 = G['wq'][:, HEADS * SLAB:].reshape(Q_RANK, HEADS, SLAB)
    o['mla_w_uq'] = jnp.concatenate([q0[..., :64], q0[..., 64:80] + q1[..., 80:96], q0[..., 80:96] + q1[..., 64:80]],
                                    -1).reshape(Q_RANK, HEADS * (NOPE + ROPE))
    kvg = G['wkv'][:KV_RANK].reshape(KV_RANK, HEADS, 2 * SLAB)
    o['mla_w_ukv'] = jnp.concatenate([kvg[..., :NOPE], kvg[..., SLAB:SLAB + VDIM]], -1).reshape(KV_RANK, HEADS * SLAB)
    wo = G['wout']
    o['w_out'] = jnp.concatenate([wo[:HEADS * SLAB].reshape(HEADS, SLAB, D)[:, :VDIM].reshape(HEADS * VDIM, D),
                                  wo[HEADS * SLAB:]], axis=0)
    for n in ('w_mq', 'w_mk', 'w_mv', 'w_mo', 'w_mlp1', 'w_mlp2'):
        o[n] = G[n]
    for n in ('mix_norm_g', 'mla_q_norm_g', 'mla_kv_norm_g', 'ssm_norm_g', 'lru_lambda', 'lru_out_g', 'xattn_norm_g',
              'mem_norm_g', 'mlp_norm_g', 'lru_b_a', 'lru_b_i'):
        o[n] = G[n].reshape(-1)
    o['lru_b_a'] = o['lru_b_a'].reshape(4, 64)
    o['lru_b_i'] = o['lru_b_i'].reshape(4, 64)
    o['mla_out_g'] = G['mla_out_g'].reshape(HEADS, SLAB)[:, :VDIM].reshape(-1)
    for n in ('ssm_dt_bias', 'ssm_a_log', 'ssm_d'):
        o[n] = G[n][0, :4]
    o['ssm_conv_w'], o['lru_conv_w'] = G['conv_w'][:4, :512], G['conv_w'][:4, 512:]
    o['ssm_conv_b'], o['lru_conv_b'] = G['conv_b'][0, :512], G['conv_b'][0, 512:]
    for n in ('lru_w_a', 'lru_w_i'):
        o[n] = jnp.stack([G[n][64 * k:64 * (k + 1), 64 * k:64 * (k + 1)] for k in range(4)])
    return o


def _rope_tables(positions):
    half = ROPE // 2
    inv_freq = ROPE_THETA ** (-jnp.arange(half, dtype=F32) * 2.0 / ROPE)
    ang = positions.astype(F32)[:, None] * inv_freq
    cos, sin = jnp.cos(ang), jnp.sin(ang)
    T = positions.shape[0]
    z = lambda k: jnp.zeros((T, k), F32)
    ck = jnp.concatenate([cos, cos, z(96)], axis=1)
    sk = jnp.concatenate([-sin, sin, z(96)], axis=1)
    cq = jnp.concatenate([jnp.ones((T, NOPE), F32), cos, cos, z(32)], axis=1)
    sq = jnp.concatenate([z(NOPE), -sin, sin, z(32)], axis=1)
    return ck, sk, cq, sq


def _add_epi(acc, res):
    return (acc + res,)


def _relu2_epi(acc):
    r = jnp.maximum(acc, 0.0)
    return r, r * r


def _drelu2_epi(acc, r):
    return (acc * (2.0 * r.astype(F32)),)


def _norm(x, g, name):
    return _rows_fwd(_f_norm, [x], [g], [(x.shape[1], BF16)], name=name)[0]


def _norm_bwd(x, g, ct, add, name):
    (dx,), (dg,) = _rows_vjp(_f_norm, [x], [g], [ct], name=name, drows=[0], dparams=[0], drow_dtypes=[F32], add=add)
    return dx, dg


def _layer_fwd(x0, mem, P, tabs):
    ck, sk, cq, sq = tabs
    S = {'x0': x0}
    h1 = S['h1'] = _norm(x0, P['mix_norm_g'], "norm_mix")
    win = P['win']
    u_mla = S['u_mla'] = _mm(h1, win[:, 0:U_MLA], name="in_mla")
    u_gate = S['u_gate'] = _mm(h1, win[:, U_MLA:U_MLA + U_GATE], name="in_gate")
    u_conv = S['u_conv'] = _mm(h1, win[:, U_MLA + U_GATE:], name="in_conv")
    cqn, akv = _rows_fwd(_f_mla_prep, [u_mla, ck, sk], [P['mla_q_norm_g'], P['mla_kv_norm_g']],
                         [(Q_RANK, BF16), (2 * SLAB, BF16)], name="mla_prep")
    S['cqn'], S['akv'] = cqn, akv
    yq = _mm(cqn, P['wq'], name="q_proj")
    q = S['q'] = _rows_fwd(_f_qrope, [yq, cq, sq], [], [(HEADS * SLAB, BF16)], name="q_rope")[0]
    kv = S['kv'] = _mm(akv, P['wkv'], name="kv_proj", out_dtypes=(BF16,))
    o, lse = _attn_fwd(q, kv, name="attn_fwd")
    S['o'], S['lse'] = o, lse
    c_ssm, c_lru = _conv_fwd(u_conv, P['conv_w'], P['conv_b'], name="conv_fwd")
    S['c_ssm'], S['c_lru'] = c_ssm, c_lru
    ys, sall = _ssd_fwd(c_ssm, u_gate, P['ssm_dt_bias'], P['ssm_a_log'], P['ssm_d'], P['ssm_norm_g'], name="ssd_fwd")
    S['ys'], S['sall'] = ys, sall
    a, b = _rows_fwd(_f_lru_gates, [c_lru], [P['lru_w_a'], P['lru_b_a'], P['lru_w_i'], P['lru_b_i'], P['lru_lambda']],
                     [(256, F32), (256, F32)], name="lru_gates")
    h, hprev = _lru_scan_fwd(a, b, name="lru_scan")
    S['a'], S['h'], S['hprev'] = a, h, hprev
    ymix = S['ymix'] = _rows_fwd(_f_mix, [o, ys, h, u_gate], [P['mla_out_g'], P['lru_out_g']],
                                 [(HEADS * SLAB + 512, BF16)], name="mix")[0]
    x1 = S['x1'] = _mm(ymix, P['wout'], name="out_proj", epi=_add_epi, extras=(x0,))
    hx = S['hx'] = _norm(x1, P['xattn_norm_g'], "norm_xattn")
    qx = S['qx'] = _mm(hx, P['w_mq'], name="mem_q", out_dtypes=(BF16,))
    mn = S['mn'] = _norm(mem, P['mem_norm_g'], "norm_mem")
    kx = S['kx'] = _mm(mn, P['w_mk'], name="mem_k", out_dtypes=(BF16,))
    vx = S['vx'] = _mm(mn, P['w_mv'], name="mem_v", out_dtypes=(BF16,))
    ox = S['ox'] = _rows_fwd(_f_xattn, [qx], [kx, vx], [(D, BF16)], name="xattn")[0]
    x2 = S['x2'] = _mm(ox, P['w_mo'], name="mem_o", epi=_add_epi, extras=(x1,))
    hm = S['hm'] = _norm(x2, P['mlp_norm_g'], "norm_mlp")
    r, s = _mm(hm, P['w_mlp1'], name="mlp_up", epi=_relu2_epi, out_dtypes=(BF16, BF16))
    S['r'], S['s'] = r, s
    x3 = _mm(s, P['w_mlp2'], name="mlp_down", epi=_add_epi, extras=(x2,))
    return x3, S


def _layer_bwd(dx3, mem, S, P, tabs):
    ck, sk, cq, sq = tabs
    G = {}
    da = _mm(dx3, P['w_mlp2'], "nt", name="mlp_down_dx", epi=_drelu2_epi, extras=(S['r'],), out_dtypes=(BF16,))
    G['w_mlp2'] = _mm(S['s'], dx3, "tn", name="mlp_down_dw")
    G['w_mlp1'] = _mm(S['hm'], da, "tn", name="mlp_up_dw")
    dhm = _mm(da, P['w_mlp1'], "nt", name="mlp_up_dx")
    dx2, G['mlp_norm_g'] = _norm_bwd(S['x2'], P['mlp_norm_g'], dhm, dx3, "norm_mlp_bwd")
    dox = _mm(dx2, P['w_mo'], "nt", name="mem_o_dx")
    G['w_mo'] = _mm(S['ox'], dx2, "tn", name="mem_o_dw")
    (dqx,), (dkx, dvx) = _rows_vjp(_f_xattn, [S['qx']], [S['kx'], S['vx']], [dox], name="xattn_bwd", drows=[0],
                                   dparams=[0, 1], drow_dtypes=[BF16])
    G['w_mq'] = _mm(S['hx'], dqx, "tn", name="mem_q_dw")
    dhx = _mm(dqx, P['w_mq'], "nt", name="mem_q_dx")
    dx1, G['xattn_norm_g'] = _norm_bwd(S['x1'], P['xattn_norm_g'], dhx, dx2, "norm_xattn_bwd")
    G['w_mk'] = _mm(S['mn'], dkx, "tn", name="mem_k_dw")
    G['w_mv'] = _mm(S['mn'], dvx, "tn", name="mem_v_dw")
    dmn = _mm(dkx, P['w_mk'], "nt", name="mem_k_dx", epi=_add_epi, extras=(_mm(dvx, P['w_mv'], "nt", name="mem_v_dx"),))
    _, G['mem_norm_g'] = _norm_bwd(mem, P['mem_norm_g'], dmn, None, "norm_mem_bwd")
    dymix = _mm(dx1, P['wout'], "nt", name="out_proj_dx")
    G['wout'] = _mm(S['ymix'], dx1, "tn", name="out_proj_dw")
    (do, dys, dh, dug_mix), (G['mla_out_g'], G['lru_out_g']) = _rows_vjp(
        _f_mix, [S['o'], S['ys'], S['h'], S['u_gate']], [P['mla_out_g'], P['lru_out_g']], [dymix], name="mix_bwd",
        drows=[0, 1, 2, 3], dparams=[0, 1], drow_dtypes=[BF16, F32, F32, F32])
    g, da_lru = _lru_scan_bwd(S['a'], dh, S['hprev'], name="lru_scan_bwd")
    lru_par = [P['lru_w_a'], P['lru_b_a'], P['lru_w_i'], P['lru_b_i'], P['lru_lambda']]
    (dc_lru,), dpar = _rows_vjp(_f_lru_gates, [S['c_lru']], lru_par, [da_lru, g], name="lru_gates_bwd", drows=[0],
                                dparams=[0, 1, 2, 3, 4], drow_dtypes=[F32])
    G['lru_w_a'], G['lru_b_a'], G['lru_w_i'], G['lru_b_i'], G['lru_lambda'] = dpar
    dc_ssm, dug_ssd, G['ssm_dt_bias'], G['ssm_a_log'], G['ssm_d'], G['ssm_norm_g'] = _ssd_bwd(
        S['c_ssm'], S['u_gate'], S['sall'], dys, P['ssm_dt_bias'], P['ssm_a_log'], P['ssm_d'], P['ssm_norm_g'],
        name="ssd_bwd")
    du_conv, G['conv_w'], G['conv_b'] = _conv_bwd(S['u_conv'], dc_ssm, dc_lru, P['conv_w'], name="conv_bwd")
    delta = _attn_delta(do, S['o'], name="attn_delta")
    dqt, dkv = _attn_bwd(S['q'], S['kv'], do, S['lse'], delta, name="attn_bwd")
    dyq = _qrope_bwd(dqt, cq, sq, name="q_rope_bwd")
    dcqn = _mm(dyq, P['wq'], "nt", name="q_proj_dx")
    G['wq'] = _mm(S['cqn'], dyq, "tn", name="q_proj_dw")
    dakv = _mm(dkv, P['wkv'], "nt", name="kv_proj_dx")
    G['wkv'] = _mm(S['akv'], dkv, "tn", name="kv_proj_dw")
    (du_mla,), (G['mla_q_norm_g'], G['mla_kv_norm_g']) = _rows_vjp(
        _f_mla_prep, [S['u_mla'], ck, sk], [P['mla_q_norm_g'], P['mla_kv_norm_g']], [dcqn, dakv], name="mla_prep_bwd",
        drows=[0], dparams=[0, 1], drow_dtypes=[BF16])
    du = jnp.concatenate([du_mla, (dug_mix + dug_ssd).astype(BF16), du_conv.astype(BF16)], axis=1)
    dh1 = _mm(du, P['win'], "nt", name="in_dx")
    G['win'] = _mm(S['h1'], du, "tn", name="in_dw")
    dx0, G['mix_norm_g'] = _norm_bwd(S['x0'], P['mix_norm_g'], dh1, dx1, "norm_mix_bwd")
    return dx0, G


def _local_step(x, mem, positions, W, tgt):
    tabs = _rope_tables(positions)
    saved, preps = [], []
    for l in range(DEPTH):
        P = _prep_layer(W, l)
        x, S = _layer_fwd(x, mem, P, tabs)
        saved.append(S)
        preps.append(P)
    loss, dx, dg_final = _loss_head(x, tgt, W['final_norm_g'].astype(F32).reshape(1, D), name="loss_head")
    per_layer = [None] * DEPTH
    for l in reversed(range(DEPTH)):
        dx, G = _layer_bwd(dx, mem, saved[l], preps[l], tabs)
        per_layer[l] = _unprep_layer(G)
    grads = {n: jnp.stack([per_layer[l][n] for l in range(DEPTH)]) for n in WEIGHTS if n != 'final_norm_g'}
    grads['final_norm_g'] = dg_final.reshape(D)
    return loss, dx, grads


def _adamw_nd(w, g, m, v, name):
    shp = w.shape
    two = lambda a: a.reshape(-1, shp[-1])
    return [r.reshape(shp) for r in _adamw(two(w), two(g), two(m), two(v), name=name)]


def kernel(x, mem, positions, mix_norm_g, w_in, mla_q_norm_g, mla_kv_norm_g, mla_w_uq, mla_w_ukv, mla_out_g, ssm_conv_w, ssm_conv_b, ssm_dt_bias, ssm_a_log, ssm_d, ssm_norm_g, lru_conv_w, lru_conv_b, lru_w_a, lru_b_a, lru_w_i, lru_b_i, lru_lambda, lru_out_g, w_out, xattn_norm_g, mem_norm_g, w_mq, w_mk, w_mv, w_mo, mlp_norm_g, w_mlp1, w_mlp2, final_norm_g, loss_target, m_mix_norm_g, m_w_in, m_mla_q_norm_g, m_mla_kv_norm_g, m_mla_w_uq, m_mla_w_ukv, m_mla_out_g, m_ssm_conv_w, m_ssm_conv_b, m_ssm_dt_bias, m_ssm_a_log, m_ssm_d, m_ssm_norm_g, m_lru_conv_w, m_lru_conv_b, m_lru_w_a, m_lru_b_a, m_lru_w_i, m_lru_b_i, m_lru_lambda, m_lru_out_g, m_w_out, m_xattn_norm_g, m_mem_norm_g, m_w_mq, m_w_mk, m_w_mv, m_w_mo, m_mlp_norm_g, m_w_mlp1, m_w_mlp2, m_final_norm_g, v_mix_norm_g, v_w_in, v_mla_q_norm_g, v_mla_kv_norm_g, v_mla_w_uq, v_mla_w_ukv, v_mla_out_g, v_ssm_conv_w, v_ssm_conv_b, v_ssm_dt_bias, v_ssm_a_log, v_ssm_d, v_ssm_norm_g, v_lru_conv_w, v_lru_conv_b, v_lru_w_a, v_lru_b_a, v_lru_w_i, v_lru_b_i, v_lru_lambda, v_lru_out_g, v_w_out, v_xattn_norm_g, v_mem_norm_g, v_w_mq, v_w_mk, v_w_mv, v_w_mo, v_mlp_norm_g, v_w_mlp1, v_w_mlp2, v_final_norm_g):
    a = locals()
    w = {n: a[n] for n in WEIGHTS}
    m = {n: a['m_' + n] for n in WEIGHTS}
    v = {n: a['v_' + n] for n in WEIGHTS}
    me = 4 * lax.axis_index("x") + 2 * lax.axis_index("y") + lax.axis_index("c")

    big_shapes = [w[n].shape for n in BIG]
    gathered = _all_gather(_pack([w[n] for n in BIG], BF16), name="gather_w")
    W = {n: _to_full(n, g) for n, g in zip(BIG, _unpack(gathered, big_shapes, lead=(N_DEV,)))}
    conv_shapes = [w[n].shape for n in CONV_SHARDED]
    conv_g = _all_gather(_pack([w[n] for n in CONV_SHARDED], F32), name="gather_conv")
    for n, g in zip(CONV_SHARDED, _unpack(conv_g, conv_shapes, lead=(N_DEV,))):
        W[n] = g.transpose(1, 2, 0, 3).reshape(g.shape[1], g.shape[2], N_DEV * g.shape[3])
    for n in SMALL:
        if n not in CONV_SHARDED:
            W[n] = w[n]

    loss_share, dx, grads = _local_step(x[0], mem[0], positions[0], W, loss_target[0])
    loss = lax.psum(loss_share[0, 0], ("x", "y", "c"))

    g_big = _reduce_scatter(_pack([_to_shards(n, grads[n]) for n in BIG], F32, lead=(N_DEV,)), name="scatter_g")
    g_out = dict(zip(BIG, _unpack(g_big, big_shapes)))
    small_shapes = [grads[n].shape for n in SMALL]
    g_small = _all_reduce(_pack([grads[n] for n in SMALL], F32), name="reduce_g")
    for n, g in zip(SMALL, _unpack(g_small, small_shapes)):
        if n in CONV_SHARDED:
            cols = w[n].shape[-1]
            g = lax.dynamic_slice_in_dim(g, me * cols, cols, axis=2)
        g_out[n] = g

    delta, new_m, new_v = {}, {}, {}
    for n in BIG:
        delta[n], new_m[n], new_v[n] = _adamw_nd(w[n], g_out[n], m[n], v[n], "adamw_" + n)
    shapes = [w[n].shape for n in SMALL]
    packed = [_pack([d[n] for n in SMALL], F32) for d in (w, g_out, m, v)]
    for d, res in zip((delta, new_m, new_v), _adamw(*packed, name="adamw_small")):
        d.update(zip(SMALL, _unpack(res, shapes)))

    return (loss, dx[None], *[g_out[n] for n in WEIGHTS], *[delta[n] for n in WEIGHTS],
            *[new_m[n] for n in WEIGHTS], *[new_v[n] for n in WEIGHTS])
```

```python
import functools
import math

import jax
import jax.numpy as jnp
from jax import lax
from jax.experimental import pallas as pl
from jax.experimental.pallas import tpu as pltpu

F32, BF16 = jnp.float32, jnp.bfloat16

D = 1024
DEPTH = 4
N_MEM = 256
EPS = 1e-6
HEADS = 8
NOPE, ROPE, VDIM = 64, 32, 64
Q_RANK, KV_RANK = 256, 128
ROPE_THETA = 10000.0
SSM_CHUNK = 128
LRU_C = 8.0
MEM_HEADS = 4
D_FF = 4 * D
SLAB = 128
LR, B1, B2, AEPS, WD, STEP = 0.001, 0.9, 0.999, 1e-08, 0.01, 10

N_DEV = 8
MESH = pl.DeviceIdType.MESH

U_MLA = 640
U_GATE = 640
U_CONV = 768

_DN = {"nn": (((1,), (0,)), ((), ())), "nt": (((1,), (1,)), ((), ())), "tn": (((0,), (0,)), ((), ()))}


def _dot(a, b, kind):
    return lax.dot_general(a.astype(BF16), b.astype(BF16), _DN[kind], preferred_element_type=F32)


@functools.partial(jax.custom_vjp, nondiff_argnums=(2,))
def _bdot(a, b, kind):
    return _dot(a, b, kind)


def _bdot_fwd(a, b, kind):
    return _dot(a, b, kind), (a, b)


def _bdot_bwd(kind, res, g):
    a, b = res
    if kind == "nn":
        da, db = _dot(g, b, "nt"), _dot(a, g, "tn")
    elif kind == "nt":
        da, db = _dot(g, b, "nn"), _dot(g, a, "tn")
    else:
        da, db = _dot(b, g, "nt"), _dot(a, g, "nn")
    return da.astype(a.dtype), db.astype(b.dtype)


_bdot.defvjp(_bdot_fwd, _bdot_bwd)


def _tile(n, pref):
    if n <= pref:
        return n
    t = pref
    while n % t:
        t -= SLAB
    return t


def _cparams(sem, vmem_mb=48):
    return pltpu.CompilerParams(dimension_semantics=sem, vmem_limit_bytes=vmem_mb * 1024 * 1024)


def _mm(a, b, kind="nn", *, name, out_dtypes=(F32,), epi=None, extras=(), tm=1024, tn=1024, tk=1024):
    if kind == "tn":
        K, M = a.shape
    else:
        M, K = a.shape
    N = b.shape[0] if kind == "nt" else b.shape[1]
    if a.dtype == F32 or b.dtype == F32:
        tk = tk // 2
    tm, tn, tk = _tile(M, tm), _tile(N, tn), _tile(K, tk)
    nk = K // tk
    a_spec = pl.BlockSpec((tk, tm), lambda i, j, k: (k, i)) if kind == "tn" else pl.BlockSpec((tm, tk), lambda i, j, k: (i, k))
    b_spec = pl.BlockSpec((tn, tk), lambda i, j, k: (j, k)) if kind == "nt" else pl.BlockSpec((tk, tn), lambda i, j, k: (k, j))
    o_spec = pl.BlockSpec((tm, tn), lambda i, j, k: (i, j))
    n_ex, n_out = len(extras), len(out_dtypes)

    def body(*refs):
        a_ref, b_ref = refs[:2]
        ex = refs[2:2 + n_ex]
        outs = refs[2 + n_ex:2 + n_ex + n_out]
        acc = refs[-1]
        k = pl.program_id(2)

        def finish(r):
            res = epi(r, *[e[...] for e in ex]) if epi is not None else (r,)
            for o, v in zip(outs, res):
                o[...] = v.astype(o.dtype)

        if nk == 1:
            finish(_dot(a_ref[...], b_ref[...], kind))
            return

        @pl.when(k == 0)
        def _():
            acc[...] = _dot(a_ref[...], b_ref[...], kind)

        @pl.when(k > 0)
        def _():
            acc[...] += _dot(a_ref[...], b_ref[...], kind)

        @pl.when(k == nk - 1)
        def _():
            finish(acc[...])

    res = pl.pallas_call(
        body,
        name=name,
        grid=(M // tm, N // tn, nk),
        in_specs=[a_spec, b_spec] + [o_spec] * n_ex,
        out_specs=[o_spec] * n_out,
        out_shape=[jax.ShapeDtypeStruct((M, N), dt) for dt in out_dtypes],
        scratch_shapes=[pltpu.VMEM((tm, tn), F32)],
        compiler_params=_cparams(("parallel", "parallel", "arbitrary")),
    )(a, b, *extras)
    return res[0] if n_out == 1 else res


def _row_spec(arr, tm):
    return pl.BlockSpec((tm, arr.shape[1]), lambda i: (i, 0))


def _full_spec(arr):
    nd = arr.ndim
    return pl.BlockSpec(arr.shape, lambda i: (0,) * nd)


def _rows_fwd(fn, rows, params, outs, *, name, tm=256):
    T = rows[0].shape[0]
    tm = min(tm, T)
    nr, npar = len(rows), len(params)

    def body(*refs):
        ins = [r[...] for r in refs[:nr + npar]]
        res = fn(*ins)
        for o, v in zip(refs[nr + npar:], res):
            o[...] = v.astype(o.dtype)

    res = pl.pallas_call(
        body,
        name=name,
        grid=(T // tm,),
        in_specs=[_row_spec(r, tm) for r in rows] + [_full_spec(p) for p in params],
        out_specs=[pl.BlockSpec((tm, c), lambda i: (i, 0)) for c, _ in outs],
        out_shape=[jax.ShapeDtypeStruct((T, c), dt) for c, dt in outs],
        compiler_params=_cparams(("parallel",)),
    )(*rows, *params)
    return res


def _rows_vjp(fn, rows, params, cts, *, name, drows, dparams, drow_dtypes, add=None, tm=256):
    T = rows[0].shape[0]
    tm = min(tm, T)
    nr, npar, nct = len(rows), len(params), len(cts)
    n_add = 0 if add is None else 1
    n_dr, n_dp = len(drows), len(dparams)

    def body(*refs):
        row_t = [r[...] for r in refs[:nr]]
        par_t = [r[...] for r in refs[nr:nr + npar]]
        ct_t = [r[...].astype(F32) for r in refs[nr + npar:nr + npar + nct]]
        pos = nr + npar + nct
        add_t = refs[pos][...] if n_add else None
        pos += n_add
        drow_refs = refs[pos:pos + n_dr]
        dpar_refs = refs[pos + n_dr:pos + n_dr + n_dp]

        def g(*dargs):
            rr, pp = list(row_t), list(par_t)
            for idx, v in zip(drows, dargs[:n_dr]):
                rr[idx] = v
            for idx, v in zip(dparams, dargs[n_dr:]):
                pp[idx] = v
            return tuple(fn(*rr, *pp))

        prim = [row_t[i].astype(F32) for i in drows] + [par_t[i].astype(F32) for i in dparams]
        _, vjp = jax.vjp(g, *prim)
        grads = vjp(tuple(ct_t))
        for n, (o, v) in enumerate(zip(drow_refs, grads[:n_dr])):
            if n == 0 and n_add:
                v = v + add_t.astype(F32)
            o[...] = v.astype(o.dtype)

        @pl.when(pl.program_id(0) == 0)
        def _():
            for o in dpar_refs:
                o[...] = jnp.zeros_like(o)

        for o, v in zip(dpar_refs, grads[n_dr:]):
            o[...] += v

    res = pl.pallas_call(
        body,
        name=name,
        grid=(T // tm,),
        in_specs=[_row_spec(r, tm) for r in rows] + [_full_spec(p) for p in params] + [_row_spec(c, tm) for c in cts]
        + ([_row_spec(add, tm)] if n_add else []),
        out_specs=[_row_spec(rows[i], tm) for i in drows] + [_full_spec(params[i]) for i in dparams],
        out_shape=[jax.ShapeDtypeStruct(rows[i].shape, dt) for i, dt in zip(drows, drow_dtypes)]
        + [jax.ShapeDtypeStruct(params[i].shape, F32) for i in dparams],
        compiler_params=_cparams(("arbitrary",)),
    )(*rows, *params, *cts, *([add] if n_add else []))
    return list(res[:n_dr]), list(res[n_dr:])


def _rms(x, g, n):
    return x * lax.rsqrt(jnp.sum(x * x, axis=-1, keepdims=True) * (1.0 / n) + EPS) * g


def _sigmoid(x):
    return 1.0 / (1.0 + jnp.exp(-x))


def _silu(x):
    return x * _sigmoid(x)


def _softplus(x):
    return jnp.maximum(x, 0.0) + jnp.log(1.0 + jnp.exp(-jnp.abs(x)))


def _gelu_tanh(x):
    return 0.5 * x * (1.0 + jnp.tanh(math.sqrt(2.0 / math.pi) * (x + 0.044715 * x * x * x)))


def _lane(shape):
    return lax.broadcasted_iota(jnp.int32, shape, len(shape) - 1)


def _col(x, h):
    return jnp.sum(jnp.where(_lane(x.shape) == h, x, 0.0), axis=-1, keepdims=True)


def _f_norm(x, g):
    return (_rms(x.astype(F32), g, x.shape[-1]),)


def _f_mla_prep(u, ck, sk, gq, gkv):
    u = u.astype(F32)
    cq = _rms(u[:, 0:256], gq, Q_RANK)
    ckv = _rms(u[:, 256:384], gkv, KV_RANK)
    kr = u[:, 384:512] * ck + u[:, 512:640] * sk
    return cq, jnp.concatenate([ckv, kr], axis=1)


def _f_qrope(y, cq, sq):
    y = y.astype(F32)
    c8, s8 = jnp.tile(cq, (1, HEADS)), jnp.tile(sq, (1, HEADS))
    return (y[:, :HEADS * SLAB] * c8 + y[:, HEADS * SLAB:] * s8,)


def _f_lru_gates(xc, wa, ba, wi, bi, lam):
    xc = xc.astype(F32)
    r = _sigmoid(_bdot(xc, wa, "nn") + ba)
    i = _sigmoid(_bdot(xc, wi, "nn") + bi)
    log_a = -LRU_C * r * _softplus(-lam)
    a = jnp.exp(log_a)
    x2 = 2.0 * log_a
    m1 = jnp.where(x2 > -0.02, -x2 * (1.0 + x2 * (0.5 + x2 * (1.0 / 6.0 + x2 * (1.0 / 24.0)))), 1.0 - jnp.exp(x2))
    return a, jnp.sqrt(m1) * (i * xc)


def _f_mix(o, ys, h, ug, g_mla, g_lru):
    o = o.astype(F32)
    y_mla = _rms(o, g_mla, HEADS * VDIM)
    y_lru = _rms(h.astype(F32) * _gelu_tanh(ug[:, 256:512].astype(F32)), g_lru, 256)
    return (jnp.concatenate([y_mla, ys.astype(F32), y_lru], axis=1),)


def _f_xattn(q, k, v):
    hd = D // MEM_HEADS
    outs = []
    for h in range(MEM_HEADS):
        sl = slice(h * hd, (h + 1) * hd)
        s = _bdot(q[:, sl], k[:, sl], "nt") * (1.0 / math.sqrt(hd))
        s = s - jnp.max(s, axis=-1, keepdims=True)
        p = jnp.exp(s)
        p = p / jnp.sum(p, axis=-1, keepdims=True)
        outs.append(_bdot(p, v[:, sl], "nn"))
    return (jnp.concatenate(outs, axis=1),)


def _split_dot(tri, a, kind):
    a_hi = a.astype(BF16)
    r1 = a - a_hi.astype(F32)
    a_mid = r1.astype(BF16)
    a_lo = (r1 - a_mid.astype(F32)).astype(BF16)
    return _dot(tri, a_hi, kind) + _dot(tri, a_mid, kind) + _dot(tri, a_lo, kind)


@jax.custom_vjp
def _tri_cumsum(tri, a):
    return _split_dot(tri, a, "nn")


def _tri_cumsum_fwd(tri, a):
    return _split_dot(tri, a, "nn"), tri


def _tri_cumsum_bwd(tri, g):
    return jnp.zeros_like(tri), _split_dot(tri, g, "tn")


_tri_cumsum.defvjp(_tri_cumsum_fwd, _tri_cumsum_bwd)


def _f_ssd_chunk(c, ug, s0, s1, dtb, alog, dsk, ng):
    L = c.shape[0]
    c = c.astype(F32)
    xbc = _silu(c)
    xs, bm, cm = xbc[:, 0:256], xbc[:, 256:384], xbc[:, 384:512]
    z = ug[:, 0:256].astype(F32)
    dt = _softplus(ug[:, 512:640].astype(F32) + dtb)
    a = dt * (-jnp.exp(alog))
    rowi = lax.broadcasted_iota(jnp.int32, (L, L), 0)
    coli = lax.broadcasted_iota(jnp.int32, (L, L), 1)
    tril = rowi >= coli
    acum = _tri_cumsum(tril.astype(BF16), a)
    acum_t = acum.T
    lane = _lane((1, SLAB))
    lo = lane < 64
    ys, new_s = [], []
    for g in range(2):
        gm = (lane >= 64 * g) & (lane < 64 * g + 64)
        bg, cg = jnp.where(gm, bm, 0.0), jnp.where(gm, cm, 0.0)
        cb = _bdot(cg, bg, "nt")
        x = xs[:, SLAB * g:SLAB * (g + 1)]
        h0, h1 = 2 * g, 2 * g + 1
        ac0, ac1 = _col(acum, h0), _col(acum, h1)
        xdt = x * jnp.where(lo, _col(dt, h0), _col(dt, h1))
        ac_l = jnp.where(lo, ac0, ac1)
        tot = acum[L - 1:L, :]
        tot_l = jnp.where(lo, _col(tot, h0), _col(tot, h1))
        yd = jnp.zeros((L, SLAB), F32)
        for hh, acc, hm in ((h0, ac0, lo), (h1, ac1, jnp.logical_not(lo))):
            seg = acc - acum_t[hh:hh + 1, :]
            lm = jnp.where(tril, jnp.exp(jnp.where(tril, seg, 0.0)), 0.0)
            yd = yd + _bdot(cb * lm, jnp.where(hm, xdt, 0.0), "nn")
        sg = (s0, s1)[g]
        y_off = _bdot(cg, sg, "nn") * jnp.exp(ac_l)
        st = _bdot(bg, xdt * jnp.exp(tot_l - ac_l), "tn")
        new_s.append(jnp.exp(tot_l) * sg + st)
        y = yd + y_off + jnp.where(lo, _col(dsk, h0), _col(dsk, h1)) * x
        y = y * _silu(z[:, SLAB * g:SLAB * (g + 1)])
        ys.append(_rms(y, ng[:, SLAB * g:SLAB * (g + 1)], SLAB))
    return jnp.concatenate(ys, axis=1), new_s[0], new_s[1]


_SSD_TILE = 512


def _ssd_fwd(c, ug, dtb, alog, dsk, ng, *, name):
    T = c.shape[0]
    tm = min(_SSD_TILE, T)
    ncs = tm // SSM_CHUNK
    nc = T // SSM_CHUNK

    def body(c_ref, ug_ref, dtb_ref, alog_ref, dsk_ref, ng_ref, y_ref, sall_ref, s_scr):
        @pl.when(pl.program_id(0) == 0)
        def _():
            s_scr[...] = jnp.zeros_like(s_scr)

        s0, s1 = s_scr[0], s_scr[1]
        for k in range(ncs):
            rows = slice(k * SSM_CHUNK, (k + 1) * SSM_CHUNK)
            sall_ref[k, 0] = s0
            sall_ref[k, 1] = s1
            y, s0, s1 = _f_ssd_chunk(c_ref[rows, :], ug_ref[rows, :], s0, s1, dtb_ref[...], alog_ref[...],
                                     dsk_ref[...], ng_ref[...])
            y_ref[rows, :] = y
        s_scr[0] = s0
        s_scr[1] = s1

    y, sall = pl.pallas_call(
        body,
        name=name,
        grid=(T // tm,),
        in_specs=[_row_spec(c, tm), _row_spec(ug, tm)] + [_full_spec(p) for p in (dtb, alog, dsk, ng)],
        out_specs=[pl.BlockSpec((tm, 256), lambda i: (i, 0)), pl.BlockSpec((ncs, 2, SLAB, SLAB), lambda i: (i, 0, 0, 0))],
        out_shape=[jax.ShapeDtypeStruct((T, 256), F32), jax.ShapeDtypeStruct((nc, 2, SLAB, SLAB), F32)],
        scratch_shapes=[pltpu.VMEM((2, SLAB, SLAB), F32)],
        compiler_params=_cparams(("arbitrary",)),
    )(c, ug, dtb, alog, dsk, ng)
    return y, sall


def _ssd_bwd(c, ug, sall, dy, dtb, alog, dsk, ng, *, name):
    T = c.shape[0]
    tm = min(_SSD_TILE, T)
    ncs = tm // SSM_CHUNK
    nt = T // tm

    def body(c_ref, ug_ref, sall_ref, dy_ref, dtb_ref, alog_ref, dsk_ref, ng_ref,
             dc_ref, dug_ref, ddtb_ref, dalog_ref, ddsk_ref, dng_ref, ds_scr):
        @pl.when(pl.program_id(0) == 0)
        def _():
            ds_scr[...] = jnp.zeros_like(ds_scr)
            for o in (ddtb_ref, dalog_ref, ddsk_ref, dng_ref):
                o[...] = jnp.zeros_like(o)

        ds0, ds1 = ds_scr[0], ds_scr[1]
        for k in reversed(range(ncs)):
            rows = slice(k * SSM_CHUNK, (k + 1) * SSM_CHUNK)
            prim = (c_ref[rows, :].astype(F32), ug_ref[rows, :].astype(F32), sall_ref[k, 0], sall_ref[k, 1],
                    dtb_ref[...], alog_ref[...], dsk_ref[...], ng_ref[...])
            _, vjp = jax.vjp(_f_ssd_chunk, *prim)
            dc, dug, ds0, ds1, g_dtb, g_alog, g_dsk, g_ng = vjp((dy_ref[rows, :].astype(F32), ds0, ds1))
            dc_ref[rows, :] = dc
            dug_ref[rows, :] = dug
            ddtb_ref[...] += g_dtb
            dalog_ref[...] += g_alog
            ddsk_ref[...] += g_dsk
            dng_ref[...] += g_ng
        ds_scr[0] = ds0
        ds_scr[1] = ds1

    rev = lambda i: (nt - 1 - i, 0)
    params = (dtb, alog, dsk, ng)
    res = pl.pallas_call(
        body,
        name=name,
        grid=(nt,),
        in_specs=[pl.BlockSpec((tm, c.shape[1]), rev), pl.BlockSpec((tm, ug.shape[1]), rev),
                  pl.BlockSpec((ncs, 2, SLAB, SLAB), lambda i: (nt - 1 - i, 0, 0, 0)), pl.BlockSpec((tm, 256), rev)]
        + [_full_spec(p) for p in params],
        out_specs=[pl.BlockSpec((tm, 512), rev), pl.BlockSpec((tm, U_GATE), rev)] + [_full_spec(p) for p in params],
        out_shape=[jax.ShapeDtypeStruct((T, 512), F32), jax.ShapeDtypeStruct((T, U_GATE), F32)]
        + [jax.ShapeDtypeStruct(p.shape, F32) for p in params],
        scratch_shapes=[pltpu.VMEM((2, SLAB, SLAB), F32)],
        compiler_params=_cparams(("arbitrary",)),
    )(c, ug, sall, dy, *params)
    return res


_CONV_TILE = 512
_HALO = 8
_CONV_W = 4


def _conv_fwd(u, w, b, *, name):
    T, C = u.shape
    tm = min(_CONV_TILE, T)
    hb = tm // _HALO

    def body(u_ref, prev_ref, w_ref, b_ref, y1_ref, y2_ref, ext):
        i = pl.program_id(0)
        ext[0:_HALO, :] = jnp.where(i > 0, prev_ref[...], 0.0)
        ext[_HALO:, :] = u_ref[...]
        y = jnp.broadcast_to(b_ref[...], (tm, C))
        for k in range(_CONV_W):
            y = y + ext[_HALO - (_CONV_W - 1) + k:_HALO - (_CONV_W - 1) + k + tm, :] * w_ref[k:k + 1, :]
        y1_ref[...] = y[:, 0:512]
        y2_ref[...] = y[:, 512:768]

    return pl.pallas_call(
        body,
        name=name,
        grid=(T // tm,),
        in_specs=[_row_spec(u, tm), pl.BlockSpec((_HALO, C), lambda i: (jnp.maximum(i * hb - 1, 0), 0)),
                  _full_spec(w), _full_spec(b)],
        out_specs=[pl.BlockSpec((tm, 512), lambda i: (i, 0)), pl.BlockSpec((tm, 256), lambda i: (i, 0))],
        out_shape=[jax.ShapeDtypeStruct((T, 512), F32), jax.ShapeDtypeStruct((T, 256), F32)],
        scratch_shapes=[pltpu.VMEM((tm + _HALO, C), F32)],
        compiler_params=_cparams(("parallel",)),
    )(u, u, w, b)


def _conv_bwd(u, dy1, dy2, w, *, name):
    T, C = u.shape
    tm = min(_CONV_TILE, T)
    hb = tm // _HALO
    nt = T // tm

    def body(u_ref, prev_ref, dy1_ref, next1_ref, dy2_ref, next2_ref, w_ref, du_ref, dw_ref, db_ref, ext, dext):
        i = pl.program_id(0)
        ext[0:_HALO, :] = jnp.where(i > 0, prev_ref[...], 0.0)
        ext[_HALO:, :] = u_ref[...]
        dext[0:tm, 0:512] = dy1_ref[...]
        dext[0:tm, 512:768] = dy2_ref[...]
        dext[tm:, 0:512] = jnp.where(i < nt - 1, next1_ref[...], 0.0)
        dext[tm:, 512:768] = jnp.where(i < nt - 1, next2_ref[...], 0.0)

        @pl.when(i == 0)
        def _():
            dw_ref[...] = jnp.zeros_like(dw_ref)
            db_ref[...] = jnp.zeros_like(db_ref)

        dy = dext[0:tm, :]
        du = jnp.zeros((tm, C), F32)
        for k in range(_CONV_W):
            du = du + dext[_CONV_W - 1 - k:_CONV_W - 1 - k + tm, :] * w_ref[k:k + 1, :]
            xk = ext[_HALO - (_CONV_W - 1) + k:_HALO - (_CONV_W - 1) + k + tm, :]
            dw_ref[k:k + 1, :] += jnp.sum(dy * xk, axis=0, keepdims=True)
        du_ref[...] = du
        db_ref[...] += jnp.sum(dy, axis=0, keepdims=True)

    nxt = lambda i: (jnp.minimum((i + 1) * hb, T // _HALO - 1), 0)
    return pl.pallas_call(
        body,
        name=name,
        grid=(nt,),
        in_specs=[_row_spec(u, tm), pl.BlockSpec((_HALO, C), lambda i: (jnp.maximum(i * hb - 1, 0), 0)),
                  _row_spec(dy1, tm), pl.BlockSpec((_HALO, 512), nxt), _row_spec(dy2, tm), pl.BlockSpec((_HALO, 256), nxt),
                  _full_spec(w)],
        out_specs=[pl.BlockSpec((tm, C), lambda i: (i, 0)), _full_spec(w), pl.BlockSpec((1, C), lambda i: (0, 0))],
        out_shape=[jax.ShapeDtypeStruct((T, C), F32), jax.ShapeDtypeStruct(w.shape, F32), jax.ShapeDtypeStruct((1, C), F32)],
        scratch_shapes=[pltpu.VMEM((tm + _HALO, C), F32), pltpu.VMEM((tm + _HALO, C), F32)],
        compiler_params=_cparams(("arbitrary",)),
    )(u, u, dy1, dy1, dy2, dy2, w)


_SCAN_TILE = 1024
_SUB = 8


def _shift_rows(x, d, fill, up):
    r = lax.broadcasted_iota(jnp.int32, x.shape, 0)
    if up:
        return jnp.where(r < _SUB - d, pltpu.roll(x, _SUB - d, 0), fill)
    return jnp.where(r >= d, pltpu.roll(x, d, 0), fill)


def _lru_scan_fwd(a, b, *, name):
    T, W = a.shape
    tr = min(_SCAN_TILE, T)

    def body(a_ref, b_ref, h_ref, hp_ref, carry):
        @pl.when(pl.program_id(0) == 0)
        def _():
            carry[...] = jnp.zeros_like(carry)

        def step(t, cr):
            rows = pl.ds(pl.multiple_of(t * _SUB, _SUB), _SUB)
            aa, bb = a_ref[rows, :], b_ref[rows, :]
            for d in (1, 2, 4):
                bb = bb + aa * _shift_rows(bb, d, 0.0, False)
                aa = aa * _shift_rows(aa, d, 1.0, False)
            h = bb + aa * cr
            h_ref[rows, :] = h
            r = lax.broadcasted_iota(jnp.int32, h.shape, 0)
            hp_ref[rows, :] = jnp.where(r >= 1, pltpu.roll(h, 1, 0), cr)
            return jnp.broadcast_to(h[_SUB - 1:_SUB, :], (_SUB, W))

        carry[...] = lax.fori_loop(0, tr // _SUB, step, carry[...])

    return pl.pallas_call(
        body,
        name=name,
        grid=(T // tr,),
        in_specs=[_row_spec(a, tr), _row_spec(b, tr)],
        out_specs=[pl.BlockSpec((tr, W), lambda i: (i, 0))] * 2,
        out_shape=[jax.ShapeDtypeStruct((T, W), F32)] * 2,
        scratch_shapes=[pltpu.VMEM((_SUB, W), F32)],
        compiler_params=_cparams(("arbitrary",)),
    )(a, b)


def _lru_scan_bwd(a, dh, hprev, *, name):
    T, W = a.shape
    tr = min(_SCAN_TILE, T)
    nt = T // tr

    def body(a_ref, dh_ref, hp_ref, g_ref, da_ref, carry):
        @pl.when(pl.program_id(0) == 0)
        def _():
            carry[...] = jnp.zeros_like(carry)

        nsub = tr // _SUB

        def step(s, cr):
            t = nsub - 1 - s
            rows = pl.ds(pl.multiple_of(t * _SUB, _SUB), _SUB)
            a_t = a_ref[rows, :]
            aa = _shift_rows(a_t, 1, 1.0, True)
            bb = dh_ref[rows, :]
            for d in (1, 2, 4):
                bb = bb + aa * _shift_rows(bb, d, 0.0, True)
                aa = aa * _shift_rows(aa, d, 1.0, True)
            g = bb + aa * cr
            g_ref[rows, :] = g
            da_ref[rows, :] = g * hp_ref[rows, :]
            return jnp.broadcast_to(a_t[0:1, :] * g[0:1, :], (_SUB, W))

        carry[...] = lax.fori_loop(0, nsub, step, carry[...])

    rev = lambda i: (nt - 1 - i, 0)
    return pl.pallas_call(
        body,
        name=name,
        grid=(nt,),
        in_specs=[pl.BlockSpec((tr, W), rev)] * 3,
        out_specs=[pl.BlockSpec((tr, W), rev)] * 2,
        out_shape=[jax.ShapeDtypeStruct((T, W), F32)] * 2,
        scratch_shapes=[pltpu.VMEM((_SUB, W), F32)],
        compiler_params=_cparams(("arbitrary",)),
    )(a, dh, hprev)


_ATT_BLK = 512
_ATT_SCALE = 1.0 / math.sqrt(NOPE + ROPE)


def _attn_fwd(q, kv, *, name):
    T = q.shape[0]
    blk = min(_ATT_BLK, T)
    nq = T // blk

    def body(q_ref, kv_ref, o_ref, lse_ref):
        i = pl.program_id(1)
        qb = q_ref[...]
        qpos = i * blk + lax.broadcasted_iota(jnp.int32, (blk, blk), 0)

        def step(j, carry, masked):
            m, l, acc = carry
            rows = pl.ds(pl.multiple_of(j * blk, blk), blk)
            kb, vb = kv_ref[rows, 0:SLAB], kv_ref[rows, SLAB:2 * SLAB]
            s = _dot(qb, kb, "nt") * _ATT_SCALE
            if masked:
                kpos = j * blk + lax.broadcasted_iota(jnp.int32, (blk, blk), 1)
                s = jnp.where(kpos <= qpos, s, -jnp.inf)
            m_new = jnp.maximum(m, jnp.max(s, axis=-1, keepdims=True))
            alpha = jnp.exp(m - m_new)
            p = jnp.exp(s - m_new)
            l = alpha * l + jnp.sum(p, axis=-1, keepdims=True)
            acc = alpha * acc + _dot(p, vb, "nn")
            return m_new, l, acc

        init = (jnp.full((blk, 1), -jnp.inf, F32), jnp.zeros((blk, 1), F32), jnp.zeros((blk, SLAB), F32))
        carry = lax.fori_loop(0, i, functools.partial(step, masked=False), init)
        m, l, acc = step(i, carry, True)
        o_ref[...] = acc / l
        lse = jnp.broadcast_to(m + jnp.log(l), (blk, SLAB))
        lse_ref[...] = lse.T[0:_SUB, :]

    return pl.pallas_call(
        body,
        name=name,
        grid=(HEADS, nq),
        in_specs=[pl.BlockSpec((blk, SLAB), lambda h, i: (i, h)), pl.BlockSpec((T, 2 * SLAB), lambda h, i: (0, h))],
        out_specs=[pl.BlockSpec((blk, SLAB), lambda h, i: (i, h)),
                   pl.BlockSpec((None, None, _SUB, blk), lambda h, i: (h, i, 0, 0))],
        out_shape=[jax.ShapeDtypeStruct((T, HEADS * SLAB), F32), jax.ShapeDtypeStruct((HEADS, nq, _SUB, blk), F32)],
        compiler_params=_cparams(("parallel", "arbitrary")),
    )(q, kv)


def _attn_delta(do, o, *, name):
    T = o.shape[0]
    blk = min(_ATT_BLK, T)
    nq = T // blk

    def body(do_ref, o_ref, d_ref):
        dl = jnp.sum(do_ref[...].astype(F32) * o_ref[...], axis=-1, keepdims=True)
        d_ref[...] = jnp.broadcast_to(dl, (blk, SLAB)).T[0:_SUB, :]

    return pl.pallas_call(
        body,
        name=name,
        grid=(HEADS, nq),
        in_specs=[pl.BlockSpec((blk, SLAB), lambda h, i: (i, h))] * 2,
        out_specs=pl.BlockSpec((None, None, _SUB, blk), lambda h, i: (h, i, 0, 0)),
        out_shape=jax.ShapeDtypeStruct((HEADS, nq, _SUB, blk), F32),
        compiler_params=_cparams(("parallel", "parallel")),
    )(do, o)


def _attn_bwd(q, kv, do, lse, delta, *, name):
    T = q.shape[0]
    blk = min(_ATT_BLK, T)
    nq = T // blk

    def body(q_ref, kv_ref, do_ref, lse_ref, dl_ref, dqt_ref, dkv_ref):
        j = pl.program_id(1)

        @pl.when(j == 0)
        def _():
            dqt_ref[...] = jnp.zeros_like(dqt_ref)

        kb, vb = kv_ref[:, 0:SLAB], kv_ref[:, SLAB:2 * SLAB]
        kbt = kb.astype(F32).T.astype(BF16)
        kpos = j * blk + lax.broadcasted_iota(jnp.int32, (blk, blk), 0)

        def step(i, carry, masked):
            dk, dv = carry
            rows = pl.ds(pl.multiple_of(i * blk, blk), blk)
            qb, dob = q_ref[rows, :], do_ref[rows, :]
            st = _dot(kb, qb, "nt") * _ATT_SCALE
            if masked:
                qpos = i * blk + lax.broadcasted_iota(jnp.int32, (blk, blk), 1)
                st = jnp.where(kpos <= qpos, st, -jnp.inf)
            pt = jnp.exp(st - lse_ref[i, 0:1, :])
            dpt = _dot(vb, dob, "nt")
            dst = pt * (dpt - dl_ref[i, 0:1, :]) * _ATT_SCALE
            dv = dv + _dot(pt, dob, "nn")
            dk = dk + _dot(dst, qb, "nn")
            dqt_ref[i] += _dot(kbt, dst, "nn")
            return dk, dv

        zero = jnp.zeros((blk, SLAB), F32)
        carry = step(j, (zero, zero), True)
        dk, dv = lax.fori_loop(j + 1, nq, functools.partial(step, masked=False), carry)
        dkv_ref[:, 0:SLAB] = dk.astype(BF16)
        dkv_ref[:, SLAB:2 * SLAB] = dv.astype(BF16)

    stat_spec = pl.BlockSpec((None, nq, _SUB, blk), lambda h, j: (h, 0, 0, 0))
    return pl.pallas_call(
        body,
        name=name,
        grid=(HEADS, nq),
        in_specs=[pl.BlockSpec((T, SLAB), lambda h, j: (0, h)), pl.BlockSpec((blk, 2 * SLAB), lambda h, j: (j, h)),
                  pl.BlockSpec((T, SLAB), lambda h, j: (0, h)), stat_spec, stat_spec],
        out_specs=[pl.BlockSpec((None, nq, SLAB, blk), lambda h, j: (h, 0, 0, 0)),
                   pl.BlockSpec((blk, 2 * SLAB), lambda h, j: (j, h))],
        out_shape=[jax.ShapeDtypeStruct((HEADS, nq, SLAB, blk), F32), jax.ShapeDtypeStruct((T, HEADS * 2 * SLAB), BF16)],
        compiler_params=_cparams(("parallel", "arbitrary")),
    )(q, kv, do, lse, delta)


def _qrope_bwd(dqt, cq, sq, *, name):
    _, nq, _, blk = dqt.shape
    T = nq * blk

    def body(dqt_ref, c_ref, s_ref, dy_ref):
        c, s = c_ref[...], s_ref[...]
        for h in range(HEADS):
            dq = dqt_ref[h].T
            dy_ref[:, h * SLAB:(h + 1) * SLAB] = (dq * c).astype(BF16)
            dy_ref[:, (HEADS + h) * SLAB:(HEADS + h + 1) * SLAB] = (dq * s).astype(BF16)

    return pl.pallas_call(
        body,
        name=name,
        grid=(nq,),
        in_specs=[pl.BlockSpec((HEADS, None, SLAB, blk), lambda i: (0, i, 0, 0)), _row_spec(cq, blk), _row_spec(sq, blk)],
        out_specs=pl.BlockSpec((blk, 2 * HEADS * SLAB), lambda i: (i, 0)),
        out_shape=jax.ShapeDtypeStruct((T, 2 * HEADS * SLAB), BF16),
        compiler_params=_cparams(("parallel",)),
    )(dqt, cq, sq)


def _loss_head(x, tgt, g, *, name, tm=256):
    T = x.shape[0]
    tm = min(tm, T)

    def body(x_ref, t_ref, g_ref, loss_ref, dx_ref, dg_ref):
        def f(xv, gv):
            e = _rms(xv, gv, D) - t_ref[...]
            row = jnp.sum(e * e, axis=1, keepdims=True)
            return jnp.sum(row, axis=0, keepdims=True) * (0.5 / D)

        val, vjp = jax.vjp(f, x_ref[...], g_ref[...])
        dxv, dgv = vjp(jnp.ones((1, 1), F32))

        @pl.when(pl.program_id(0) == 0)
        def _():
            loss_ref[...] = jnp.zeros_like(loss_ref)
            dg_ref[...] = jnp.zeros_like(dg_ref)

        dx_ref[...] = dxv
        dg_ref[...] += dgv
        loss_ref[...] += jnp.broadcast_to(val, loss_ref.shape)

    return pl.pallas_call(
        body,
        name=name,
        grid=(T // tm,),
        in_specs=[_row_spec(x, tm), _row_spec(tgt, tm), _full_spec(g)],
        out_specs=[pl.BlockSpec((1, SLAB), lambda i: (0, 0)), _row_spec(x, tm), _full_spec(g)],
        out_shape=[jax.ShapeDtypeStruct((1, SLAB), F32), jax.ShapeDtypeStruct(x.shape, F32), jax.ShapeDtypeStruct(g.shape, F32)],
        compiler_params=_cparams(("arbitrary",)),
    )(x, tgt, g)


def _row_tile(rows, cols, budget=256 * 1024):
    best = None
    for t in range(16, rows + 1, 16):
        if rows % t == 0 and t * cols <= budget:
            best = t
    return best or rows


def _sum_fixed(x, out_dtype, *, name):
    n, R, C = x.shape
    tr = _row_tile(R, C)

    def body(x_ref, o_ref):
        acc = x_ref[0].astype(F32)
        for k in range(1, n):
            acc = acc + x_ref[k].astype(F32)
        o_ref[...] = acc.astype(o_ref.dtype)

    return pl.pallas_call(
        body,
        name=name,
        grid=(R // tr,),
        in_specs=[pl.BlockSpec((n, tr, C), lambda i: (0, i, 0))],
        out_specs=pl.BlockSpec((tr, C), lambda i: (i, 0)),
        out_shape=jax.ShapeDtypeStruct((R, C), out_dtype),
        compiler_params=_cparams(("parallel",)),
    )(x)


def _adamw(w, g, m, v, *, name):
    R, C = w.shape
    tr = _row_tile(R, C, 128 * 1024)

    def body(w_ref, g_ref, m_ref, v_ref, d_ref, nm_ref, nv_ref):
        gv = g_ref[...]
        mv = B1 * m_ref[...] + (1.0 - B1) * gv
        vv = B2 * v_ref[...] + (1.0 - B2) * (gv * gv)
        m_hat = mv / (1.0 - B1 ** STEP)
        v_hat = vv / (1.0 - B2 ** STEP)
        d_ref[...] = -LR * (m_hat / (jnp.sqrt(v_hat) + AEPS) + WD * w_ref[...])
        nm_ref[...] = mv
        nv_ref[...] = vv

    spec = pl.BlockSpec((tr, C), lambda i: (i, 0))
    return pl.pallas_call(
        body, name=name, grid=(R // tr,), in_specs=[spec] * 4, out_specs=[spec] * 3,
        out_shape=[jax.ShapeDtypeStruct((R, C), F32)] * 3, compiler_params=_cparams(("parallel",)),
    )(w, g, m, v)


_FLIPS = ((1, 0), (0, 1), (1, 1))
_ANY = pl.BlockSpec(memory_space=pl.ANY)


def _me():
    return lax.axis_index("x"), lax.axis_index("y"), lax.axis_index("c")


def _flip(mx, my, f):
    return (1 - mx if f[0] else mx), (1 - my if f[1] else my)


def _chip_exchange(x, gather, *, name):
    shape = x.shape if not gather else (4,) + x.shape

    def body(x_ref, out_ref, send_sems, recv_sems, local_sem):
        mx, my, mc = _me()
        mine = 2 * mx + my
        src_own = x_ref if gather else x_ref.at[mine]
        local = pltpu.make_async_copy(src_own, out_ref.at[mine], local_sem)
        local.start()
        sends = []
        for k, f in enumerate(_FLIPS):
            px, py = _flip(mx, my, f)
            src = x_ref if gather else x_ref.at[2 * px + py]
            cp = pltpu.make_async_remote_copy(src_ref=src, dst_ref=out_ref.at[mine], send_sem=send_sems.at[k],
                                              recv_sem=recv_sems.at[k], device_id=(px, py, mc), device_id_type=MESH)
            cp.start()
            sends.append(cp)
        for k, f in enumerate(_FLIPS):
            px, py = _flip(mx, my, f)
            pltpu.make_async_remote_copy(src_ref=src_own, dst_ref=out_ref.at[2 * px + py], send_sem=send_sems.at[k],
                                         recv_sem=recv_sems.at[k], device_id=(px, py, mc), device_id_type=MESH).wait_recv()
        for cp in sends:
            cp.wait_send()
        local.wait()

    return pl.pallas_call(
        body, name=name, in_specs=[_ANY], out_specs=_ANY, out_shape=jax.ShapeDtypeStruct(shape, x.dtype),
        scratch_shapes=[pltpu.SemaphoreType.DMA((3,)), pltpu.SemaphoreType.DMA((3,)), pltpu.SemaphoreType.DMA],
    )(x)


def _core_exchange(x, gather, *, name):
    shape = x.shape[1:] if not gather else (2,) + x.shape

    def body(x_ref, out_ref, send_sem, recv_sem, local_sem):
        mx, my, mc = _me()
        if gather:
            src, dst_there, dst_here = x_ref, out_ref.at[mc], out_ref.at[1 - mc]
            local = pltpu.make_async_copy(x_ref, out_ref.at[mc], local_sem)
            local.start()
        else:
            src, dst_there, dst_here = x_ref.at[1 - mc], out_ref, out_ref
        cp = pltpu.make_async_remote_copy(src_ref=src, dst_ref=dst_there, send_sem=send_sem, recv_sem=recv_sem,
                                          device_id=(mx, my, 1 - mc), device_id_type=MESH)
        cp.start()
        pltpu.make_async_remote_copy(src_ref=src, dst_ref=dst_here, send_sem=send_sem, recv_sem=recv_sem,
                                     device_id=(mx, my, 1 - mc), device_id_type=MESH).wait_recv()
        cp.wait_send()
        if gather:
            local.wait()

    return pl.pallas_call(
        body, name=name, in_specs=[_ANY], out_specs=_ANY, out_shape=jax.ShapeDtypeStruct(shape, x.dtype),
        scratch_shapes=[pltpu.SemaphoreType.DMA, pltpu.SemaphoreType.DMA, pltpu.SemaphoreType.DMA],
    )(x)


def _all_gather(x, *, name):
    return _core_exchange(_chip_exchange(x, True, name=name + "_chips"), True, name=name + "_cores")


def _add_own_half(x, got, *, name):
    _, R, C = x.shape
    tr = _row_tile(R, C)

    def body(c_ref, x_ref, g_ref, o_ref):
        o_ref[...] = (x_ref[...] + g_ref[...]).astype(o_ref.dtype)

    return pl.pallas_call(
        body,
        name=name,
        grid_spec=pltpu.PrefetchScalarGridSpec(
            num_scalar_prefetch=1, grid=(R // tr,),
            in_specs=[pl.BlockSpec((None, tr, C), lambda i, c: (c[0], i, 0)), pl.BlockSpec((tr, C), lambda i, c: (i, 0))],
            out_specs=pl.BlockSpec((tr, C), lambda i, c: (i, 0))),
        out_shape=jax.ShapeDtypeStruct((R, C), BF16),
        compiler_params=_cparams(("parallel",)),
    )(lax.axis_index("c").astype(jnp.int32).reshape(1), x, got)


def _reduce_scatter(x, *, name):
    _, _, R, C = x.shape
    got = _core_exchange(x, False, name=name + "_cores")
    t = _add_own_half(x.reshape(2, 4 * R, C), got.reshape(4 * R, C), name=name + "_add").reshape(4, R, C)
    parts = _chip_exchange(t, False, name=name + "_chips")
    return _sum_fixed(parts, F32, name=name + "_sum")


def _all_reduce(x, *, name):
    g = _all_gather(x, name=name)
    return _sum_fixed(g.reshape((N_DEV,) + x.shape), F32, name=name + "_sum")


WEIGHTS = ['mix_norm_g', 'w_in', 'mla_q_norm_g', 'mla_kv_norm_g', 'mla_w_uq', 'mla_w_ukv', 'mla_out_g', 'ssm_conv_w',
           'ssm_conv_b', 'ssm_dt_bias', 'ssm_a_log', 'ssm_d', 'ssm_norm_g', 'lru_conv_w', 'lru_conv_b', 'lru_w_a',
           'lru_b_a', 'lru_w_i', 'lru_b_i', 'lru_lambda', 'lru_out_g', 'w_out', 'xattn_norm_g', 'mem_norm_g', 'w_mq',
           'w_mk', 'w_mv', 'w_mo', 'mlp_norm_g', 'w_mlp1', 'w_mlp2', 'final_norm_g']
ROW_SHARDED = ('w_in', 'w_out', 'w_mq', 'w_mk', 'w_mv', 'w_mo', 'w_mlp2')
COL_SHARDED = ('mla_w_uq', 'mla_w_ukv', 'w_mlp1')
BIG = tuple(n for n in WEIGHTS if n in ROW_SHARDED + COL_SHARDED)
CONV_SHARDED = ('ssm_conv_w', 'lru_conv_w')
SMALL = tuple(n for n in WEIGHTS if n not in BIG)
PACK_C = 1024


def _pack(arrs, dtype, lead=()):
    flat = jnp.concatenate([a.reshape(lead + (-1,)).astype(dtype) for a in arrs], axis=-1)
    n = flat.shape[-1]
    rows = -(-n // (16 * PACK_C)) * 16
    flat = jnp.pad(flat, [(0, 0)] * len(lead) + [(0, rows * PACK_C - n)])
    return flat.reshape(lead + (rows, PACK_C))


def _unpack(packed, shapes, lead=()):
    flat = packed.reshape(lead + (-1,))
    out, off = [], 0
    for s in shapes:
        n = math.prod(s)
        out.append(flat[..., off:off + n].reshape(lead + tuple(s)))
        off += n
    return out


def _pad_lanes(v, n=SLAB):
    return jnp.pad(v.astype(F32), (0, n - v.shape[0])).reshape(1, n)


PIECES = ('win', 'wq', 'wkv', 'wout', 'w_mq', 'w_mk', 'w_mv', 'w_mo', 'w_mlp1', 'w_mlp2')
PIECE_SOURCE = {'win': 'w_in', 'wq': 'mla_w_uq', 'wkv': 'mla_w_ukv', 'wout': 'w_out'}
PIECE_SHAPE = {'win': (128, 2048), 'wq': (Q_RANK, 2 * SLAB), 'wkv': (KV_RANK, 2 * SLAB), 'wout': (128, D),
               'w_mq': (128, D), 'w_mk': (128, D), 'w_mv': (128, D), 'w_mo': (128, D), 'w_mlp1': (D, 512),
               'w_mlp2': (512, D)}
PIECE_COLS = ('wq', 'wkv', 'w_mlp1')


def _k_win(w):
    kr = w[..., 384:416]
    zc = lambda k: jnp.zeros(w.shape[:-1] + (k,), w.dtype)
    return jnp.concatenate(
        [w[..., 0:384], kr, zc(96), kr[..., 16:32], kr[..., 0:16], zc(96),
         w[..., 416:672], w[..., 1444:1700], w[..., 1184:1188], zc(124),
         w[..., 672:1184], w[..., 1188:1444]], axis=-1)


def _k_win_inv(m):
    gk = m[..., 384:416] + jnp.concatenate([m[..., 528:544], m[..., 512:528]], axis=-1)
    return jnp.concatenate([m[..., 0:384], gk, m[..., 640:896], m[..., 1280:1792], m[..., 1152:1156], m[..., 1792:2048],
                            m[..., 896:1152]], axis=-1)


def _k_wq(w):
    nh = w.shape[-1] // (NOPE + ROPE)
    w = w.reshape(w.shape[:-1] + (nh, NOPE + ROPE))
    nope, r1, r2 = w[..., :NOPE], w[..., NOPE:NOPE + 16], w[..., NOPE + 16:]
    z = lambda k: jnp.zeros(w.shape[:-1] + (k,), w.dtype)
    both = jnp.stack([jnp.concatenate([nope, r1, r2, z(32)], -1), jnp.concatenate([z(64), r2, r1, z(32)], -1)], axis=-3)
    return both.reshape(w.shape[:-2] + (2 * nh * SLAB,))


def _k_wq_inv(m):
    nh = m.shape[-1] // (2 * SLAB)
    m = m.reshape(m.shape[:-1] + (2, nh, SLAB))
    q0, q1 = m[..., 0, :, :], m[..., 1, :, :]
    w = jnp.concatenate([q0[..., :64], q0[..., 64:80] + q1[..., 80:96], q0[..., 80:96] + q1[..., 64:80]], -1)
    return w.reshape(w.shape[:-2] + (nh * (NOPE + ROPE),))


def _k_wkv(w):
    nh = w.shape[-1] // (NOPE + VDIM)
    w = w.reshape(w.shape[:-1] + (nh, NOPE + VDIM))
    z = jnp.zeros(w.shape[:-1] + (64,), w.dtype)
    return jnp.concatenate([w[..., :NOPE], z, w[..., NOPE:], z], -1).reshape(w.shape[:-2] + (nh * 2 * SLAB,))


def _k_wkv_inv(m):
    nh = m.shape[-1] // (2 * SLAB)
    m = m.reshape(m.shape[:-1] + (nh, 2 * SLAB))
    return jnp.concatenate([m[..., :NOPE], m[..., SLAB:SLAB + VDIM]], -1).reshape(m.shape[:-2] + (nh * (NOPE + VDIM),))


_K_FWD = {'win': _k_win, 'wq': _k_wq, 'wkv': _k_wkv}
_K_INV = {'win': _k_win_inv, 'wq': _k_wq_inv, 'wkv': _k_wkv_inv}


def _assemble(piece, g):
    _, _, a, b = g.shape
    if piece == 'wq':
        return g.reshape(2, 4, a, 2, SLAB).transpose(2, 3, 1, 0, 4).reshape(a, N_DEV * b)
    if piece in PIECE_COLS:
        return g.transpose(2, 1, 0, 3).reshape(a, N_DEV * b)
    return g.transpose(1, 0, 2, 3).reshape(N_DEV * a, b)


def _disassemble(piece, full):
    a, b = PIECE_SHAPE[piece]
    if piece == 'wq':
        return full.reshape(a, 2, 4, 2, SLAB).transpose(3, 2, 0, 1, 4).reshape(2, 4, a, b)
    if piece in PIECE_COLS:
        return full.reshape(a, 4, 2, b).transpose(2, 1, 0, 3)
    return full.reshape(4, 2, a, b).transpose(1, 0, 2, 3)


def _piece_rows(piece):
    a, b = PIECE_SHAPE[piece]
    return a * b // PACK_C


def _prep_layer(Wk, Ws, l):
    P = {n: Wk[n][l] for n in PIECES}
    ri, ci = jnp.arange(SLAB)[:, None], jnp.arange(2 * SLAB)[None, :]
    sel = ((ri < ROPE) & (ci == ri + NOPE)).astype(P['wkv'].dtype)
    P['wkv'] = jnp.concatenate([P['wkv'], jnp.tile(sel, (1, HEADS))], axis=0)
    wout = P['wout']
    mla_rows = jnp.pad(wout[:HEADS * VDIM].reshape(HEADS, VDIM, D), ((0, 0), (0, SLAB - VDIM), (0, 0)))
    P['wout'] = jnp.concatenate([mla_rows.reshape(HEADS * SLAB, D), wout[HEADS * VDIM:]], axis=0)
    W = Ws
    row = lambda n: W[n][l].astype(F32).reshape(1, -1)
    for n in ('mix_norm_g', 'mla_q_norm_g', 'mla_kv_norm_g', 'ssm_norm_g', 'lru_lambda', 'lru_out_g', 'xattn_norm_g',
              'mem_norm_g', 'mlp_norm_g'):
        P[n] = row(n)
    P['mla_out_g'] = jnp.pad(W['mla_out_g'][l].astype(F32).reshape(HEADS, VDIM), ((0, 0), (0, SLAB - VDIM))).reshape(1, -1)
    for n in ('ssm_dt_bias', 'ssm_a_log', 'ssm_d'):
        P[n] = _pad_lanes(W[n][l])
    P['conv_w'] = jnp.pad(jnp.concatenate([W['ssm_conv_w'][l], W['lru_conv_w'][l]], axis=1).astype(F32), ((0, 4), (0, 0)))
    P['conv_b'] = jnp.concatenate([W['ssm_conv_b'][l], W['lru_conv_b'][l]]).astype(F32).reshape(1, -1)
    for n in ('lru_w_a', 'lru_w_i'):
        P[n] = jnp.concatenate([jnp.pad(W[n][l, k].astype(F32), ((0, 0), (64 * k, 192 - 64 * k))) for k in range(4)], axis=0)
    for n in ('lru_b_a', 'lru_b_i'):
        P[n] = W[n][l].astype(F32).reshape(1, -1)
    return P


def _unprep_pieces(G):
    o = {n: G[n] for n in PIECES}
    o['wkv'] = G['wkv'][:KV_RANK]
    wo = G['wout']
    o['wout'] = jnp.concatenate([wo[:HEADS * SLAB].reshape(HEADS, SLAB, D)[:, :VDIM].reshape(HEADS * VDIM, D),
                                 wo[HEADS * SLAB:]], axis=0)
    return o


def _unprep_small(G):
    o = {}
    for n in ('mix_norm_g', 'mla_q_norm_g', 'mla_kv_norm_g', 'ssm_norm_g', 'lru_lambda', 'lru_out_g', 'xattn_norm_g',
              'mem_norm_g', 'mlp_norm_g', 'lru_b_a', 'lru_b_i'):
        o[n] = G[n].reshape(-1)
    o['lru_b_a'] = o['lru_b_a'].reshape(4, 64)
    o['lru_b_i'] = o['lru_b_i'].reshape(4, 64)
    o['mla_out_g'] = G['mla_out_g'].reshape(HEADS, SLAB)[:, :VDIM].reshape(-1)
    for n in ('ssm_dt_bias', 'ssm_a_log', 'ssm_d'):
        o[n] = G[n][0, :4]
    o['ssm_conv_w'], o['lru_conv_w'] = G['conv_w'][:4, :512], G['conv_w'][:4, 512:]
    o['ssm_conv_b'], o['lru_conv_b'] = G['conv_b'][0, :512], G['conv_b'][0, 512:]
    for n in ('lru_w_a', 'lru_w_i'):
        o[n] = jnp.stack([G[n][64 * k:64 * (k + 1), 64 * k:64 * (k + 1)] for k in range(4)])
    return o


def _rope_tables(positions):
    half = ROPE // 2
    inv_freq = ROPE_THETA ** (-jnp.arange(half, dtype=F32) * 2.0 / ROPE)
    ang = positions.astype(F32)[:, None] * inv_freq
    cos, sin = jnp.cos(ang), jnp.sin(ang)
    T = positions.shape[0]
    z = lambda k: jnp.zeros((T, k), F32)
    ck = jnp.concatenate([cos, cos, z(96)], axis=1)
    sk = jnp.concatenate([-sin, sin, z(96)], axis=1)
    cq = jnp.concatenate([jnp.ones((T, NOPE), F32), cos, cos, z(32)], axis=1)
    sq = jnp.concatenate([z(NOPE), -sin, sin, z(32)], axis=1)
    return ck, sk, cq, sq


def _add_epi(acc, res):
    return (acc + res,)


def _relu2_epi(acc):
    r = jnp.maximum(acc, 0.0)
    return r, r * r


def _drelu2_epi(acc, r):
    return (acc * (2.0 * r.astype(F32)),)


def _norm(x, g, name):
    return _rows_fwd(_f_norm, [x], [g], [(x.shape[1], BF16)], name=name)[0]


def _norm_bwd(x, g, ct, add, name):
    (dx,), (dg,) = _rows_vjp(_f_norm, [x], [g], [ct], name=name, drows=[0], dparams=[0], drow_dtypes=[F32], add=add)
    return dx, dg


def _layer_fwd(x0, mem, P, tabs):
    ck, sk, cq, sq = tabs
    S = {'x0': x0}
    h1 = S['h1'] = _norm(x0, P['mix_norm_g'], "norm_mix")
    win = P['win']
    u_mla = S['u_mla'] = _mm(h1, win[:, 0:U_MLA], name="in_mla")
    u_gate = S['u_gate'] = _mm(h1, win[:, U_MLA:U_MLA + U_GATE], name="in_gate")
    u_conv = S['u_conv'] = _mm(h1, win[:, U_MLA + U_GATE:], name="in_conv")
    cqn, akv = _rows_fwd(_f_mla_prep, [u_mla, ck, sk], [P['mla_q_norm_g'], P['mla_kv_norm_g']],
                         [(Q_RANK, BF16), (2 * SLAB, BF16)], name="mla_prep")
    S['cqn'], S['akv'] = cqn, akv
    yq = _mm(cqn, P['wq'], name="q_proj")
    q = S['q'] = _rows_fwd(_f_qrope, [yq, cq, sq], [], [(HEADS * SLAB, BF16)], name="q_rope")[0]
    kv = S['kv'] = _mm(akv, P['wkv'], name="kv_proj", out_dtypes=(BF16,))
    o, lse = _attn_fwd(q, kv, name="attn_fwd")
    S['o'], S['lse'] = o, lse
    c_ssm, c_lru = _conv_fwd(u_conv, P['conv_w'], P['conv_b'], name="conv_fwd")
    S['c_ssm'], S['c_lru'] = c_ssm, c_lru
    ys, sall = _ssd_fwd(c_ssm, u_gate, P['ssm_dt_bias'], P['ssm_a_log'], P['ssm_d'], P['ssm_norm_g'], name="ssd_fwd")
    S['ys'], S['sall'] = ys, sall
    a, b = _rows_fwd(_f_lru_gates, [c_lru], [P['lru_w_a'], P['lru_b_a'], P['lru_w_i'], P['lru_b_i'], P['lru_lambda']],
                     [(256, F32), (256, F32)], name="lru_gates")
    h, hprev = _lru_scan_fwd(a, b, name="lru_scan")
    S['a'], S['h'], S['hprev'] = a, h, hprev
    ymix = S['ymix'] = _rows_fwd(_f_mix, [o, ys, h, u_gate], [P['mla_out_g'], P['lru_out_g']],
                                 [(HEADS * SLAB + 512, BF16)], name="mix")[0]
    x1 = S['x1'] = _mm(ymix, P['wout'], name="out_proj", epi=_add_epi, extras=(x0,))
    hx = S['hx'] = _norm(x1, P['xattn_norm_g'], "norm_xattn")
    qx = S['qx'] = _mm(hx, P['w_mq'], name="mem_q", out_dtypes=(BF16,))
    mn = S['mn'] = _norm(mem, P['mem_norm_g'], "norm_mem")
    kx = S['kx'] = _mm(mn, P['w_mk'], name="mem_k", out_dtypes=(BF16,))
    vx = S['vx'] = _mm(mn, P['w_mv'], name="mem_v", out_dtypes=(BF16,))
    ox = S['ox'] = _rows_fwd(_f_xattn, [qx], [kx, vx], [(D, BF16)], name="xattn")[0]
    x2 = S['x2'] = _mm(ox, P['w_mo'], name="mem_o", epi=_add_epi, extras=(x1,))
    hm = S['hm'] = _norm(x2, P['mlp_norm_g'], "norm_mlp")
    r, s = _mm(hm, P['w_mlp1'], name="mlp_up", epi=_relu2_epi, out_dtypes=(BF16, BF16))
    S['r'], S['s'] = r, s
    x3 = _mm(s, P['w_mlp2'], name="mlp_down", epi=_add_epi, extras=(x2,))
    return x3, S


def _layer_bwd(dx3, mem, S, P, tabs):
    ck, sk, cq, sq = tabs
    G = {}
    da = _mm(dx3, P['w_mlp2'], "nt", name="mlp_down_dx", epi=_drelu2_epi, extras=(S['r'],), out_dtypes=(BF16,))
    G['w_mlp2'] = _mm(S['s'], dx3, "tn", name="mlp_down_dw")
    G['w_mlp1'] = _mm(S['hm'], da, "tn", name="mlp_up_dw")
    dhm = _mm(da, P['w_mlp1'], "nt", name="mlp_up_dx")
    dx2, G['mlp_norm_g'] = _norm_bwd(S['x2'], P['mlp_norm_g'], dhm, dx3, "norm_mlp_bwd")
    dox = _mm(dx2, P['w_mo'], "nt", name="mem_o_dx")
    G['w_mo'] = _mm(S['ox'], dx2, "tn", name="mem_o_dw")
    (dqx,), (dkx, dvx) = _rows_vjp(_f_xattn, [S['qx']], [S['kx'], S['vx']], [dox], name="xattn_bwd", drows=[0],
                                   dparams=[0, 1], drow_dtypes=[BF16])
    G['w_mq'] = _mm(S['hx'], dqx, "tn", name="mem_q_dw")
    dhx = _mm(dqx, P['w_mq'], "nt", name="mem_q_dx")
    dx1, G['xattn_norm_g'] = _norm_bwd(S['x1'], P['xattn_norm_g'], dhx, dx2, "norm_xattn_bwd")
    G['w_mk'] = _mm(S['mn'], dkx, "tn", name="mem_k_dw")
    G['w_mv'] = _mm(S['mn'], dvx, "tn", name="mem_v_dw")
    dmn = _mm(dkx, P['w_mk'], "nt", name="mem_k_dx", epi=_add_epi, extras=(_mm(dvx, P['w_mv'], "nt", name="mem_v_dx"),))
    _, G['mem_norm_g'] = _norm_bwd(mem, P['mem_norm_g'], dmn, None, "norm_mem_bwd")
    dymix = _mm(dx1, P['wout'], "nt", name="out_proj_dx")
    G['wout'] = _mm(S['ymix'], dx1, "tn", name="out_proj_dw")
    (do, dys, dh, dug_mix), (G['mla_out_g'], G['lru_out_g']) = _rows_vjp(
        _f_mix, [S['o'], S['ys'], S['h'], S['u_gate']], [P['mla_out_g'], P['lru_out_g']], [dymix], name="mix_bwd",
        drows=[0, 1, 2, 3], dparams=[0, 1], drow_dtypes=[BF16, F32, F32, F32])
    g, da_lru = _lru_scan_bwd(S['a'], dh, S['hprev'], name="lru_scan_bwd")
    lru_par = [P['lru_w_a'], P['lru_b_a'], P['lru_w_i'], P['lru_b_i'], P['lru_lambda']]
    (dc_lru,), dpar = _rows_vjp(_f_lru_gates, [S['c_lru']], lru_par, [da_lru, g], name="lru_gates_bwd", drows=[0],
                                dparams=[0, 1, 2, 3, 4], drow_dtypes=[F32])
    G['lru_w_a'], G['lru_b_a'], G['lru_w_i'], G['lru_b_i'], G['lru_lambda'] = dpar
    dc_ssm, dug_ssd, G['ssm_dt_bias'], G['ssm_a_log'], G['ssm_d'], G['ssm_norm_g'] = _ssd_bwd(
        S['c_ssm'], S['u_gate'], S['sall'], dys, P['ssm_dt_bias'], P['ssm_a_log'], P['ssm_d'], P['ssm_norm_g'],
        name="ssd_bwd")
    du_conv, G['conv_w'], G['conv_b'] = _conv_bwd(S['u_conv'], dc_ssm, dc_lru, P['conv_w'], name="conv_bwd")
    delta = _attn_delta(do, S['o'], name="attn_delta")
    dqt, dkv = _attn_bwd(S['q'], S['kv'], do, S['lse'], delta, name="attn_bwd")
    dyq = _qrope_bwd(dqt, cq, sq, name="q_rope_bwd")
    dcqn = _mm(dyq, P['wq'], "nt", name="q_proj_dx")
    G['wq'] = _mm(S['cqn'], dyq, "tn", name="q_proj_dw")
    dakv = _mm(dkv, P['wkv'], "nt", name="kv_proj_dx")
    G['wkv'] = _mm(S['akv'], dkv, "tn", name="kv_proj_dw")
    (du_mla,), (G['mla_q_norm_g'], G['mla_kv_norm_g']) = _rows_vjp(
        _f_mla_prep, [S['u_mla'], ck, sk], [P['mla_q_norm_g'], P['mla_kv_norm_g']], [dcqn, dakv], name="mla_prep_bwd",
        drows=[0], dparams=[0, 1], drow_dtypes=[BF16])
    du = jnp.concatenate([du_mla, (dug_mix + dug_ssd).astype(BF16), du_conv.astype(BF16)], axis=1)
    dh1 = _mm(du, P['win'], "nt", name="in_dx")
    G['win'] = _mm(S['h1'], du, "tn", name="in_dw")
    dx0, G['mix_norm_g'] = _norm_bwd(S['x0'], P['mix_norm_g'], dh1, dx1, "norm_mix_bwd")
    return dx0, G


def _local_step(x, mem, positions, Wk, Ws, tgt):
    tabs = _rope_tables(positions)
    saved, preps = [], []
    for l in range(DEPTH):
        P = _prep_layer(Wk, Ws, l)
        x, S = _layer_fwd(x, mem, P, tabs)
        saved.append(S)
        preps.append(P)
    loss, dx, dg_final = _loss_head(x, tgt, Ws['final_norm_g'].astype(F32).reshape(1, D), name="loss_head")
    pieces, small = [None] * DEPTH, [None] * DEPTH
    for l in reversed(range(DEPTH)):
        dx, G = _layer_bwd(dx, mem, saved[l], preps[l], tabs)
        pieces[l], small[l] = _unprep_pieces(G), _unprep_small(G)
    grads = {n: jnp.stack([small[l][n] for l in range(DEPTH)]) for n in SMALL if n != 'final_norm_g'}
    grads['final_norm_g'] = dg_final.reshape(D)
    return loss, dx, pieces, grads


def _adamw_nd(w, g, m, v, name):
    shp = w.shape
    two = lambda a: a.reshape(-1, shp[-1])
    return [r.reshape(shp) for r in _adamw(two(w), two(g), two(m), two(v), name=name)]


def kernel(x, mem, positions, mix_norm_g, w_in, mla_q_norm_g, mla_kv_norm_g, mla_w_uq, mla_w_ukv, mla_out_g, ssm_conv_w, ssm_conv_b, ssm_dt_bias, ssm_a_log, ssm_d, ssm_norm_g, lru_conv_w, lru_conv_b, lru_w_a, lru_b_a, lru_w_i, lru_b_i, lru_lambda, lru_out_g, w_out, xattn_norm_g, mem_norm_g, w_mq, w_mk, w_mv, w_mo, mlp_norm_g, w_mlp1, w_mlp2, final_norm_g, loss_target, m_mix_norm_g, m_w_in, m_mla_q_norm_g, m_mla_kv_norm_g, m_mla_w_uq, m_mla_w_ukv, m_mla_out_g, m_ssm_conv_w, m_ssm_conv_b, m_ssm_dt_bias, m_ssm_a_log, m_ssm_d, m_ssm_norm_g, m_lru_conv_w, m_lru_conv_b, m_lru_w_a, m_lru_b_a, m_lru_w_i, m_lru_b_i, m_lru_lambda, m_lru_out_g, m_w_out, m_xattn_norm_g, m_mem_norm_g, m_w_mq, m_w_mk, m_w_mv, m_w_mo, m_mlp_norm_g, m_w_mlp1, m_w_mlp2, m_final_norm_g, v_mix_norm_g, v_w_in, v_mla_q_norm_g, v_mla_kv_norm_g, v_mla_w_uq, v_mla_w_ukv, v_mla_out_g, v_ssm_conv_w, v_ssm_conv_b, v_ssm_dt_bias, v_ssm_a_log, v_ssm_d, v_ssm_norm_g, v_lru_conv_w, v_lru_conv_b, v_lru_w_a, v_lru_b_a, v_lru_w_i, v_lru_b_i, v_lru_lambda, v_lru_out_g, v_w_out, v_xattn_norm_g, v_mem_norm_g, v_w_mq, v_w_mk, v_w_mv, v_w_mo, v_mlp_norm_g, v_w_mlp1, v_w_mlp2, v_final_norm_g):
    a = locals()
    w = {n: a[n] for n in WEIGHTS}
    m = {n: a['m_' + n] for n in WEIGHTS}
    v = {n: a['v_' + n] for n in WEIGHTS}
    me = 4 * lax.axis_index("x") + 2 * lax.axis_index("y") + lax.axis_index("c")

    shard = {n: _K_FWD[n](w[PIECE_SOURCE[n]]) if n in _K_FWD else w[PIECE_SOURCE.get(n, n)] for n in PIECES}
    order = [(l, n) for l in range(DEPTH) for n in PIECES]
    packed_w = jnp.concatenate([shard[n][l].astype(BF16).reshape(-1, PACK_C) for l, n in order], axis=0)
    gathered = _all_gather(packed_w, name="gather_w")
    Wk, off = {n: [] for n in PIECES}, 0
    for l, n in order:
        rows = _piece_rows(n)
        Wk[n].append(_assemble(n, gathered[:, :, off:off + rows].reshape((2, 4) + PIECE_SHAPE[n])))
        off += rows
    Ws = {}
    conv_shapes = [w[n].shape for n in CONV_SHARDED]
    conv_g = _all_gather(_pack([w[n] for n in CONV_SHARDED], F32), name="gather_conv")
    conv_g = conv_g.transpose(1, 0, 2, 3).reshape((N_DEV,) + conv_g.shape[2:])
    for n, g in zip(CONV_SHARDED, _unpack(conv_g, conv_shapes, lead=(N_DEV,))):
        Ws[n] = g.transpose(1, 2, 0, 3).reshape(g.shape[1], g.shape[2], N_DEV * g.shape[3])
    for n in SMALL:
        if n not in CONV_SHARDED:
            Ws[n] = w[n]

    loss_share, dx, g_pieces, grads = _local_step(x[0], mem[0], positions[0], Wk, Ws, loss_target[0])
    loss = lax.psum(loss_share[0, 0], ("x", "y", "c"))

    packed_g = jnp.concatenate([_disassemble(n, g_pieces[l][n]).reshape(2, 4, -1, PACK_C) for l, n in order], axis=2)
    g_big = _reduce_scatter(packed_g, name="scatter_g")
    got, off = {n: [] for n in PIECES}, 0
    for l, n in order:
        rows = _piece_rows(n)
        got[n].append(g_big[off:off + rows].reshape(PIECE_SHAPE[n]))
        off += rows
    g_out = {}
    for n in PIECES:
        g = jnp.stack(got[n])
        g_out[PIECE_SOURCE.get(n, n)] = _K_INV[n](g) if n in _K_INV else g
    small_shapes = [grads[n].shape for n in SMALL]
    g_small = _all_reduce(_pack([grads[n] for n in SMALL], F32), name="reduce_g")
    for n, g in zip(SMALL, _unpack(g_small, small_shapes)):
        if n in CONV_SHARDED:
            cols = w[n].shape[-1]
            g = lax.dynamic_slice_in_dim(g, me * cols, cols, axis=2)
        g_out[n] = g

    delta, new_m, new_v = {}, {}, {}
    for n in BIG:
        delta[n], new_m[n], new_v[n] = _adamw_nd(w[n], g_out[n], m[n], v[n], "adamw_" + n)
    shapes = [w[n].shape for n in SMALL]
    packed = [_pack([d[n] for n in SMALL], F32) for d in (w, g_out, m, v)]
    for d, res in zip((delta, new_m, new_v), _adamw(*packed, name="adamw_small")):
        d.update(zip(SMALL, _unpack(res, shapes)))

    return (loss, dx[None], *[g_out[n] for n in WEIGHTS], *[delta[n] for n in WEIGHTS],
            *[new_m[n] for n in WEIGHTS], *[new_v[n] for n in WEIGHTS])
```

```python
import functools
import math

import jax
import jax.numpy as jnp
from jax import lax
from jax.experimental import pallas as pl
from jax.experimental.pallas import tpu as pltpu

F32, BF16 = jnp.float32, jnp.bfloat16

D = 1024
DEPTH = 4
N_MEM = 256
EPS = 1e-6
HEADS = 8
NOPE, ROPE, VDIM = 64, 32, 64
Q_RANK, KV_RANK = 256, 128
ROPE_THETA = 10000.0
SSM_CHUNK = 128
LRU_C = 8.0
MEM_HEADS = 4
D_FF = 4 * D
SLAB = 128
LR, B1, B2, AEPS, WD, STEP = 0.001, 0.9, 0.999, 1e-08, 0.01, 10

N_DEV = 8
MESH = pl.DeviceIdType.MESH

U_MLA = 640
U_GATE = 640
U_CONV = 768

_DN = {"nn": (((1,), (0,)), ((), ())), "nt": (((1,), (1,)), ((), ())), "tn": (((0,), (0,)), ((), ()))}


def _dot(a, b, kind):
    return lax.dot_general(a.astype(BF16), b.astype(BF16), _DN[kind], preferred_element_type=F32)


@functools.partial(jax.custom_vjp, nondiff_argnums=(2,))
def _bdot(a, b, kind):
    return _dot(a, b, kind)


def _bdot_fwd(a, b, kind):
    return _dot(a, b, kind), (a, b)


def _bdot_bwd(kind, res, g):
    a, b = res
    if kind == "nn":
        da, db = _dot(g, b, "nt"), _dot(a, g, "tn")
    elif kind == "nt":
        da, db = _dot(g, b, "nn"), _dot(g, a, "tn")
    else:
        da, db = _dot(b, g, "nt"), _dot(a, g, "nn")
    return da.astype(a.dtype), db.astype(b.dtype)


_bdot.defvjp(_bdot_fwd, _bdot_bwd)


def _tile(n, pref):
    if n <= pref:
        return n
    t = pref
    while n % t:
        t -= SLAB
    return t


def _cparams(sem, vmem_mb=48):
    return pltpu.CompilerParams(dimension_semantics=sem, vmem_limit_bytes=vmem_mb * 1024 * 1024)


def _mm(a, b, kind="nn", *, name, out_dtypes=(F32,), epi=None, extras=(), tm=1024, tn=1024, tk=1024):
    if kind == "tn":
        K, M = a.shape
    else:
        M, K = a.shape
    N = b.shape[0] if kind == "nt" else b.shape[1]
    if a.dtype == F32 or b.dtype == F32:
        tk = tk // 2
    tm, tn, tk = _tile(M, tm), _tile(N, tn), _tile(K, tk)
    nk = K // tk
    a_spec = pl.BlockSpec((tk, tm), lambda i, j, k: (k, i)) if kind == "tn" else pl.BlockSpec((tm, tk), lambda i, j, k: (i, k))
    b_spec = pl.BlockSpec((tn, tk), lambda i, j, k: (j, k)) if kind == "nt" else pl.BlockSpec((tk, tn), lambda i, j, k: (k, j))
    o_spec = pl.BlockSpec((tm, tn), lambda i, j, k: (i, j))
    n_ex, n_out = len(extras), len(out_dtypes)

    def body(*refs):
        a_ref, b_ref = refs[:2]
        ex = refs[2:2 + n_ex]
        outs = refs[2 + n_ex:2 + n_ex + n_out]
        acc = refs[-1]
        k = pl.program_id(2)

        def finish(r):
            res = epi(r, *[e[...] for e in ex]) if epi is not None else (r,)
            for o, v in zip(outs, res):
                o[...] = v.astype(o.dtype)

        if nk == 1:
            finish(_dot(a_ref[...], b_ref[...], kind))
            return

        @pl.when(k == 0)
        def _():
            acc[...] = _dot(a_ref[...], b_ref[...], kind)

        @pl.when(k > 0)
        def _():
            acc[...] += _dot(a_ref[...], b_ref[...], kind)

        @pl.when(k == nk - 1)
        def _():
            finish(acc[...])

    res = pl.pallas_call(
        body,
        name=name,
        grid=(M // tm, N // tn, nk),
        in_specs=[a_spec, b_spec] + [o_spec] * n_ex,
        out_specs=[o_spec] * n_out,
        out_shape=[jax.ShapeDtypeStruct((M, N), dt) for dt in out_dtypes],
        scratch_shapes=[pltpu.VMEM((tm, tn), F32)],
        compiler_params=_cparams(("parallel", "parallel", "arbitrary")),
    )(a, b, *extras)
    return res[0] if n_out == 1 else res


def _row_spec(arr, tm):
    return pl.BlockSpec((tm, arr.shape[1]), lambda i: (i, 0))


def _full_spec(arr):
    nd = arr.ndim
    return pl.BlockSpec(arr.shape, lambda i: (0,) * nd)


def _rows_fwd(fn, rows, params, outs, *, name, tm=256):
    T = rows[0].shape[0]
    tm = min(tm, T)
    nr, npar = len(rows), len(params)

    def body(*refs):
        ins = [r[...] for r in refs[:nr + npar]]
        res = fn(*ins)
        for o, v in zip(refs[nr + npar:], res):
            o[...] = v.astype(o.dtype)

    res = pl.pallas_call(
        body,
        name=name,
        grid=(T // tm,),
        in_specs=[_row_spec(r, tm) for r in rows] + [_full_spec(p) for p in params],
        out_specs=[pl.BlockSpec((tm, c), lambda i: (i, 0)) for c, _ in outs],
        out_shape=[jax.ShapeDtypeStruct((T, c), dt) for c, dt in outs],
        compiler_params=_cparams(("parallel",)),
    )(*rows, *params)
    return res


def _rows_vjp(fn, rows, params, cts, *, name, drows, dparams, drow_dtypes, add=None, tm=256):
    T = rows[0].shape[0]
    tm = min(tm, T)
    nr, npar, nct = len(rows), len(params), len(cts)
    n_add = 0 if add is None else 1
    n_dr, n_dp = len(drows), len(dparams)

    def body(*refs):
        row_t = [r[...] for r in refs[:nr]]
        par_t = [r[...] for r in refs[nr:nr + npar]]
        ct_t = [r[...].astype(F32) for r in refs[nr + npar:nr + npar + nct]]
        pos = nr + npar + nct
        add_t = refs[pos][...] if n_add else None
        pos += n_add
        drow_refs = refs[pos:pos + n_dr]
        dpar_refs = refs[pos + n_dr:pos + n_dr + n_dp]

        def g(*dargs):
            rr, pp = list(row_t), list(par_t)
            for idx, v in zip(drows, dargs[:n_dr]):
                rr[idx] = v
            for idx, v in zip(dparams, dargs[n_dr:]):
                pp[idx] = v
            return tuple(fn(*rr, *pp))

        prim = [row_t[i].astype(F32) for i in drows] + [par_t[i].astype(F32) for i in dparams]
        _, vjp = jax.vjp(g, *prim)
        grads = vjp(tuple(ct_t))
        for n, (o, v) in enumerate(zip(drow_refs, grads[:n_dr])):
            if n == 0 and n_add:
                v = v + add_t.astype(F32)
            o[...] = v.astype(o.dtype)

        @pl.when(pl.program_id(0) == 0)
        def _():
            for o in dpar_refs:
                o[...] = jnp.zeros_like(o)

        for o, v in zip(dpar_refs, grads[n_dr:]):
            o[...] += v

    res = pl.pallas_call(
        body,
        name=name,
        grid=(T // tm,),
        in_specs=[_row_spec(r, tm) for r in rows] + [_full_spec(p) for p in params] + [_row_spec(c, tm) for c in cts]
        + ([_row_spec(add, tm)] if n_add else []),
        out_specs=[_row_spec(rows[i], tm) for i in drows] + [_full_spec(params[i]) for i in dparams],
        out_shape=[jax.ShapeDtypeStruct(rows[i].shape, dt) for i, dt in zip(drows, drow_dtypes)]
        + [jax.ShapeDtypeStruct(params[i].shape, F32) for i in dparams],
        compiler_params=_cparams(("arbitrary",)),
    )(*rows, *params, *cts, *([add] if n_add else []))
    return list(res[:n_dr]), list(res[n_dr:])


def _rms(x, g, n):
    return x * lax.rsqrt(jnp.sum(x * x, axis=-1, keepdims=True) * (1.0 / n) + EPS) * g


def _sigmoid(x):
    return 1.0 / (1.0 + jnp.exp(-x))


def _silu(x):
    return x * _sigmoid(x)


def _softplus(x):
    return jnp.maximum(x, 0.0) + jnp.log(1.0 + jnp.exp(-jnp.abs(x)))


def _gelu_tanh(x):
    return 0.5 * x * (1.0 + jnp.tanh(math.sqrt(2.0 / math.pi) * (x + 0.044715 * x * x * x)))


def _lane(shape):
    return lax.broadcasted_iota(jnp.int32, shape, len(shape) - 1)


def _col(x, h):
    return jnp.sum(jnp.where(_lane(x.shape) == h, x, 0.0), axis=-1, keepdims=True)


def _f_norm(x, g):
    return (_rms(x.astype(F32), g, x.shape[-1]),)


def _f_mla_prep(u, ck, sk, gq, gkv):
    u = u.astype(F32)
    cq = _rms(u[:, 0:256], gq, Q_RANK)
    ckv = _rms(u[:, 256:384], gkv, KV_RANK)
    kr = u[:, 384:512] * ck + u[:, 512:640] * sk
    return cq, jnp.concatenate([ckv, kr], axis=1)


def _f_qrope(y, cq, sq):
    y = y.astype(F32)
    c8, s8 = jnp.tile(cq, (1, HEADS)), jnp.tile(sq, (1, HEADS))
    return (y[:, :HEADS * SLAB] * c8 + y[:, HEADS * SLAB:] * s8,)


def _f_lru_gates(xc, wa, ba, wi, bi, lam):
    xc = xc.astype(F32)
    r = _sigmoid(_bdot(xc, wa, "nn") + ba)
    i = _sigmoid(_bdot(xc, wi, "nn") + bi)
    log_a = -LRU_C * r * _softplus(-lam)
    a = jnp.exp(log_a)
    x2 = 2.0 * log_a
    m1 = jnp.where(x2 > -0.02, -x2 * (1.0 + x2 * (0.5 + x2 * (1.0 / 6.0 + x2 * (1.0 / 24.0)))), 1.0 - jnp.exp(x2))
    return a, jnp.sqrt(m1) * (i * xc)


def _f_mix(o, ys, h, ug, g_mla, g_lru):
    o = o.astype(F32)
    y_mla = _rms(o, g_mla, HEADS * VDIM)
    y_lru = _rms(h.astype(F32) * _gelu_tanh(ug[:, 256:512].astype(F32)), g_lru, 256)
    return (jnp.concatenate([y_mla, ys.astype(F32), y_lru], axis=1),)


def _f_xattn(q, k, v):
    hd = D // MEM_HEADS
    outs = []
    for h in range(MEM_HEADS):
        sl = slice(h * hd, (h + 1) * hd)
        s = _bdot(q[:, sl], k[:, sl], "nt") * (1.0 / math.sqrt(hd))
        s = s - jnp.max(s, axis=-1, keepdims=True)
        p = jnp.exp(s)
        p = p / jnp.sum(p, axis=-1, keepdims=True)
        outs.append(_bdot(p, v[:, sl], "nn"))
    return (jnp.concatenate(outs, axis=1),)


def _split_dot(tri, a, kind):
    a_hi = a.astype(BF16)
    r1 = a - a_hi.astype(F32)
    a_mid = r1.astype(BF16)
    a_lo = (r1 - a_mid.astype(F32)).astype(BF16)
    return _dot(tri, a_hi, kind) + _dot(tri, a_mid, kind) + _dot(tri, a_lo, kind)


@jax.custom_vjp
def _tri_cumsum(tri, a):
    return _split_dot(tri, a, "nn")


def _tri_cumsum_fwd(tri, a):
    return _split_dot(tri, a, "nn"), tri


def _tri_cumsum_bwd(tri, g):
    return jnp.zeros_like(tri), _split_dot(tri, g, "tn")


_tri_cumsum.defvjp(_tri_cumsum_fwd, _tri_cumsum_bwd)


def _f_ssd_chunk(c, ug, s0, s1, dtb, alog, dsk, ng):
    L = c.shape[0]
    c = c.astype(F32)
    xbc = _silu(c)
    xs, bm, cm = xbc[:, 0:256], xbc[:, 256:384], xbc[:, 384:512]
    z = ug[:, 0:256].astype(F32)
    dt = _softplus(ug[:, 512:640].astype(F32) + dtb)
    a = dt * (-jnp.exp(alog))
    rowi = lax.broadcasted_iota(jnp.int32, (L, L), 0)
    coli = lax.broadcasted_iota(jnp.int32, (L, L), 1)
    tril = rowi >= coli
    acum = _tri_cumsum(tril.astype(BF16), a)
    acum_t = acum.T
    lane = _lane((1, SLAB))
    lo = lane < 64
    ys, new_s = [], []
    for g in range(2):
        gm = (lane >= 64 * g) & (lane < 64 * g + 64)
        bg, cg = jnp.where(gm, bm, 0.0), jnp.where(gm, cm, 0.0)
        cb = _bdot(cg, bg, "nt")
        x = xs[:, SLAB * g:SLAB * (g + 1)]
        h0, h1 = 2 * g, 2 * g + 1
        ac0, ac1 = _col(acum, h0), _col(acum, h1)
        xdt = x * jnp.where(lo, _col(dt, h0), _col(dt, h1))
        ac_l = jnp.where(lo, ac0, ac1)
        tot = acum[L - 1:L, :]
        tot_l = jnp.where(lo, _col(tot, h0), _col(tot, h1))
        yd = jnp.zeros((L, SLAB), F32)
        for hh, acc, hm in ((h0, ac0, lo), (h1, ac1, jnp.logical_not(lo))):
            seg = acc - acum_t[hh:hh + 1, :]
            lm = jnp.where(tril, jnp.exp(jnp.where(tril, seg, 0.0)), 0.0)
            yd = yd + _bdot(cb * lm, jnp.where(hm, xdt, 0.0), "nn")
        sg = (s0, s1)[g]
        y_off = _bdot(cg, sg, "nn") * jnp.exp(ac_l)
        st = _bdot(bg, xdt * jnp.exp(tot_l - ac_l), "tn")
        new_s.append(jnp.exp(tot_l) * sg + st)
        y = yd + y_off + jnp.where(lo, _col(dsk, h0), _col(dsk, h1)) * x
        y = y * _silu(z[:, SLAB * g:SLAB * (g + 1)])
        ys.append(_rms(y, ng[:, SLAB * g:SLAB * (g + 1)], SLAB))
    return jnp.concatenate(ys, axis=1), new_s[0], new_s[1]


_SSD_TILE = 512


def _ssd_fwd(c, ug, dtb, alog, dsk, ng, *, name):
    T = c.shape[0]
    tm = min(_SSD_TILE, T)
    ncs = tm // SSM_CHUNK
    nc = T // SSM_CHUNK

    def body(c_ref, ug_ref, dtb_ref, alog_ref, dsk_ref, ng_ref, y_ref, sall_ref, s_scr):
        @pl.when(pl.program_id(0) == 0)
        def _():
            s_scr[...] = jnp.zeros_like(s_scr)

        s0, s1 = s_scr[0], s_scr[1]
        for k in range(ncs):
            rows = slice(k * SSM_CHUNK, (k + 1) * SSM_CHUNK)
            sall_ref[k, 0] = s0
            sall_ref[k, 1] = s1
            y, s0, s1 = _f_ssd_chunk(c_ref[rows, :], ug_ref[rows, :], s0, s1, dtb_ref[...], alog_ref[...],
                                     dsk_ref[...], ng_ref[...])
            y_ref[rows, :] = y
        s_scr[0] = s0
        s_scr[1] = s1

    y, sall = pl.pallas_call(
        body,
        name=name,
        grid=(T // tm,),
        in_specs=[_row_spec(c, tm), _row_spec(ug, tm)] + [_full_spec(p) for p in (dtb, alog, dsk, ng)],
        out_specs=[pl.BlockSpec((tm, 256), lambda i: (i, 0)), pl.BlockSpec((ncs, 2, SLAB, SLAB), lambda i: (i, 0, 0, 0))],
        out_shape=[jax.ShapeDtypeStruct((T, 256), F32), jax.ShapeDtypeStruct((nc, 2, SLAB, SLAB), F32)],
        scratch_shapes=[pltpu.VMEM((2, SLAB, SLAB), F32)],
        compiler_params=_cparams(("arbitrary",)),
    )(c, ug, dtb, alog, dsk, ng)
    return y, sall


def _ssd_bwd(c, ug, sall, dy, dtb, alog, dsk, ng, *, name):
    T = c.shape[0]
    tm = min(_SSD_TILE, T)
    ncs = tm // SSM_CHUNK
    nt = T // tm

    def body(c_ref, ug_ref, sall_ref, dy_ref, dtb_ref, alog_ref, dsk_ref, ng_ref,
             dc_ref, dug_ref, ddtb_ref, dalog_ref, ddsk_ref, dng_ref, ds_scr):
        @pl.when(pl.program_id(0) == 0)
        def _():
            ds_scr[...] = jnp.zeros_like(ds_scr)
            for o in (ddtb_ref, dalog_ref, ddsk_ref, dng_ref):
                o[...] = jnp.zeros_like(o)

        ds0, ds1 = ds_scr[0], ds_scr[1]
        for k in reversed(range(ncs)):
            rows = slice(k * SSM_CHUNK, (k + 1) * SSM_CHUNK)
            prim = (c_ref[rows, :].astype(F32), ug_ref[rows, :].astype(F32), sall_ref[k, 0], sall_ref[k, 1],
                    dtb_ref[...], alog_ref[...], dsk_ref[...], ng_ref[...])
            _, vjp = jax.vjp(_f_ssd_chunk, *prim)
            dc, dug, ds0, ds1, g_dtb, g_alog, g_dsk, g_ng = vjp((dy_ref[rows, :].astype(F32), ds0, ds1))
            dc_ref[rows, :] = dc
            dug_ref[rows, :] = dug
            ddtb_ref[...] += g_dtb
            dalog_ref[...] += g_alog
            ddsk_ref[...] += g_dsk
            dng_ref[...] += g_ng
        ds_scr[0] = ds0
        ds_scr[1] = ds1

    rev = lambda i: (nt - 1 - i, 0)
    params = (dtb, alog, dsk, ng)
    res = pl.pallas_call(
        body,
        name=name,
        grid=(nt,),
        in_specs=[pl.BlockSpec((tm, c.shape[1]), rev), pl.BlockSpec((tm, ug.shape[1]), rev),
                  pl.BlockSpec((ncs, 2, SLAB, SLAB), lambda i: (nt - 1 - i, 0, 0, 0)), pl.BlockSpec((tm, 256), rev)]
        + [_full_spec(p) for p in params],
        out_specs=[pl.BlockSpec((tm, 512), rev), pl.BlockSpec((tm, U_GATE), rev)] + [_full_spec(p) for p in params],
        out_shape=[jax.ShapeDtypeStruct((T, 512), F32), jax.ShapeDtypeStruct((T, U_GATE), F32)]
        + [jax.ShapeDtypeStruct(p.shape, F32) for p in params],
        scratch_shapes=[pltpu.VMEM((2, SLAB, SLAB), F32)],
        compiler_params=_cparams(("arbitrary",)),
    )(c, ug, sall, dy, *params)
    return res


_CONV_TILE = 512
_HALO = 8
_CONV_W = 4


def _conv_fwd(u, w, b, *, name):
    T, C = u.shape
    tm = min(_CONV_TILE, T)
    hb = tm // _HALO

    def body(u_ref, prev_ref, w_ref, b_ref, y1_ref, y2_ref, ext):
        i = pl.program_id(0)
        ext[0:_HALO, :] = jnp.where(i > 0, prev_ref[...], 0.0)
        ext[_HALO:, :] = u_ref[...]
        y = jnp.broadcast_to(b_ref[...], (tm, C))
        for k in range(_CONV_W):
            y = y + ext[_HALO - (_CONV_W - 1) + k:_HALO - (_CONV_W - 1) + k + tm, :] * w_ref[k:k + 1, :]
        y1_ref[...] = y[:, 0:512]
        y2_ref[...] = y[:, 512:768]

    return pl.pallas_call(
        body,
        name=name,
        grid=(T // tm,),
        in_specs=[_row_spec(u, tm), pl.BlockSpec((_HALO, C), lambda i: (jnp.maximum(i * hb - 1, 0), 0)),
                  _full_spec(w), _full_spec(b)],
        out_specs=[pl.BlockSpec((tm, 512), lambda i: (i, 0)), pl.BlockSpec((tm, 256), lambda i: (i, 0))],
        out_shape=[jax.ShapeDtypeStruct((T, 512), F32), jax.ShapeDtypeStruct((T, 256), F32)],
        scratch_shapes=[pltpu.VMEM((tm + _HALO, C), F32)],
        compiler_params=_cparams(("parallel",)),
    )(u, u, w, b)


def _conv_bwd(u, dy1, dy2, w, *, name):
    T, C = u.shape
    tm = min(_CONV_TILE, T)
    hb = tm // _HALO
    nt = T // tm

    def body(u_ref, prev_ref, dy1_ref, next1_ref, dy2_ref, next2_ref, w_ref, du_ref, dw_ref, db_ref, ext, dext):
        i = pl.program_id(0)
        ext[0:_HALO, :] = jnp.where(i > 0, prev_ref[...], 0.0)
        ext[_HALO:, :] = u_ref[...]
        dext[0:tm, 0:512] = dy1_ref[...]
        dext[0:tm, 512:768] = dy2_ref[...]
        dext[tm:, 0:512] = jnp.where(i < nt - 1, next1_ref[...], 0.0)
        dext[tm:, 512:768] = jnp.where(i < nt - 1, next2_ref[...], 0.0)

        @pl.when(i == 0)
        def _():
            dw_ref[...] = jnp.zeros_like(dw_ref)
            db_ref[...] = jnp.zeros_like(db_ref)

        dy = dext[0:tm, :]
        du = jnp.zeros((tm, C), F32)
        for k in range(_CONV_W):
            du = du + dext[_CONV_W - 1 - k:_CONV_W - 1 - k + tm, :] * w_ref[k:k + 1, :]
            xk = ext[_HALO - (_CONV_W - 1) + k:_HALO - (_CONV_W - 1) + k + tm, :]
            dw_ref[k:k + 1, :] += jnp.sum(dy * xk, axis=0, keepdims=True)
        du_ref[...] = du
        db_ref[...] += jnp.sum(dy, axis=0, keepdims=True)

    nxt = lambda i: (jnp.minimum((i + 1) * hb, T // _HALO - 1), 0)
    return pl.pallas_call(
        body,
        name=name,
        grid=(nt,),
        in_specs=[_row_spec(u, tm), pl.BlockSpec((_HALO, C), lambda i: (jnp.maximum(i * hb - 1, 0), 0)),
                  _row_spec(dy1, tm), pl.BlockSpec((_HALO, 512), nxt), _row_spec(dy2, tm), pl.BlockSpec((_HALO, 256), nxt),
                  _full_spec(w)],
        out_specs=[pl.BlockSpec((tm, C), lambda i: (i, 0)), _full_spec(w), pl.BlockSpec((1, C), lambda i: (0, 0))],
        out_shape=[jax.ShapeDtypeStruct((T, C), F32), jax.ShapeDtypeStruct(w.shape, F32), jax.ShapeDtypeStruct((1, C), F32)],
        scratch_shapes=[pltpu.VMEM((tm + _HALO, C), F32), pltpu.VMEM((tm + _HALO, C), F32)],
        compiler_params=_cparams(("arbitrary",)),
    )(u, u, dy1, dy1, dy2, dy2, w)


_SCAN_TILE = 1024
_SUB = 8


def _shift_rows(x, d, fill, up):
    r = lax.broadcasted_iota(jnp.int32, x.shape, 0)
    if up:
        return jnp.where(r < _SUB - d, pltpu.roll(x, _SUB - d, 0), fill)
    return jnp.where(r >= d, pltpu.roll(x, d, 0), fill)


def _lru_scan_fwd(a, b, *, name):
    T, W = a.shape
    tr = min(_SCAN_TILE, T)

    def body(a_ref, b_ref, h_ref, hp_ref, carry):
        @pl.when(pl.program_id(0) == 0)
        def _():
            carry[...] = jnp.zeros_like(carry)

        def step(t, cr):
            rows = pl.ds(pl.multiple_of(t * _SUB, _SUB), _SUB)
            aa, bb = a_ref[rows, :], b_ref[rows, :]
            for d in (1, 2, 4):
                bb = bb + aa * _shift_rows(bb, d, 0.0, False)
                aa = aa * _shift_rows(aa, d, 1.0, False)
            h = bb + aa * cr
            h_ref[rows, :] = h
            r = lax.broadcasted_iota(jnp.int32, h.shape, 0)
            hp_ref[rows, :] = jnp.where(r >= 1, pltpu.roll(h, 1, 0), cr)
            return jnp.broadcast_to(h[_SUB - 1:_SUB, :], (_SUB, W))

        carry[...] = lax.fori_loop(0, tr // _SUB, step, carry[...])

    return pl.pallas_call(
        body,
        name=name,
        grid=(T // tr,),
        in_specs=[_row_spec(a, tr), _row_spec(b, tr)],
        out_specs=[pl.BlockSpec((tr, W), lambda i: (i, 0))] * 2,
        out_shape=[jax.ShapeDtypeStruct((T, W), F32)] * 2,
        scratch_shapes=[pltpu.VMEM((_SUB, W), F32)],
        compiler_params=_cparams(("arbitrary",)),
    )(a, b)


def _lru_scan_bwd(a, dh, hprev, *, name):
    T, W = a.shape
    tr = min(_SCAN_TILE, T)
    nt = T // tr

    def body(a_ref, dh_ref, hp_ref, g_ref, da_ref, carry):
        @pl.when(pl.program_id(0) == 0)
        def _():
            carry[...] = jnp.zeros_like(carry)

        nsub = tr // _SUB

        def step(s, cr):
            t = nsub - 1 - s
            rows = pl.ds(pl.multiple_of(t * _SUB, _SUB), _SUB)
            a_t = a_ref[rows, :]
            aa = _shift_rows(a_t, 1, 1.0, True)
            bb = dh_ref[rows, :]
            for d in (1, 2, 4):
                bb = bb + aa * _shift_rows(bb, d, 0.0, True)
                aa = aa * _shift_rows(aa, d, 1.0, True)
            g = bb + aa * cr
            g_ref[rows, :] = g
            da_ref[rows, :] = g * hp_ref[rows, :]
            return jnp.broadcast_to(a_t[0:1, :] * g[0:1, :], (_SUB, W))

        carry[...] = lax.fori_loop(0, nsub, step, carry[...])

    rev = lambda i: (nt - 1 - i, 0)
    return pl.pallas_call(
        body,
        name=name,
        grid=(nt,),
        in_specs=[pl.BlockSpec((tr, W), rev)] * 3,
        out_specs=[pl.BlockSpec((tr, W), rev)] * 2,
        out_shape=[jax.ShapeDtypeStruct((T, W), F32)] * 2,
        scratch_shapes=[pltpu.VMEM((_SUB, W), F32)],
        compiler_params=_cparams(("arbitrary",)),
    )(a, dh, hprev)


_ATT_BLK = 512
_ATT_SCALE = 1.0 / math.sqrt(NOPE + ROPE)


def _attn_fwd(q, kv, *, name):
    T = q.shape[0]
    blk = min(_ATT_BLK, T)
    nq = T // blk

    def body(q_ref, kv_ref, o_ref, lse_ref):
        i = pl.program_id(1)
        qb = q_ref[...]
        qpos = i * blk + lax.broadcasted_iota(jnp.int32, (blk, blk), 0)

        def rows_of(j):
            return pl.ds(pl.multiple_of(j * blk, blk), blk)

        def scores(j):
            return _dot(qb, kv_ref[rows_of(j), 0:SLAB], "nt")

        def update(j, s, m, l, acc):
            m_new = jnp.maximum(m, jnp.max(s, axis=-1, keepdims=True))
            alpha = jnp.exp(m - m_new)
            p = jnp.exp(s - m_new)
            l = alpha * l + jnp.sum(p, axis=-1, keepdims=True)
            acc = alpha * acc + _dot(p, kv_ref[rows_of(j), SLAB:2 * SLAB], "nn")
            return m_new, l, acc

        def step(j, carry):
            s, m, l, acc = carry
            s_next = scores(j + 1)
            return (s_next,) + update(j, s * _ATT_SCALE, m, l, acc)

        init = (scores(0), jnp.full((blk, 1), -jnp.inf, F32), jnp.zeros((blk, 1), F32), jnp.zeros((blk, SLAB), F32))
        s, m, l, acc = lax.fori_loop(0, i, step, init)
        kpos = i * blk + lax.broadcasted_iota(jnp.int32, (blk, blk), 1)
        m, l, acc = update(i, jnp.where(kpos <= qpos, s * _ATT_SCALE, -jnp.inf), m, l, acc)
        o_ref[...] = acc / l
        lse = jnp.broadcast_to(m + jnp.log(l), (blk, SLAB))
        lse_ref[...] = lse.T[0:_SUB, :]

    return pl.pallas_call(
        body,
        name=name,
        grid=(HEADS, nq),
        in_specs=[pl.BlockSpec((blk, SLAB), lambda h, i: (i, h)), pl.BlockSpec((T, 2 * SLAB), lambda h, i: (0, h))],
        out_specs=[pl.BlockSpec((blk, SLAB), lambda h, i: (i, h)),
                   pl.BlockSpec((None, None, _SUB, blk), lambda h, i: (h, i, 0, 0))],
        out_shape=[jax.ShapeDtypeStruct((T, HEADS * SLAB), F32), jax.ShapeDtypeStruct((HEADS, nq, _SUB, blk), F32)],
        compiler_params=_cparams(("parallel", "arbitrary")),
    )(q, kv)


def _attn_delta(do, o, *, name):
    T = o.shape[0]
    blk = min(_ATT_BLK, T)
    nq = T // blk

    def body(do_ref, o_ref, d_ref):
        dl = jnp.sum(do_ref[...].astype(F32) * o_ref[...], axis=-1, keepdims=True)
        d_ref[...] = jnp.broadcast_to(dl, (blk, SLAB)).T[0:_SUB, :]

    return pl.pallas_call(
        body,
        name=name,
        grid=(HEADS, nq),
        in_specs=[pl.BlockSpec((blk, SLAB), lambda h, i: (i, h))] * 2,
        out_specs=pl.BlockSpec((None, None, _SUB, blk), lambda h, i: (h, i, 0, 0)),
        out_shape=jax.ShapeDtypeStruct((HEADS, nq, _SUB, blk), F32),
        compiler_params=_cparams(("parallel", "parallel")),
    )(do, o)


def _attn_bwd(q, kv, do, lse, delta, *, name):
    T = q.shape[0]
    blk = min(_ATT_BLK, T)
    nq = T // blk

    def body(q_ref, kv_ref, do_ref, lse_ref, dl_ref, dqt_ref, dkv_ref):
        j = pl.program_id(1)

        @pl.when(j == 0)
        def _():
            dqt_ref[...] = jnp.zeros_like(dqt_ref)

        kb, vb = kv_ref[:, 0:SLAB], kv_ref[:, SLAB:2 * SLAB]
        kbt = kb.astype(F32).T.astype(BF16)
        kpos = j * blk + lax.broadcasted_iota(jnp.int32, (blk, blk), 0)

        def rows_of(i):
            return pl.ds(pl.multiple_of(i * blk, blk), blk)

        def products(i):
            i = jnp.minimum(i, nq - 1)
            return _dot(kb, q_ref[rows_of(i), :], "nt"), _dot(vb, do_ref[rows_of(i), :], "nt")

        def consume(i, st, dpt, dk, dv):
            qb, dob = q_ref[rows_of(i), :], do_ref[rows_of(i), :]
            pt = jnp.exp(st - lse_ref[i, 0:1, :])
            dst = pt * (dpt - dl_ref[i, 0:1, :]) * _ATT_SCALE
            dv = dv + _dot(pt, dob, "nn")
            dk = dk + _dot(dst, qb, "nn")
            dqt_ref[i] += _dot(kbt, dst, "nn")
            return dk, dv

        def step(i, carry):
            st, dpt, dk, dv = carry
            nxt = products(i + 1)
            return nxt + consume(i, st * _ATT_SCALE, dpt, dk, dv)

        zero = jnp.zeros((blk, SLAB), F32)
        st, dpt = products(j)
        nxt = products(j + 1)
        qpos = j * blk + lax.broadcasted_iota(jnp.int32, (blk, blk), 1)
        carry = consume(j, jnp.where(kpos <= qpos, st * _ATT_SCALE, -jnp.inf), dpt, zero, zero)
        _, _, dk, dv = lax.fori_loop(j + 1, nq, step, nxt + carry)
        dkv_ref[:, 0:SLAB] = dk.astype(BF16)
        dkv_ref[:, SLAB:2 * SLAB] = dv.astype(BF16)

    stat_spec = pl.BlockSpec((None, nq, _SUB, blk), lambda h, j: (h, 0, 0, 0))
    return pl.pallas_call(
        body,
        name=name,
        grid=(HEADS, nq),
        in_specs=[pl.BlockSpec((T, SLAB), lambda h, j: (0, h)), pl.BlockSpec((blk, 2 * SLAB), lambda h, j: (j, h)),
                  pl.BlockSpec((T, SLAB), lambda h, j: (0, h)), stat_spec, stat_spec],
        out_specs=[pl.BlockSpec((None, nq, SLAB, blk), lambda h, j: (h, 0, 0, 0)),
                   pl.BlockSpec((blk, 2 * SLAB), lambda h, j: (j, h))],
        out_shape=[jax.ShapeDtypeStruct((HEADS, nq, SLAB, blk), F32), jax.ShapeDtypeStruct((T, HEADS * 2 * SLAB), BF16)],
        compiler_params=_cparams(("parallel", "arbitrary")),
    )(q, kv, do, lse, delta)


def _qrope_bwd(dqt, cq, sq, *, name):
    _, nq, _, blk = dqt.shape
    T = nq * blk

    def body(dqt_ref, c_ref, s_ref, dy_ref):
        c, s = c_ref[...], s_ref[...]
        for h in range(HEADS):
            dq = dqt_ref[h].T
            dy_ref[:, h * SLAB:(h + 1) * SLAB] = (dq * c).astype(BF16)
            dy_ref[:, (HEADS + h) * SLAB:(HEADS + h + 1) * SLAB] = (dq * s).astype(BF16)

    return pl.pallas_call(
        body,
        name=name,
        grid=(nq,),
        in_specs=[pl.BlockSpec((HEADS, None, SLAB, blk), lambda i: (0, i, 0, 0)), _row_spec(cq, blk), _row_spec(sq, blk)],
        out_specs=pl.BlockSpec((blk, 2 * HEADS * SLAB), lambda i: (i, 0)),
        out_shape=jax.ShapeDtypeStruct((T, 2 * HEADS * SLAB), BF16),
        compiler_params=_cparams(("parallel",)),
    )(dqt, cq, sq)


def _loss_head(x, tgt, g, *, name, tm=256):
    T = x.shape[0]
    tm = min(tm, T)

    def body(x_ref, t_ref, g_ref, loss_ref, dx_ref, dg_ref):
        def f(xv, gv):
            e = _rms(xv, gv, D) - t_ref[...]
            row = jnp.sum(e * e, axis=1, keepdims=True)
            return jnp.sum(row, axis=0, keepdims=True) * (0.5 / D)

        val, vjp = jax.vjp(f, x_ref[...], g_ref[...])
        dxv, dgv = vjp(jnp.ones((1, 1), F32))

        @pl.when(pl.program_id(0) == 0)
        def _():
            loss_ref[...] = jnp.zeros_like(loss_ref)
            dg_ref[...] = jnp.zeros_like(dg_ref)

        dx_ref[...] = dxv
        dg_ref[...] += dgv
        loss_ref[...] += jnp.broadcast_to(val, loss_ref.shape)

    return pl.pallas_call(
        body,
        name=name,
        grid=(T // tm,),
        in_specs=[_row_spec(x, tm), _row_spec(tgt, tm), _full_spec(g)],
        out_specs=[pl.BlockSpec((1, SLAB), lambda i: (0, 0)), _row_spec(x, tm), _full_spec(g)],
        out_shape=[jax.ShapeDtypeStruct((1, SLAB), F32), jax.ShapeDtypeStruct(x.shape, F32), jax.ShapeDtypeStruct(g.shape, F32)],
        compiler_params=_cparams(("arbitrary",)),
    )(x, tgt, g)


def _row_tile(rows, cols, budget=256 * 1024):
    best = None
    for t in range(16, rows + 1, 16):
        if rows % t == 0 and t * cols <= budget:
            best = t
    return best or rows


def _sum_fixed(x, out_dtype, *, name):
    n, R, C = x.shape
    tr = _row_tile(R, C)

    def body(x_ref, o_ref):
        acc = x_ref[0].astype(F32)
        for k in range(1, n):
            acc = acc + x_ref[k].astype(F32)
        o_ref[...] = acc.astype(o_ref.dtype)

    return pl.pallas_call(
        body,
        name=name,
        grid=(R // tr,),
        in_specs=[pl.BlockSpec((n, tr, C), lambda i: (0, i, 0))],
        out_specs=pl.BlockSpec((tr, C), lambda i: (i, 0)),
        out_shape=jax.ShapeDtypeStruct((R, C), out_dtype),
        compiler_params=_cparams(("parallel",)),
    )(x)


def _adamw(w, g, m, v, *, name):
    R, C = w.shape
    tr = _row_tile(R, C, 128 * 1024)

    def body(w_ref, g_ref, m_ref, v_ref, d_ref, nm_ref, nv_ref):
        gv = g_ref[...]
        mv = B1 * m_ref[...] + (1.0 - B1) * gv
        vv = B2 * v_ref[...] + (1.0 - B2) * (gv * gv)
        m_hat = mv / (1.0 - B1 ** STEP)
        v_hat = vv / (1.0 - B2 ** STEP)
        d_ref[...] = -LR * (m_hat / (jnp.sqrt(v_hat) + AEPS) + WD * w_ref[...])
        nm_ref[...] = mv
        nv_ref[...] = vv

    spec = pl.BlockSpec((tr, C), lambda i: (i, 0))
    return pl.pallas_call(
        body, name=name, grid=(R // tr,), in_specs=[spec] * 4, out_specs=[spec] * 3,
        out_shape=[jax.ShapeDtypeStruct((R, C), F32)] * 3, compiler_params=_cparams(("parallel",)),
    )(w, g, m, v)


_FLIPS = ((1, 0), (0, 1), (1, 1))
_ANY = pl.BlockSpec(memory_space=pl.ANY)


def _me():
    return lax.axis_index("x"), lax.axis_index("y"), lax.axis_index("c")


def _flip(mx, my, f):
    return (1 - mx if f[0] else mx), (1 - my if f[1] else my)


def _chip_exchange(x, gather, *, name):
    shape = x.shape if not gather else (4,) + x.shape

    def body(x_ref, out_ref, send_sems, recv_sems, local_sem):
        mx, my, mc = _me()
        mine = 2 * mx + my
        src_own = x_ref if gather else x_ref.at[mine]
        if not gather:
            local = pltpu.make_async_copy(src_own, out_ref.at[mine], local_sem)
            local.start()
        sends = []
        for k, f in enumerate(_FLIPS):
            px, py = _flip(mx, my, f)
            src = x_ref if gather else x_ref.at[2 * px + py]
            cp = pltpu.make_async_remote_copy(src_ref=src, dst_ref=out_ref.at[mine], send_sem=send_sems.at[k],
                                              recv_sem=recv_sems.at[k], device_id=(px, py, mc), device_id_type=MESH)
            cp.start()
            sends.append(cp)
        for k, f in enumerate(_FLIPS):
            px, py = _flip(mx, my, f)
            pltpu.make_async_remote_copy(src_ref=src_own, dst_ref=out_ref.at[2 * px + py], send_sem=send_sems.at[k],
                                         recv_sem=recv_sems.at[k], device_id=(px, py, mc), device_id_type=MESH).wait_recv()
        for cp in sends:
            cp.wait_send()
        if not gather:
            local.wait()

    return pl.pallas_call(
        body, name=name, in_specs=[_ANY], out_specs=_ANY, out_shape=jax.ShapeDtypeStruct(shape, x.dtype),
        scratch_shapes=[pltpu.SemaphoreType.DMA((3,)), pltpu.SemaphoreType.DMA((3,)), pltpu.SemaphoreType.DMA],
    )(x)


def _core_exchange(x, half, *, name):
    shape = x.shape[1:] if half else x.shape

    def body(x_ref, out_ref, send_sem, recv_sem):
        mx, my, mc = _me()
        src = x_ref.at[1 - mc] if half else x_ref
        cp = pltpu.make_async_remote_copy(src_ref=src, dst_ref=out_ref, send_sem=send_sem, recv_sem=recv_sem,
                                          device_id=(mx, my, 1 - mc), device_id_type=MESH)
        cp.start()
        cp.wait()

    return pl.pallas_call(
        body, name=name, in_specs=[_ANY], out_specs=_ANY, out_shape=jax.ShapeDtypeStruct(shape, x.dtype),
        scratch_shapes=[pltpu.SemaphoreType.DMA, pltpu.SemaphoreType.DMA],
    )(x)


def _all_gather(x, *, name):
    mx, my, mc = _me()
    g4 = _chip_exchange(x, True, name=name + "_chips")
    own = (jnp.arange(4) == 2 * mx + my).reshape((4,) + (1,) * x.ndim)
    g4 = jnp.where(own, x[None], g4)
    sib = _core_exchange(g4, False, name=name + "_cores")
    return jnp.where(mc == 0, jnp.stack([g4, sib]), jnp.stack([sib, g4]))


def _add_own_half(x, got, *, name):
    _, R, C = x.shape
    tr = _row_tile(R, C)

    def body(c_ref, x_ref, g_ref, o_ref):
        o_ref[...] = (x_ref[...] + g_ref[...]).astype(o_ref.dtype)

    return pl.pallas_call(
        body,
        name=name,
        grid_spec=pltpu.PrefetchScalarGridSpec(
            num_scalar_prefetch=1, grid=(R // tr,),
            in_specs=[pl.BlockSpec((None, tr, C), lambda i, c: (c[0], i, 0)), pl.BlockSpec((tr, C), lambda i, c: (i, 0))],
            out_specs=pl.BlockSpec((tr, C), lambda i, c: (i, 0))),
        out_shape=jax.ShapeDtypeStruct((R, C), BF16),
        compiler_params=_cparams(("parallel",)),
    )(lax.axis_index("c").astype(jnp.int32).reshape(1), x, got)


def _reduce_scatter(x, *, name):
    _, _, R, C = x.shape
    got = _core_exchange(x, True, name=name + "_cores")
    t = _add_own_half(x.reshape(2, 4 * R, C), got.reshape(4 * R, C), name=name + "_add").reshape(4, R, C)
    parts = _chip_exchange(t, False, name=name + "_chips")
    return _sum_fixed(parts, F32, name=name + "_sum")


def _all_reduce(x, *, name):
    g = _all_gather(x, name=name)
    return _sum_fixed(g.reshape((N_DEV,) + x.shape), F32, name=name + "_sum")


WEIGHTS = ['mix_norm_g', 'w_in', 'mla_q_norm_g', 'mla_kv_norm_g', 'mla_w_uq', 'mla_w_ukv', 'mla_out_g', 'ssm_conv_w',
           'ssm_conv_b', 'ssm_dt_bias', 'ssm_a_log', 'ssm_d', 'ssm_norm_g', 'lru_conv_w', 'lru_conv_b', 'lru_w_a',
           'lru_b_a', 'lru_w_i', 'lru_b_i', 'lru_lambda', 'lru_out_g', 'w_out', 'xattn_norm_g', 'mem_norm_g', 'w_mq',
           'w_mk', 'w_mv', 'w_mo', 'mlp_norm_g', 'w_mlp1', 'w_mlp2', 'final_norm_g']
ROW_SHARDED = ('w_in', 'w_out', 'w_mq', 'w_mk', 'w_mv', 'w_mo', 'w_mlp2')
COL_SHARDED = ('mla_w_uq', 'mla_w_ukv', 'w_mlp1')
BIG = tuple(n for n in WEIGHTS if n in ROW_SHARDED + COL_SHARDED)
CONV_SHARDED = ('ssm_conv_w', 'lru_conv_w')
SMALL = tuple(n for n in WEIGHTS if n not in BIG)
PACK_C = 1024


def _pack(arrs, dtype, lead=()):
    flat = jnp.concatenate([a.reshape(lead + (-1,)).astype(dtype) for a in arrs], axis=-1)
    n = flat.shape[-1]
    rows = -(-n // (16 * PACK_C)) * 16
    flat = jnp.pad(flat, [(0, 0)] * len(lead) + [(0, rows * PACK_C - n)])
    return flat.reshape(lead + (rows, PACK_C))


def _unpack(packed, shapes, lead=()):
    flat = packed.reshape(lead + (-1,))
    out, off = [], 0
    for s in shapes:
        n = math.prod(s)
        out.append(flat[..., off:off + n].reshape(lead + tuple(s)))
        off += n
    return out


def _pad_lanes(v, n=SLAB):
    return jnp.pad(v.astype(F32), (0, n - v.shape[0])).reshape(1, n)


PIECES = ('win', 'wq', 'wkv', 'wout', 'w_mq', 'w_mk', 'w_mv', 'w_mo', 'w_mlp1', 'w_mlp2')
PIECE_SOURCE = {'win': 'w_in', 'wq': 'mla_w_uq', 'wkv': 'mla_w_ukv', 'wout': 'w_out'}
PIECE_SHAPE = {'win': (128, 2048), 'wq': (Q_RANK, 2 * SLAB), 'wkv': (KV_RANK, 2 * SLAB), 'wout': (128, D),
               'w_mq': (128, D), 'w_mk': (128, D), 'w_mv': (128, D), 'w_mo': (128, D), 'w_mlp1': (D, 512),
               'w_mlp2': (512, D)}
PIECE_COLS = ('wq', 'wkv', 'w_mlp1')


def _k_win(w):
    kr = w[..., 384:416]
    zc = lambda k: jnp.zeros(w.shape[:-1] + (k,), w.dtype)
    return jnp.concatenate(
        [w[..., 0:384], kr, zc(96), kr[..., 16:32], kr[..., 0:16], zc(96),
         w[..., 416:672], w[..., 1444:1700], w[..., 1184:1188], zc(124),
         w[..., 672:1184], w[..., 1188:1444]], axis=-1)


def _k_win_inv(m):
    gk = m[..., 384:416] + jnp.concatenate([m[..., 528:544], m[..., 512:528]], axis=-1)
    return jnp.concatenate([m[..., 0:384], gk, m[..., 640:896], m[..., 1280:1792], m[..., 1152:1156], m[..., 1792:2048],
                            m[..., 896:1152]], axis=-1)


def _k_wq(w):
    nh = w.shape[-1] // (NOPE + ROPE)
    w = w.reshape(w.shape[:-1] + (nh, NOPE + ROPE))
    nope, r1, r2 = w[..., :NOPE], w[..., NOPE:NOPE + 16], w[..., NOPE + 16:]
    z = lambda k: jnp.zeros(w.shape[:-1] + (k,), w.dtype)
    both = jnp.stack([jnp.concatenate([nope, r1, r2, z(32)], -1), jnp.concatenate([z(64), r2, r1, z(32)], -1)], axis=-3)
    return both.reshape(w.shape[:-2] + (2 * nh * SLAB,))


def _k_wq_inv(m):
    nh = m.shape[-1] // (2 * SLAB)
    m = m.reshape(m.shape[:-1] + (2, nh, SLAB))
    q0, q1 = m[..., 0, :, :], m[..., 1, :, :]
    w = jnp.concatenate([q0[..., :64], q0[..., 64:80] + q1[..., 80:96], q0[..., 80:96] + q1[..., 64:80]], -1)
    return w.reshape(w.shape[:-2] + (nh * (NOPE + ROPE),))


def _k_wkv(w):
    nh = w.shape[-1] // (NOPE + VDIM)
    w = w.reshape(w.shape[:-1] + (nh, NOPE + VDIM))
    z = jnp.zeros(w.shape[:-1] + (64,), w.dtype)
    return jnp.concatenate([w[..., :NOPE], z, w[..., NOPE:], z], -1).reshape(w.shape[:-2] + (nh * 2 * SLAB,))


def _k_wkv_inv(m):
    nh = m.shape[-1] // (2 * SLAB)
    m = m.reshape(m.shape[:-1] + (nh, 2 * SLAB))
    return jnp.concatenate([m[..., :NOPE], m[..., SLAB:SLAB + VDIM]], -1).reshape(m.shape[:-2] + (nh * (NOPE + VDIM),))


_K_FWD = {'win': _k_win, 'wq': _k_wq, 'wkv': _k_wkv}
_K_INV = {'win': _k_win_inv, 'wq': _k_wq_inv, 'wkv': _k_wkv_inv}


def _assemble(piece, g):
    _, _, a, b = g.shape
    if piece == 'wq':
        return g.reshape(2, 4, a, 2, SLAB).transpose(2, 3, 1, 0, 4).reshape(a, N_DEV * b)
    if piece in PIECE_COLS:
        return g.transpose(2, 1, 0, 3).reshape(a, N_DEV * b)
    return g.transpose(1, 0, 2, 3).reshape(N_DEV * a, b)


def _disassemble(piece, full):
    a, b = PIECE_SHAPE[piece]
    if piece == 'wq':
        return full.reshape(a, 2, 4, 2, SLAB).transpose(3, 2, 0, 1, 4).reshape(2, 4, a, b)
    if piece in PIECE_COLS:
        return full.reshape(a, 4, 2, b).transpose(2, 1, 0, 3)
    return full.reshape(4, 2, a, b).transpose(1, 0, 2, 3)


def _piece_rows(piece):
    a, b = PIECE_SHAPE[piece]
    return a * b // PACK_C


def _prep_layer(Wk, Ws, l):
    P = {n: Wk[n][l] for n in PIECES}
    ri, ci = jnp.arange(SLAB)[:, None], jnp.arange(2 * SLAB)[None, :]
    sel = ((ri < ROPE) & (ci == ri + NOPE)).astype(P['wkv'].dtype)
    P['wkv'] = jnp.concatenate([P['wkv'], jnp.tile(sel, (1, HEADS))], axis=0)
    wout = P['wout']
    mla_rows = jnp.pad(wout[:HEADS * VDIM].reshape(HEADS, VDIM, D), ((0, 0), (0, SLAB - VDIM), (0, 0)))
    P['wout'] = jnp.concatenate([mla_rows.reshape(HEADS * SLAB, D), wout[HEADS * VDIM:]], axis=0)
    W = Ws
    row = lambda n: W[n][l].astype(F32).reshape(1, -1)
    for n in ('mix_norm_g', 'mla_q_norm_g', 'mla_kv_norm_g', 'ssm_norm_g', 'lru_lambda', 'lru_out_g', 'xattn_norm_g',
              'mem_norm_g', 'mlp_norm_g'):
        P[n] = row(n)
    P['mla_out_g'] = jnp.pad(W['mla_out_g'][l].astype(F32).reshape(HEADS, VDIM), ((0, 0), (0, SLAB - VDIM))).reshape(1, -1)
    for n in ('ssm_dt_bias', 'ssm_a_log', 'ssm_d'):
        P[n] = _pad_lanes(W[n][l])
    P['conv_w'] = jnp.pad(jnp.concatenate([W['ssm_conv_w'][l], W['lru_conv_w'][l]], axis=1).astype(F32), ((0, 4), (0, 0)))
    P['conv_b'] = jnp.concatenate([W['ssm_conv_b'][l], W['lru_conv_b'][l]]).astype(F32).reshape(1, -1)
    for n in ('lru_w_a', 'lru_w_i'):
        P[n] = jnp.concatenate([jnp.pad(W[n][l, k].astype(F32), ((0, 0), (64 * k, 192 - 64 * k))) for k in range(4)], axis=0)
    for n in ('lru_b_a', 'lru_b_i'):
        P[n] = W[n][l].astype(F32).reshape(1, -1)
    return P


def _unprep_pieces(G):
    o = {n: G[n] for n in PIECES}
    o['wkv'] = G['wkv'][:KV_RANK]
    wo = G['wout']
    o['wout'] = jnp.concatenate([wo[:HEADS * SLAB].reshape(HEADS, SLAB, D)[:, :VDIM].reshape(HEADS * VDIM, D),
                                 wo[HEADS * SLAB:]], axis=0)
    return o


def _unprep_small(G):
    o = {}
    for n in ('mix_norm_g', 'mla_q_norm_g', 'mla_kv_norm_g', 'ssm_norm_g', 'lru_lambda', 'lru_out_g', 'xattn_norm_g',
              'mem_norm_g', 'mlp_norm_g', 'lru_b_a', 'lru_b_i'):
        o[n] = G[n].reshape(-1)
    o['lru_b_a'] = o['lru_b_a'].reshape(4, 64)
    o['lru_b_i'] = o['lru_b_i'].reshape(4, 64)
    o['mla_out_g'] = G['mla_out_g'].reshape(HEADS, SLAB)[:, :VDIM].reshape(-1)
    for n in ('ssm_dt_bias', 'ssm_a_log', 'ssm_d'):
        o[n] = G[n][0, :4]
    o['ssm_conv_w'], o['lru_conv_w'] = G['conv_w'][:4, :512], G['conv_w'][:4, 512:]
    o['ssm_conv_b'], o['lru_conv_b'] = G['conv_b'][0, :512], G['conv_b'][0, 512:]
    for n in ('lru_w_a', 'lru_w_i'):
        o[n] = jnp.stack([G[n][64 * k:64 * (k + 1), 64 * k:64 * (k + 1)] for k in range(4)])
    return o


def _rope_tables(positions):
    half = ROPE // 2
    inv_freq = ROPE_THETA ** (-jnp.arange(half, dtype=F32) * 2.0 / ROPE)
    ang = positions.astype(F32)[:, None] * inv_freq
    cos, sin = jnp.cos(ang), jnp.sin(ang)
    T = positions.shape[0]
    z = lambda k: jnp.zeros((T, k), F32)
    ck = jnp.concatenate([cos, cos, z(96)], axis=1)
    sk = jnp.concatenate([-sin, sin, z(96)], axis=1)
    cq = jnp.concatenate([jnp.ones((T, NOPE), F32), cos, cos, z(32)], axis=1)
    sq = jnp.concatenate([z(NOPE), -sin, sin, z(32)], axis=1)
    return ck, sk, cq, sq


def _add_epi(acc, res):
    return (acc + res,)


def _relu2_epi(acc):
    r = jnp.maximum(acc, 0.0)
    return r, r * r


def _drelu2_epi(acc, r):
    return (acc * (2.0 * r.astype(F32)),)


def _norm(x, g, name):
    return _rows_fwd(_f_norm, [x], [g], [(x.shape[1], BF16)], name=name)[0]


def _norm_bwd(x, g, ct, add, name):
    (dx,), (dg,) = _rows_vjp(_f_norm, [x], [g], [ct], name=name, drows=[0], dparams=[0], drow_dtypes=[F32], add=add)
    return dx, dg


def _layer_fwd(x0, mem, P, tabs):
    ck, sk, cq, sq = tabs
    S = {'x0': x0}
    h1 = S['h1'] = _norm(x0, P['mix_norm_g'], "norm_mix")
    win = P['win']
    u_mla = S['u_mla'] = _mm(h1, win[:, 0:U_MLA], name="in_mla")
    u_gate = S['u_gate'] = _mm(h1, win[:, U_MLA:U_MLA + U_GATE], name="in_gate")
    u_conv = S['u_conv'] = _mm(h1, win[:, U_MLA + U_GATE:], name="in_conv")
    cqn, akv = _rows_fwd(_f_mla_prep, [u_mla, ck, sk], [P['mla_q_norm_g'], P['mla_kv_norm_g']],
                         [(Q_RANK, BF16), (2 * SLAB, BF16)], name="mla_prep")
    S['cqn'], S['akv'] = cqn, akv
    yq = _mm(cqn, P['wq'], name="q_proj")
    q = S['q'] = _rows_fwd(_f_qrope, [yq, cq, sq], [], [(HEADS * SLAB, BF16)], name="q_rope")[0]
    kv = S['kv'] = _mm(akv, P['wkv'], name="kv_proj", out_dtypes=(BF16,))
    o, lse = _attn_fwd(q, kv, name="attn_fwd")
    S['o'], S['lse'] = o, lse
    c_ssm, c_lru = _conv_fwd(u_conv, P['conv_w'], P['conv_b'], name="conv_fwd")
    S['c_ssm'], S['c_lru'] = c_ssm, c_lru
    ys, sall = _ssd_fwd(c_ssm, u_gate, P['ssm_dt_bias'], P['ssm_a_log'], P['ssm_d'], P['ssm_norm_g'], name="ssd_fwd")
    S['ys'], S['sall'] = ys, sall
    a, b = _rows_fwd(_f_lru_gates, [c_lru], [P['lru_w_a'], P['lru_b_a'], P['lru_w_i'], P['lru_b_i'], P['lru_lambda']],
                     [(256, F32), (256, F32)], name="lru_gates")
    h, hprev = _lru_scan_fwd(a, b, name="lru_scan")
    S['a'], S['h'], S['hprev'] = a, h, hprev
    ymix = S['ymix'] = _rows_fwd(_f_mix, [o, ys, h, u_gate], [P['mla_out_g'], P['lru_out_g']],
                                 [(HEADS * SLAB + 512, BF16)], name="mix")[0]
    x1 = S['x1'] = _mm(ymix, P['wout'], name="out_proj", epi=_add_epi, extras=(x0,))
    hx = S['hx'] = _norm(x1, P['xattn_norm_g'], "norm_xattn")
    qx = S['qx'] = _mm(hx, P['w_mq'], name="mem_q", out_dtypes=(BF16,))
    mn = S['mn'] = _norm(mem, P['mem_norm_g'], "norm_mem")
    kx = S['kx'] = _mm(mn, P['w_mk'], name="mem_k", out_dtypes=(BF16,))
    vx = S['vx'] = _mm(mn, P['w_mv'], name="mem_v", out_dtypes=(BF16,))
    ox = S['ox'] = _rows_fwd(_f_xattn, [qx], [kx, vx], [(D, BF16)], name="xattn")[0]
    x2 = S['x2'] = _mm(ox, P['w_mo'], name="mem_o", epi=_add_epi, extras=(x1,))
    hm = S['hm'] = _norm(x2, P['mlp_norm_g'], "norm_mlp")
    r, s = _mm(hm, P['w_mlp1'], name="mlp_up", epi=_relu2_epi, out_dtypes=(BF16, BF16))
    S['r'], S['s'] = r, s
    x3 = _mm(s, P['w_mlp2'], name="mlp_down", epi=_add_epi, extras=(x2,))
    return x3, S


def _layer_bwd(dx3, mem, S, P, tabs):
    ck, sk, cq, sq = tabs
    G = {}
    da = _mm(dx3, P['w_mlp2'], "nt", name="mlp_down_dx", epi=_drelu2_epi, extras=(S['r'],), out_dtypes=(BF16,))
    G['w_mlp2'] = _mm(S['s'], dx3, "tn", name="mlp_down_dw")
    G['w_mlp1'] = _mm(S['hm'], da, "tn", name="mlp_up_dw")
    dhm = _mm(da, P['w_mlp1'], "nt", name="mlp_up_dx")
    dx2, G['mlp_norm_g'] = _norm_bwd(S['x2'], P['mlp_norm_g'], dhm, dx3, "norm_mlp_bwd")
    dox = _mm(dx2, P['w_mo'], "nt", name="mem_o_dx")
    G['w_mo'] = _mm(S['ox'], dx2, "tn", name="mem_o_dw")
    (dqx,), (dkx, dvx) = _rows_vjp(_f_xattn, [S['qx']], [S['kx'], S['vx']], [dox], name="xattn_bwd", drows=[0],
                                   dparams=[0, 1], drow_dtypes=[BF16])
    G['w_mq'] = _mm(S['hx'], dqx, "tn", name="mem_q_dw")
    dhx = _mm(dqx, P['w_mq'], "nt", name="mem_q_dx")
    dx1, G['xattn_norm_g'] = _norm_bwd(S['x1'], P['xattn_norm_g'], dhx, dx2, "norm_xattn_bwd")
    G['w_mk'] = _mm(S['mn'], dkx, "tn", name="mem_k_dw")
    G['w_mv'] = _mm(S['mn'], dvx, "tn", name="mem_v_dw")
    dmn = _mm(dkx, P['w_mk'], "nt", name="mem_k_dx", epi=_add_epi, extras=(_mm(dvx, P['w_mv'], "nt", name="mem_v_dx"),))
    _, G['mem_norm_g'] = _norm_bwd(mem, P['mem_norm_g'], dmn, None, "norm_mem_bwd")
    dymix = _mm(dx1, P['wout'], "nt", name="out_proj_dx")
    G['wout'] = _mm(S['ymix'], dx1, "tn", name="out_proj_dw")
    (do, dys, dh, dug_mix), (G['mla_out_g'], G['lru_out_g']) = _rows_vjp(
        _f_mix, [S['o'], S['ys'], S['h'], S['u_gate']], [P['mla_out_g'], P['lru_out_g']], [dymix], name="mix_bwd",
        drows=[0, 1, 2, 3], dparams=[0, 1], drow_dtypes=[BF16, F32, F32, F32])
    g, da_lru = _lru_scan_bwd(S['a'], dh, S['hprev'], name="lru_scan_bwd")
    lru_par = [P['lru_w_a'], P['lru_b_a'], P['lru_w_i'], P['lru_b_i'], P['lru_lambda']]
    (dc_lru,), dpar = _rows_vjp(_f_lru_gates, [S['c_lru']], lru_par, [da_lru, g], name="lru_gates_bwd", drows=[0],
                                dparams=[0, 1, 2, 3, 4], drow_dtypes=[F32])
    G['lru_w_a'], G['lru_b_a'], G['lru_w_i'], G['lru_b_i'], G['lru_lambda'] = dpar
    dc_ssm, dug_ssd, G['ssm_dt_bias'], G['ssm_a_log'], G['ssm_d'], G['ssm_norm_g'] = _ssd_bwd(
        S['c_ssm'], S['u_gate'], S['sall'], dys, P['ssm_dt_bias'], P['ssm_a_log'], P['ssm_d'], P['ssm_norm_g'],
        name="ssd_bwd")
    du_conv, G['conv_w'], G['conv_b'] = _conv_bwd(S['u_conv'], dc_ssm, dc_lru, P['conv_w'], name="conv_bwd")
    delta = _attn_delta(do, S['o'], name="attn_delta")
    dqt, dkv = _attn_bwd(S['q'], S['kv'], do, S['lse'], delta, name="attn_bwd")
    dyq = _qrope_bwd(dqt, cq, sq, name="q_rope_bwd")
    dcqn = _mm(dyq, P['wq'], "nt", name="q_proj_dx")
    G['wq'] = _mm(S['cqn'], dyq, "tn", name="q_proj_dw")
    dakv = _mm(dkv, P['wkv'], "nt", name="kv_proj_dx")
    G['wkv'] = _mm(S['akv'], dkv, "tn", name="kv_proj_dw")
    (du_mla,), (G['mla_q_norm_g'], G['mla_kv_norm_g']) = _rows_vjp(
        _f_mla_prep, [S['u_mla'], ck, sk], [P['mla_q_norm_g'], P['mla_kv_norm_g']], [dcqn, dakv], name="mla_prep_bwd",
        drows=[0], dparams=[0, 1], drow_dtypes=[BF16])
    du = jnp.concatenate([du_mla, (dug_mix + dug_ssd).astype(BF16), du_conv.astype(BF16)], axis=1)
    dh1 = _mm(du, P['win'], "nt", name="in_dx")
    G['win'] = _mm(S['h1'], du, "tn", name="in_dw")
    dx0, G['mix_norm_g'] = _norm_bwd(S['x0'], P['mix_norm_g'], dh1, dx1, "norm_mix_bwd")
    return dx0, G


def _local_step(x, mem, positions, Wk, Ws, tgt):
    tabs = _rope_tables(positions)
    saved, preps = [], []
    for l in range(DEPTH):
        P = _prep_layer(Wk, Ws, l)
        x, S = _layer_fwd(x, mem, P, tabs)
        saved.append(S)
        preps.append(P)
    loss, dx, dg_final = _loss_head(x, tgt, Ws['final_norm_g'].astype(F32).reshape(1, D), name="loss_head")
    pieces, small = [None] * DEPTH, [None] * DEPTH
    for l in reversed(range(DEPTH)):
        dx, G = _layer_bwd(dx, mem, saved[l], preps[l], tabs)
        pieces[l], small[l] = _unprep_pieces(G), _unprep_small(G)
    grads = {n: jnp.stack([small[l][n] for l in range(DEPTH)]) for n in SMALL if n != 'final_norm_g'}
    grads['final_norm_g'] = dg_final.reshape(D)
    return loss, dx, pieces, grads


def _adamw_nd(w, g, m, v, name):
    shp = w.shape
    two = lambda a: a.reshape(-1, shp[-1])
    return [r.reshape(shp) for r in _adamw(two(w), two(g), two(m), two(v), name=name)]


def kernel(x, mem, positions, mix_norm_g, w_in, mla_q_norm_g, mla_kv_norm_g, mla_w_uq, mla_w_ukv, mla_out_g, ssm_conv_w, ssm_conv_b, ssm_dt_bias, ssm_a_log, ssm_d, ssm_norm_g, lru_conv_w, lru_conv_b, lru_w_a, lru_b_a, lru_w_i, lru_b_i, lru_lambda, lru_out_g, w_out, xattn_norm_g, mem_norm_g, w_mq, w_mk, w_mv, w_mo, mlp_norm_g, w_mlp1, w_mlp2, final_norm_g, loss_target, m_mix_norm_g, m_w_in, m_mla_q_norm_g, m_mla_kv_norm_g, m_mla_w_uq, m_mla_w_ukv, m_mla_out_g, m_ssm_conv_w, m_ssm_conv_b, m_ssm_dt_bias, m_ssm_a_log, m_ssm_d, m_ssm_norm_g, m_lru_conv_w, m_lru_conv_b, m_lru_w_a, m_lru_b_a, m_lru_w_i, m_lru_b_i, m_lru_lambda, m_lru_out_g, m_w_out, m_xattn_norm_g, m_mem_norm_g, m_w_mq, m_w_mk, m_w_mv, m_w_mo, m_mlp_norm_g, m_w_mlp1, m_w_mlp2, m_final_norm_g, v_mix_norm_g, v_w_in, v_mla_q_norm_g, v_mla_kv_norm_g, v_mla_w_uq, v_mla_w_ukv, v_mla_out_g, v_ssm_conv_w, v_ssm_conv_b, v_ssm_dt_bias, v_ssm_a_log, v_ssm_d, v_ssm_norm_g, v_lru_conv_w, v_lru_conv_b, v_lru_w_a, v_lru_b_a, v_lru_w_i, v_lru_b_i, v_lru_lambda, v_lru_out_g, v_w_out, v_xattn_norm_g, v_mem_norm_g, v_w_mq, v_w_mk, v_w_mv, v_w_mo, v_mlp_norm_g, v_w_mlp1, v_w_mlp2, v_final_norm_g):
    a = locals()
    w = {n: a[n] for n in WEIGHTS}
    m = {n: a['m_' + n] for n in WEIGHTS}
    v = {n: a['v_' + n] for n in WEIGHTS}
    me = 4 * lax.axis_index("x") + 2 * lax.axis_index("y") + lax.axis_index("c")

    shard = {n: _K_FWD[n](w[PIECE_SOURCE[n]]) if n in _K_FWD else w[PIECE_SOURCE.get(n, n)] for n in PIECES}
    order = [(l, n) for l in range(DEPTH) for n in PIECES]
    packed_w = jnp.concatenate([shard[n][l].astype(BF16).reshape(-1, PACK_C) for l, n in order], axis=0)
    gathered = _all_gather(packed_w, name="gather_w")
    Wk, off = {n: [] for n in PIECES}, 0
    for l, n in order:
        rows = _piece_rows(n)
        Wk[n].append(_assemble(n, gathered[:, :, off:off + rows].reshape((2, 4) + PIECE_SHAPE[n])))
        off += rows
    Ws = {}
    conv_shapes = [w[n].shape for n in CONV_SHARDED]
    conv_g = _all_gather(_pack([w[n] for n in CONV_SHARDED], F32), name="gather_conv")
    conv_g = conv_g.transpose(1, 0, 2, 3).reshape((N_DEV,) + conv_g.shape[2:])
    for n, g in zip(CONV_SHARDED, _unpack(conv_g, conv_shapes, lead=(N_DEV,))):
        Ws[n] = g.transpose(1, 2, 0, 3).reshape(g.shape[1], g.shape[2], N_DEV * g.shape[3])
    for n in SMALL:
        if n not in CONV_SHARDED:
            Ws[n] = w[n]

    loss_share, dx, g_pieces, grads = _local_step(x[0], mem[0], positions[0], Wk, Ws, loss_target[0])
    loss = lax.psum(loss_share[0, 0], ("x", "y", "c"))

    packed_g = jnp.concatenate([_disassemble(n, g_pieces[l][n]).reshape(2, 4, -1, PACK_C) for l, n in order], axis=2)
    g_big = _reduce_scatter(packed_g, name="scatter_g")
    got, off = {n: [] for n in PIECES}, 0
    for l, n in order:
        rows = _piece_rows(n)
        got[n].append(g_big[off:off + rows].reshape(PIECE_SHAPE[n]))
        off += rows
    g_out = {}
    for n in PIECES:
        g = jnp.stack(got[n])
        g_out[PIECE_SOURCE.get(n, n)] = _K_INV[n](g) if n in _K_INV else g
    small_shapes = [grads[n].shape for n in SMALL]
    g_small = _all_reduce(_pack([grads[n] for n in SMALL], F32), name="reduce_g")
    for n, g in zip(SMALL, _unpack(g_small, small_shapes)):
        if n in CONV_SHARDED:
            cols = w[n].shape[-1]
            g = lax.dynamic_slice_in_dim(g, me * cols, cols, axis=2)
        g_out[n] = g

    delta, new_m, new_v = {}, {}, {}
    for n in BIG:
        delta[n], new_m[n], new_v[n] = _adamw_nd(w[n], g_out[n], m[n], v[n], "adamw_" + n)
    shapes = [w[n].shape for n in SMALL]
    packed = [_pack([d[n] for n in SMALL], F32) for d in (w, g_out, m, v)]
    for d, res in zip((delta, new_m, new_v), _adamw(*packed, name="adamw_small")):
        d.update(zip(SMALL, _unpack(res, shapes)))

    return (loss, dx[None], *[g_out[n] for n in WEIGHTS], *[delta[n] for n in WEIGHTS],
            *[new_m[n] for n in WEIGHTS], *[new_v[n] for n in WEIGHTS])
```

```python
import functools
import math

import jax
import jax.numpy as jnp
from jax import lax
from jax.experimental import pallas as pl
from jax.experimental.pallas import tpu as pltpu

F32, BF16 = jnp.float32, jnp.bfloat16

D = 1024
DEPTH = 4
N_MEM = 256
EPS = 1e-6
HEADS = 8
NOPE, ROPE, VDIM = 64, 32, 64
Q_RANK, KV_RANK = 256, 128
ROPE_THETA = 10000.0
SSM_CHUNK = 128
LRU_C = 8.0
MEM_HEADS = 4
D_FF = 4 * D
SLAB = 128
LR, B1, B2, AEPS, WD, STEP = 0.001, 0.9, 0.999, 1e-08, 0.01, 10

N_DEV = 8
MESH = pl.DeviceIdType.MESH

U_MLA = 640
U_GATE = 640
U_CONV = 768

_DN = {"nn": (((1,), (0,)), ((), ())), "nt": (((1,), (1,)), ((), ())), "tn": (((0,), (0,)), ((), ()))}


def _dot(a, b, kind):
    return lax.dot_general(a.astype(BF16), b.astype(BF16), _DN[kind], preferred_element_type=F32)


@functools.partial(jax.custom_vjp, nondiff_argnums=(2,))
def _bdot(a, b, kind):
    return _dot(a, b, kind)


def _bdot_fwd(a, b, kind):
    return _dot(a, b, kind), (a, b)


def _bdot_bwd(kind, res, g):
    a, b = res
    if kind == "nn":
        da, db = _dot(g, b, "nt"), _dot(a, g, "tn")
    elif kind == "nt":
        da, db = _dot(g, b, "nn"), _dot(g, a, "tn")
    else:
        da, db = _dot(b, g, "nt"), _dot(a, g, "nn")
    return da.astype(a.dtype), db.astype(b.dtype)


_bdot.defvjp(_bdot_fwd, _bdot_bwd)


def _tile(n, pref):
    if n <= pref:
        return n
    t = pref
    while n % t:
        t -= SLAB
    return t


def _cparams(sem, vmem_mb=48):
    return pltpu.CompilerParams(dimension_semantics=sem, vmem_limit_bytes=vmem_mb * 1024 * 1024)


def _mm(a, b, kind="nn", *, name, out_dtypes=(F32,), epi=None, extras=(), tm=1024, tn=1024, tk=1024):
    if kind == "tn":
        K, M = a.shape
    else:
        M, K = a.shape
    N = b.shape[0] if kind == "nt" else b.shape[1]
    if a.dtype == F32 or b.dtype == F32:
        tk = tk // 2
    tm, tn, tk = _tile(M, tm), _tile(N, tn), _tile(K, tk)
    nk = K // tk
    a_spec = pl.BlockSpec((tk, tm), lambda i, j, k: (k, i)) if kind == "tn" else pl.BlockSpec((tm, tk), lambda i, j, k: (i, k))
    b_spec = pl.BlockSpec((tn, tk), lambda i, j, k: (j, k)) if kind == "nt" else pl.BlockSpec((tk, tn), lambda i, j, k: (k, j))
    o_spec = pl.BlockSpec((tm, tn), lambda i, j, k: (i, j))
    n_ex, n_out = len(extras), len(out_dtypes)

    def body(*refs):
        a_ref, b_ref = refs[:2]
        ex = refs[2:2 + n_ex]
        outs = refs[2 + n_ex:2 + n_ex + n_out]
        acc = refs[-1]
        k = pl.program_id(2)

        def finish(r):
            res = epi(r, *[e[...] for e in ex]) if epi is not None else (r,)
            for o, v in zip(outs, res):
                o[...] = v.astype(o.dtype)

        if nk == 1:
            finish(_dot(a_ref[...], b_ref[...], kind))
            return

        @pl.when(k == 0)
        def _():
            acc[...] = _dot(a_ref[...], b_ref[...], kind)

        @pl.when(k > 0)
        def _():
            acc[...] += _dot(a_ref[...], b_ref[...], kind)

        @pl.when(k == nk - 1)
        def _():
            finish(acc[...])

    res = pl.pallas_call(
        body,
        name=name,
        grid=(M // tm, N // tn, nk),
        in_specs=[a_spec, b_spec] + [o_spec] * n_ex,
        out_specs=[o_spec] * n_out,
        out_shape=[jax.ShapeDtypeStruct((M, N), dt) for dt in out_dtypes],
        scratch_shapes=[pltpu.VMEM((tm, tn), F32)],
        compiler_params=_cparams(("parallel", "parallel", "arbitrary")),
    )(a, b, *extras)
    return res[0] if n_out == 1 else res


def _row_spec(arr, tm):
    return pl.BlockSpec((tm, arr.shape[1]), lambda i: (i, 0))


def _full_spec(arr):
    nd = arr.ndim
    return pl.BlockSpec(arr.shape, lambda i: (0,) * nd)


def _rows_fwd(fn, rows, params, outs, *, name, tm=256):
    T = rows[0].shape[0]
    tm = min(tm, T)
    nr, npar = len(rows), len(params)

    def body(*refs):
        ins = [r[...] for r in refs[:nr + npar]]
        res = fn(*ins)
        for o, v in zip(refs[nr + npar:], res):
            o[...] = v.astype(o.dtype)

    res = pl.pallas_call(
        body,
        name=name,
        grid=(T // tm,),
        in_specs=[_row_spec(r, tm) for r in rows] + [_full_spec(p) for p in params],
        out_specs=[pl.BlockSpec((tm, c), lambda i: (i, 0)) for c, _ in outs],
        out_shape=[jax.ShapeDtypeStruct((T, c), dt) for c, dt in outs],
        compiler_params=_cparams(("parallel",)),
    )(*rows, *params)
    return res


def _rows_vjp(fn, rows, params, cts, *, name, drows, dparams, drow_dtypes, add=None, twin=False, tm=256):
    T = rows[0].shape[0]
    tm = min(tm, T)
    nr, npar, nct = len(rows), len(params), len(cts)
    n_add = 0 if add is None else 1
    n_dr, n_dp = len(drows), len(dparams)
    n_tw = 1 if twin else 0

    def body(*refs):
        row_t = [r[...] for r in refs[:nr]]
        par_t = [r[...] for r in refs[nr:nr + npar]]
        ct_t = [r[...].astype(F32) for r in refs[nr + npar:nr + npar + nct]]
        pos = nr + npar + nct
        add_t = refs[pos][...] if n_add else None
        pos += n_add
        drow_refs = refs[pos:pos + n_dr]
        dpar_refs = refs[pos + n_dr:pos + n_dr + n_dp]

        def g(*dargs):
            rr, pp = list(row_t), list(par_t)
            for idx, v in zip(drows, dargs[:n_dr]):
                rr[idx] = v
            for idx, v in zip(dparams, dargs[n_dr:]):
                pp[idx] = v
            return tuple(fn(*rr, *pp))

        prim = [row_t[i].astype(F32) for i in drows] + [par_t[i].astype(F32) for i in dparams]
        _, vjp = jax.vjp(g, *prim)
        grads = vjp(tuple(ct_t))
        for n, (o, v) in enumerate(zip(drow_refs, grads[:n_dr])):
            if n == 0 and n_add:
                v = v + add_t.astype(F32)
            o[...] = v.astype(o.dtype)
            if n == 0 and n_tw:
                refs[-1][...] = v.astype(BF16)

        @pl.when(pl.program_id(0) == 0)
        def _():
            for o in dpar_refs:
                o[...] = jnp.zeros_like(o)

        for o, v in zip(dpar_refs, grads[n_dr:]):
            o[...] += v

    res = pl.pallas_call(
        body,
        name=name,
        grid=(T // tm,),
        in_specs=[_row_spec(r, tm) for r in rows] + [_full_spec(p) for p in params] + [_row_spec(c, tm) for c in cts]
        + ([_row_spec(add, tm)] if n_add else []),
        out_specs=[_row_spec(rows[i], tm) for i in drows] + [_full_spec(params[i]) for i in dparams]
        + [_row_spec(rows[drows[0]], tm)] * n_tw,
        out_shape=[jax.ShapeDtypeStruct(rows[i].shape, dt) for i, dt in zip(drows, drow_dtypes)]
        + [jax.ShapeDtypeStruct(params[i].shape, F32) for i in dparams]
        + [jax.ShapeDtypeStruct(rows[drows[0]].shape, BF16)] * n_tw,
        compiler_params=_cparams(("arbitrary",)),
    )(*rows, *params, *cts, *([add] if n_add else []))
    return list(res[:n_dr]) + list(res[n_dr + n_dp:]), list(res[n_dr:n_dr + n_dp])


def _rms(x, g, n):
    return x * lax.rsqrt(jnp.sum(x * x, axis=-1, keepdims=True) * (1.0 / n) + EPS) * g


def _sigmoid(x):
    return 1.0 / (1.0 + jnp.exp(-x))


def _silu(x):
    return x * _sigmoid(x)


def _softplus(x):
    return jnp.maximum(x, 0.0) + jnp.log(1.0 + jnp.exp(-jnp.abs(x)))


def _gelu_tanh(x):
    return 0.5 * x * (1.0 + jnp.tanh(math.sqrt(2.0 / math.pi) * (x + 0.044715 * x * x * x)))


def _lane(shape):
    return lax.broadcasted_iota(jnp.int32, shape, len(shape) - 1)


def _col(x, h):
    return jnp.sum(jnp.where(_lane(x.shape) == h, x, 0.0), axis=-1, keepdims=True)


def _f_norm(x, g):
    return (_rms(x.astype(F32), g, x.shape[-1]),)


def _f_mla_prep(u, ck, sk, gq, gkv):
    u = u.astype(F32)
    cq = _rms(u[:, 0:256], gq, Q_RANK)
    ckv = _rms(u[:, 256:384], gkv, KV_RANK)
    kr = u[:, 384:512] * ck + u[:, 512:640] * sk
    return cq, jnp.concatenate([ckv, kr], axis=1)


def _f_qrope(y, cq, sq):
    y = y.astype(F32)
    c8, s8 = jnp.tile(cq, (1, HEADS)), jnp.tile(sq, (1, HEADS))
    return (y[:, :HEADS * SLAB] * c8 + y[:, HEADS * SLAB:] * s8,)


def _f_lru_gates(xc, wa, ba, wi, bi, lam):
    xc = xc.astype(F32)
    r = _sigmoid(_bdot(xc, wa, "nn") + ba)
    i = _sigmoid(_bdot(xc, wi, "nn") + bi)
    log_a = -LRU_C * r * _softplus(-lam)
    a = jnp.exp(log_a)
    x2 = 2.0 * log_a
    m1 = jnp.where(x2 > -0.02, -x2 * (1.0 + x2 * (0.5 + x2 * (1.0 / 6.0 + x2 * (1.0 / 24.0)))), 1.0 - jnp.exp(x2))
    return a, jnp.sqrt(m1) * (i * xc)


def _f_mix(o, ys, h, ug, g_mla, g_lru):
    o = o.astype(F32)
    y_mla = _rms(o, g_mla, HEADS * VDIM)
    y_lru = _rms(h.astype(F32) * _gelu_tanh(ug[:, 256:512].astype(F32)), g_lru, 256)
    return (jnp.concatenate([y_mla, ys.astype(F32), y_lru], axis=1),)


def _f_xattn(q, k, v):
    hd = D // MEM_HEADS
    outs = []
    for h in range(MEM_HEADS):
        sl = slice(h * hd, (h + 1) * hd)
        s = _bdot(q[:, sl], k[:, sl], "nt") * (1.0 / math.sqrt(hd))
        s = s - jnp.max(s, axis=-1, keepdims=True)
        p = jnp.exp(s)
        p = p / jnp.sum(p, axis=-1, keepdims=True)
        outs.append(_bdot(p, v[:, sl], "nn"))
    return (jnp.concatenate(outs, axis=1),)


def _split_dot(tri, a, kind):
    a_hi = a.astype(BF16)
    r1 = a - a_hi.astype(F32)
    a_mid = r1.astype(BF16)
    a_lo = (r1 - a_mid.astype(F32)).astype(BF16)
    return _dot(tri, a_hi, kind) + _dot(tri, a_mid, kind) + _dot(tri, a_lo, kind)


@jax.custom_vjp
def _tri_cumsum(tri, a):
    return _split_dot(tri, a, "nn")


def _tri_cumsum_fwd(tri, a):
    return _split_dot(tri, a, "nn"), tri


def _tri_cumsum_bwd(tri, g):
    return jnp.zeros_like(tri), _split_dot(tri, g, "tn")


_tri_cumsum.defvjp(_tri_cumsum_fwd, _tri_cumsum_bwd)


def _f_ssd_chunk(c, ug, s0, s1, dtb, alog, dsk, ng):
    L = c.shape[0]
    c = c.astype(F32)
    xbc = _silu(c)
    xs, bm, cm = xbc[:, 0:256], xbc[:, 256:384], xbc[:, 384:512]
    z = ug[:, 0:256].astype(F32)
    dt = _softplus(ug[:, 512:640].astype(F32) + dtb)
    a = dt * (-jnp.exp(alog))
    rowi = lax.broadcasted_iota(jnp.int32, (L, L), 0)
    coli = lax.broadcasted_iota(jnp.int32, (L, L), 1)
    tril = rowi >= coli
    acum = _tri_cumsum(tril.astype(BF16), a)
    acum_t = acum.T
    lane = _lane((1, SLAB))
    lo = lane < 64
    ys, new_s = [], []
    for g in range(2):
        gm = (lane >= 64 * g) & (lane < 64 * g + 64)
        bg, cg = jnp.where(gm, bm, 0.0), jnp.where(gm, cm, 0.0)
        cb = _bdot(cg, bg, "nt")
        x = xs[:, SLAB * g:SLAB * (g + 1)]
        h0, h1 = 2 * g, 2 * g + 1
        ac0, ac1 = _col(acum, h0), _col(acum, h1)
        xdt = x * jnp.where(lo, _col(dt, h0), _col(dt, h1))
        ac_l = jnp.where(lo, ac0, ac1)
        tot = acum[L - 1:L, :]
        tot_l = jnp.where(lo, _col(tot, h0), _col(tot, h1))
        yd = jnp.zeros((L, SLAB), F32)
        for hh, acc, hm in ((h0, ac0, lo), (h1, ac1, jnp.logical_not(lo))):
            seg = acc - acum_t[hh:hh + 1, :]
            lm = jnp.where(tril, jnp.exp(jnp.where(tril, seg, 0.0)), 0.0)
            yd = yd + _bdot(cb * lm, jnp.where(hm, xdt, 0.0), "nn")
        sg = (s0, s1)[g]
        y_off = _bdot(cg, sg, "nn") * jnp.exp(ac_l)
        st = _bdot(bg, xdt * jnp.exp(tot_l - ac_l), "tn")
        new_s.append(jnp.exp(tot_l) * sg + st)
        y = yd + y_off + jnp.where(lo, _col(dsk, h0), _col(dsk, h1)) * x
        y = y * _silu(z[:, SLAB * g:SLAB * (g + 1)])
        ys.append(_rms(y, ng[:, SLAB * g:SLAB * (g + 1)], SLAB))
    return jnp.concatenate(ys, axis=1), new_s[0], new_s[1]


_SSD_TILE = 512


def _ssd_fwd(c, ug, dtb, alog, dsk, ng, *, name):
    T = c.shape[0]
    tm = min(_SSD_TILE, T)
    ncs = tm // SSM_CHUNK
    nc = T // SSM_CHUNK

    def body(c_ref, ug_ref, dtb_ref, alog_ref, dsk_ref, ng_ref, y_ref, sall_ref, s_scr):
        @pl.when(pl.program_id(0) == 0)
        def _():
            s_scr[...] = jnp.zeros_like(s_scr)

        s0, s1 = s_scr[0], s_scr[1]
        for k in range(ncs):
            rows = slice(k * SSM_CHUNK, (k + 1) * SSM_CHUNK)
            sall_ref[k, 0] = s0
            sall_ref[k, 1] = s1
            y, s0, s1 = _f_ssd_chunk(c_ref[rows, :], ug_ref[rows, :], s0, s1, dtb_ref[...], alog_ref[...],
                                     dsk_ref[...], ng_ref[...])
            y_ref[rows, :] = y
        s_scr[0] = s0
        s_scr[1] = s1

    y, sall = pl.pallas_call(
        body,
        name=name,
        grid=(T // tm,),
        in_specs=[_row_spec(c, tm), _row_spec(ug, tm)] + [_full_spec(p) for p in (dtb, alog, dsk, ng)],
        out_specs=[pl.BlockSpec((tm, 256), lambda i: (i, 0)), pl.BlockSpec((ncs, 2, SLAB, SLAB), lambda i: (i, 0, 0, 0))],
        out_shape=[jax.ShapeDtypeStruct((T, 256), F32), jax.ShapeDtypeStruct((nc, 2, SLAB, SLAB), F32)],
        scratch_shapes=[pltpu.VMEM((2, SLAB, SLAB), F32)],
        compiler_params=_cparams(("arbitrary",)),
    )(c, ug, dtb, alog, dsk, ng)
    return y, sall


def _ssd_bwd(c, ug, sall, dy, dtb, alog, dsk, ng, *, name):
    T = c.shape[0]
    tm = min(_SSD_TILE, T)
    ncs = tm // SSM_CHUNK
    nt = T // tm

    def body(c_ref, ug_ref, sall_ref, dy_ref, dtb_ref, alog_ref, dsk_ref, ng_ref,
             dc_ref, dug_ref, ddtb_ref, dalog_ref, ddsk_ref, dng_ref, ds_scr):
        @pl.when(pl.program_id(0) == 0)
        def _():
            ds_scr[...] = jnp.zeros_like(ds_scr)
            for o in (ddtb_ref, dalog_ref, ddsk_ref, dng_ref):
                o[...] = jnp.zeros_like(o)

        ds0, ds1 = ds_scr[0], ds_scr[1]
        for k in reversed(range(ncs)):
            rows = slice(k * SSM_CHUNK, (k + 1) * SSM_CHUNK)
            prim = (c_ref[rows, :].astype(F32), ug_ref[rows, :].astype(F32), sall_ref[k, 0], sall_ref[k, 1],
                    dtb_ref[...], alog_ref[...], dsk_ref[...], ng_ref[...])
            _, vjp = jax.vjp(_f_ssd_chunk, *prim)
            dc, dug, ds0, ds1, g_dtb, g_alog, g_dsk, g_ng = vjp((dy_ref[rows, :].astype(F32), ds0, ds1))
            dc_ref[rows, :] = dc
            dug_ref[rows, :] = dug
            ddtb_ref[...] += g_dtb
            dalog_ref[...] += g_alog
            ddsk_ref[...] += g_dsk
            dng_ref[...] += g_ng
        ds_scr[0] = ds0
        ds_scr[1] = ds1

    rev = lambda i: (nt - 1 - i, 0)
    params = (dtb, alog, dsk, ng)
    res = pl.pallas_call(
        body,
        name=name,
        grid=(nt,),
        in_specs=[pl.BlockSpec((tm, c.shape[1]), rev), pl.BlockSpec((tm, ug.shape[1]), rev),
                  pl.BlockSpec((ncs, 2, SLAB, SLAB), lambda i: (nt - 1 - i, 0, 0, 0)), pl.BlockSpec((tm, 256), rev)]
        + [_full_spec(p) for p in params],
        out_specs=[pl.BlockSpec((tm, 512), rev), pl.BlockSpec((tm, U_GATE), rev)] + [_full_spec(p) for p in params],
        out_shape=[jax.ShapeDtypeStruct((T, 512), F32), jax.ShapeDtypeStruct((T, U_GATE), F32)]
        + [jax.ShapeDtypeStruct(p.shape, F32) for p in params],
        scratch_shapes=[pltpu.VMEM((2, SLAB, SLAB), F32)],
        compiler_params=_cparams(("arbitrary",)),
    )(c, ug, sall, dy, *params)
    return res


_CONV_TILE = 512
_HALO = 8
_CONV_W = 4


def _conv_fwd(u, w, b, *, name):
    T, C = u.shape
    tm = min(_CONV_TILE, T)
    hb = tm // _HALO

    def body(u_ref, prev_ref, w_ref, b_ref, y1_ref, y2_ref, ext):
        i = pl.program_id(0)
        ext[0:_HALO, :] = jnp.where(i > 0, prev_ref[...], 0.0)
        ext[_HALO:, :] = u_ref[...]
        y = jnp.broadcast_to(b_ref[...], (tm, C))
        for k in range(_CONV_W):
            y = y + ext[_HALO - (_CONV_W - 1) + k:_HALO - (_CONV_W - 1) + k + tm, :] * w_ref[k:k + 1, :]
        y1_ref[...] = y[:, 0:512]
        y2_ref[...] = y[:, 512:768]

    return pl.pallas_call(
        body,
        name=name,
        grid=(T // tm,),
        in_specs=[_row_spec(u, tm), pl.BlockSpec((_HALO, C), lambda i: (jnp.maximum(i * hb - 1, 0), 0)),
                  _full_spec(w), _full_spec(b)],
        out_specs=[pl.BlockSpec((tm, 512), lambda i: (i, 0)), pl.BlockSpec((tm, 256), lambda i: (i, 0))],
        out_shape=[jax.ShapeDtypeStruct((T, 512), F32), jax.ShapeDtypeStruct((T, 256), F32)],
        scratch_shapes=[pltpu.VMEM((tm + _HALO, C), F32)],
        compiler_params=_cparams(("parallel",)),
    )(u, u, w, b)


def _conv_bwd(u, dy1, dy2, w, *, name):
    T, C = u.shape
    tm = min(_CONV_TILE, T)
    hb = tm // _HALO
    nt = T // tm

    def body(u_ref, prev_ref, dy1_ref, next1_ref, dy2_ref, next2_ref, w_ref, du_ref, dw_ref, db_ref, ext, dext):
        i = pl.program_id(0)
        ext[0:_HALO, :] = jnp.where(i > 0, prev_ref[...], 0.0)
        ext[_HALO:, :] = u_ref[...]
        dext[0:tm, 0:512] = dy1_ref[...]
        dext[0:tm, 512:768] = dy2_ref[...]
        dext[tm:, 0:512] = jnp.where(i < nt - 1, next1_ref[...], 0.0)
        dext[tm:, 512:768] = jnp.where(i < nt - 1, next2_ref[...], 0.0)

        @pl.when(i == 0)
        def _():
            dw_ref[...] = jnp.zeros_like(dw_ref)
            db_ref[...] = jnp.zeros_like(db_ref)

        dy = dext[0:tm, :]
        du = jnp.zeros((tm, C), F32)
        for k in range(_CONV_W):
            du = du + dext[_CONV_W - 1 - k:_CONV_W - 1 - k + tm, :] * w_ref[k:k + 1, :]
            xk = ext[_HALO - (_CONV_W - 1) + k:_HALO - (_CONV_W - 1) + k + tm, :]
            dw_ref[k:k + 1, :] += jnp.sum(dy * xk, axis=0, keepdims=True)
        du_ref[...] = du
        db_ref[...] += jnp.sum(dy, axis=0, keepdims=True)

    nxt = lambda i: (jnp.minimum((i + 1) * hb, T // _HALO - 1), 0)
    return pl.pallas_call(
        body,
        name=name,
        grid=(nt,),
        in_specs=[_row_spec(u, tm), pl.BlockSpec((_HALO, C), lambda i: (jnp.maximum(i * hb - 1, 0), 0)),
                  _row_spec(dy1, tm), pl.BlockSpec((_HALO, 512), nxt), _row_spec(dy2, tm), pl.BlockSpec((_HALO, 256), nxt),
                  _full_spec(w)],
        out_specs=[pl.BlockSpec((tm, C), lambda i: (i, 0)), _full_spec(w), pl.BlockSpec((1, C), lambda i: (0, 0))],
        out_shape=[jax.ShapeDtypeStruct((T, C), F32), jax.ShapeDtypeStruct(w.shape, F32), jax.ShapeDtypeStruct((1, C), F32)],
        scratch_shapes=[pltpu.VMEM((tm + _HALO, C), F32), pltpu.VMEM((tm + _HALO, C), F32)],
        compiler_params=_cparams(("arbitrary",)),
    )(u, u, dy1, dy1, dy2, dy2, w)


_SCAN_TILE = 1024
_SUB = 8


def _shift_rows(x, d, fill, up):
    r = lax.broadcasted_iota(jnp.int32, x.shape, 0)
    if up:
        return jnp.where(r < _SUB - d, pltpu.roll(x, _SUB - d, 0), fill)
    return jnp.where(r >= d, pltpu.roll(x, d, 0), fill)


def _lru_scan_fwd(a, b, *, name):
    T, W = a.shape
    tr = min(_SCAN_TILE, T)

    def body(a_ref, b_ref, h_ref, hp_ref, carry):
        @pl.when(pl.program_id(0) == 0)
        def _():
            carry[...] = jnp.zeros_like(carry)

        def step(t, cr):
            rows = pl.ds(pl.multiple_of(t * _SUB, _SUB), _SUB)
            aa, bb = a_ref[rows, :], b_ref[rows, :]
            for d in (1, 2, 4):
                bb = bb + aa * _shift_rows(bb, d, 0.0, False)
                aa = aa * _shift_rows(aa, d, 1.0, False)
            h = bb + aa * cr
            h_ref[rows, :] = h
            r = lax.broadcasted_iota(jnp.int32, h.shape, 0)
            hp_ref[rows, :] = jnp.where(r >= 1, pltpu.roll(h, 1, 0), cr)
            return jnp.broadcast_to(h[_SUB - 1:_SUB, :], (_SUB, W))

        carry[...] = lax.fori_loop(0, tr // _SUB, step, carry[...])

    return pl.pallas_call(
        body,
        name=name,
        grid=(T // tr,),
        in_specs=[_row_spec(a, tr), _row_spec(b, tr)],
        out_specs=[pl.BlockSpec((tr, W), lambda i: (i, 0))] * 2,
        out_shape=[jax.ShapeDtypeStruct((T, W), F32)] * 2,
        scratch_shapes=[pltpu.VMEM((_SUB, W), F32)],
        compiler_params=_cparams(("arbitrary",)),
    )(a, b)


def _lru_scan_bwd(a, dh, hprev, *, name):
    T, W = a.shape
    tr = min(_SCAN_TILE, T)
    nt = T // tr

    def body(a_ref, dh_ref, hp_ref, g_ref, da_ref, carry):
        @pl.when(pl.program_id(0) == 0)
        def _():
            carry[...] = jnp.zeros_like(carry)

        nsub = tr // _SUB

        def step(s, cr):
            t = nsub - 1 - s
            rows = pl.ds(pl.multiple_of(t * _SUB, _SUB), _SUB)
            a_t = a_ref[rows, :]
            aa = _shift_rows(a_t, 1, 1.0, True)
            bb = dh_ref[rows, :]
            for d in (1, 2, 4):
                bb = bb + aa * _shift_rows(bb, d, 0.0, True)
                aa = aa * _shift_rows(aa, d, 1.0, True)
            g = bb + aa * cr
            g_ref[rows, :] = g
            da_ref[rows, :] = g * hp_ref[rows, :]
            return jnp.broadcast_to(a_t[0:1, :] * g[0:1, :], (_SUB, W))

        carry[...] = lax.fori_loop(0, nsub, step, carry[...])

    rev = lambda i: (nt - 1 - i, 0)
    return pl.pallas_call(
        body,
        name=name,
        grid=(nt,),
        in_specs=[pl.BlockSpec((tr, W), rev)] * 3,
        out_specs=[pl.BlockSpec((tr, W), rev)] * 2,
        out_shape=[jax.ShapeDtypeStruct((T, W), F32)] * 2,
        scratch_shapes=[pltpu.VMEM((_SUB, W), F32)],
        compiler_params=_cparams(("arbitrary",)),
    )(a, dh, hprev)


_ATT_BLK = 512
_ATT_QPARTS = 2
_ATT_SCALE = 1.0 / math.sqrt(NOPE + ROPE)


def _attn_fwd(q, kv, *, name):
    T = q.shape[0]
    blk = min(_ATT_BLK, T)
    nq = T // blk
    parts = _ATT_QPARTS if nq % _ATT_QPARTS == 0 else 1

    def body(q_ref, kv_ref, o_ref, lse_ref):
        i = pl.program_id(1)
        diag = lax.broadcasted_iota(jnp.int32, (blk, blk), 1) <= lax.broadcasted_iota(jnp.int32, (blk, blk), 0)

        def one(part, kb, vb, carry, masked):
            m, l, acc = carry
            s = _dot(q_ref[part * blk:(part + 1) * blk, :], kb, "nt") * _ATT_SCALE
            if masked:
                s = jnp.where(diag, s, -jnp.inf)
            m_new = jnp.maximum(m, jnp.max(s, axis=-1, keepdims=True))
            alpha = jnp.exp(m - m_new)
            p = jnp.exp(s - m_new)
            l = alpha * l + jnp.sum(p, axis=-1, keepdims=True)
            acc = alpha * acc + _dot(p, vb, "nn")
            return m_new, l, acc

        def kv_block(j):
            rows = pl.ds(pl.multiple_of(j * blk, blk), blk)
            return kv_ref[rows, 0:SLAB], kv_ref[rows, SLAB:2 * SLAB]

        def step(j, carry):
            kb, vb = kv_block(j)
            return tuple(one(part, kb, vb, carry[part], False) for part in range(parts))

        init = ((jnp.full((blk, 1), -jnp.inf, F32), jnp.zeros((blk, 1), F32), jnp.zeros((blk, SLAB), F32)),) * parts
        carry = lax.fori_loop(0, i * parts, step, init)
        for part in range(parts):
            c = carry[part]
            for kk in range(part + 1):
                c = one(part, *kv_block(i * parts + kk), c, kk == part)
            m, l, acc = c
            o_ref[part * blk:(part + 1) * blk, :] = acc / l
            lse_ref[part] = jnp.broadcast_to(m + jnp.log(l), (blk, SLAB)).T[0:_SUB, :]

    return pl.pallas_call(
        body,
        name=name,
        grid=(HEADS, nq // parts),
        in_specs=[pl.BlockSpec((parts * blk, SLAB), lambda h, i: (i, h)), pl.BlockSpec((T, 2 * SLAB), lambda h, i: (0, h))],
        out_specs=[pl.BlockSpec((parts * blk, SLAB), lambda h, i: (i, h)),
                   pl.BlockSpec((None, parts, _SUB, blk), lambda h, i: (h, i, 0, 0))],
        out_shape=[jax.ShapeDtypeStruct((T, HEADS * SLAB), F32), jax.ShapeDtypeStruct((HEADS, nq, _SUB, blk), F32)],
        compiler_params=_cparams(("parallel", "arbitrary")),
    )(q, kv)


def _attn_delta(do, o, *, name):
    T = o.shape[0]
    blk = min(_ATT_BLK, T)
    nq = T // blk

    def body(do_ref, o_ref, d_ref):
        for h in range(HEADS):
            cols = slice(h * SLAB, (h + 1) * SLAB)
            dl = jnp.sum(do_ref[:, cols].astype(F32) * o_ref[:, cols], axis=-1, keepdims=True)
            d_ref[h] = jnp.broadcast_to(dl, (blk, SLAB)).T[0:_SUB, :]

    return pl.pallas_call(
        body,
        name=name,
        grid=(nq,),
        in_specs=[pl.BlockSpec((blk, HEADS * SLAB), lambda i: (i, 0))] * 2,
        out_specs=pl.BlockSpec((HEADS, None, _SUB, blk), lambda i: (0, i, 0, 0)),
        out_shape=jax.ShapeDtypeStruct((HEADS, nq, _SUB, blk), F32),
        compiler_params=_cparams(("parallel",)),
    )(do, o)


def _attn_bwd(q, kv, do, lse, delta, *, name):
    T = q.shape[0]
    blk = min(_ATT_BLK, T)
    nq = T // blk

    def body(q_ref, kv_ref, do_ref, lse_ref, dl_ref, dqt_ref, dkv_ref):
        j = pl.program_id(1)

        @pl.when(j == 0)
        def _():
            dqt_ref[...] = jnp.zeros_like(dqt_ref)

        kb, vb = kv_ref[:, 0:SLAB], kv_ref[:, SLAB:2 * SLAB]
        kbt = kb.astype(F32).T.astype(BF16)
        kpos = j * blk + lax.broadcasted_iota(jnp.int32, (blk, blk), 0)

        def step(i, carry, masked):
            dk, dv = carry
            rows = pl.ds(pl.multiple_of(i * blk, blk), blk)
            qb, dob = q_ref[rows, :], do_ref[rows, :]
            st = _dot(kb, qb, "nt") * _ATT_SCALE
            if masked:
                qpos = i * blk + lax.broadcasted_iota(jnp.int32, (blk, blk), 1)
                st = jnp.where(kpos <= qpos, st, -jnp.inf)
            pt = jnp.exp(st - lse_ref[i, 0:1, :])
            dpt = _dot(vb, dob, "nt")
            dst = pt * (dpt - dl_ref[i, 0:1, :]) * _ATT_SCALE
            dv = dv + _dot(pt, dob, "nn")
            dk = dk + _dot(dst, qb, "nn")
            dqt_ref[i] += _dot(kbt, dst, "nn")
            return dk, dv

        zero = jnp.zeros((blk, SLAB), F32)
        carry = step(j, (zero, zero), True)
        dk, dv = lax.fori_loop(j + 1, nq, functools.partial(step, masked=False), carry)
        dkv_ref[:, 0:SLAB] = dk.astype(BF16)
        dkv_ref[:, SLAB:2 * SLAB] = dv.astype(BF16)

    stat_spec = pl.BlockSpec((None, nq, _SUB, blk), lambda h, j: (h, 0, 0, 0))
    return pl.pallas_call(
        body,
        name=name,
        grid=(HEADS, nq),
        in_specs=[pl.BlockSpec((T, SLAB), lambda h, j: (0, h)), pl.BlockSpec((blk, 2 * SLAB), lambda h, j: (j, h)),
                  pl.BlockSpec((T, SLAB), lambda h, j: (0, h)), stat_spec, stat_spec],
        out_specs=[pl.BlockSpec((None, nq, SLAB, blk), lambda h, j: (h, 0, 0, 0)),
                   pl.BlockSpec((blk, 2 * SLAB), lambda h, j: (j, h))],
        out_shape=[jax.ShapeDtypeStruct((HEADS, nq, SLAB, blk), F32), jax.ShapeDtypeStruct((T, HEADS * 2 * SLAB), BF16)],
        compiler_params=_cparams(("parallel", "arbitrary")),
    )(q, kv, do, lse, delta)


def _qrope_bwd(dqt, cq, sq, *, name):
    _, nq, _, blk = dqt.shape
    T = nq * blk

    def body(dqt_ref, c_ref, s_ref, dy_ref):
        c, s = c_ref[...], s_ref[...]
        for h in range(HEADS):
            dq = dqt_ref[h].T
            dy_ref[:, h * SLAB:(h + 1) * SLAB] = (dq * c).astype(BF16)
            dy_ref[:, (HEADS + h) * SLAB:(HEADS + h + 1) * SLAB] = (dq * s).astype(BF16)

    return pl.pallas_call(
        body,
        name=name,
        grid=(nq,),
        in_specs=[pl.BlockSpec((HEADS, None, SLAB, blk), lambda i: (0, i, 0, 0)), _row_spec(cq, blk), _row_spec(sq, blk)],
        out_specs=pl.BlockSpec((blk, 2 * HEADS * SLAB), lambda i: (i, 0)),
        out_shape=jax.ShapeDtypeStruct((T, 2 * HEADS * SLAB), BF16),
        compiler_params=_cparams(("parallel",)),
    )(dqt, cq, sq)


def _loss_head(x, tgt, g, *, name, tm=256):
    T = x.shape[0]
    tm = min(tm, T)

    def body(x_ref, t_ref, g_ref, loss_ref, dx_ref, dxh_ref, dg_ref):
        def f(xv, gv):
            e = _rms(xv, gv, D) - t_ref[...]
            row = jnp.sum(e * e, axis=1, keepdims=True)
            return jnp.sum(row, axis=0, keepdims=True) * (0.5 / D)

        val, vjp = jax.vjp(f, x_ref[...], g_ref[...])
        dxv, dgv = vjp(jnp.ones((1, 1), F32))

        @pl.when(pl.program_id(0) == 0)
        def _():
            loss_ref[...] = jnp.zeros_like(loss_ref)
            dg_ref[...] = jnp.zeros_like(dg_ref)

        dx_ref[...] = dxv
        dxh_ref[...] = dxv.astype(BF16)
        dg_ref[...] += dgv
        loss_ref[...] += jnp.broadcast_to(val, loss_ref.shape)

    return pl.pallas_call(
        body,
        name=name,
        grid=(T // tm,),
        in_specs=[_row_spec(x, tm), _row_spec(tgt, tm), _full_spec(g)],
        out_specs=[pl.BlockSpec((1, SLAB), lambda i: (0, 0)), _row_spec(x, tm), _row_spec(x, tm), _full_spec(g)],
        out_shape=[jax.ShapeDtypeStruct((1, SLAB), F32), jax.ShapeDtypeStruct(x.shape, F32),
                   jax.ShapeDtypeStruct(x.shape, BF16), jax.ShapeDtypeStruct(g.shape, F32)],
        compiler_params=_cparams(("arbitrary",)),
    )(x, tgt, g)


def _row_tile(rows, cols, budget=256 * 1024):
    best = None
    for t in range(16, rows + 1, 16):
        if rows % t == 0 and t * cols <= budget:
            best = t
    return best or rows


def _sum_fixed(x, out_dtype, *, name):
    n, R, C = x.shape
    tr = _row_tile(R, C)

    def body(x_ref, o_ref):
        acc = x_ref[0].astype(F32)
        for k in range(1, n):
            acc = acc + x_ref[k].astype(F32)
        o_ref[...] = acc.astype(o_ref.dtype)

    return pl.pallas_call(
        body,
        name=name,
        grid=(R // tr,),
        in_specs=[pl.BlockSpec((n, tr, C), lambda i: (0, i, 0))],
        out_specs=pl.BlockSpec((tr, C), lambda i: (i, 0)),
        out_shape=jax.ShapeDtypeStruct((R, C), out_dtype),
        compiler_params=_cparams(("parallel",)),
    )(x)


def _adamw(w, g, m, v, *, name):
    R, C = w.shape
    tr = _row_tile(R, C, 128 * 1024)

    def body(w_ref, g_ref, m_ref, v_ref, d_ref, nm_ref, nv_ref):
        gv = g_ref[...]
        mv = B1 * m_ref[...] + (1.0 - B1) * gv
        vv = B2 * v_ref[...] + (1.0 - B2) * (gv * gv)
        m_hat = mv / (1.0 - B1 ** STEP)
        v_hat = vv / (1.0 - B2 ** STEP)
        d_ref[...] = -LR * (m_hat / (jnp.sqrt(v_hat) + AEPS) + WD * w_ref[...])
        nm_ref[...] = mv
        nv_ref[...] = vv

    spec = pl.BlockSpec((tr, C), lambda i: (i, 0))
    return pl.pallas_call(
        body, name=name, grid=(R // tr,), in_specs=[spec] * 4, out_specs=[spec] * 3,
        out_shape=[jax.ShapeDtypeStruct((R, C), F32)] * 3, compiler_params=_cparams(("parallel",)),
    )(w, g, m, v)


_FLIPS = ((1, 0), (0, 1), (1, 1))
_ANY = pl.BlockSpec(memory_space=pl.ANY)


def _me():
    return lax.axis_index("x"), lax.axis_index("y"), lax.axis_index("c")


def _flip(mx, my, f):
    return (1 - mx if f[0] else mx), (1 - my if f[1] else my)


def _chip_exchange(x, gather, *, name):
    shape = x.shape if not gather else (4,) + x.shape

    def body(x_ref, out_ref, send_sems, recv_sems, local_sem):
        mx, my, mc = _me()
        mine = 2 * mx + my
        src_own = x_ref if gather else x_ref.at[mine]
        if not gather:
            local = pltpu.make_async_copy(src_own, out_ref.at[mine], local_sem)
            local.start()
        sends = []
        for k, f in enumerate(_FLIPS):
            px, py = _flip(mx, my, f)
            src = x_ref if gather else x_ref.at[2 * px + py]
            cp = pltpu.make_async_remote_copy(src_ref=src, dst_ref=out_ref.at[mine], send_sem=send_sems.at[k],
                                              recv_sem=recv_sems.at[k], device_id=(px, py, mc), device_id_type=MESH)
            cp.start()
            sends.append(cp)
        for k, f in enumerate(_FLIPS):
            px, py = _flip(mx, my, f)
            pltpu.make_async_remote_copy(src_ref=src_own, dst_ref=out_ref.at[2 * px + py], send_sem=send_sems.at[k],
                                         recv_sem=recv_sems.at[k], device_id=(px, py, mc), device_id_type=MESH).wait_recv()
        for cp in sends:
            cp.wait_send()
        if not gather:
            local.wait()

    return pl.pallas_call(
        body, name=name, in_specs=[_ANY], out_specs=_ANY, out_shape=jax.ShapeDtypeStruct(shape, x.dtype),
        scratch_shapes=[pltpu.SemaphoreType.DMA((3,)), pltpu.SemaphoreType.DMA((3,)), pltpu.SemaphoreType.DMA],
    )(x)


def _core_exchange(x, half, *, name):
    shape = x.shape[1:] if half else x.shape

    def body(x_ref, out_ref, send_sem, recv_sem):
        mx, my, mc = _me()
        src = x_ref.at[1 - mc] if half else x_ref
        cp = pltpu.make_async_remote_copy(src_ref=src, dst_ref=out_ref, send_sem=send_sem, recv_sem=recv_sem,
                                          device_id=(mx, my, 1 - mc), device_id_type=MESH)
        cp.start()
        cp.wait()

    return pl.pallas_call(
        body, name=name, in_specs=[_ANY], out_specs=_ANY, out_shape=jax.ShapeDtypeStruct(shape, x.dtype),
        scratch_shapes=[pltpu.SemaphoreType.DMA, pltpu.SemaphoreType.DMA],
    )(x)


def _all_gather(x, *, name):
    mx, my, mc = _me()
    g4 = _chip_exchange(x, True, name=name + "_chips")
    own = (jnp.arange(4) == 2 * mx + my).reshape((4,) + (1,) * x.ndim)
    g4 = jnp.where(own, x[None], g4)
    sib = _core_exchange(g4, False, name=name + "_cores")
    return jnp.where(mc == 0, jnp.stack([g4, sib]), jnp.stack([sib, g4]))


def _add_own_half(x, got, *, name):
    _, R, C = x.shape
    tr = _row_tile(R, C)

    def body(c_ref, x_ref, g_ref, o_ref):
        o_ref[...] = (x_ref[...] + g_ref[...]).astype(o_ref.dtype)

    return pl.pallas_call(
        body,
        name=name,
        grid_spec=pltpu.PrefetchScalarGridSpec(
            num_scalar_prefetch=1, grid=(R // tr,),
            in_specs=[pl.BlockSpec((None, tr, C), lambda i, c: (c[0], i, 0)), pl.BlockSpec((tr, C), lambda i, c: (i, 0))],
            out_specs=pl.BlockSpec((tr, C), lambda i, c: (i, 0))),
        out_shape=jax.ShapeDtypeStruct((R, C), BF16),
        compiler_params=_cparams(("parallel",)),
    )(lax.axis_index("c").astype(jnp.int32).reshape(1), x, got)


def _reduce_scatter(x, *, name):
    _, _, R, C = x.shape
    got = _core_exchange(x, True, name=name + "_cores")
    t = _add_own_half(x.reshape(2, 4 * R, C), got.reshape(4 * R, C), name=name + "_add").reshape(4, R, C)
    parts = _chip_exchange(t, False, name=name + "_chips")
    return _sum_fixed(parts, F32, name=name + "_sum")


def _all_reduce(x, *, name):
    g = _all_gather(x, name=name)
    return _sum_fixed(g.reshape((N_DEV,) + x.shape), F32, name=name + "_sum")


WEIGHTS = ['mix_norm_g', 'w_in', 'mla_q_norm_g', 'mla_kv_norm_g', 'mla_w_uq', 'mla_w_ukv', 'mla_out_g', 'ssm_conv_w',
           'ssm_conv_b', 'ssm_dt_bias', 'ssm_a_log', 'ssm_d', 'ssm_norm_g', 'lru_conv_w', 'lru_conv_b', 'lru_w_a',
           'lru_b_a', 'lru_w_i', 'lru_b_i', 'lru_lambda', 'lru_out_g', 'w_out', 'xattn_norm_g', 'mem_norm_g', 'w_mq',
           'w_mk', 'w_mv', 'w_mo', 'mlp_norm_g', 'w_mlp1', 'w_mlp2', 'final_norm_g']
ROW_SHARDED = ('w_in', 'w_out', 'w_mq', 'w_mk', 'w_mv', 'w_mo', 'w_mlp2')
COL_SHARDED = ('mla_w_uq', 'mla_w_ukv', 'w_mlp1')
BIG = tuple(n for n in WEIGHTS if n in ROW_SHARDED + COL_SHARDED)
CONV_SHARDED = ('ssm_conv_w', 'lru_conv_w')
SMALL = tuple(n for n in WEIGHTS if n not in BIG)
PACK_C = 1024


def _pack(arrs, dtype, lead=()):
    flat = jnp.concatenate([a.reshape(lead + (-1,)).astype(dtype) for a in arrs], axis=-1)
    n = flat.shape[-1]
    rows = -(-n // (16 * PACK_C)) * 16
    flat = jnp.pad(flat, [(0, 0)] * len(lead) + [(0, rows * PACK_C - n)])
    return flat.reshape(lead + (rows, PACK_C))


def _unpack(packed, shapes, lead=()):
    flat = packed.reshape(lead + (-1,))
    out, off = [], 0
    for s in shapes:
        n = math.prod(s)
        out.append(flat[..., off:off + n].reshape(lead + tuple(s)))
        off += n
    return out


def _pad_lanes(v, n=SLAB):
    return jnp.pad(v.astype(F32), (0, n - v.shape[0])).reshape(1, n)


PIECES = ('win', 'wq', 'wkv', 'wout', 'w_mq', 'w_mk', 'w_mv', 'w_mo', 'w_mlp1', 'w_mlp2')
PIECE_SOURCE = {'win': 'w_in', 'wq': 'mla_w_uq', 'wkv': 'mla_w_ukv', 'wout': 'w_out'}
PIECE_SHAPE = {'win': (128, 2048), 'wq': (Q_RANK, 2 * SLAB), 'wkv': (KV_RANK, 2 * SLAB), 'wout': (128, D),
               'w_mq': (128, D), 'w_mk': (128, D), 'w_mv': (128, D), 'w_mo': (128, D), 'w_mlp1': (D, 512),
               'w_mlp2': (512, D)}
PIECE_COLS = ('wq', 'wkv', 'w_mlp1')


def _k_win(w):
    kr = w[..., 384:416]
    zc = lambda k: jnp.zeros(w.shape[:-1] + (k,), w.dtype)
    return jnp.concatenate(
        [w[..., 0:384], kr, zc(96), kr[..., 16:32], kr[..., 0:16], zc(96),
         w[..., 416:672], w[..., 1444:1700], w[..., 1184:1188], zc(124),
         w[..., 672:1184], w[..., 1188:1444]], axis=-1)


def _k_win_inv(m):
    gk = m[..., 384:416] + jnp.concatenate([m[..., 528:544], m[..., 512:528]], axis=-1)
    return jnp.concatenate([m[..., 0:384], gk, m[..., 640:896], m[..., 1280:1792], m[..., 1152:1156], m[..., 1792:2048],
                            m[..., 896:1152]], axis=-1)


def _k_wq(w):
    nh = w.shape[-1] // (NOPE + ROPE)
    w = w.reshape(w.shape[:-1] + (nh, NOPE + ROPE))
    nope, r1, r2 = w[..., :NOPE], w[..., NOPE:NOPE + 16], w[..., NOPE + 16:]
    z = lambda k: jnp.zeros(w.shape[:-1] + (k,), w.dtype)
    both = jnp.stack([jnp.concatenate([nope, r1, r2, z(32)], -1), jnp.concatenate([z(64), r2, r1, z(32)], -1)], axis=-3)
    return both.reshape(w.shape[:-2] + (2 * nh * SLAB,))


def _k_wq_inv(m):
    nh = m.shape[-1] // (2 * SLAB)
    m = m.reshape(m.shape[:-1] + (2, nh, SLAB))
    q0, q1 = m[..., 0, :, :], m[..., 1, :, :]
    w = jnp.concatenate([q0[..., :64], q0[..., 64:80] + q1[..., 80:96], q0[..., 80:96] + q1[..., 64:80]], -1)
    return w.reshape(w.shape[:-2] + (nh * (NOPE + ROPE),))


def _k_wkv(w):
    nh = w.shape[-1] // (NOPE + VDIM)
    w = w.reshape(w.shape[:-1] + (nh, NOPE + VDIM))
    z = jnp.zeros(w.shape[:-1] + (64,), w.dtype)
    return jnp.concatenate([w[..., :NOPE], z, w[..., NOPE:], z], -1).reshape(w.shape[:-2] + (nh * 2 * SLAB,))


def _k_wkv_inv(m):
    nh = m.shape[-1] // (2 * SLAB)
    m = m.reshape(m.shape[:-1] + (nh, 2 * SLAB))
    return jnp.concatenate([m[..., :NOPE], m[..., SLAB:SLAB + VDIM]], -1).reshape(m.shape[:-2] + (nh * (NOPE + VDIM),))


_K_FWD = {'win': _k_win, 'wq': _k_wq, 'wkv': _k_wkv}
_K_INV = {'win': _k_win_inv, 'wq': _k_wq_inv, 'wkv': _k_wkv_inv}


def _assemble(piece, g):
    _, _, a, b = g.shape
    if piece == 'wq':
        return g.reshape(2, 4, a, 2, SLAB).transpose(2, 3, 1, 0, 4).reshape(a, N_DEV * b)
    if piece in PIECE_COLS:
        return g.transpose(2, 1, 0, 3).reshape(a, N_DEV * b)
    return g.transpose(1, 0, 2, 3).reshape(N_DEV * a, b)


def _disassemble(piece, full):
    a, b = PIECE_SHAPE[piece]
    if piece == 'wq':
        return full.reshape(a, 2, 4, 2, SLAB).transpose(3, 2, 0, 1, 4).reshape(2, 4, a, b)
    if piece in PIECE_COLS:
        return full.reshape(a, 4, 2, b).transpose(2, 1, 0, 3)
    return full.reshape(4, 2, a, b).transpose(1, 0, 2, 3)


def _piece_rows(piece):
    a, b = PIECE_SHAPE[piece]
    return a * b // PACK_C


def _prep_layer(Wk, Ws, l):
    P = {n: Wk[n][l] for n in PIECES}
    ri, ci = jnp.arange(SLAB)[:, None], jnp.arange(2 * SLAB)[None, :]
    sel = ((ri < ROPE) & (ci == ri + NOPE)).astype(P['wkv'].dtype)
    P['wkv'] = jnp.concatenate([P['wkv'], jnp.tile(sel, (1, HEADS))], axis=0)
    wout = P['wout']
    mla_rows = jnp.pad(wout[:HEADS * VDIM].reshape(HEADS, VDIM, D), ((0, 0), (0, SLAB - VDIM), (0, 0)))
    P['wout'] = jnp.concatenate([mla_rows.reshape(HEADS * SLAB, D), wout[HEADS * VDIM:]], axis=0)
    W = Ws
    row = lambda n: W[n][l].astype(F32).reshape(1, -1)
    for n in ('mix_norm_g', 'mla_q_norm_g', 'mla_kv_norm_g', 'ssm_norm_g', 'lru_lambda', 'lru_out_g', 'xattn_norm_g',
              'mem_norm_g', 'mlp_norm_g'):
        P[n] = row(n)
    P['mla_out_g'] = jnp.pad(W['mla_out_g'][l].astype(F32).reshape(HEADS, VDIM), ((0, 0), (0, SLAB - VDIM))).reshape(1, -1)
    for n in ('ssm_dt_bias', 'ssm_a_log', 'ssm_d'):
        P[n] = _pad_lanes(W[n][l])
    P['conv_w'] = jnp.pad(jnp.concatenate([W['ssm_conv_w'][l], W['lru_conv_w'][l]], axis=1).astype(F32), ((0, 4), (0, 0)))
    P['conv_b'] = jnp.concatenate([W['ssm_conv_b'][l], W['lru_conv_b'][l]]).astype(F32).reshape(1, -1)
    for n in ('lru_w_a', 'lru_w_i'):
        P[n] = jnp.concatenate([jnp.pad(W[n][l, k].astype(F32), ((0, 0), (64 * k, 192 - 64 * k))) for k in range(4)], axis=0)
    for n in ('lru_b_a', 'lru_b_i'):
        P[n] = W[n][l].astype(F32).reshape(1, -1)
    return P


def _unprep_pieces(G):
    o = {n: G[n] for n in PIECES}
    o['wkv'] = G['wkv'][:KV_RANK]
    wo = G['wout']
    o['wout'] = jnp.concatenate([wo[:HEADS * SLAB].reshape(HEADS, SLAB, D)[:, :VDIM].reshape(HEADS * VDIM, D),
                                 wo[HEADS * SLAB:]], axis=0)
    return o


def _unprep_small(G):
    o = {}
    for n in ('mix_norm_g', 'mla_q_norm_g', 'mla_kv_norm_g', 'ssm_norm_g', 'lru_lambda', 'lru_out_g', 'xattn_norm_g',
              'mem_norm_g', 'mlp_norm_g', 'lru_b_a', 'lru_b_i'):
        o[n] = G[n].reshape(-1)
    o['lru_b_a'] = o['lru_b_a'].reshape(4, 64)
    o['lru_b_i'] = o['lru_b_i'].reshape(4, 64)
    o['mla_out_g'] = G['mla_out_g'].reshape(HEADS, SLAB)[:, :VDIM].reshape(-1)
    for n in ('ssm_dt_bias', 'ssm_a_log', 'ssm_d'):
        o[n] = G[n][0, :4]
    o['ssm_conv_w'], o['lru_conv_w'] = G['conv_w'][:4, :512], G['conv_w'][:4, 512:]
    o['ssm_conv_b'], o['lru_conv_b'] = G['conv_b'][0, :512], G['conv_b'][0, 512:]
    for n in ('lru_w_a', 'lru_w_i'):
        o[n] = jnp.stack([G[n][64 * k:64 * (k + 1), 64 * k:64 * (k + 1)] for k in range(4)])
    return o


def _rope_tables(positions):
    half = ROPE // 2
    inv_freq = ROPE_THETA ** (-jnp.arange(half, dtype=F32) * 2.0 / ROPE)
    ang = positions.astype(F32)[:, None] * inv_freq
    cos, sin = jnp.cos(ang), jnp.sin(ang)
    T = positions.shape[0]
    z = lambda k: jnp.zeros((T, k), F32)
    ck = jnp.concatenate([cos, cos, z(96)], axis=1)
    sk = jnp.concatenate([-sin, sin, z(96)], axis=1)
    cq = jnp.concatenate([jnp.ones((T, NOPE), F32), cos, cos, z(32)], axis=1)
    sq = jnp.concatenate([z(NOPE), -sin, sin, z(32)], axis=1)
    return ck, sk, cq, sq


def _add_epi(acc, res):
    return (acc + res,)


def _relu2_epi(acc):
    r = jnp.maximum(acc, 0.0)
    return r, r * r


def _drelu2_epi(acc, r):
    return (acc * (2.0 * r.astype(F32)),)


def _norm(x, g, name):
    return _rows_fwd(_f_norm, [x], [g], [(x.shape[1], BF16)], name=name)[0]


def _norm_bwd(x, g, ct, add, name):
    (dx, dx16), (dg,) = _rows_vjp(_f_norm, [x], [g], [ct], name=name, drows=[0], dparams=[0], drow_dtypes=[F32],
                                  add=add, twin=True)
    return dx, dx16, dg


def _layer_fwd(x0, mem, P, tabs):
    ck, sk, cq, sq = tabs
    S = {'x0': x0}
    h1 = S['h1'] = _norm(x0, P['mix_norm_g'], "norm_mix")
    win = P['win']
    u_mla = S['u_mla'] = _mm(h1, win[:, 0:U_MLA], name="in_mla")
    u_gate = S['u_gate'] = _mm(h1, win[:, U_MLA:U_MLA + U_GATE], name="in_gate")
    u_conv = S['u_conv'] = _mm(h1, win[:, U_MLA + U_GATE:], name="in_conv")
    cqn, akv = _rows_fwd(_f_mla_prep, [u_mla, ck, sk], [P['mla_q_norm_g'], P['mla_kv_norm_g']],
                         [(Q_RANK, BF16), (2 * SLAB, BF16)], name="mla_prep")
    S['cqn'], S['akv'] = cqn, akv
    yq = _mm(cqn, P['wq'], name="q_proj")
    q = S['q'] = _rows_fwd(_f_qrope, [yq, cq, sq], [], [(HEADS * SLAB, BF16)], name="q_rope")[0]
    kv = S['kv'] = _mm(akv, P['wkv'], name="kv_proj", out_dtypes=(BF16,))
    o, lse = _attn_fwd(q, kv, name="attn_fwd")
    S['o'], S['lse'] = o, lse
    c_ssm, c_lru = _conv_fwd(u_conv, P['conv_w'], P['conv_b'], name="conv_fwd")
    S['c_ssm'], S['c_lru'] = c_ssm, c_lru
    ys, sall = _ssd_fwd(c_ssm, u_gate, P['ssm_dt_bias'], P['ssm_a_log'], P['ssm_d'], P['ssm_norm_g'], name="ssd_fwd")
    S['ys'], S['sall'] = ys, sall
    a, b = _rows_fwd(_f_lru_gates, [c_lru], [P['lru_w_a'], P['lru_b_a'], P['lru_w_i'], P['lru_b_i'], P['lru_lambda']],
                     [(256, F32), (256, F32)], name="lru_gates")
    h, hprev = _lru_scan_fwd(a, b, name="lru_scan")
    S['a'], S['h'], S['hprev'] = a, h, hprev
    ymix = S['ymix'] = _rows_fwd(_f_mix, [o, ys, h, u_gate], [P['mla_out_g'], P['lru_out_g']],
                                 [(HEADS * SLAB + 512, BF16)], name="mix")[0]
    x1 = S['x1'] = _mm(ymix, P['wout'], name="out_proj", epi=_add_epi, extras=(x0,))
    hx = S['hx'] = _norm(x1, P['xattn_norm_g'], "norm_xattn")
    qx = S['qx'] = _mm(hx, P['w_mq'], name="mem_q", out_dtypes=(BF16,))
    mn = S['mn'] = _norm(mem, P['mem_norm_g'], "norm_mem")
    kx = S['kx'] = _mm(mn, P['w_mk'], name="mem_k", out_dtypes=(BF16,))
    vx = S['vx'] = _mm(mn, P['w_mv'], name="mem_v", out_dtypes=(BF16,))
    ox = S['ox'] = _rows_fwd(_f_xattn, [qx], [kx, vx], [(D, BF16)], name="xattn")[0]
    x2 = S['x2'] = _mm(ox, P['w_mo'], name="mem_o", epi=_add_epi, extras=(x1,))
    hm = S['hm'] = _norm(x2, P['mlp_norm_g'], "norm_mlp")
    r, s = _mm(hm, P['w_mlp1'], name="mlp_up", epi=_relu2_epi, out_dtypes=(BF16, BF16))
    S['r'], S['s'] = r, s
    x3 = _mm(s, P['w_mlp2'], name="mlp_down", epi=_add_epi, extras=(x2,))
    return x3, S


def _layer_bwd(dx3, dx3h, mem, S, P, tabs):
    ck, sk, cq, sq = tabs
    G = {}
    da = _mm(dx3h, P['w_mlp2'], "nt", name="mlp_down_dx", epi=_drelu2_epi, extras=(S['r'],), out_dtypes=(BF16,))
    G['w_mlp2'] = _mm(S['s'], dx3h, "tn", name="mlp_down_dw")
    G['w_mlp1'] = _mm(S['hm'], da, "tn", name="mlp_up_dw")
    dhm = _mm(da, P['w_mlp1'], "nt", name="mlp_up_dx")
    dx2, dx2h, G['mlp_norm_g'] = _norm_bwd(S['x2'], P['mlp_norm_g'], dhm, dx3, "norm_mlp_bwd")
    dox = _mm(dx2h, P['w_mo'], "nt", name="mem_o_dx")
    G['w_mo'] = _mm(S['ox'], dx2h, "tn", name="mem_o_dw")
    (dqx,), (dkx, dvx) = _rows_vjp(_f_xattn, [S['qx']], [S['kx'], S['vx']], [dox], name="xattn_bwd", drows=[0],
                                   dparams=[0, 1], drow_dtypes=[BF16])
    G['w_mq'] = _mm(S['hx'], dqx, "tn", name="mem_q_dw")
    dhx = _mm(dqx, P['w_mq'], "nt", name="mem_q_dx")
    dx1, dx1h, G['xattn_norm_g'] = _norm_bwd(S['x1'], P['xattn_norm_g'], dhx, dx2, "norm_xattn_bwd")
    G['w_mk'] = _mm(S['mn'], dkx, "tn", name="mem_k_dw")
    G['w_mv'] = _mm(S['mn'], dvx, "tn", name="mem_v_dw")
    dmn = _mm(dkx, P['w_mk'], "nt", name="mem_k_dx", epi=_add_epi, extras=(_mm(dvx, P['w_mv'], "nt", name="mem_v_dx"),))
    _, _, G['mem_norm_g'] = _norm_bwd(mem, P['mem_norm_g'], dmn, None, "norm_mem_bwd")
    dymix = _mm(dx1h, P['wout'], "nt", name="out_proj_dx")
    G['wout'] = _mm(S['ymix'], dx1h, "tn", name="out_proj_dw")
    (do, dys, dh, dug_mix), (G['mla_out_g'], G['lru_out_g']) = _rows_vjp(
        _f_mix, [S['o'], S['ys'], S['h'], S['u_gate']], [P['mla_out_g'], P['lru_out_g']], [dymix], name="mix_bwd",
        drows=[0, 1, 2, 3], dparams=[0, 1], drow_dtypes=[BF16, F32, F32, F32])
    g, da_lru = _lru_scan_bwd(S['a'], dh, S['hprev'], name="lru_scan_bwd")
    lru_par = [P['lru_w_a'], P['lru_b_a'], P['lru_w_i'], P['lru_b_i'], P['lru_lambda']]
    (dc_lru,), dpar = _rows_vjp(_f_lru_gates, [S['c_lru']], lru_par, [da_lru, g], name="lru_gates_bwd", drows=[0],
                                dparams=[0, 1, 2, 3, 4], drow_dtypes=[F32])
    G['lru_w_a'], G['lru_b_a'], G['lru_w_i'], G['lru_b_i'], G['lru_lambda'] = dpar
    dc_ssm, dug_ssd, G['ssm_dt_bias'], G['ssm_a_log'], G['ssm_d'], G['ssm_norm_g'] = _ssd_bwd(
        S['c_ssm'], S['u_gate'], S['sall'], dys, P['ssm_dt_bias'], P['ssm_a_log'], P['ssm_d'], P['ssm_norm_g'],
        name="ssd_bwd")
    du_conv, G['conv_w'], G['conv_b'] = _conv_bwd(S['u_conv'], dc_ssm, dc_lru, P['conv_w'], name="conv_bwd")
    delta = _attn_delta(do, S['o'], name="attn_delta")
    dqt, dkv = _attn_bwd(S['q'], S['kv'], do, S['lse'], delta, name="attn_bwd")
    dyq = _qrope_bwd(dqt, cq, sq, name="q_rope_bwd")
    dcqn = _mm(dyq, P['wq'], "nt", name="q_proj_dx")
    G['wq'] = _mm(S['cqn'], dyq, "tn", name="q_proj_dw")
    dakv = _mm(dkv, P['wkv'], "nt", name="kv_proj_dx")
    G['wkv'] = _mm(S['akv'], dkv, "tn", name="kv_proj_dw")
    (du_mla,), (G['mla_q_norm_g'], G['mla_kv_norm_g']) = _rows_vjp(
        _f_mla_prep, [S['u_mla'], ck, sk], [P['mla_q_norm_g'], P['mla_kv_norm_g']], [dcqn, dakv], name="mla_prep_bwd",
        drows=[0], dparams=[0, 1], drow_dtypes=[BF16])
    du = jnp.concatenate([du_mla, (dug_mix + dug_ssd).astype(BF16), du_conv.astype(BF16)], axis=1)
    dh1 = _mm(du, P['win'], "nt", name="in_dx")
    G['win'] = _mm(S['h1'], du, "tn", name="in_dw")
    dx0, dx0h, G['mix_norm_g'] = _norm_bwd(S['x0'], P['mix_norm_g'], dh1, dx1, "norm_mix_bwd")
    return dx0, dx0h, G


def _local_step(x, mem, positions, Wk, Ws, tgt):
    tabs = _rope_tables(positions)
    saved, preps = [], []
    for l in range(DEPTH):
        P = _prep_layer(Wk, Ws, l)
        x, S = _layer_fwd(x, mem, P, tabs)
        saved.append(S)
        preps.append(P)
    loss, dx, dxh, dg_final = _loss_head(x, tgt, Ws['final_norm_g'].astype(F32).reshape(1, D), name="loss_head")
    pieces, small = [None] * DEPTH, [None] * DEPTH
    for l in reversed(range(DEPTH)):
        dx, dxh, G = _layer_bwd(dx, dxh, mem, saved[l], preps[l], tabs)
        pieces[l], small[l] = _unprep_pieces(G), _unprep_small(G)
    grads = {n: jnp.stack([small[l][n] for l in range(DEPTH)]) for n in SMALL if n != 'final_norm_g'}
    grads['final_norm_g'] = dg_final.reshape(D)
    return loss, dx, pieces, grads


def _adamw_nd(w, g, m, v, name):
    shp = w.shape
    two = lambda a: a.reshape(-1, shp[-1])
    return [r.reshape(shp) for r in _adamw(two(w), two(g), two(m), two(v), name=name)]


def kernel(x, mem, positions, mix_norm_g, w_in, mla_q_norm_g, mla_kv_norm_g, mla_w_uq, mla_w_ukv, mla_out_g, ssm_conv_w, ssm_conv_b, ssm_dt_bias, ssm_a_log, ssm_d, ssm_norm_g, lru_conv_w, lru_conv_b, lru_w_a, lru_b_a, lru_w_i, lru_b_i, lru_lambda, lru_out_g, w_out, xattn_norm_g, mem_norm_g, w_mq, w_mk, w_mv, w_mo, mlp_norm_g, w_mlp1, w_mlp2, final_norm_g, loss_target, m_mix_norm_g, m_w_in, m_mla_q_norm_g, m_mla_kv_norm_g, m_mla_w_uq, m_mla_w_ukv, m_mla_out_g, m_ssm_conv_w, m_ssm_conv_b, m_ssm_dt_bias, m_ssm_a_log, m_ssm_d, m_ssm_norm_g, m_lru_conv_w, m_lru_conv_b, m_lru_w_a, m_lru_b_a, m_lru_w_i, m_lru_b_i, m_lru_lambda, m_lru_out_g, m_w_out, m_xattn_norm_g, m_mem_norm_g, m_w_mq, m_w_mk, m_w_mv, m_w_mo, m_mlp_norm_g, m_w_mlp1, m_w_mlp2, m_final_norm_g, v_mix_norm_g, v_w_in, v_mla_q_norm_g, v_mla_kv_norm_g, v_mla_w_uq, v_mla_w_ukv, v_mla_out_g, v_ssm_conv_w, v_ssm_conv_b, v_ssm_dt_bias, v_ssm_a_log, v_ssm_d, v_ssm_norm_g, v_lru_conv_w, v_lru_conv_b, v_lru_w_a, v_lru_b_a, v_lru_w_i, v_lru_b_i, v_lru_lambda, v_lru_out_g, v_w_out, v_xattn_norm_g, v_mem_norm_g, v_w_mq, v_w_mk, v_w_mv, v_w_mo, v_mlp_norm_g, v_w_mlp1, v_w_mlp2, v_final_norm_g):
    a = locals()
    w = {n: a[n] for n in WEIGHTS}
    m = {n: a['m_' + n] for n in WEIGHTS}
    v = {n: a['v_' + n] for n in WEIGHTS}
    me = 4 * lax.axis_index("x") + 2 * lax.axis_index("y") + lax.axis_index("c")

    shard = {n: _K_FWD[n](w[PIECE_SOURCE[n]]) if n in _K_FWD else w[PIECE_SOURCE.get(n, n)] for n in PIECES}
    order = [(l, n) for l in range(DEPTH) for n in PIECES]
    packed_w = jnp.concatenate([shard[n][l].astype(BF16).reshape(-1, PACK_C) for l, n in order], axis=0)
    gathered = _all_gather(packed_w, name="gather_w")
    Wk, off = {n: [] for n in PIECES}, 0
    for l, n in order:
        rows = _piece_rows(n)
        Wk[n].append(_assemble(n, gathered[:, :, off:off + rows].reshape((2, 4) + PIECE_SHAPE[n])))
        off += rows
    Ws = {}
    conv_shapes = [w[n].shape for n in CONV_SHARDED]
    conv_g = _all_gather(_pack([w[n] for n in CONV_SHARDED], F32), name="gather_conv")
    conv_g = conv_g.transpose(1, 0, 2, 3).reshape((N_DEV,) + conv_g.shape[2:])
    for n, g in zip(CONV_SHARDED, _unpack(conv_g, conv_shapes, lead=(N_DEV,))):
        Ws[n] = g.transpose(1, 2, 0, 3).reshape(g.shape[1], g.shape[2], N_DEV * g.shape[3])
    for n in SMALL:
        if n not in CONV_SHARDED:
            Ws[n] = w[n]

    loss_share, dx, g_pieces, grads = _local_step(x[0], mem[0], positions[0], Wk, Ws, loss_target[0])
    loss = lax.psum(loss_share[0, 0], ("x", "y", "c"))

    packed_g = jnp.concatenate([_disassemble(n, g_pieces[l][n]).reshape(2, 4, -1, PACK_C) for l, n in order], axis=2)
    g_big = _reduce_scatter(packed_g, name="scatter_g")
    got, off = {n: [] for n in PIECES}, 0
    for l, n in order:
        rows = _piece_rows(n)
        got[n].append(g_big[off:off + rows].reshape(PIECE_SHAPE[n]))
        off += rows
    g_out = {}
    for n in PIECES:
        g = jnp.stack(got[n])
        g_out[PIECE_SOURCE.get(n, n)] = _K_INV[n](g) if n in _K_INV else g
    small_shapes = [grads[n].shape for n in SMALL]
    g_small = _all_reduce(_pack([grads[n] for n in SMALL], F32), name="reduce_g")
    for n, g in zip(SMALL, _unpack(g_small, small_shapes)):
        if n in CONV_SHARDED:
            cols = w[n].shape[-1]
            g = lax.dynamic_slice_in_dim(g, me * cols, cols, axis=2)
        g_out[n] = g

    delta, new_m, new_v = {}, {}, {}
    for n in BIG:
        delta[n], new_m[n], new_v[n] = _adamw_nd(w[n], g_out[n], m[n], v[n], "adamw_" + n)
    shapes = [w[n].shape for n in SMALL]
    packed = [_pack([d[n] for n in SMALL], F32) for d in (w, g_out, m, v)]
    for d, res in zip((delta, new_m, new_v), _adamw(*packed, name="adamw_small")):
        d.update(zip(SMALL, _unpack(res, shapes)))

    return (loss, dx[None], *[g_out[n] for n in WEIGHTS], *[delta[n] for n in WEIGHTS],
            *[new_m[n] for n in WEIGHTS], *[new_v[n] for n in WEIGHTS])
```

```python
import functools
import math

import jax
import jax.numpy as jnp
from jax import lax
from jax.experimental import pallas as pl
from jax.experimental.pallas import tpu as pltpu

F32, BF16 = jnp.float32, jnp.bfloat16

D = 1024
DEPTH = 4
N_MEM = 256
EPS = 1e-6
HEADS = 8
NOPE, ROPE, VDIM = 64, 32, 64
Q_RANK, KV_RANK = 256, 128
ROPE_THETA = 10000.0
SSM_CHUNK = 128
LRU_C = 8.0
MEM_HEADS = 4
D_FF = 4 * D
SLAB = 128
LR, B1, B2, AEPS, WD, STEP = 0.001, 0.9, 0.999, 1e-08, 0.01, 10

N_DEV = 8
MESH = pl.DeviceIdType.MESH

U_MLA = 640
U_GATE = 640
U_CONV = 768

_DN = {"nn": (((1,), (0,)), ((), ())), "nt": (((1,), (1,)), ((), ())), "tn": (((0,), (0,)), ((), ()))}


def _dot(a, b, kind):
    return lax.dot_general(a.astype(BF16), b.astype(BF16), _DN[kind], preferred_element_type=F32)


@functools.partial(jax.custom_vjp, nondiff_argnums=(2,))
def _bdot(a, b, kind):
    return _dot(a, b, kind)


def _bdot_fwd(a, b, kind):
    return _dot(a, b, kind), (a, b)


def _bdot_bwd(kind, res, g):
    a, b = res
    if kind == "nn":
        da, db = _dot(g, b, "nt"), _dot(a, g, "tn")
    elif kind == "nt":
        da, db = _dot(g, b, "nn"), _dot(g, a, "tn")
    else:
        da, db = _dot(b, g, "nt"), _dot(a, g, "nn")
    return da.astype(a.dtype), db.astype(b.dtype)


_bdot.defvjp(_bdot_fwd, _bdot_bwd)


def _tile(n, pref):
    if n <= pref:
        return n
    t = pref
    while n % t:
        t -= SLAB
    return t


def _cparams(sem, vmem_mb=48):
    return pltpu.CompilerParams(dimension_semantics=sem, vmem_limit_bytes=vmem_mb * 1024 * 1024)


def _mm(a, b, kind="nn", *, name, out_dtypes=(F32,), epi=None, extras=(), tm=1024, tn=1024, tk=1024):
    if kind == "tn":
        K, M = a.shape
    else:
        M, K = a.shape
    N = b.shape[0] if kind == "nt" else b.shape[1]
    if a.dtype == F32 or b.dtype == F32:
        tk = tk // 2
    tm, tn, tk = _tile(M, tm), _tile(N, tn), _tile(K, tk)
    nk = K // tk
    a_spec = pl.BlockSpec((tk, tm), lambda i, j, k: (k, i)) if kind == "tn" else pl.BlockSpec((tm, tk), lambda i, j, k: (i, k))
    b_spec = pl.BlockSpec((tn, tk), lambda i, j, k: (j, k)) if kind == "nt" else pl.BlockSpec((tk, tn), lambda i, j, k: (k, j))
    o_spec = pl.BlockSpec((tm, tn), lambda i, j, k: (i, j))
    n_ex, n_out = len(extras), len(out_dtypes)

    def body(*refs):
        a_ref, b_ref = refs[:2]
        ex = refs[2:2 + n_ex]
        outs = refs[2 + n_ex:2 + n_ex + n_out]
        acc = refs[-1]
        k = pl.program_id(2)

        def finish(r):
            res = epi(r, *[e[...] for e in ex]) if epi is not None else (r,)
            for o, v in zip(outs, res):
                o[...] = v.astype(o.dtype)

        if nk == 1:
            finish(_dot(a_ref[...], b_ref[...], kind))
            return

        @pl.when(k == 0)
        def _():
            acc[...] = _dot(a_ref[...], b_ref[...], kind)

        @pl.when(k > 0)
        def _():
            acc[...] += _dot(a_ref[...], b_ref[...], kind)

        @pl.when(k == nk - 1)
        def _():
            finish(acc[...])

    res = pl.pallas_call(
        body,
        name=name,
        grid=(M // tm, N // tn, nk),
        in_specs=[a_spec, b_spec] + [o_spec] * n_ex,
        out_specs=[o_spec] * n_out,
        out_shape=[jax.ShapeDtypeStruct((M, N), dt) for dt in out_dtypes],
        scratch_shapes=[pltpu.VMEM((tm, tn), F32)],
        compiler_params=_cparams(("parallel", "parallel", "arbitrary")),
    )(a, b, *extras)
    return res[0] if n_out == 1 else res


_MM_ROWS = 1024


def _row_spec(arr, tm):
    return pl.BlockSpec((tm, arr.shape[1]), lambda i: (i, 0))


def _full_spec(arr):
    nd = arr.ndim
    return pl.BlockSpec(arr.shape, lambda i: (0,) * nd)


def _rows_fwd(fn, rows, params, outs, *, name, tm=256):
    T = rows[0].shape[0]
    tm = min(tm, T)
    nr, npar = len(rows), len(params)

    def body(*refs):
        ins = [r[...] for r in refs[:nr + npar]]
        res = fn(*ins)
        for o, v in zip(refs[nr + npar:], res):
            o[...] = v.astype(o.dtype)

    res = pl.pallas_call(
        body,
        name=name,
        grid=(T // tm,),
        in_specs=[_row_spec(r, tm) for r in rows] + [_full_spec(p) for p in params],
        out_specs=[pl.BlockSpec((tm, c), lambda i: (i, 0)) for c, _ in outs],
        out_shape=[jax.ShapeDtypeStruct((T, c), dt) for c, dt in outs],
        compiler_params=_cparams(("parallel",)),
    )(*rows, *params)
    return res


def _rows_vjp(fn, rows, params, cts, *, name, drows, dparams, drow_dtypes, add=None, twin=False, tm=256):
    T = rows[0].shape[0]
    tm = min(tm, T)
    nr, npar, nct = len(rows), len(params), len(cts)
    n_add = 0 if add is None else 1
    n_dr, n_dp = len(drows), len(dparams)
    n_tw = 1 if twin else 0

    def body(*refs):
        row_t = [r[...] for r in refs[:nr]]
        par_t = [r[...] for r in refs[nr:nr + npar]]
        ct_t = [r[...].astype(F32) for r in refs[nr + npar:nr + npar + nct]]
        pos = nr + npar + nct
        add_t = refs[pos][...] if n_add else None
        pos += n_add
        drow_refs = refs[pos:pos + n_dr]
        dpar_refs = refs[pos + n_dr:pos + n_dr + n_dp]

        def g(*dargs):
            rr, pp = list(row_t), list(par_t)
            for idx, v in zip(drows, dargs[:n_dr]):
                rr[idx] = v
            for idx, v in zip(dparams, dargs[n_dr:]):
                pp[idx] = v
            return tuple(fn(*rr, *pp))

        prim = [row_t[i].astype(F32) for i in drows] + [par_t[i].astype(F32) for i in dparams]
        _, vjp = jax.vjp(g, *prim)
        grads = vjp(tuple(ct_t))
        for n, (o, v) in enumerate(zip(drow_refs, grads[:n_dr])):
            if n == 0 and n_add:
                v = v + add_t.astype(F32)
            o[...] = v.astype(o.dtype)
            if n == 0 and n_tw:
                refs[-1][...] = v.astype(BF16)

        @pl.when(pl.program_id(0) == 0)
        def _():
            for o in dpar_refs:
                o[...] = jnp.zeros_like(o)

        for o, v in zip(dpar_refs, grads[n_dr:]):
            o[...] += v

    res = pl.pallas_call(
        body,
        name=name,
        grid=(T // tm,),
        in_specs=[_row_spec(r, tm) for r in rows] + [_full_spec(p) for p in params] + [_row_spec(c, tm) for c in cts]
        + ([_row_spec(add, tm)] if n_add else []),
        out_specs=[_row_spec(rows[i], tm) for i in drows] + [_full_spec(params[i]) for i in dparams]
        + [_row_spec(rows[drows[0]], tm)] * n_tw,
        out_shape=[jax.ShapeDtypeStruct(rows[i].shape, dt) for i, dt in zip(drows, drow_dtypes)]
        + [jax.ShapeDtypeStruct(params[i].shape, F32) for i in dparams]
        + [jax.ShapeDtypeStruct(rows[drows[0]].shape, BF16)] * n_tw,
        compiler_params=_cparams(("arbitrary",)),
    )(*rows, *params, *cts, *([add] if n_add else []))
    return list(res[:n_dr]) + list(res[n_dr + n_dp:]), list(res[n_dr:n_dr + n_dp])


def _rms(x, g, n):
    return x * lax.rsqrt(jnp.sum(x * x, axis=-1, keepdims=True) * (1.0 / n) + EPS) * g


def _sigmoid(x):
    return 1.0 / (1.0 + jnp.exp(-x))


def _silu(x):
    return x * _sigmoid(x)


def _softplus(x):
    return jnp.maximum(x, 0.0) + jnp.log(1.0 + jnp.exp(-jnp.abs(x)))


def _gelu_tanh(x):
    return 0.5 * x * (1.0 + jnp.tanh(math.sqrt(2.0 / math.pi) * (x + 0.044715 * x * x * x)))


def _lane(shape):
    return lax.broadcasted_iota(jnp.int32, shape, len(shape) - 1)


def _col(x, h):
    return jnp.sum(jnp.where(_lane(x.shape) == h, x, 0.0), axis=-1, keepdims=True)


def _f_norm(x, g):
    return (_rms(x.astype(F32), g, x.shape[-1]),)


def _f_mla_prep(u, ck, sk, gq, gkv):
    u = u.astype(F32)
    cq = _rms(u[:, 0:256], gq, Q_RANK)
    ckv = _rms(u[:, 256:384], gkv, KV_RANK)
    kr = u[:, 384:512] * ck + u[:, 512:640] * sk
    return cq, jnp.concatenate([ckv, kr], axis=1)


def _f_qrope(y, cq, sq):
    y = y.astype(F32)
    c8, s8 = jnp.tile(cq, (1, HEADS)), jnp.tile(sq, (1, HEADS))
    return (y[:, :HEADS * SLAB] * c8 + y[:, HEADS * SLAB:] * s8,)


def _f_lru_gates(xc, wa, ba, wi, bi, lam):
    xc = xc.astype(F32)
    r = _sigmoid(_bdot(xc, wa, "nn") + ba)
    i = _sigmoid(_bdot(xc, wi, "nn") + bi)
    log_a = -LRU_C * r * _softplus(-lam)
    a = jnp.exp(log_a)
    x2 = 2.0 * log_a
    m1 = jnp.where(x2 > -0.02, -x2 * (1.0 + x2 * (0.5 + x2 * (1.0 / 6.0 + x2 * (1.0 / 24.0)))), 1.0 - jnp.exp(x2))
    return a, jnp.sqrt(m1) * (i * xc)


def _f_mix(o, ys, h, ug, g_mla, g_lru):
    o = o.astype(F32)
    y_mla = _rms(o, g_mla, HEADS * VDIM)
    y_lru = _rms(h.astype(F32) * _gelu_tanh(ug[:, 256:512].astype(F32)), g_lru, 256)
    return (jnp.concatenate([y_mla, ys.astype(F32), y_lru], axis=1),)


def _f_xattn(q, k, v):
    hd = D // MEM_HEADS
    outs = []
    for h in range(MEM_HEADS):
        sl = slice(h * hd, (h + 1) * hd)
        s = _bdot(q[:, sl], k[:, sl], "nt") * (1.0 / math.sqrt(hd))
        s = s - jnp.max(s, axis=-1, keepdims=True)
        p = jnp.exp(s)
        p = p / jnp.sum(p, axis=-1, keepdims=True)
        outs.append(_bdot(p, v[:, sl], "nn"))
    return (jnp.concatenate(outs, axis=1),)


def _split_dot(tri, a, kind):
    a_hi = a.astype(BF16)
    r1 = a - a_hi.astype(F32)
    a_mid = r1.astype(BF16)
    a_lo = (r1 - a_mid.astype(F32)).astype(BF16)
    return _dot(tri, a_hi, kind) + _dot(tri, a_mid, kind) + _dot(tri, a_lo, kind)


@jax.custom_vjp
def _tri_cumsum(tri, a):
    return _split_dot(tri, a, "nn")


def _tri_cumsum_fwd(tri, a):
    return _split_dot(tri, a, "nn"), tri


def _tri_cumsum_bwd(tri, g):
    return jnp.zeros_like(tri), _split_dot(tri, g, "tn")


_tri_cumsum.defvjp(_tri_cumsum_fwd, _tri_cumsum_bwd)


def _f_ssd_chunk(c, ug, s0, s1, dtb, alog, dsk, ng):
    L = c.shape[0]
    c = c.astype(F32)
    xbc = _silu(c)
    xs, bm, cm = xbc[:, 0:256], xbc[:, 256:384], xbc[:, 384:512]
    z = ug[:, 0:256].astype(F32)
    dt = _softplus(ug[:, 512:640].astype(F32) + dtb)
    a = dt * (-jnp.exp(alog))
    rowi = lax.broadcasted_iota(jnp.int32, (L, L), 0)
    coli = lax.broadcasted_iota(jnp.int32, (L, L), 1)
    tril = rowi >= coli
    acum = _tri_cumsum(tril.astype(BF16), a)
    acum_t = acum.T
    lane = _lane((1, SLAB))
    lo = lane < 64
    ys, new_s = [], []
    for g in range(2):
        gm = (lane >= 64 * g) & (lane < 64 * g + 64)
        bg, cg = jnp.where(gm, bm, 0.0), jnp.where(gm, cm, 0.0)
        cb = _bdot(cg, bg, "nt")
        x = xs[:, SLAB * g:SLAB * (g + 1)]
        h0, h1 = 2 * g, 2 * g + 1
        ac0, ac1 = _col(acum, h0), _col(acum, h1)
        xdt = x * jnp.where(lo, _col(dt, h0), _col(dt, h1))
        ac_l = jnp.where(lo, ac0, ac1)
        tot = acum[L - 1:L, :]
        tot_l = jnp.where(lo, _col(tot, h0), _col(tot, h1))
        yd = jnp.zeros((L, SLAB), F32)
        for hh, acc, hm in ((h0, ac0, lo), (h1, ac1, jnp.logical_not(lo))):
            seg = acc - acum_t[hh:hh + 1, :]
            lm = jnp.where(tril, jnp.exp(jnp.where(tril, seg, 0.0)), 0.0)
            yd = yd + _bdot(cb * lm, jnp.where(hm, xdt, 0.0), "nn")
        sg = (s0, s1)[g]
        y_off = _bdot(cg, sg, "nn") * jnp.exp(ac_l)
        st = _bdot(bg, xdt * jnp.exp(tot_l - ac_l), "tn")
        new_s.append(jnp.exp(tot_l) * sg + st)
        y = yd + y_off + jnp.where(lo, _col(dsk, h0), _col(dsk, h1)) * x
        y = y * _silu(z[:, SLAB * g:SLAB * (g + 1)])
        ys.append(_rms(y, ng[:, SLAB * g:SLAB * (g + 1)], SLAB))
    return jnp.concatenate(ys, axis=1), new_s[0], new_s[1]


_SSD_TILE = 512


def _ssd_fwd(c, ug, dtb, alog, dsk, ng, *, name):
    T = c.shape[0]
    tm = min(_SSD_TILE, T)
    ncs = tm // SSM_CHUNK
    nc = T // SSM_CHUNK

    def body(c_ref, ug_ref, dtb_ref, alog_ref, dsk_ref, ng_ref, y_ref, sall_ref, s_scr):
        @pl.when(pl.program_id(0) == 0)
        def _():
            s_scr[...] = jnp.zeros_like(s_scr)

        s0, s1 = s_scr[0], s_scr[1]
        for k in range(ncs):
            rows = slice(k * SSM_CHUNK, (k + 1) * SSM_CHUNK)
            sall_ref[k, 0] = s0
            sall_ref[k, 1] = s1
            y, s0, s1 = _f_ssd_chunk(c_ref[rows, :], ug_ref[rows, :], s0, s1, dtb_ref[...], alog_ref[...],
                                     dsk_ref[...], ng_ref[...])
            y_ref[rows, :] = y
        s_scr[0] = s0
        s_scr[1] = s1

    y, sall = pl.pallas_call(
        body,
        name=name,
        grid=(T // tm,),
        in_specs=[_row_spec(c, tm), _row_spec(ug, tm)] + [_full_spec(p) for p in (dtb, alog, dsk, ng)],
        out_specs=[pl.BlockSpec((tm, 256), lambda i: (i, 0)), pl.BlockSpec((ncs, 2, SLAB, SLAB), lambda i: (i, 0, 0, 0))],
        out_shape=[jax.ShapeDtypeStruct((T, 256), F32), jax.ShapeDtypeStruct((nc, 2, SLAB, SLAB), F32)],
        scratch_shapes=[pltpu.VMEM((2, SLAB, SLAB), F32)],
        compiler_params=_cparams(("arbitrary",)),
    )(c, ug, dtb, alog, dsk, ng)
    return y, sall


def _ssd_bwd(c, ug, sall, dy, dtb, alog, dsk, ng, *, name):
    T = c.shape[0]
    tm = min(_SSD_TILE, T)
    ncs = tm // SSM_CHUNK
    nt = T // tm

    def body(c_ref, ug_ref, sall_ref, dy_ref, dtb_ref, alog_ref, dsk_ref, ng_ref,
             dc_ref, dug_ref, ddtb_ref, dalog_ref, ddsk_ref, dng_ref, ds_scr):
        @pl.when(pl.program_id(0) == 0)
        def _():
            ds_scr[...] = jnp.zeros_like(ds_scr)
            for o in (ddtb_ref, dalog_ref, ddsk_ref, dng_ref):
                o[...] = jnp.zeros_like(o)

        ds0, ds1 = ds_scr[0], ds_scr[1]
        for k in reversed(range(ncs)):
            rows = slice(k * SSM_CHUNK, (k + 1) * SSM_CHUNK)
            prim = (c_ref[rows, :].astype(F32), ug_ref[rows, :].astype(F32), sall_ref[k, 0], sall_ref[k, 1],
                    dtb_ref[...], alog_ref[...], dsk_ref[...], ng_ref[...])
            _, vjp = jax.vjp(_f_ssd_chunk, *prim)
            dc, dug, ds0, ds1, g_dtb, g_alog, g_dsk, g_ng = vjp((dy_ref[rows, :].astype(F32), ds0, ds1))
            dc_ref[rows, :] = dc
            dug_ref[rows, :] = dug
            ddtb_ref[...] += g_dtb
            dalog_ref[...] += g_alog
            ddsk_ref[...] += g_dsk
            dng_ref[...] += g_ng
        ds_scr[0] = ds0
        ds_scr[1] = ds1

    rev = lambda i: (nt - 1 - i, 0)
    params = (dtb, alog, dsk, ng)
    res = pl.pallas_call(
        body,
        name=name,
        grid=(nt,),
        in_specs=[pl.BlockSpec((tm, c.shape[1]), rev), pl.BlockSpec((tm, ug.shape[1]), rev),
                  pl.BlockSpec((ncs, 2, SLAB, SLAB), lambda i: (nt - 1 - i, 0, 0, 0)), pl.BlockSpec((tm, 256), rev)]
        + [_full_spec(p) for p in params],
        out_specs=[pl.BlockSpec((tm, 512), rev), pl.BlockSpec((tm, U_GATE), rev)] + [_full_spec(p) for p in params],
        out_shape=[jax.ShapeDtypeStruct((T, 512), F32), jax.ShapeDtypeStruct((T, U_GATE), F32)]
        + [jax.ShapeDtypeStruct(p.shape, F32) for p in params],
        scratch_shapes=[pltpu.VMEM((2, SLAB, SLAB), F32)],
        compiler_params=_cparams(("arbitrary",)),
    )(c, ug, sall, dy, *params)
    return res


_CONV_TILE = 512
_HALO = 8
_CONV_W = 4


def _conv_fwd(u, w, b, *, name):
    T, C = u.shape
    tm = min(_CONV_TILE, T)
    hb = tm // _HALO

    def body(u_ref, prev_ref, w_ref, b_ref, y1_ref, y2_ref, ext):
        i = pl.program_id(0)
        ext[0:_HALO, :] = jnp.where(i > 0, prev_ref[...], 0.0)
        ext[_HALO:, :] = u_ref[...]
        y = jnp.broadcast_to(b_ref[...], (tm, C))
        for k in range(_CONV_W):
            y = y + ext[_HALO - (_CONV_W - 1) + k:_HALO - (_CONV_W - 1) + k + tm, :] * w_ref[k:k + 1, :]
        y1_ref[...] = y[:, 0:512]
        y2_ref[...] = y[:, 512:768]

    return pl.pallas_call(
        body,
        name=name,
        grid=(T // tm,),
        in_specs=[_row_spec(u, tm), pl.BlockSpec((_HALO, C), lambda i: (jnp.maximum(i * hb - 1, 0), 0)),
                  _full_spec(w), _full_spec(b)],
        out_specs=[pl.BlockSpec((tm, 512), lambda i: (i, 0)), pl.BlockSpec((tm, 256), lambda i: (i, 0))],
        out_shape=[jax.ShapeDtypeStruct((T, 512), F32), jax.ShapeDtypeStruct((T, 256), F32)],
        scratch_shapes=[pltpu.VMEM((tm + _HALO, C), F32)],
        compiler_params=_cparams(("parallel",)),
    )(u, u, w, b)


def _conv_bwd(u, dy1, dy2, w, *, name):
    T, C = u.shape
    tm = min(_CONV_TILE, T)
    hb = tm // _HALO
    nt = T // tm

    def body(u_ref, prev_ref, dy1_ref, next1_ref, dy2_ref, next2_ref, w_ref, du_ref, dw_ref, db_ref, ext, dext):
        i = pl.program_id(0)
        ext[0:_HALO, :] = jnp.where(i > 0, prev_ref[...], 0.0)
        ext[_HALO:, :] = u_ref[...]
        dext[0:tm, 0:512] = dy1_ref[...]
        dext[0:tm, 512:768] = dy2_ref[...]
        dext[tm:, 0:512] = jnp.where(i < nt - 1, next1_ref[...], 0.0)
        dext[tm:, 512:768] = jnp.where(i < nt - 1, next2_ref[...], 0.0)

        @pl.when(i == 0)
        def _():
            dw_ref[...] = jnp.zeros_like(dw_ref)
            db_ref[...] = jnp.zeros_like(db_ref)

        dy = dext[0:tm, :]
        du = jnp.zeros((tm, C), F32)
        for k in range(_CONV_W):
            du = du + dext[_CONV_W - 1 - k:_CONV_W - 1 - k + tm, :] * w_ref[k:k + 1, :]
            xk = ext[_HALO - (_CONV_W - 1) + k:_HALO - (_CONV_W - 1) + k + tm, :]
            dw_ref[k:k + 1, :] += jnp.sum(dy * xk, axis=0, keepdims=True)
        du_ref[...] = du
        db_ref[...] += jnp.sum(dy, axis=0, keepdims=True)

    nxt = lambda i: (jnp.minimum((i + 1) * hb, T // _HALO - 1), 0)
    return pl.pallas_call(
        body,
        name=name,
        grid=(nt,),
        in_specs=[_row_spec(u, tm), pl.BlockSpec((_HALO, C), lambda i: (jnp.maximum(i * hb - 1, 0), 0)),
                  _row_spec(dy1, tm), pl.BlockSpec((_HALO, 512), nxt), _row_spec(dy2, tm), pl.BlockSpec((_HALO, 256), nxt),
                  _full_spec(w)],
        out_specs=[pl.BlockSpec((tm, C), lambda i: (i, 0)), _full_spec(w), pl.BlockSpec((1, C), lambda i: (0, 0))],
        out_shape=[jax.ShapeDtypeStruct((T, C), F32), jax.ShapeDtypeStruct(w.shape, F32), jax.ShapeDtypeStruct((1, C), F32)],
        scratch_shapes=[pltpu.VMEM((tm + _HALO, C), F32), pltpu.VMEM((tm + _HALO, C), F32)],
        compiler_params=_cparams(("arbitrary",)),
    )(u, u, dy1, dy1, dy2, dy2, w)


_SCAN_TILE = 1024
_SUB = 8


def _shift_rows(x, d, fill, up):
    r = lax.broadcasted_iota(jnp.int32, x.shape, 0)
    if up:
        return jnp.where(r < _SUB - d, pltpu.roll(x, _SUB - d, 0), fill)
    return jnp.where(r >= d, pltpu.roll(x, d, 0), fill)


def _lru_scan_fwd(a, b, *, name):
    T, W = a.shape
    tr = min(_SCAN_TILE, T)

    def body(a_ref, b_ref, h_ref, hp_ref, carry):
        @pl.when(pl.program_id(0) == 0)
        def _():
            carry[...] = jnp.zeros_like(carry)

        def step(t, cr):
            rows = pl.ds(pl.multiple_of(t * _SUB, _SUB), _SUB)
            aa, bb = a_ref[rows, :], b_ref[rows, :]
            for d in (1, 2, 4):
                bb = bb + aa * _shift_rows(bb, d, 0.0, False)
                aa = aa * _shift_rows(aa, d, 1.0, False)
            h = bb + aa * cr
            h_ref[rows, :] = h
            r = lax.broadcasted_iota(jnp.int32, h.shape, 0)
            hp_ref[rows, :] = jnp.where(r >= 1, pltpu.roll(h, 1, 0), cr)
            return jnp.broadcast_to(h[_SUB - 1:_SUB, :], (_SUB, W))

        carry[...] = lax.fori_loop(0, tr // _SUB, step, carry[...])

    return pl.pallas_call(
        body,
        name=name,
        grid=(T // tr,),
        in_specs=[_row_spec(a, tr), _row_spec(b, tr)],
        out_specs=[pl.BlockSpec((tr, W), lambda i: (i, 0))] * 2,
        out_shape=[jax.ShapeDtypeStruct((T, W), F32)] * 2,
        scratch_shapes=[pltpu.VMEM((_SUB, W), F32)],
        compiler_params=_cparams(("arbitrary",)),
    )(a, b)


def _lru_scan_bwd(a, dh, hprev, *, name):
    T, W = a.shape
    tr = min(_SCAN_TILE, T)
    nt = T // tr

    def body(a_ref, dh_ref, hp_ref, g_ref, da_ref, carry):
        @pl.when(pl.program_id(0) == 0)
        def _():
            carry[...] = jnp.zeros_like(carry)

        nsub = tr // _SUB

        def step(s, cr):
            t = nsub - 1 - s
            rows = pl.ds(pl.multiple_of(t * _SUB, _SUB), _SUB)
            a_t = a_ref[rows, :]
            aa = _shift_rows(a_t, 1, 1.0, True)
            bb = dh_ref[rows, :]
            for d in (1, 2, 4):
                bb = bb + aa * _shift_rows(bb, d, 0.0, True)
                aa = aa * _shift_rows(aa, d, 1.0, True)
            g = bb + aa * cr
            g_ref[rows, :] = g
            da_ref[rows, :] = g * hp_ref[rows, :]
            return jnp.broadcast_to(a_t[0:1, :] * g[0:1, :], (_SUB, W))

        carry[...] = lax.fori_loop(0, nsub, step, carry[...])

    rev = lambda i: (nt - 1 - i, 0)
    return pl.pallas_call(
        body,
        name=name,
        grid=(nt,),
        in_specs=[pl.BlockSpec((tr, W), rev)] * 3,
        out_specs=[pl.BlockSpec((tr, W), rev)] * 2,
        out_shape=[jax.ShapeDtypeStruct((T, W), F32)] * 2,
        scratch_shapes=[pltpu.VMEM((_SUB, W), F32)],
        compiler_params=_cparams(("arbitrary",)),
    )(a, dh, hprev)


_ATT_BLK = 512
_ATT_QPARTS = 2
_ATT_SCALE = 1.0 / math.sqrt(NOPE + ROPE)
_ATT_SCALE2 = _ATT_SCALE * math.log2(math.e)


def _call_with_exchange(body, send, gather, *, name, grid, in_specs, out_specs, out_shape, args):
    if send is None:
        return pl.pallas_call(body, name=name, grid=grid, in_specs=in_specs, out_specs=out_specs, out_shape=out_shape,
                              compiler_params=_cparams(("parallel", "arbitrary")))(*args)
    n_in, n_out = len(in_specs), len(out_specs)

    def riding(*refs):
        comm = (refs[n_in], refs[n_in + 1 + n_out]) + tuple(refs[n_in + 2 + n_out:])
        h, i = pl.program_id(0), pl.program_id(1)

        @pl.when((h == 0) & (i == 0))
        def _():
            _chip_start(*comm, gather=gather)

        body(*refs[:n_in], *refs[n_in + 1:n_in + 1 + n_out])

        @pl.when((h == grid[0] - 1) & (i == grid[1] - 1))
        def _():
            _chip_wait(*comm, gather=gather)

    shape = (4,) + send.shape if gather else send.shape
    return pl.pallas_call(
        riding, name=name, grid=grid, in_specs=in_specs + [_ANY], out_specs=out_specs + [_ANY],
        out_shape=out_shape + [jax.ShapeDtypeStruct(shape, send.dtype)], scratch_shapes=_CHIP_SEMS,
        compiler_params=_cparams(("arbitrary", "arbitrary")))(*args, send)


def _attn_fwd(q, kv, *, name, send=None):
    T = q.shape[0]
    blk = min(_ATT_BLK, T)
    nq = T // blk
    parts = _ATT_QPARTS if nq % _ATT_QPARTS == 0 else 1

    def body(q_ref, kv_ref, o_ref, lse_ref):
        i = pl.program_id(1)
        diag = lax.broadcasted_iota(jnp.int32, (blk, blk), 1) <= lax.broadcasted_iota(jnp.int32, (blk, blk), 0)

        def one(part, kb, vb, carry, masked):
            m, l, acc = carry
            s = _dot(q_ref[part * blk:(part + 1) * blk, :], kb, "nt") * _ATT_SCALE2
            if masked:
                s = jnp.where(diag, s, -jnp.inf)
            m_new = jnp.maximum(m, jnp.max(s, axis=-1, keepdims=True))
            alpha = jnp.exp2(m - m_new)
            p = jnp.exp2(s - m_new)
            l = alpha * l + jnp.sum(p, axis=-1, keepdims=True)
            acc = alpha * acc + _dot(p, vb, "nn")
            return m_new, l, acc

        def kv_block(j):
            rows = pl.ds(pl.multiple_of(j * blk, blk), blk)
            return kv_ref[rows, 0:SLAB], kv_ref[rows, SLAB:2 * SLAB]

        def step(j, carry):
            kb, vb = kv_block(j)
            return tuple(one(part, kb, vb, carry[part], False) for part in range(parts))

        init = ((jnp.full((blk, 1), -jnp.inf, F32), jnp.zeros((blk, 1), F32), jnp.zeros((blk, SLAB), F32)),) * parts
        if parts % 2 == 0:
            carry = lax.fori_loop(0, i * parts // 2, lambda t, c: step(2 * t + 1, step(2 * t, c)), init)
        else:
            carry = lax.fori_loop(0, i * parts, step, init)
        for part in range(parts):
            c = carry[part]
            for kk in range(part + 1):
                c = one(part, *kv_block(i * parts + kk), c, kk == part)
            m, l, acc = c
            o_ref[part * blk:(part + 1) * blk, :] = acc / l
            lse_ref[part] = jnp.broadcast_to(m + jnp.log(l) * math.log2(math.e), (blk, SLAB)).T[0:_SUB, :]

    return _call_with_exchange(
        body, send, True,
        name=name,
        grid=(HEADS, nq // parts),
        in_specs=[pl.BlockSpec((parts * blk, SLAB), lambda h, i: (i, h)), pl.BlockSpec((T, 2 * SLAB), lambda h, i: (0, h))],
        out_specs=[pl.BlockSpec((parts * blk, SLAB), lambda h, i: (i, h)),
                   pl.BlockSpec((None, parts, _SUB, blk), lambda h, i: (h, i, 0, 0))],
        out_shape=[jax.ShapeDtypeStruct((T, HEADS * SLAB), F32), jax.ShapeDtypeStruct((HEADS, nq, _SUB, blk), F32)],
        args=(q, kv))


def _attn_delta(do, o, *, name):
    T = o.shape[0]
    blk = min(_ATT_BLK, T)
    nq = T // blk

    def body(do_ref, o_ref, d_ref):
        for h in range(HEADS):
            cols = slice(h * SLAB, (h + 1) * SLAB)
            dl = jnp.sum(do_ref[:, cols].astype(F32) * o_ref[:, cols], axis=-1, keepdims=True)
            d_ref[h] = jnp.broadcast_to(dl, (blk, SLAB)).T[0:_SUB, :]

    return pl.pallas_call(
        body,
        name=name,
        grid=(nq,),
        in_specs=[pl.BlockSpec((blk, HEADS * SLAB), lambda i: (i, 0))] * 2,
        out_specs=pl.BlockSpec((HEADS, None, _SUB, blk), lambda i: (0, i, 0, 0)),
        out_shape=jax.ShapeDtypeStruct((HEADS, nq, _SUB, blk), F32),
        compiler_params=_cparams(("parallel",)),
    )(do, o)


def _attn_bwd(q, kv, do, lse, delta, *, name, send=None):
    T = q.shape[0]
    blk = min(_ATT_BLK, T)
    nq = T // blk

    def body(q_ref, kv_ref, do_ref, lse_ref, dl_ref, dqt_ref, dkv_ref):
        j = pl.program_id(1)

        @pl.when(j == 0)
        def _():
            dqt_ref[...] = jnp.zeros_like(dqt_ref)

        kb, vb = kv_ref[:, 0:SLAB], kv_ref[:, SLAB:2 * SLAB]
        kbt = kb.astype(F32).T.astype(BF16)
        kpos = j * blk + lax.broadcasted_iota(jnp.int32, (blk, blk), 0)

        def step(i, carry, masked):
            dk, dv = carry
            rows = pl.ds(pl.multiple_of(i * blk, blk), blk)
            qb, dob = q_ref[rows, :], do_ref[rows, :]
            st = _dot(kb, qb, "nt") * _ATT_SCALE2
            if masked:
                qpos = i * blk + lax.broadcasted_iota(jnp.int32, (blk, blk), 1)
                st = jnp.where(kpos <= qpos, st, -jnp.inf)
            pt = jnp.exp2(st - lse_ref[i, 0:1, :])
            dpt = _dot(vb, dob, "nt")
            dst = pt * (dpt - dl_ref[i, 0:1, :]) * _ATT_SCALE
            dv = dv + _dot(pt, dob, "nn")
            dk = dk + _dot(dst, qb, "nn")
            dqt_ref[i] += _dot(kbt, dst, "nn")
            return dk, dv

        zero = jnp.zeros((blk, SLAB), F32)
        carry = step(j, (zero, zero), True)
        rest = nq - 1 - j

        def two_steps(t, c):
            i = j + 1 + 2 * t
            return step(i + 1, step(i, c, False), False)

        carry = lax.fori_loop(0, rest // 2, two_steps, carry)
        dk, dv = lax.cond(rest % 2 == 1, lambda c: step(nq - 1, c, False), lambda c: c, carry)
        dkv_ref[:, 0:SLAB] = dk.astype(BF16)
        dkv_ref[:, SLAB:2 * SLAB] = dv.astype(BF16)

    stat_spec = pl.BlockSpec((None, nq, _SUB, blk), lambda h, j: (h, 0, 0, 0))
    return _call_with_exchange(
        body, send, False,
        name=name,
        grid=(HEADS, nq),
        in_specs=[pl.BlockSpec((T, SLAB), lambda h, j: (0, h)), pl.BlockSpec((blk, 2 * SLAB), lambda h, j: (j, h)),
                  pl.BlockSpec((T, SLAB), lambda h, j: (0, h)), stat_spec, stat_spec],
        out_specs=[pl.BlockSpec((None, nq, SLAB, blk), lambda h, j: (h, 0, 0, 0)),
                   pl.BlockSpec((blk, 2 * SLAB), lambda h, j: (j, h))],
        out_shape=[jax.ShapeDtypeStruct((HEADS, nq, SLAB, blk), F32), jax.ShapeDtypeStruct((T, HEADS * 2 * SLAB), BF16)],
        args=(q, kv, do, lse, delta))


def _qrope_bwd(dqt, cq, sq, *, name):
    _, nq, _, blk = dqt.shape
    T = nq * blk

    def body(dqt_ref, c_ref, s_ref, dy_ref):
        c, s = c_ref[...], s_ref[...]
        for h in range(HEADS):
            dq = dqt_ref[h].T
            dy_ref[:, h * SLAB:(h + 1) * SLAB] = (dq * c).astype(BF16)
            dy_ref[:, (HEADS + h) * SLAB:(HEADS + h + 1) * SLAB] = (dq * s).astype(BF16)

    return pl.pallas_call(
        body,
        name=name,
        grid=(nq,),
        in_specs=[pl.BlockSpec((HEADS, None, SLAB, blk), lambda i: (0, i, 0, 0)), _row_spec(cq, blk), _row_spec(sq, blk)],
        out_specs=pl.BlockSpec((blk, 2 * HEADS * SLAB), lambda i: (i, 0)),
        out_shape=jax.ShapeDtypeStruct((T, 2 * HEADS * SLAB), BF16),
        compiler_params=_cparams(("parallel",)),
    )(dqt, cq, sq)


def _loss_head(x, tgt, g, *, name, tm=256):
    T = x.shape[0]
    tm = min(tm, T)

    def body(x_ref, t_ref, g_ref, loss_ref, dx_ref, dxh_ref, dg_ref):
        def f(xv, gv):
            e = _rms(xv, gv, D) - t_ref[...]
            row = jnp.sum(e * e, axis=1, keepdims=True)
            return jnp.sum(row, axis=0, keepdims=True) * (0.5 / D)

        val, vjp = jax.vjp(f, x_ref[...], g_ref[...])
        dxv, dgv = vjp(jnp.ones((1, 1), F32))

        @pl.when(pl.program_id(0) == 0)
        def _():
            loss_ref[...] = jnp.zeros_like(loss_ref)
            dg_ref[...] = jnp.zeros_like(dg_ref)

        dx_ref[...] = dxv
        dxh_ref[...] = dxv.astype(BF16)
        dg_ref[...] += dgv
        loss_ref[...] += jnp.broadcast_to(val, loss_ref.shape)

    return pl.pallas_call(
        body,
        name=name,
        grid=(T // tm,),
        in_specs=[_row_spec(x, tm), _row_spec(tgt, tm), _full_spec(g)],
        out_specs=[pl.BlockSpec((1, SLAB), lambda i: (0, 0)), _row_spec(x, tm), _row_spec(x, tm), _full_spec(g)],
        out_shape=[jax.ShapeDtypeStruct((1, SLAB), F32), jax.ShapeDtypeStruct(x.shape, F32),
                   jax.ShapeDtypeStruct(x.shape, BF16), jax.ShapeDtypeStruct(g.shape, F32)],
        compiler_params=_cparams(("arbitrary",)),
    )(x, tgt, g)


def _row_tile(rows, cols, budget=256 * 1024):
    best = None
    for t in range(16, rows + 1, 16):
        if rows % t == 0 and t * cols <= budget:
            best = t
    return best or rows


def _sum_fixed(x, out_dtype, *, name):
    n, R, C = x.shape
    tr = _row_tile(R, C)

    def body(x_ref, o_ref):
        acc = x_ref[0].astype(F32)
        for k in range(1, n):
            acc = acc + x_ref[k].astype(F32)
        o_ref[...] = acc.astype(o_ref.dtype)

    return pl.pallas_call(
        body,
        name=name,
        grid=(R // tr,),
        in_specs=[pl.BlockSpec((n, tr, C), lambda i: (0, i, 0))],
        out_specs=pl.BlockSpec((tr, C), lambda i: (i, 0)),
        out_shape=jax.ShapeDtypeStruct((R, C), out_dtype),
        compiler_params=_cparams(("parallel",)),
    )(x)


def _adamw(w, g, m, v, *, name):
    R, C = w.shape
    tr = _row_tile(R, C, 128 * 1024)

    def body(w_ref, g_ref, m_ref, v_ref, d_ref, nm_ref, nv_ref):
        gv = g_ref[...]
        mv = B1 * m_ref[...] + (1.0 - B1) * gv
        vv = B2 * v_ref[...] + (1.0 - B2) * (gv * gv)
        m_hat = mv / (1.0 - B1 ** STEP)
        v_hat = vv / (1.0 - B2 ** STEP)
        d_ref[...] = -LR * (m_hat / (jnp.sqrt(v_hat) + AEPS) + WD * w_ref[...])
        nm_ref[...] = mv
        nv_ref[...] = vv

    spec = pl.BlockSpec((tr, C), lambda i: (i, 0))
    return pl.pallas_call(
        body, name=name, grid=(R // tr,), in_specs=[spec] * 4, out_specs=[spec] * 3,
        out_shape=[jax.ShapeDtypeStruct((R, C), F32)] * 3, compiler_params=_cparams(("parallel",)),
    )(w, g, m, v)


_FLIPS = ((1, 0), (0, 1), (1, 1))
_ANY = pl.BlockSpec(memory_space=pl.ANY)


def _me():
    return lax.axis_index("x"), lax.axis_index("y"), lax.axis_index("c")


def _flip(mx, my, f):
    return (1 - mx if f[0] else mx), (1 - my if f[1] else my)


_CHIP_SEMS = [pltpu.SemaphoreType.DMA((3,)), pltpu.SemaphoreType.DMA((3,)), pltpu.SemaphoreType.DMA]


def _chip_copies(x_ref, out_ref, send_sems, recv_sems, local_sem, gather):
    mx, my, mc = _me()
    mine = 2 * mx + my
    outgoing, incoming = [], []
    for k, f in enumerate(_FLIPS):
        px, py = _flip(mx, my, f)
        peer = 2 * px + py
        src = x_ref if gather else x_ref.at[peer]
        for dst, to in ((out_ref.at[mine], outgoing), (out_ref.at[peer], incoming)):
            to.append(pltpu.make_async_remote_copy(src_ref=src, dst_ref=dst, send_sem=send_sems.at[k],
                                                   recv_sem=recv_sems.at[k], device_id=(px, py, mc), device_id_type=MESH))
    local = None if gather else pltpu.make_async_copy(x_ref.at[mine], out_ref.at[mine], local_sem)
    return outgoing, incoming, local


def _chip_start(*refs, gather):
    outgoing, _, local = _chip_copies(*refs, gather)
    if local is not None:
        local.start()
    for cp in outgoing:
        cp.start()


def _chip_wait(*refs, gather):
    outgoing, incoming, local = _chip_copies(*refs, gather)
    for cp in incoming:
        cp.wait_recv()
    for cp in outgoing:
        cp.wait_send()
    if local is not None:
        local.wait()


def _chip_exchange(x, gather, *, name):
    shape = x.shape if not gather else (4,) + x.shape

    def body(*refs):
        _chip_start(*refs, gather=gather)
        _chip_wait(*refs, gather=gather)

    return pl.pallas_call(
        body, name=name, in_specs=[_ANY], out_specs=_ANY, out_shape=jax.ShapeDtypeStruct(shape, x.dtype),
        scratch_shapes=_CHIP_SEMS,
    )(x)


def _core_exchange(x, half, *, name):
    shape = x.shape[1:] if half else x.shape

    def body(x_ref, out_ref, send_sem, recv_sem):
        mx, my, mc = _me()
        src = x_ref.at[1 - mc] if half else x_ref
        cp = pltpu.make_async_remote_copy(src_ref=src, dst_ref=out_ref, send_sem=send_sem, recv_sem=recv_sem,
                                          device_id=(mx, my, 1 - mc), device_id_type=MESH)
        cp.start()
        cp.wait()

    return pl.pallas_call(
        body, name=name, in_specs=[_ANY], out_specs=_ANY, out_shape=jax.ShapeDtypeStruct(shape, x.dtype),
        scratch_shapes=[pltpu.SemaphoreType.DMA, pltpu.SemaphoreType.DMA],
    )(x)


def _all_gather(x, *, name):
    return _finish_gather(x, _chip_exchange(x, True, name=name + "_chips"), name=name)


def _finish_gather(x, g4, *, name):
    mx, my, mc = _me()
    own = (jnp.arange(4) == 2 * mx + my).reshape((4,) + (1,) * x.ndim)
    g4 = jnp.where(own, x[None], g4)
    sib = _core_exchange(g4, False, name=name + "_cores")
    return jnp.where(mc == 0, jnp.stack([g4, sib]), jnp.stack([sib, g4]))


def _add_own_half(x, got, *, name):
    _, R, C = x.shape
    tr = _row_tile(R, C)

    def body(c_ref, x_ref, g_ref, o_ref):
        o_ref[...] = (x_ref[...] + g_ref[...]).astype(o_ref.dtype)

    return pl.pallas_call(
        body,
        name=name,
        grid_spec=pltpu.PrefetchScalarGridSpec(
            num_scalar_prefetch=1, grid=(R // tr,),
            in_specs=[pl.BlockSpec((None, tr, C), lambda i, c: (c[0], i, 0)), pl.BlockSpec((tr, C), lambda i, c: (i, 0))],
            out_specs=pl.BlockSpec((tr, C), lambda i, c: (i, 0))),
        out_shape=jax.ShapeDtypeStruct((R, C), BF16),
        compiler_params=_cparams(("parallel",)),
    )(lax.axis_index("c").astype(jnp.int32).reshape(1), x, got)


def _chip_sums(x, *, name):
    _, _, R, C = x.shape
    got = _core_exchange(x, True, name=name + "_cores")
    return _add_own_half(x.reshape(2, 4 * R, C), got.reshape(4 * R, C), name=name + "_add").reshape(4, R, C)


def _all_reduce(x, *, name):
    g = _all_gather(x, name=name)
    return _sum_fixed(g.reshape((N_DEV,) + x.shape), F32, name=name + "_sum")


WEIGHTS = ['mix_norm_g', 'w_in', 'mla_q_norm_g', 'mla_kv_norm_g', 'mla_w_uq', 'mla_w_ukv', 'mla_out_g', 'ssm_conv_w',
           'ssm_conv_b', 'ssm_dt_bias', 'ssm_a_log', 'ssm_d', 'ssm_norm_g', 'lru_conv_w', 'lru_conv_b', 'lru_w_a',
           'lru_b_a', 'lru_w_i', 'lru_b_i', 'lru_lambda', 'lru_out_g', 'w_out', 'xattn_norm_g', 'mem_norm_g', 'w_mq',
           'w_mk', 'w_mv', 'w_mo', 'mlp_norm_g', 'w_mlp1', 'w_mlp2', 'final_norm_g']
ROW_SHARDED = ('w_in', 'w_out', 'w_mq', 'w_mk', 'w_mv', 'w_mo', 'w_mlp2')
COL_SHARDED = ('mla_w_uq', 'mla_w_ukv', 'w_mlp1')
BIG = tuple(n for n in WEIGHTS if n in ROW_SHARDED + COL_SHARDED)
CONV_SHARDED = ('ssm_conv_w', 'lru_conv_w')
SMALL = tuple(n for n in WEIGHTS if n not in BIG)
PACK_C = 1024


def _pack(arrs, dtype, lead=()):
    flat = jnp.concatenate([a.reshape(lead + (-1,)).astype(dtype) for a in arrs], axis=-1)
    n = flat.shape[-1]
    rows = -(-n // (16 * PACK_C)) * 16
    flat = jnp.pad(flat, [(0, 0)] * len(lead) + [(0, rows * PACK_C - n)])
    return flat.reshape(lead + (rows, PACK_C))


def _unpack(packed, shapes, lead=()):
    flat = packed.reshape(lead + (-1,))
    out, off = [], 0
    for s in shapes:
        n = math.prod(s)
        out.append(flat[..., off:off + n].reshape(lead + tuple(s)))
        off += n
    return out


def _pad_lanes(v, n=SLAB):
    return jnp.pad(v.astype(F32), (0, n - v.shape[0])).reshape(1, n)


PIECES = ('win', 'wq', 'wkv', 'wout', 'w_mq', 'w_mk', 'w_mv', 'w_mo', 'w_mlp1', 'w_mlp2')
PIECE_SOURCE = {'win': 'w_in', 'wq': 'mla_w_uq', 'wkv': 'mla_w_ukv', 'wout': 'w_out'}
PIECE_SHAPE = {'win': (128, 2048), 'wq': (Q_RANK, 2 * SLAB), 'wkv': (KV_RANK, 2 * SLAB), 'wout': (128, D),
               'w_mq': (128, D), 'w_mk': (128, D), 'w_mv': (128, D), 'w_mo': (128, D), 'w_mlp1': (D, 512),
               'w_mlp2': (512, D)}
PIECE_COLS = ('wq', 'wkv', 'w_mlp1')


def _k_win(w):
    kr = w[..., 384:416]
    zc = lambda k: jnp.zeros(w.shape[:-1] + (k,), w.dtype)
    return jnp.concatenate(
        [w[..., 0:384], kr, zc(96), kr[..., 16:32], kr[..., 0:16], zc(96),
         w[..., 416:672], w[..., 1444:1700], w[..., 1184:1188], zc(124),
         w[..., 672:1184], w[..., 1188:1444]], axis=-1)


def _k_win_inv(m):
    gk = m[..., 384:416] + jnp.concatenate([m[..., 528:544], m[..., 512:528]], axis=-1)
    return jnp.concatenate([m[..., 0:384], gk, m[..., 640:896], m[..., 1280:1792], m[..., 1152:1156], m[..., 1792:2048],
                            m[..., 896:1152]], axis=-1)


def _k_wq(w):
    nh = w.shape[-1] // (NOPE + ROPE)
    w = w.reshape(w.shape[:-1] + (nh, NOPE + ROPE))
    nope, r1, r2 = w[..., :NOPE], w[..., NOPE:NOPE + 16], w[..., NOPE + 16:]
    z = lambda k: jnp.zeros(w.shape[:-1] + (k,), w.dtype)
    both = jnp.stack([jnp.concatenate([nope, r1, r2, z(32)], -1), jnp.concatenate([z(64), r2, r1, z(32)], -1)], axis=-3)
    return both.reshape(w.shape[:-2] + (2 * nh * SLAB,))


def _k_wq_inv(m):
    nh = m.shape[-1] // (2 * SLAB)
    m = m.reshape(m.shape[:-1] + (2, nh, SLAB))
    q0, q1 = m[..., 0, :, :], m[..., 1, :, :]
    w = jnp.concatenate([q0[..., :64], q0[..., 64:80] + q1[..., 80:96], q0[..., 80:96] + q1[..., 64:80]], -1)
    return w.reshape(w.shape[:-2] + (nh * (NOPE + ROPE),))


def _k_wkv(w):
    nh = w.shape[-1] // (NOPE + VDIM)
    w = w.reshape(w.shape[:-1] + (nh, NOPE + VDIM))
    z = jnp.zeros(w.shape[:-1] + (64,), w.dtype)
    return jnp.concatenate([w[..., :NOPE], z, w[..., NOPE:], z], -1).reshape(w.shape[:-2] + (nh * 2 * SLAB,))


def _k_wkv_inv(m):
    nh = m.shape[-1] // (2 * SLAB)
    m = m.reshape(m.shape[:-1] + (nh, 2 * SLAB))
    return jnp.concatenate([m[..., :NOPE], m[..., SLAB:SLAB + VDIM]], -1).reshape(m.shape[:-2] + (nh * (NOPE + VDIM),))


_K_FWD = {'win': _k_win, 'wq': _k_wq, 'wkv': _k_wkv}
_K_INV = {'win': _k_win_inv, 'wq': _k_wq_inv, 'wkv': _k_wkv_inv}


def _assemble(piece, g):
    _, _, a, b = g.shape
    if piece == 'wq':
        return g.reshape(2, 4, a, 2, SLAB).transpose(2, 3, 1, 0, 4).reshape(a, N_DEV * b)
    if piece in PIECE_COLS:
        return g.transpose(2, 1, 0, 3).reshape(a, N_DEV * b)
    return g.transpose(1, 0, 2, 3).reshape(N_DEV * a, b)


def _disassemble(piece, full):
    a, b = PIECE_SHAPE[piece]
    if piece == 'wq':
        return full.reshape(a, 2, 4, 2, SLAB).transpose(3, 2, 0, 1, 4).reshape(2, 4, a, b)
    if piece in PIECE_COLS:
        return full.reshape(a, 4, 2, b).transpose(2, 1, 0, 3)
    return full.reshape(4, 2, a, b).transpose(1, 0, 2, 3)


def _piece_rows(piece):
    a, b = PIECE_SHAPE[piece]
    return a * b // PACK_C


def _prep_layer(pieces, Ws, l):
    P = dict(pieces)
    ri, ci = jnp.arange(SLAB)[:, None], jnp.arange(2 * SLAB)[None, :]
    sel = ((ri < ROPE) & (ci == ri + NOPE)).astype(P['wkv'].dtype)
    P['wkv'] = jnp.concatenate([P['wkv'], jnp.tile(sel, (1, HEADS))], axis=0)
    wout = P['wout']
    mla_rows = jnp.pad(wout[:HEADS * VDIM].reshape(HEADS, VDIM, D), ((0, 0), (0, SLAB - VDIM), (0, 0)))
    P['wout'] = jnp.concatenate([mla_rows.reshape(HEADS * SLAB, D), wout[HEADS * VDIM:]], axis=0)
    W = Ws
    row = lambda n: W[n][l].astype(F32).reshape(1, -1)
    for n in ('mix_norm_g', 'mla_q_norm_g', 'mla_kv_norm_g', 'ssm_norm_g', 'lru_lambda', 'lru_out_g', 'xattn_norm_g',
              'mem_norm_g', 'mlp_norm_g'):
        P[n] = row(n)
    P['mla_out_g'] = jnp.pad(W['mla_out_g'][l].astype(F32).reshape(HEADS, VDIM), ((0, 0), (0, SLAB - VDIM))).reshape(1, -1)
    for n in ('ssm_dt_bias', 'ssm_a_log', 'ssm_d'):
        P[n] = _pad_lanes(W[n][l])
    P['conv_w'] = jnp.pad(jnp.concatenate([W['ssm_conv_w'][l], W['lru_conv_w'][l]], axis=1).astype(F32), ((0, 4), (0, 0)))
    P['conv_b'] = jnp.concatenate([W['ssm_conv_b'][l], W['lru_conv_b'][l]]).astype(F32).reshape(1, -1)
    for n in ('lru_w_a', 'lru_w_i'):
        P[n] = jnp.concatenate([jnp.pad(W[n][l, k].astype(F32), ((0, 0), (64 * k, 192 - 64 * k))) for k in range(4)], axis=0)
    for n in ('lru_b_a', 'lru_b_i'):
        P[n] = W[n][l].astype(F32).reshape(1, -1)
    return P


def _unprep_pieces(G):
    o = {n: G[n] for n in PIECES}
    o['wkv'] = G['wkv'][:KV_RANK]
    wo = G['wout']
    o['wout'] = jnp.concatenate([wo[:HEADS * SLAB].reshape(HEADS, SLAB, D)[:, :VDIM].reshape(HEADS * VDIM, D),
                                 wo[HEADS * SLAB:]], axis=0)
    return o


def _unprep_small(G):
    o = {}
    for n in ('mix_norm_g', 'mla_q_norm_g', 'mla_kv_norm_g', 'ssm_norm_g', 'lru_lambda', 'lru_out_g', 'xattn_norm_g',
              'mem_norm_g', 'mlp_norm_g', 'lru_b_a', 'lru_b_i'):
        o[n] = G[n].reshape(-1)
    o['lru_b_a'] = o['lru_b_a'].reshape(4, 64)
    o['lru_b_i'] = o['lru_b_i'].reshape(4, 64)
    o['mla_out_g'] = G['mla_out_g'].reshape(HEADS, SLAB)[:, :VDIM].reshape(-1)
    for n in ('ssm_dt_bias', 'ssm_a_log', 'ssm_d'):
        o[n] = G[n][0, :4]
    o['ssm_conv_w'], o['lru_conv_w'] = G['conv_w'][:4, :512], G['conv_w'][:4, 512:]
    o['ssm_conv_b'], o['lru_conv_b'] = G['conv_b'][0, :512], G['conv_b'][0, 512:]
    for n in ('lru_w_a', 'lru_w_i'):
        o[n] = jnp.stack([G[n][64 * k:64 * (k + 1), 64 * k:64 * (k + 1)] for k in range(4)])
    return o


def _rope_tables(positions):
    half = ROPE // 2
    inv_freq = ROPE_THETA ** (-jnp.arange(half, dtype=F32) * 2.0 / ROPE)
    ang = positions.astype(F32)[:, None] * inv_freq
    cos, sin = jnp.cos(ang), jnp.sin(ang)
    T = positions.shape[0]
    z = lambda k: jnp.zeros((T, k), F32)
    ck = jnp.concatenate([cos, cos, z(96)], axis=1)
    sk = jnp.concatenate([-sin, sin, z(96)], axis=1)
    cq = jnp.concatenate([jnp.ones((T, NOPE), F32), cos, cos, z(32)], axis=1)
    sq = jnp.concatenate([z(NOPE), -sin, sin, z(32)], axis=1)
    return ck, sk, cq, sq


def _add_epi(acc, res):
    return (acc + res,)


def _relu2_epi(acc):
    r = jnp.maximum(acc, 0.0)
    return r, r * r


def _drelu2_epi(acc, r):
    return (acc * (2.0 * r.astype(F32)),)


def _norm(x, g, name):
    return _rows_fwd(_f_norm, [x], [g], [(x.shape[1], BF16)], name=name)[0]


def _norm_bwd(x, g, ct, add, name):
    (dx, dx16), (dg,) = _rows_vjp(_f_norm, [x], [g], [ct], name=name, drows=[0], dparams=[0], drow_dtypes=[F32],
                                  add=add, twin=True)
    return dx, dx16, dg


def _layer_fwd(x0, mem, P, tabs, send=None):
    ck, sk, cq, sq = tabs
    S = {'x0': x0}
    h1 = S['h1'] = _norm(x0, P['mix_norm_g'], "norm_mix")
    win = P['win']
    u_mla = S['u_mla'] = _mm(h1, win[:, 0:U_MLA], name="in_mla")
    u_gate = S['u_gate'] = _mm(h1, win[:, U_MLA:U_MLA + U_GATE], name="in_gate")
    u_conv = S['u_conv'] = _mm(h1, win[:, U_MLA + U_GATE:], name="in_conv")
    cqn, akv = _rows_fwd(_f_mla_prep, [u_mla, ck, sk], [P['mla_q_norm_g'], P['mla_kv_norm_g']],
                         [(Q_RANK, BF16), (2 * SLAB, BF16)], name="mla_prep")
    S['cqn'], S['akv'] = cqn, akv
    yq = _mm(cqn, P['wq'], name="q_proj")
    q = S['q'] = _rows_fwd(_f_qrope, [yq, cq, sq], [], [(HEADS * SLAB, BF16)], name="q_rope")[0]
    kv = S['kv'] = _mm(akv, P['wkv'], name="kv_proj", out_dtypes=(BF16,))
    o, lse, *got = _attn_fwd(q, kv, name="attn_fwd" if send is None else "attn_fwd_gather", send=send)
    S['o'], S['lse'] = o, lse
    c_ssm, c_lru = _conv_fwd(u_conv, P['conv_w'], P['conv_b'], name="conv_fwd")
    S['c_ssm'], S['c_lru'] = c_ssm, c_lru
    ys, sall = _ssd_fwd(c_ssm, u_gate, P['ssm_dt_bias'], P['ssm_a_log'], P['ssm_d'], P['ssm_norm_g'], name="ssd_fwd")
    S['ys'], S['sall'] = ys, sall
    a, b = _rows_fwd(_f_lru_gates, [c_lru], [P['lru_w_a'], P['lru_b_a'], P['lru_w_i'], P['lru_b_i'], P['lru_lambda']],
                     [(256, F32), (256, F32)], name="lru_gates", tm=_MM_ROWS)
    h, hprev = _lru_scan_fwd(a, b, name="lru_scan")
    S['a'], S['h'], S['hprev'] = a, h, hprev
    ymix = S['ymix'] = _rows_fwd(_f_mix, [o, ys, h, u_gate], [P['mla_out_g'], P['lru_out_g']],
                                 [(HEADS * SLAB + 512, BF16)], name="mix")[0]
    x1 = S['x1'] = _mm(ymix, P['wout'], name="out_proj", epi=_add_epi, extras=(x0,))
    hx = S['hx'] = _norm(x1, P['xattn_norm_g'], "norm_xattn")
    qx = S['qx'] = _mm(hx, P['w_mq'], name="mem_q", out_dtypes=(BF16,))
    mn = S['mn'] = _norm(mem, P['mem_norm_g'], "norm_mem")
    kx = S['kx'] = _mm(mn, P['w_mk'], name="mem_k", out_dtypes=(BF16,))
    vx = S['vx'] = _mm(mn, P['w_mv'], name="mem_v", out_dtypes=(BF16,))
    ox = S['ox'] = _rows_fwd(_f_xattn, [qx], [kx, vx], [(D, BF16)], name="xattn", tm=_MM_ROWS)[0]
    x2 = S['x2'] = _mm(ox, P['w_mo'], name="mem_o", epi=_add_epi, extras=(x1,))
    hm = S['hm'] = _norm(x2, P['mlp_norm_g'], "norm_mlp")
    r, s = _mm(hm, P['w_mlp1'], name="mlp_up", epi=_relu2_epi, out_dtypes=(BF16, BF16))
    S['r'], S['s'] = r, s
    x3 = _mm(s, P['w_mlp2'], name="mlp_down", epi=_add_epi, extras=(x2,))
    return x3, S, (got[0] if got else None)


def _layer_bwd(dx3, dx3h, mem, S, P, tabs, send=None):
    ck, sk, cq, sq = tabs
    G = {}
    da = _mm(dx3h, P['w_mlp2'], "nt", name="mlp_down_dx", epi=_drelu2_epi, extras=(S['r'],), out_dtypes=(BF16,))
    G['w_mlp2'] = _mm(S['s'], dx3h, "tn", name="mlp_down_dw")
    G['w_mlp1'] = _mm(S['hm'], da, "tn", name="mlp_up_dw")
    dhm = _mm(da, P['w_mlp1'], "nt", name="mlp_up_dx")
    dx2, dx2h, G['mlp_norm_g'] = _norm_bwd(S['x2'], P['mlp_norm_g'], dhm, dx3, "norm_mlp_bwd")
    dox = _mm(dx2h, P['w_mo'], "nt", name="mem_o_dx")
    G['w_mo'] = _mm(S['ox'], dx2h, "tn", name="mem_o_dw")
    (dqx,), (dkx, dvx) = _rows_vjp(_f_xattn, [S['qx']], [S['kx'], S['vx']], [dox], name="xattn_bwd", drows=[0],
                                   dparams=[0, 1], drow_dtypes=[BF16], tm=_MM_ROWS)
    G['w_mq'] = _mm(S['hx'], dqx, "tn", name="mem_q_dw")
    dhx = _mm(dqx, P['w_mq'], "nt", name="mem_q_dx")
    dx1, dx1h, G['xattn_norm_g'] = _norm_bwd(S['x1'], P['xattn_norm_g'], dhx, dx2, "norm_xattn_bwd")
    G['w_mk'] = _mm(S['mn'], dkx, "tn", name="mem_k_dw")
    G['w_mv'] = _mm(S['mn'], dvx, "tn", name="mem_v_dw")
    dmn = _mm(dkx, P['w_mk'], "nt", name="mem_k_dx", epi=_add_epi, extras=(_mm(dvx, P['w_mv'], "nt", name="mem_v_dx"),))
    _, _, G['mem_norm_g'] = _norm_bwd(mem, P['mem_norm_g'], dmn, None, "norm_mem_bwd")
    dymix = _mm(dx1h, P['wout'], "nt", name="out_proj_dx")
    G['wout'] = _mm(S['ymix'], dx1h, "tn", name="out_proj_dw")
    (do, dys, dh, dug_mix), (G['mla_out_g'], G['lru_out_g']) = _rows_vjp(
        _f_mix, [S['o'], S['ys'], S['h'], S['u_gate']], [P['mla_out_g'], P['lru_out_g']], [dymix], name="mix_bwd",
        drows=[0, 1, 2, 3], dparams=[0, 1], drow_dtypes=[BF16, F32, F32, F32])
    g, da_lru = _lru_scan_bwd(S['a'], dh, S['hprev'], name="lru_scan_bwd")
    lru_par = [P['lru_w_a'], P['lru_b_a'], P['lru_w_i'], P['lru_b_i'], P['lru_lambda']]
    (dc_lru,), dpar = _rows_vjp(_f_lru_gates, [S['c_lru']], lru_par, [da_lru, g], name="lru_gates_bwd", drows=[0],
                                dparams=[0, 1, 2, 3, 4], drow_dtypes=[F32], tm=_MM_ROWS)
    G['lru_w_a'], G['lru_b_a'], G['lru_w_i'], G['lru_b_i'], G['lru_lambda'] = dpar
    dc_ssm, dug_ssd, G['ssm_dt_bias'], G['ssm_a_log'], G['ssm_d'], G['ssm_norm_g'] = _ssd_bwd(
        S['c_ssm'], S['u_gate'], S['sall'], dys, P['ssm_dt_bias'], P['ssm_a_log'], P['ssm_d'], P['ssm_norm_g'],
        name="ssd_bwd")
    du_conv, G['conv_w'], G['conv_b'] = _conv_bwd(S['u_conv'], dc_ssm, dc_lru, P['conv_w'], name="conv_bwd")
    delta = _attn_delta(do, S['o'], name="attn_delta")
    dqt, dkv, *got = _attn_bwd(S['q'], S['kv'], do, S['lse'], delta,
                               name="attn_bwd" if send is None else "attn_bwd_scatter", send=send)
    dyq = _qrope_bwd(dqt, cq, sq, name="q_rope_bwd")
    dcqn = _mm(dyq, P['wq'], "nt", name="q_proj_dx")
    G['wq'] = _mm(S['cqn'], dyq, "tn", name="q_proj_dw")
    dakv = _mm(dkv, P['wkv'], "nt", name="kv_proj_dx")
    G['wkv'] = _mm(S['akv'], dkv, "tn", name="kv_proj_dw")
    (du_mla,), (G['mla_q_norm_g'], G['mla_kv_norm_g']) = _rows_vjp(
        _f_mla_prep, [S['u_mla'], ck, sk], [P['mla_q_norm_g'], P['mla_kv_norm_g']], [dcqn, dakv], name="mla_prep_bwd",
        drows=[0], dparams=[0, 1], drow_dtypes=[BF16])
    du = jnp.concatenate([du_mla, (dug_mix + dug_ssd).astype(BF16), du_conv.astype(BF16)], axis=1)
    dh1 = _mm(du, P['win'], "nt", name="in_dx")
    G['win'] = _mm(S['h1'], du, "tn", name="in_dw")
    dx0, dx0h, G['mix_norm_g'] = _norm_bwd(S['x0'], P['mix_norm_g'], dh1, dx1, "norm_mix_bwd")
    return dx0, dx0h, G, (got[0] if got else None)


class _NoExchange:
    def __init__(self, layers):
        self.layers = layers

    def pieces(self, l):
        return self.layers[l]

    def fwd_send(self, l):
        return None

    def fwd_got(self, l, got):
        pass

    def bwd_send(self, l):
        return None

    def bwd_got(self, l, got):
        pass

    def grads_ready(self, l, pieces):
        pass


def _local_step(x, mem, positions, ex, Ws, tgt):
    tabs = _rope_tables(positions)
    saved, preps = [], []
    for l in range(DEPTH):
        P = _prep_layer(ex.pieces(l), Ws, l)
        send = ex.fwd_send(l)
        x, S, got = _layer_fwd(x, mem, P, tabs, send)
        if send is not None:
            ex.fwd_got(l, got)
        saved.append(S)
        preps.append(P)
    loss, dx, dxh, dg_final = _loss_head(x, tgt, Ws['final_norm_g'].astype(F32).reshape(1, D), name="loss_head")
    pieces, small = [None] * DEPTH, [None] * DEPTH
    for l in reversed(range(DEPTH)):
        send = ex.bwd_send(l)
        dx, dxh, G, got = _layer_bwd(dx, dxh, mem, saved[l], preps[l], tabs, send)
        if send is not None:
            ex.bwd_got(l, got)
        pieces[l], small[l] = _unprep_pieces(G), _unprep_small(G)
        ex.grads_ready(l, pieces[l])
    grads = {n: jnp.stack([small[l][n] for l in range(DEPTH)]) for n in SMALL if n != 'final_norm_g'}
    grads['final_norm_g'] = dg_final.reshape(D)
    return loss, dx, pieces, grads


def _pack_rows(pieces):
    return jnp.concatenate([pieces[n].reshape(pieces[n].shape[:-2] + (-1, PACK_C)) for n in PIECES], axis=-2)


def _unpack_rows(packed, lead=()):
    out, off = {}, 0
    for n in PIECES:
        rows = _piece_rows(n)
        out[n] = packed[..., off:off + rows, :].reshape(lead + PIECE_SHAPE[n])
        off += rows
    return out


class _StepExchange(_NoExchange):
    def __init__(self, shard):
        self.shard = {n: a.astype(BF16) for n, a in shard.items()}
        self.layers = {}
        self.sums, self.reduced = {}, {}
        self._take(_all_gather(_pack_rows({n: a[0] for n, a in self.shard.items()}), name="gather_w0"), [0])

    def _take(self, gathered, layers):
        per = gathered.reshape(2, 4, len(layers), -1, PACK_C)
        for k, l in enumerate(layers):
            self.layers[l] = {n: _assemble(n, g) for n, g in _unpack_rows(per[:, :, k], lead=(2, 4)).items()}

    def _rest(self):
        return _pack_rows({n: a[1:] for n, a in self.shard.items()}).reshape(-1, PACK_C)

    def fwd_send(self, l):
        return self._rest() if l == 0 else None

    def fwd_got(self, l, got):
        self._take(_finish_gather(self._rest(), got, name="gather_w"), list(range(1, DEPTH)))

    def grads_ready(self, l, pieces):
        x = _pack_rows({n: _disassemble(n, pieces[n]) for n in PIECES})
        self.sums[l] = _chip_sums(x, name="scatter_g")
        if l == 0:
            self.bwd_got(-1, _chip_exchange(self.sums.pop(0), False, name="scatter_g_chips"))

    def bwd_send(self, l):
        return self.sums.pop(l + 1, None)

    def bwd_got(self, l, got):
        self.reduced[l + 1] = _sum_fixed(got, F32, name="scatter_g_sum")


def _adamw_nd(w, g, m, v, name):
    shp = w.shape
    two = lambda a: a.reshape(-1, shp[-1])
    return [r.reshape(shp) for r in _adamw(two(w), two(g), two(m), two(v), name=name)]


def kernel(x, mem, positions, mix_norm_g, w_in, mla_q_norm_g, mla_kv_norm_g, mla_w_uq, mla_w_ukv, mla_out_g, ssm_conv_w, ssm_conv_b, ssm_dt_bias, ssm_a_log, ssm_d, ssm_norm_g, lru_conv_w, lru_conv_b, lru_w_a, lru_b_a, lru_w_i, lru_b_i, lru_lambda, lru_out_g, w_out, xattn_norm_g, mem_norm_g, w_mq, w_mk, w_mv, w_mo, mlp_norm_g, w_mlp1, w_mlp2, final_norm_g, loss_target, m_mix_norm_g, m_w_in, m_mla_q_norm_g, m_mla_kv_norm_g, m_mla_w_uq, m_mla_w_ukv, m_mla_out_g, m_ssm_conv_w, m_ssm_conv_b, m_ssm_dt_bias, m_ssm_a_log, m_ssm_d, m_ssm_norm_g, m_lru_conv_w, m_lru_conv_b, m_lru_w_a, m_lru_b_a, m_lru_w_i, m_lru_b_i, m_lru_lambda, m_lru_out_g, m_w_out, m_xattn_norm_g, m_mem_norm_g, m_w_mq, m_w_mk, m_w_mv, m_w_mo, m_mlp_norm_g, m_w_mlp1, m_w_mlp2, m_final_norm_g, v_mix_norm_g, v_w_in, v_mla_q_norm_g, v_mla_kv_norm_g, v_mla_w_uq, v_mla_w_ukv, v_mla_out_g, v_ssm_conv_w, v_ssm_conv_b, v_ssm_dt_bias, v_ssm_a_log, v_ssm_d, v_ssm_norm_g, v_lru_conv_w, v_lru_conv_b, v_lru_w_a, v_lru_b_a, v_lru_w_i, v_lru_b_i, v_lru_lambda, v_lru_out_g, v_w_out, v_xattn_norm_g, v_mem_norm_g, v_w_mq, v_w_mk, v_w_mv, v_w_mo, v_mlp_norm_g, v_w_mlp1, v_w_mlp2, v_final_norm_g):
    a = locals()
    w = {n: a[n] for n in WEIGHTS}
    m = {n: a['m_' + n] for n in WEIGHTS}
    v = {n: a['v_' + n] for n in WEIGHTS}
    me = 4 * lax.axis_index("x") + 2 * lax.axis_index("y") + lax.axis_index("c")

    ex = _StepExchange({n: _K_FWD[n](w[PIECE_SOURCE[n]]) if n in _K_FWD else w[PIECE_SOURCE.get(n, n)] for n in PIECES})
    Ws = {}
    conv_shapes = [w[n].shape for n in CONV_SHARDED]
    conv_g = _all_gather(_pack([w[n] for n in CONV_SHARDED], F32), name="gather_conv")
    conv_g = conv_g.transpose(1, 0, 2, 3).reshape((N_DEV,) + conv_g.shape[2:])
    for n, g in zip(CONV_SHARDED, _unpack(conv_g, conv_shapes, lead=(N_DEV,))):
        Ws[n] = g.transpose(1, 2, 0, 3).reshape(g.shape[1], g.shape[2], N_DEV * g.shape[3])
    for n in SMALL:
        if n not in CONV_SHARDED:
            Ws[n] = w[n]

    loss_share, dx, _, grads = _local_step(x[0], mem[0], positions[0], ex, Ws, loss_target[0])
    loss = lax.psum(loss_share[0, 0], ("x", "y", "c"))

    g_out = {}
    for n in PIECES:
        g = jnp.stack([_unpack_rows(ex.reduced[l])[n] for l in range(DEPTH)])
        g_out[PIECE_SOURCE.get(n, n)] = _K_INV[n](g) if n in _K_INV else g
    small_shapes = [grads[n].shape for n in SMALL]
    g_small = _all_reduce(_pack([grads[n] for n in SMALL], F32), name="reduce_g")
    for n, g in zip(SMALL, _unpack(g_small, small_shapes)):
        if n in CONV_SHARDED:
            cols = w[n].shape[-1]
            g = lax.dynamic_slice_in_dim(g, me * cols, cols, axis=2)
        g_out[n] = g

    delta, new_m, new_v = {}, {}, {}
    for n in BIG:
        delta[n], new_m[n], new_v[n] = _adamw_nd(w[n], g_out[n], m[n], v[n], "adamw_" + n)
    shapes = [w[n].shape for n in SMALL]
    packed = [_pack([d[n] for n in SMALL], F32) for d in (w, g_out, m, v)]
    for d, res in zip((delta, new_m, new_v), _adamw(*packed, name="adamw_small")):
        d.update(zip(SMALL, _unpack(res, shapes)))

    return (loss, dx[None], *[g_out[n] for n in WEIGHTS], *[delta[n] for n in WEIGHTS],
            *[new_m[n] for n in WEIGHTS], *[new_v[n] for n in WEIGHTS])
```

```python
import functools
import math

import jax
import jax.numpy as jnp
from jax import lax
from jax.experimental import pallas as pl
from jax.experimental.pallas import tpu as pltpu

F32, BF16 = jnp.float32, jnp.bfloat16

D = 1024
DEPTH = 4
N_MEM = 256
EPS = 1e-6
HEADS = 8
NOPE, ROPE, VDIM = 64, 32, 64
Q_RANK, KV_RANK = 256, 128
ROPE_THETA = 10000.0
SSM_CHUNK = 128
LRU_C = 8.0
MEM_HEADS = 4
D_FF = 4 * D
SLAB = 128
LR, B1, B2, AEPS, WD, STEP = 0.001, 0.9, 0.999, 1e-08, 0.01, 10

N_DEV = 8
MESH = pl.DeviceIdType.MESH

U_MLA = 640
U_GATE = 640
U_CONV = 768

_DN = {"nn": (((1,), (0,)), ((), ())), "nt": (((1,), (1,)), ((), ())), "tn": (((0,), (0,)), ((), ()))}


def _dot(a, b, kind):
    return lax.dot_general(a.astype(BF16), b.astype(BF16), _DN[kind], preferred_element_type=F32)


@functools.partial(jax.custom_vjp, nondiff_argnums=(2,))
def _bdot(a, b, kind):
    return _dot(a, b, kind)


def _bdot_fwd(a, b, kind):
    return _dot(a, b, kind), (a, b)


def _bdot_bwd(kind, res, g):
    a, b = res
    if kind == "nn":
        da, db = _dot(g, b, "nt"), _dot(a, g, "tn")
    elif kind == "nt":
        da, db = _dot(g, b, "nn"), _dot(g, a, "tn")
    else:
        da, db = _dot(b, g, "nt"), _dot(a, g, "nn")
    return da.astype(a.dtype), db.astype(b.dtype)


_bdot.defvjp(_bdot_fwd, _bdot_bwd)


def _tile(n, pref):
    if n <= pref:
        return n
    t = pref
    while n % t:
        t -= SLAB
    return t


def _cparams(sem, vmem_mb=48):
    return pltpu.CompilerParams(dimension_semantics=sem, vmem_limit_bytes=vmem_mb * 1024 * 1024)


def _mm(a, b, kind="nn", *, name, out_dtypes=(F32,), epi=None, extras=(), col_sums=0, tm=1024, tn=1024, tk=1024):
    if kind == "tn":
        K, M = a.shape
    else:
        M, K = a.shape
    N = b.shape[0] if kind == "nt" else b.shape[1]
    if a.dtype == F32 or b.dtype == F32:
        tk = tk // 2
    tm, tn, tk = _tile(M, tm), _tile(N, tn), _tile(K, tk)
    assert not col_sums or tn == N, "column sums need one tile across the columns"
    nk = K // tk
    a_spec = pl.BlockSpec((tk, tm), lambda i, j, k: (k, i)) if kind == "tn" else pl.BlockSpec((tm, tk), lambda i, j, k: (i, k))
    b_spec = pl.BlockSpec((tn, tk), lambda i, j, k: (j, k)) if kind == "nt" else pl.BlockSpec((tk, tn), lambda i, j, k: (k, j))
    o_spec = pl.BlockSpec((tm, tn), lambda i, j, k: (i, j))
    vec_spec = pl.BlockSpec((1, tn), lambda i, j, k: (0, j))
    ex_specs = [vec_spec if e.shape[0] == 1 else o_spec for e in extras]
    n_ex, n_out = len(extras), len(out_dtypes)

    def body(*refs):
        a_ref, b_ref = refs[:2]
        ex = refs[2:2 + n_ex]
        outs = refs[2 + n_ex:2 + n_ex + n_out]
        acc = refs[-1]
        k = pl.program_id(2)
        first_row_tile = pl.program_id(0) == 0

        def finish(r):
            res = epi(r, *[e[...] for e in ex]) if epi is not None else (r,)
            for o, v in zip(outs, res):
                o[...] = v.astype(o.dtype)
            sums = refs[2 + n_ex + n_out:2 + n_ex + n_out + col_sums]

            @pl.when(first_row_tile)
            def _():
                for o in sums:
                    o[...] = jnp.zeros_like(o)

            for o, v in zip(sums, res[n_out:]):
                o[...] += v

        if nk == 1:
            finish(_dot(a_ref[...], b_ref[...], kind))
            return

        @pl.when(k == 0)
        def _():
            acc[...] = _dot(a_ref[...], b_ref[...], kind)

        @pl.when(k > 0)
        def _():
            acc[...] += _dot(a_ref[...], b_ref[...], kind)

        @pl.when(k == nk - 1)
        def _():
            finish(acc[...])

    res = pl.pallas_call(
        body,
        name=name,
        grid=(M // tm, N // tn, nk),
        in_specs=[a_spec, b_spec] + ex_specs,
        out_specs=[o_spec] * n_out + [vec_spec] * col_sums,
        out_shape=[jax.ShapeDtypeStruct((M, N), dt) for dt in out_dtypes] + [jax.ShapeDtypeStruct((1, N), F32)] * col_sums,
        scratch_shapes=[pltpu.VMEM((tm, tn), F32)],
        compiler_params=_cparams(("arbitrary" if col_sums else "parallel", "parallel", "arbitrary")),
    )(a, b, *extras)
    return res[0] if n_out + col_sums == 1 else res


_MM_ROWS = 1024


def _row_spec(arr, tm):
    return pl.BlockSpec((tm, arr.shape[1]), lambda i: (i, 0))


def _full_spec(arr):
    nd = arr.ndim
    return pl.BlockSpec(arr.shape, lambda i: (0,) * nd)


def _rows_fwd(fn, rows, params, outs, *, name, tm=256):
    T = rows[0].shape[0]
    tm = min(tm, T)
    nr, npar = len(rows), len(params)

    def body(*refs):
        ins = [r[...] for r in refs[:nr + npar]]
        res = fn(*ins)
        for o, v in zip(refs[nr + npar:], res):
            o[...] = v.astype(o.dtype)

    res = pl.pallas_call(
        body,
        name=name,
        grid=(T // tm,),
        in_specs=[_row_spec(r, tm) for r in rows] + [_full_spec(p) for p in params],
        out_specs=[pl.BlockSpec((tm, c), lambda i: (i, 0)) for c, _ in outs],
        out_shape=[jax.ShapeDtypeStruct((T, c), dt) for c, dt in outs],
        compiler_params=_cparams(("parallel",)),
    )(*rows, *params)
    return res


def _rows_vjp(fn, rows, params, cts, *, name, drows, dparams, drow_dtypes, add=None, twin=False, tm=256):
    T = rows[0].shape[0]
    tm = min(tm, T)
    nr, npar, nct = len(rows), len(params), len(cts)
    n_add = 0 if add is None else 1
    n_dr, n_dp = len(drows), len(dparams)
    n_tw = 1 if twin else 0

    def body(*refs):
        row_t = [r[...] for r in refs[:nr]]
        par_t = [r[...] for r in refs[nr:nr + npar]]
        ct_t = [r[...].astype(F32) for r in refs[nr + npar:nr + npar + nct]]
        pos = nr + npar + nct
        add_t = refs[pos][...] if n_add else None
        pos += n_add
        drow_refs = refs[pos:pos + n_dr]
        dpar_refs = refs[pos + n_dr:pos + n_dr + n_dp]

        def g(*dargs):
            rr, pp = list(row_t), list(par_t)
            for idx, v in zip(drows, dargs[:n_dr]):
                rr[idx] = v
            for idx, v in zip(dparams, dargs[n_dr:]):
                pp[idx] = v
            return tuple(fn(*rr, *pp))

        prim = [row_t[i].astype(F32) for i in drows] + [par_t[i].astype(F32) for i in dparams]
        _, vjp = jax.vjp(g, *prim)
        grads = vjp(tuple(ct_t))
        for n, (o, v) in enumerate(zip(drow_refs, grads[:n_dr])):
            if n == 0 and n_add:
                v = v + add_t.astype(F32)
            o[...] = v.astype(o.dtype)
            if n == 0 and n_tw:
                refs[-1][...] = v.astype(BF16)

        @pl.when(pl.program_id(0) == 0)
        def _():
            for o in dpar_refs:
                o[...] = jnp.zeros_like(o)

        for o, v in zip(dpar_refs, grads[n_dr:]):
            o[...] += v

    res = pl.pallas_call(
        body,
        name=name,
        grid=(T // tm,),
        in_specs=[_row_spec(r, tm) for r in rows] + [_full_spec(p) for p in params] + [_row_spec(c, tm) for c in cts]
        + ([_row_spec(add, tm)] if n_add else []),
        out_specs=[_row_spec(rows[i], tm) for i in drows] + [_full_spec(params[i]) for i in dparams]
        + [_row_spec(rows[drows[0]], tm)] * n_tw,
        out_shape=[jax.ShapeDtypeStruct(rows[i].shape, dt) for i, dt in zip(drows, drow_dtypes)]
        + [jax.ShapeDtypeStruct(params[i].shape, F32) for i in dparams]
        + [jax.ShapeDtypeStruct(rows[drows[0]].shape, BF16)] * n_tw,
        compiler_params=_cparams(("arbitrary",)),
    )(*rows, *params, *cts, *([add] if n_add else []))
    return list(res[:n_dr]) + list(res[n_dr + n_dp:]), list(res[n_dr:n_dr + n_dp])


def _rms(x, g, n):
    return x * lax.rsqrt(jnp.sum(x * x, axis=-1, keepdims=True) * (1.0 / n) + EPS) * g


def _sigmoid(x):
    return 1.0 / (1.0 + jnp.exp(-x))


def _silu(x):
    return x * _sigmoid(x)


def _softplus(x):
    return jnp.maximum(x, 0.0) + jnp.log(1.0 + jnp.exp(-jnp.abs(x)))


def _gelu_tanh(x):
    return 0.5 * x * (1.0 + jnp.tanh(math.sqrt(2.0 / math.pi) * (x + 0.044715 * x * x * x)))


def _lane(shape):
    return lax.broadcasted_iota(jnp.int32, shape, len(shape) - 1)


def _col(x, h):
    return jnp.sum(jnp.where(_lane(x.shape) == h, x, 0.0), axis=-1, keepdims=True)


def _f_norm(x, g):
    return (_rms(x.astype(F32), g, x.shape[-1]),)


def _f_mla_prep(u, ck, sk, gq, gkv):
    u = u.astype(F32)
    cq = _rms(u[:, 0:256], gq, Q_RANK)
    ckv = _rms(u[:, 256:384], gkv, KV_RANK)
    kr = u[:, 384:512] * ck + u[:, 512:640] * sk
    return cq, jnp.concatenate([ckv, kr], axis=1)


def _f_qrope(y, cq, sq):
    y = y.astype(F32)
    c8, s8 = jnp.tile(cq, (1, HEADS)), jnp.tile(sq, (1, HEADS))
    return (y[:, :HEADS * SLAB] * c8 + y[:, HEADS * SLAB:] * s8,)


def _f_lru_gates(xc, wa, ba, wi, bi, lam):
    xc = xc.astype(F32)
    r = _sigmoid(_bdot(xc, wa, "nn") + ba)
    i = _sigmoid(_bdot(xc, wi, "nn") + bi)
    log_a = -LRU_C * r * _softplus(-lam)
    a = jnp.exp(log_a)
    x2 = 2.0 * log_a
    m1 = jnp.where(x2 > -0.02, -x2 * (1.0 + x2 * (0.5 + x2 * (1.0 / 6.0 + x2 * (1.0 / 24.0)))), 1.0 - jnp.exp(x2))
    return a, jnp.sqrt(m1) * (i * xc)


def _f_mix(o, ys, h, ug, g_mla, g_lru):
    o = o.astype(F32)
    y_mla = _rms(o, g_mla, HEADS * VDIM)
    y_lru = _rms(h.astype(F32) * _gelu_tanh(ug[:, 256:512].astype(F32)), g_lru, 256)
    return (jnp.concatenate([y_mla, ys.astype(F32), y_lru], axis=1),)


def _f_xattn(q, k, v):
    hd = D // MEM_HEADS
    outs = []
    for h in range(MEM_HEADS):
        sl = slice(h * hd, (h + 1) * hd)
        s = _bdot(q[:, sl], k[:, sl], "nt") * (1.0 / math.sqrt(hd))
        s = s - jnp.max(s, axis=-1, keepdims=True)
        p = jnp.exp(s)
        p = p / jnp.sum(p, axis=-1, keepdims=True)
        outs.append(_bdot(p, v[:, sl], "nn"))
    return (jnp.concatenate(outs, axis=1),)


def _split_dot(tri, a, kind):
    a_hi = a.astype(BF16)
    r1 = a - a_hi.astype(F32)
    a_mid = r1.astype(BF16)
    a_lo = (r1 - a_mid.astype(F32)).astype(BF16)
    return _dot(tri, a_hi, kind) + _dot(tri, a_mid, kind) + _dot(tri, a_lo, kind)


@jax.custom_vjp
def _tri_cumsum(tri, a):
    return _split_dot(tri, a, "nn")


def _tri_cumsum_fwd(tri, a):
    return _split_dot(tri, a, "nn"), tri


def _tri_cumsum_bwd(tri, g):
    return jnp.zeros_like(tri), _split_dot(tri, g, "tn")


_tri_cumsum.defvjp(_tri_cumsum_fwd, _tri_cumsum_bwd)


def _f_ssd_chunk(c, ug, s0, s1, dtb, alog, dsk, ng):
    L = c.shape[0]
    c = c.astype(F32)
    xbc = _silu(c)
    xs, bm, cm = xbc[:, 0:256], xbc[:, 256:384], xbc[:, 384:512]
    z = ug[:, 0:256].astype(F32)
    dt = _softplus(ug[:, 512:640].astype(F32) + dtb)
    a = dt * (-jnp.exp(alog))
    rowi = lax.broadcasted_iota(jnp.int32, (L, L), 0)
    coli = lax.broadcasted_iota(jnp.int32, (L, L), 1)
    tril = rowi >= coli
    acum = _tri_cumsum(tril.astype(BF16), a)
    acum_t = acum.T
    lane = _lane((1, SLAB))
    lo = lane < 64
    ys, new_s = [], []
    for g in range(2):
        gm = (lane >= 64 * g) & (lane < 64 * g + 64)
        bg, cg = jnp.where(gm, bm, 0.0), jnp.where(gm, cm, 0.0)
        cb = _bdot(cg, bg, "nt")
        x = xs[:, SLAB * g:SLAB * (g + 1)]
        h0, h1 = 2 * g, 2 * g + 1
        ac0, ac1 = _col(acum, h0), _col(acum, h1)
        xdt = x * jnp.where(lo, _col(dt, h0), _col(dt, h1))
        ac_l = jnp.where(lo, ac0, ac1)
        tot = acum[L - 1:L, :]
        tot_l = jnp.where(lo, _col(tot, h0), _col(tot, h1))
        yd = jnp.zeros((L, SLAB), F32)
        for hh, acc, hm in ((h0, ac0, lo), (h1, ac1, jnp.logical_not(lo))):
            seg = acc - acum_t[hh:hh + 1, :]
            lm = jnp.where(tril, jnp.exp(jnp.where(tril, seg, 0.0)), 0.0)
            yd = yd + _bdot(cb * lm, jnp.where(hm, xdt, 0.0), "nn")
        sg = (s0, s1)[g]
        y_off = _bdot(cg, sg, "nn") * jnp.exp(ac_l)
        st = _bdot(bg, xdt * jnp.exp(tot_l - ac_l), "tn")
        new_s.append(jnp.exp(tot_l) * sg + st)
        y = yd + y_off + jnp.where(lo, _col(dsk, h0), _col(dsk, h1)) * x
        y = y * _silu(z[:, SLAB * g:SLAB * (g + 1)])
        ys.append(_rms(y, ng[:, SLAB * g:SLAB * (g + 1)], SLAB))
    return jnp.concatenate(ys, axis=1), new_s[0], new_s[1]


_SSD_TILE = 512


def _ssd_fwd(c, ug, dtb, alog, dsk, ng, *, name):
    T = c.shape[0]
    tm = min(_SSD_TILE, T)
    ncs = tm // SSM_CHUNK
    nc = T // SSM_CHUNK

    def body(c_ref, ug_ref, dtb_ref, alog_ref, dsk_ref, ng_ref, y_ref, sall_ref, s_scr):
        @pl.when(pl.program_id(0) == 0)
        def _():
            s_scr[...] = jnp.zeros_like(s_scr)

        s0, s1 = s_scr[0], s_scr[1]
        for k in range(ncs):
            rows = slice(k * SSM_CHUNK, (k + 1) * SSM_CHUNK)
            sall_ref[k, 0] = s0
            sall_ref[k, 1] = s1
            y, s0, s1 = _f_ssd_chunk(c_ref[rows, :], ug_ref[rows, :], s0, s1, dtb_ref[...], alog_ref[...],
                                     dsk_ref[...], ng_ref[...])
            y_ref[rows, :] = y
        s_scr[0] = s0
        s_scr[1] = s1

    y, sall = pl.pallas_call(
        body,
        name=name,
        grid=(T // tm,),
        in_specs=[_row_spec(c, tm), _row_spec(ug, tm)] + [_full_spec(p) for p in (dtb, alog, dsk, ng)],
        out_specs=[pl.BlockSpec((tm, 256), lambda i: (i, 0)), pl.BlockSpec((ncs, 2, SLAB, SLAB), lambda i: (i, 0, 0, 0))],
        out_shape=[jax.ShapeDtypeStruct((T, 256), F32), jax.ShapeDtypeStruct((nc, 2, SLAB, SLAB), F32)],
        scratch_shapes=[pltpu.VMEM((2, SLAB, SLAB), F32)],
        compiler_params=_cparams(("arbitrary",)),
    )(c, ug, dtb, alog, dsk, ng)
    return y, sall


def _ssd_bwd(c, ug, sall, dy, dtb, alog, dsk, ng, *, name):
    T = c.shape[0]
    tm = min(_SSD_TILE, T)
    ncs = tm // SSM_CHUNK
    nt = T // tm

    def body(c_ref, ug_ref, sall_ref, dy_ref, dtb_ref, alog_ref, dsk_ref, ng_ref,
             dc_ref, dug_ref, ddtb_ref, dalog_ref, ddsk_ref, dng_ref, ds_scr):
        @pl.when(pl.program_id(0) == 0)
        def _():
            ds_scr[...] = jnp.zeros_like(ds_scr)
            for o in (ddtb_ref, dalog_ref, ddsk_ref, dng_ref):
                o[...] = jnp.zeros_like(o)

        ds0, ds1 = ds_scr[0], ds_scr[1]
        for k in reversed(range(ncs)):
            rows = slice(k * SSM_CHUNK, (k + 1) * SSM_CHUNK)
            prim = (c_ref[rows, :].astype(F32), ug_ref[rows, :].astype(F32), sall_ref[k, 0], sall_ref[k, 1],
                    dtb_ref[...], alog_ref[...], dsk_ref[...], ng_ref[...])
            _, vjp = jax.vjp(_f_ssd_chunk, *prim)
            dc, dug, ds0, ds1, g_dtb, g_alog, g_dsk, g_ng = vjp((dy_ref[rows, :].astype(F32), ds0, ds1))
            dc_ref[rows, :] = dc
            dug_ref[rows, :] = dug
            ddtb_ref[...] += g_dtb
            dalog_ref[...] += g_alog
            ddsk_ref[...] += g_dsk
            dng_ref[...] += g_ng
        ds_scr[0] = ds0
        ds_scr[1] = ds1

    rev = lambda i: (nt - 1 - i, 0)
    params = (dtb, alog, dsk, ng)
    res = pl.pallas_call(
        body,
        name=name,
        grid=(nt,),
        in_specs=[pl.BlockSpec((tm, c.shape[1]), rev), pl.BlockSpec((tm, ug.shape[1]), rev),
                  pl.BlockSpec((ncs, 2, SLAB, SLAB), lambda i: (nt - 1 - i, 0, 0, 0)), pl.BlockSpec((tm, 256), rev)]
        + [_full_spec(p) for p in params],
        out_specs=[pl.BlockSpec((tm, 512), rev), pl.BlockSpec((tm, U_GATE), rev)] + [_full_spec(p) for p in params],
        out_shape=[jax.ShapeDtypeStruct((T, 512), F32), jax.ShapeDtypeStruct((T, U_GATE), F32)]
        + [jax.ShapeDtypeStruct(p.shape, F32) for p in params],
        scratch_shapes=[pltpu.VMEM((2, SLAB, SLAB), F32)],
        compiler_params=_cparams(("arbitrary",)),
    )(c, ug, sall, dy, *params)
    return res


_CONV_TILE = 512
_HALO = 8
_CONV_W = 4


def _conv_fwd(u, w, b, *, name):
    T, C = u.shape
    tm = min(_CONV_TILE, T)
    hb = tm // _HALO

    def body(u_ref, prev_ref, w_ref, b_ref, y1_ref, y2_ref, ext):
        i = pl.program_id(0)
        ext[0:_HALO, :] = jnp.where(i > 0, prev_ref[...], 0.0)
        ext[_HALO:, :] = u_ref[...]
        y = jnp.broadcast_to(b_ref[...], (tm, C))
        for k in range(_CONV_W):
            y = y + ext[_HALO - (_CONV_W - 1) + k:_HALO - (_CONV_W - 1) + k + tm, :] * w_ref[k:k + 1, :]
        y1_ref[...] = y[:, 0:512]
        y2_ref[...] = y[:, 512:768]

    return pl.pallas_call(
        body,
        name=name,
        grid=(T // tm,),
        in_specs=[_row_spec(u, tm), pl.BlockSpec((_HALO, C), lambda i: (jnp.maximum(i * hb - 1, 0), 0)),
                  _full_spec(w), _full_spec(b)],
        out_specs=[pl.BlockSpec((tm, 512), lambda i: (i, 0)), pl.BlockSpec((tm, 256), lambda i: (i, 0))],
        out_shape=[jax.ShapeDtypeStruct((T, 512), F32), jax.ShapeDtypeStruct((T, 256), F32)],
        scratch_shapes=[pltpu.VMEM((tm + _HALO, C), F32)],
        compiler_params=_cparams(("parallel",)),
    )(u, u, w, b)


def _conv_bwd(u, dy1, dy2, w, *, name):
    T, C = u.shape
    tm = min(_CONV_TILE, T)
    hb = tm // _HALO
    nt = T // tm

    def body(u_ref, prev_ref, dy1_ref, next1_ref, dy2_ref, next2_ref, w_ref, du_ref, dw_ref, db_ref, ext, dext):
        i = pl.program_id(0)
        ext[0:_HALO, :] = jnp.where(i > 0, prev_ref[...], 0.0)
        ext[_HALO:, :] = u_ref[...]
        dext[0:tm, 0:512] = dy1_ref[...]
        dext[0:tm, 512:768] = dy2_ref[...]
        dext[tm:, 0:512] = jnp.where(i < nt - 1, next1_ref[...], 0.0)
        dext[tm:, 512:768] = jnp.where(i < nt - 1, next2_ref[...], 0.0)

        @pl.when(i == 0)
        def _():
            dw_ref[...] = jnp.zeros_like(dw_ref)
            db_ref[...] = jnp.zeros_like(db_ref)

        dy = dext[0:tm, :]
        du = jnp.zeros((tm, C), F32)
        for k in range(_CONV_W):
            du = du + dext[_CONV_W - 1 - k:_CONV_W - 1 - k + tm, :] * w_ref[k:k + 1, :]
            xk = ext[_HALO - (_CONV_W - 1) + k:_HALO - (_CONV_W - 1) + k + tm, :]
            dw_ref[k:k + 1, :] += jnp.sum(dy * xk, axis=0, keepdims=True)
        du_ref[...] = du
        db_ref[...] += jnp.sum(dy, axis=0, keepdims=True)

    nxt = lambda i: (jnp.minimum((i + 1) * hb, T // _HALO - 1), 0)
    return pl.pallas_call(
        body,
        name=name,
        grid=(nt,),
        in_specs=[_row_spec(u, tm), pl.BlockSpec((_HALO, C), lambda i: (jnp.maximum(i * hb - 1, 0), 0)),
                  _row_spec(dy1, tm), pl.BlockSpec((_HALO, 512), nxt), _row_spec(dy2, tm), pl.BlockSpec((_HALO, 256), nxt),
                  _full_spec(w)],
        out_specs=[pl.BlockSpec((tm, C), lambda i: (i, 0)), _full_spec(w), pl.BlockSpec((1, C), lambda i: (0, 0))],
        out_shape=[jax.ShapeDtypeStruct((T, C), F32), jax.ShapeDtypeStruct(w.shape, F32), jax.ShapeDtypeStruct((1, C), F32)],
        scratch_shapes=[pltpu.VMEM((tm + _HALO, C), F32), pltpu.VMEM((tm + _HALO, C), F32)],
        compiler_params=_cparams(("arbitrary",)),
    )(u, u, dy1, dy1, dy2, dy2, w)


_SCAN_TILE = 1024
_SUB = 8


def _shift_rows(x, d, fill, up):
    r = lax.broadcasted_iota(jnp.int32, x.shape, 0)
    if up:
        return jnp.where(r < _SUB - d, pltpu.roll(x, _SUB - d, 0), fill)
    return jnp.where(r >= d, pltpu.roll(x, d, 0), fill)


def _lru_scan_fwd(a, b, *, name):
    T, W = a.shape
    tr = min(_SCAN_TILE, T)

    def body(a_ref, b_ref, h_ref, hp_ref, carry):
        @pl.when(pl.program_id(0) == 0)
        def _():
            carry[...] = jnp.zeros_like(carry)

        def step(t, cr):
            rows = pl.ds(pl.multiple_of(t * _SUB, _SUB), _SUB)
            aa, bb = a_ref[rows, :], b_ref[rows, :]
            for d in (1, 2, 4):
                bb = bb + aa * _shift_rows(bb, d, 0.0, False)
                aa = aa * _shift_rows(aa, d, 1.0, False)
            h = bb + aa * cr
            h_ref[rows, :] = h
            r = lax.broadcasted_iota(jnp.int32, h.shape, 0)
            hp_ref[rows, :] = jnp.where(r >= 1, pltpu.roll(h, 1, 0), cr)
            return jnp.broadcast_to(h[_SUB - 1:_SUB, :], (_SUB, W))

        carry[...] = lax.fori_loop(0, tr // _SUB, step, carry[...])

    return pl.pallas_call(
        body,
        name=name,
        grid=(T // tr,),
        in_specs=[_row_spec(a, tr), _row_spec(b, tr)],
        out_specs=[pl.BlockSpec((tr, W), lambda i: (i, 0))] * 2,
        out_shape=[jax.ShapeDtypeStruct((T, W), F32)] * 2,
        scratch_shapes=[pltpu.VMEM((_SUB, W), F32)],
        compiler_params=_cparams(("arbitrary",)),
    )(a, b)


def _lru_scan_bwd(a, dh, hprev, *, name):
    T, W = a.shape
    tr = min(_SCAN_TILE, T)
    nt = T // tr

    def body(a_ref, dh_ref, hp_ref, g_ref, da_ref, carry):
        @pl.when(pl.program_id(0) == 0)
        def _():
            carry[...] = jnp.zeros_like(carry)

        nsub = tr // _SUB

        def step(s, cr):
            t = nsub - 1 - s
            rows = pl.ds(pl.multiple_of(t * _SUB, _SUB), _SUB)
            a_t = a_ref[rows, :]
            aa = _shift_rows(a_t, 1, 1.0, True)
            bb = dh_ref[rows, :]
            for d in (1, 2, 4):
                bb = bb + aa * _shift_rows(bb, d, 0.0, True)
                aa = aa * _shift_rows(aa, d, 1.0, True)
            g = bb + aa * cr
            g_ref[rows, :] = g
            da_ref[rows, :] = g * hp_ref[rows, :]
            return jnp.broadcast_to(a_t[0:1, :] * g[0:1, :], (_SUB, W))

        carry[...] = lax.fori_loop(0, nsub, step, carry[...])

    rev = lambda i: (nt - 1 - i, 0)
    return pl.pallas_call(
        body,
        name=name,
        grid=(nt,),
        in_specs=[pl.BlockSpec((tr, W), rev)] * 3,
        out_specs=[pl.BlockSpec((tr, W), rev)] * 2,
        out_shape=[jax.ShapeDtypeStruct((T, W), F32)] * 2,
        scratch_shapes=[pltpu.VMEM((_SUB, W), F32)],
        compiler_params=_cparams(("arbitrary",)),
    )(a, dh, hprev)


_ATT_BLK = 512
_ATT_QPARTS = 2
_ATT_SCALE = 1.0 / math.sqrt(NOPE + ROPE)
_ATT_SCALE2 = _ATT_SCALE * math.log2(math.e)


def _call_with_exchange(body, send, gather, *, name, grid, in_specs, out_specs, out_shape, args):
    if send is None:
        return pl.pallas_call(body, name=name, grid=grid, in_specs=in_specs, out_specs=out_specs, out_shape=out_shape,
                              compiler_params=_cparams(("parallel", "arbitrary")))(*args)
    n_in, n_out = len(in_specs), len(out_specs)

    def riding(*refs):
        comm = (refs[n_in], refs[n_in + 1 + n_out]) + tuple(refs[n_in + 2 + n_out:])
        h, i = pl.program_id(0), pl.program_id(1)

        @pl.when((h == 0) & (i == 0))
        def _():
            _chip_start(*comm, gather=gather)

        body(*refs[:n_in], *refs[n_in + 1:n_in + 1 + n_out])

        @pl.when((h == grid[0] - 1) & (i == grid[1] - 1))
        def _():
            _chip_wait(*comm, gather=gather)

    shape = (4,) + send.shape if gather else send.shape
    return pl.pallas_call(
        riding, name=name, grid=grid, in_specs=in_specs + [_ANY], out_specs=out_specs + [_ANY],
        out_shape=out_shape + [jax.ShapeDtypeStruct(shape, send.dtype)], scratch_shapes=_CHIP_SEMS,
        compiler_params=_cparams(("arbitrary", "arbitrary")))(*args, send)


def _attn_fwd(q, kv, *, name, send=None):
    T = q.shape[0]
    blk = min(_ATT_BLK, T)
    nq = T // blk
    parts = _ATT_QPARTS if nq % _ATT_QPARTS == 0 else 1

    def body(q_ref, kv_ref, o_ref, lse_ref):
        i = pl.program_id(1)
        def one(part, j0, nblk, carry, masked):
            m, l, acc = carry
            rows = pl.ds(pl.multiple_of(j0 * blk, blk), nblk * blk)
            s = _dot(q_ref[part * blk:(part + 1) * blk, :], kv_ref[rows, 0:SLAB], "nt")
            if masked:
                col = lax.broadcasted_iota(jnp.int32, s.shape, 1)
                row = lax.broadcasted_iota(jnp.int32, s.shape, 0)
                s = jnp.where(col <= row + (nblk - 1) * blk, s, -jnp.inf)
            m_new = jnp.maximum(m, jnp.max(s, axis=-1, keepdims=True))
            alpha = jnp.exp2(m - m_new)
            p = jnp.exp2(s - m_new)
            l = alpha * l + jnp.sum(p, axis=-1, keepdims=True)
            acc = alpha * acc + _dot(p, kv_ref[rows, SLAB:2 * SLAB], "nn")
            return m_new, l, acc

        def step(t, carry):
            return tuple(one(part, t * parts, parts, carry[part], False) for part in range(parts))

        init = ((jnp.full((blk, 1), -jnp.inf, F32), jnp.zeros((blk, 1), F32), jnp.zeros((blk, SLAB), F32)),) * parts
        carry = lax.fori_loop(0, i, step, init)
        for part in range(parts):
            m, l, acc = one(part, i * parts, part + 1, carry[part], True)
            o_ref[part * blk:(part + 1) * blk, :] = acc / l
            lse_ref[part] = jnp.broadcast_to(m + jnp.log(l) * math.log2(math.e), (blk, SLAB)).T[0:_SUB, :]

    return _call_with_exchange(
        body, send, True,
        name=name,
        grid=(HEADS, nq // parts),
        in_specs=[pl.BlockSpec((parts * blk, SLAB), lambda h, i: (i, h)), pl.BlockSpec((T, 2 * SLAB), lambda h, i: (0, h))],
        out_specs=[pl.BlockSpec((parts * blk, SLAB), lambda h, i: (i, h)),
                   pl.BlockSpec((None, parts, _SUB, blk), lambda h, i: (h, i, 0, 0))],
        out_shape=[jax.ShapeDtypeStruct((T, HEADS * SLAB), F32), jax.ShapeDtypeStruct((HEADS, nq, _SUB, blk), F32)],
        args=(q, kv))


def _attn_delta(do, o, *, name):
    T = o.shape[0]
    blk = min(_ATT_BLK, T)
    nq = T // blk

    def body(do_ref, o_ref, d_ref):
        for h in range(HEADS):
            cols = slice(h * SLAB, (h + 1) * SLAB)
            dl = jnp.sum(do_ref[:, cols].astype(F32) * o_ref[:, cols], axis=-1, keepdims=True)
            d_ref[h] = jnp.broadcast_to(dl, (blk, SLAB)).T[0:_SUB, :]

    return pl.pallas_call(
        body,
        name=name,
        grid=(nq,),
        in_specs=[pl.BlockSpec((blk, HEADS * SLAB), lambda i: (i, 0))] * 2,
        out_specs=pl.BlockSpec((HEADS, None, _SUB, blk), lambda i: (0, i, 0, 0)),
        out_shape=jax.ShapeDtypeStruct((HEADS, nq, _SUB, blk), F32),
        compiler_params=_cparams(("parallel",)),
    )(do, o)


def _attn_bwd(q, kv, do, lse, delta, *, name, send=None):
    T = q.shape[0]
    blk = min(_ATT_BLK, T)
    nq = T // blk

    def body(q_ref, kv_ref, do_ref, lse_ref, dl_ref, dqt_ref, dkv_ref):
        j = pl.program_id(1)

        @pl.when(j == 0)
        def _():
            dqt_ref[...] = jnp.zeros_like(dqt_ref)

        kb, vb = kv_ref[:, 0:SLAB], kv_ref[:, SLAB:2 * SLAB]
        kbt = kb.astype(F32).T.astype(BF16)
        kpos = j * blk + lax.broadcasted_iota(jnp.int32, (blk, blk), 0)

        def step(i, carry, masked):
            dk, dv = carry
            rows = pl.ds(pl.multiple_of(i * blk, blk), blk)
            qb, dob = q_ref[rows, :], do_ref[rows, :]
            st = _dot(kb, qb, "nt")
            if masked:
                qpos = i * blk + lax.broadcasted_iota(jnp.int32, (blk, blk), 1)
                st = jnp.where(kpos <= qpos, st, -jnp.inf)
            pt = jnp.exp2(st - lse_ref[i, 0:1, :])
            dpt = _dot(vb, dob, "nt")
            dst = pt * (dpt - dl_ref[i, 0:1, :]) * math.log(2.0)
            dv = dv + _dot(pt, dob, "nn")
            dk = dk + _dot(dst, qb, "nn")
            dqt_ref[i] += _dot(kbt, dst, "nn")
            return dk, dv

        zero = jnp.zeros((blk, SLAB), F32)
        carry = step(j, (zero, zero), True)
        rest = nq - 1 - j

        def two_steps(t, c):
            i = j + 1 + 2 * t
            return step(i + 1, step(i, c, False), False)

        carry = lax.fori_loop(0, rest // 2, two_steps, carry)
        dk, dv = lax.cond(rest % 2 == 1, lambda c: step(nq - 1, c, False), lambda c: c, carry)
        dkv_ref[:, 0:SLAB] = dk.astype(BF16)
        dkv_ref[:, SLAB:2 * SLAB] = dv.astype(BF16)

    stat_spec = pl.BlockSpec((None, nq, _SUB, blk), lambda h, j: (h, 0, 0, 0))
    return _call_with_exchange(
        body, send, False,
        name=name,
        grid=(HEADS, nq),
        in_specs=[pl.BlockSpec((T, SLAB), lambda h, j: (0, h)), pl.BlockSpec((blk, 2 * SLAB), lambda h, j: (j, h)),
                  pl.BlockSpec((T, SLAB), lambda h, j: (0, h)), stat_spec, stat_spec],
        out_specs=[pl.BlockSpec((None, nq, SLAB, blk), lambda h, j: (h, 0, 0, 0)),
                   pl.BlockSpec((blk, 2 * SLAB), lambda h, j: (j, h))],
        out_shape=[jax.ShapeDtypeStruct((HEADS, nq, SLAB, blk), F32), jax.ShapeDtypeStruct((T, HEADS * 2 * SLAB), BF16)],
        args=(q, kv, do, lse, delta))


def _qrope_bwd(dqt, cq, sq, *, name):
    _, nq, _, blk = dqt.shape
    T = nq * blk

    def body(dqt_ref, c_ref, s_ref, dy_ref):
        c, s = c_ref[...], s_ref[...]
        for h in range(HEADS):
            dq = dqt_ref[h].T
            dy_ref[:, h * SLAB:(h + 1) * SLAB] = (dq * c).astype(BF16)
            dy_ref[:, (HEADS + h) * SLAB:(HEADS + h + 1) * SLAB] = (dq * s).astype(BF16)

    return pl.pallas_call(
        body,
        name=name,
        grid=(nq,),
        in_specs=[pl.BlockSpec((HEADS, None, SLAB, blk), lambda i: (0, i, 0, 0)), _row_spec(cq, blk), _row_spec(sq, blk)],
        out_specs=pl.BlockSpec((blk, 2 * HEADS * SLAB), lambda i: (i, 0)),
        out_shape=jax.ShapeDtypeStruct((T, 2 * HEADS * SLAB), BF16),
        compiler_params=_cparams(("parallel",)),
    )(dqt, cq, sq)


def _loss_head(x, tgt, g, *, name, tm=256):
    T = x.shape[0]
    tm = min(tm, T)

    def body(x_ref, t_ref, g_ref, loss_ref, dx_ref, dxh_ref, dg_ref):
        def f(xv, gv):
            e = _rms(xv, gv, D) - t_ref[...]
            row = jnp.sum(e * e, axis=1, keepdims=True)
            return jnp.sum(row, axis=0, keepdims=True) * (0.5 / D)

        val, vjp = jax.vjp(f, x_ref[...], g_ref[...])
        dxv, dgv = vjp(jnp.ones((1, 1), F32))

        @pl.when(pl.program_id(0) == 0)
        def _():
            loss_ref[...] = jnp.zeros_like(loss_ref)
            dg_ref[...] = jnp.zeros_like(dg_ref)

        dx_ref[...] = dxv
        dxh_ref[...] = dxv.astype(BF16)
        dg_ref[...] += dgv
        loss_ref[...] += jnp.broadcast_to(val, loss_ref.shape)

    return pl.pallas_call(
        body,
        name=name,
        grid=(T // tm,),
        in_specs=[_row_spec(x, tm), _row_spec(tgt, tm), _full_spec(g)],
        out_specs=[pl.BlockSpec((1, SLAB), lambda i: (0, 0)), _row_spec(x, tm), _row_spec(x, tm), _full_spec(g)],
        out_shape=[jax.ShapeDtypeStruct((1, SLAB), F32), jax.ShapeDtypeStruct(x.shape, F32),
                   jax.ShapeDtypeStruct(x.shape, BF16), jax.ShapeDtypeStruct(g.shape, F32)],
        compiler_params=_cparams(("arbitrary",)),
    )(x, tgt, g)


def _row_tile(rows, cols, budget=256 * 1024):
    best = None
    for t in range(16, rows + 1, 16):
        if rows % t == 0 and t * cols <= budget:
            best = t
    return best or rows


def _sum_fixed(x, out_dtype, *, name):
    n, R, C = x.shape
    tr = _row_tile(R, C)

    def body(x_ref, o_ref):
        acc = x_ref[0].astype(F32)
        for k in range(1, n):
            acc = acc + x_ref[k].astype(F32)
        o_ref[...] = acc.astype(o_ref.dtype)

    return pl.pallas_call(
        body,
        name=name,
        grid=(R // tr,),
        in_specs=[pl.BlockSpec((n, tr, C), lambda i: (0, i, 0))],
        out_specs=pl.BlockSpec((tr, C), lambda i: (i, 0)),
        out_shape=jax.ShapeDtypeStruct((R, C), out_dtype),
        compiler_params=_cparams(("parallel",)),
    )(x)


def _adamw(w, g, m, v, *, name):
    R, C = w.shape
    tr = _row_tile(R, C, 128 * 1024)

    def body(w_ref, g_ref, m_ref, v_ref, d_ref, nm_ref, nv_ref):
        gv = g_ref[...]
        mv = B1 * m_ref[...] + (1.0 - B1) * gv
        vv = B2 * v_ref[...] + (1.0 - B2) * (gv * gv)
        m_hat = mv / (1.0 - B1 ** STEP)
        v_hat = vv / (1.0 - B2 ** STEP)
        d_ref[...] = -LR * (m_hat / (jnp.sqrt(v_hat) + AEPS) + WD * w_ref[...])
        nm_ref[...] = mv
        nv_ref[...] = vv

    spec = pl.BlockSpec((tr, C), lambda i: (i, 0))
    return pl.pallas_call(
        body, name=name, grid=(R // tr,), in_specs=[spec] * 4, out_specs=[spec] * 3,
        out_shape=[jax.ShapeDtypeStruct((R, C), F32)] * 3, compiler_params=_cparams(("parallel",)),
    )(w, g, m, v)


_FLIPS = ((1, 0), (0, 1), (1, 1))
_ANY = pl.BlockSpec(memory_space=pl.ANY)


def _me():
    return lax.axis_index("x"), lax.axis_index("y"), lax.axis_index("c")


def _flip(mx, my, f):
    return (1 - mx if f[0] else mx), (1 - my if f[1] else my)


_CHIP_SEMS = [pltpu.SemaphoreType.DMA((3,)), pltpu.SemaphoreType.DMA((3,)), pltpu.SemaphoreType.DMA]


def _chip_copies(x_ref, out_ref, send_sems, recv_sems, local_sem, gather):
    mx, my, mc = _me()
    mine = 2 * mx + my
    outgoing, incoming = [], []
    for k, f in enumerate(_FLIPS):
        px, py = _flip(mx, my, f)
        peer = 2 * px + py
        src = x_ref if gather else x_ref.at[peer]
        for dst, to in ((out_ref.at[mine], outgoing), (out_ref.at[peer], incoming)):
            to.append(pltpu.make_async_remote_copy(src_ref=src, dst_ref=dst, send_sem=send_sems.at[k],
                                                   recv_sem=recv_sems.at[k], device_id=(px, py, mc), device_id_type=MESH))
    local = None if gather else pltpu.make_async_copy(x_ref.at[mine], out_ref.at[mine], local_sem)
    return outgoing, incoming, local


def _chip_start(*refs, gather):
    outgoing, _, local = _chip_copies(*refs, gather)
    if local is not None:
        local.start()
    for cp in outgoing:
        cp.start()


def _chip_wait(*refs, gather):
    outgoing, incoming, local = _chip_copies(*refs, gather)
    for cp in incoming:
        cp.wait_recv()
    for cp in outgoing:
        cp.wait_send()
    if local is not None:
        local.wait()


def _chip_exchange(x, gather, *, name):
    shape = x.shape if not gather else (4,) + x.shape

    def body(*refs):
        _chip_start(*refs, gather=gather)
        _chip_wait(*refs, gather=gather)

    return pl.pallas_call(
        body, name=name, in_specs=[_ANY], out_specs=_ANY, out_shape=jax.ShapeDtypeStruct(shape, x.dtype),
        scratch_shapes=_CHIP_SEMS,
    )(x)


def _core_exchange(x, half, *, name):
    shape = x.shape[1:] if half else x.shape

    def body(x_ref, out_ref, send_sem, recv_sem):
        mx, my, mc = _me()
        src = x_ref.at[1 - mc] if half else x_ref
        cp = pltpu.make_async_remote_copy(src_ref=src, dst_ref=out_ref, send_sem=send_sem, recv_sem=recv_sem,
                                          device_id=(mx, my, 1 - mc), device_id_type=MESH)
        cp.start()
        cp.wait()

    return pl.pallas_call(
        body, name=name, in_specs=[_ANY], out_specs=_ANY, out_shape=jax.ShapeDtypeStruct(shape, x.dtype),
        scratch_shapes=[pltpu.SemaphoreType.DMA, pltpu.SemaphoreType.DMA],
    )(x)


def _all_gather(x, *, name):
    return _finish_gather(x, _chip_exchange(x, True, name=name + "_chips"), name=name)


def _finish_gather(x, g4, *, name):
    mx, my, mc = _me()
    own = (jnp.arange(4) == 2 * mx + my).reshape((4,) + (1,) * x.ndim)
    g4 = jnp.where(own, x[None], g4)
    sib = _core_exchange(g4, False, name=name + "_cores")
    return jnp.where(mc == 0, jnp.stack([g4, sib]), jnp.stack([sib, g4]))


def _add_own_half(x, got, *, name):
    _, R, C = x.shape
    tr = _row_tile(R, C)

    def body(c_ref, x_ref, g_ref, o_ref):
        o_ref[...] = (x_ref[...] + g_ref[...]).astype(o_ref.dtype)

    return pl.pallas_call(
        body,
        name=name,
        grid_spec=pltpu.PrefetchScalarGridSpec(
            num_scalar_prefetch=1, grid=(R // tr,),
            in_specs=[pl.BlockSpec((None, tr, C), lambda i, c: (c[0], i, 0)), pl.BlockSpec((tr, C), lambda i, c: (i, 0))],
            out_specs=pl.BlockSpec((tr, C), lambda i, c: (i, 0))),
        out_shape=jax.ShapeDtypeStruct((R, C), BF16),
        compiler_params=_cparams(("parallel",)),
    )(lax.axis_index("c").astype(jnp.int32).reshape(1), x, got)


def _chip_sums(x, *, name):
    _, _, R, C = x.shape
    got = _core_exchange(x, True, name=name + "_cores")
    return _add_own_half(x.reshape(2, 4 * R, C), got.reshape(4 * R, C), name=name + "_add").reshape(4, R, C)


def _all_reduce(x, *, name):
    g = _all_gather(x, name=name)
    return _sum_fixed(g.reshape((N_DEV,) + x.shape), F32, name=name + "_sum")


WEIGHTS = ['mix_norm_g', 'w_in', 'mla_q_norm_g', 'mla_kv_norm_g', 'mla_w_uq', 'mla_w_ukv', 'mla_out_g', 'ssm_conv_w',
           'ssm_conv_b', 'ssm_dt_bias', 'ssm_a_log', 'ssm_d', 'ssm_norm_g', 'lru_conv_w', 'lru_conv_b', 'lru_w_a',
           'lru_b_a', 'lru_w_i', 'lru_b_i', 'lru_lambda', 'lru_out_g', 'w_out', 'xattn_norm_g', 'mem_norm_g', 'w_mq',
           'w_mk', 'w_mv', 'w_mo', 'mlp_norm_g', 'w_mlp1', 'w_mlp2', 'final_norm_g']
ROW_SHARDED = ('w_in', 'w_out', 'w_mq', 'w_mk', 'w_mv', 'w_mo', 'w_mlp2')
COL_SHARDED = ('mla_w_uq', 'mla_w_ukv', 'w_mlp1')
BIG = tuple(n for n in WEIGHTS if n in ROW_SHARDED + COL_SHARDED)
CONV_SHARDED = ('ssm_conv_w', 'lru_conv_w')
SMALL = tuple(n for n in WEIGHTS if n not in BIG)
PACK_C = 1024


def _pack(arrs, dtype, lead=()):
    flat = jnp.concatenate([a.reshape(lead + (-1,)).astype(dtype) for a in arrs], axis=-1)
    n = flat.shape[-1]
    rows = -(-n // (16 * PACK_C)) * 16
    flat = jnp.pad(flat, [(0, 0)] * len(lead) + [(0, rows * PACK_C - n)])
    return flat.reshape(lead + (rows, PACK_C))


def _unpack(packed, shapes, lead=()):
    flat = packed.reshape(lead + (-1,))
    out, off = [], 0
    for s in shapes:
        n = math.prod(s)
        out.append(flat[..., off:off + n].reshape(lead + tuple(s)))
        off += n
    return out


def _pad_lanes(v, n=SLAB):
    return jnp.pad(v.astype(F32), (0, n - v.shape[0])).reshape(1, n)


PIECES = ('win', 'wq', 'wkv', 'wout', 'w_mq', 'w_mk', 'w_mv', 'w_mo', 'w_mlp1', 'w_mlp2')
PIECE_SOURCE = {'win': 'w_in', 'wq': 'mla_w_uq', 'wkv': 'mla_w_ukv', 'wout': 'w_out'}
PIECE_SHAPE = {'win': (128, 2048), 'wq': (Q_RANK, 2 * SLAB), 'wkv': (KV_RANK, 2 * SLAB), 'wout': (128, D),
               'w_mq': (128, D), 'w_mk': (128, D), 'w_mv': (128, D), 'w_mo': (128, D), 'w_mlp1': (D, 512),
               'w_mlp2': (512, D)}
PIECE_COLS = ('wq', 'wkv', 'w_mlp1')


def _k_win(w):
    kr = w[..., 384:416]
    zc = lambda k: jnp.zeros(w.shape[:-1] + (k,), w.dtype)
    return jnp.concatenate(
        [w[..., 0:384], kr, zc(96), kr[..., 16:32], kr[..., 0:16], zc(96),
         w[..., 416:672], w[..., 1444:1700], w[..., 1184:1188], zc(124),
         w[..., 672:1184], w[..., 1188:1444]], axis=-1)


def _k_win_inv(m):
    gk = m[..., 384:416] + jnp.concatenate([m[..., 528:544], m[..., 512:528]], axis=-1)
    return jnp.concatenate([m[..., 0:384], gk, m[..., 640:896], m[..., 1280:1792], m[..., 1152:1156], m[..., 1792:2048],
                            m[..., 896:1152]], axis=-1)


def _k_wq(w):
    nh = w.shape[-1] // (NOPE + ROPE)
    w = w.reshape(w.shape[:-1] + (nh, NOPE + ROPE))
    nope, r1, r2 = w[..., :NOPE], w[..., NOPE:NOPE + 16], w[..., NOPE + 16:]
    z = lambda k: jnp.zeros(w.shape[:-1] + (k,), w.dtype)
    both = jnp.stack([jnp.concatenate([nope, r1, r2, z(32)], -1), jnp.concatenate([z(64), r2, r1, z(32)], -1)], axis=-3)
    return both.reshape(w.shape[:-2] + (2 * nh * SLAB,))


def _k_wq_inv(m):
    nh = m.shape[-1] // (2 * SLAB)
    m = m.reshape(m.shape[:-1] + (2, nh, SLAB))
    q0, q1 = m[..., 0, :, :], m[..., 1, :, :]
    w = jnp.concatenate([q0[..., :64], q0[..., 64:80] + q1[..., 80:96], q0[..., 80:96] + q1[..., 64:80]], -1)
    return w.reshape(w.shape[:-2] + (nh * (NOPE + ROPE),))


def _k_wkv(w):
    nh = w.shape[-1] // (NOPE + VDIM)
    w = w.reshape(w.shape[:-1] + (nh, NOPE + VDIM))
    z = jnp.zeros(w.shape[:-1] + (64,), w.dtype)
    return jnp.concatenate([w[..., :NOPE], z, w[..., NOPE:], z], -1).reshape(w.shape[:-2] + (nh * 2 * SLAB,))


def _k_wkv_inv(m):
    nh = m.shape[-1] // (2 * SLAB)
    m = m.reshape(m.shape[:-1] + (nh, 2 * SLAB))
    return jnp.concatenate([m[..., :NOPE], m[..., SLAB:SLAB + VDIM]], -1).reshape(m.shape[:-2] + (nh * (NOPE + VDIM),))


_K_FWD = {'win': _k_win, 'wq': _k_wq, 'wkv': _k_wkv}
_K_INV = {'win': _k_win_inv, 'wq': _k_wq_inv, 'wkv': _k_wkv_inv}


def _assemble(piece, g):
    _, _, a, b = g.shape
    if piece == 'wq':
        return g.reshape(2, 4, a, 2, SLAB).transpose(2, 3, 1, 0, 4).reshape(a, N_DEV * b)
    if piece in PIECE_COLS:
        return g.transpose(2, 1, 0, 3).reshape(a, N_DEV * b)
    return g.transpose(1, 0, 2, 3).reshape(N_DEV * a, b)


def _disassemble(piece, full):
    a, b = PIECE_SHAPE[piece]
    if piece == 'wq':
        return full.reshape(a, 2, 4, 2, SLAB).transpose(3, 2, 0, 1, 4).reshape(2, 4, a, b)
    if piece in PIECE_COLS:
        return full.reshape(a, 4, 2, b).transpose(2, 1, 0, 3)
    return full.reshape(4, 2, a, b).transpose(1, 0, 2, 3)


def _piece_rows(piece):
    a, b = PIECE_SHAPE[piece]
    return a * b // PACK_C


def _prep_layer(pieces, Ws, l):
    P = dict(pieces)
    ri, ci = jnp.arange(SLAB)[:, None], jnp.arange(2 * SLAB)[None, :]
    sel = ((ri < ROPE) & (ci == ri + NOPE)).astype(P['wkv'].dtype)
    P['wkv'] = jnp.concatenate([P['wkv'], jnp.tile(sel, (1, HEADS))], axis=0)
    wout = P['wout']
    mla_rows = jnp.pad(wout[:HEADS * VDIM].reshape(HEADS, VDIM, D), ((0, 0), (0, SLAB - VDIM), (0, 0)))
    P['wout'] = jnp.concatenate([mla_rows.reshape(HEADS * SLAB, D), wout[HEADS * VDIM:]], axis=0)
    W = Ws
    row = lambda n: W[n][l].astype(F32).reshape(1, -1)
    for n in ('mix_norm_g', 'mla_q_norm_g', 'mla_kv_norm_g', 'ssm_norm_g', 'lru_lambda', 'lru_out_g', 'xattn_norm_g',
              'mem_norm_g', 'mlp_norm_g'):
        P[n] = row(n)
    P['mla_out_g'] = jnp.pad(W['mla_out_g'][l].astype(F32).reshape(HEADS, VDIM), ((0, 0), (0, SLAB - VDIM))).reshape(1, -1)
    for n in ('ssm_dt_bias', 'ssm_a_log', 'ssm_d'):
        P[n] = _pad_lanes(W[n][l])
    P['conv_w'] = jnp.pad(jnp.concatenate([W['ssm_conv_w'][l], W['lru_conv_w'][l]], axis=1).astype(F32), ((0, 4), (0, 0)))
    P['conv_b'] = jnp.concatenate([W['ssm_conv_b'][l], W['lru_conv_b'][l]]).astype(F32).reshape(1, -1)
    for n in ('lru_w_a', 'lru_w_i'):
        P[n] = jnp.concatenate([jnp.pad(W[n][l, k].astype(F32), ((0, 0), (64 * k, 192 - 64 * k))) for k in range(4)], axis=0)
    for n in ('lru_b_a', 'lru_b_i'):
        P[n] = W[n][l].astype(F32).reshape(1, -1)
    return P


def _unprep_pieces(G):
    o = {n: G[n] for n in PIECES}
    o['wkv'] = G['wkv'][:KV_RANK]
    wo = G['wout']
    o['wout'] = jnp.concatenate([wo[:HEADS * SLAB].reshape(HEADS, SLAB, D)[:, :VDIM].reshape(HEADS * VDIM, D),
                                 wo[HEADS * SLAB:]], axis=0)
    return o


def _unprep_small(G):
    o = {}
    for n in ('mix_norm_g', 'mla_q_norm_g', 'mla_kv_norm_g', 'ssm_norm_g', 'lru_lambda', 'lru_out_g', 'xattn_norm_g',
              'mem_norm_g', 'mlp_norm_g', 'lru_b_a', 'lru_b_i'):
        o[n] = G[n].reshape(-1)
    o['lru_b_a'] = o['lru_b_a'].reshape(4, 64)
    o['lru_b_i'] = o['lru_b_i'].reshape(4, 64)
    o['mla_out_g'] = G['mla_out_g'].reshape(HEADS, SLAB)[:, :VDIM].reshape(-1)
    for n in ('ssm_dt_bias', 'ssm_a_log', 'ssm_d'):
        o[n] = G[n][0, :4]
    o['ssm_conv_w'], o['lru_conv_w'] = G['conv_w'][:4, :512], G['conv_w'][:4, 512:]
    o['ssm_conv_b'], o['lru_conv_b'] = G['conv_b'][0, :512], G['conv_b'][0, 512:]
    for n in ('lru_w_a', 'lru_w_i'):
        o[n] = jnp.stack([G[n][64 * k:64 * (k + 1), 64 * k:64 * (k + 1)] for k in range(4)])
    return o


def _rope_tables(positions):
    half = ROPE // 2
    inv_freq = ROPE_THETA ** (-jnp.arange(half, dtype=F32) * 2.0 / ROPE)
    ang = positions.astype(F32)[:, None] * inv_freq
    cos, sin = jnp.cos(ang), jnp.sin(ang)
    T = positions.shape[0]
    z = lambda k: jnp.zeros((T, k), F32)
    ck = jnp.concatenate([cos, cos, z(96)], axis=1)
    sk = jnp.concatenate([-sin, sin, z(96)], axis=1)
    cq = jnp.concatenate([jnp.ones((T, NOPE), F32), cos, cos, z(32)], axis=1)
    sq = jnp.concatenate([z(NOPE), -sin, sin, z(32)], axis=1)
    return ck, sk, cq * _ATT_SCALE2, sq * _ATT_SCALE2


def _add_epi(acc, res):
    return (acc + res,)


def _add_norm_epi(acc, res, g):
    x = acc + res
    return x, _rms(x, g, x.shape[-1])


def _norm_bwd_epi(acc, x, res, g):
    _, vjp = jax.vjp(lambda xv, gv: _rms(xv, gv, xv.shape[-1]), x, g)
    dx, dg = vjp(acc)
    dx = dx + res
    return dx, dx, dg


def _relu2_epi(acc):
    r = jnp.maximum(acc, 0.0)
    return r, r * r


def _drelu2_epi(acc, r):
    return (acc * (2.0 * r.astype(F32)),)


def _norm(x, g, name):
    return _rows_fwd(_f_norm, [x], [g], [(x.shape[1], BF16)], name=name)[0]


def _norm_bwd(x, g, ct, add, name):
    (dx, dx16), (dg,) = _rows_vjp(_f_norm, [x], [g], [ct], name=name, drows=[0], dparams=[0], drow_dtypes=[F32],
                                  add=add, twin=True)
    return dx, dx16, dg


def _layer_fwd(x0, h1, mem, P, tabs, g_next=None, send=None):
    ck, sk, cq, sq = tabs
    S = {'x0': x0}
    if h1 is None:
        h1 = _norm(x0, P['mix_norm_g'], "norm_mix")
    S['h1'] = h1
    win = P['win']
    u_mla = S['u_mla'] = _mm(h1, win[:, 0:U_MLA], name="in_mla")
    u_gate = S['u_gate'] = _mm(h1, win[:, U_MLA:U_MLA + U_GATE], name="in_gate")
    u_conv = S['u_conv'] = _mm(h1, win[:, U_MLA + U_GATE:], name="in_conv")
    cqn, akv = _rows_fwd(_f_mla_prep, [u_mla, ck, sk], [P['mla_q_norm_g'], P['mla_kv_norm_g']],
                         [(Q_RANK, BF16), (2 * SLAB, BF16)], name="mla_prep")
    S['cqn'], S['akv'] = cqn, akv
    yq = _mm(cqn, P['wq'], name="q_proj")
    q = S['q'] = _rows_fwd(_f_qrope, [yq, cq, sq], [], [(HEADS * SLAB, BF16)], name="q_rope")[0]
    kv = S['kv'] = _mm(akv, P['wkv'], name="kv_proj", out_dtypes=(BF16,))
    o, lse, *got = _attn_fwd(q, kv, name="attn_fwd" if send is None else "attn_fwd_gather", send=send)
    S['o'], S['lse'] = o, lse
    c_ssm, c_lru = _conv_fwd(u_conv, P['conv_w'], P['conv_b'], name="conv_fwd")
    S['c_ssm'], S['c_lru'] = c_ssm, c_lru
    ys, sall = _ssd_fwd(c_ssm, u_gate, P['ssm_dt_bias'], P['ssm_a_log'], P['ssm_d'], P['ssm_norm_g'], name="ssd_fwd")
    S['ys'], S['sall'] = ys, sall
    a, b = _rows_fwd(_f_lru_gates, [c_lru], [P['lru_w_a'], P['lru_b_a'], P['lru_w_i'], P['lru_b_i'], P['lru_lambda']],
                     [(256, F32), (256, F32)], name="lru_gates", tm=_MM_ROWS)
    h, hprev = _lru_scan_fwd(a, b, name="lru_scan")
    S['a'], S['h'], S['hprev'] = a, h, hprev
    ymix = S['ymix'] = _rows_fwd(_f_mix, [o, ys, h, u_gate], [P['mla_out_g'], P['lru_out_g']],
                                 [(HEADS * SLAB + 512, BF16)], name="mix")[0]
    x1, hx = _mm(ymix, P['wout'], name="out_proj", epi=_add_norm_epi, extras=(x0, P['xattn_norm_g']), out_dtypes=(F32, BF16))
    S['x1'], S['hx'] = x1, hx
    qx = S['qx'] = _mm(hx, P['w_mq'], name="mem_q", out_dtypes=(BF16,))
    mn = S['mn'] = _norm(mem, P['mem_norm_g'], "norm_mem")
    kx = S['kx'] = _mm(mn, P['w_mk'], name="mem_k", out_dtypes=(BF16,))
    vx = S['vx'] = _mm(mn, P['w_mv'], name="mem_v", out_dtypes=(BF16,))
    ox = S['ox'] = _rows_fwd(_f_xattn, [qx], [kx, vx], [(D, BF16)], name="xattn", tm=_MM_ROWS)[0]
    x2, hm = _mm(ox, P['w_mo'], name="mem_o", epi=_add_norm_epi, extras=(x1, P['mlp_norm_g']), out_dtypes=(F32, BF16))
    S['x2'], S['hm'] = x2, hm
    r, s = _mm(hm, P['w_mlp1'], name="mlp_up", epi=_relu2_epi, out_dtypes=(BF16, BF16))
    S['r'], S['s'] = r, s
    if g_next is None:
        x3, h_next = _mm(s, P['w_mlp2'], name="mlp_down_last", epi=_add_epi, extras=(x2,)), None
    else:
        x3, h_next = _mm(s, P['w_mlp2'], name="mlp_down", epi=_add_norm_epi, extras=(x2, g_next), out_dtypes=(F32, BF16))
    return x3, h_next, S, (got[0] if got else None)


def _layer_bwd(dx3, dx3h, mem, S, P, tabs, send=None):
    ck, sk, cq, sq = tabs
    G = {}
    da = _mm(dx3h, P['w_mlp2'], "nt", name="mlp_down_dx", epi=_drelu2_epi, extras=(S['r'],), out_dtypes=(BF16,))
    G['w_mlp2'] = _mm(S['s'], dx3h, "tn", name="mlp_down_dw")
    G['w_mlp1'] = _mm(S['hm'], da, "tn", name="mlp_up_dw")
    norm_out = dict(epi=_norm_bwd_epi, out_dtypes=(F32, BF16), col_sums=1)
    dx2, dx2h, G['mlp_norm_g'] = _mm(da, P['w_mlp1'], "nt", name="mlp_up_dx", extras=(S['x2'], dx3, P['mlp_norm_g']),
                                     **norm_out)
    dox = _mm(dx2h, P['w_mo'], "nt", name="mem_o_dx")
    G['w_mo'] = _mm(S['ox'], dx2h, "tn", name="mem_o_dw")
    (dqx,), (dkx, dvx) = _rows_vjp(_f_xattn, [S['qx']], [S['kx'], S['vx']], [dox], name="xattn_bwd", drows=[0],
                                   dparams=[0, 1], drow_dtypes=[BF16], tm=_MM_ROWS)
    G['w_mq'] = _mm(S['hx'], dqx, "tn", name="mem_q_dw")
    dx1, dx1h, G['xattn_norm_g'] = _mm(dqx, P['w_mq'], "nt", name="mem_q_dx", extras=(S['x1'], dx2, P['xattn_norm_g']),
                                       **norm_out)
    G['w_mk'] = _mm(S['mn'], dkx, "tn", name="mem_k_dw")
    G['w_mv'] = _mm(S['mn'], dvx, "tn", name="mem_v_dw")
    dmn = _mm(dkx, P['w_mk'], "nt", name="mem_k_dx", epi=_add_epi, extras=(_mm(dvx, P['w_mv'], "nt", name="mem_v_dx"),))
    _, _, G['mem_norm_g'] = _norm_bwd(mem, P['mem_norm_g'], dmn, None, "norm_mem_bwd")
    dymix = _mm(dx1h, P['wout'], "nt", name="out_proj_dx")
    G['wout'] = _mm(S['ymix'], dx1h, "tn", name="out_proj_dw")
    (do, dys, dh, dug_mix), (G['mla_out_g'], G['lru_out_g']) = _rows_vjp(
        _f_mix, [S['o'], S['ys'], S['h'], S['u_gate']], [P['mla_out_g'], P['lru_out_g']], [dymix], name="mix_bwd",
        drows=[0, 1, 2, 3], dparams=[0, 1], drow_dtypes=[BF16, F32, F32, F32])
    g, da_lru = _lru_scan_bwd(S['a'], dh, S['hprev'], name="lru_scan_bwd")
    lru_par = [P['lru_w_a'], P['lru_b_a'], P['lru_w_i'], P['lru_b_i'], P['lru_lambda']]
    (dc_lru,), dpar = _rows_vjp(_f_lru_gates, [S['c_lru']], lru_par, [da_lru, g], name="lru_gates_bwd", drows=[0],
                                dparams=[0, 1, 2, 3, 4], drow_dtypes=[F32], tm=_MM_ROWS)
    G['lru_w_a'], G['lru_b_a'], G['lru_w_i'], G['lru_b_i'], G['lru_lambda'] = dpar
    dc_ssm, dug_ssd, G['ssm_dt_bias'], G['ssm_a_log'], G['ssm_d'], G['ssm_norm_g'] = _ssd_bwd(
        S['c_ssm'], S['u_gate'], S['sall'], dys, P['ssm_dt_bias'], P['ssm_a_log'], P['ssm_d'], P['ssm_norm_g'],
        name="ssd_bwd")
    du_conv, G['conv_w'], G['conv_b'] = _conv_bwd(S['u_conv'], dc_ssm, dc_lru, P['conv_w'], name="conv_bwd")
    delta = _attn_delta(do, S['o'], name="attn_delta")
    dqt, dkv, *got = _attn_bwd(S['q'], S['kv'], do, S['lse'], delta,
                               name="attn_bwd" if send is None else "attn_bwd_scatter", send=send)
    dyq = _qrope_bwd(dqt, cq, sq, name="q_rope_bwd")
    dcqn = _mm(dyq, P['wq'], "nt", name="q_proj_dx")
    G['wq'] = _mm(S['cqn'], dyq, "tn", name="q_proj_dw")
    dakv = _mm(dkv, P['wkv'], "nt", name="kv_proj_dx")
    G['wkv'] = _mm(S['akv'], dkv, "tn", name="kv_proj_dw")
    (du_mla,), (G['mla_q_norm_g'], G['mla_kv_norm_g']) = _rows_vjp(
        _f_mla_prep, [S['u_mla'], ck, sk], [P['mla_q_norm_g'], P['mla_kv_norm_g']], [dcqn, dakv], name="mla_prep_bwd",
        drows=[0], dparams=[0, 1], drow_dtypes=[BF16])
    du = jnp.concatenate([du_mla, (dug_mix + dug_ssd).astype(BF16), du_conv.astype(BF16)], axis=1)
    G['win'] = _mm(S['h1'], du, "tn", name="in_dw")
    dx0, dx0h, G['mix_norm_g'] = _mm(du, P['win'], "nt", name="in_dx", extras=(S['x0'], dx1, P['mix_norm_g']), **norm_out)
    return dx0, dx0h, G, (got[0] if got else None)


class _NoExchange:
    def __init__(self, layers):
        self.layers = layers

    def pieces(self, l):
        return self.layers[l]

    def fwd_send(self, l):
        return None

    def fwd_got(self, l, got):
        pass

    def bwd_send(self, l):
        return None

    def bwd_got(self, l, got):
        pass

    def grads_ready(self, l, pieces):
        pass


def _local_step(x, mem, positions, ex, Ws, tgt):
    tabs = _rope_tables(positions)
    saved, preps, h = [], [], None
    for l in range(DEPTH):
        P = _prep_layer(ex.pieces(l), Ws, l)
        send = ex.fwd_send(l)
        g_next = Ws['mix_norm_g'][l + 1].astype(F32).reshape(1, D) if l + 1 < DEPTH else None
        x, h, S, got = _layer_fwd(x, h, mem, P, tabs, g_next, send)
        if send is not None:
            ex.fwd_got(l, got)
        saved.append(S)
        preps.append(P)
    loss, dx, dxh, dg_final = _loss_head(x, tgt, Ws['final_norm_g'].astype(F32).reshape(1, D), name="loss_head")
    pieces, small = [None] * DEPTH, [None] * DEPTH
    for l in reversed(range(DEPTH)):
        send = ex.bwd_send(l)
        dx, dxh, G, got = _layer_bwd(dx, dxh, mem, saved[l], preps[l], tabs, send)
        if send is not None:
            ex.bwd_got(l, got)
        pieces[l], small[l] = _unprep_pieces(G), _unprep_small(G)
        ex.grads_ready(l, pieces[l])
    grads = {n: jnp.stack([small[l][n] for l in range(DEPTH)]) for n in SMALL if n != 'final_norm_g'}
    grads['final_norm_g'] = dg_final.reshape(D)
    return loss, dx, pieces, grads


def _pack_rows(pieces):
    return jnp.concatenate([pieces[n].reshape(pieces[n].shape[:-2] + (-1, PACK_C)) for n in PIECES], axis=-2)


def _unpack_rows(packed, lead=()):
    out, off = {}, 0
    for n in PIECES:
        rows = _piece_rows(n)
        out[n] = packed[..., off:off + rows, :].reshape(lead + PIECE_SHAPE[n])
        off += rows
    return out


class _StepExchange(_NoExchange):
    def __init__(self, shard):
        self.shard = {n: a.astype(BF16) for n, a in shard.items()}
        self.layers = {}
        self.sums, self.reduced = {}, {}
        self._take(_all_gather(_pack_rows({n: a[0] for n, a in self.shard.items()}), name="gather_w0"), [0])

    def _take(self, gathered, layers):
        per = gathered.reshape(2, 4, len(layers), -1, PACK_C)
        for k, l in enumerate(layers):
            self.layers[l] = {n: _assemble(n, g) for n, g in _unpack_rows(per[:, :, k], lead=(2, 4)).items()}

    def _rest(self):
        return _pack_rows({n: a[1:] for n, a in self.shard.items()}).reshape(-1, PACK_C)

    def fwd_send(self, l):
        return self._rest() if l == 0 else None

    def fwd_got(self, l, got):
        self._take(_finish_gather(self._rest(), got, name="gather_w"), list(range(1, DEPTH)))

    def grads_ready(self, l, pieces):
        x = _pack_rows({n: _disassemble(n, pieces[n]) for n in PIECES})
        self.sums[l] = _chip_sums(x, name="scatter_g")
        if l == 0:
            self.bwd_got(-1, _chip_exchange(self.sums.pop(0), False, name="scatter_g_chips"))

    def bwd_send(self, l):
        return self.sums.pop(l + 1, None)

    def bwd_got(self, l, got):
        self.reduced[l + 1] = _sum_fixed(got, F32, name="scatter_g_sum")


def _adamw_nd(w, g, m, v, name):
    shp = w.shape
    two = lambda a: a.reshape(-1, shp[-1])
    return [r.reshape(shp) for r in _adamw(two(w), two(g), two(m), two(v), name=name)]


def kernel(x, mem, positions, mix_norm_g, w_in, mla_q_norm_g, mla_kv_norm_g, mla_w_uq, mla_w_ukv, mla_out_g, ssm_conv_w, ssm_conv_b, ssm_dt_bias, ssm_a_log, ssm_d, ssm_norm_g, lru_conv_w, lru_conv_b, lru_w_a, lru_b_a, lru_w_i, lru_b_i, lru_lambda, lru_out_g, w_out, xattn_norm_g, mem_norm_g, w_mq, w_mk, w_mv, w_mo, mlp_norm_g, w_mlp1, w_mlp2, final_norm_g, loss_target, m_mix_norm_g, m_w_in, m_mla_q_norm_g, m_mla_kv_norm_g, m_mla_w_uq, m_mla_w_ukv, m_mla_out_g, m_ssm_conv_w, m_ssm_conv_b, m_ssm_dt_bias, m_ssm_a_log, m_ssm_d, m_ssm_norm_g, m_lru_conv_w, m_lru_conv_b, m_lru_w_a, m_lru_b_a, m_lru_w_i, m_lru_b_i, m_lru_lambda, m_lru_out_g, m_w_out, m_xattn_norm_g, m_mem_norm_g, m_w_mq, m_w_mk, m_w_mv, m_w_mo, m_mlp_norm_g, m_w_mlp1, m_w_mlp2, m_final_norm_g, v_mix_norm_g, v_w_in, v_mla_q_norm_g, v_mla_kv_norm_g, v_mla_w_uq, v_mla_w_ukv, v_mla_out_g, v_ssm_conv_w, v_ssm_conv_b, v_ssm_dt_bias, v_ssm_a_log, v_ssm_d, v_ssm_norm_g, v_lru_conv_w, v_lru_conv_b, v_lru_w_a, v_lru_b_a, v_lru_w_i, v_lru_b_i, v_lru_lambda, v_lru_out_g, v_w_out, v_xattn_norm_g, v_mem_norm_g, v_w_mq, v_w_mk, v_w_mv, v_w_mo, v_mlp_norm_g, v_w_mlp1, v_w_mlp2, v_final_norm_g):
    a = locals()
    w = {n: a[n] for n in WEIGHTS}
    m = {n: a['m_' + n] for n in WEIGHTS}
    v = {n: a['v_' + n] for n in WEIGHTS}
    me = 4 * lax.axis_index("x") + 2 * lax.axis_index("y") + lax.axis_index("c")

    ex = _StepExchange({n: _K_FWD[n](w[PIECE_SOURCE[n]]) if n in _K_FWD else w[PIECE_SOURCE.get(n, n)] for n in PIECES})
    Ws = {}
    conv_shapes = [w[n].shape for n in CONV_SHARDED]
    conv_g = _all_gather(_pack([w[n] for n in CONV_SHARDED], F32), name="gather_conv")
    conv_g = conv_g.transpose(1, 0, 2, 3).reshape((N_DEV,) + conv_g.shape[2:])
    for n, g in zip(CONV_SHARDED, _unpack(conv_g, conv_shapes, lead=(N_DEV,))):
        Ws[n] = g.transpose(1, 2, 0, 3).reshape(g.shape[1], g.shape[2], N_DEV * g.shape[3])
    for n in SMALL:
        if n not in CONV_SHARDED:
            Ws[n] = w[n]

    loss_share, dx, _, grads = _local_step(x[0], mem[0], positions[0], ex, Ws, loss_target[0])
    loss = lax.psum(loss_share[0, 0], ("x", "y", "c"))

    g_out = {}
    for n in PIECES:
        g = jnp.stack([_unpack_rows(ex.reduced[l])[n] for l in range(DEPTH)])
        g_out[PIECE_SOURCE.get(n, n)] = _K_INV[n](g) if n in _K_INV else g
    small_shapes = [grads[n].shape for n in SMALL]
    g_small = _all_reduce(_pack([grads[n] for n in SMALL], F32), name="reduce_g")
    for n, g in zip(SMALL, _unpack(g_small, small_shapes)):
        if n in CONV_SHARDED:
            cols = w[n].shape[-1]
            g = lax.dynamic_slice_in_dim(g, me * cols, cols, axis=2)
        g_out[n] = g

    delta, new_m, new_v = {}, {}, {}
    for n in BIG:
        delta[n], new_m[n], new_v[n] = _adamw_nd(w[n], g_out[n], m[n], v[n], "adamw_" + n)
    shapes = [w[n].shape for n in SMALL]
    packed = [_pack([d[n] for n in SMALL], F32) for d in (w, g_out, m, v)]
    for d, res in zip((delta, new_m, new_v), _adamw(*packed, name="adamw_small")):
        d.update(zip(SMALL, _unpack(res, shapes)))

    return (loss, dx[None], *[g_out[n] for n in WEIGHTS], *[delta[n] for n in WEIGHTS],
            *[new_m[n] for n in WEIGHTS], *[new_v[n] for n in WEIGHTS])
```

```python
import functools
import math

import jax
import jax.numpy as jnp
from jax import lax
from jax.experimental import pallas as pl
from jax.experimental.pallas import tpu as pltpu

F32, BF16 = jnp.float32, jnp.bfloat16

D = 1024
DEPTH = 4
N_MEM = 256
EPS = 1e-6
HEADS = 8
NOPE, ROPE, VDIM = 64, 32, 64
Q_RANK, KV_RANK = 256, 128
ROPE_THETA = 10000.0
SSM_CHUNK = 128
LRU_C = 8.0
MEM_HEADS = 4
D_FF = 4 * D
SLAB = 128
LR, B1, B2, AEPS, WD, STEP = 0.001, 0.9, 0.999, 1e-08, 0.01, 10

N_DEV = 8
MESH = pl.DeviceIdType.MESH

U_MLA = 640
U_GATE = 640
U_CONV = 768

_DN = {"nn": (((1,), (0,)), ((), ())), "nt": (((1,), (1,)), ((), ())), "tn": (((0,), (0,)), ((), ()))}


def _dot(a, b, kind):
    return lax.dot_general(a.astype(BF16), b.astype(BF16), _DN[kind], preferred_element_type=F32)


@functools.partial(jax.custom_vjp, nondiff_argnums=(2,))
def _bdot(a, b, kind):
    return _dot(a, b, kind)


def _bdot_fwd(a, b, kind):
    return _dot(a, b, kind), (a, b)


def _bdot_bwd(kind, res, g):
    a, b = res
    if kind == "nn":
        da, db = _dot(g, b, "nt"), _dot(a, g, "tn")
    elif kind == "nt":
        da, db = _dot(g, b, "nn"), _dot(g, a, "tn")
    else:
        da, db = _dot(b, g, "nt"), _dot(a, g, "nn")
    return da.astype(a.dtype), db.astype(b.dtype)


_bdot.defvjp(_bdot_fwd, _bdot_bwd)


def _tile(n, pref):
    if n <= pref:
        return n
    t = pref
    while n % t:
        t -= SLAB
    return t


def _cparams(sem, vmem_mb=48):
    return pltpu.CompilerParams(dimension_semantics=sem, vmem_limit_bytes=vmem_mb * 1024 * 1024)


def _mm(a, b, kind="nn", *, name, out_dtypes=(F32,), epi=None, extras=(), col_sums=0, tm=1024, tn=1024, tk=1024):
    if kind == "tn":
        K, M = a.shape
    else:
        M, K = a.shape
    N = b.shape[0] if kind == "nt" else b.shape[1]
    if a.dtype == F32 or b.dtype == F32:
        tk = tk // 2
    tm, tn, tk = _tile(M, tm), _tile(N, tn), _tile(K, tk)
    assert not col_sums or tn == N, "column sums need one tile across the columns"
    nk = K // tk
    a_spec = pl.BlockSpec((tk, tm), lambda i, j, k: (k, i)) if kind == "tn" else pl.BlockSpec((tm, tk), lambda i, j, k: (i, k))
    b_spec = pl.BlockSpec((tn, tk), lambda i, j, k: (j, k)) if kind == "nt" else pl.BlockSpec((tk, tn), lambda i, j, k: (k, j))
    o_spec = pl.BlockSpec((tm, tn), lambda i, j, k: (i, j))
    vec_spec = pl.BlockSpec((1, tn), lambda i, j, k: (0, j))
    ex_specs = [vec_spec if e.shape[0] == 1 else o_spec for e in extras]
    n_ex, n_out = len(extras), len(out_dtypes)

    def body(*refs):
        a_ref, b_ref = refs[:2]
        ex = refs[2:2 + n_ex]
        outs = refs[2 + n_ex:2 + n_ex + n_out]
        acc = refs[-1]
        k = pl.program_id(2)
        first_row_tile = pl.program_id(0) == 0

        def finish(r):
            res = epi(r, *[e[...] for e in ex]) if epi is not None else (r,)
            for o, v in zip(outs, res):
                o[...] = v.astype(o.dtype)
            sums = refs[2 + n_ex + n_out:2 + n_ex + n_out + col_sums]

            @pl.when(first_row_tile)
            def _():
                for o in sums:
                    o[...] = jnp.zeros_like(o)

            for o, v in zip(sums, res[n_out:]):
                o[...] += v

        if nk == 1:
            finish(_dot(a_ref[...], b_ref[...], kind))
            return

        @pl.when(k == 0)
        def _():
            acc[...] = _dot(a_ref[...], b_ref[...], kind)

        @pl.when(k > 0)
        def _():
            acc[...] += _dot(a_ref[...], b_ref[...], kind)

        @pl.when(k == nk - 1)
        def _():
            finish(acc[...])

    res = pl.pallas_call(
        body,
        name=name,
        grid=(M // tm, N // tn, nk),
        in_specs=[a_spec, b_spec] + ex_specs,
        out_specs=[o_spec] * n_out + [vec_spec] * col_sums,
        out_shape=[jax.ShapeDtypeStruct((M, N), dt) for dt in out_dtypes] + [jax.ShapeDtypeStruct((1, N), F32)] * col_sums,
        scratch_shapes=[pltpu.VMEM((tm, tn), F32)],
        compiler_params=_cparams(("arbitrary" if col_sums else "parallel", "parallel", "arbitrary")),
    )(a, b, *extras)
    return res[0] if n_out + col_sums == 1 else res


_MM_ROWS = 1024


def _row_spec(arr, tm):
    return pl.BlockSpec((tm, arr.shape[1]), lambda i: (i, 0))


def _full_spec(arr):
    nd = arr.ndim
    return pl.BlockSpec(arr.shape, lambda i: (0,) * nd)


def _rows_fwd(fn, rows, params, outs, *, name, tm=256):
    T = rows[0].shape[0]
    tm = min(tm, T)
    nr, npar = len(rows), len(params)

    def body(*refs):
        ins = [r[...] for r in refs[:nr + npar]]
        res = fn(*ins)
        for o, v in zip(refs[nr + npar:], res):
            o[...] = v.astype(o.dtype)

    res = pl.pallas_call(
        body,
        name=name,
        grid=(T // tm,),
        in_specs=[_row_spec(r, tm) for r in rows] + [_full_spec(p) for p in params],
        out_specs=[pl.BlockSpec((tm, c), lambda i: (i, 0)) for c, _ in outs],
        out_shape=[jax.ShapeDtypeStruct((T, c), dt) for c, dt in outs],
        compiler_params=_cparams(("parallel",)),
    )(*rows, *params)
    return res


def _rows_vjp(fn, rows, params, cts, *, name, drows, dparams, drow_dtypes, add=None, twin=False, tm=256):
    T = rows[0].shape[0]
    tm = min(tm, T)
    nr, npar, nct = len(rows), len(params), len(cts)
    n_add = 0 if add is None else 1
    n_dr, n_dp = len(drows), len(dparams)
    n_tw = 1 if twin else 0

    def body(*refs):
        row_t = [r[...] for r in refs[:nr]]
        par_t = [r[...] for r in refs[nr:nr + npar]]
        ct_t = [r[...].astype(F32) for r in refs[nr + npar:nr + npar + nct]]
        pos = nr + npar + nct
        add_t = refs[pos][...] if n_add else None
        pos += n_add
        drow_refs = refs[pos:pos + n_dr]
        dpar_refs = refs[pos + n_dr:pos + n_dr + n_dp]

        def g(*dargs):
            rr, pp = list(row_t), list(par_t)
            for idx, v in zip(drows, dargs[:n_dr]):
                rr[idx] = v
            for idx, v in zip(dparams, dargs[n_dr:]):
                pp[idx] = v
            return tuple(fn(*rr, *pp))

        prim = [row_t[i].astype(F32) for i in drows] + [par_t[i].astype(F32) for i in dparams]
        _, vjp = jax.vjp(g, *prim)
        grads = vjp(tuple(ct_t))
        for n, (o, v) in enumerate(zip(drow_refs, grads[:n_dr])):
            if n == 0 and n_add:
                v = v + add_t.astype(F32)
            o[...] = v.astype(o.dtype)
            if n == 0 and n_tw:
                refs[-1][...] = v.astype(BF16)

        @pl.when(pl.program_id(0) == 0)
        def _():
            for o in dpar_refs:
                o[...] = jnp.zeros_like(o)

        for o, v in zip(dpar_refs, grads[n_dr:]):
            o[...] += v

    res = pl.pallas_call(
        body,
        name=name,
        grid=(T // tm,),
        in_specs=[_row_spec(r, tm) for r in rows] + [_full_spec(p) for p in params] + [_row_spec(c, tm) for c in cts]
        + ([_row_spec(add, tm)] if n_add else []),
        out_specs=[_row_spec(rows[i], tm) for i in drows] + [_full_spec(params[i]) for i in dparams]
        + [_row_spec(rows[drows[0]], tm)] * n_tw,
        out_shape=[jax.ShapeDtypeStruct(rows[i].shape, dt) for i, dt in zip(drows, drow_dtypes)]
        + [jax.ShapeDtypeStruct(params[i].shape, F32) for i in dparams]
        + [jax.ShapeDtypeStruct(rows[drows[0]].shape, BF16)] * n_tw,
        compiler_params=_cparams(("arbitrary",)),
    )(*rows, *params, *cts, *([add] if n_add else []))
    return list(res[:n_dr]) + list(res[n_dr + n_dp:]), list(res[n_dr:n_dr + n_dp])


def _rms(x, g, n):
    return x * lax.rsqrt(jnp.sum(x * x, axis=-1, keepdims=True) * (1.0 / n) + EPS) * g


def _sigmoid(x):
    return 1.0 / (1.0 + jnp.exp(-x))


def _silu(x):
    return x * _sigmoid(x)


def _softplus(x):
    return jnp.maximum(x, 0.0) + jnp.log(1.0 + jnp.exp(-jnp.abs(x)))


def _gelu_tanh(x):
    return 0.5 * x * (1.0 + jnp.tanh(math.sqrt(2.0 / math.pi) * (x + 0.044715 * x * x * x)))


def _lane(shape):
    return lax.broadcasted_iota(jnp.int32, shape, len(shape) - 1)


def _col(x, h):
    return jnp.sum(jnp.where(_lane(x.shape) == h, x, 0.0), axis=-1, keepdims=True)


def _f_norm(x, g):
    return (_rms(x.astype(F32), g, x.shape[-1]),)


def _f_mla_prep(u, ck, sk, gq, gkv):
    u = u.astype(F32)
    cq = _rms(u[:, 0:256], gq, Q_RANK)
    ckv = _rms(u[:, 256:384], gkv, KV_RANK)
    kr = u[:, 384:512] * ck + u[:, 512:640] * sk
    return cq, jnp.concatenate([ckv, kr], axis=1)


def _f_qrope(y, cq, sq):
    y = y.astype(F32)
    c8, s8 = jnp.tile(cq, (1, HEADS)), jnp.tile(sq, (1, HEADS))
    return (y[:, :HEADS * SLAB] * c8 + y[:, HEADS * SLAB:] * s8,)


def _f_lru_gates(xc, wa, ba, wi, bi, lam):
    xc = xc.astype(F32)
    r = _sigmoid(_bdot(xc, wa, "nn") + ba)
    i = _sigmoid(_bdot(xc, wi, "nn") + bi)
    log_a = -LRU_C * r * _softplus(-lam)
    a = jnp.exp(log_a)
    x2 = 2.0 * log_a
    m1 = jnp.where(x2 > -0.02, -x2 * (1.0 + x2 * (0.5 + x2 * (1.0 / 6.0 + x2 * (1.0 / 24.0)))), 1.0 - jnp.exp(x2))
    return a, jnp.sqrt(m1) * (i * xc)


def _f_mix(o, ys, h, ug, g_mla, g_lru):
    o = o.astype(F32)
    y_mla = _rms(o, g_mla, HEADS * VDIM)
    y_lru = _rms(h.astype(F32) * _gelu_tanh(ug[:, 256:512].astype(F32)), g_lru, 256)
    return (jnp.concatenate([y_mla, ys.astype(F32), y_lru], axis=1),)


def _f_xattn(q, k, v):
    hd = D // MEM_HEADS
    outs = []
    for h in range(MEM_HEADS):
        sl = slice(h * hd, (h + 1) * hd)
        s = _bdot(q[:, sl], k[:, sl], "nt") * (1.0 / math.sqrt(hd))
        s = s - jnp.max(s, axis=-1, keepdims=True)
        p = jnp.exp(s)
        p = p / jnp.sum(p, axis=-1, keepdims=True)
        outs.append(_bdot(p, v[:, sl], "nn"))
    return (jnp.concatenate(outs, axis=1),)


def _split_dot(tri, a, kind):
    a_hi = a.astype(BF16)
    r1 = a - a_hi.astype(F32)
    a_mid = r1.astype(BF16)
    a_lo = (r1 - a_mid.astype(F32)).astype(BF16)
    return _dot(tri, a_hi, kind) + _dot(tri, a_mid, kind) + _dot(tri, a_lo, kind)


@jax.custom_vjp
def _tri_cumsum(tri, a):
    return _split_dot(tri, a, "nn")


def _tri_cumsum_fwd(tri, a):
    return _split_dot(tri, a, "nn"), tri


def _tri_cumsum_bwd(tri, g):
    return jnp.zeros_like(tri), _split_dot(tri, g, "tn")


_tri_cumsum.defvjp(_tri_cumsum_fwd, _tri_cumsum_bwd)


def _f_ssd_chunk(c, ug, s0, s1, dtb, alog, dsk, ng):
    L = c.shape[0]
    c = c.astype(F32)
    xbc = _silu(c)
    xs, bm, cm = xbc[:, 0:256], xbc[:, 256:384], xbc[:, 384:512]
    z = ug[:, 0:256].astype(F32)
    dt = _softplus(ug[:, 512:640].astype(F32) + dtb)
    a = dt * (-jnp.exp(alog))
    rowi = lax.broadcasted_iota(jnp.int32, (L, L), 0)
    coli = lax.broadcasted_iota(jnp.int32, (L, L), 1)
    tril = rowi >= coli
    acum = _tri_cumsum(tril.astype(BF16), a)
    acum_t = acum.T
    lane = _lane((1, SLAB))
    lo = lane < 64
    ys, new_s = [], []
    for g in range(2):
        gm = (lane >= 64 * g) & (lane < 64 * g + 64)
        bg, cg = jnp.where(gm, bm, 0.0), jnp.where(gm, cm, 0.0)
        cb = _bdot(cg, bg, "nt")
        x = xs[:, SLAB * g:SLAB * (g + 1)]
        h0, h1 = 2 * g, 2 * g + 1
        ac0, ac1 = _col(acum, h0), _col(acum, h1)
        xdt = x * jnp.where(lo, _col(dt, h0), _col(dt, h1))
        ac_l = jnp.where(lo, ac0, ac1)
        tot = acum[L - 1:L, :]
        tot_l = jnp.where(lo, _col(tot, h0), _col(tot, h1))
        yd = jnp.zeros((L, SLAB), F32)
        for hh, acc, hm in ((h0, ac0, lo), (h1, ac1, jnp.logical_not(lo))):
            seg = acc - acum_t[hh:hh + 1, :]
            lm = jnp.where(tril, jnp.exp(jnp.where(tril, seg, 0.0)), 0.0)
            yd = yd + _bdot(cb * lm, jnp.where(hm, xdt, 0.0), "nn")
        sg = (s0, s1)[g]
        y_off = _bdot(cg, sg, "nn") * jnp.exp(ac_l)
        st = _bdot(bg, xdt * jnp.exp(tot_l - ac_l), "tn")
        new_s.append(jnp.exp(tot_l) * sg + st)
        y = yd + y_off + jnp.where(lo, _col(dsk, h0), _col(dsk, h1)) * x
        y = y * _silu(z[:, SLAB * g:SLAB * (g + 1)])
        ys.append(_rms(y, ng[:, SLAB * g:SLAB * (g + 1)], SLAB))
    return jnp.concatenate(ys, axis=1), new_s[0], new_s[1]


_SSD_TILE = 512


def _ssd_fwd(c, ug, dtb, alog, dsk, ng, *, name):
    T = c.shape[0]
    tm = min(_SSD_TILE, T)
    ncs = tm // SSM_CHUNK
    nc = T // SSM_CHUNK

    def body(c_ref, ug_ref, dtb_ref, alog_ref, dsk_ref, ng_ref, y_ref, sall_ref, s_scr):
        @pl.when(pl.program_id(0) == 0)
        def _():
            s_scr[...] = jnp.zeros_like(s_scr)

        s0, s1 = s_scr[0], s_scr[1]
        for k in range(ncs):
            rows = slice(k * SSM_CHUNK, (k + 1) * SSM_CHUNK)
            sall_ref[k, 0] = s0
            sall_ref[k, 1] = s1
            y, s0, s1 = _f_ssd_chunk(c_ref[rows, :], ug_ref[rows, :], s0, s1, dtb_ref[...], alog_ref[...],
                                     dsk_ref[...], ng_ref[...])
            y_ref[rows, :] = y
        s_scr[0] = s0
        s_scr[1] = s1

    y, sall = pl.pallas_call(
        body,
        name=name,
        grid=(T // tm,),
        in_specs=[_row_spec(c, tm), _row_spec(ug, tm)] + [_full_spec(p) for p in (dtb, alog, dsk, ng)],
        out_specs=[pl.BlockSpec((tm, 256), lambda i: (i, 0)), pl.BlockSpec((ncs, 2, SLAB, SLAB), lambda i: (i, 0, 0, 0))],
        out_shape=[jax.ShapeDtypeStruct((T, 256), F32), jax.ShapeDtypeStruct((nc, 2, SLAB, SLAB), F32)],
        scratch_shapes=[pltpu.VMEM((2, SLAB, SLAB), F32)],
        compiler_params=_cparams(("arbitrary",)),
    )(c, ug, dtb, alog, dsk, ng)
    return y, sall


def _ssd_bwd(c, ug, sall, dy, dtb, alog, dsk, ng, *, name):
    T = c.shape[0]
    tm = min(_SSD_TILE, T)
    ncs = tm // SSM_CHUNK
    nt = T // tm

    def body(c_ref, ug_ref, sall_ref, dy_ref, dtb_ref, alog_ref, dsk_ref, ng_ref,
             dc_ref, dug_ref, ddtb_ref, dalog_ref, ddsk_ref, dng_ref, ds_scr):
        @pl.when(pl.program_id(0) == 0)
        def _():
            ds_scr[...] = jnp.zeros_like(ds_scr)
            for o in (ddtb_ref, dalog_ref, ddsk_ref, dng_ref):
                o[...] = jnp.zeros_like(o)

        ds0, ds1 = ds_scr[0], ds_scr[1]
        for k in reversed(range(ncs)):
            rows = slice(k * SSM_CHUNK, (k + 1) * SSM_CHUNK)
            prim = (c_ref[rows, :].astype(F32), ug_ref[rows, :].astype(F32), sall_ref[k, 0], sall_ref[k, 1],
                    dtb_ref[...], alog_ref[...], dsk_ref[...], ng_ref[...])
            _, vjp = jax.vjp(_f_ssd_chunk, *prim)
            dc, dug, ds0, ds1, g_dtb, g_alog, g_dsk, g_ng = vjp((dy_ref[rows, :].astype(F32), ds0, ds1))
            dc_ref[rows, :] = dc
            dug_ref[rows, :] = dug
            ddtb_ref[...] += g_dtb
            dalog_ref[...] += g_alog
            ddsk_ref[...] += g_dsk
            dng_ref[...] += g_ng
        ds_scr[0] = ds0
        ds_scr[1] = ds1

    rev = lambda i: (nt - 1 - i, 0)
    params = (dtb, alog, dsk, ng)
    res = pl.pallas_call(
        body,
        name=name,
        grid=(nt,),
        in_specs=[pl.BlockSpec((tm, c.shape[1]), rev), pl.BlockSpec((tm, ug.shape[1]), rev),
                  pl.BlockSpec((ncs, 2, SLAB, SLAB), lambda i: (nt - 1 - i, 0, 0, 0)), pl.BlockSpec((tm, 256), rev)]
        + [_full_spec(p) for p in params],
        out_specs=[pl.BlockSpec((tm, 512), rev), pl.BlockSpec((tm, U_GATE), rev)] + [_full_spec(p) for p in params],
        out_shape=[jax.ShapeDtypeStruct((T, 512), F32), jax.ShapeDtypeStruct((T, U_GATE), F32)]
        + [jax.ShapeDtypeStruct(p.shape, F32) for p in params],
        scratch_shapes=[pltpu.VMEM((2, SLAB, SLAB), F32)],
        compiler_params=_cparams(("arbitrary",)),
    )(c, ug, sall, dy, *params)
    return res


_CONV_TILE = 512
_HALO = 8
_CONV_W = 4


def _conv_fwd(u, w, b, *, name):
    T, C = u.shape
    tm = min(_CONV_TILE, T)
    hb = tm // _HALO

    def body(u_ref, prev_ref, w_ref, b_ref, y1_ref, y2_ref, ext):
        i = pl.program_id(0)
        ext[0:_HALO, :] = jnp.where(i > 0, prev_ref[...], 0.0)
        ext[_HALO:, :] = u_ref[...]
        y = jnp.broadcast_to(b_ref[...], (tm, C))
        for k in range(_CONV_W):
            y = y + ext[_HALO - (_CONV_W - 1) + k:_HALO - (_CONV_W - 1) + k + tm, :] * w_ref[k:k + 1, :]
        y1_ref[...] = y[:, 0:512]
        y2_ref[...] = y[:, 512:768]

    return pl.pallas_call(
        body,
        name=name,
        grid=(T // tm,),
        in_specs=[_row_spec(u, tm), pl.BlockSpec((_HALO, C), lambda i: (jnp.maximum(i * hb - 1, 0), 0)),
                  _full_spec(w), _full_spec(b)],
        out_specs=[pl.BlockSpec((tm, 512), lambda i: (i, 0)), pl.BlockSpec((tm, 256), lambda i: (i, 0))],
        out_shape=[jax.ShapeDtypeStruct((T, 512), F32), jax.ShapeDtypeStruct((T, 256), F32)],
        scratch_shapes=[pltpu.VMEM((tm + _HALO, C), F32)],
        compiler_params=_cparams(("parallel",)),
    )(u, u, w, b)


def _conv_bwd(u, dy1, dy2, w, *, name):
    T, C = u.shape
    tm = min(_CONV_TILE, T)
    hb = tm // _HALO
    nt = T // tm

    def body(u_ref, prev_ref, dy1_ref, next1_ref, dy2_ref, next2_ref, w_ref, du_ref, dw_ref, db_ref, ext, dext):
        i = pl.program_id(0)
        ext[0:_HALO, :] = jnp.where(i > 0, prev_ref[...], 0.0)
        ext[_HALO:, :] = u_ref[...]
        dext[0:tm, 0:512] = dy1_ref[...]
        dext[0:tm, 512:768] = dy2_ref[...]
        dext[tm:, 0:512] = jnp.where(i < nt - 1, next1_ref[...], 0.0)
        dext[tm:, 512:768] = jnp.where(i < nt - 1, next2_ref[...], 0.0)

        @pl.when(i == 0)
        def _():
            dw_ref[...] = jnp.zeros_like(dw_ref)
            db_ref[...] = jnp.zeros_like(db_ref)

        dy = dext[0:tm, :]
        du = jnp.zeros((tm, C), F32)
        for k in range(_CONV_W):
            du = du + dext[_CONV_W - 1 - k:_CONV_W - 1 - k + tm, :] * w_ref[k:k + 1, :]
            xk = ext[_HALO - (_CONV_W - 1) + k:_HALO - (_CONV_W - 1) + k + tm, :]
            dw_ref[k:k + 1, :] += jnp.sum(dy * xk, axis=0, keepdims=True)
        du_ref[...] = du
        db_ref[...] += jnp.sum(dy, axis=0, keepdims=True)

    nxt = lambda i: (jnp.minimum((i + 1) * hb, T // _HALO - 1), 0)
    return pl.pallas_call(
        body,
        name=name,
        grid=(nt,),
        in_specs=[_row_spec(u, tm), pl.BlockSpec((_HALO, C), lambda i: (jnp.maximum(i * hb - 1, 0), 0)),
                  _row_spec(dy1, tm), pl.BlockSpec((_HALO, 512), nxt), _row_spec(dy2, tm), pl.BlockSpec((_HALO, 256), nxt),
                  _full_spec(w)],
        out_specs=[pl.BlockSpec((tm, C), lambda i: (i, 0)), _full_spec(w), pl.BlockSpec((1, C), lambda i: (0, 0))],
        out_shape=[jax.ShapeDtypeStruct((T, C), F32), jax.ShapeDtypeStruct(w.shape, F32), jax.ShapeDtypeStruct((1, C), F32)],
        scratch_shapes=[pltpu.VMEM((tm + _HALO, C), F32), pltpu.VMEM((tm + _HALO, C), F32)],
        compiler_params=_cparams(("arbitrary",)),
    )(u, u, dy1, dy1, dy2, dy2, w)


_SCAN_TILE = 1024
_SUB = 8


def _shift_rows(x, d, fill, up):
    r = lax.broadcasted_iota(jnp.int32, x.shape, 0)
    if up:
        return jnp.where(r < _SUB - d, pltpu.roll(x, _SUB - d, 0), fill)
    return jnp.where(r >= d, pltpu.roll(x, d, 0), fill)


def _lru_scan_fwd(a, b, *, name):
    T, W = a.shape
    tr = min(_SCAN_TILE, T)

    def body(a_ref, b_ref, h_ref, hp_ref, carry):
        @pl.when(pl.program_id(0) == 0)
        def _():
            carry[...] = jnp.zeros_like(carry)

        def step(t, cr):
            rows = pl.ds(pl.multiple_of(t * _SUB, _SUB), _SUB)
            aa, bb = a_ref[rows, :], b_ref[rows, :]
            for d in (1, 2, 4):
                bb = bb + aa * _shift_rows(bb, d, 0.0, False)
                aa = aa * _shift_rows(aa, d, 1.0, False)
            h = bb + aa * cr
            h_ref[rows, :] = h
            r = lax.broadcasted_iota(jnp.int32, h.shape, 0)
            hp_ref[rows, :] = jnp.where(r >= 1, pltpu.roll(h, 1, 0), cr)
            return jnp.broadcast_to(h[_SUB - 1:_SUB, :], (_SUB, W))

        carry[...] = lax.fori_loop(0, tr // _SUB, step, carry[...])

    return pl.pallas_call(
        body,
        name=name,
        grid=(T // tr,),
        in_specs=[_row_spec(a, tr), _row_spec(b, tr)],
        out_specs=[pl.BlockSpec((tr, W), lambda i: (i, 0))] * 2,
        out_shape=[jax.ShapeDtypeStruct((T, W), F32)] * 2,
        scratch_shapes=[pltpu.VMEM((_SUB, W), F32)],
        compiler_params=_cparams(("arbitrary",)),
    )(a, b)


def _lru_scan_bwd(a, dh, hprev, *, name):
    T, W = a.shape
    tr = min(_SCAN_TILE, T)
    nt = T // tr

    def body(a_ref, dh_ref, hp_ref, g_ref, da_ref, carry):
        @pl.when(pl.program_id(0) == 0)
        def _():
            carry[...] = jnp.zeros_like(carry)

        nsub = tr // _SUB

        def step(s, cr):
            t = nsub - 1 - s
            rows = pl.ds(pl.multiple_of(t * _SUB, _SUB), _SUB)
            a_t = a_ref[rows, :]
            aa = _shift_rows(a_t, 1, 1.0, True)
            bb = dh_ref[rows, :]
            for d in (1, 2, 4):
                bb = bb + aa * _shift_rows(bb, d, 0.0, True)
                aa = aa * _shift_rows(aa, d, 1.0, True)
            g = bb + aa * cr
            g_ref[rows, :] = g
            da_ref[rows, :] = g * hp_ref[rows, :]
            return jnp.broadcast_to(a_t[0:1, :] * g[0:1, :], (_SUB, W))

        carry[...] = lax.fori_loop(0, nsub, step, carry[...])

    rev = lambda i: (nt - 1 - i, 0)
    return pl.pallas_call(
        body,
        name=name,
        grid=(nt,),
        in_specs=[pl.BlockSpec((tr, W), rev)] * 3,
        out_specs=[pl.BlockSpec((tr, W), rev)] * 2,
        out_shape=[jax.ShapeDtypeStruct((T, W), F32)] * 2,
        scratch_shapes=[pltpu.VMEM((_SUB, W), F32)],
        compiler_params=_cparams(("arbitrary",)),
    )(a, dh, hprev)


_ATT_BLK = 512
_ATT_QPARTS = 2
_ATT_BWD_TRIP = 4
_ATT_SCALE = 1.0 / math.sqrt(NOPE + ROPE)
_ATT_SCALE2 = _ATT_SCALE * math.log2(math.e)


def _call_with_exchange(body, send, gather, *, name, grid, in_specs, out_specs, out_shape, args):
    if send is None:
        return pl.pallas_call(body, name=name, grid=grid, in_specs=in_specs, out_specs=out_specs, out_shape=out_shape,
                              compiler_params=_cparams(("parallel", "arbitrary")))(*args)
    n_in, n_out = len(in_specs), len(out_specs)

    def riding(*refs):
        comm = (refs[n_in], refs[n_in + 1 + n_out]) + tuple(refs[n_in + 2 + n_out:])
        h, i = pl.program_id(0), pl.program_id(1)

        @pl.when((h == 0) & (i == 0))
        def _():
            _chip_start(*comm, gather=gather)

        body(*refs[:n_in], *refs[n_in + 1:n_in + 1 + n_out])

        @pl.when((h == grid[0] - 1) & (i == grid[1] - 1))
        def _():
            _chip_wait(*comm, gather=gather)

    shape = (4,) + send.shape if gather else send.shape
    return pl.pallas_call(
        riding, name=name, grid=grid, in_specs=in_specs + [_ANY], out_specs=out_specs + [_ANY],
        out_shape=out_shape + [jax.ShapeDtypeStruct(shape, send.dtype)], scratch_shapes=_CHIP_SEMS,
        compiler_params=_cparams(("arbitrary", "arbitrary")))(*args, send)


def _attn_fwd(q, kv, *, name, send=None):
    T = q.shape[0]
    blk = min(_ATT_BLK, T)
    nq = T // blk
    parts = _ATT_QPARTS if nq % _ATT_QPARTS == 0 else 1

    def body(q_ref, kv_ref, o_ref, lse_ref):
        i = pl.program_id(1)
        def one(part, j0, nblk, carry, masked):
            m, l, acc = carry
            rows = pl.ds(pl.multiple_of(j0 * blk, blk), nblk * blk)
            s = _dot(q_ref[part * blk:(part + 1) * blk, :], kv_ref[rows, 0:SLAB], "nt")
            if masked:
                col = lax.broadcasted_iota(jnp.int32, s.shape, 1)
                row = lax.broadcasted_iota(jnp.int32, s.shape, 0)
                s = jnp.where(col <= row + (nblk - 1) * blk, s, -jnp.inf)
            m_new = jnp.maximum(m, jnp.max(s, axis=-1, keepdims=True))
            alpha = jnp.exp2(m - m_new)
            p = jnp.exp2(s - m_new)
            l = alpha * l + jnp.sum(p, axis=-1, keepdims=True)
            acc = alpha * acc + _dot(p, kv_ref[rows, SLAB:2 * SLAB], "nn")
            return m_new, l, acc

        def step(t, carry):
            return tuple(one(part, t * parts, parts, carry[part], False) for part in range(parts))

        init = ((jnp.full((blk, 1), -jnp.inf, F32), jnp.zeros((blk, 1), F32), jnp.zeros((blk, SLAB), F32)),) * parts
        carry = lax.fori_loop(0, i, step, init)
        for part in range(parts):
            m, l, acc = one(part, i * parts, part + 1, carry[part], True)
            o_ref[part * blk:(part + 1) * blk, :] = acc / l
            lse_ref[part] = jnp.broadcast_to(m + jnp.log(l) * math.log2(math.e), (blk, SLAB)).T[0:_SUB, :]

    return _call_with_exchange(
        body, send, True,
        name=name,
        grid=(HEADS, nq // parts),
        in_specs=[pl.BlockSpec((parts * blk, SLAB), lambda h, i: (i, h)), pl.BlockSpec((T, 2 * SLAB), lambda h, i: (0, h))],
        out_specs=[pl.BlockSpec((parts * blk, SLAB), lambda h, i: (i, h)),
                   pl.BlockSpec((None, parts, _SUB, blk), lambda h, i: (h, i, 0, 0))],
        out_shape=[jax.ShapeDtypeStruct((T, HEADS * SLAB), F32), jax.ShapeDtypeStruct((HEADS, nq, _SUB, blk), F32)],
        args=(q, kv))


def _attn_delta(do, o, *, name):
    T = o.shape[0]
    blk = min(_ATT_BLK, T)
    nq = T // blk

    def body(do_ref, o_ref, d_ref):
        for h in range(HEADS):
            cols = slice(h * SLAB, (h + 1) * SLAB)
            dl = jnp.sum(do_ref[:, cols].astype(F32) * o_ref[:, cols], axis=-1, keepdims=True)
            d_ref[h] = jnp.broadcast_to(dl, (blk, SLAB)).T[0:_SUB, :]

    return pl.pallas_call(
        body,
        name=name,
        grid=(nq,),
        in_specs=[pl.BlockSpec((blk, HEADS * SLAB), lambda i: (i, 0))] * 2,
        out_specs=pl.BlockSpec((HEADS, None, _SUB, blk), lambda i: (0, i, 0, 0)),
        out_shape=jax.ShapeDtypeStruct((HEADS, nq, _SUB, blk), F32),
        compiler_params=_cparams(("parallel",)),
    )(do, o)


def _attn_bwd(q, kv, do, lse, delta, *, name, send=None):
    T = q.shape[0]
    blk = min(_ATT_BLK, T)
    nq = T // blk
    by_head = lambda a: a.reshape(nq, blk, HEADS, SLAB).transpose(2, 0, 3, 1)
    qt, dot_ = by_head(q), by_head(do)

    def body(q_ref, kv_ref, do_ref, lse_ref, dl_ref, qt_ref, dot_ref, dqt_ref, dkv_ref):
        j = pl.program_id(1)

        @pl.when(j == 0)
        def _():
            dqt_ref[...] = jnp.zeros_like(dqt_ref)

        kb, vb = kv_ref[:, 0:SLAB], kv_ref[:, SLAB:2 * SLAB]
        kbt = kb.astype(F32).T.astype(BF16)
        kpos = j * blk + lax.broadcasted_iota(jnp.int32, (blk, blk), 0)

        def step(i, carry, masked):
            dkt, dvt = carry
            rows = pl.ds(pl.multiple_of(i * blk, blk), blk)
            st = _dot(kb, q_ref[rows, :], "nt")
            if masked:
                qpos = i * blk + lax.broadcasted_iota(jnp.int32, (blk, blk), 1)
                st = jnp.where(kpos <= qpos, st, -jnp.inf)
            pt = jnp.exp2(st - (lse_ref[i, 0:1, :] - math.log2(math.log(2.0))))
            dpt = _dot(vb, do_ref[rows, :], "nt")
            dst = pt * (dpt - dl_ref[i, 0:1, :])
            dvt = dvt + _dot(dot_ref[i], pt, "nt")
            dkt = dkt + _dot(qt_ref[i], dst, "nt")
            dqt_ref[i] += _dot(kbt, dst, "nn")
            return dkt, dvt

        zero = jnp.zeros((SLAB, blk), F32)
        carry = step(j, (zero, zero), True)
        rest = nq - 1 - j

        def trip(t, c):
            for u in range(_ATT_BWD_TRIP):
                c = step(j + 1 + _ATT_BWD_TRIP * t + u, c, False)
            return c

        carry = lax.fori_loop(0, rest // _ATT_BWD_TRIP, trip, carry)
        tail = j + 1 + (rest // _ATT_BWD_TRIP) * _ATT_BWD_TRIP
        dkt, dvt = lax.fori_loop(tail, nq, functools.partial(step, masked=False), carry)
        dkv_ref[:, 0:SLAB] = dkt.T.astype(BF16)
        dkv_ref[:, SLAB:2 * SLAB] = (dvt * (1.0 / math.log(2.0))).T.astype(BF16)

    stat_spec = pl.BlockSpec((None, nq, _SUB, blk), lambda h, j: (h, 0, 0, 0))
    head_t_spec = pl.BlockSpec((None, nq, SLAB, blk), lambda h, j: (h, 0, 0, 0))
    return _call_with_exchange(
        body, send, False,
        name=name,
        grid=(HEADS, nq),
        in_specs=[pl.BlockSpec((T, SLAB), lambda h, j: (0, h)), pl.BlockSpec((blk, 2 * SLAB), lambda h, j: (j, h)),
                  pl.BlockSpec((T, SLAB), lambda h, j: (0, h)), stat_spec, stat_spec, head_t_spec, head_t_spec],
        out_specs=[head_t_spec, pl.BlockSpec((blk, 2 * SLAB), lambda h, j: (j, h))],
        out_shape=[jax.ShapeDtypeStruct((HEADS, nq, SLAB, blk), F32), jax.ShapeDtypeStruct((T, HEADS * 2 * SLAB), BF16)],
        args=(q, kv, do, lse, delta, qt, dot_))


def _qrope_bwd(dqt, cq, sq, *, name):
    _, nq, _, blk = dqt.shape
    T = nq * blk

    def body(dqt_ref, c_ref, s_ref, dy_ref):
        c, s = c_ref[...], s_ref[...]
        for h in range(HEADS):
            dq = dqt_ref[h].T
            dy_ref[:, h * SLAB:(h + 1) * SLAB] = (dq * c).astype(BF16)
            dy_ref[:, (HEADS + h) * SLAB:(HEADS + h + 1) * SLAB] = (dq * s).astype(BF16)

    return pl.pallas_call(
        body,
        name=name,
        grid=(nq,),
        in_specs=[pl.BlockSpec((HEADS, None, SLAB, blk), lambda i: (0, i, 0, 0)), _row_spec(cq, blk), _row_spec(sq, blk)],
        out_specs=pl.BlockSpec((blk, 2 * HEADS * SLAB), lambda i: (i, 0)),
        out_shape=jax.ShapeDtypeStruct((T, 2 * HEADS * SLAB), BF16),
        compiler_params=_cparams(("parallel",)),
    )(dqt, cq, sq)


def _loss_head(x, tgt, g, *, name, tm=256):
    T = x.shape[0]
    tm = min(tm, T)

    def body(x_ref, t_ref, g_ref, loss_ref, dx_ref, dxh_ref, dg_ref):
        def f(xv, gv):
            e = _rms(xv, gv, D) - t_ref[...]
            row = jnp.sum(e * e, axis=1, keepdims=True)
            return jnp.sum(row, axis=0, keepdims=True) * (0.5 / D)

        val, vjp = jax.vjp(f, x_ref[...], g_ref[...])
        dxv, dgv = vjp(jnp.ones((1, 1), F32))

        @pl.when(pl.program_id(0) == 0)
        def _():
            loss_ref[...] = jnp.zeros_like(loss_ref)
            dg_ref[...] = jnp.zeros_like(dg_ref)

        dx_ref[...] = dxv
        dxh_ref[...] = dxv.astype(BF16)
        dg_ref[...] += dgv
        loss_ref[...] += jnp.broadcast_to(val, loss_ref.shape)

    return pl.pallas_call(
        body,
        name=name,
        grid=(T // tm,),
        in_specs=[_row_spec(x, tm), _row_spec(tgt, tm), _full_spec(g)],
        out_specs=[pl.BlockSpec((1, SLAB), lambda i: (0, 0)), _row_spec(x, tm), _row_spec(x, tm), _full_spec(g)],
        out_shape=[jax.ShapeDtypeStruct((1, SLAB), F32), jax.ShapeDtypeStruct(x.shape, F32),
                   jax.ShapeDtypeStruct(x.shape, BF16), jax.ShapeDtypeStruct(g.shape, F32)],
        compiler_params=_cparams(("arbitrary",)),
    )(x, tgt, g)


def _row_tile(rows, cols, budget=256 * 1024):
    best = None
    for t in range(16, rows + 1, 16):
        if rows % t == 0 and t * cols <= budget:
            best = t
    return best or rows


def _sum_fixed(x, out_dtype, *, name):
    n, R, C = x.shape
    tr = _row_tile(R, C)

    def body(x_ref, o_ref):
        acc = x_ref[0].astype(F32)
        for k in range(1, n):
            acc = acc + x_ref[k].astype(F32)
        o_ref[...] = acc.astype(o_ref.dtype)

    return pl.pallas_call(
        body,
        name=name,
        grid=(R // tr,),
        in_specs=[pl.BlockSpec((n, tr, C), lambda i: (0, i, 0))],
        out_specs=pl.BlockSpec((tr, C), lambda i: (i, 0)),
        out_shape=jax.ShapeDtypeStruct((R, C), out_dtype),
        compiler_params=_cparams(("parallel",)),
    )(x)


def _adamw(w, g, m, v, *, name):
    R, C = w.shape
    tr = _row_tile(R, C, 128 * 1024)

    def body(w_ref, g_ref, m_ref, v_ref, d_ref, nm_ref, nv_ref):
        gv = g_ref[...]
        mv = B1 * m_ref[...] + (1.0 - B1) * gv
        vv = B2 * v_ref[...] + (1.0 - B2) * (gv * gv)
        m_hat = mv / (1.0 - B1 ** STEP)
        v_hat = vv / (1.0 - B2 ** STEP)
        d_ref[...] = -LR * (m_hat / (jnp.sqrt(v_hat) + AEPS) + WD * w_ref[...])
        nm_ref[...] = mv
        nv_ref[...] = vv

    spec = pl.BlockSpec((tr, C), lambda i: (i, 0))
    return pl.pallas_call(
        body, name=name, grid=(R // tr,), in_specs=[spec] * 4, out_specs=[spec] * 3,
        out_shape=[jax.ShapeDtypeStruct((R, C), F32)] * 3, compiler_params=_cparams(("parallel",)),
    )(w, g, m, v)


_FLIPS = ((1, 0), (0, 1), (1, 1))
_ANY = pl.BlockSpec(memory_space=pl.ANY)


def _me():
    return lax.axis_index("x"), lax.axis_index("y"), lax.axis_index("c")


def _flip(mx, my, f):
    return (1 - mx if f[0] else mx), (1 - my if f[1] else my)


_CHIP_SEMS = [pltpu.SemaphoreType.DMA((3,)), pltpu.SemaphoreType.DMA((3,)), pltpu.SemaphoreType.DMA]


def _chip_copies(x_ref, out_ref, send_sems, recv_sems, local_sem, gather):
    mx, my, mc = _me()
    mine = 2 * mx + my
    outgoing, incoming = [], []
    for k, f in enumerate(_FLIPS):
        px, py = _flip(mx, my, f)
        peer = 2 * px + py
        src = x_ref if gather else x_ref.at[peer]
        for dst, to in ((out_ref.at[mine], outgoing), (out_ref.at[peer], incoming)):
            to.append(pltpu.make_async_remote_copy(src_ref=src, dst_ref=dst, send_sem=send_sems.at[k],
                                                   recv_sem=recv_sems.at[k], device_id=(px, py, mc), device_id_type=MESH))
    local = None if gather else pltpu.make_async_copy(x_ref.at[mine], out_ref.at[mine], local_sem)
    return outgoing, incoming, local


def _chip_start(*refs, gather):
    outgoing, _, local = _chip_copies(*refs, gather)
    if local is not None:
        local.start()
    for cp in outgoing:
        cp.start()


def _chip_wait(*refs, gather):
    outgoing, incoming, local = _chip_copies(*refs, gather)
    for cp in incoming:
        cp.wait_recv()
    for cp in outgoing:
        cp.wait_send()
    if local is not None:
        local.wait()


def _chip_exchange(x, gather, *, name):
    shape = x.shape if not gather else (4,) + x.shape

    def body(*refs):
        _chip_start(*refs, gather=gather)
        _chip_wait(*refs, gather=gather)

    return pl.pallas_call(
        body, name=name, in_specs=[_ANY], out_specs=_ANY, out_shape=jax.ShapeDtypeStruct(shape, x.dtype),
        scratch_shapes=_CHIP_SEMS,
    )(x)


def _core_exchange(x, half, *, name):
    shape = x.shape[1:] if half else x.shape

    def body(x_ref, out_ref, send_sem, recv_sem):
        mx, my, mc = _me()
        src = x_ref.at[1 - mc] if half else x_ref
        cp = pltpu.make_async_remote_copy(src_ref=src, dst_ref=out_ref, send_sem=send_sem, recv_sem=recv_sem,
                                          device_id=(mx, my, 1 - mc), device_id_type=MESH)
        cp.start()
        cp.wait()

    return pl.pallas_call(
        body, name=name, in_specs=[_ANY], out_specs=_ANY, out_shape=jax.ShapeDtypeStruct(shape, x.dtype),
        scratch_shapes=[pltpu.SemaphoreType.DMA, pltpu.SemaphoreType.DMA],
    )(x)


def _all_gather(x, *, name):
    return _finish_gather(x, _chip_exchange(x, True, name=name + "_chips"), name=name)


def _finish_gather(x, g4, *, name):
    mx, my, mc = _me()
    own = (jnp.arange(4) == 2 * mx + my).reshape((4,) + (1,) * x.ndim)
    g4 = jnp.where(own, x[None], g4)
    sib = _core_exchange(g4, False, name=name + "_cores")
    return jnp.where(mc == 0, jnp.stack([g4, sib]), jnp.stack([sib, g4]))


def _add_own_half(x, got, *, name):
    _, R, C = x.shape
    tr = _row_tile(R, C)

    def body(c_ref, x_ref, g_ref, o_ref):
        o_ref[...] = (x_ref[...] + g_ref[...]).astype(o_ref.dtype)

    return pl.pallas_call(
        body,
        name=name,
        grid_spec=pltpu.PrefetchScalarGridSpec(
            num_scalar_prefetch=1, grid=(R // tr,),
            in_specs=[pl.BlockSpec((None, tr, C), lambda i, c: (c[0], i, 0)), pl.BlockSpec((tr, C), lambda i, c: (i, 0))],
            out_specs=pl.BlockSpec((tr, C), lambda i, c: (i, 0))),
        out_shape=jax.ShapeDtypeStruct((R, C), BF16),
        compiler_params=_cparams(("parallel",)),
    )(lax.axis_index("c").astype(jnp.int32).reshape(1), x, got)


def _chip_sums(x, *, name):
    _, _, R, C = x.shape
    got = _core_exchange(x, True, name=name + "_cores")
    return _add_own_half(x.reshape(2, 4 * R, C), got.reshape(4 * R, C), name=name + "_add").reshape(4, R, C)


def _all_reduce(x, *, name):
    g = _all_gather(x, name=name)
    return _sum_fixed(g.reshape((N_DEV,) + x.shape), F32, name=name + "_sum")


WEIGHTS = ['mix_norm_g', 'w_in', 'mla_q_norm_g', 'mla_kv_norm_g', 'mla_w_uq', 'mla_w_ukv', 'mla_out_g', 'ssm_conv_w',
           'ssm_conv_b', 'ssm_dt_bias', 'ssm_a_log', 'ssm_d', 'ssm_norm_g', 'lru_conv_w', 'lru_conv_b', 'lru_w_a',
           'lru_b_a', 'lru_w_i', 'lru_b_i', 'lru_lambda', 'lru_out_g', 'w_out', 'xattn_norm_g', 'mem_norm_g', 'w_mq',
           'w_mk', 'w_mv', 'w_mo', 'mlp_norm_g', 'w_mlp1', 'w_mlp2', 'final_norm_g']
ROW_SHARDED = ('w_in', 'w_out', 'w_mq', 'w_mk', 'w_mv', 'w_mo', 'w_mlp2')
COL_SHARDED = ('mla_w_uq', 'mla_w_ukv', 'w_mlp1')
BIG = tuple(n for n in WEIGHTS if n in ROW_SHARDED + COL_SHARDED)
CONV_SHARDED = ('ssm_conv_w', 'lru_conv_w')
SMALL = tuple(n for n in WEIGHTS if n not in BIG)
PACK_C = 1024


def _pack(arrs, dtype, lead=()):
    flat = jnp.concatenate([a.reshape(lead + (-1,)).astype(dtype) for a in arrs], axis=-1)
    n = flat.shape[-1]
    rows = -(-n // (16 * PACK_C)) * 16
    flat = jnp.pad(flat, [(0, 0)] * len(lead) + [(0, rows * PACK_C - n)])
    return flat.reshape(lead + (rows, PACK_C))


def _unpack(packed, shapes, lead=()):
    flat = packed.reshape(lead + (-1,))
    out, off = [], 0
    for s in shapes:
        n = math.prod(s)
        out.append(flat[..., off:off + n].reshape(lead + tuple(s)))
        off += n
    return out


def _pad_lanes(v, n=SLAB):
    return jnp.pad(v.astype(F32), (0, n - v.shape[0])).reshape(1, n)


PIECES = ('win', 'wq', 'wkv', 'wout', 'w_mq', 'w_mk', 'w_mv', 'w_mo', 'w_mlp1', 'w_mlp2')
PIECE_SOURCE = {'win': 'w_in', 'wq': 'mla_w_uq', 'wkv': 'mla_w_ukv', 'wout': 'w_out'}
PIECE_SHAPE = {'win': (128, 2048), 'wq': (Q_RANK, 2 * SLAB), 'wkv': (KV_RANK, 2 * SLAB), 'wout': (128, D),
               'w_mq': (128, D), 'w_mk': (128, D), 'w_mv': (128, D), 'w_mo': (128, D), 'w_mlp1': (D, 512),
               'w_mlp2': (512, D)}
PIECE_COLS = ('wq', 'wkv', 'w_mlp1')


def _k_win(w):
    kr = w[..., 384:416]
    zc = lambda k: jnp.zeros(w.shape[:-1] + (k,), w.dtype)
    return jnp.concatenate(
        [w[..., 0:384], kr, zc(96), kr[..., 16:32], kr[..., 0:16], zc(96),
         w[..., 416:672], w[..., 1444:1700], w[..., 1184:1188], zc(124),
         w[..., 672:1184], w[..., 1188:1444]], axis=-1)


def _k_win_inv(m):
    gk = m[..., 384:416] + jnp.concatenate([m[..., 528:544], m[..., 512:528]], axis=-1)
    return jnp.concatenate([m[..., 0:384], gk, m[..., 640:896], m[..., 1280:1792], m[..., 1152:1156], m[..., 1792:2048],
                            m[..., 896:1152]], axis=-1)


def _k_wq(w):
    nh = w.shape[-1] // (NOPE + ROPE)
    w = w.reshape(w.shape[:-1] + (nh, NOPE + ROPE))
    nope, r1, r2 = w[..., :NOPE], w[..., NOPE:NOPE + 16], w[..., NOPE + 16:]
    z = lambda k: jnp.zeros(w.shape[:-1] + (k,), w.dtype)
    both = jnp.stack([jnp.concatenate([nope, r1, r2, z(32)], -1), jnp.concatenate([z(64), r2, r1, z(32)], -1)], axis=-3)
    return both.reshape(w.shape[:-2] + (2 * nh * SLAB,))


def _k_wq_inv(m):
    nh = m.shape[-1] // (2 * SLAB)
    m = m.reshape(m.shape[:-1] + (2, nh, SLAB))
    q0, q1 = m[..., 0, :, :], m[..., 1, :, :]
    w = jnp.concatenate([q0[..., :64], q0[..., 64:80] + q1[..., 80:96], q0[..., 80:96] + q1[..., 64:80]], -1)
    return w.reshape(w.shape[:-2] + (nh * (NOPE + ROPE),))


def _k_wkv(w):
    nh = w.shape[-1] // (NOPE + VDIM)
    w = w.reshape(w.shape[:-1] + (nh, NOPE + VDIM))
    z = jnp.zeros(w.shape[:-1] + (64,), w.dtype)
    return jnp.concatenate([w[..., :NOPE], z, w[..., NOPE:], z], -1).reshape(w.shape[:-2] + (nh * 2 * SLAB,))


def _k_wkv_inv(m):
    nh = m.shape[-1] // (2 * SLAB)
    m = m.reshape(m.shape[:-1] + (nh, 2 * SLAB))
    return jnp.concatenate([m[..., :NOPE], m[..., SLAB:SLAB + VDIM]], -1).reshape(m.shape[:-2] + (nh * (NOPE + VDIM),))


_K_FWD = {'win': _k_win, 'wq': _k_wq, 'wkv': _k_wkv}
_K_INV = {'win': _k_win_inv, 'wq': _k_wq_inv, 'wkv': _k_wkv_inv}


def _assemble(piece, g):
    _, _, a, b = g.shape
    if piece == 'wq':
        return g.reshape(2, 4, a, 2, SLAB).transpose(2, 3, 1, 0, 4).reshape(a, N_DEV * b)
    if piece in PIECE_COLS:
        return g.transpose(2, 1, 0, 3).reshape(a, N_DEV * b)
    return g.transpose(1, 0, 2, 3).reshape(N_DEV * a, b)


def _disassemble(piece, full):
    a, b = PIECE_SHAPE[piece]
    if piece == 'wq':
        return full.reshape(a, 2, 4, 2, SLAB).transpose(3, 2, 0, 1, 4).reshape(2, 4, a, b)
    if piece in PIECE_COLS:
        return full.reshape(a, 4, 2, b).transpose(2, 1, 0, 3)
    return full.reshape(4, 2, a, b).transpose(1, 0, 2, 3)


def _piece_rows(piece):
    a, b = PIECE_SHAPE[piece]
    return a * b // PACK_C


def _prep_layer(pieces, Ws, l):
    P = dict(pieces)
    ri, ci = jnp.arange(SLAB)[:, None], jnp.arange(2 * SLAB)[None, :]
    sel = ((ri < ROPE) & (ci == ri + NOPE)).astype(P['wkv'].dtype)
    P['wkv'] = jnp.concatenate([P['wkv'], jnp.tile(sel, (1, HEADS))], axis=0)
    wout = P['wout']
    mla_rows = jnp.pad(wout[:HEADS * VDIM].reshape(HEADS, VDIM, D), ((0, 0), (0, SLAB - VDIM), (0, 0)))
    P['wout'] = jnp.concatenate([mla_rows.reshape(HEADS * SLAB, D), wout[HEADS * VDIM:]], axis=0)
    W = Ws
    row = lambda n: W[n][l].astype(F32).reshape(1, -1)
    for n in ('mix_norm_g', 'mla_q_norm_g', 'mla_kv_norm_g', 'ssm_norm_g', 'lru_lambda', 'lru_out_g', 'xattn_norm_g',
              'mem_norm_g', 'mlp_norm_g'):
        P[n] = row(n)
    P['mla_out_g'] = jnp.pad(W['mla_out_g'][l].astype(F32).reshape(HEADS, VDIM), ((0, 0), (0, SLAB - VDIM))).reshape(1, -1)
    for n in ('ssm_dt_bias', 'ssm_a_log', 'ssm_d'):
        P[n] = _pad_lanes(W[n][l])
    P['conv_w'] = jnp.pad(jnp.concatenate([W['ssm_conv_w'][l], W['lru_conv_w'][l]], axis=1).astype(F32), ((0, 4), (0, 0)))
    P['conv_b'] = jnp.concatenate([W['ssm_conv_b'][l], W['lru_conv_b'][l]]).astype(F32).reshape(1, -1)
    for n in ('lru_w_a', 'lru_w_i'):
        P[n] = jnp.concatenate([jnp.pad(W[n][l, k].astype(F32), ((0, 0), (64 * k, 192 - 64 * k))) for k in range(4)], axis=0)
    for n in ('lru_b_a', 'lru_b_i'):
        P[n] = W[n][l].astype(F32).reshape(1, -1)
    return P


def _unprep_pieces(G):
    o = {n: G[n] for n in PIECES}
    o['wkv'] = G['wkv'][:KV_RANK]
    wo = G['wout']
    o['wout'] = jnp.concatenate([wo[:HEADS * SLAB].reshape(HEADS, SLAB, D)[:, :VDIM].reshape(HEADS * VDIM, D),
                                 wo[HEADS * SLAB:]], axis=0)
    return o


def _unprep_small(G):
    o = {}
    for n in ('mix_norm_g', 'mla_q_norm_g', 'mla_kv_norm_g', 'ssm_norm_g', 'lru_lambda', 'lru_out_g', 'xattn_norm_g',
              'mem_norm_g', 'mlp_norm_g', 'lru_b_a', 'lru_b_i'):
        o[n] = G[n].reshape(-1)
    o['lru_b_a'] = o['lru_b_a'].reshape(4, 64)
    o['lru_b_i'] = o['lru_b_i'].reshape(4, 64)
    o['mla_out_g'] = G['mla_out_g'].reshape(HEADS, SLAB)[:, :VDIM].reshape(-1)
    for n in ('ssm_dt_bias', 'ssm_a_log', 'ssm_d'):
        o[n] = G[n][0, :4]
    o['ssm_conv_w'], o['lru_conv_w'] = G['conv_w'][:4, :512], G['conv_w'][:4, 512:]
    o['ssm_conv_b'], o['lru_conv_b'] = G['conv_b'][0, :512], G['conv_b'][0, 512:]
    for n in ('lru_w_a', 'lru_w_i'):
        o[n] = jnp.stack([G[n][64 * k:64 * (k + 1), 64 * k:64 * (k + 1)] for k in range(4)])
    return o


def _rope_tables(positions):
    half = ROPE // 2
    inv_freq = ROPE_THETA ** (-jnp.arange(half, dtype=F32) * 2.0 / ROPE)
    ang = positions.astype(F32)[:, None] * inv_freq
    cos, sin = jnp.cos(ang), jnp.sin(ang)
    T = positions.shape[0]
    z = lambda k: jnp.zeros((T, k), F32)
    ck = jnp.concatenate([cos, cos, z(96)], axis=1)
    sk = jnp.concatenate([-sin, sin, z(96)], axis=1)
    cq = jnp.concatenate([jnp.ones((T, NOPE), F32), cos, cos, z(32)], axis=1)
    sq = jnp.concatenate([z(NOPE), -sin, sin, z(32)], axis=1)
    return ck, sk, cq * _ATT_SCALE2, sq * _ATT_SCALE2


def _add_epi(acc, res):
    return (acc + res,)


def _add_norm_epi(acc, res, g):
    x = acc + res
    return x, _rms(x, g, x.shape[-1])


def _norm_bwd_epi(acc, x, res, g):
    _, vjp = jax.vjp(lambda xv, gv: _rms(xv, gv, xv.shape[-1]), x, g)
    dx, dg = vjp(acc)
    dx = dx + res
    return dx, dx, dg


def _relu2_epi(acc):
    r = jnp.maximum(acc, 0.0)
    return r, r * r


def _drelu2_epi(acc, r):
    return (acc * (2.0 * r.astype(F32)),)


def _norm(x, g, name):
    return _rows_fwd(_f_norm, [x], [g], [(x.shape[1], BF16)], name=name)[0]


def _norm_bwd(x, g, ct, add, name):
    (dx, dx16), (dg,) = _rows_vjp(_f_norm, [x], [g], [ct], name=name, drows=[0], dparams=[0], drow_dtypes=[F32],
                                  add=add, twin=True)
    return dx, dx16, dg


def _layer_fwd(x0, h1, mem, P, tabs, g_next=None, send=None):
    ck, sk, cq, sq = tabs
    S = {'x0': x0}
    if h1 is None:
        h1 = _norm(x0, P['mix_norm_g'], "norm_mix")
    S['h1'] = h1
    win = P['win']
    u_mla = S['u_mla'] = _mm(h1, win[:, 0:U_MLA], name="in_mla")
    u_gate = S['u_gate'] = _mm(h1, win[:, U_MLA:U_MLA + U_GATE], name="in_gate")
    u_conv = S['u_conv'] = _mm(h1, win[:, U_MLA + U_GATE:], name="in_conv")
    cqn, akv = _rows_fwd(_f_mla_prep, [u_mla, ck, sk], [P['mla_q_norm_g'], P['mla_kv_norm_g']],
                         [(Q_RANK, BF16), (2 * SLAB, BF16)], name="mla_prep")
    S['cqn'], S['akv'] = cqn, akv
    yq = _mm(cqn, P['wq'], name="q_proj")
    q = S['q'] = _rows_fwd(_f_qrope, [yq, cq, sq], [], [(HEADS * SLAB, BF16)], name="q_rope")[0]
    kv = S['kv'] = _mm(akv, P['wkv'], name="kv_proj", out_dtypes=(BF16,))
    o, lse, *got = _attn_fwd(q, kv, name="attn_fwd" if send is None else "attn_fwd_gather", send=send)
    S['o'], S['lse'] = o, lse
    c_ssm, c_lru = _conv_fwd(u_conv, P['conv_w'], P['conv_b'], name="conv_fwd")
    S['c_ssm'], S['c_lru'] = c_ssm, c_lru
    ys, sall = _ssd_fwd(c_ssm, u_gate, P['ssm_dt_bias'], P['ssm_a_log'], P['ssm_d'], P['ssm_norm_g'], name="ssd_fwd")
    S['ys'], S['sall'] = ys, sall
    a, b = _rows_fwd(_f_lru_gates, [c_lru], [P['lru_w_a'], P['lru_b_a'], P['lru_w_i'], P['lru_b_i'], P['lru_lambda']],
                     [(256, F32), (256, F32)], name="lru_gates", tm=_MM_ROWS)
    h, hprev = _lru_scan_fwd(a, b, name="lru_scan")
    S['a'], S['h'], S['hprev'] = a, h, hprev
    ymix = S['ymix'] = _rows_fwd(_f_mix, [o, ys, h, u_gate], [P['mla_out_g'], P['lru_out_g']],
                                 [(HEADS * SLAB + 512, BF16)], name="mix")[0]
    x1, hx = _mm(ymix, P['wout'], name="out_proj", epi=_add_norm_epi, extras=(x0, P['xattn_norm_g']), out_dtypes=(F32, BF16))
    S['x1'], S['hx'] = x1, hx
    qx = S['qx'] = _mm(hx, P['w_mq'], name="mem_q", out_dtypes=(BF16,))
    mn = S['mn'] = _norm(mem, P['mem_norm_g'], "norm_mem")
    kx = S['kx'] = _mm(mn, P['w_mk'], name="mem_k", out_dtypes=(BF16,))
    vx = S['vx'] = _mm(mn, P['w_mv'], name="mem_v", out_dtypes=(BF16,))
    ox = S['ox'] = _rows_fwd(_f_xattn, [qx], [kx, vx], [(D, BF16)], name="xattn", tm=_MM_ROWS)[0]
    x2, hm = _mm(ox, P['w_mo'], name="mem_o", epi=_add_norm_epi, extras=(x1, P['mlp_norm_g']), out_dtypes=(F32, BF16))
    S['x2'], S['hm'] = x2, hm
    r, s = _mm(hm, P['w_mlp1'], name="mlp_up", epi=_relu2_epi, out_dtypes=(BF16, BF16))
    S['r'], S['s'] = r, s
    if g_next is None:
        x3, h_next = _mm(s, P['w_mlp2'], name="mlp_down_last", epi=_add_epi, extras=(x2,)), None
    else:
        x3, h_next = _mm(s, P['w_mlp2'], name="mlp_down", epi=_add_norm_epi, extras=(x2, g_next), out_dtypes=(F32, BF16))
    return x3, h_next, S, (got[0] if got else None)


def _layer_bwd(dx3, dx3h, mem, S, P, tabs, send=None):
    ck, sk, cq, sq = tabs
    G = {}
    da = _mm(dx3h, P['w_mlp2'], "nt", name="mlp_down_dx", epi=_drelu2_epi, extras=(S['r'],), out_dtypes=(BF16,))
    G['w_mlp2'] = _mm(S['s'], dx3h, "tn", name="mlp_down_dw")
    G['w_mlp1'] = _mm(S['hm'], da, "tn", name="mlp_up_dw")
    norm_out = dict(epi=_norm_bwd_epi, out_dtypes=(F32, BF16), col_sums=1)
    dx2, dx2h, G['mlp_norm_g'] = _mm(da, P['w_mlp1'], "nt", name="mlp_up_dx", extras=(S['x2'], dx3, P['mlp_norm_g']),
                                     **norm_out)
    dox = _mm(dx2h, P['w_mo'], "nt", name="mem_o_dx")
    G['w_mo'] = _mm(S['ox'], dx2h, "tn", name="mem_o_dw")
    (dqx,), (dkx, dvx) = _rows_vjp(_f_xattn, [S['qx']], [S['kx'], S['vx']], [dox], name="xattn_bwd", drows=[0],
                                   dparams=[0, 1], drow_dtypes=[BF16], tm=_MM_ROWS)
    G['w_mq'] = _mm(S['hx'], dqx, "tn", name="mem_q_dw")
    dx1, dx1h, G['xattn_norm_g'] = _mm(dqx, P['w_mq'], "nt", name="mem_q_dx", extras=(S['x1'], dx2, P['xattn_norm_g']),
                                       **norm_out)
    G['w_mk'] = _mm(S['mn'], dkx, "tn", name="mem_k_dw")
    G['w_mv'] = _mm(S['mn'], dvx, "tn", name="mem_v_dw")
    dmn = _mm(dkx, P['w_mk'], "nt", name="mem_k_dx", epi=_add_epi, extras=(_mm(dvx, P['w_mv'], "nt", name="mem_v_dx"),))
    _, _, G['mem_norm_g'] = _norm_bwd(mem, P['mem_norm_g'], dmn, None, "norm_mem_bwd")
    dymix = _mm(dx1h, P['wout'], "nt", name="out_proj_dx")
    G['wout'] = _mm(S['ymix'], dx1h, "tn", name="out_proj_dw")
    (do, dys, dh, dug_mix), (G['mla_out_g'], G['lru_out_g']) = _rows_vjp(
        _f_mix, [S['o'], S['ys'], S['h'], S['u_gate']], [P['mla_out_g'], P['lru_out_g']], [dymix], name="mix_bwd",
        drows=[0, 1, 2, 3], dparams=[0, 1], drow_dtypes=[BF16, F32, F32, F32])
    g, da_lru = _lru_scan_bwd(S['a'], dh, S['hprev'], name="lru_scan_bwd")
    lru_par = [P['lru_w_a'], P['lru_b_a'], P['lru_w_i'], P['lru_b_i'], P['lru_lambda']]
    (dc_lru,), dpar = _rows_vjp(_f_lru_gates, [S['c_lru']], lru_par, [da_lru, g], name="lru_gates_bwd", drows=[0],
                                dparams=[0, 1, 2, 3, 4], drow_dtypes=[F32], tm=_MM_ROWS)
    G['lru_w_a'], G['lru_b_a'], G['lru_w_i'], G['lru_b_i'], G['lru_lambda'] = dpar
    dc_ssm, dug_ssd, G['ssm_dt_bias'], G['ssm_a_log'], G['ssm_d'], G['ssm_norm_g'] = _ssd_bwd(
        S['c_ssm'], S['u_gate'], S['sall'], dys, P['ssm_dt_bias'], P['ssm_a_log'], P['ssm_d'], P['ssm_norm_g'],
        name="ssd_bwd")
    du_conv, G['conv_w'], G['conv_b'] = _conv_bwd(S['u_conv'], dc_ssm, dc_lru, P['conv_w'], name="conv_bwd")
    delta = _attn_delta(do, S['o'], name="attn_delta")
    dqt, dkv, *got = _attn_bwd(S['q'], S['kv'], do, S['lse'], delta,
                               name="attn_bwd" if send is None else "attn_bwd_scatter", send=send)
    dyq = _qrope_bwd(dqt, cq, sq, name="q_rope_bwd")
    dcqn = _mm(dyq, P['wq'], "nt", name="q_proj_dx")
    G['wq'] = _mm(S['cqn'], dyq, "tn", name="q_proj_dw")
    dakv = _mm(dkv, P['wkv'], "nt", name="kv_proj_dx")
    G['wkv'] = _mm(S['akv'], dkv, "tn", name="kv_proj_dw")
    (du_mla,), (G['mla_q_norm_g'], G['mla_kv_norm_g']) = _rows_vjp(
        _f_mla_prep, [S['u_mla'], ck, sk], [P['mla_q_norm_g'], P['mla_kv_norm_g']], [dcqn, dakv], name="mla_prep_bwd",
        drows=[0], dparams=[0, 1], drow_dtypes=[BF16])
    du = jnp.concatenate([du_mla, (dug_mix + dug_ssd).astype(BF16), du_conv.astype(BF16)], axis=1)
    G['win'] = _mm(S['h1'], du, "tn", name="in_dw")
    dx0, dx0h, G['mix_norm_g'] = _mm(du, P['win'], "nt", name="in_dx", extras=(S['x0'], dx1, P['mix_norm_g']), **norm_out)
    return dx0, dx0h, G, (got[0] if got else None)


class _NoExchange:
    def __init__(self, layers):
        self.layers = layers

    def pieces(self, l):
        return self.layers[l]

    def fwd_send(self, l):
        return None

    def fwd_got(self, l, got):
        pass

    def bwd_send(self, l):
        return None

    def bwd_got(self, l, got):
        pass

    def grads_ready(self, l, pieces):
        pass


def _local_step(x, mem, positions, ex, Ws, tgt):
    tabs = _rope_tables(positions)
    saved, preps, h = [], [], None
    for l in range(DEPTH):
        P = _prep_layer(ex.pieces(l), Ws, l)
        send = ex.fwd_send(l)
        g_next = Ws['mix_norm_g'][l + 1].astype(F32).reshape(1, D) if l + 1 < DEPTH else None
        x, h, S, got = _layer_fwd(x, h, mem, P, tabs, g_next, send)
        if send is not None:
            ex.fwd_got(l, got)
        saved.append(S)
        preps.append(P)
    loss, dx, dxh, dg_final = _loss_head(x, tgt, Ws['final_norm_g'].astype(F32).reshape(1, D), name="loss_head")
    pieces, small = [None] * DEPTH, [None] * DEPTH
    for l in reversed(range(DEPTH)):
        send = ex.bwd_send(l)
        dx, dxh, G, got = _layer_bwd(dx, dxh, mem, saved[l], preps[l], tabs, send)
        if send is not None:
            ex.bwd_got(l, got)
        pieces[l], small[l] = _unprep_pieces(G), _unprep_small(G)
        ex.grads_ready(l, pieces[l])
    grads = {n: jnp.stack([small[l][n] for l in range(DEPTH)]) for n in SMALL if n != 'final_norm_g'}
    grads['final_norm_g'] = dg_final.reshape(D)
    return loss, dx, pieces, grads


def _pack_rows(pieces):
    return jnp.concatenate([pieces[n].reshape(pieces[n].shape[:-2] + (-1, PACK_C)) for n in PIECES], axis=-2)


def _unpack_rows(packed, lead=()):
    out, off = {}, 0
    for n in PIECES:
        rows = _piece_rows(n)
        out[n] = packed[..., off:off + rows, :].reshape(lead + PIECE_SHAPE[n])
        off += rows
    return out


class _StepExchange(_NoExchange):
    def __init__(self, shard):
        self.shard = {n: a.astype(BF16) for n, a in shard.items()}
        self.layers = {}
        self.sums, self.reduced = {}, {}
        self._take(_all_gather(_pack_rows({n: a[0] for n, a in self.shard.items()}), name="gather_w0"), [0])

    def _take(self, gathered, layers):
        per = gathered.reshape(2, 4, len(layers), -1, PACK_C)
        for k, l in enumerate(layers):
            self.layers[l] = {n: _assemble(n, g) for n, g in _unpack_rows(per[:, :, k], lead=(2, 4)).items()}

    def _rest(self):
        return _pack_rows({n: a[1:] for n, a in self.shard.items()}).reshape(-1, PACK_C)

    def fwd_send(self, l):
        return self._rest() if l == 0 else None

    def fwd_got(self, l, got):
        self._take(_finish_gather(self._rest(), got, name="gather_w"), list(range(1, DEPTH)))

    def grads_ready(self, l, pieces):
        x = _pack_rows({n: _disassemble(n, pieces[n]) for n in PIECES})
        self.sums[l] = _chip_sums(x, name="scatter_g")
        if l == 0:
            self.bwd_got(-1, _chip_exchange(self.sums.pop(0), False, name="scatter_g_chips"))

    def bwd_send(self, l):
        return self.sums.pop(l + 1, None)

    def bwd_got(self, l, got):
        self.reduced[l + 1] = _sum_fixed(got, F32, name="scatter_g_sum")


def _adamw_nd(w, g, m, v, name):
    shp = w.shape
    two = lambda a: a.reshape(-1, shp[-1])
    return [r.reshape(shp) for r in _adamw(two(w), two(g), two(m), two(v), name=name)]


def kernel(x, mem, positions, mix_norm_g, w_in, mla_q_norm_g, mla_kv_norm_g, mla_w_uq, mla_w_ukv, mla_out_g, ssm_conv_w, ssm_conv_b, ssm_dt_bias, ssm_a_log, ssm_d, ssm_norm_g, lru_conv_w, lru_conv_b, lru_w_a, lru_b_a, lru_w_i, lru_b_i, lru_lambda, lru_out_g, w_out, xattn_norm_g, mem_norm_g, w_mq, w_mk, w_mv, w_mo, mlp_norm_g, w_mlp1, w_mlp2, final_norm_g, loss_target, m_mix_norm_g, m_w_in, m_mla_q_norm_g, m_mla_kv_norm_g, m_mla_w_uq, m_mla_w_ukv, m_mla_out_g, m_ssm_conv_w, m_ssm_conv_b, m_ssm_dt_bias, m_ssm_a_log, m_ssm_d, m_ssm_norm_g, m_lru_conv_w, m_lru_conv_b, m_lru_w_a, m_lru_b_a, m_lru_w_i, m_lru_b_i, m_lru_lambda, m_lru_out_g, m_w_out, m_xattn_norm_g, m_mem_norm_g, m_w_mq, m_w_mk, m_w_mv, m_w_mo, m_mlp_norm_g, m_w_mlp1, m_w_mlp2, m_final_norm_g, v_mix_norm_g, v_w_in, v_mla_q_norm_g, v_mla_kv_norm_g, v_mla_w_uq, v_mla_w_ukv, v_mla_out_g, v_ssm_conv_w, v_ssm_conv_b, v_ssm_dt_bias, v_ssm_a_log, v_ssm_d, v_ssm_norm_g, v_lru_conv_w, v_lru_conv_b, v_lru_w_a, v_lru_b_a, v_lru_w_i, v_lru_b_i, v_lru_lambda, v_lru_out_g, v_w_out, v_xattn_norm_g, v_mem_norm_g, v_w_mq, v_w_mk, v_w_mv, v_w_mo, v_mlp_norm_g, v_w_mlp1, v_w_mlp2, v_final_norm_g):
    a = locals()
    w = {n: a[n] for n in WEIGHTS}
    m = {n: a['m_' + n] for n in WEIGHTS}
    v = {n: a['v_' + n] for n in WEIGHTS}
    me = 4 * lax.axis_index("x") + 2 * lax.axis_index("y") + lax.axis_index("c")

    ex = _StepExchange({n: _K_FWD[n](w[PIECE_SOURCE[n]]) if n in _K_FWD else w[PIECE_SOURCE.get(n, n)] for n in PIECES})
    Ws = {}
    conv_shapes = [w[n].shape for n in CONV_SHARDED]
    conv_g = _all_gather(_pack([w[n] for n in CONV_SHARDED], F32), name="gather_conv")
    conv_g = conv_g.transpose(1, 0, 2, 3).reshape((N_DEV,) + conv_g.shape[2:])
    for n, g in zip(CONV_SHARDED, _unpack(conv_g, conv_shapes, lead=(N_DEV,))):
        Ws[n] = g.transpose(1, 2, 0, 3).reshape(g.shape[1], g.shape[2], N_DEV * g.shape[3])
    for n in SMALL:
        if n not in CONV_SHARDED:
            Ws[n] = w[n]

    loss_share, dx, _, grads = _local_step(x[0], mem[0], positions[0], ex, Ws, loss_target[0])
    loss = lax.psum(loss_share[0, 0], ("x", "y", "c"))

    g_out = {}
    for n in PIECES:
        g = jnp.stack([_unpack_rows(ex.reduced[l])[n] for l in range(DEPTH)])
        g_out[PIECE_SOURCE.get(n, n)] = _K_INV[n](g) if n in _K_INV else g
    small_shapes = [grads[n].shape for n in SMALL]
    g_small = _all_reduce(_pack([grads[n] for n in SMALL], F32), name="reduce_g")
    for n, g in zip(SMALL, _unpack(g_small, small_shapes)):
        if n in CONV_SHARDED:
            cols = w[n].shape[-1]
            g = lax.dynamic_slice_in_dim(g, me * cols, cols, axis=2)
        g_out[n] = g

    delta, new_m, new_v = {}, {}, {}
    for n in BIG:
        delta[n], new_m[n], new_v[n] = _adamw_nd(w[n], g_out[n], m[n], v[n], "adamw_" + n)
    shapes = [w[n].shape for n in SMALL]
    packed = [_pack([d[n] for n in SMALL], F32) for d in (w, g_out, m, v)]
    for d, res in zip((delta, new_m, new_v), _adamw(*packed, name="adamw_small")):
        d.update(zip(SMALL, _unpack(res, shapes)))

    return (loss, dx[None], *[g_out[n] for n in WEIGHTS], *[delta[n] for n in WEIGHTS],
            *[new_m[n] for n in WEIGHTS], *[new_v[n] for n in WEIGHTS])
```

```python
import functools
import math

import jax
import jax.numpy as jnp
from jax import lax
from jax.experimental import pallas as pl
from jax.experimental.pallas import tpu as pltpu

F32, BF16 = jnp.float32, jnp.bfloat16

D = 1024
DEPTH = 4
N_MEM = 256
EPS = 1e-6
HEADS = 8
NOPE, ROPE, VDIM = 64, 32, 64
Q_RANK, KV_RANK = 256, 128
ROPE_THETA = 10000.0
SSM_CHUNK = 128
LRU_C = 8.0
MEM_HEADS = 4
D_FF = 4 * D
SLAB = 128
LR, B1, B2, AEPS, WD, STEP = 0.001, 0.9, 0.999, 1e-08, 0.01, 10

N_DEV = 8
MESH = pl.DeviceIdType.MESH

U_MLA = 640
U_GATE = 640
U_CONV = 768

_DN = {"nn": (((1,), (0,)), ((), ())), "nt": (((1,), (1,)), ((), ())), "tn": (((0,), (0,)), ((), ()))}


def _dot(a, b, kind):
    return lax.dot_general(a.astype(BF16), b.astype(BF16), _DN[kind], preferred_element_type=F32)


@functools.partial(jax.custom_vjp, nondiff_argnums=(2,))
def _bdot(a, b, kind):
    return _dot(a, b, kind)


def _bdot_fwd(a, b, kind):
    return _dot(a, b, kind), (a, b)


def _bdot_bwd(kind, res, g):
    a, b = res
    if kind == "nn":
        da, db = _dot(g, b, "nt"), _dot(a, g, "tn")
    elif kind == "nt":
        da, db = _dot(g, b, "nn"), _dot(g, a, "tn")
    else:
        da, db = _dot(b, g, "nt"), _dot(a, g, "nn")
    return da.astype(a.dtype), db.astype(b.dtype)


_bdot.defvjp(_bdot_fwd, _bdot_bwd)


def _tile(n, pref):
    if n <= pref:
        return n
    t = pref
    while n % t:
        t -= SLAB
    return t


def _cparams(sem, vmem_mb=48):
    return pltpu.CompilerParams(dimension_semantics=sem, vmem_limit_bytes=vmem_mb * 1024 * 1024)


def _mm(a, b, kind="nn", *, name, out_dtypes=(F32,), epi=None, extras=(), col_sums=0, tm=1024, tn=1024, tk=1024):
    if kind == "tn":
        K, M = a.shape
    else:
        M, K = a.shape
    N = b.shape[0] if kind == "nt" else b.shape[1]
    if a.dtype == F32 or b.dtype == F32:
        tk = tk // 2
    tm, tn, tk = _tile(M, tm), _tile(N, tn), _tile(K, tk)
    assert not col_sums or tn == N, "column sums need one tile across the columns"
    nk = K // tk
    a_spec = pl.BlockSpec((tk, tm), lambda i, j, k: (k, i)) if kind == "tn" else pl.BlockSpec((tm, tk), lambda i, j, k: (i, k))
    b_spec = pl.BlockSpec((tn, tk), lambda i, j, k: (j, k)) if kind == "nt" else pl.BlockSpec((tk, tn), lambda i, j, k: (k, j))
    o_spec = pl.BlockSpec((tm, tn), lambda i, j, k: (i, j))
    vec_spec = pl.BlockSpec((1, tn), lambda i, j, k: (0, j))
    ex_specs = [vec_spec if e.shape[0] == 1 else o_spec for e in extras]
    n_ex, n_out = len(extras), len(out_dtypes)

    def body(*refs):
        a_ref, b_ref = refs[:2]
        ex = refs[2:2 + n_ex]
        outs = refs[2 + n_ex:2 + n_ex + n_out]
        acc = refs[-1]
        k = pl.program_id(2)
        first_row_tile = pl.program_id(0) == 0

        def finish(r):
            res = epi(r, *[e[...] for e in ex]) if epi is not None else (r,)
            for o, v in zip(outs, res):
                o[...] = v.astype(o.dtype)
            sums = refs[2 + n_ex + n_out:2 + n_ex + n_out + col_sums]

            @pl.when(first_row_tile)
            def _():
                for o in sums:
                    o[...] = jnp.zeros_like(o)

            for o, v in zip(sums, res[n_out:]):
                o[...] += v

        if nk == 1:
            finish(_dot(a_ref[...], b_ref[...], kind))
            return

        @pl.when(k == 0)
        def _():
            acc[...] = _dot(a_ref[...], b_ref[...], kind)

        @pl.when(k > 0)
        def _():
            acc[...] += _dot(a_ref[...], b_ref[...], kind)

        @pl.when(k == nk - 1)
        def _():
            finish(acc[...])

    res = pl.pallas_call(
        body,
        name=name,
        grid=(M // tm, N // tn, nk),
        in_specs=[a_spec, b_spec] + ex_specs,
        out_specs=[o_spec] * n_out + [vec_spec] * col_sums,
        out_shape=[jax.ShapeDtypeStruct((M, N), dt) for dt in out_dtypes] + [jax.ShapeDtypeStruct((1, N), F32)] * col_sums,
        scratch_shapes=[pltpu.VMEM((tm, tn), F32)],
        compiler_params=_cparams(("arbitrary" if col_sums else "parallel", "parallel", "arbitrary")),
    )(a, b, *extras)
    return res[0] if n_out + col_sums == 1 else res


_MM_ROWS = 1024


def _row_spec(arr, tm):
    return pl.BlockSpec((tm, arr.shape[1]), lambda i: (i, 0))


def _full_spec(arr):
    nd = arr.ndim
    return pl.BlockSpec(arr.shape, lambda i: (0,) * nd)


def _rows_fwd(fn, rows, params, outs, *, name, tm=256):
    T = rows[0].shape[0]
    tm = min(tm, T)
    nr, npar = len(rows), len(params)

    def body(*refs):
        ins = [r[...] for r in refs[:nr + npar]]
        res = fn(*ins)
        for o, v in zip(refs[nr + npar:], res):
            o[...] = v.astype(o.dtype)

    res = pl.pallas_call(
        body,
        name=name,
        grid=(T // tm,),
        in_specs=[_row_spec(r, tm) for r in rows] + [_full_spec(p) for p in params],
        out_specs=[pl.BlockSpec((tm, c), lambda i: (i, 0)) for c, _ in outs],
        out_shape=[jax.ShapeDtypeStruct((T, c), dt) for c, dt in outs],
        compiler_params=_cparams(("parallel",)),
    )(*rows, *params)
    return res


def _rows_vjp(fn, rows, params, cts, *, name, drows, dparams, drow_dtypes, add=None, twin=False, tm=256):
    T = rows[0].shape[0]
    tm = min(tm, T)
    nr, npar, nct = len(rows), len(params), len(cts)
    n_add = 0 if add is None else 1
    n_dr, n_dp = len(drows), len(dparams)
    n_tw = 1 if twin else 0

    def body(*refs):
        row_t = [r[...] for r in refs[:nr]]
        par_t = [r[...] for r in refs[nr:nr + npar]]
        ct_t = [r[...].astype(F32) for r in refs[nr + npar:nr + npar + nct]]
        pos = nr + npar + nct
        add_t = refs[pos][...] if n_add else None
        pos += n_add
        drow_refs = refs[pos:pos + n_dr]
        dpar_refs = refs[pos + n_dr:pos + n_dr + n_dp]

        def g(*dargs):
            rr, pp = list(row_t), list(par_t)
            for idx, v in zip(drows, dargs[:n_dr]):
                rr[idx] = v
            for idx, v in zip(dparams, dargs[n_dr:]):
                pp[idx] = v
            return tuple(fn(*rr, *pp))

        prim = [row_t[i].astype(F32) for i in drows] + [par_t[i].astype(F32) for i in dparams]
        _, vjp = jax.vjp(g, *prim)
        grads = vjp(tuple(ct_t))
        for n, (o, v) in enumerate(zip(drow_refs, grads[:n_dr])):
            if n == 0 and n_add:
                v = v + add_t.astype(F32)
            o[...] = v.astype(o.dtype)
            if n == 0 and n_tw:
                refs[-1][...] = v.astype(BF16)

        @pl.when(pl.program_id(0) == 0)
        def _():
            for o in dpar_refs:
                o[...] = jnp.zeros_like(o)

        for o, v in zip(dpar_refs, grads[n_dr:]):
            o[...] += v

    res = pl.pallas_call(
        body,
        name=name,
        grid=(T // tm,),
        in_specs=[_row_spec(r, tm) for r in rows] + [_full_spec(p) for p in params] + [_row_spec(c, tm) for c in cts]
        + ([_row_spec(add, tm)] if n_add else []),
        out_specs=[_row_spec(rows[i], tm) for i in drows] + [_full_spec(params[i]) for i in dparams]
        + [_row_spec(rows[drows[0]], tm)] * n_tw,
        out_shape=[jax.ShapeDtypeStruct(rows[i].shape, dt) for i, dt in zip(drows, drow_dtypes)]
        + [jax.ShapeDtypeStruct(params[i].shape, F32) for i in dparams]
        + [jax.ShapeDtypeStruct(rows[drows[0]].shape, BF16)] * n_tw,
        compiler_params=_cparams(("arbitrary",)),
    )(*rows, *params, *cts, *([add] if n_add else []))
    return list(res[:n_dr]) + list(res[n_dr + n_dp:]), list(res[n_dr:n_dr + n_dp])


def _rms(x, g, n):
    return x * lax.rsqrt(jnp.sum(x * x, axis=-1, keepdims=True) * (1.0 / n) + EPS) * g


def _sigmoid(x):
    return 1.0 / (1.0 + jnp.exp(-x))


def _silu(x):
    return x * _sigmoid(x)


def _softplus(x):
    return jnp.maximum(x, 0.0) + jnp.log(1.0 + jnp.exp(-jnp.abs(x)))


def _gelu_tanh(x):
    return 0.5 * x * (1.0 + jnp.tanh(math.sqrt(2.0 / math.pi) * (x + 0.044715 * x * x * x)))


def _lane(shape):
    return lax.broadcasted_iota(jnp.int32, shape, len(shape) - 1)


def _col(x, h):
    return jnp.sum(jnp.where(_lane(x.shape) == h, x, 0.0), axis=-1, keepdims=True)


def _f_norm(x, g):
    return (_rms(x.astype(F32), g, x.shape[-1]),)


def _f_mla_prep(u, ck, sk, gq, gkv):
    u = u.astype(F32)
    cq = _rms(u[:, 0:256], gq, Q_RANK)
    ckv = _rms(u[:, 256:384], gkv, KV_RANK)
    kr = u[:, 384:512] * ck + u[:, 512:640] * sk
    return cq, jnp.concatenate([ckv, kr], axis=1)


def _f_qrope(y, cq, sq):
    y = y.astype(F32)
    c8, s8 = jnp.tile(cq, (1, HEADS)), jnp.tile(sq, (1, HEADS))
    return (y[:, :HEADS * SLAB] * c8 + y[:, HEADS * SLAB:] * s8,)


def _f_lru_gates(xc, wa, ba, wi, bi, lam):
    xc = xc.astype(F32)
    r = _sigmoid(_bdot(xc, wa, "nn") + ba)
    i = _sigmoid(_bdot(xc, wi, "nn") + bi)
    log_a = -LRU_C * r * _softplus(-lam)
    a = jnp.exp(log_a)
    x2 = 2.0 * log_a
    m1 = jnp.where(x2 > -0.02, -x2 * (1.0 + x2 * (0.5 + x2 * (1.0 / 6.0 + x2 * (1.0 / 24.0)))), 1.0 - jnp.exp(x2))
    return a, jnp.sqrt(m1) * (i * xc)


def _f_mix(o, ys, h, ug, g_mla, g_lru):
    o = o.astype(F32)
    y_mla = _rms(o, g_mla, HEADS * VDIM)
    y_lru = _rms(h.astype(F32) * _gelu_tanh(ug[:, 256:512].astype(F32)), g_lru, 256)
    return (jnp.concatenate([y_mla, ys.astype(F32), y_lru], axis=1),)


def _f_xattn(q, k, v):
    hd = D // MEM_HEADS
    outs = []
    for h in range(MEM_HEADS):
        sl = slice(h * hd, (h + 1) * hd)
        s = _bdot(q[:, sl], k[:, sl], "nt") * (1.0 / math.sqrt(hd))
        s = s - jnp.max(s, axis=-1, keepdims=True)
        p = jnp.exp(s)
        p = p / jnp.sum(p, axis=-1, keepdims=True)
        outs.append(_bdot(p, v[:, sl], "nn"))
    return (jnp.concatenate(outs, axis=1),)


def _split_dot(tri, a, kind):
    a_hi = a.astype(BF16)
    r1 = a - a_hi.astype(F32)
    a_mid = r1.astype(BF16)
    a_lo = (r1 - a_mid.astype(F32)).astype(BF16)
    return _dot(tri, a_hi, kind) + _dot(tri, a_mid, kind) + _dot(tri, a_lo, kind)


@jax.custom_vjp
def _tri_cumsum(tri, a):
    return _split_dot(tri, a, "nn")


def _tri_cumsum_fwd(tri, a):
    return _split_dot(tri, a, "nn"), tri


def _tri_cumsum_bwd(tri, g):
    return jnp.zeros_like(tri), _split_dot(tri, g, "tn")


_tri_cumsum.defvjp(_tri_cumsum_fwd, _tri_cumsum_bwd)


def _f_ssd_chunk(c, ug, s0, s1, dtb, alog, dsk, ng):
    L = c.shape[0]
    c = c.astype(F32)
    xbc = _silu(c)
    xs, bm, cm = xbc[:, 0:256], xbc[:, 256:384], xbc[:, 384:512]
    z = ug[:, 0:256].astype(F32)
    dt = _softplus(ug[:, 512:640].astype(F32) + dtb)
    a = dt * (-jnp.exp(alog))
    rowi = lax.broadcasted_iota(jnp.int32, (L, L), 0)
    coli = lax.broadcasted_iota(jnp.int32, (L, L), 1)
    tril = rowi >= coli
    acum = _tri_cumsum(tril.astype(BF16), a)
    acum_t = acum.T
    lane = _lane((1, SLAB))
    lo = lane < 64
    ys, new_s = [], []
    for g in range(2):
        gm = (lane >= 64 * g) & (lane < 64 * g + 64)
        bg, cg = jnp.where(gm, bm, 0.0), jnp.where(gm, cm, 0.0)
        cb = _bdot(cg, bg, "nt")
        x = xs[:, SLAB * g:SLAB * (g + 1)]
        h0, h1 = 2 * g, 2 * g + 1
        ac0, ac1 = _col(acum, h0), _col(acum, h1)
        xdt = x * jnp.where(lo, _col(dt, h0), _col(dt, h1))
        ac_l = jnp.where(lo, ac0, ac1)
        tot = acum[L - 1:L, :]
        tot_l = jnp.where(lo, _col(tot, h0), _col(tot, h1))
        yd = jnp.zeros((L, SLAB), F32)
        for hh, acc, hm in ((h0, ac0, lo), (h1, ac1, jnp.logical_not(lo))):
            seg = acc - acum_t[hh:hh + 1, :]
            lm = jnp.where(tril, jnp.exp(jnp.where(tril, seg, 0.0)), 0.0)
            yd = yd + _bdot(cb * lm, jnp.where(hm, xdt, 0.0), "nn")
        sg = (s0, s1)[g]
        y_off = _bdot(cg, sg, "nn") * jnp.exp(ac_l)
        st = _bdot(bg, xdt * jnp.exp(tot_l - ac_l), "tn")
        new_s.append(jnp.exp(tot_l) * sg + st)
        y = yd + y_off + jnp.where(lo, _col(dsk, h0), _col(dsk, h1)) * x
        y = y * _silu(z[:, SLAB * g:SLAB * (g + 1)])
        ys.append(_rms(y, ng[:, SLAB * g:SLAB * (g + 1)], SLAB))
    return jnp.concatenate(ys, axis=1), new_s[0], new_s[1]


_SSD_TILE = 512


def _ssd_fwd(c, ug, dtb, alog, dsk, ng, *, name):
    T = c.shape[0]
    tm = min(_SSD_TILE, T)
    ncs = tm // SSM_CHUNK
    nc = T // SSM_CHUNK

    def body(c_ref, ug_ref, dtb_ref, alog_ref, dsk_ref, ng_ref, y_ref, sall_ref, s_scr):
        @pl.when(pl.program_id(0) == 0)
        def _():
            s_scr[...] = jnp.zeros_like(s_scr)

        s0, s1 = s_scr[0], s_scr[1]
        for k in range(ncs):
            rows = slice(k * SSM_CHUNK, (k + 1) * SSM_CHUNK)
            sall_ref[k, 0] = s0
            sall_ref[k, 1] = s1
            y, s0, s1 = _f_ssd_chunk(c_ref[rows, :], ug_ref[rows, :], s0, s1, dtb_ref[...], alog_ref[...],
                                     dsk_ref[...], ng_ref[...])
            y_ref[rows, :] = y
        s_scr[0] = s0
        s_scr[1] = s1

    y, sall = pl.pallas_call(
        body,
        name=name,
        grid=(T // tm,),
        in_specs=[_row_spec(c, tm), _row_spec(ug, tm)] + [_full_spec(p) for p in (dtb, alog, dsk, ng)],
        out_specs=[pl.BlockSpec((tm, 256), lambda i: (i, 0)), pl.BlockSpec((ncs, 2, SLAB, SLAB), lambda i: (i, 0, 0, 0))],
        out_shape=[jax.ShapeDtypeStruct((T, 256), F32), jax.ShapeDtypeStruct((nc, 2, SLAB, SLAB), F32)],
        scratch_shapes=[pltpu.VMEM((2, SLAB, SLAB), F32)],
        compiler_params=_cparams(("arbitrary",)),
    )(c, ug, dtb, alog, dsk, ng)
    return y, sall


def _ssd_bwd(c, ug, sall, dy, dtb, alog, dsk, ng, *, name):
    T = c.shape[0]
    tm = min(_SSD_TILE, T)
    ncs = tm // SSM_CHUNK
    nt = T // tm

    def body(c_ref, ug_ref, sall_ref, dy_ref, dtb_ref, alog_ref, dsk_ref, ng_ref,
             dc_ref, dug_ref, ddtb_ref, dalog_ref, ddsk_ref, dng_ref, ds_scr):
        @pl.when(pl.program_id(0) == 0)
        def _():
            ds_scr[...] = jnp.zeros_like(ds_scr)
            for o in (ddtb_ref, dalog_ref, ddsk_ref, dng_ref):
                o[...] = jnp.zeros_like(o)

        ds0, ds1 = ds_scr[0], ds_scr[1]
        for k in reversed(range(ncs)):
            rows = slice(k * SSM_CHUNK, (k + 1) * SSM_CHUNK)
            prim = (c_ref[rows, :].astype(F32), ug_ref[rows, :].astype(F32), sall_ref[k, 0], sall_ref[k, 1],
                    dtb_ref[...], alog_ref[...], dsk_ref[...], ng_ref[...])
            _, vjp = jax.vjp(_f_ssd_chunk, *prim)
            dc, dug, ds0, ds1, g_dtb, g_alog, g_dsk, g_ng = vjp((dy_ref[rows, :].astype(F32), ds0, ds1))
            dc_ref[rows, :] = dc
            dug_ref[rows, :] = dug
            ddtb_ref[...] += g_dtb
            dalog_ref[...] += g_alog
            ddsk_ref[...] += g_dsk
            dng_ref[...] += g_ng
        ds_scr[0] = ds0
        ds_scr[1] = ds1

    rev = lambda i: (nt - 1 - i, 0)
    params = (dtb, alog, dsk, ng)
    res = pl.pallas_call(
        body,
        name=name,
        grid=(nt,),
        in_specs=[pl.BlockSpec((tm, c.shape[1]), rev), pl.BlockSpec((tm, ug.shape[1]), rev),
                  pl.BlockSpec((ncs, 2, SLAB, SLAB), lambda i: (nt - 1 - i, 0, 0, 0)), pl.BlockSpec((tm, 256), rev)]
        + [_full_spec(p) for p in params],
        out_specs=[pl.BlockSpec((tm, 512), rev), pl.BlockSpec((tm, U_GATE), rev)] + [_full_spec(p) for p in params],
        out_shape=[jax.ShapeDtypeStruct((T, 512), F32), jax.ShapeDtypeStruct((T, U_GATE), F32)]
        + [jax.ShapeDtypeStruct(p.shape, F32) for p in params],
        scratch_shapes=[pltpu.VMEM((2, SLAB, SLAB), F32)],
        compiler_params=_cparams(("arbitrary",)),
    )(c, ug, sall, dy, *params)
    return res


_CONV_TILE = 512
_HALO = 8
_CONV_W = 4


def _conv_fwd(u, w, b, *, name):
    T, C = u.shape
    tm = min(_CONV_TILE, T)
    hb = tm // _HALO

    def body(u_ref, prev_ref, w_ref, b_ref, y1_ref, y2_ref, ext):
        i = pl.program_id(0)
        ext[0:_HALO, :] = jnp.where(i > 0, prev_ref[...], 0.0)
        ext[_HALO:, :] = u_ref[...]
        y = jnp.broadcast_to(b_ref[...], (tm, C))
        for k in range(_CONV_W):
            y = y + ext[_HALO - (_CONV_W - 1) + k:_HALO - (_CONV_W - 1) + k + tm, :] * w_ref[k:k + 1, :]
        y1_ref[...] = y[:, 0:512]
        y2_ref[...] = y[:, 512:768]

    return pl.pallas_call(
        body,
        name=name,
        grid=(T // tm,),
        in_specs=[_row_spec(u, tm), pl.BlockSpec((_HALO, C), lambda i: (jnp.maximum(i * hb - 1, 0), 0)),
                  _full_spec(w), _full_spec(b)],
        out_specs=[pl.BlockSpec((tm, 512), lambda i: (i, 0)), pl.BlockSpec((tm, 256), lambda i: (i, 0))],
        out_shape=[jax.ShapeDtypeStruct((T, 512), F32), jax.ShapeDtypeStruct((T, 256), F32)],
        scratch_shapes=[pltpu.VMEM((tm + _HALO, C), F32)],
        compiler_params=_cparams(("parallel",)),
    )(u, u, w, b)


def _conv_bwd(u, dy1, dy2, w, *, name):
    T, C = u.shape
    tm = min(_CONV_TILE, T)
    hb = tm // _HALO
    nt = T // tm

    def body(u_ref, prev_ref, dy1_ref, next1_ref, dy2_ref, next2_ref, w_ref, du_ref, dw_ref, db_ref, ext, dext):
        i = pl.program_id(0)
        ext[0:_HALO, :] = jnp.where(i > 0, prev_ref[...], 0.0)
        ext[_HALO:, :] = u_ref[...]
        dext[0:tm, 0:512] = dy1_ref[...]
        dext[0:tm, 512:768] = dy2_ref[...]
        dext[tm:, 0:512] = jnp.where(i < nt - 1, next1_ref[...], 0.0)
        dext[tm:, 512:768] = jnp.where(i < nt - 1, next2_ref[...], 0.0)

        @pl.when(i == 0)
        def _():
            dw_ref[...] = jnp.zeros_like(dw_ref)
            db_ref[...] = jnp.zeros_like(db_ref)

        dy = dext[0:tm, :]
        du = jnp.zeros((tm, C), F32)
        for k in range(_CONV_W):
            du = du + dext[_CONV_W - 1 - k:_CONV_W - 1 - k + tm, :] * w_ref[k:k + 1, :]
            xk = ext[_HALO - (_CONV_W - 1) + k:_HALO - (_CONV_W - 1) + k + tm, :]
            dw_ref[k:k + 1, :] += jnp.sum(dy * xk, axis=0, keepdims=True)
        du_ref[...] = du
        db_ref[...] += jnp.sum(dy, axis=0, keepdims=True)

    nxt = lambda i: (jnp.minimum((i + 1) * hb, T // _HALO - 1), 0)
    return pl.pallas_call(
        body,
        name=name,
        grid=(nt,),
        in_specs=[_row_spec(u, tm), pl.BlockSpec((_HALO, C), lambda i: (jnp.maximum(i * hb - 1, 0), 0)),
                  _row_spec(dy1, tm), pl.BlockSpec((_HALO, 512), nxt), _row_spec(dy2, tm), pl.BlockSpec((_HALO, 256), nxt),
                  _full_spec(w)],
        out_specs=[pl.BlockSpec((tm, C), lambda i: (i, 0)), _full_spec(w), pl.BlockSpec((1, C), lambda i: (0, 0))],
        out_shape=[jax.ShapeDtypeStruct((T, C), F32), jax.ShapeDtypeStruct(w.shape, F32), jax.ShapeDtypeStruct((1, C), F32)],
        scratch_shapes=[pltpu.VMEM((tm + _HALO, C), F32), pltpu.VMEM((tm + _HALO, C), F32)],
        compiler_params=_cparams(("arbitrary",)),
    )(u, u, dy1, dy1, dy2, dy2, w)


_SCAN_TILE = 1024
_SUB = 8


def _shift_rows(x, d, fill, up):
    r = lax.broadcasted_iota(jnp.int32, x.shape, 0)
    if up:
        return jnp.where(r < _SUB - d, pltpu.roll(x, _SUB - d, 0), fill)
    return jnp.where(r >= d, pltpu.roll(x, d, 0), fill)


def _lru_scan_fwd(a, b, *, name):
    T, W = a.shape
    tr = min(_SCAN_TILE, T)

    def body(a_ref, b_ref, h_ref, hp_ref, carry):
        @pl.when(pl.program_id(0) == 0)
        def _():
            carry[...] = jnp.zeros_like(carry)

        def step(t, cr):
            rows = pl.ds(pl.multiple_of(t * _SUB, _SUB), _SUB)
            aa, bb = a_ref[rows, :], b_ref[rows, :]
            for d in (1, 2, 4):
                bb = bb + aa * _shift_rows(bb, d, 0.0, False)
                aa = aa * _shift_rows(aa, d, 1.0, False)
            h = bb + aa * cr
            h_ref[rows, :] = h
            r = lax.broadcasted_iota(jnp.int32, h.shape, 0)
            hp_ref[rows, :] = jnp.where(r >= 1, pltpu.roll(h, 1, 0), cr)
            return jnp.broadcast_to(h[_SUB - 1:_SUB, :], (_SUB, W))

        carry[...] = lax.fori_loop(0, tr // _SUB, step, carry[...])

    return pl.pallas_call(
        body,
        name=name,
        grid=(T // tr,),
        in_specs=[_row_spec(a, tr), _row_spec(b, tr)],
        out_specs=[pl.BlockSpec((tr, W), lambda i: (i, 0))] * 2,
        out_shape=[jax.ShapeDtypeStruct((T, W), F32)] * 2,
        scratch_shapes=[pltpu.VMEM((_SUB, W), F32)],
        compiler_params=_cparams(("arbitrary",)),
    )(a, b)


def _lru_scan_bwd(a, dh, hprev, *, name):
    T, W = a.shape
    tr = min(_SCAN_TILE, T)
    nt = T // tr

    def body(a_ref, dh_ref, hp_ref, g_ref, da_ref, carry):
        @pl.when(pl.program_id(0) == 0)
        def _():
            carry[...] = jnp.zeros_like(carry)

        nsub = tr // _SUB

        def step(s, cr):
            t = nsub - 1 - s
            rows = pl.ds(pl.multiple_of(t * _SUB, _SUB), _SUB)
            a_t = a_ref[rows, :]
            aa = _shift_rows(a_t, 1, 1.0, True)
            bb = dh_ref[rows, :]
            for d in (1, 2, 4):
                bb = bb + aa * _shift_rows(bb, d, 0.0, True)
                aa = aa * _shift_rows(aa, d, 1.0, True)
            g = bb + aa * cr
            g_ref[rows, :] = g
            da_ref[rows, :] = g * hp_ref[rows, :]
            return jnp.broadcast_to(a_t[0:1, :] * g[0:1, :], (_SUB, W))

        carry[...] = lax.fori_loop(0, nsub, step, carry[...])

    rev = lambda i: (nt - 1 - i, 0)
    return pl.pallas_call(
        body,
        name=name,
        grid=(nt,),
        in_specs=[pl.BlockSpec((tr, W), rev)] * 3,
        out_specs=[pl.BlockSpec((tr, W), rev)] * 2,
        out_shape=[jax.ShapeDtypeStruct((T, W), F32)] * 2,
        scratch_shapes=[pltpu.VMEM((_SUB, W), F32)],
        compiler_params=_cparams(("arbitrary",)),
    )(a, dh, hprev)


_ATT_BLK = 512
_ATT_QPARTS = 2
_ATT_BWD_TRIP = 4
_ATT_SCALE = 1.0 / math.sqrt(NOPE + ROPE)
_ATT_SCALE2 = _ATT_SCALE * math.log2(math.e)


def _call_with_exchange(body, send, gather, *, name, grid, in_specs, out_specs, out_shape, args):
    if send is None:
        return pl.pallas_call(body, name=name, grid=grid, in_specs=in_specs, out_specs=out_specs, out_shape=out_shape,
                              compiler_params=_cparams(("parallel", "arbitrary")))(*args)
    n_in, n_out = len(in_specs), len(out_specs)

    def riding(*refs):
        comm = (refs[n_in], refs[n_in + 1 + n_out]) + tuple(refs[n_in + 2 + n_out:])
        h, i = pl.program_id(0), pl.program_id(1)

        @pl.when((h == 0) & (i == 0))
        def _():
            _chip_start(*comm, gather=gather)

        body(*refs[:n_in], *refs[n_in + 1:n_in + 1 + n_out])

        @pl.when((h == grid[0] - 1) & (i == grid[1] - 1))
        def _():
            _chip_wait(*comm, gather=gather)

    shape = (4,) + send.shape if gather else send.shape
    return pl.pallas_call(
        riding, name=name, grid=grid, in_specs=in_specs + [_ANY], out_specs=out_specs + [_ANY],
        out_shape=out_shape + [jax.ShapeDtypeStruct(shape, send.dtype)], scratch_shapes=_CHIP_SEMS,
        compiler_params=_cparams(("arbitrary", "arbitrary")))(*args, send)


def _attn_fwd(q, kv, *, name, send=None):
    T = q.shape[0]
    blk = min(_ATT_BLK, T)
    nq = T // blk
    parts = _ATT_QPARTS if nq % _ATT_QPARTS == 0 else 1

    def body(q_ref, kv_ref, o_ref, lse_ref):
        i = pl.program_id(1)
        def one(part, j0, nblk, carry, masked):
            m, l, acc = carry
            rows = pl.ds(pl.multiple_of(j0 * blk, blk), nblk * blk)
            s = _dot(q_ref[part * blk:(part + 1) * blk, :], kv_ref[rows, 0:SLAB], "nt")
            if masked:
                col = lax.broadcasted_iota(jnp.int32, s.shape, 1)
                row = lax.broadcasted_iota(jnp.int32, s.shape, 0)
                s = jnp.where(col <= row + (nblk - 1) * blk, s, -jnp.inf)
            m_new = jnp.maximum(m, jnp.max(s, axis=-1, keepdims=True))
            alpha = jnp.exp2(m - m_new)
            p = jnp.exp2(s - m_new)
            l = alpha * l + jnp.sum(p, axis=-1, keepdims=True)
            acc = alpha * acc + _dot(p, kv_ref[rows, SLAB:2 * SLAB], "nn")
            return m_new, l, acc

        def step(t, carry):
            return tuple(one(part, t * parts, parts, carry[part], False) for part in range(parts))

        init = ((jnp.full((blk, 1), -jnp.inf, F32), jnp.zeros((blk, 1), F32), jnp.zeros((blk, SLAB), F32)),) * parts
        carry = lax.fori_loop(0, i, step, init)
        for part in range(parts):
            m, l, acc = one(part, i * parts, part + 1, carry[part], True)
            o_ref[part * blk:(part + 1) * blk, :] = acc / l
            lse_ref[part] = jnp.broadcast_to(m + jnp.log(l) * math.log2(math.e), (blk, SLAB)).T[0:_SUB, :]

    return _call_with_exchange(
        body, send, True,
        name=name,
        grid=(HEADS, nq // parts),
        in_specs=[pl.BlockSpec((parts * blk, SLAB), lambda h, i: (i, h)), pl.BlockSpec((T, 2 * SLAB), lambda h, i: (0, h))],
        out_specs=[pl.BlockSpec((parts * blk, SLAB), lambda h, i: (i, h)),
                   pl.BlockSpec((None, parts, _SUB, blk), lambda h, i: (h, i, 0, 0))],
        out_shape=[jax.ShapeDtypeStruct((T, HEADS * SLAB), F32), jax.ShapeDtypeStruct((HEADS, nq, _SUB, blk), F32)],
        args=(q, kv))


def _attn_delta(do, o, *, name):
    T = o.shape[0]
    blk = min(_ATT_BLK, T)
    nq = T // blk

    def body(do_ref, o_ref, d_ref):
        for h in range(HEADS):
            cols = slice(h * SLAB, (h + 1) * SLAB)
            dl = jnp.sum(do_ref[:, cols].astype(F32) * o_ref[:, cols], axis=-1, keepdims=True)
            d_ref[h] = jnp.broadcast_to(dl, (blk, SLAB)).T[0:_SUB, :]

    return pl.pallas_call(
        body,
        name=name,
        grid=(nq,),
        in_specs=[pl.BlockSpec((blk, HEADS * SLAB), lambda i: (i, 0))] * 2,
        out_specs=pl.BlockSpec((HEADS, None, _SUB, blk), lambda i: (0, i, 0, 0)),
        out_shape=jax.ShapeDtypeStruct((HEADS, nq, _SUB, blk), F32),
        compiler_params=_cparams(("parallel",)),
    )(do, o)


def _attn_bwd(q, kv, do, lse, delta, *, name, send=None):
    T = q.shape[0]
    blk = min(_ATT_BLK, T)
    nq = T // blk

    def body(q_ref, kv_ref, do_ref, lse_ref, dl_ref, dqt_ref, dkv_ref):
        j = pl.program_id(1)

        @pl.when(j == 0)
        def _():
            dqt_ref[...] = jnp.zeros_like(dqt_ref)

        kb, vb = kv_ref[:, 0:SLAB], kv_ref[:, SLAB:2 * SLAB]
        kbt = kb.astype(F32).T.astype(BF16)
        kpos = j * blk + lax.broadcasted_iota(jnp.int32, (blk, blk), 0)

        def step(i, carry, masked):
            dk, dv = carry
            rows = pl.ds(pl.multiple_of(i * blk, blk), blk)
            qb, dob = q_ref[rows, :], do_ref[rows, :]
            st = _dot(kb, qb, "nt")
            if masked:
                qpos = i * blk + lax.broadcasted_iota(jnp.int32, (blk, blk), 1)
                st = jnp.where(kpos <= qpos, st, -jnp.inf)
            pt = jnp.exp2(st - (lse_ref[i, 0:1, :] - math.log2(math.log(2.0))))
            dpt = _dot(vb, dob, "nt")
            dst = pt * (dpt - dl_ref[i, 0:1, :])
            dv = dv + _dot(pt, dob, "nn")
            dk = dk + _dot(dst, qb, "nn")
            dqt_ref[i] += _dot(kbt, dst, "nn")
            return dk, dv

        zero = jnp.zeros((blk, SLAB), F32)
        carry = step(j, (zero, zero), True)
        rest = nq - 1 - j

        def trip(t, c):
            for u in range(_ATT_BWD_TRIP):
                c = step(j + 1 + _ATT_BWD_TRIP * t + u, c, False)
            return c

        carry = lax.fori_loop(0, rest // _ATT_BWD_TRIP, trip, carry)
        tail = j + 1 + (rest // _ATT_BWD_TRIP) * _ATT_BWD_TRIP
        dk, dv = lax.fori_loop(tail, nq, functools.partial(step, masked=False), carry)
        dkv_ref[:, 0:SLAB] = dk.astype(BF16)
        dkv_ref[:, SLAB:2 * SLAB] = (dv * (1.0 / math.log(2.0))).astype(BF16)

    stat_spec = pl.BlockSpec((None, nq, _SUB, blk), lambda h, j: (h, 0, 0, 0))
    head_t_spec = pl.BlockSpec((None, nq, SLAB, blk), lambda h, j: (h, 0, 0, 0))
    return _call_with_exchange(
        body, send, False,
        name=name,
        grid=(HEADS, nq),
        in_specs=[pl.BlockSpec((T, SLAB), lambda h, j: (0, h)), pl.BlockSpec((blk, 2 * SLAB), lambda h, j: (j, h)),
                  pl.BlockSpec((T, SLAB), lambda h, j: (0, h)), stat_spec, stat_spec],
        out_specs=[head_t_spec, pl.BlockSpec((blk, 2 * SLAB), lambda h, j: (j, h))],
        out_shape=[jax.ShapeDtypeStruct((HEADS, nq, SLAB, blk), F32), jax.ShapeDtypeStruct((T, HEADS * 2 * SLAB), BF16)],
        args=(q, kv, do, lse, delta))


def _qrope_bwd(dqt, cq, sq, *, name):
    _, nq, _, blk = dqt.shape
    T = nq * blk

    def body(dqt_ref, c_ref, s_ref, dy_ref):
        c, s = c_ref[...], s_ref[...]
        for h in range(HEADS):
            dq = dqt_ref[h].T
            dy_ref[:, h * SLAB:(h + 1) * SLAB] = (dq * c).astype(BF16)
            dy_ref[:, (HEADS + h) * SLAB:(HEADS + h + 1) * SLAB] = (dq * s).astype(BF16)

    return pl.pallas_call(
        body,
        name=name,
        grid=(nq,),
        in_specs=[pl.BlockSpec((HEADS, None, SLAB, blk), lambda i: (0, i, 0, 0)), _row_spec(cq, blk), _row_spec(sq, blk)],
        out_specs=pl.BlockSpec((blk, 2 * HEADS * SLAB), lambda i: (i, 0)),
        out_shape=jax.ShapeDtypeStruct((T, 2 * HEADS * SLAB), BF16),
        compiler_params=_cparams(("parallel",)),
    )(dqt, cq, sq)


def _loss_head(x, tgt, g, *, name, tm=256):
    T = x.shape[0]
    tm = min(tm, T)

    def body(x_ref, t_ref, g_ref, loss_ref, dx_ref, dxh_ref, dg_ref):
        def f(xv, gv):
            e = _rms(xv, gv, D) - t_ref[...]
            row = jnp.sum(e * e, axis=1, keepdims=True)
            return jnp.sum(row, axis=0, keepdims=True) * (0.5 / D)

        val, vjp = jax.vjp(f, x_ref[...], g_ref[...])
        dxv, dgv = vjp(jnp.ones((1, 1), F32))

        @pl.when(pl.program_id(0) == 0)
        def _():
            loss_ref[...] = jnp.zeros_like(loss_ref)
            dg_ref[...] = jnp.zeros_like(dg_ref)

        dx_ref[...] = dxv
        dxh_ref[...] = dxv.astype(BF16)
        dg_ref[...] += dgv
        loss_ref[...] += jnp.broadcast_to(val, loss_ref.shape)

    return pl.pallas_call(
        body,
        name=name,
        grid=(T // tm,),
        in_specs=[_row_spec(x, tm), _row_spec(tgt, tm), _full_spec(g)],
        out_specs=[pl.BlockSpec((1, SLAB), lambda i: (0, 0)), _row_spec(x, tm), _row_spec(x, tm), _full_spec(g)],
        out_shape=[jax.ShapeDtypeStruct((1, SLAB), F32), jax.ShapeDtypeStruct(x.shape, F32),
                   jax.ShapeDtypeStruct(x.shape, BF16), jax.ShapeDtypeStruct(g.shape, F32)],
        compiler_params=_cparams(("arbitrary",)),
    )(x, tgt, g)


def _row_tile(rows, cols, budget=256 * 1024):
    best = None
    for t in range(16, rows + 1, 16):
        if rows % t == 0 and t * cols <= budget:
            best = t
    return best or rows


def _sum_fixed(x, out_dtype, *, name):
    n, R, C = x.shape
    tr = _row_tile(R, C)

    def body(x_ref, o_ref):
        acc = x_ref[0].astype(F32)
        for k in range(1, n):
            acc = acc + x_ref[k].astype(F32)
        o_ref[...] = acc.astype(o_ref.dtype)

    return pl.pallas_call(
        body,
        name=name,
        grid=(R // tr,),
        in_specs=[pl.BlockSpec((n, tr, C), lambda i: (0, i, 0))],
        out_specs=pl.BlockSpec((tr, C), lambda i: (i, 0)),
        out_shape=jax.ShapeDtypeStruct((R, C), out_dtype),
        compiler_params=_cparams(("parallel",)),
    )(x)


def _adamw(w, g, m, v, *, name):
    R, C = w.shape
    tr = _row_tile(R, C, 128 * 1024)

    def body(w_ref, g_ref, m_ref, v_ref, d_ref, nm_ref, nv_ref):
        gv = g_ref[...]
        mv = B1 * m_ref[...] + (1.0 - B1) * gv
        vv = B2 * v_ref[...] + (1.0 - B2) * (gv * gv)
        m_hat = mv / (1.0 - B1 ** STEP)
        v_hat = vv / (1.0 - B2 ** STEP)
        d_ref[...] = -LR * (m_hat / (jnp.sqrt(v_hat) + AEPS) + WD * w_ref[...])
        nm_ref[...] = mv
        nv_ref[...] = vv

    spec = pl.BlockSpec((tr, C), lambda i: (i, 0))
    return pl.pallas_call(
        body, name=name, grid=(R // tr,), in_specs=[spec] * 4, out_specs=[spec] * 3,
        out_shape=[jax.ShapeDtypeStruct((R, C), F32)] * 3, compiler_params=_cparams(("parallel",)),
    )(w, g, m, v)


_FLIPS = ((1, 0), (0, 1), (1, 1))
_ANY = pl.BlockSpec(memory_space=pl.ANY)


def _me():
    return lax.axis_index("x"), lax.axis_index("y"), lax.axis_index("c")


def _flip(mx, my, f):
    return (1 - mx if f[0] else mx), (1 - my if f[1] else my)


_CHIP_SEMS = [pltpu.SemaphoreType.DMA((3,)), pltpu.SemaphoreType.DMA((3,)), pltpu.SemaphoreType.DMA]


def _chip_copies(x_ref, out_ref, send_sems, recv_sems, local_sem, gather):
    mx, my, mc = _me()
    mine = 2 * mx + my
    outgoing, incoming = [], []
    for k, f in enumerate(_FLIPS):
        px, py = _flip(mx, my, f)
        peer = 2 * px + py
        src = x_ref if gather else x_ref.at[peer]
        for dst, to in ((out_ref.at[mine], outgoing), (out_ref.at[peer], incoming)):
            to.append(pltpu.make_async_remote_copy(src_ref=src, dst_ref=dst, send_sem=send_sems.at[k],
                                                   recv_sem=recv_sems.at[k], device_id=(px, py, mc), device_id_type=MESH))
    local = None if gather else pltpu.make_async_copy(x_ref.at[mine], out_ref.at[mine], local_sem)
    return outgoing, incoming, local


def _chip_start(*refs, gather):
    outgoing, _, local = _chip_copies(*refs, gather)
    if local is not None:
        local.start()
    for cp in outgoing:
        cp.start()


def _chip_wait(*refs, gather):
    outgoing, incoming, local = _chip_copies(*refs, gather)
    for cp in incoming:
        cp.wait_recv()
    for cp in outgoing:
        cp.wait_send()
    if local is not None:
        local.wait()


def _chip_exchange(x, gather, *, name):
    shape = x.shape if not gather else (4,) + x.shape

    def body(*refs):
        _chip_start(*refs, gather=gather)
        _chip_wait(*refs, gather=gather)

    return pl.pallas_call(
        body, name=name, in_specs=[_ANY], out_specs=_ANY, out_shape=jax.ShapeDtypeStruct(shape, x.dtype),
        scratch_shapes=_CHIP_SEMS,
    )(x)


def _core_exchange(x, half, *, name):
    shape = x.shape[1:] if half else x.shape

    def body(x_ref, out_ref, send_sem, recv_sem):
        mx, my, mc = _me()
        src = x_ref.at[1 - mc] if half else x_ref
        cp = pltpu.make_async_remote_copy(src_ref=src, dst_ref=out_ref, send_sem=send_sem, recv_sem=recv_sem,
                                          device_id=(mx, my, 1 - mc), device_id_type=MESH)
        cp.start()
        cp.wait()

    return pl.pallas_call(
        body, name=name, in_specs=[_ANY], out_specs=_ANY, out_shape=jax.ShapeDtypeStruct(shape, x.dtype),
        scratch_shapes=[pltpu.SemaphoreType.DMA, pltpu.SemaphoreType.DMA],
    )(x)


def _all_gather(x, *, name):
    return _finish_gather(x, _chip_exchange(x, True, name=name + "_chips"), name=name)


def _finish_gather(x, g4, *, name):
    mx, my, mc = _me()
    own = (jnp.arange(4) == 2 * mx + my).reshape((4,) + (1,) * x.ndim)
    g4 = jnp.where(own, x[None], g4)
    sib = _core_exchange(g4, False, name=name + "_cores")
    return jnp.where(mc == 0, jnp.stack([g4, sib]), jnp.stack([sib, g4]))


def _add_own_half(x, got, *, name):
    _, R, C = x.shape
    tr = _row_tile(R, C)

    def body(c_ref, x_ref, g_ref, o_ref):
        o_ref[...] = (x_ref[...] + g_ref[...]).astype(o_ref.dtype)

    return pl.pallas_call(
        body,
        name=name,
        grid_spec=pltpu.PrefetchScalarGridSpec(
            num_scalar_prefetch=1, grid=(R // tr,),
            in_specs=[pl.BlockSpec((None, tr, C), lambda i, c: (c[0], i, 0)), pl.BlockSpec((tr, C), lambda i, c: (i, 0))],
            out_specs=pl.BlockSpec((tr, C), lambda i, c: (i, 0))),
        out_shape=jax.ShapeDtypeStruct((R, C), BF16),
        compiler_params=_cparams(("parallel",)),
    )(lax.axis_index("c").astype(jnp.int32).reshape(1), x, got)


def _chip_sums(x, *, name):
    _, _, R, C = x.shape
    got = _core_exchange(x, True, name=name + "_cores")
    return _add_own_half(x.reshape(2, 4 * R, C), got.reshape(4 * R, C), name=name + "_add").reshape(4, R, C)


def _all_reduce(x, *, name):
    g = _all_gather(x, name=name)
    return _sum_fixed(g.reshape((N_DEV,) + x.shape), F32, name=name + "_sum")


WEIGHTS = ['mix_norm_g', 'w_in', 'mla_q_norm_g', 'mla_kv_norm_g', 'mla_w_uq', 'mla_w_ukv', 'mla_out_g', 'ssm_conv_w',
           'ssm_conv_b', 'ssm_dt_bias', 'ssm_a_log', 'ssm_d', 'ssm_norm_g', 'lru_conv_w', 'lru_conv_b', 'lru_w_a',
           'lru_b_a', 'lru_w_i', 'lru_b_i', 'lru_lambda', 'lru_out_g', 'w_out', 'xattn_norm_g', 'mem_norm_g', 'w_mq',
           'w_mk', 'w_mv', 'w_mo', 'mlp_norm_g', 'w_mlp1', 'w_mlp2', 'final_norm_g']
ROW_SHARDED = ('w_in', 'w_out', 'w_mq', 'w_mk', 'w_mv', 'w_mo', 'w_mlp2')
COL_SHARDED = ('mla_w_uq', 'mla_w_ukv', 'w_mlp1')
BIG = tuple(n for n in WEIGHTS if n in ROW_SHARDED + COL_SHARDED)
CONV_SHARDED = ('ssm_conv_w', 'lru_conv_w')
SMALL = tuple(n for n in WEIGHTS if n not in BIG)
PACK_C = 1024


def _pack(arrs, dtype, lead=()):
    flat = jnp.concatenate([a.reshape(lead + (-1,)).astype(dtype) for a in arrs], axis=-1)
    n = flat.shape[-1]
    rows = -(-n // (16 * PACK_C)) * 16
    flat = jnp.pad(flat, [(0, 0)] * len(lead) + [(0, rows * PACK_C - n)])
    return flat.reshape(lead + (rows, PACK_C))


def _unpack(packed, shapes, lead=()):
    flat = packed.reshape(lead + (-1,))
    out, off = [], 0
    for s in shapes:
        n = math.prod(s)
        out.append(flat[..., off:off + n].reshape(lead + tuple(s)))
        off += n
    return out


def _pad_lanes(v, n=SLAB):
    return jnp.pad(v.astype(F32), (0, n - v.shape[0])).reshape(1, n)


PIECES = ('win', 'wq', 'wkv', 'wout', 'w_mq', 'w_mk', 'w_mv', 'w_mo', 'w_mlp1', 'w_mlp2')
PIECE_SOURCE = {'win': 'w_in', 'wq': 'mla_w_uq', 'wkv': 'mla_w_ukv', 'wout': 'w_out'}
PIECE_SHAPE = {'win': (128, 2048), 'wq': (Q_RANK, 2 * SLAB), 'wkv': (KV_RANK, 2 * SLAB), 'wout': (128, D),
               'w_mq': (128, D), 'w_mk': (128, D), 'w_mv': (128, D), 'w_mo': (128, D), 'w_mlp1': (512, D),
               'w_mlp2': (512, D)}
PIECE_COLS = ('wq', 'wkv')


def _k_win(w):
    kr = w[..., 384:416]
    zc = lambda k: jnp.zeros(w.shape[:-1] + (k,), w.dtype)
    return jnp.concatenate(
        [w[..., 0:384], kr, zc(96), kr[..., 16:32], kr[..., 0:16], zc(96),
         w[..., 416:672], w[..., 1444:1700], w[..., 1184:1188], zc(124),
         w[..., 672:1184], w[..., 1188:1444]], axis=-1)


def _k_win_inv(m):
    gk = m[..., 384:416] + jnp.concatenate([m[..., 528:544], m[..., 512:528]], axis=-1)
    return jnp.concatenate([m[..., 0:384], gk, m[..., 640:896], m[..., 1280:1792], m[..., 1152:1156], m[..., 1792:2048],
                            m[..., 896:1152]], axis=-1)


def _k_wq(w):
    nh = w.shape[-1] // (NOPE + ROPE)
    w = w.reshape(w.shape[:-1] + (nh, NOPE + ROPE))
    nope, r1, r2 = w[..., :NOPE], w[..., NOPE:NOPE + 16], w[..., NOPE + 16:]
    z = lambda k: jnp.zeros(w.shape[:-1] + (k,), w.dtype)
    both = jnp.stack([jnp.concatenate([nope, r1, r2, z(32)], -1), jnp.concatenate([z(64), r2, r1, z(32)], -1)], axis=-3)
    return both.reshape(w.shape[:-2] + (2 * nh * SLAB,))


def _k_wq_inv(m):
    nh = m.shape[-1] // (2 * SLAB)
    m = m.reshape(m.shape[:-1] + (2, nh, SLAB))
    q0, q1 = m[..., 0, :, :], m[..., 1, :, :]
    w = jnp.concatenate([q0[..., :64], q0[..., 64:80] + q1[..., 80:96], q0[..., 80:96] + q1[..., 64:80]], -1)
    return w.reshape(w.shape[:-2] + (nh * (NOPE + ROPE),))


def _k_wkv(w):
    nh = w.shape[-1] // (NOPE + VDIM)
    w = w.reshape(w.shape[:-1] + (nh, NOPE + VDIM))
    z = jnp.zeros(w.shape[:-1] + (64,), w.dtype)
    return jnp.concatenate([w[..., :NOPE], z, w[..., NOPE:], z], -1).reshape(w.shape[:-2] + (nh * 2 * SLAB,))


def _k_wkv_inv(m):
    nh = m.shape[-1] // (2 * SLAB)
    m = m.reshape(m.shape[:-1] + (nh, 2 * SLAB))
    return jnp.concatenate([m[..., :NOPE], m[..., SLAB:SLAB + VDIM]], -1).reshape(m.shape[:-2] + (nh * (NOPE + VDIM),))


def _k_swap(w):
    return jnp.swapaxes(w, -1, -2)


_K_FWD = {'win': _k_win, 'wq': _k_wq, 'wkv': _k_wkv, 'w_mlp1': _k_swap}
_K_INV = {'win': _k_win_inv, 'wq': _k_wq_inv, 'wkv': _k_wkv_inv, 'w_mlp1': _k_swap}


def _assemble(piece, g):
    _, _, a, b = g.shape
    if piece == 'wq':
        return g.reshape(2, 4, a, 2, SLAB).transpose(2, 3, 1, 0, 4).reshape(a, N_DEV * b)
    if piece in PIECE_COLS:
        return g.transpose(2, 1, 0, 3).reshape(a, N_DEV * b)
    return g.transpose(1, 0, 2, 3).reshape(N_DEV * a, b)


def _disassemble(piece, full):
    a, b = PIECE_SHAPE[piece]
    if piece == 'wq':
        return full.reshape(a, 2, 4, 2, SLAB).transpose(3, 2, 0, 1, 4).reshape(2, 4, a, b)
    if piece in PIECE_COLS:
        return full.reshape(a, 4, 2, b).transpose(2, 1, 0, 3)
    return full.reshape(4, 2, a, b).transpose(1, 0, 2, 3)


def _piece_rows(piece):
    a, b = PIECE_SHAPE[piece]
    return a * b // PACK_C


def _prep_layer(pieces, Ws, l):
    P = dict(pieces)
    ri, ci = jnp.arange(SLAB)[:, None], jnp.arange(2 * SLAB)[None, :]
    sel = ((ri < ROPE) & (ci == ri + NOPE)).astype(P['wkv'].dtype)
    P['wkv'] = jnp.concatenate([P['wkv'], jnp.tile(sel, (1, HEADS))], axis=0)
    wout = P['wout']
    mla_rows = jnp.pad(wout[:HEADS * VDIM].reshape(HEADS, VDIM, D), ((0, 0), (0, SLAB - VDIM), (0, 0)))
    P['wout'] = jnp.concatenate([mla_rows.reshape(HEADS * SLAB, D), wout[HEADS * VDIM:]], axis=0)
    W = Ws
    row = lambda n: W[n][l].astype(F32).reshape(1, -1)
    for n in ('mix_norm_g', 'mla_q_norm_g', 'mla_kv_norm_g', 'ssm_norm_g', 'lru_lambda', 'lru_out_g', 'xattn_norm_g',
              'mem_norm_g', 'mlp_norm_g'):
        P[n] = row(n)
    P['mla_out_g'] = jnp.pad(W['mla_out_g'][l].astype(F32).reshape(HEADS, VDIM), ((0, 0), (0, SLAB - VDIM))).reshape(1, -1)
    for n in ('ssm_dt_bias', 'ssm_a_log', 'ssm_d'):
        P[n] = _pad_lanes(W[n][l])
    P['conv_w'] = jnp.pad(jnp.concatenate([W['ssm_conv_w'][l], W['lru_conv_w'][l]], axis=1).astype(F32), ((0, 4), (0, 0)))
    P['conv_b'] = jnp.concatenate([W['ssm_conv_b'][l], W['lru_conv_b'][l]]).astype(F32).reshape(1, -1)
    for n in ('lru_w_a', 'lru_w_i'):
        P[n] = jnp.concatenate([jnp.pad(W[n][l, k].astype(F32), ((0, 0), (64 * k, 192 - 64 * k))) for k in range(4)], axis=0)
    for n in ('lru_b_a', 'lru_b_i'):
        P[n] = W[n][l].astype(F32).reshape(1, -1)
    return P


def _unprep_pieces(G):
    o = {n: G[n] for n in PIECES}
    o['wkv'] = G['wkv'][:KV_RANK]
    wo = G['wout']
    o['wout'] = jnp.concatenate([wo[:HEADS * SLAB].reshape(HEADS, SLAB, D)[:, :VDIM].reshape(HEADS * VDIM, D),
                                 wo[HEADS * SLAB:]], axis=0)
    return o


def _unprep_small(G):
    o = {}
    for n in ('mix_norm_g', 'mla_q_norm_g', 'mla_kv_norm_g', 'ssm_norm_g', 'lru_lambda', 'lru_out_g', 'xattn_norm_g',
              'mem_norm_g', 'mlp_norm_g', 'lru_b_a', 'lru_b_i'):
        o[n] = G[n].reshape(-1)
    o['lru_b_a'] = o['lru_b_a'].reshape(4, 64)
    o['lru_b_i'] = o['lru_b_i'].reshape(4, 64)
    o['mla_out_g'] = G['mla_out_g'].reshape(HEADS, SLAB)[:, :VDIM].reshape(-1)
    for n in ('ssm_dt_bias', 'ssm_a_log', 'ssm_d'):
        o[n] = G[n][0, :4]
    o['ssm_conv_w'], o['lru_conv_w'] = G['conv_w'][:4, :512], G['conv_w'][:4, 512:]
    o['ssm_conv_b'], o['lru_conv_b'] = G['conv_b'][0, :512], G['conv_b'][0, 512:]
    for n in ('lru_w_a', 'lru_w_i'):
        o[n] = jnp.stack([G[n][64 * k:64 * (k + 1), 64 * k:64 * (k + 1)] for k in range(4)])
    return o


def _rope_tables(positions):
    half = ROPE // 2
    inv_freq = ROPE_THETA ** (-jnp.arange(half, dtype=F32) * 2.0 / ROPE)
    ang = positions.astype(F32)[:, None] * inv_freq
    cos, sin = jnp.cos(ang), jnp.sin(ang)
    T = positions.shape[0]
    z = lambda k: jnp.zeros((T, k), F32)
    ck = jnp.concatenate([cos, cos, z(96)], axis=1)
    sk = jnp.concatenate([-sin, sin, z(96)], axis=1)
    cq = jnp.concatenate([jnp.ones((T, NOPE), F32), cos, cos, z(32)], axis=1)
    sq = jnp.concatenate([z(NOPE), -sin, sin, z(32)], axis=1)
    return ck, sk, cq * _ATT_SCALE2, sq * _ATT_SCALE2


def _add_epi(acc, res):
    return (acc + res,)


def _add_norm_epi(acc, res, g):
    x = acc + res
    return x, _rms(x, g, x.shape[-1])


def _norm_bwd_epi(acc, x, res, g):
    _, vjp = jax.vjp(lambda xv, gv: _rms(xv, gv, xv.shape[-1]), x, g)
    dx, dg = vjp(acc)
    dx = dx + res
    return dx, dx, dg


def _relu2_epi(acc):
    r = jnp.maximum(acc, 0.0)
    return r, r * r


def _drelu2_epi(acc, r):
    return (acc * (2.0 * r.astype(F32)),)


def _norm(x, g, name):
    return _rows_fwd(_f_norm, [x], [g], [(x.shape[1], BF16)], name=name)[0]


def _norm_bwd(x, g, ct, add, name):
    (dx, dx16), (dg,) = _rows_vjp(_f_norm, [x], [g], [ct], name=name, drows=[0], dparams=[0], drow_dtypes=[F32],
                                  add=add, twin=True)
    return dx, dx16, dg


def _layer_fwd(x0, h1, mem, P, tabs, g_next=None, send=None):
    ck, sk, cq, sq = tabs
    S = {'x0': x0}
    if h1 is None:
        h1 = _norm(x0, P['mix_norm_g'], "norm_mix")
    S['h1'] = h1
    win = P['win']
    u_mla = S['u_mla'] = _mm(h1, win[:, 0:U_MLA], name="in_mla")
    u_gate = S['u_gate'] = _mm(h1, win[:, U_MLA:U_MLA + U_GATE], name="in_gate")
    u_conv = S['u_conv'] = _mm(h1, win[:, U_MLA + U_GATE:], name="in_conv")
    cqn, akv = _rows_fwd(_f_mla_prep, [u_mla, ck, sk], [P['mla_q_norm_g'], P['mla_kv_norm_g']],
                         [(Q_RANK, BF16), (2 * SLAB, BF16)], name="mla_prep")
    S['cqn'], S['akv'] = cqn, akv
    yq = _mm(cqn, P['wq'], name="q_proj")
    q = S['q'] = _rows_fwd(_f_qrope, [yq, cq, sq], [], [(HEADS * SLAB, BF16)], name="q_rope")[0]
    kv = S['kv'] = _mm(akv, P['wkv'], name="kv_proj", out_dtypes=(BF16,))
    o, lse, *got = _attn_fwd(q, kv, name="attn_fwd" if send is None else "attn_fwd_gather", send=send)
    S['o'], S['lse'] = o, lse
    c_ssm, c_lru = _conv_fwd(u_conv, P['conv_w'], P['conv_b'], name="conv_fwd")
    S['c_ssm'], S['c_lru'] = c_ssm, c_lru
    ys, sall = _ssd_fwd(c_ssm, u_gate, P['ssm_dt_bias'], P['ssm_a_log'], P['ssm_d'], P['ssm_norm_g'], name="ssd_fwd")
    S['ys'], S['sall'] = ys, sall
    a, b = _rows_fwd(_f_lru_gates, [c_lru], [P['lru_w_a'], P['lru_b_a'], P['lru_w_i'], P['lru_b_i'], P['lru_lambda']],
                     [(256, F32), (256, F32)], name="lru_gates", tm=_MM_ROWS)
    h, hprev = _lru_scan_fwd(a, b, name="lru_scan")
    S['a'], S['h'], S['hprev'] = a, h, hprev
    ymix = S['ymix'] = _rows_fwd(_f_mix, [o, ys, h, u_gate], [P['mla_out_g'], P['lru_out_g']],
                                 [(HEADS * SLAB + 512, BF16)], name="mix")[0]
    x1, hx = _mm(ymix, P['wout'], name="out_proj", epi=_add_norm_epi, extras=(x0, P['xattn_norm_g']), out_dtypes=(F32, BF16))
    S['x1'], S['hx'] = x1, hx
    qx = S['qx'] = _mm(hx, P['w_mq'], name="mem_q", out_dtypes=(BF16,))
    mn = S['mn'] = _norm(mem, P['mem_norm_g'], "norm_mem")
    kx = S['kx'] = _mm(mn, P['w_mk'], name="mem_k", out_dtypes=(BF16,))
    vx = S['vx'] = _mm(mn, P['w_mv'], name="mem_v", out_dtypes=(BF16,))
    ox = S['ox'] = _rows_fwd(_f_xattn, [qx], [kx, vx], [(D, BF16)], name="xattn", tm=_MM_ROWS)[0]
    x2, hm = _mm(ox, P['w_mo'], name="mem_o", epi=_add_norm_epi, extras=(x1, P['mlp_norm_g']), out_dtypes=(F32, BF16))
    S['x2'], S['hm'] = x2, hm
    r, s = _mm(hm, P['w_mlp1'], "nt", name="mlp_up", epi=_relu2_epi, out_dtypes=(BF16, BF16))
    S['r'], S['s'] = r, s
    if g_next is None:
        x3, h_next = _mm(s, P['w_mlp2'], name="mlp_down_last", epi=_add_epi, extras=(x2,)), None
    else:
        x3, h_next = _mm(s, P['w_mlp2'], name="mlp_down", epi=_add_norm_epi, extras=(x2, g_next), out_dtypes=(F32, BF16))
    return x3, h_next, S, (got[0] if got else None)


def _layer_bwd(dx3, dx3h, mem, S, P, tabs, send=None):
    ck, sk, cq, sq = tabs
    G = {}
    da = _mm(dx3h, P['w_mlp2'], "nt", name="mlp_down_dx", epi=_drelu2_epi, extras=(S['r'],), out_dtypes=(BF16,))
    G['w_mlp2'] = _mm(S['s'], dx3h, "tn", name="mlp_down_dw")
    G['w_mlp1'] = _mm(da, S['hm'], "tn", name="mlp_up_dw")
    norm_out = dict(epi=_norm_bwd_epi, out_dtypes=(F32, BF16), col_sums=1)
    dx2, dx2h, G['mlp_norm_g'] = _mm(da, P['w_mlp1'], "nn", name="mlp_up_dx", extras=(S['x2'], dx3, P['mlp_norm_g']),
                                     **norm_out)
    dox = _mm(dx2h, P['w_mo'], "nt", name="mem_o_dx")
    G['w_mo'] = _mm(S['ox'], dx2h, "tn", name="mem_o_dw")
    (dqx,), (dkx, dvx) = _rows_vjp(_f_xattn, [S['qx']], [S['kx'], S['vx']], [dox], name="xattn_bwd", drows=[0],
                                   dparams=[0, 1], drow_dtypes=[BF16], tm=_MM_ROWS)
    G['w_mq'] = _mm(S['hx'], dqx, "tn", name="mem_q_dw")
    dx1, dx1h, G['xattn_norm_g'] = _mm(dqx, P['w_mq'], "nt", name="mem_q_dx", extras=(S['x1'], dx2, P['xattn_norm_g']),
                                       **norm_out)
    G['w_mk'] = _mm(S['mn'], dkx, "tn", name="mem_k_dw")
    G['w_mv'] = _mm(S['mn'], dvx, "tn", name="mem_v_dw")
    dmn = _mm(dkx, P['w_mk'], "nt", name="mem_k_dx", epi=_add_epi, extras=(_mm(dvx, P['w_mv'], "nt", name="mem_v_dx"),))
    _, _, G['mem_norm_g'] = _norm_bwd(mem, P['mem_norm_g'], dmn, None, "norm_mem_bwd")
    dymix = _mm(dx1h, P['wout'], "nt", name="out_proj_dx")
    G['wout'] = _mm(S['ymix'], dx1h, "tn", name="out_proj_dw")
    (do, dys, dh, dug_mix), (G['mla_out_g'], G['lru_out_g']) = _rows_vjp(
        _f_mix, [S['o'], S['ys'], S['h'], S['u_gate']], [P['mla_out_g'], P['lru_out_g']], [dymix], name="mix_bwd",
        drows=[0, 1, 2, 3], dparams=[0, 1], drow_dtypes=[BF16, F32, F32, F32])
    g, da_lru = _lru_scan_bwd(S['a'], dh, S['hprev'], name="lru_scan_bwd")
    lru_par = [P['lru_w_a'], P['lru_b_a'], P['lru_w_i'], P['lru_b_i'], P['lru_lambda']]
    (dc_lru,), dpar = _rows_vjp(_f_lru_gates, [S['c_lru']], lru_par, [da_lru, g], name="lru_gates_bwd", drows=[0],
                                dparams=[0, 1, 2, 3, 4], drow_dtypes=[F32], tm=_MM_ROWS)
    G['lru_w_a'], G['lru_b_a'], G['lru_w_i'], G['lru_b_i'], G['lru_lambda'] = dpar
    dc_ssm, dug_ssd, G['ssm_dt_bias'], G['ssm_a_log'], G['ssm_d'], G['ssm_norm_g'] = _ssd_bwd(
        S['c_ssm'], S['u_gate'], S['sall'], dys, P['ssm_dt_bias'], P['ssm_a_log'], P['ssm_d'], P['ssm_norm_g'],
        name="ssd_bwd")
    du_conv, G['conv_w'], G['conv_b'] = _conv_bwd(S['u_conv'], dc_ssm, dc_lru, P['conv_w'], name="conv_bwd")
    delta = _attn_delta(do, S['o'], name="attn_delta")
    dqt, dkv, *got = _attn_bwd(S['q'], S['kv'], do, S['lse'], delta,
                               name="attn_bwd" if send is None else "attn_bwd_scatter", send=send)
    dyq = _qrope_bwd(dqt, cq, sq, name="q_rope_bwd")
    dcqn = _mm(dyq, P['wq'], "nt", name="q_proj_dx")
    G['wq'] = _mm(S['cqn'], dyq, "tn", name="q_proj_dw")
    dakv = _mm(dkv, P['wkv'], "nt", name="kv_proj_dx")
    G['wkv'] = _mm(S['akv'], dkv, "tn", name="kv_proj_dw")
    (du_mla,), (G['mla_q_norm_g'], G['mla_kv_norm_g']) = _rows_vjp(
        _f_mla_prep, [S['u_mla'], ck, sk], [P['mla_q_norm_g'], P['mla_kv_norm_g']], [dcqn, dakv], name="mla_prep_bwd",
        drows=[0], dparams=[0, 1], drow_dtypes=[BF16])
    du = jnp.concatenate([du_mla, (dug_mix + dug_ssd).astype(BF16), du_conv.astype(BF16)], axis=1)
    G['win'] = _mm(S['h1'], du, "tn", name="in_dw")
    dx0, dx0h, G['mix_norm_g'] = _mm(du, P['win'], "nt", name="in_dx", extras=(S['x0'], dx1, P['mix_norm_g']), **norm_out)
    return dx0, dx0h, G, (got[0] if got else None)


class _NoExchange:
    def __init__(self, layers):
        self.layers = layers

    def pieces(self, l):
        return self.layers[l]

    def fwd_send(self, l):
        return None

    def fwd_got(self, l, got):
        pass

    def bwd_send(self, l):
        return None

    def bwd_got(self, l, got):
        pass

    def grads_ready(self, l, pieces):
        pass


def _local_step(x, mem, positions, ex, Ws, tgt):
    tabs = _rope_tables(positions)
    saved, preps, h = [], [], None
    for l in range(DEPTH):
        P = _prep_layer(ex.pieces(l), Ws, l)
        send = ex.fwd_send(l)
        g_next = Ws['mix_norm_g'][l + 1].astype(F32).reshape(1, D) if l + 1 < DEPTH else None
        x, h, S, got = _layer_fwd(x, h, mem, P, tabs, g_next, send)
        if send is not None:
            ex.fwd_got(l, got)
        saved.append(S)
        preps.append(P)
    loss, dx, dxh, dg_final = _loss_head(x, tgt, Ws['final_norm_g'].astype(F32).reshape(1, D), name="loss_head")
    pieces, small = [None] * DEPTH, [None] * DEPTH
    for l in reversed(range(DEPTH)):
        send = ex.bwd_send(l)
        dx, dxh, G, got = _layer_bwd(dx, dxh, mem, saved[l], preps[l], tabs, send)
        if send is not None:
            ex.bwd_got(l, got)
        pieces[l], small[l] = _unprep_pieces(G), _unprep_small(G)
        ex.grads_ready(l, pieces[l])
    grads = {n: jnp.stack([small[l][n] for l in range(DEPTH)]) for n in SMALL if n != 'final_norm_g'}
    grads['final_norm_g'] = dg_final.reshape(D)
    return loss, dx, pieces, grads


def _pack_rows(pieces):
    return jnp.concatenate([pieces[n].reshape(pieces[n].shape[:-2] + (-1, PACK_C)) for n in PIECES], axis=-2)


def _unpack_rows(packed, lead=()):
    out, off = {}, 0
    for n in PIECES:
        rows = _piece_rows(n)
        out[n] = packed[..., off:off + rows, :].reshape(lead + PIECE_SHAPE[n])
        off += rows
    return out


class _StepExchange(_NoExchange):
    def __init__(self, shard):
        self.shard = {n: a.astype(BF16) for n, a in shard.items()}
        self.layers = {}
        self.sums, self.reduced = {}, {}
        self._take(_all_gather(_pack_rows({n: a[0] for n, a in self.shard.items()}), name="gather_w0"), [0])

    def _take(self, gathered, layers):
        per = gathered.reshape(2, 4, len(layers), -1, PACK_C)
        for k, l in enumerate(layers):
            self.layers[l] = {n: _assemble(n, g) for n, g in _unpack_rows(per[:, :, k], lead=(2, 4)).items()}

    def _rest(self):
        return _pack_rows({n: a[1:] for n, a in self.shard.items()}).reshape(-1, PACK_C)

    def fwd_send(self, l):
        return self._rest() if l == 0 else None

    def fwd_got(self, l, got):
        self._take(_finish_gather(self._rest(), got, name="gather_w"), list(range(1, DEPTH)))

    def grads_ready(self, l, pieces):
        x = _pack_rows({n: _disassemble(n, pieces[n]) for n in PIECES})
        self.sums[l] = _chip_sums(x, name="scatter_g")
        if l == 0:
            self.bwd_got(-1, _chip_exchange(self.sums.pop(0), False, name="scatter_g_chips"))

    def bwd_send(self, l):
        return self.sums.pop(l + 1, None)

    def bwd_got(self, l, got):
        self.reduced[l + 1] = _sum_fixed(got, F32, name="scatter_g_sum")


def _adamw_nd(w, g, m, v, name):
    shp = w.shape
    two = lambda a: a.reshape(-1, shp[-1])
    return [r.reshape(shp) for r in _adamw(two(w), two(g), two(m), two(v), name=name)]


def kernel(x, mem, positions, mix_norm_g, w_in, mla_q_norm_g, mla_kv_norm_g, mla_w_uq, mla_w_ukv, mla_out_g, ssm_conv_w, ssm_conv_b, ssm_dt_bias, ssm_a_log, ssm_d, ssm_norm_g, lru_conv_w, lru_conv_b, lru_w_a, lru_b_a, lru_w_i, lru_b_i, lru_lambda, lru_out_g, w_out, xattn_norm_g, mem_norm_g, w_mq, w_mk, w_mv, w_mo, mlp_norm_g, w_mlp1, w_mlp2, final_norm_g, loss_target, m_mix_norm_g, m_w_in, m_mla_q_norm_g, m_mla_kv_norm_g, m_mla_w_uq, m_mla_w_ukv, m_mla_out_g, m_ssm_conv_w, m_ssm_conv_b, m_ssm_dt_bias, m_ssm_a_log, m_ssm_d, m_ssm_norm_g, m_lru_conv_w, m_lru_conv_b, m_lru_w_a, m_lru_b_a, m_lru_w_i, m_lru_b_i, m_lru_lambda, m_lru_out_g, m_w_out, m_xattn_norm_g, m_mem_norm_g, m_w_mq, m_w_mk, m_w_mv, m_w_mo, m_mlp_norm_g, m_w_mlp1, m_w_mlp2, m_final_norm_g, v_mix_norm_g, v_w_in, v_mla_q_norm_g, v_mla_kv_norm_g, v_mla_w_uq, v_mla_w_ukv, v_mla_out_g, v_ssm_conv_w, v_ssm_conv_b, v_ssm_dt_bias, v_ssm_a_log, v_ssm_d, v_ssm_norm_g, v_lru_conv_w, v_lru_conv_b, v_lru_w_a, v_lru_b_a, v_lru_w_i, v_lru_b_i, v_lru_lambda, v_lru_out_g, v_w_out, v_xattn_norm_g, v_mem_norm_g, v_w_mq, v_w_mk, v_w_mv, v_w_mo, v_mlp_norm_g, v_w_mlp1, v_w_mlp2, v_final_norm_g):
    a = locals()
    w = {n: a[n] for n in WEIGHTS}
    m = {n: a['m_' + n] for n in WEIGHTS}
    v = {n: a['v_' + n] for n in WEIGHTS}
    me = 4 * lax.axis_index("x") + 2 * lax.axis_index("y") + lax.axis_index("c")

    ex = _StepExchange({n: _K_FWD.get(n, lambda a: a)(w[PIECE_SOURCE.get(n, n)]) for n in PIECES})
    Ws = {}
    conv_shapes = [w[n].shape for n in CONV_SHARDED]
    conv_g = _all_gather(_pack([w[n] for n in CONV_SHARDED], F32), name="gather_conv")
    conv_g = conv_g.transpose(1, 0, 2, 3).reshape((N_DEV,) + conv_g.shape[2:])
    for n, g in zip(CONV_SHARDED, _unpack(conv_g, conv_shapes, lead=(N_DEV,))):
        Ws[n] = g.transpose(1, 2, 0, 3).reshape(g.shape[1], g.shape[2], N_DEV * g.shape[3])
    for n in SMALL:
        if n not in CONV_SHARDED:
            Ws[n] = w[n]

    loss_share, dx, _, grads = _local_step(x[0], mem[0], positions[0], ex, Ws, loss_target[0])
    loss = lax.psum(loss_share[0, 0], ("x", "y", "c"))

    g_out = {}
    for n in PIECES:
        g = jnp.stack([_unpack_rows(ex.reduced[l])[n] for l in range(DEPTH)])
        g_out[PIECE_SOURCE.get(n, n)] = _K_INV[n](g) if n in _K_INV else g
    small_shapes = [grads[n].shape for n in SMALL]
    g_small = _all_reduce(_pack([grads[n] for n in SMALL], F32), name="reduce_g")
    for n, g in zip(SMALL, _unpack(g_small, small_shapes)):
        if n in CONV_SHARDED:
            cols = w[n].shape[-1]
            g = lax.dynamic_slice_in_dim(g, me * cols, cols, axis=2)
        g_out[n] = g

    delta, new_m, new_v = {}, {}, {}
    for n in BIG:
        delta[n], new_m[n], new_v[n] = _adamw_nd(w[n], g_out[n], m[n], v[n], "adamw_" + n)
    shapes = [w[n].shape for n in SMALL]
    packed = [_pack([d[n] for n in SMALL], F32) for d in (w, g_out, m, v)]
    for d, res in zip((delta, new_m, new_v), _adamw(*packed, name="adamw_small")):
        d.update(zip(SMALL, _unpack(res, shapes)))

    return (loss, dx[None], *[g_out[n] for n in WEIGHTS], *[delta[n] for n in WEIGHTS],
            *[new_m[n] for n in WEIGHTS], *[new_v[n] for n in WEIGHTS])
```

```python
import functools
import math

import jax
import jax.numpy as jnp
from jax import lax
from jax.experimental import pallas as pl
from jax.experimental.pallas import tpu as pltpu

F32, BF16 = jnp.float32, jnp.bfloat16

D = 1024
DEPTH = 4
N_MEM = 256
EPS = 1e-6
HEADS = 8
NOPE, ROPE, VDIM = 64, 32, 64
Q_RANK, KV_RANK = 256, 128
ROPE_THETA = 10000.0
SSM_CHUNK = 128
LRU_C = 8.0
MEM_HEADS = 4
D_FF = 4 * D
SLAB = 128
LR, B1, B2, AEPS, WD, STEP = 0.001, 0.9, 0.999, 1e-08, 0.01, 10

N_DEV = 8
MESH = pl.DeviceIdType.MESH

U_MLA = 640
U_GATE = 640
U_CONV = 768

_DN = {"nn": (((1,), (0,)), ((), ())), "nt": (((1,), (1,)), ((), ())), "tn": (((0,), (0,)), ((), ()))}


def _dot(a, b, kind):
    return lax.dot_general(a.astype(BF16), b.astype(BF16), _DN[kind], preferred_element_type=F32)


@functools.partial(jax.custom_vjp, nondiff_argnums=(2,))
def _bdot(a, b, kind):
    return _dot(a, b, kind)


def _bdot_fwd(a, b, kind):
    return _dot(a, b, kind), (a, b)


def _bdot_bwd(kind, res, g):
    a, b = res
    if kind == "nn":
        da, db = _dot(g, b, "nt"), _dot(a, g, "tn")
    elif kind == "nt":
        da, db = _dot(g, b, "nn"), _dot(g, a, "tn")
    else:
        da, db = _dot(b, g, "nt"), _dot(a, g, "nn")
    return da.astype(a.dtype), db.astype(b.dtype)


_bdot.defvjp(_bdot_fwd, _bdot_bwd)


def _tile(n, pref):
    if n <= pref:
        return n
    t = pref
    while n % t:
        t -= SLAB
    return t


def _cparams(sem, vmem_mb=48):
    return pltpu.CompilerParams(dimension_semantics=sem, vmem_limit_bytes=vmem_mb * 1024 * 1024)


def _mm(a, b, kind="nn", *, name, out_dtypes=(F32,), epi=None, extras=(), col_sums=0, tm=1024, tn=1024, tk=1024):
    if kind == "tn":
        K, M = a.shape
    else:
        M, K = a.shape
    N = b.shape[0] if kind == "nt" else b.shape[1]
    if a.dtype == F32 or b.dtype == F32:
        tk = tk // 2
    tm, tn, tk = _tile(M, tm), _tile(N, tn), _tile(K, tk)
    assert not col_sums or tn == N, "column sums need one tile across the columns"
    nk = K // tk
    a_spec = pl.BlockSpec((tk, tm), lambda i, j, k: (k, i)) if kind == "tn" else pl.BlockSpec((tm, tk), lambda i, j, k: (i, k))
    b_spec = pl.BlockSpec((tn, tk), lambda i, j, k: (j, k)) if kind == "nt" else pl.BlockSpec((tk, tn), lambda i, j, k: (k, j))
    o_spec = pl.BlockSpec((tm, tn), lambda i, j, k: (i, j))
    vec_spec = pl.BlockSpec((1, tn), lambda i, j, k: (0, j))
    ex_specs = [vec_spec if e.shape[0] == 1 else o_spec for e in extras]
    n_ex, n_out = len(extras), len(out_dtypes)

    def body(*refs):
        a_ref, b_ref = refs[:2]
        ex = refs[2:2 + n_ex]
        outs = refs[2 + n_ex:2 + n_ex + n_out]
        acc = refs[-1]
        k = pl.program_id(2)
        first_row_tile = pl.program_id(0) == 0

        def finish(r):
            res = epi(r, *[e[...] for e in ex]) if epi is not None else (r,)
            for o, v in zip(outs, res):
                o[...] = v.astype(o.dtype)
            sums = refs[2 + n_ex + n_out:2 + n_ex + n_out + col_sums]

            @pl.when(first_row_tile)
            def _():
                for o in sums:
                    o[...] = jnp.zeros_like(o)

            for o, v in zip(sums, res[n_out:]):
                o[...] += v

        if nk == 1:
            finish(_dot(a_ref[...], b_ref[...], kind))
            return

        @pl.when(k == 0)
        def _():
            acc[...] = _dot(a_ref[...], b_ref[...], kind)

        @pl.when(k > 0)
        def _():
            acc[...] += _dot(a_ref[...], b_ref[...], kind)

        @pl.when(k == nk - 1)
        def _():
            finish(acc[...])

    res = pl.pallas_call(
        body,
        name=name,
        grid=(M // tm, N // tn, nk),
        in_specs=[a_spec, b_spec] + ex_specs,
        out_specs=[o_spec] * n_out + [vec_spec] * col_sums,
        out_shape=[jax.ShapeDtypeStruct((M, N), dt) for dt in out_dtypes] + [jax.ShapeDtypeStruct((1, N), F32)] * col_sums,
        scratch_shapes=[pltpu.VMEM((tm, tn), F32)],
        compiler_params=_cparams(("arbitrary" if col_sums else "parallel", "parallel", "arbitrary")),
    )(a, b, *extras)
    return res[0] if n_out + col_sums == 1 else res


_MM_ROWS = 1024


def _row_spec(arr, tm):
    return pl.BlockSpec((tm, arr.shape[1]), lambda i: (i, 0))


def _full_spec(arr):
    nd = arr.ndim
    return pl.BlockSpec(arr.shape, lambda i: (0,) * nd)


def _rows_fwd(fn, rows, params, outs, *, name, tm=256):
    T = rows[0].shape[0]
    tm = min(tm, T)
    nr, npar = len(rows), len(params)

    def body(*refs):
        ins = [r[...] for r in refs[:nr + npar]]
        res = fn(*ins)
        for o, v in zip(refs[nr + npar:], res):
            o[...] = v.astype(o.dtype)

    res = pl.pallas_call(
        body,
        name=name,
        grid=(T // tm,),
        in_specs=[_row_spec(r, tm) for r in rows] + [_full_spec(p) for p in params],
        out_specs=[pl.BlockSpec((tm, c), lambda i: (i, 0)) for c, _ in outs],
        out_shape=[jax.ShapeDtypeStruct((T, c), dt) for c, dt in outs],
        compiler_params=_cparams(("parallel",)),
    )(*rows, *params)
    return res


def _rows_vjp(fn, rows, params, cts, *, name, drows, dparams, drow_dtypes, add=None, twin=False, tm=256):
    T = rows[0].shape[0]
    tm = min(tm, T)
    nr, npar, nct = len(rows), len(params), len(cts)
    n_add = 0 if add is None else 1
    n_dr, n_dp = len(drows), len(dparams)
    n_tw = 1 if twin else 0

    def body(*refs):
        row_t = [r[...] for r in refs[:nr]]
        par_t = [r[...] for r in refs[nr:nr + npar]]
        ct_t = [r[...].astype(F32) for r in refs[nr + npar:nr + npar + nct]]
        pos = nr + npar + nct
        add_t = refs[pos][...] if n_add else None
        pos += n_add
        drow_refs = refs[pos:pos + n_dr]
        dpar_refs = refs[pos + n_dr:pos + n_dr + n_dp]

        def g(*dargs):
            rr, pp = list(row_t), list(par_t)
            for idx, v in zip(drows, dargs[:n_dr]):
                rr[idx] = v
            for idx, v in zip(dparams, dargs[n_dr:]):
                pp[idx] = v
            return tuple(fn(*rr, *pp))

        prim = [row_t[i].astype(F32) for i in drows] + [par_t[i].astype(F32) for i in dparams]
        _, vjp = jax.vjp(g, *prim)
        grads = vjp(tuple(ct_t))
        for n, (o, v) in enumerate(zip(drow_refs, grads[:n_dr])):
            if n == 0 and n_add:
                v = v + add_t.astype(F32)
            o[...] = v.astype(o.dtype)
            if n == 0 and n_tw:
                refs[-1][...] = v.astype(BF16)

        @pl.when(pl.program_id(0) == 0)
        def _():
            for o in dpar_refs:
                o[...] = jnp.zeros_like(o)

        for o, v in zip(dpar_refs, grads[n_dr:]):
            o[...] += v

    res = pl.pallas_call(
        body,
        name=name,
        grid=(T // tm,),
        in_specs=[_row_spec(r, tm) for r in rows] + [_full_spec(p) for p in params] + [_row_spec(c, tm) for c in cts]
        + ([_row_spec(add, tm)] if n_add else []),
        out_specs=[_row_spec(rows[i], tm) for i in drows] + [_full_spec(params[i]) for i in dparams]
        + [_row_spec(rows[drows[0]], tm)] * n_tw,
        out_shape=[jax.ShapeDtypeStruct(rows[i].shape, dt) for i, dt in zip(drows, drow_dtypes)]
        + [jax.ShapeDtypeStruct(params[i].shape, F32) for i in dparams]
        + [jax.ShapeDtypeStruct(rows[drows[0]].shape, BF16)] * n_tw,
        compiler_params=_cparams(("arbitrary",)),
    )(*rows, *params, *cts, *([add] if n_add else []))
    return list(res[:n_dr]) + list(res[n_dr + n_dp:]), list(res[n_dr:n_dr + n_dp])


def _rms(x, g, n):
    return x * lax.rsqrt(jnp.sum(x * x, axis=-1, keepdims=True) * (1.0 / n) + EPS) * g


def _sigmoid(x):
    return 1.0 / (1.0 + jnp.exp(-x))


def _silu(x):
    return x * _sigmoid(x)


def _softplus(x):
    return jnp.maximum(x, 0.0) + jnp.log(1.0 + jnp.exp(-jnp.abs(x)))


def _gelu_tanh(x):
    return 0.5 * x * (1.0 + jnp.tanh(math.sqrt(2.0 / math.pi) * (x + 0.044715 * x * x * x)))


def _lane(shape):
    return lax.broadcasted_iota(jnp.int32, shape, len(shape) - 1)


def _col(x, h):
    return jnp.sum(jnp.where(_lane(x.shape) == h, x, 0.0), axis=-1, keepdims=True)


def _f_norm(x, g):
    return (_rms(x.astype(F32), g, x.shape[-1]),)


def _f_mla_prep(u, ck, sk, gq, gkv):
    u = u.astype(F32)
    cq = _rms(u[:, 0:256], gq, Q_RANK)
    ckv = _rms(u[:, 256:384], gkv, KV_RANK)
    kr = u[:, 384:512] * ck + u[:, 512:640] * sk
    return cq, jnp.concatenate([ckv, kr], axis=1)


def _f_qrope(y, cq, sq):
    y = y.astype(F32)
    c8, s8 = jnp.tile(cq, (1, HEADS)), jnp.tile(sq, (1, HEADS))
    return (y[:, :HEADS * SLAB] * c8 + y[:, HEADS * SLAB:] * s8,)


def _f_lru_gates(xc, wa, ba, wi, bi, lam):
    xc = xc.astype(F32)
    r = _sigmoid(_bdot(xc, wa, "nn") + ba)
    i = _sigmoid(_bdot(xc, wi, "nn") + bi)
    log_a = -LRU_C * r * _softplus(-lam)
    a = jnp.exp(log_a)
    x2 = 2.0 * log_a
    m1 = jnp.where(x2 > -0.02, -x2 * (1.0 + x2 * (0.5 + x2 * (1.0 / 6.0 + x2 * (1.0 / 24.0)))), 1.0 - jnp.exp(x2))
    return a, jnp.sqrt(m1) * (i * xc)


def _f_mix(o, ys, h, ug, g_mla, g_lru):
    o = o.astype(F32)
    y_mla = _rms(o, g_mla, HEADS * VDIM)
    y_lru = _rms(h.astype(F32) * _gelu_tanh(ug[:, 256:512].astype(F32)), g_lru, 256)
    return (jnp.concatenate([y_mla, ys.astype(F32), y_lru], axis=1),)


def _f_xattn(q, k, v):
    hd = D // MEM_HEADS
    outs = []
    for h in range(MEM_HEADS):
        sl = slice(h * hd, (h + 1) * hd)
        s = _bdot(q[:, sl], k[:, sl], "nt") * (1.0 / math.sqrt(hd))
        s = s - jnp.max(s, axis=-1, keepdims=True)
        p = jnp.exp(s)
        p = p / jnp.sum(p, axis=-1, keepdims=True)
        outs.append(_bdot(p, v[:, sl], "nn"))
    return (jnp.concatenate(outs, axis=1),)


def _split_dot(tri, a, kind):
    a_hi = a.astype(BF16)
    r1 = a - a_hi.astype(F32)
    a_mid = r1.astype(BF16)
    a_lo = (r1 - a_mid.astype(F32)).astype(BF16)
    return _dot(tri, a_hi, kind) + _dot(tri, a_mid, kind) + _dot(tri, a_lo, kind)


@jax.custom_vjp
def _tri_cumsum(tri, a):
    return _split_dot(tri, a, "nn")


def _tri_cumsum_fwd(tri, a):
    return _split_dot(tri, a, "nn"), tri


def _tri_cumsum_bwd(tri, g):
    return jnp.zeros_like(tri), _split_dot(tri, g, "tn")


_tri_cumsum.defvjp(_tri_cumsum_fwd, _tri_cumsum_bwd)


def _f_ssd_chunk(c, ug, s0, s1, dtb, alog, dsk, ng):
    L = c.shape[0]
    c = c.astype(F32)
    xbc = _silu(c)
    xs, bm, cm = xbc[:, 0:256], xbc[:, 256:384], xbc[:, 384:512]
    z = ug[:, 0:256].astype(F32)
    dt = _softplus(ug[:, 512:640].astype(F32) + dtb)
    a = dt * (-jnp.exp(alog))
    rowi = lax.broadcasted_iota(jnp.int32, (L, L), 0)
    coli = lax.broadcasted_iota(jnp.int32, (L, L), 1)
    tril = rowi >= coli
    acum = _tri_cumsum(tril.astype(BF16), a)
    acum_t = acum.T
    lane = _lane((1, SLAB))
    lo = lane < 64
    ys, new_s = [], []
    for g in range(2):
        gm = (lane >= 64 * g) & (lane < 64 * g + 64)
        bg, cg = jnp.where(gm, bm, 0.0), jnp.where(gm, cm, 0.0)
        cb = _bdot(cg, bg, "nt")
        x = xs[:, SLAB * g:SLAB * (g + 1)]
        h0, h1 = 2 * g, 2 * g + 1
        ac0, ac1 = _col(acum, h0), _col(acum, h1)
        xdt = x * jnp.where(lo, _col(dt, h0), _col(dt, h1))
        ac_l = jnp.where(lo, ac0, ac1)
        tot = acum[L - 1:L, :]
        tot_l = jnp.where(lo, _col(tot, h0), _col(tot, h1))
        yd = jnp.zeros((L, SLAB), F32)
        for hh, acc, hm in ((h0, ac0, lo), (h1, ac1, jnp.logical_not(lo))):
            seg = acc - acum_t[hh:hh + 1, :]
            lm = jnp.where(tril, jnp.exp(jnp.where(tril, seg, 0.0)), 0.0)
            yd = yd + _bdot(cb * lm, jnp.where(hm, xdt, 0.0), "nn")
        sg = (s0, s1)[g]
        y_off = _bdot(cg, sg, "nn") * jnp.exp(ac_l)
        st = _bdot(bg, xdt * jnp.exp(tot_l - ac_l), "tn")
        new_s.append(jnp.exp(tot_l) * sg + st)
        y = yd + y_off + jnp.where(lo, _col(dsk, h0), _col(dsk, h1)) * x
        y = y * _silu(z[:, SLAB * g:SLAB * (g + 1)])
        ys.append(_rms(y, ng[:, SLAB * g:SLAB * (g + 1)], SLAB))
    return jnp.concatenate(ys, axis=1), new_s[0], new_s[1]


_SSD_TILE = 512


def _ssd_fwd(c, ug, dtb, alog, dsk, ng, *, name):
    T = c.shape[0]
    tm = min(_SSD_TILE, T)
    ncs = tm // SSM_CHUNK
    nc = T // SSM_CHUNK

    def body(c_ref, ug_ref, dtb_ref, alog_ref, dsk_ref, ng_ref, y_ref, sall_ref, s_scr):
        @pl.when(pl.program_id(0) == 0)
        def _():
            s_scr[...] = jnp.zeros_like(s_scr)

        s0, s1 = s_scr[0], s_scr[1]
        for k in range(ncs):
            rows = slice(k * SSM_CHUNK, (k + 1) * SSM_CHUNK)
            sall_ref[k, 0] = s0
            sall_ref[k, 1] = s1
            y, s0, s1 = _f_ssd_chunk(c_ref[rows, :], ug_ref[rows, :], s0, s1, dtb_ref[...], alog_ref[...],
                                     dsk_ref[...], ng_ref[...])
            y_ref[rows, :] = y
        s_scr[0] = s0
        s_scr[1] = s1

    y, sall = pl.pallas_call(
        body,
        name=name,
        grid=(T // tm,),
        in_specs=[_row_spec(c, tm), _row_spec(ug, tm)] + [_full_spec(p) for p in (dtb, alog, dsk, ng)],
        out_specs=[pl.BlockSpec((tm, 256), lambda i: (i, 0)), pl.BlockSpec((ncs, 2, SLAB, SLAB), lambda i: (i, 0, 0, 0))],
        out_shape=[jax.ShapeDtypeStruct((T, 256), F32), jax.ShapeDtypeStruct((nc, 2, SLAB, SLAB), F32)],
        scratch_shapes=[pltpu.VMEM((2, SLAB, SLAB), F32)],
        compiler_params=_cparams(("arbitrary",)),
    )(c, ug, dtb, alog, dsk, ng)
    return y, sall


def _ssd_bwd(c, ug, sall, dy, dtb, alog, dsk, ng, *, name):
    T = c.shape[0]
    tm = min(_SSD_TILE, T)
    ncs = tm // SSM_CHUNK
    nt = T // tm

    def body(c_ref, ug_ref, sall_ref, dy_ref, dtb_ref, alog_ref, dsk_ref, ng_ref,
             dc_ref, dug_ref, ddtb_ref, dalog_ref, ddsk_ref, dng_ref, ds_scr):
        @pl.when(pl.program_id(0) == 0)
        def _():
            ds_scr[...] = jnp.zeros_like(ds_scr)
            for o in (ddtb_ref, dalog_ref, ddsk_ref, dng_ref):
                o[...] = jnp.zeros_like(o)

        ds0, ds1 = ds_scr[0], ds_scr[1]
        for k in reversed(range(ncs)):
            rows = slice(k * SSM_CHUNK, (k + 1) * SSM_CHUNK)
            prim = (c_ref[rows, :].astype(F32), ug_ref[rows, :].astype(F32), sall_ref[k, 0], sall_ref[k, 1],
                    dtb_ref[...], alog_ref[...], dsk_ref[...], ng_ref[...])
            _, vjp = jax.vjp(_f_ssd_chunk, *prim)
            dc, dug, ds0, ds1, g_dtb, g_alog, g_dsk, g_ng = vjp((dy_ref[rows, :].astype(F32), ds0, ds1))
            dc_ref[rows, :] = dc
            dug_ref[rows, :] = dug
            ddtb_ref[...] += g_dtb
            dalog_ref[...] += g_alog
            ddsk_ref[...] += g_dsk
            dng_ref[...] += g_ng
        ds_scr[0] = ds0
        ds_scr[1] = ds1

    rev = lambda i: (nt - 1 - i, 0)
    params = (dtb, alog, dsk, ng)
    res = pl.pallas_call(
        body,
        name=name,
        grid=(nt,),
        in_specs=[pl.BlockSpec((tm, c.shape[1]), rev), pl.BlockSpec((tm, ug.shape[1]), rev),
                  pl.BlockSpec((ncs, 2, SLAB, SLAB), lambda i: (nt - 1 - i, 0, 0, 0)), pl.BlockSpec((tm, 256), rev)]
        + [_full_spec(p) for p in params],
        out_specs=[pl.BlockSpec((tm, 512), rev), pl.BlockSpec((tm, U_GATE), rev)] + [_full_spec(p) for p in params],
        out_shape=[jax.ShapeDtypeStruct((T, 512), F32), jax.ShapeDtypeStruct((T, U_GATE), F32)]
        + [jax.ShapeDtypeStruct(p.shape, F32) for p in params],
        scratch_shapes=[pltpu.VMEM((2, SLAB, SLAB), F32)],
        compiler_params=_cparams(("arbitrary",)),
    )(c, ug, sall, dy, *params)
    return res


_CONV_TILE = 512
_HALO = 8
_CONV_W = 4


def _conv_fwd(u, w, b, *, name):
    T, C = u.shape
    tm = min(_CONV_TILE, T)
    hb = tm // _HALO

    def body(u_ref, prev_ref, w_ref, b_ref, y1_ref, y2_ref, ext):
        i = pl.program_id(0)
        ext[0:_HALO, :] = jnp.where(i > 0, prev_ref[...], 0.0)
        ext[_HALO:, :] = u_ref[...]
        y = jnp.broadcast_to(b_ref[...], (tm, C))
        for k in range(_CONV_W):
            y = y + ext[_HALO - (_CONV_W - 1) + k:_HALO - (_CONV_W - 1) + k + tm, :] * w_ref[k:k + 1, :]
        y1_ref[...] = y[:, 0:512]
        y2_ref[...] = y[:, 512:768]

    return pl.pallas_call(
        body,
        name=name,
        grid=(T // tm,),
        in_specs=[_row_spec(u, tm), pl.BlockSpec((_HALO, C), lambda i: (jnp.maximum(i * hb - 1, 0), 0)),
                  _full_spec(w), _full_spec(b)],
        out_specs=[pl.BlockSpec((tm, 512), lambda i: (i, 0)), pl.BlockSpec((tm, 256), lambda i: (i, 0))],
        out_shape=[jax.ShapeDtypeStruct((T, 512), F32), jax.ShapeDtypeStruct((T, 256), F32)],
        scratch_shapes=[pltpu.VMEM((tm + _HALO, C), F32)],
        compiler_params=_cparams(("parallel",)),
    )(u, u, w, b)


def _conv_bwd(u, dy1, dy2, w, *, name):
    T, C = u.shape
    tm = min(_CONV_TILE, T)
    hb = tm // _HALO
    nt = T // tm

    def body(u_ref, prev_ref, dy1_ref, next1_ref, dy2_ref, next2_ref, w_ref, du_ref, dw_ref, db_ref, ext, dext):
        i = pl.program_id(0)
        ext[0:_HALO, :] = jnp.where(i > 0, prev_ref[...], 0.0)
        ext[_HALO:, :] = u_ref[...]
        dext[0:tm, 0:512] = dy1_ref[...]
        dext[0:tm, 512:768] = dy2_ref[...]
        dext[tm:, 0:512] = jnp.where(i < nt - 1, next1_ref[...], 0.0)
        dext[tm:, 512:768] = jnp.where(i < nt - 1, next2_ref[...], 0.0)

        @pl.when(i == 0)
        def _():
            dw_ref[...] = jnp.zeros_like(dw_ref)
            db_ref[...] = jnp.zeros_like(db_ref)

        dy = dext[0:tm, :]
        du = jnp.zeros((tm, C), F32)
        for k in range(_CONV_W):
            du = du + dext[_CONV_W - 1 - k:_CONV_W - 1 - k + tm, :] * w_ref[k:k + 1, :]
            xk = ext[_HALO - (_CONV_W - 1) + k:_HALO - (_CONV_W - 1) + k + tm, :]
            dw_ref[k:k + 1, :] += jnp.sum(dy * xk, axis=0, keepdims=True)
        du_ref[...] = du
        db_ref[...] += jnp.sum(dy, axis=0, keepdims=True)

    nxt = lambda i: (jnp.minimum((i + 1) * hb, T // _HALO - 1), 0)
    return pl.pallas_call(
        body,
        name=name,
        grid=(nt,),
        in_specs=[_row_spec(u, tm), pl.BlockSpec((_HALO, C), lambda i: (jnp.maximum(i * hb - 1, 0), 0)),
                  _row_spec(dy1, tm), pl.BlockSpec((_HALO, 512), nxt), _row_spec(dy2, tm), pl.BlockSpec((_HALO, 256), nxt),
                  _full_spec(w)],
        out_specs=[pl.BlockSpec((tm, C), lambda i: (i, 0)), _full_spec(w), pl.BlockSpec((1, C), lambda i: (0, 0))],
        out_shape=[jax.ShapeDtypeStruct((T, C), F32), jax.ShapeDtypeStruct(w.shape, F32), jax.ShapeDtypeStruct((1, C), F32)],
        scratch_shapes=[pltpu.VMEM((tm + _HALO, C), F32), pltpu.VMEM((tm + _HALO, C), F32)],
        compiler_params=_cparams(("arbitrary",)),
    )(u, u, dy1, dy1, dy2, dy2, w)


_SCAN_TILE = 1024
_SUB = 8


def _shift_rows(x, d, fill, up):
    r = lax.broadcasted_iota(jnp.int32, x.shape, 0)
    if up:
        return jnp.where(r < _SUB - d, pltpu.roll(x, _SUB - d, 0), fill)
    return jnp.where(r >= d, pltpu.roll(x, d, 0), fill)


def _lru_scan_fwd(a, b, *, name):
    T, W = a.shape
    tr = min(_SCAN_TILE, T)

    def body(a_ref, b_ref, h_ref, hp_ref, carry):
        @pl.when(pl.program_id(0) == 0)
        def _():
            carry[...] = jnp.zeros_like(carry)

        def step(t, cr):
            rows = pl.ds(pl.multiple_of(t * _SUB, _SUB), _SUB)
            aa, bb = a_ref[rows, :], b_ref[rows, :]
            for d in (1, 2, 4):
                bb = bb + aa * _shift_rows(bb, d, 0.0, False)
                aa = aa * _shift_rows(aa, d, 1.0, False)
            h = bb + aa * cr
            h_ref[rows, :] = h
            r = lax.broadcasted_iota(jnp.int32, h.shape, 0)
            hp_ref[rows, :] = jnp.where(r >= 1, pltpu.roll(h, 1, 0), cr)
            return jnp.broadcast_to(h[_SUB - 1:_SUB, :], (_SUB, W))

        carry[...] = lax.fori_loop(0, tr // _SUB, step, carry[...])

    return pl.pallas_call(
        body,
        name=name,
        grid=(T // tr,),
        in_specs=[_row_spec(a, tr), _row_spec(b, tr)],
        out_specs=[pl.BlockSpec((tr, W), lambda i: (i, 0))] * 2,
        out_shape=[jax.ShapeDtypeStruct((T, W), F32)] * 2,
        scratch_shapes=[pltpu.VMEM((_SUB, W), F32)],
        compiler_params=_cparams(("arbitrary",)),
    )(a, b)


def _lru_scan_bwd(a, dh, hprev, *, name):
    T, W = a.shape
    tr = min(_SCAN_TILE, T)
    nt = T // tr

    def body(a_ref, dh_ref, hp_ref, g_ref, da_ref, carry):
        @pl.when(pl.program_id(0) == 0)
        def _():
            carry[...] = jnp.zeros_like(carry)

        nsub = tr // _SUB

        def step(s, cr):
            t = nsub - 1 - s
            rows = pl.ds(pl.multiple_of(t * _SUB, _SUB), _SUB)
            a_t = a_ref[rows, :]
            aa = _shift_rows(a_t, 1, 1.0, True)
            bb = dh_ref[rows, :]
            for d in (1, 2, 4):
                bb = bb + aa * _shift_rows(bb, d, 0.0, True)
                aa = aa * _shift_rows(aa, d, 1.0, True)
            g = bb + aa * cr
            g_ref[rows, :] = g
            da_ref[rows, :] = g * hp_ref[rows, :]
            return jnp.broadcast_to(a_t[0:1, :] * g[0:1, :], (_SUB, W))

        carry[...] = lax.fori_loop(0, nsub, step, carry[...])

    rev = lambda i: (nt - 1 - i, 0)
    return pl.pallas_call(
        body,
        name=name,
        grid=(nt,),
        in_specs=[pl.BlockSpec((tr, W), rev)] * 3,
        out_specs=[pl.BlockSpec((tr, W), rev)] * 2,
        out_shape=[jax.ShapeDtypeStruct((T, W), F32)] * 2,
        scratch_shapes=[pltpu.VMEM((_SUB, W), F32)],
        compiler_params=_cparams(("arbitrary",)),
    )(a, dh, hprev)


_ATT_BLK = 512
_ATT_QPARTS = 2
_ATT_BWD_TRIP = 4
_ATT_SCALE = 1.0 / math.sqrt(NOPE + ROPE)
_ATT_SCALE2 = _ATT_SCALE * math.log2(math.e)


def _call_with_exchange(body, send, gather, *, name, grid, in_specs, out_specs, out_shape, args):
    if send is None:
        return pl.pallas_call(body, name=name, grid=grid, in_specs=in_specs, out_specs=out_specs, out_shape=out_shape,
                              compiler_params=_cparams(("parallel", "arbitrary")))(*args)
    n_in, n_out = len(in_specs), len(out_specs)

    def riding(*refs):
        comm = (refs[n_in], refs[n_in + 1 + n_out]) + tuple(refs[n_in + 2 + n_out:])
        h, i = pl.program_id(0), pl.program_id(1)

        @pl.when((h == 0) & (i == 0))
        def _():
            _chip_start(*comm, gather=gather)

        body(*refs[:n_in], *refs[n_in + 1:n_in + 1 + n_out])

        @pl.when((h == grid[0] - 1) & (i == grid[1] - 1))
        def _():
            _chip_wait(*comm, gather=gather)

    shape = (4,) + send.shape if gather else send.shape
    return pl.pallas_call(
        riding, name=name, grid=grid, in_specs=in_specs + [_ANY], out_specs=out_specs + [_ANY],
        out_shape=out_shape + [jax.ShapeDtypeStruct(shape, send.dtype)], scratch_shapes=_CHIP_SEMS,
        compiler_params=_cparams(("arbitrary", "arbitrary")))(*args, send)


def _attn_fwd(q, kv, *, name, send=None):
    T = q.shape[0]
    blk = min(_ATT_BLK, T)
    nq = T // blk
    parts = _ATT_QPARTS if nq % _ATT_QPARTS == 0 else 1

    def body(q_ref, kv_ref, o_ref, lse_ref):
        i = pl.program_id(1)
        def one(part, j0, nblk, carry, masked):
            m, l, acc = carry
            rows = pl.ds(pl.multiple_of(j0 * blk, blk), nblk * blk)
            s = _dot(q_ref[part * blk:(part + 1) * blk, :], kv_ref[rows, 0:SLAB], "nt")
            if masked:
                col = lax.broadcasted_iota(jnp.int32, s.shape, 1)
                row = lax.broadcasted_iota(jnp.int32, s.shape, 0)
                s = jnp.where(col <= row + (nblk - 1) * blk, s, -jnp.inf)
            m_new = jnp.maximum(m, jnp.max(s, axis=-1, keepdims=True))
            alpha = jnp.exp2(m - m_new)
            p = jnp.exp2(s - m_new)
            l = alpha * l + jnp.sum(p, axis=-1, keepdims=True)
            acc = alpha * acc + _dot(p, kv_ref[rows, SLAB:2 * SLAB], "nn")
            return m_new, l, acc

        def step(t, carry):
            return tuple(one(part, t * parts, parts, carry[part], False) for part in range(parts))

        init = ((jnp.full((blk, 1), -jnp.inf, F32), jnp.zeros((blk, 1), F32), jnp.zeros((blk, SLAB), F32)),) * parts
        carry = lax.fori_loop(0, i // 2, lambda t, c: step(2 * t + 1, step(2 * t, c)), init)
        carry = lax.fori_loop((i // 2) * 2, i, step, carry)
        for part in range(parts):
            m, l, acc = one(part, i * parts, part + 1, carry[part], True)
            o_ref[part * blk:(part + 1) * blk, :] = acc / l
            lse_ref[part] = jnp.broadcast_to(m + jnp.log(l) * math.log2(math.e), (blk, SLAB)).T[0:_SUB, :]

    return _call_with_exchange(
        body, send, True,
        name=name,
        grid=(HEADS, nq // parts),
        in_specs=[pl.BlockSpec((parts * blk, SLAB), lambda h, i: (i, h)), pl.BlockSpec((T, 2 * SLAB), lambda h, i: (0, h))],
        out_specs=[pl.BlockSpec((parts * blk, SLAB), lambda h, i: (i, h)),
                   pl.BlockSpec((None, parts, _SUB, blk), lambda h, i: (h, i, 0, 0))],
        out_shape=[jax.ShapeDtypeStruct((T, HEADS * SLAB), F32), jax.ShapeDtypeStruct((HEADS, nq, _SUB, blk), F32)],
        args=(q, kv))


def _attn_delta(do, o, *, name):
    T = o.shape[0]
    blk = min(_ATT_BLK, T)
    nq = T // blk

    def body(do_ref, o_ref, d_ref):
        for h in range(HEADS):
            cols = slice(h * SLAB, (h + 1) * SLAB)
            dl = jnp.sum(do_ref[:, cols].astype(F32) * o_ref[:, cols], axis=-1, keepdims=True)
            d_ref[h] = jnp.broadcast_to(dl, (blk, SLAB)).T[0:_SUB, :]

    return pl.pallas_call(
        body,
        name=name,
        grid=(nq,),
        in_specs=[pl.BlockSpec((blk, HEADS * SLAB), lambda i: (i, 0))] * 2,
        out_specs=pl.BlockSpec((HEADS, None, _SUB, blk), lambda i: (0, i, 0, 0)),
        out_shape=jax.ShapeDtypeStruct((HEADS, nq, _SUB, blk), F32),
        compiler_params=_cparams(("parallel",)),
    )(do, o)


def _attn_bwd(q, kv, do, lse, delta, *, name, send=None):
    T = q.shape[0]
    blk = min(_ATT_BLK, T)
    nq = T // blk

    def body(q_ref, kv_ref, do_ref, lse_ref, dl_ref, dqt_ref, dkv_ref):
        j = pl.program_id(1)

        @pl.when(j == 0)
        def _():
            dqt_ref[...] = jnp.zeros_like(dqt_ref)

        kb, vb = kv_ref[:, 0:SLAB], kv_ref[:, SLAB:2 * SLAB]
        kbt = kb.astype(F32).T.astype(BF16)
        kpos = j * blk + lax.broadcasted_iota(jnp.int32, (blk, blk), 0)

        def step(i, carry, masked):
            dk, dv = carry
            rows = pl.ds(pl.multiple_of(i * blk, blk), blk)
            qb, dob = q_ref[rows, :], do_ref[rows, :]
            st = _dot(kb, qb, "nt")
            if masked:
                qpos = i * blk + lax.broadcasted_iota(jnp.int32, (blk, blk), 1)
                st = jnp.where(kpos <= qpos, st, -jnp.inf)
            pt = jnp.exp2(st - (lse_ref[i, 0:1, :] - math.log2(math.log(2.0))))
            dpt = _dot(vb, dob, "nt")
            dst = pt * (dpt - dl_ref[i, 0:1, :])
            dv = dv + _dot(pt, dob, "nn")
            dk = dk + _dot(dst, qb, "nn")
            dqt_ref[i] += _dot(kbt, dst, "nn")
            return dk, dv

        zero = jnp.zeros((blk, SLAB), F32)
        carry = step(j, (zero, zero), True)
        rest = nq - 1 - j

        def trip(t, c):
            for u in range(_ATT_BWD_TRIP):
                c = step(j + 1 + _ATT_BWD_TRIP * t + u, c, False)
            return c

        carry = lax.fori_loop(0, rest // _ATT_BWD_TRIP, trip, carry)
        tail = j + 1 + (rest // _ATT_BWD_TRIP) * _ATT_BWD_TRIP
        dk, dv = lax.fori_loop(tail, nq, functools.partial(step, masked=False), carry)
        dkv_ref[:, 0:SLAB] = dk.astype(BF16)
        dkv_ref[:, SLAB:2 * SLAB] = (dv * (1.0 / math.log(2.0))).astype(BF16)

    stat_spec = pl.BlockSpec((None, nq, _SUB, blk), lambda h, j: (h, 0, 0, 0))
    head_t_spec = pl.BlockSpec((None, nq, SLAB, blk), lambda h, j: (h, 0, 0, 0))
    return _call_with_exchange(
        body, send, False,
        name=name,
        grid=(HEADS, nq),
        in_specs=[pl.BlockSpec((T, SLAB), lambda h, j: (0, h)), pl.BlockSpec((blk, 2 * SLAB), lambda h, j: (j, h)),
                  pl.BlockSpec((T, SLAB), lambda h, j: (0, h)), stat_spec, stat_spec],
        out_specs=[head_t_spec, pl.BlockSpec((blk, 2 * SLAB), lambda h, j: (j, h))],
        out_shape=[jax.ShapeDtypeStruct((HEADS, nq, SLAB, blk), F32), jax.ShapeDtypeStruct((T, HEADS * 2 * SLAB), BF16)],
        args=(q, kv, do, lse, delta))


def _qrope_bwd(dqt, cq, sq, *, name):
    _, nq, _, blk = dqt.shape
    T = nq * blk

    def body(dqt_ref, c_ref, s_ref, dy_ref):
        c, s = c_ref[...], s_ref[...]
        for h in range(HEADS):
            dq = dqt_ref[h].T
            dy_ref[:, h * SLAB:(h + 1) * SLAB] = (dq * c).astype(BF16)
            dy_ref[:, (HEADS + h) * SLAB:(HEADS + h + 1) * SLAB] = (dq * s).astype(BF16)

    return pl.pallas_call(
        body,
        name=name,
        grid=(nq,),
        in_specs=[pl.BlockSpec((HEADS, None, SLAB, blk), lambda i: (0, i, 0, 0)), _row_spec(cq, blk), _row_spec(sq, blk)],
        out_specs=pl.BlockSpec((blk, 2 * HEADS * SLAB), lambda i: (i, 0)),
        out_shape=jax.ShapeDtypeStruct((T, 2 * HEADS * SLAB), BF16),
        compiler_params=_cparams(("parallel",)),
    )(dqt, cq, sq)


def _loss_head(x, tgt, g, *, name, tm=256):
    T = x.shape[0]
    tm = min(tm, T)

    def body(x_ref, t_ref, g_ref, loss_ref, dx_ref, dxh_ref, dg_ref):
        def f(xv, gv):
            e = _rms(xv, gv, D) - t_ref[...]
            row = jnp.sum(e * e, axis=1, keepdims=True)
            return jnp.sum(row, axis=0, keepdims=True) * (0.5 / D)

        val, vjp = jax.vjp(f, x_ref[...], g_ref[...])
        dxv, dgv = vjp(jnp.ones((1, 1), F32))

        @pl.when(pl.program_id(0) == 0)
        def _():
            loss_ref[...] = jnp.zeros_like(loss_ref)
            dg_ref[...] = jnp.zeros_like(dg_ref)

        dx_ref[...] = dxv
        dxh_ref[...] = dxv.astype(BF16)
        dg_ref[...] += dgv
        loss_ref[...] += jnp.broadcast_to(val, loss_ref.shape)

    return pl.pallas_call(
        body,
        name=name,
        grid=(T // tm,),
        in_specs=[_row_spec(x, tm), _row_spec(tgt, tm), _full_spec(g)],
        out_specs=[pl.BlockSpec((1, SLAB), lambda i: (0, 0)), _row_spec(x, tm), _row_spec(x, tm), _full_spec(g)],
        out_shape=[jax.ShapeDtypeStruct((1, SLAB), F32), jax.ShapeDtypeStruct(x.shape, F32),
                   jax.ShapeDtypeStruct(x.shape, BF16), jax.ShapeDtypeStruct(g.shape, F32)],
        compiler_params=_cparams(("arbitrary",)),
    )(x, tgt, g)


def _row_tile(rows, cols, budget=256 * 1024):
    best = None
    for t in range(16, rows + 1, 16):
        if rows % t == 0 and t * cols <= budget:
            best = t
    return best or rows


def _sum_fixed(x, out_dtype, *, name):
    n, R, C = x.shape
    tr = _row_tile(R, C)

    def body(x_ref, o_ref):
        acc = x_ref[0].astype(F32)
        for k in range(1, n):
            acc = acc + x_ref[k].astype(F32)
        o_ref[...] = acc.astype(o_ref.dtype)

    return pl.pallas_call(
        body,
        name=name,
        grid=(R // tr,),
        in_specs=[pl.BlockSpec((n, tr, C), lambda i: (0, i, 0))],
        out_specs=pl.BlockSpec((tr, C), lambda i: (i, 0)),
        out_shape=jax.ShapeDtypeStruct((R, C), out_dtype),
        compiler_params=_cparams(("parallel",)),
    )(x)


def _adamw(w, g, m, v, *, name):
    R, C = w.shape
    tr = _row_tile(R, C, 128 * 1024)

    def body(w_ref, g_ref, m_ref, v_ref, d_ref, nm_ref, nv_ref):
        gv = g_ref[...]
        mv = B1 * m_ref[...] + (1.0 - B1) * gv
        vv = B2 * v_ref[...] + (1.0 - B2) * (gv * gv)
        m_hat = mv / (1.0 - B1 ** STEP)
        v_hat = vv / (1.0 - B2 ** STEP)
        d_ref[...] = -LR * (m_hat / (jnp.sqrt(v_hat) + AEPS) + WD * w_ref[...])
        nm_ref[...] = mv
        nv_ref[...] = vv

    spec = pl.BlockSpec((tr, C), lambda i: (i, 0))
    return pl.pallas_call(
        body, name=name, grid=(R // tr,), in_specs=[spec] * 4, out_specs=[spec] * 3,
        out_shape=[jax.ShapeDtypeStruct((R, C), F32)] * 3, compiler_params=_cparams(("parallel",)),
    )(w, g, m, v)


_FLIPS = ((1, 0), (0, 1), (1, 1))
_ANY = pl.BlockSpec(memory_space=pl.ANY)


def _me():
    return lax.axis_index("x"), lax.axis_index("y"), lax.axis_index("c")


def _flip(mx, my, f):
    return (1 - mx if f[0] else mx), (1 - my if f[1] else my)


_CHIP_SEMS = [pltpu.SemaphoreType.DMA((3,)), pltpu.SemaphoreType.DMA((3,)), pltpu.SemaphoreType.DMA]


def _chip_copies(x_ref, out_ref, send_sems, recv_sems, local_sem, gather):
    mx, my, mc = _me()
    mine = 2 * mx + my
    outgoing, incoming = [], []
    for k, f in enumerate(_FLIPS):
        px, py = _flip(mx, my, f)
        peer = 2 * px + py
        src = x_ref if gather else x_ref.at[peer]
        for dst, to in ((out_ref.at[mine], outgoing), (out_ref.at[peer], incoming)):
            to.append(pltpu.make_async_remote_copy(src_ref=src, dst_ref=dst, send_sem=send_sems.at[k],
                                                   recv_sem=recv_sems.at[k], device_id=(px, py, mc), device_id_type=MESH))
    local = None if gather else pltpu.make_async_copy(x_ref.at[mine], out_ref.at[mine], local_sem)
    return outgoing, incoming, local


def _chip_start(*refs, gather):
    outgoing, _, local = _chip_copies(*refs, gather)
    if local is not None:
        local.start()
    for cp in outgoing:
        cp.start()


def _chip_wait(*refs, gather):
    outgoing, incoming, local = _chip_copies(*refs, gather)
    for cp in incoming:
        cp.wait_recv()
    for cp in outgoing:
        cp.wait_send()
    if local is not None:
        local.wait()


def _chip_exchange(x, gather, *, name):
    shape = x.shape if not gather else (4,) + x.shape

    def body(*refs):
        _chip_start(*refs, gather=gather)
        _chip_wait(*refs, gather=gather)

    return pl.pallas_call(
        body, name=name, in_specs=[_ANY], out_specs=_ANY, out_shape=jax.ShapeDtypeStruct(shape, x.dtype),
        scratch_shapes=_CHIP_SEMS,
    )(x)


def _core_exchange(x, half, *, name):
    shape = x.shape[1:] if half else x.shape

    def body(x_ref, out_ref, send_sem, recv_sem):
        mx, my, mc = _me()
        src = x_ref.at[1 - mc] if half else x_ref
        cp = pltpu.make_async_remote_copy(src_ref=src, dst_ref=out_ref, send_sem=send_sem, recv_sem=recv_sem,
                                          device_id=(mx, my, 1 - mc), device_id_type=MESH)
        cp.start()
        cp.wait()

    return pl.pallas_call(
        body, name=name, in_specs=[_ANY], out_specs=_ANY, out_shape=jax.ShapeDtypeStruct(shape, x.dtype),
        scratch_shapes=[pltpu.SemaphoreType.DMA, pltpu.SemaphoreType.DMA],
    )(x)


def _core_gather(x, g4, *, name):
    def body(x_ref, g_ref, out_ref, send_sems, recv_sems):
        mx, my, mc = _me()
        srcs = [(x_ref, 2 * mx + my)] + [(g_ref.at[2 * px + py], 2 * px + py) for px, py in (_flip(mx, my, f) for f in _FLIPS)]
        copies = [pltpu.make_async_remote_copy(src_ref=src, dst_ref=out_ref.at[slot], send_sem=send_sems.at[k],
                                               recv_sem=recv_sems.at[k], device_id=(mx, my, 1 - mc), device_id_type=MESH)
                  for k, (src, slot) in enumerate(srcs)]
        for cp in copies:
            cp.start()
        for cp in copies:
            cp.wait()

    return pl.pallas_call(
        body, name=name, in_specs=[_ANY, _ANY], out_specs=_ANY, out_shape=jax.ShapeDtypeStruct(g4.shape, g4.dtype),
        scratch_shapes=[pltpu.SemaphoreType.DMA((4,)), pltpu.SemaphoreType.DMA((4,))],
    )(x, g4)


def _finish_gather(x, g4, *, name):
    mx, my, _ = _me()
    own = (jnp.arange(4) == 2 * mx + my).reshape((4,) + (1,) * x.ndim)
    return jnp.where(own, x[None], g4), _core_gather(x, g4, name=name + "_cores")


def _by_core(mine, other):
    mc = lax.axis_index("c")
    return jnp.stack([jnp.where(mc == 0, mine, other), jnp.where(mc == 0, other, mine)])


def _all_gather(x, *, name):
    return _by_core(*_finish_gather(x, _chip_exchange(x, True, name=name + "_chips"), name=name))


def _add_own_half(x, got, *, name):
    _, R, C = x.shape
    tr = _row_tile(R, C)

    def body(c_ref, x_ref, g_ref, o_ref):
        o_ref[...] = (x_ref[...] + g_ref[...]).astype(o_ref.dtype)

    return pl.pallas_call(
        body,
        name=name,
        grid_spec=pltpu.PrefetchScalarGridSpec(
            num_scalar_prefetch=1, grid=(R // tr,),
            in_specs=[pl.BlockSpec((None, tr, C), lambda i, c: (c[0], i, 0)), pl.BlockSpec((tr, C), lambda i, c: (i, 0))],
            out_specs=pl.BlockSpec((tr, C), lambda i, c: (i, 0))),
        out_shape=jax.ShapeDtypeStruct((R, C), BF16),
        compiler_params=_cparams(("parallel",)),
    )(lax.axis_index("c").astype(jnp.int32).reshape(1), x, got)


def _chip_sums(x, *, name):
    _, _, R, C = x.shape
    got = _core_exchange(x, True, name=name + "_cores")
    return _add_own_half(x.reshape(2, 4 * R, C), got.reshape(4 * R, C), name=name + "_add").reshape(4, R, C)


def _all_reduce(x, *, name):
    g = _all_gather(x, name=name)
    return _sum_fixed(g.reshape((N_DEV,) + x.shape), F32, name=name + "_sum")


WEIGHTS = ['mix_norm_g', 'w_in', 'mla_q_norm_g', 'mla_kv_norm_g', 'mla_w_uq', 'mla_w_ukv', 'mla_out_g', 'ssm_conv_w',
           'ssm_conv_b', 'ssm_dt_bias', 'ssm_a_log', 'ssm_d', 'ssm_norm_g', 'lru_conv_w', 'lru_conv_b', 'lru_w_a',
           'lru_b_a', 'lru_w_i', 'lru_b_i', 'lru_lambda', 'lru_out_g', 'w_out', 'xattn_norm_g', 'mem_norm_g', 'w_mq',
           'w_mk', 'w_mv', 'w_mo', 'mlp_norm_g', 'w_mlp1', 'w_mlp2', 'final_norm_g']
ROW_SHARDED = ('w_in', 'w_out', 'w_mq', 'w_mk', 'w_mv', 'w_mo', 'w_mlp2')
COL_SHARDED = ('mla_w_uq', 'mla_w_ukv', 'w_mlp1')
BIG = tuple(n for n in WEIGHTS if n in ROW_SHARDED + COL_SHARDED)
CONV_SHARDED = ('ssm_conv_w', 'lru_conv_w')
SMALL = tuple(n for n in WEIGHTS if n not in BIG)
PACK_C = 1024


def _pack(arrs, dtype, lead=()):
    flat = jnp.concatenate([a.reshape(lead + (-1,)).astype(dtype) for a in arrs], axis=-1)
    n = flat.shape[-1]
    rows = -(-n // (16 * PACK_C)) * 16
    flat = jnp.pad(flat, [(0, 0)] * len(lead) + [(0, rows * PACK_C - n)])
    return flat.reshape(lead + (rows, PACK_C))


def _unpack(packed, shapes, lead=()):
    flat = packed.reshape(lead + (-1,))
    out, off = [], 0
    for s in shapes:
        n = math.prod(s)
        out.append(flat[..., off:off + n].reshape(lead + tuple(s)))
        off += n
    return out


def _pad_lanes(v, n=SLAB):
    return jnp.pad(v.astype(F32), (0, n - v.shape[0])).reshape(1, n)


PIECES = ('win', 'wq', 'wkv', 'wout', 'w_mq', 'w_mk', 'w_mv', 'w_mo', 'w_mlp1', 'w_mlp2')
PIECE_SOURCE = {'win': 'w_in', 'wq': 'mla_w_uq', 'wkv': 'mla_w_ukv', 'wout': 'w_out'}
PIECE_SHAPE = {'win': (128, 2048), 'wq': (Q_RANK, 2 * SLAB), 'wkv': (KV_RANK, 2 * SLAB), 'wout': (128, D),
               'w_mq': (128, D), 'w_mk': (128, D), 'w_mv': (128, D), 'w_mo': (128, D), 'w_mlp1': (512, D),
               'w_mlp2': (512, D)}
PIECE_COLS = ('wq', 'wkv')


def _k_win(w):
    kr = w[..., 384:416]
    zc = lambda k: jnp.zeros(w.shape[:-1] + (k,), w.dtype)
    return jnp.concatenate(
        [w[..., 0:384], kr, zc(96), kr[..., 16:32], kr[..., 0:16], zc(96),
         w[..., 416:672], w[..., 1444:1700], w[..., 1184:1188], zc(124),
         w[..., 672:1184], w[..., 1188:1444]], axis=-1)


def _k_win_inv(m):
    gk = m[..., 384:416] + jnp.concatenate([m[..., 528:544], m[..., 512:528]], axis=-1)
    return jnp.concatenate([m[..., 0:384], gk, m[..., 640:896], m[..., 1280:1792], m[..., 1152:1156], m[..., 1792:2048],
                            m[..., 896:1152]], axis=-1)


def _k_wq(w):
    nh = w.shape[-1] // (NOPE + ROPE)
    w = w.reshape(w.shape[:-1] + (nh, NOPE + ROPE))
    nope, r1, r2 = w[..., :NOPE], w[..., NOPE:NOPE + 16], w[..., NOPE + 16:]
    z = lambda k: jnp.zeros(w.shape[:-1] + (k,), w.dtype)
    both = jnp.stack([jnp.concatenate([nope, r1, r2, z(32)], -1), jnp.concatenate([z(64), r2, r1, z(32)], -1)], axis=-3)
    return both.reshape(w.shape[:-2] + (2 * nh * SLAB,))


def _k_wq_inv(m):
    nh = m.shape[-1] // (2 * SLAB)
    m = m.reshape(m.shape[:-1] + (2, nh, SLAB))
    q0, q1 = m[..., 0, :, :], m[..., 1, :, :]
    w = jnp.concatenate([q0[..., :64], q0[..., 64:80] + q1[..., 80:96], q0[..., 80:96] + q1[..., 64:80]], -1)
    return w.reshape(w.shape[:-2] + (nh * (NOPE + ROPE),))


def _k_wkv(w):
    nh = w.shape[-1] // (NOPE + VDIM)
    w = w.reshape(w.shape[:-1] + (nh, NOPE + VDIM))
    z = jnp.zeros(w.shape[:-1] + (64,), w.dtype)
    return jnp.concatenate([w[..., :NOPE], z, w[..., NOPE:], z], -1).reshape(w.shape[:-2] + (nh * 2 * SLAB,))


def _k_wkv_inv(m):
    nh = m.shape[-1] // (2 * SLAB)
    m = m.reshape(m.shape[:-1] + (nh, 2 * SLAB))
    return jnp.concatenate([m[..., :NOPE], m[..., SLAB:SLAB + VDIM]], -1).reshape(m.shape[:-2] + (nh * (NOPE + VDIM),))


def _k_swap(w):
    return jnp.swapaxes(w, -1, -2)


_K_FWD = {'win': _k_win, 'wq': _k_wq, 'wkv': _k_wkv, 'w_mlp1': _k_swap}
_K_INV = {'win': _k_win_inv, 'wq': _k_wq_inv, 'wkv': _k_wkv_inv, 'w_mlp1': _k_swap}


def _assemble(piece, g):
    _, _, a, b = g.shape
    if piece == 'wq':
        return g.reshape(2, 4, a, 2, SLAB).transpose(2, 3, 1, 0, 4).reshape(a, N_DEV * b)
    if piece in PIECE_COLS:
        return g.transpose(2, 1, 0, 3).reshape(a, N_DEV * b)
    return g.transpose(1, 0, 2, 3).reshape(N_DEV * a, b)


def _disassemble(piece, full):
    a, b = PIECE_SHAPE[piece]
    if piece == 'wq':
        return full.reshape(a, 2, 4, 2, SLAB).transpose(3, 2, 0, 1, 4).reshape(2, 4, a, b)
    if piece in PIECE_COLS:
        return full.reshape(a, 4, 2, b).transpose(2, 1, 0, 3)
    return full.reshape(4, 2, a, b).transpose(1, 0, 2, 3)


def _piece_rows(piece):
    a, b = PIECE_SHAPE[piece]
    return a * b // PACK_C


def _prep_layer(pieces, Ws, l):
    P = dict(pieces)
    ri, ci = jnp.arange(SLAB)[:, None], jnp.arange(2 * SLAB)[None, :]
    sel = ((ri < ROPE) & (ci == ri + NOPE)).astype(P['wkv'].dtype)
    P['wkv'] = jnp.concatenate([P['wkv'], jnp.tile(sel, (1, HEADS))], axis=0)
    wout = P['wout']
    mla_rows = jnp.pad(wout[:HEADS * VDIM].reshape(HEADS, VDIM, D), ((0, 0), (0, SLAB - VDIM), (0, 0)))
    P['wout'] = jnp.concatenate([mla_rows.reshape(HEADS * SLAB, D), wout[HEADS * VDIM:]], axis=0)
    W = Ws
    row = lambda n: W[n][l].astype(F32).reshape(1, -1)
    for n in ('mix_norm_g', 'mla_q_norm_g', 'mla_kv_norm_g', 'ssm_norm_g', 'lru_lambda', 'lru_out_g', 'xattn_norm_g',
              'mem_norm_g', 'mlp_norm_g'):
        P[n] = row(n)
    P['mla_out_g'] = jnp.pad(W['mla_out_g'][l].astype(F32).reshape(HEADS, VDIM), ((0, 0), (0, SLAB - VDIM))).reshape(1, -1)
    for n in ('ssm_dt_bias', 'ssm_a_log', 'ssm_d'):
        P[n] = _pad_lanes(W[n][l])
    P['conv_w'] = jnp.pad(jnp.concatenate([W['ssm_conv_w'][l], W['lru_conv_w'][l]], axis=1).astype(F32), ((0, 4), (0, 0)))
    P['conv_b'] = jnp.concatenate([W['ssm_conv_b'][l], W['lru_conv_b'][l]]).astype(F32).reshape(1, -1)
    for n in ('lru_w_a', 'lru_w_i'):
        P[n] = jnp.concatenate([jnp.pad(W[n][l, k].astype(F32), ((0, 0), (64 * k, 192 - 64 * k))) for k in range(4)], axis=0)
    for n in ('lru_b_a', 'lru_b_i'):
        P[n] = W[n][l].astype(F32).reshape(1, -1)
    return P


def _unprep_pieces(G):
    o = {n: G[n] for n in PIECES}
    o['wkv'] = G['wkv'][:KV_RANK]
    wo = G['wout']
    o['wout'] = jnp.concatenate([wo[:HEADS * SLAB].reshape(HEADS, SLAB, D)[:, :VDIM].reshape(HEADS * VDIM, D),
                                 wo[HEADS * SLAB:]], axis=0)
    return o


def _unprep_small(G):
    o = {}
    for n in ('mix_norm_g', 'mla_q_norm_g', 'mla_kv_norm_g', 'ssm_norm_g', 'lru_lambda', 'lru_out_g', 'xattn_norm_g',
              'mem_norm_g', 'mlp_norm_g', 'lru_b_a', 'lru_b_i'):
        o[n] = G[n].reshape(-1)
    o['lru_b_a'] = o['lru_b_a'].reshape(4, 64)
    o['lru_b_i'] = o['lru_b_i'].reshape(4, 64)
    o['mla_out_g'] = G['mla_out_g'].reshape(HEADS, SLAB)[:, :VDIM].reshape(-1)
    for n in ('ssm_dt_bias', 'ssm_a_log', 'ssm_d'):
        o[n] = G[n][0, :4]
    o['ssm_conv_w'], o['lru_conv_w'] = G['conv_w'][:4, :512], G['conv_w'][:4, 512:]
    o['ssm_conv_b'], o['lru_conv_b'] = G['conv_b'][0, :512], G['conv_b'][0, 512:]
    for n in ('lru_w_a', 'lru_w_i'):
        o[n] = jnp.stack([G[n][64 * k:64 * (k + 1), 64 * k:64 * (k + 1)] for k in range(4)])
    return o


def _rope_tables(positions):
    half = ROPE // 2
    inv_freq = ROPE_THETA ** (-jnp.arange(half, dtype=F32) * 2.0 / ROPE)
    ang = positions.astype(F32)[:, None] * inv_freq
    cos, sin = jnp.cos(ang), jnp.sin(ang)
    T = positions.shape[0]
    z = lambda k: jnp.zeros((T, k), F32)
    ck = jnp.concatenate([cos, cos, z(96)], axis=1)
    sk = jnp.concatenate([-sin, sin, z(96)], axis=1)
    cq = jnp.concatenate([jnp.ones((T, NOPE), F32), cos, cos, z(32)], axis=1)
    sq = jnp.concatenate([z(NOPE), -sin, sin, z(32)], axis=1)
    return ck, sk, cq * _ATT_SCALE2, sq * _ATT_SCALE2


def _add_epi(acc, res):
    return (acc + res,)


def _add_norm_epi(acc, res, g):
    x = acc + res
    return x, _rms(x, g, x.shape[-1])


def _norm_bwd_epi(acc, x, res, g):
    _, vjp = jax.vjp(lambda xv, gv: _rms(xv, gv, xv.shape[-1]), x, g)
    dx, dg = vjp(acc)
    dx = dx + res
    return dx, dx, dg


def _relu2_epi(acc):
    r = jnp.maximum(acc, 0.0)
    return r, r * r


def _drelu2_epi(acc, r):
    return (acc * (2.0 * r.astype(F32)),)


def _norm(x, g, name):
    return _rows_fwd(_f_norm, [x], [g], [(x.shape[1], BF16)], name=name)[0]


def _norm_bwd(x, g, ct, add, name):
    (dx, dx16), (dg,) = _rows_vjp(_f_norm, [x], [g], [ct], name=name, drows=[0], dparams=[0], drow_dtypes=[F32],
                                  add=add, twin=True)
    return dx, dx16, dg


def _layer_fwd(x0, h1, mem, P, tabs, g_next=None, send=None):
    ck, sk, cq, sq = tabs
    S = {'x0': x0}
    if h1 is None:
        h1 = _norm(x0, P['mix_norm_g'], "norm_mix")
    S['h1'] = h1
    win = P['win']
    u_mla = S['u_mla'] = _mm(h1, win[:, 0:U_MLA], name="in_mla")
    u_gate = S['u_gate'] = _mm(h1, win[:, U_MLA:U_MLA + U_GATE], name="in_gate")
    u_conv = S['u_conv'] = _mm(h1, win[:, U_MLA + U_GATE:], name="in_conv")
    cqn, akv = _rows_fwd(_f_mla_prep, [u_mla, ck, sk], [P['mla_q_norm_g'], P['mla_kv_norm_g']],
                         [(Q_RANK, BF16), (2 * SLAB, BF16)], name="mla_prep")
    S['cqn'], S['akv'] = cqn, akv
    yq = _mm(cqn, P['wq'], name="q_proj")
    q = S['q'] = _rows_fwd(_f_qrope, [yq, cq, sq], [], [(HEADS * SLAB, BF16)], name="q_rope")[0]
    kv = S['kv'] = _mm(akv, P['wkv'], name="kv_proj", out_dtypes=(BF16,))
    o, lse, *got = _attn_fwd(q, kv, name="attn_fwd" if send is None else "attn_fwd_gather", send=send)
    S['o'], S['lse'] = o, lse
    c_ssm, c_lru = _conv_fwd(u_conv, P['conv_w'], P['conv_b'], name="conv_fwd")
    S['c_ssm'], S['c_lru'] = c_ssm, c_lru
    ys, sall = _ssd_fwd(c_ssm, u_gate, P['ssm_dt_bias'], P['ssm_a_log'], P['ssm_d'], P['ssm_norm_g'], name="ssd_fwd")
    S['ys'], S['sall'] = ys, sall
    a, b = _rows_fwd(_f_lru_gates, [c_lru], [P['lru_w_a'], P['lru_b_a'], P['lru_w_i'], P['lru_b_i'], P['lru_lambda']],
                     [(256, F32), (256, F32)], name="lru_gates", tm=_MM_ROWS)
    h, hprev = _lru_scan_fwd(a, b, name="lru_scan")
    S['a'], S['h'], S['hprev'] = a, h, hprev
    ymix = S['ymix'] = _rows_fwd(_f_mix, [o, ys, h, u_gate], [P['mla_out_g'], P['lru_out_g']],
                                 [(HEADS * SLAB + 512, BF16)], name="mix")[0]
    x1, hx = _mm(ymix, P['wout'], name="out_proj", epi=_add_norm_epi, extras=(x0, P['xattn_norm_g']), out_dtypes=(F32, BF16))
    S['x1'], S['hx'] = x1, hx
    qx = S['qx'] = _mm(hx, P['w_mq'], name="mem_q", out_dtypes=(BF16,))
    mn = S['mn'] = _norm(mem, P['mem_norm_g'], "norm_mem")
    kx = S['kx'] = _mm(mn, P['w_mk'], name="mem_k", out_dtypes=(BF16,))
    vx = S['vx'] = _mm(mn, P['w_mv'], name="mem_v", out_dtypes=(BF16,))
    ox = S['ox'] = _rows_fwd(_f_xattn, [qx], [kx, vx], [(D, BF16)], name="xattn", tm=_MM_ROWS)[0]
    x2, hm = _mm(ox, P['w_mo'], name="mem_o", epi=_add_norm_epi, extras=(x1, P['mlp_norm_g']), out_dtypes=(F32, BF16))
    S['x2'], S['hm'] = x2, hm
    r, s = _mm(hm, P['w_mlp1'], "nt", name="mlp_up", epi=_relu2_epi, out_dtypes=(BF16, BF16))
    S['r'], S['s'] = r, s
    if g_next is None:
        x3, h_next = _mm(s, P['w_mlp2'], name="mlp_down_last", epi=_add_epi, extras=(x2,)), None
    else:
        x3, h_next = _mm(s, P['w_mlp2'], name="mlp_down", epi=_add_norm_epi, extras=(x2, g_next), out_dtypes=(F32, BF16))
    return x3, h_next, S, (got[0] if got else None)


def _layer_bwd(dx3, dx3h, mem, S, P, tabs, send=None):
    ck, sk, cq, sq = tabs
    G = {}
    da = _mm(dx3h, P['w_mlp2'], "nt", name="mlp_down_dx", epi=_drelu2_epi, extras=(S['r'],), out_dtypes=(BF16,))
    G['w_mlp2'] = _mm(S['s'], dx3h, "tn", name="mlp_down_dw")
    G['w_mlp1'] = _mm(da, S['hm'], "tn", name="mlp_up_dw")
    norm_out = dict(epi=_norm_bwd_epi, out_dtypes=(F32, BF16), col_sums=1)
    dx2, dx2h, G['mlp_norm_g'] = _mm(da, P['w_mlp1'], "nn", name="mlp_up_dx", extras=(S['x2'], dx3, P['mlp_norm_g']),
                                     **norm_out)
    dox = _mm(dx2h, P['w_mo'], "nt", name="mem_o_dx")
    G['w_mo'] = _mm(S['ox'], dx2h, "tn", name="mem_o_dw")
    (dqx,), (dkx, dvx) = _rows_vjp(_f_xattn, [S['qx']], [S['kx'], S['vx']], [dox], name="xattn_bwd", drows=[0],
                                   dparams=[0, 1], drow_dtypes=[BF16], tm=_MM_ROWS)
    G['w_mq'] = _mm(S['hx'], dqx, "tn", name="mem_q_dw")
    dx1, dx1h, G['xattn_norm_g'] = _mm(dqx, P['w_mq'], "nt", name="mem_q_dx", extras=(S['x1'], dx2, P['xattn_norm_g']),
                                       **norm_out)
    G['w_mk'] = _mm(S['mn'], dkx, "tn", name="mem_k_dw")
    G['w_mv'] = _mm(S['mn'], dvx, "tn", name="mem_v_dw")
    dmn = _mm(dkx, P['w_mk'], "nt", name="mem_k_dx", epi=_add_epi, extras=(_mm(dvx, P['w_mv'], "nt", name="mem_v_dx"),))
    _, _, G['mem_norm_g'] = _norm_bwd(mem, P['mem_norm_g'], dmn, None, "norm_mem_bwd")
    dymix = _mm(dx1h, P['wout'], "nt", name="out_proj_dx")
    G['wout'] = _mm(S['ymix'], dx1h, "tn", name="out_proj_dw")
    (do, dys, dh, dug_mix), (G['mla_out_g'], G['lru_out_g']) = _rows_vjp(
        _f_mix, [S['o'], S['ys'], S['h'], S['u_gate']], [P['mla_out_g'], P['lru_out_g']], [dymix], name="mix_bwd",
        drows=[0, 1, 2, 3], dparams=[0, 1], drow_dtypes=[BF16, F32, F32, F32])
    g, da_lru = _lru_scan_bwd(S['a'], dh, S['hprev'], name="lru_scan_bwd")
    lru_par = [P['lru_w_a'], P['lru_b_a'], P['lru_w_i'], P['lru_b_i'], P['lru_lambda']]
    (dc_lru,), dpar = _rows_vjp(_f_lru_gates, [S['c_lru']], lru_par, [da_lru, g], name="lru_gates_bwd", drows=[0],
                                dparams=[0, 1, 2, 3, 4], drow_dtypes=[F32], tm=_MM_ROWS)
    G['lru_w_a'], G['lru_b_a'], G['lru_w_i'], G['lru_b_i'], G['lru_lambda'] = dpar
    dc_ssm, dug_ssd, G['ssm_dt_bias'], G['ssm_a_log'], G['ssm_d'], G['ssm_norm_g'] = _ssd_bwd(
        S['c_ssm'], S['u_gate'], S['sall'], dys, P['ssm_dt_bias'], P['ssm_a_log'], P['ssm_d'], P['ssm_norm_g'],
        name="ssd_bwd")
    du_conv, G['conv_w'], G['conv_b'] = _conv_bwd(S['u_conv'], dc_ssm, dc_lru, P['conv_w'], name="conv_bwd")
    delta = _attn_delta(do, S['o'], name="attn_delta")
    dqt, dkv, *got = _attn_bwd(S['q'], S['kv'], do, S['lse'], delta,
                               name="attn_bwd" if send is None else "attn_bwd_scatter", send=send)
    dyq = _qrope_bwd(dqt, cq, sq, name="q_rope_bwd")
    dcqn = _mm(dyq, P['wq'], "nt", name="q_proj_dx")
    G['wq'] = _mm(S['cqn'], dyq, "tn", name="q_proj_dw")
    dakv = _mm(dkv, P['wkv'], "nt", name="kv_proj_dx")
    G['wkv'] = _mm(S['akv'], dkv, "tn", name="kv_proj_dw")
    (du_mla,), (G['mla_q_norm_g'], G['mla_kv_norm_g']) = _rows_vjp(
        _f_mla_prep, [S['u_mla'], ck, sk], [P['mla_q_norm_g'], P['mla_kv_norm_g']], [dcqn, dakv], name="mla_prep_bwd",
        drows=[0], dparams=[0, 1], drow_dtypes=[BF16])
    du = jnp.concatenate([du_mla, (dug_mix + dug_ssd).astype(BF16), du_conv.astype(BF16)], axis=1)
    G['win'] = _mm(S['h1'], du, "tn", name="in_dw")
    dx0, dx0h, G['mix_norm_g'] = _mm(du, P['win'], "nt", name="in_dx", extras=(S['x0'], dx1, P['mix_norm_g']), **norm_out)
    return dx0, dx0h, G, (got[0] if got else None)


class _NoExchange:
    def __init__(self, layers):
        self.layers = layers

    def pieces(self, l):
        return self.layers[l]

    def fwd_send(self, l):
        return None

    def fwd_got(self, l, got):
        pass

    def bwd_send(self, l):
        return None

    def bwd_got(self, l, got):
        pass

    def grads_ready(self, l, pieces):
        pass


def _local_step(x, mem, positions, ex, Ws, tgt):
    tabs = _rope_tables(positions)
    saved, preps, h = [], [], None
    for l in range(DEPTH):
        P = _prep_layer(ex.pieces(l), Ws, l)
        send = ex.fwd_send(l)
        g_next = Ws['mix_norm_g'][l + 1].astype(F32).reshape(1, D) if l + 1 < DEPTH else None
        x, h, S, got = _layer_fwd(x, h, mem, P, tabs, g_next, send)
        if send is not None:
            ex.fwd_got(l, got)
        saved.append(S)
        preps.append(P)
    loss, dx, dxh, dg_final = _loss_head(x, tgt, Ws['final_norm_g'].astype(F32).reshape(1, D), name="loss_head")
    pieces, small = [None] * DEPTH, [None] * DEPTH
    for l in reversed(range(DEPTH)):
        send = ex.bwd_send(l)
        dx, dxh, G, got = _layer_bwd(dx, dxh, mem, saved[l], preps[l], tabs, send)
        if send is not None:
            ex.bwd_got(l, got)
        pieces[l], small[l] = _unprep_pieces(G), _unprep_small(G)
        ex.grads_ready(l, pieces[l])
    grads = {n: jnp.stack([small[l][n] for l in range(DEPTH)]) for n in SMALL if n != 'final_norm_g'}
    grads['final_norm_g'] = dg_final.reshape(D)
    return loss, dx, pieces, grads


def _pack_rows(pieces):
    return jnp.concatenate([pieces[n].reshape(pieces[n].shape[:-2] + (-1, PACK_C)) for n in PIECES], axis=-2)


def _unpack_rows(packed, lead=()):
    out, off = {}, 0
    for n in PIECES:
        rows = _piece_rows(n)
        out[n] = packed[..., off:off + rows, :].reshape(lead + PIECE_SHAPE[n])
        off += rows
    return out


class _StepExchange(_NoExchange):
    def __init__(self, shard):
        self.shard = {n: a.astype(BF16) for n, a in shard.items()}
        self.layers = {}
        self.sums, self.reduced = {}, {}
        first = _pack_rows({n: a[0] for n, a in self.shard.items()})
        self._take(*_finish_gather(first, _chip_exchange(first, True, name="gather_w0_chips"), name="gather_w0"), [0])

    def _take(self, mine, other, layers):
        mine, other = (g.reshape(4, len(layers), -1, PACK_C) for g in (mine, other))
        for k, l in enumerate(layers):
            a, b = _unpack_rows(mine[:, k], lead=(4,)), _unpack_rows(other[:, k], lead=(4,))
            self.layers[l] = {n: _assemble(n, _by_core(a[n], b[n])) for n in PIECES}

    def _rest(self):
        return _pack_rows({n: a[1:] for n, a in self.shard.items()}).reshape(-1, PACK_C)

    def fwd_send(self, l):
        return self._rest() if l == 0 else None

    def fwd_got(self, l, got):
        self._take(*_finish_gather(self._rest(), got, name="gather_w"), list(range(1, DEPTH)))

    def grads_ready(self, l, pieces):
        x = _pack_rows({n: _disassemble(n, pieces[n]) for n in PIECES})
        self.sums[l] = _chip_sums(x, name="scatter_g")
        if l == 0:
            self.bwd_got(-1, _chip_exchange(self.sums.pop(0), False, name="scatter_g_chips"))

    def bwd_send(self, l):
        return self.sums.pop(l + 1, None)

    def bwd_got(self, l, got):
        self.reduced[l + 1] = _sum_fixed(got, F32, name="scatter_g_sum")


def _adamw_nd(w, g, m, v, name):
    shp = w.shape
    two = lambda a: a.reshape(-1, shp[-1])
    return [r.reshape(shp) for r in _adamw(two(w), two(g), two(m), two(v), name=name)]


def kernel(x, mem, positions, mix_norm_g, w_in, mla_q_norm_g, mla_kv_norm_g, mla_w_uq, mla_w_ukv, mla_out_g, ssm_conv_w, ssm_conv_b, ssm_dt_bias, ssm_a_log, ssm_d, ssm_norm_g, lru_conv_w, lru_conv_b, lru_w_a, lru_b_a, lru_w_i, lru_b_i, lru_lambda, lru_out_g, w_out, xattn_norm_g, mem_norm_g, w_mq, w_mk, w_mv, w_mo, mlp_norm_g, w_mlp1, w_mlp2, final_norm_g, loss_target, m_mix_norm_g, m_w_in, m_mla_q_norm_g, m_mla_kv_norm_g, m_mla_w_uq, m_mla_w_ukv, m_mla_out_g, m_ssm_conv_w, m_ssm_conv_b, m_ssm_dt_bias, m_ssm_a_log, m_ssm_d, m_ssm_norm_g, m_lru_conv_w, m_lru_conv_b, m_lru_w_a, m_lru_b_a, m_lru_w_i, m_lru_b_i, m_lru_lambda, m_lru_out_g, m_w_out, m_xattn_norm_g, m_mem_norm_g, m_w_mq, m_w_mk, m_w_mv, m_w_mo, m_mlp_norm_g, m_w_mlp1, m_w_mlp2, m_final_norm_g, v_mix_norm_g, v_w_in, v_mla_q_norm_g, v_mla_kv_norm_g, v_mla_w_uq, v_mla_w_ukv, v_mla_out_g, v_ssm_conv_w, v_ssm_conv_b, v_ssm_dt_bias, v_ssm_a_log, v_ssm_d, v_ssm_norm_g, v_lru_conv_w, v_lru_conv_b, v_lru_w_a, v_lru_b_a, v_lru_w_i, v_lru_b_i, v_lru_lambda, v_lru_out_g, v_w_out, v_xattn_norm_g, v_mem_norm_g, v_w_mq, v_w_mk, v_w_mv, v_w_mo, v_mlp_norm_g, v_w_mlp1, v_w_mlp2, v_final_norm_g):
    a = locals()
    w = {n: a[n] for n in WEIGHTS}
    m = {n: a['m_' + n] for n in WEIGHTS}
    v = {n: a['v_' + n] for n in WEIGHTS}
    me = 4 * lax.axis_index("x") + 2 * lax.axis_index("y") + lax.axis_index("c")

    ex = _StepExchange({n: _K_FWD.get(n, lambda a: a)(w[PIECE_SOURCE.get(n, n)]) for n in PIECES})
    Ws = {}
    conv_shapes = [w[n].shape for n in CONV_SHARDED]
    conv_g = _all_gather(_pack([w[n] for n in CONV_SHARDED], F32), name="gather_conv")
    conv_g = conv_g.transpose(1, 0, 2, 3).reshape((N_DEV,) + conv_g.shape[2:])
    for n, g in zip(CONV_SHARDED, _unpack(conv_g, conv_shapes, lead=(N_DEV,))):
        Ws[n] = g.transpose(1, 2, 0, 3).reshape(g.shape[1], g.shape[2], N_DEV * g.shape[3])
    for n in SMALL:
        if n not in CONV_SHARDED:
            Ws[n] = w[n]

    loss_share, dx, _, grads = _local_step(x[0], mem[0], positions[0], ex, Ws, loss_target[0])
    loss = lax.psum(loss_share[0, 0], ("x", "y", "c"))

    g_out = {}
    for n in PIECES:
        g = jnp.stack([_unpack_rows(ex.reduced[l])[n] for l in range(DEPTH)])
        g_out[PIECE_SOURCE.get(n, n)] = _K_INV[n](g) if n in _K_INV else g
    small_shapes = [grads[n].shape for n in SMALL]
    g_small = _all_reduce(_pack([grads[n] for n in SMALL], F32), name="reduce_g")
    for n, g in zip(SMALL, _unpack(g_small, small_shapes)):
        if n in CONV_SHARDED:
            cols = w[n].shape[-1]
            g = lax.dynamic_slice_in_dim(g, me * cols, cols, axis=2)
        g_out[n] = g

    delta, new_m, new_v = {}, {}, {}
    for n in BIG:
        delta[n], new_m[n], new_v[n] = _adamw_nd(w[n], g_out[n], m[n], v[n], "adamw_" + n)
    shapes = [w[n].shape for n in SMALL]
    packed = [_pack([d[n] for n in SMALL], F32) for d in (w, g_out, m, v)]
    for d, res in zip((delta, new_m, new_v), _adamw(*packed, name="adamw_small")):
        d.update(zip(SMALL, _unpack(res, shapes)))

    return (loss, dx[None], *[g_out[n] for n in WEIGHTS], *[delta[n] for n in WEIGHTS],
            *[new_m[n] for n in WEIGHTS], *[new_v[n] for n in WEIGHTS])
```

```python
import functools
import math

import jax
import jax.numpy as jnp
from jax import lax
from jax.experimental import pallas as pl
from jax.experimental.pallas import tpu as pltpu

F32, BF16 = jnp.float32, jnp.bfloat16

D = 1024
DEPTH = 4
N_MEM = 256
EPS = 1e-6
HEADS = 8
NOPE, ROPE, VDIM = 64, 32, 64
Q_RANK, KV_RANK = 256, 128
ROPE_THETA = 10000.0
SSM_CHUNK = 128
LRU_C = 8.0
MEM_HEADS = 4
D_FF = 4 * D
SLAB = 128
LR, B1, B2, AEPS, WD, STEP = 0.001, 0.9, 0.999, 1e-08, 0.01, 10

N_DEV = 8
MESH = pl.DeviceIdType.MESH

U_MLA = 640
U_GATE = 640
U_CONV = 768

_DN = {"nn": (((1,), (0,)), ((), ())), "nt": (((1,), (1,)), ((), ())), "tn": (((0,), (0,)), ((), ()))}


def _dot(a, b, kind):
    return lax.dot_general(a.astype(BF16), b.astype(BF16), _DN[kind], preferred_element_type=F32)


@functools.partial(jax.custom_vjp, nondiff_argnums=(2,))
def _bdot(a, b, kind):
    return _dot(a, b, kind)


def _bdot_fwd(a, b, kind):
    return _dot(a, b, kind), (a, b)


def _bdot_bwd(kind, res, g):
    a, b = res
    if kind == "nn":
        da, db = _dot(g, b, "nt"), _dot(a, g, "tn")
    elif kind == "nt":
        da, db = _dot(g, b, "nn"), _dot(g, a, "tn")
    else:
        da, db = _dot(b, g, "nt"), _dot(a, g, "nn")
    return da.astype(a.dtype), db.astype(b.dtype)


_bdot.defvjp(_bdot_fwd, _bdot_bwd)


def _tile(n, pref):
    if n <= pref:
        return n
    t = pref
    while n % t:
        t -= SLAB
    return t


def _cparams(sem, vmem_mb=48):
    return pltpu.CompilerParams(dimension_semantics=sem, vmem_limit_bytes=vmem_mb * 1024 * 1024)


def _mm(a, b, kind="nn", *, name, out_dtypes=(F32,), epi=None, extras=(), col_sums=0, tm=1024, tn=1024, tk=1024):
    if kind == "tn":
        K, M = a.shape
    else:
        M, K = a.shape
    N = b.shape[0] if kind == "nt" else b.shape[1]
    if a.dtype == F32 or b.dtype == F32:
        tk = tk // 2
    tm, tn, tk = _tile(M, tm), _tile(N, tn), _tile(K, tk)
    assert not col_sums or tn == N, "column sums need one tile across the columns"
    nk = K // tk
    a_spec = pl.BlockSpec((tk, tm), lambda i, j, k: (k, i)) if kind == "tn" else pl.BlockSpec((tm, tk), lambda i, j, k: (i, k))
    b_spec = pl.BlockSpec((tn, tk), lambda i, j, k: (j, k)) if kind == "nt" else pl.BlockSpec((tk, tn), lambda i, j, k: (k, j))
    o_spec = pl.BlockSpec((tm, tn), lambda i, j, k: (i, j))
    vec_spec = pl.BlockSpec((1, tn), lambda i, j, k: (0, j))
    ex_specs = [vec_spec if e.shape[0] == 1 else o_spec for e in extras]
    n_ex, n_out = len(extras), len(out_dtypes)

    def body(*refs):
        a_ref, b_ref = refs[:2]
        ex = refs[2:2 + n_ex]
        outs = refs[2 + n_ex:2 + n_ex + n_out]
        acc = refs[-1]
        k = pl.program_id(2)
        first_row_tile = pl.program_id(0) == 0

        def finish(r):
            res = epi(r, *[e[...] for e in ex]) if epi is not None else (r,)
            for o, v in zip(outs, res):
                o[...] = v.astype(o.dtype)
            sums = refs[2 + n_ex + n_out:2 + n_ex + n_out + col_sums]

            @pl.when(first_row_tile)
            def _():
                for o in sums:
                    o[...] = jnp.zeros_like(o)

            for o, v in zip(sums, res[n_out:]):
                o[...] += v

        if nk == 1:
            finish(_dot(a_ref[...], b_ref[...], kind))
            return

        @pl.when(k == 0)
        def _():
            acc[...] = _dot(a_ref[...], b_ref[...], kind)

        @pl.when(k > 0)
        def _():
            acc[...] += _dot(a_ref[...], b_ref[...], kind)

        @pl.when(k == nk - 1)
        def _():
            finish(acc[...])

    res = pl.pallas_call(
        body,
        name=name,
        grid=(M // tm, N // tn, nk),
        in_specs=[a_spec, b_spec] + ex_specs,
        out_specs=[o_spec] * n_out + [vec_spec] * col_sums,
        out_shape=[jax.ShapeDtypeStruct((M, N), dt) for dt in out_dtypes] + [jax.ShapeDtypeStruct((1, N), F32)] * col_sums,
        scratch_shapes=[pltpu.VMEM((tm, tn), F32)],
        compiler_params=_cparams(("arbitrary" if col_sums else "parallel", "parallel", "arbitrary")),
    )(a, b, *extras)
    return res[0] if n_out + col_sums == 1 else res


_MM_ROWS = 1024


def _row_spec(arr, tm):
    return pl.BlockSpec((tm, arr.shape[1]), lambda i: (i, 0))


def _full_spec(arr):
    nd = arr.ndim
    return pl.BlockSpec(arr.shape, lambda i: (0,) * nd)


def _rows_fwd(fn, rows, params, outs, *, name, tm=256):
    T = rows[0].shape[0]
    tm = min(tm, T)
    nr, npar = len(rows), len(params)

    def body(*refs):
        ins = [r[...] for r in refs[:nr + npar]]
        res = fn(*ins)
        for o, v in zip(refs[nr + npar:], res):
            o[...] = v.astype(o.dtype)

    res = pl.pallas_call(
        body,
        name=name,
        grid=(T // tm,),
        in_specs=[_row_spec(r, tm) for r in rows] + [_full_spec(p) for p in params],
        out_specs=[pl.BlockSpec((tm, c), lambda i: (i, 0)) for c, _ in outs],
        out_shape=[jax.ShapeDtypeStruct((T, c), dt) for c, dt in outs],
        compiler_params=_cparams(("parallel",)),
    )(*rows, *params)
    return res


def _rows_vjp(fn, rows, params, cts, *, name, drows, dparams, drow_dtypes, add=None, twin=False, tm=256):
    T = rows[0].shape[0]
    tm = min(tm, T)
    nr, npar, nct = len(rows), len(params), len(cts)
    n_add = 0 if add is None else 1
    n_dr, n_dp = len(drows), len(dparams)
    n_tw = 1 if twin else 0

    def body(*refs):
        row_t = [r[...] for r in refs[:nr]]
        par_t = [r[...] for r in refs[nr:nr + npar]]
        ct_t = [r[...].astype(F32) for r in refs[nr + npar:nr + npar + nct]]
        pos = nr + npar + nct
        add_t = refs[pos][...] if n_add else None
        pos += n_add
        drow_refs = refs[pos:pos + n_dr]
        dpar_refs = refs[pos + n_dr:pos + n_dr + n_dp]

        def g(*dargs):
            rr, pp = list(row_t), list(par_t)
            for idx, v in zip(drows, dargs[:n_dr]):
                rr[idx] = v
            for idx, v in zip(dparams, dargs[n_dr:]):
                pp[idx] = v
            return tuple(fn(*rr, *pp))

        prim = [row_t[i].astype(F32) for i in drows] + [par_t[i].astype(F32) for i in dparams]
        _, vjp = jax.vjp(g, *prim)
        grads = vjp(tuple(ct_t))
        for n, (o, v) in enumerate(zip(drow_refs, grads[:n_dr])):
            if n == 0 and n_add:
                v = v + add_t.astype(F32)
            o[...] = v.astype(o.dtype)
            if n == 0 and n_tw:
                refs[-1][...] = v.astype(BF16)

        @pl.when(pl.program_id(0) == 0)
        def _():
            for o in dpar_refs:
                o[...] = jnp.zeros_like(o)

        for o, v in zip(dpar_refs, grads[n_dr:]):
            o[...] += v

    res = pl.pallas_call(
        body,
        name=name,
        grid=(T // tm,),
        in_specs=[_row_spec(r, tm) for r in rows] + [_full_spec(p) for p in params] + [_row_spec(c, tm) for c in cts]
        + ([_row_spec(add, tm)] if n_add else []),
        out_specs=[_row_spec(rows[i], tm) for i in drows] + [_full_spec(params[i]) for i in dparams]
        + [_row_spec(rows[drows[0]], tm)] * n_tw,
        out_shape=[jax.ShapeDtypeStruct(rows[i].shape, dt) for i, dt in zip(drows, drow_dtypes)]
        + [jax.ShapeDtypeStruct(params[i].shape, F32) for i in dparams]
        + [jax.ShapeDtypeStruct(rows[drows[0]].shape, BF16)] * n_tw,
        compiler_params=_cparams(("arbitrary",)),
    )(*rows, *params, *cts, *([add] if n_add else []))
    return list(res[:n_dr]) + list(res[n_dr + n_dp:]), list(res[n_dr:n_dr + n_dp])


def _rms(x, g, n):
    return x * lax.rsqrt(jnp.sum(x * x, axis=-1, keepdims=True) * (1.0 / n) + EPS) * g


def _sigmoid(x):
    return 1.0 / (1.0 + jnp.exp(-x))


def _silu(x):
    return x * _sigmoid(x)


def _softplus(x):
    return jnp.maximum(x, 0.0) + jnp.log(1.0 + jnp.exp(-jnp.abs(x)))


def _gelu_tanh(x):
    return 0.5 * x * (1.0 + jnp.tanh(math.sqrt(2.0 / math.pi) * (x + 0.044715 * x * x * x)))


def _lane(shape):
    return lax.broadcasted_iota(jnp.int32, shape, len(shape) - 1)


def _col(x, h):
    return jnp.sum(jnp.where(_lane(x.shape) == h, x, 0.0), axis=-1, keepdims=True)


def _f_norm(x, g):
    return (_rms(x.astype(F32), g, x.shape[-1]),)


def _f_mla_prep(u, ck, sk, gq, gkv):
    u = u.astype(F32)
    cq = _rms(u[:, 0:256], gq, Q_RANK)
    ckv = _rms(u[:, 256:384], gkv, KV_RANK)
    kr = u[:, 384:512] * ck + u[:, 512:640] * sk
    return cq, jnp.concatenate([ckv, kr], axis=1)


def _f_qrope(y, cq, sq):
    y = y.astype(F32)
    c8, s8 = jnp.tile(cq, (1, HEADS)), jnp.tile(sq, (1, HEADS))
    return (y[:, :HEADS * SLAB] * c8 + y[:, HEADS * SLAB:] * s8,)


def _f_lru_gates(xc, wa, ba, wi, bi, lam):
    xc = xc.astype(F32)
    r = _sigmoid(_bdot(xc, wa, "nn") + ba)
    i = _sigmoid(_bdot(xc, wi, "nn") + bi)
    log_a = -LRU_C * r * _softplus(-lam)
    a = jnp.exp(log_a)
    x2 = 2.0 * log_a
    m1 = jnp.where(x2 > -0.02, -x2 * (1.0 + x2 * (0.5 + x2 * (1.0 / 6.0 + x2 * (1.0 / 24.0)))), 1.0 - jnp.exp(x2))
    return a, jnp.sqrt(m1) * (i * xc)


def _f_mix(o, ys, h, ug, g_mla, g_lru):
    o = o.astype(F32)
    y_mla = _rms(o, g_mla, HEADS * VDIM)
    y_lru = _rms(h.astype(F32) * _gelu_tanh(ug[:, 256:512].astype(F32)), g_lru, 256)
    return (jnp.concatenate([y_mla, ys.astype(F32), y_lru], axis=1),)


def _f_xattn(q, k, v):
    hd = D // MEM_HEADS
    outs = []
    for h in range(MEM_HEADS):
        sl = slice(h * hd, (h + 1) * hd)
        s = _bdot(q[:, sl], k[:, sl], "nt") * (1.0 / math.sqrt(hd))
        s = s - jnp.max(s, axis=-1, keepdims=True)
        p = jnp.exp(s)
        p = p / jnp.sum(p, axis=-1, keepdims=True)
        outs.append(_bdot(p, v[:, sl], "nn"))
    return (jnp.concatenate(outs, axis=1),)


def _split_dot(tri, a, kind):
    a_hi = a.astype(BF16)
    r1 = a - a_hi.astype(F32)
    a_mid = r1.astype(BF16)
    a_lo = (r1 - a_mid.astype(F32)).astype(BF16)
    return _dot(tri, a_hi, kind) + _dot(tri, a_mid, kind) + _dot(tri, a_lo, kind)


@jax.custom_vjp
def _tri_cumsum(tri, a):
    return _split_dot(tri, a, "nn")


def _tri_cumsum_fwd(tri, a):
    return _split_dot(tri, a, "nn"), tri


def _tri_cumsum_bwd(tri, g):
    return jnp.zeros_like(tri), _split_dot(tri, g, "tn")


_tri_cumsum.defvjp(_tri_cumsum_fwd, _tri_cumsum_bwd)


def _f_ssd_chunk(c, ug, s0, s1, dtb, alog, dsk, ng):
    L = c.shape[0]
    c = c.astype(F32)
    xbc = _silu(c)
    xs, bm, cm = xbc[:, 0:256], xbc[:, 256:384], xbc[:, 384:512]
    z = ug[:, 0:256].astype(F32)
    dt = _softplus(ug[:, 512:640].astype(F32) + dtb)
    a = dt * (-jnp.exp(alog))
    rowi = lax.broadcasted_iota(jnp.int32, (L, L), 0)
    coli = lax.broadcasted_iota(jnp.int32, (L, L), 1)
    tril = rowi >= coli
    acum = _tri_cumsum(tril.astype(BF16), a)
    acum_t = acum.T
    lane = _lane((1, SLAB))
    lo = lane < 64
    ys, new_s = [], []
    for g in range(2):
        gm = (lane >= 64 * g) & (lane < 64 * g + 64)
        bg, cg = jnp.where(gm, bm, 0.0), jnp.where(gm, cm, 0.0)
        cb = _bdot(cg, bg, "nt")
        x = xs[:, SLAB * g:SLAB * (g + 1)]
        h0, h1 = 2 * g, 2 * g + 1
        ac0, ac1 = _col(acum, h0), _col(acum, h1)
        xdt = x * jnp.where(lo, _col(dt, h0), _col(dt, h1))
        ac_l = jnp.where(lo, ac0, ac1)
        tot = acum[L - 1:L, :]
        tot_l = jnp.where(lo, _col(tot, h0), _col(tot, h1))
        yd = jnp.zeros((L, SLAB), F32)
        for hh, acc, hm in ((h0, ac0, lo), (h1, ac1, jnp.logical_not(lo))):
            seg = acc - acum_t[hh:hh + 1, :]
            lm = jnp.where(tril, jnp.exp(jnp.where(tril, seg, 0.0)), 0.0)
            yd = yd + _bdot(cb * lm, jnp.where(hm, xdt, 0.0), "nn")
        sg = (s0, s1)[g]
        y_off = _bdot(cg, sg, "nn") * jnp.exp(ac_l)
        st = _bdot(bg, xdt * jnp.exp(tot_l - ac_l), "tn")
        new_s.append(jnp.exp(tot_l) * sg + st)
        y = yd + y_off + jnp.where(lo, _col(dsk, h0), _col(dsk, h1)) * x
        y = y * _silu(z[:, SLAB * g:SLAB * (g + 1)])
        ys.append(_rms(y, ng[:, SLAB * g:SLAB * (g + 1)], SLAB))
    return jnp.concatenate(ys, axis=1), new_s[0], new_s[1]


_SSD_TILE = 512


def _ssd_fwd(c, ug, dtb, alog, dsk, ng, *, name):
    T = c.shape[0]
    tm = min(_SSD_TILE, T)
    ncs = tm // SSM_CHUNK
    nc = T // SSM_CHUNK

    def body(c_ref, ug_ref, dtb_ref, alog_ref, dsk_ref, ng_ref, y_ref, sall_ref, s_scr):
        @pl.when(pl.program_id(0) == 0)
        def _():
            s_scr[...] = jnp.zeros_like(s_scr)

        s0, s1 = s_scr[0], s_scr[1]
        for k in range(ncs):
            rows = slice(k * SSM_CHUNK, (k + 1) * SSM_CHUNK)
            sall_ref[k, 0] = s0
            sall_ref[k, 1] = s1
            y, s0, s1 = _f_ssd_chunk(c_ref[rows, :], ug_ref[rows, :], s0, s1, dtb_ref[...], alog_ref[...],
                                     dsk_ref[...], ng_ref[...])
            y_ref[rows, :] = y
        s_scr[0] = s0
        s_scr[1] = s1

    y, sall = pl.pallas_call(
        body,
        name=name,
        grid=(T // tm,),
        in_specs=[_row_spec(c, tm), _row_spec(ug, tm)] + [_full_spec(p) for p in (dtb, alog, dsk, ng)],
        out_specs=[pl.BlockSpec((tm, 256), lambda i: (i, 0)), pl.BlockSpec((ncs, 2, SLAB, SLAB), lambda i: (i, 0, 0, 0))],
        out_shape=[jax.ShapeDtypeStruct((T, 256), F32), jax.ShapeDtypeStruct((nc, 2, SLAB, SLAB), F32)],
        scratch_shapes=[pltpu.VMEM((2, SLAB, SLAB), F32)],
        compiler_params=_cparams(("arbitrary",)),
    )(c, ug, dtb, alog, dsk, ng)
    return y, sall


def _ssd_bwd(c, ug, sall, dy, dtb, alog, dsk, ng, *, name):
    T = c.shape[0]
    tm = min(_SSD_TILE, T)
    ncs = tm // SSM_CHUNK
    nt = T // tm

    def body(c_ref, ug_ref, sall_ref, dy_ref, dtb_ref, alog_ref, dsk_ref, ng_ref,
             dc_ref, dug_ref, ddtb_ref, dalog_ref, ddsk_ref, dng_ref, ds_scr):
        @pl.when(pl.program_id(0) == 0)
        def _():
            ds_scr[...] = jnp.zeros_like(ds_scr)
            for o in (ddtb_ref, dalog_ref, ddsk_ref, dng_ref):
                o[...] = jnp.zeros_like(o)

        ds0, ds1 = ds_scr[0], ds_scr[1]
        for k in reversed(range(ncs)):
            rows = slice(k * SSM_CHUNK, (k + 1) * SSM_CHUNK)
            prim = (c_ref[rows, :].astype(F32), ug_ref[rows, :].astype(F32), sall_ref[k, 0], sall_ref[k, 1],
                    dtb_ref[...], alog_ref[...], dsk_ref[...], ng_ref[...])
            _, vjp = jax.vjp(_f_ssd_chunk, *prim)
            dc, dug, ds0, ds1, g_dtb, g_alog, g_dsk, g_ng = vjp((dy_ref[rows, :].astype(F32), ds0, ds1))
            dc_ref[rows, :] = dc
            dug_ref[rows, :] = dug
            ddtb_ref[...] += g_dtb
            dalog_ref[...] += g_alog
            ddsk_ref[...] += g_dsk
            dng_ref[...] += g_ng
        ds_scr[0] = ds0
        ds_scr[1] = ds1

    rev = lambda i: (nt - 1 - i, 0)
    params = (dtb, alog, dsk, ng)
    res = pl.pallas_call(
        body,
        name=name,
        grid=(nt,),
        in_specs=[pl.BlockSpec((tm, c.shape[1]), rev), pl.BlockSpec((tm, ug.shape[1]), rev),
                  pl.BlockSpec((ncs, 2, SLAB, SLAB), lambda i: (nt - 1 - i, 0, 0, 0)), pl.BlockSpec((tm, 256), rev)]
        + [_full_spec(p) for p in params],
        out_specs=[pl.BlockSpec((tm, 512), rev), pl.BlockSpec((tm, U_GATE), rev)] + [_full_spec(p) for p in params],
        out_shape=[jax.ShapeDtypeStruct((T, 512), F32), jax.ShapeDtypeStruct((T, U_GATE), F32)]
        + [jax.ShapeDtypeStruct(p.shape, F32) for p in params],
        scratch_shapes=[pltpu.VMEM((2, SLAB, SLAB), F32)],
        compiler_params=_cparams(("arbitrary",)),
    )(c, ug, sall, dy, *params)
    return res


_CONV_TILE = 512
_HALO = 8
_CONV_W = 4


def _conv_fwd(u, w, b, *, name):
    T, C = u.shape
    tm = min(_CONV_TILE, T)
    hb = tm // _HALO

    def body(u_ref, prev_ref, w_ref, b_ref, y1_ref, y2_ref, ext):
        i = pl.program_id(0)
        ext[0:_HALO, :] = jnp.where(i > 0, prev_ref[...], 0.0)
        ext[_HALO:, :] = u_ref[...]
        y = jnp.broadcast_to(b_ref[...], (tm, C))
        for k in range(_CONV_W):
            y = y + ext[_HALO - (_CONV_W - 1) + k:_HALO - (_CONV_W - 1) + k + tm, :] * w_ref[k:k + 1, :]
        y1_ref[...] = y[:, 0:512]
        y2_ref[...] = y[:, 512:768]

    return pl.pallas_call(
        body,
        name=name,
        grid=(T // tm,),
        in_specs=[_row_spec(u, tm), pl.BlockSpec((_HALO, C), lambda i: (jnp.maximum(i * hb - 1, 0), 0)),
                  _full_spec(w), _full_spec(b)],
        out_specs=[pl.BlockSpec((tm, 512), lambda i: (i, 0)), pl.BlockSpec((tm, 256), lambda i: (i, 0))],
        out_shape=[jax.ShapeDtypeStruct((T, 512), F32), jax.ShapeDtypeStruct((T, 256), F32)],
        scratch_shapes=[pltpu.VMEM((tm + _HALO, C), F32)],
        compiler_params=_cparams(("parallel",)),
    )(u, u, w, b)


def _conv_bwd(u, dy1, dy2, w, *, name):
    T, C = u.shape
    tm = min(_CONV_TILE, T)
    hb = tm // _HALO
    nt = T // tm

    def body(u_ref, prev_ref, dy1_ref, next1_ref, dy2_ref, next2_ref, w_ref, du_ref, dw_ref, db_ref, ext, dext):
        i = pl.program_id(0)
        ext[0:_HALO, :] = jnp.where(i > 0, prev_ref[...], 0.0)
        ext[_HALO:, :] = u_ref[...]
        dext[0:tm, 0:512] = dy1_ref[...]
        dext[0:tm, 512:768] = dy2_ref[...]
        dext[tm:, 0:512] = jnp.where(i < nt - 1, next1_ref[...], 0.0)
        dext[tm:, 512:768] = jnp.where(i < nt - 1, next2_ref[...], 0.0)

        @pl.when(i == 0)
        def _():
            dw_ref[...] = jnp.zeros_like(dw_ref)
            db_ref[...] = jnp.zeros_like(db_ref)

        dy = dext[0:tm, :]
        du = jnp.zeros((tm, C), F32)
        for k in range(_CONV_W):
            du = du + dext[_CONV_W - 1 - k:_CONV_W - 1 - k + tm, :] * w_ref[k:k + 1, :]
            xk = ext[_HALO - (_CONV_W - 1) + k:_HALO - (_CONV_W - 1) + k + tm, :]
            dw_ref[k:k + 1, :] += jnp.sum(dy * xk, axis=0, keepdims=True)
        du_ref[...] = du
        db_ref[...] += jnp.sum(dy, axis=0, keepdims=True)

    nxt = lambda i: (jnp.minimum((i + 1) * hb, T // _HALO - 1), 0)
    return pl.pallas_call(
        body,
        name=name,
        grid=(nt,),
        in_specs=[_row_spec(u, tm), pl.BlockSpec((_HALO, C), lambda i: (jnp.maximum(i * hb - 1, 0), 0)),
                  _row_spec(dy1, tm), pl.BlockSpec((_HALO, 512), nxt), _row_spec(dy2, tm), pl.BlockSpec((_HALO, 256), nxt),
                  _full_spec(w)],
        out_specs=[pl.BlockSpec((tm, C), lambda i: (i, 0)), _full_spec(w), pl.BlockSpec((1, C), lambda i: (0, 0))],
        out_shape=[jax.ShapeDtypeStruct((T, C), F32), jax.ShapeDtypeStruct(w.shape, F32), jax.ShapeDtypeStruct((1, C), F32)],
        scratch_shapes=[pltpu.VMEM((tm + _HALO, C), F32), pltpu.VMEM((tm + _HALO, C), F32)],
        compiler_params=_cparams(("arbitrary",)),
    )(u, u, dy1, dy1, dy2, dy2, w)


_SCAN_TILE = 1024
_SUB = 8


def _shift_rows(x, d, fill, up):
    r = lax.broadcasted_iota(jnp.int32, x.shape, 0)
    if up:
        return jnp.where(r < _SUB - d, pltpu.roll(x, _SUB - d, 0), fill)
    return jnp.where(r >= d, pltpu.roll(x, d, 0), fill)


def _lru_scan_fwd(a, b, *, name):
    T, W = a.shape
    tr = min(_SCAN_TILE, T)

    def body(a_ref, b_ref, h_ref, hp_ref, carry):
        @pl.when(pl.program_id(0) == 0)
        def _():
            carry[...] = jnp.zeros_like(carry)

        def step(t, cr):
            rows = pl.ds(pl.multiple_of(t * _SUB, _SUB), _SUB)
            aa, bb = a_ref[rows, :], b_ref[rows, :]
            for d in (1, 2, 4):
                bb = bb + aa * _shift_rows(bb, d, 0.0, False)
                aa = aa * _shift_rows(aa, d, 1.0, False)
            h = bb + aa * cr
            h_ref[rows, :] = h
            r = lax.broadcasted_iota(jnp.int32, h.shape, 0)
            hp_ref[rows, :] = jnp.where(r >= 1, pltpu.roll(h, 1, 0), cr)
            return jnp.broadcast_to(h[_SUB - 1:_SUB, :], (_SUB, W))

        carry[...] = lax.fori_loop(0, tr // _SUB, step, carry[...])

    return pl.pallas_call(
        body,
        name=name,
        grid=(T // tr,),
        in_specs=[_row_spec(a, tr), _row_spec(b, tr)],
        out_specs=[pl.BlockSpec((tr, W), lambda i: (i, 0))] * 2,
        out_shape=[jax.ShapeDtypeStruct((T, W), F32)] * 2,
        scratch_shapes=[pltpu.VMEM((_SUB, W), F32)],
        compiler_params=_cparams(("arbitrary",)),
    )(a, b)


def _lru_scan_bwd(a, dh, hprev, *, name):
    T, W = a.shape
    tr = min(_SCAN_TILE, T)
    nt = T // tr

    def body(a_ref, dh_ref, hp_ref, g_ref, da_ref, carry):
        @pl.when(pl.program_id(0) == 0)
        def _():
            carry[...] = jnp.zeros_like(carry)

        nsub = tr // _SUB

        def step(s, cr):
            t = nsub - 1 - s
            rows = pl.ds(pl.multiple_of(t * _SUB, _SUB), _SUB)
            a_t = a_ref[rows, :]
            aa = _shift_rows(a_t, 1, 1.0, True)
            bb = dh_ref[rows, :]
            for d in (1, 2, 4):
                bb = bb + aa * _shift_rows(bb, d, 0.0, True)
                aa = aa * _shift_rows(aa, d, 1.0, True)
            g = bb + aa * cr
            g_ref[rows, :] = g
            da_ref[rows, :] = g * hp_ref[rows, :]
            return jnp.broadcast_to(a_t[0:1, :] * g[0:1, :], (_SUB, W))

        carry[...] = lax.fori_loop(0, nsub, step, carry[...])

    rev = lambda i: (nt - 1 - i, 0)
    return pl.pallas_call(
        body,
        name=name,
        grid=(nt,),
        in_specs=[pl.BlockSpec((tr, W), rev)] * 3,
        out_specs=[pl.BlockSpec((tr, W), rev)] * 2,
        out_shape=[jax.ShapeDtypeStruct((T, W), F32)] * 2,
        scratch_shapes=[pltpu.VMEM((_SUB, W), F32)],
        compiler_params=_cparams(("arbitrary",)),
    )(a, dh, hprev)


_ATT_BLK = 512
_ATT_QPARTS = 2
_ATT_BWD_TRIP = 4
_ATT_SCALE = 1.0 / math.sqrt(NOPE + ROPE)
_ATT_SCALE2 = _ATT_SCALE * math.log2(math.e)


def _call_with_exchange(body, send, gather, *, name, grid, in_specs, out_specs, out_shape, args):
    if send is None:
        return pl.pallas_call(body, name=name, grid=grid, in_specs=in_specs, out_specs=out_specs, out_shape=out_shape,
                              compiler_params=_cparams(("parallel", "arbitrary")))(*args)
    n_in, n_out = len(in_specs), len(out_specs)

    def riding(*refs):
        comm = (refs[n_in], refs[n_in + 1 + n_out]) + tuple(refs[n_in + 2 + n_out:])
        h, i = pl.program_id(0), pl.program_id(1)

        @pl.when((h == 0) & (i == 0))
        def _():
            _chip_start(*comm, gather=gather)

        body(*refs[:n_in], *refs[n_in + 1:n_in + 1 + n_out])

        @pl.when((h == grid[0] - 1) & (i == grid[1] - 1))
        def _():
            _chip_wait(*comm, gather=gather)

    shape = (4,) + send.shape if gather else send.shape
    return pl.pallas_call(
        riding, name=name, grid=grid, in_specs=in_specs + [_ANY], out_specs=out_specs + [_ANY],
        out_shape=out_shape + [jax.ShapeDtypeStruct(shape, send.dtype)], scratch_shapes=_CHIP_SEMS,
        compiler_params=_cparams(("arbitrary", "arbitrary")))(*args, send)


def _attn_fwd(q, kv, *, name, send=None):
    T = q.shape[0]
    blk = min(_ATT_BLK, T)
    nq = T // blk
    parts = _ATT_QPARTS if nq % _ATT_QPARTS == 0 else 1

    def body(q_ref, kv_ref, o_ref, lse_ref):
        i = pl.program_id(1)
        def one(part, j0, nblk, carry, masked):
            m, l, acc = carry
            rows = pl.ds(pl.multiple_of(j0 * blk, blk), nblk * blk)
            s = _dot(q_ref[part * blk:(part + 1) * blk, :], kv_ref[rows, 0:SLAB], "nt")
            if masked:
                col = lax.broadcasted_iota(jnp.int32, s.shape, 1)
                row = lax.broadcasted_iota(jnp.int32, s.shape, 0)
                s = jnp.where(col <= row + (nblk - 1) * blk, s, -jnp.inf)
            m_new = jnp.maximum(m, jnp.max(s, axis=-1, keepdims=True))
            alpha = jnp.exp2(m - m_new)
            p = jnp.exp2(s - m_new)
            l = alpha * l + jnp.sum(p, axis=-1, keepdims=True)
            acc = alpha * acc + _dot(p, kv_ref[rows, SLAB:2 * SLAB], "nn")
            return m_new, l, acc

        def step(t, carry):
            return tuple(one(part, t * parts, parts, carry[part], False) for part in range(parts))

        init = ((jnp.full((blk, 1), -jnp.inf, F32), jnp.zeros((blk, 1), F32), jnp.zeros((blk, SLAB), F32)),) * parts
        carry = lax.fori_loop(0, i // 2, lambda t, c: step(2 * t + 1, step(2 * t, c)), init)
        carry = lax.fori_loop((i // 2) * 2, i, step, carry)
        for part in range(parts):
            m, l, acc = one(part, i * parts, part + 1, carry[part], True)
            o_ref[part * blk:(part + 1) * blk, :] = acc / l
            lse_ref[part] = jnp.broadcast_to(m + jnp.log(l) * math.log2(math.e), (blk, SLAB)).T[0:_SUB, :]

    return _call_with_exchange(
        body, send, True,
        name=name,
        grid=(HEADS, nq // parts),
        in_specs=[pl.BlockSpec((parts * blk, SLAB), lambda h, i: (i, h)), pl.BlockSpec((T, 2 * SLAB), lambda h, i: (0, h))],
        out_specs=[pl.BlockSpec((parts * blk, SLAB), lambda h, i: (i, h)),
                   pl.BlockSpec((None, parts, _SUB, blk), lambda h, i: (h, i, 0, 0))],
        out_shape=[jax.ShapeDtypeStruct((T, HEADS * SLAB), F32), jax.ShapeDtypeStruct((HEADS, nq, _SUB, blk), F32)],
        args=(q, kv))


def _attn_delta(do, o, *, name):
    T = o.shape[0]
    blk = min(_ATT_BLK, T)
    nq = T // blk

    def body(do_ref, o_ref, d_ref):
        for h in range(HEADS):
            cols = slice(h * SLAB, (h + 1) * SLAB)
            dl = jnp.sum(do_ref[:, cols].astype(F32) * o_ref[:, cols], axis=-1, keepdims=True)
            d_ref[h] = jnp.broadcast_to(dl, (blk, SLAB)).T[0:_SUB, :]

    return pl.pallas_call(
        body,
        name=name,
        grid=(nq,),
        in_specs=[pl.BlockSpec((blk, HEADS * SLAB), lambda i: (i, 0))] * 2,
        out_specs=pl.BlockSpec((HEADS, None, _SUB, blk), lambda i: (0, i, 0, 0)),
        out_shape=jax.ShapeDtypeStruct((HEADS, nq, _SUB, blk), F32),
        compiler_params=_cparams(("parallel",)),
    )(do, o)


def _attn_bwd(q, kv, do, lse, delta, *, name, send=None):
    T = q.shape[0]
    blk = min(_ATT_BLK, T)
    nq = T // blk

    def body(q_ref, kv_ref, do_ref, lse_ref, dl_ref, dqt_ref, dkv_ref):
        j = pl.program_id(1)

        @pl.when(j == 0)
        def _():
            dqt_ref[...] = jnp.zeros_like(dqt_ref)

        kb, vb = kv_ref[:, 0:SLAB], kv_ref[:, SLAB:2 * SLAB]
        kbt = kb.astype(F32).T.astype(BF16)
        kpos = j * blk + lax.broadcasted_iota(jnp.int32, (blk, blk), 0)

        def step(i, carry, masked):
            dk, dv = carry
            rows = pl.ds(pl.multiple_of(i * blk, blk), blk)
            qb, dob = q_ref[rows, :], do_ref[rows, :]
            st = _dot(kb, qb, "nt")
            if masked:
                qpos = i * blk + lax.broadcasted_iota(jnp.int32, (blk, blk), 1)
                st = jnp.where(kpos <= qpos, st, -jnp.inf)
            pt = jnp.exp2(st - (lse_ref[i, 0:1, :] - math.log2(math.log(2.0))))
            dpt = _dot(vb, dob, "nt")
            dst = pt * (dpt - dl_ref[i, 0:1, :])
            dv = dv + _dot(pt, dob, "nn")
            dk = dk + _dot(dst, qb, "nn")
            dqt_ref[i] += _dot(kbt, dst, "nn")
            return dk, dv

        zero = jnp.zeros((blk, SLAB), F32)
        carry = step(j, (zero, zero), True)
        rest = nq - 1 - j

        def trip(t, c):
            for u in range(_ATT_BWD_TRIP):
                c = step(j + 1 + _ATT_BWD_TRIP * t + u, c, False)
            return c

        carry = lax.fori_loop(0, rest // _ATT_BWD_TRIP, trip, carry)
        tail = j + 1 + (rest // _ATT_BWD_TRIP) * _ATT_BWD_TRIP
        dk, dv = lax.fori_loop(tail, nq, functools.partial(step, masked=False), carry)
        dkv_ref[:, 0:SLAB] = dk.astype(BF16)
        dkv_ref[:, SLAB:2 * SLAB] = (dv * (1.0 / math.log(2.0))).astype(BF16)

    stat_spec = pl.BlockSpec((None, nq, _SUB, blk), lambda h, j: (h, 0, 0, 0))
    head_t_spec = pl.BlockSpec((None, nq, SLAB, blk), lambda h, j: (h, 0, 0, 0))
    return _call_with_exchange(
        body, send, False,
        name=name,
        grid=(HEADS, nq),
        in_specs=[pl.BlockSpec((T, SLAB), lambda h, j: (0, h)), pl.BlockSpec((blk, 2 * SLAB), lambda h, j: (j, h)),
                  pl.BlockSpec((T, SLAB), lambda h, j: (0, h)), stat_spec, stat_spec],
        out_specs=[head_t_spec, pl.BlockSpec((blk, 2 * SLAB), lambda h, j: (j, h))],
        out_shape=[jax.ShapeDtypeStruct((HEADS, nq, SLAB, blk), F32), jax.ShapeDtypeStruct((T, HEADS * 2 * SLAB), BF16)],
        args=(q, kv, do, lse, delta))


def _qrope_bwd(dqt, cq, sq, *, name):
    _, nq, _, blk = dqt.shape
    T = nq * blk

    def body(dqt_ref, c_ref, s_ref, dy_ref):
        c, s = c_ref[...], s_ref[...]
        for h in range(HEADS):
            dq = dqt_ref[h].T
            dy_ref[:, h * SLAB:(h + 1) * SLAB] = (dq * c).astype(BF16)
            dy_ref[:, (HEADS + h) * SLAB:(HEADS + h + 1) * SLAB] = (dq * s).astype(BF16)

    return pl.pallas_call(
        body,
        name=name,
        grid=(nq,),
        in_specs=[pl.BlockSpec((HEADS, None, SLAB, blk), lambda i: (0, i, 0, 0)), _row_spec(cq, blk), _row_spec(sq, blk)],
        out_specs=pl.BlockSpec((blk, 2 * HEADS * SLAB), lambda i: (i, 0)),
        out_shape=jax.ShapeDtypeStruct((T, 2 * HEADS * SLAB), BF16),
        compiler_params=_cparams(("parallel",)),
    )(dqt, cq, sq)


def _loss_head(x, tgt, g, *, name, tm=256):
    T = x.shape[0]
    tm = min(tm, T)

    def body(x_ref, t_ref, g_ref, loss_ref, dx_ref, dxh_ref, dg_ref):
        def f(xv, gv):
            e = _rms(xv, gv, D) - t_ref[...]
            row = jnp.sum(e * e, axis=1, keepdims=True)
            return jnp.sum(row, axis=0, keepdims=True) * (0.5 / D)

        val, vjp = jax.vjp(f, x_ref[...], g_ref[...])
        dxv, dgv = vjp(jnp.ones((1, 1), F32))

        @pl.when(pl.program_id(0) == 0)
        def _():
            loss_ref[...] = jnp.zeros_like(loss_ref)
            dg_ref[...] = jnp.zeros_like(dg_ref)

        dx_ref[...] = dxv
        dxh_ref[...] = dxv.astype(BF16)
        dg_ref[...] += dgv
        loss_ref[...] += jnp.broadcast_to(val, loss_ref.shape)

    return pl.pallas_call(
        body,
        name=name,
        grid=(T // tm,),
        in_specs=[_row_spec(x, tm), _row_spec(tgt, tm), _full_spec(g)],
        out_specs=[pl.BlockSpec((1, SLAB), lambda i: (0, 0)), _row_spec(x, tm), _row_spec(x, tm), _full_spec(g)],
        out_shape=[jax.ShapeDtypeStruct((1, SLAB), F32), jax.ShapeDtypeStruct(x.shape, F32),
                   jax.ShapeDtypeStruct(x.shape, BF16), jax.ShapeDtypeStruct(g.shape, F32)],
        compiler_params=_cparams(("arbitrary",)),
    )(x, tgt, g)


def _row_tile(rows, cols, budget=256 * 1024):
    best = None
    for t in range(16, rows + 1, 16):
        if rows % t == 0 and t * cols <= budget:
            best = t
    return best or rows


def _sum_fixed(x, out_dtype, *, name):
    n, R, C = x.shape
    tr = _row_tile(R, C)

    def body(x_ref, o_ref):
        acc = x_ref[0].astype(F32)
        for k in range(1, n):
            acc = acc + x_ref[k].astype(F32)
        o_ref[...] = acc.astype(o_ref.dtype)

    return pl.pallas_call(
        body,
        name=name,
        grid=(R // tr,),
        in_specs=[pl.BlockSpec((n, tr, C), lambda i: (0, i, 0))],
        out_specs=pl.BlockSpec((tr, C), lambda i: (i, 0)),
        out_shape=jax.ShapeDtypeStruct((R, C), out_dtype),
        compiler_params=_cparams(("parallel",)),
    )(x)


def _adamw(w, g, m, v, *, name):
    R, C = w.shape
    tr = _row_tile(R, C, 128 * 1024)

    def body(w_ref, g_ref, m_ref, v_ref, d_ref, nm_ref, nv_ref):
        gv = g_ref[...]
        mv = B1 * m_ref[...] + (1.0 - B1) * gv
        vv = B2 * v_ref[...] + (1.0 - B2) * (gv * gv)
        m_hat = mv / (1.0 - B1 ** STEP)
        v_hat = vv / (1.0 - B2 ** STEP)
        d_ref[...] = -LR * (m_hat / (jnp.sqrt(v_hat) + AEPS) + WD * w_ref[...])
        nm_ref[...] = mv
        nv_ref[...] = vv

    spec = pl.BlockSpec((tr, C), lambda i: (i, 0))
    return pl.pallas_call(
        body, name=name, grid=(R // tr,), in_specs=[spec] * 4, out_specs=[spec] * 3,
        out_shape=[jax.ShapeDtypeStruct((R, C), F32)] * 3, compiler_params=_cparams(("parallel",)),
    )(w, g, m, v)


_FLIPS = ((1, 0), (0, 1), (1, 1))
_ANY = pl.BlockSpec(memory_space=pl.ANY)


def _me():
    return lax.axis_index("x"), lax.axis_index("y"), lax.axis_index("c")


def _flip(mx, my, f):
    return (1 - mx if f[0] else mx), (1 - my if f[1] else my)


_CHIP_SEMS = [pltpu.SemaphoreType.DMA((3,)), pltpu.SemaphoreType.DMA((3,)), pltpu.SemaphoreType.DMA]


def _chip_copies(x_ref, out_ref, send_sems, recv_sems, local_sem, gather):
    mx, my, mc = _me()
    mine = 2 * mx + my
    outgoing, incoming = [], []
    for k, f in enumerate(_FLIPS):
        px, py = _flip(mx, my, f)
        peer = 2 * px + py
        src = x_ref if gather else x_ref.at[peer]
        for dst, to in ((out_ref.at[mine], outgoing), (out_ref.at[peer], incoming)):
            to.append(pltpu.make_async_remote_copy(src_ref=src, dst_ref=dst, send_sem=send_sems.at[k],
                                                   recv_sem=recv_sems.at[k], device_id=(px, py, mc), device_id_type=MESH))
    local = None if gather else pltpu.make_async_copy(x_ref.at[mine], out_ref.at[mine], local_sem)
    return outgoing, incoming, local


def _chip_start(*refs, gather):
    outgoing, _, local = _chip_copies(*refs, gather)
    if local is not None:
        local.start()
    for cp in outgoing:
        cp.start()


def _chip_wait(*refs, gather):
    outgoing, incoming, local = _chip_copies(*refs, gather)
    for cp in incoming:
        cp.wait_recv()
    for cp in outgoing:
        cp.wait_send()
    if local is not None:
        local.wait()


def _chip_exchange(x, gather, *, name):
    shape = x.shape if not gather else (4,) + x.shape

    def body(*refs):
        _chip_start(*refs, gather=gather)
        _chip_wait(*refs, gather=gather)

    return pl.pallas_call(
        body, name=name, in_specs=[_ANY], out_specs=_ANY, out_shape=jax.ShapeDtypeStruct(shape, x.dtype),
        scratch_shapes=_CHIP_SEMS,
    )(x)


def _core_exchange(x, half, *, name):
    shape = x.shape[1:] if half else x.shape

    def body(x_ref, out_ref, send_sem, recv_sem):
        mx, my, mc = _me()
        src = x_ref.at[1 - mc] if half else x_ref
        cp = pltpu.make_async_remote_copy(src_ref=src, dst_ref=out_ref, send_sem=send_sem, recv_sem=recv_sem,
                                          device_id=(mx, my, 1 - mc), device_id_type=MESH)
        cp.start()
        cp.wait()

    return pl.pallas_call(
        body, name=name, in_specs=[_ANY], out_specs=_ANY, out_shape=jax.ShapeDtypeStruct(shape, x.dtype),
        scratch_shapes=[pltpu.SemaphoreType.DMA, pltpu.SemaphoreType.DMA],
    )(x)


def _core_gather(x, g4, *, name):
    def body(x_ref, g_ref, out_ref, send_sems, recv_sems):
        mx, my, mc = _me()
        srcs = [(x_ref, 2 * mx + my)] + [(g_ref.at[2 * px + py], 2 * px + py) for px, py in (_flip(mx, my, f) for f in _FLIPS)]
        copies = [pltpu.make_async_remote_copy(src_ref=src, dst_ref=out_ref.at[slot], send_sem=send_sems.at[k],
                                               recv_sem=recv_sems.at[k], device_id=(mx, my, 1 - mc), device_id_type=MESH)
                  for k, (src, slot) in enumerate(srcs)]
        for cp in copies:
            cp.start()
        for cp in copies:
            cp.wait()

    return pl.pallas_call(
        body, name=name, in_specs=[_ANY, _ANY], out_specs=_ANY, out_shape=jax.ShapeDtypeStruct(g4.shape, g4.dtype),
        scratch_shapes=[pltpu.SemaphoreType.DMA((4,)), pltpu.SemaphoreType.DMA((4,))],
    )(x, g4)


def _finish_gather(x, g4, *, name):
    mx, my, _ = _me()
    own = (jnp.arange(4) == 2 * mx + my).reshape((4,) + (1,) * x.ndim)
    return jnp.where(own, x[None], g4), _core_gather(x, g4, name=name + "_cores")


def _by_core(mine, other):
    mc = lax.axis_index("c")
    return jnp.stack([jnp.where(mc == 0, mine, other), jnp.where(mc == 0, other, mine)])


def _all_gather(x, *, name):
    return _by_core(*_finish_gather(x, _chip_exchange(x, True, name=name + "_chips"), name=name))


def _add_own_half(x, got, *, name, out_dtype=BF16):
    _, R, C = x.shape
    tr = _row_tile(R, C)

    def body(c_ref, x_ref, g_ref, o_ref):
        o_ref[...] = (x_ref[...] + g_ref[...]).astype(o_ref.dtype)

    return pl.pallas_call(
        body,
        name=name,
        grid_spec=pltpu.PrefetchScalarGridSpec(
            num_scalar_prefetch=1, grid=(R // tr,),
            in_specs=[pl.BlockSpec((None, tr, C), lambda i, c: (c[0], i, 0)), pl.BlockSpec((tr, C), lambda i, c: (i, 0))],
            out_specs=pl.BlockSpec((tr, C), lambda i, c: (i, 0))),
        out_shape=jax.ShapeDtypeStruct((R, C), out_dtype),
        compiler_params=_cparams(("parallel",)),
    )(lax.axis_index("c").astype(jnp.int32).reshape(1), x, got)


def _chip_sums(x, *, name):
    _, _, R, C = x.shape
    got = _core_exchange(x, True, name=name + "_cores")
    return _add_own_half(x.reshape(2, 4 * R, C), got.reshape(4 * R, C), name=name + "_add").reshape(4, R, C)


def _all_reduce(x, *, name):
    g = _all_gather(x, name=name)
    return _sum_fixed(g.reshape((N_DEV,) + x.shape), F32, name=name + "_sum")


WEIGHTS = ['mix_norm_g', 'w_in', 'mla_q_norm_g', 'mla_kv_norm_g', 'mla_w_uq', 'mla_w_ukv', 'mla_out_g', 'ssm_conv_w',
           'ssm_conv_b', 'ssm_dt_bias', 'ssm_a_log', 'ssm_d', 'ssm_norm_g', 'lru_conv_w', 'lru_conv_b', 'lru_w_a',
           'lru_b_a', 'lru_w_i', 'lru_b_i', 'lru_lambda', 'lru_out_g', 'w_out', 'xattn_norm_g', 'mem_norm_g', 'w_mq',
           'w_mk', 'w_mv', 'w_mo', 'mlp_norm_g', 'w_mlp1', 'w_mlp2', 'final_norm_g']
ROW_SHARDED = ('w_in', 'w_out', 'w_mq', 'w_mk', 'w_mv', 'w_mo', 'w_mlp2')
COL_SHARDED = ('mla_w_uq', 'mla_w_ukv', 'w_mlp1')
BIG = tuple(n for n in WEIGHTS if n in ROW_SHARDED + COL_SHARDED)
CONV_SHARDED = ('ssm_conv_w', 'lru_conv_w')
SMALL = tuple(n for n in WEIGHTS if n not in BIG)
PACK_C = 1024


def _pack(arrs, dtype, lead=()):
    flat = jnp.concatenate([a.reshape(lead + (-1,)).astype(dtype) for a in arrs], axis=-1)
    n = flat.shape[-1]
    rows = -(-n // (16 * PACK_C)) * 16
    flat = jnp.pad(flat, [(0, 0)] * len(lead) + [(0, rows * PACK_C - n)])
    return flat.reshape(lead + (rows, PACK_C))


def _unpack(packed, shapes, lead=()):
    flat = packed.reshape(lead + (-1,))
    out, off = [], 0
    for s in shapes:
        n = math.prod(s)
        out.append(flat[..., off:off + n].reshape(lead + tuple(s)))
        off += n
    return out


def _pad_lanes(v, n=SLAB):
    return jnp.pad(v.astype(F32), (0, n - v.shape[0])).reshape(1, n)


PIECES = ('win', 'wq', 'wkv', 'wout', 'w_mq', 'w_mk', 'w_mv', 'w_mo', 'w_mlp1', 'w_mlp2')
PIECE_SOURCE = {'win': 'w_in', 'wq': 'mla_w_uq', 'wkv': 'mla_w_ukv', 'wout': 'w_out'}
PIECE_SHAPE = {'win': (128, 2048), 'wq': (Q_RANK, 2 * SLAB), 'wkv': (KV_RANK, 2 * SLAB), 'wout': (128, D),
               'w_mq': (128, D), 'w_mk': (128, D), 'w_mv': (128, D), 'w_mo': (128, D), 'w_mlp1': (512, D),
               'w_mlp2': (512, D)}
PIECE_COLS = ('wq', 'wkv')


def _k_win(w):
    kr = w[..., 384:416]
    zc = lambda k: jnp.zeros(w.shape[:-1] + (k,), w.dtype)
    return jnp.concatenate(
        [w[..., 0:384], kr, zc(96), kr[..., 16:32], kr[..., 0:16], zc(96),
         w[..., 416:672], w[..., 1444:1700], w[..., 1184:1188], zc(124),
         w[..., 672:1184], w[..., 1188:1444]], axis=-1)


def _k_win_inv(m):
    gk = m[..., 384:416] + jnp.concatenate([m[..., 528:544], m[..., 512:528]], axis=-1)
    return jnp.concatenate([m[..., 0:384], gk, m[..., 640:896], m[..., 1280:1792], m[..., 1152:1156], m[..., 1792:2048],
                            m[..., 896:1152]], axis=-1)


def _k_wq(w):
    nh = w.shape[-1] // (NOPE + ROPE)
    w = w.reshape(w.shape[:-1] + (nh, NOPE + ROPE))
    nope, r1, r2 = w[..., :NOPE], w[..., NOPE:NOPE + 16], w[..., NOPE + 16:]
    z = lambda k: jnp.zeros(w.shape[:-1] + (k,), w.dtype)
    both = jnp.stack([jnp.concatenate([nope, r1, r2, z(32)], -1), jnp.concatenate([z(64), r2, r1, z(32)], -1)], axis=-3)
    return both.reshape(w.shape[:-2] + (2 * nh * SLAB,))


def _k_wq_inv(m):
    nh = m.shape[-1] // (2 * SLAB)
    m = m.reshape(m.shape[:-1] + (2, nh, SLAB))
    q0, q1 = m[..., 0, :, :], m[..., 1, :, :]
    w = jnp.concatenate([q0[..., :64], q0[..., 64:80] + q1[..., 80:96], q0[..., 80:96] + q1[..., 64:80]], -1)
    return w.reshape(w.shape[:-2] + (nh * (NOPE + ROPE),))


def _k_wkv(w):
    nh = w.shape[-1] // (NOPE + VDIM)
    w = w.reshape(w.shape[:-1] + (nh, NOPE + VDIM))
    z = jnp.zeros(w.shape[:-1] + (64,), w.dtype)
    return jnp.concatenate([w[..., :NOPE], z, w[..., NOPE:], z], -1).reshape(w.shape[:-2] + (nh * 2 * SLAB,))


def _k_wkv_inv(m):
    nh = m.shape[-1] // (2 * SLAB)
    m = m.reshape(m.shape[:-1] + (nh, 2 * SLAB))
    return jnp.concatenate([m[..., :NOPE], m[..., SLAB:SLAB + VDIM]], -1).reshape(m.shape[:-2] + (nh * (NOPE + VDIM),))


def _k_swap(w):
    return jnp.swapaxes(w, -1, -2)


_K_FWD = {'win': _k_win, 'wq': _k_wq, 'wkv': _k_wkv, 'w_mlp1': _k_swap}
_K_INV = {'win': _k_win_inv, 'wq': _k_wq_inv, 'wkv': _k_wkv_inv, 'w_mlp1': _k_swap}


def _assemble(piece, g):
    _, _, a, b = g.shape
    if piece == 'wq':
        return g.reshape(2, 4, a, 2, SLAB).transpose(2, 3, 1, 0, 4).reshape(a, N_DEV * b)
    if piece in PIECE_COLS:
        return g.transpose(2, 1, 0, 3).reshape(a, N_DEV * b)
    return g.transpose(1, 0, 2, 3).reshape(N_DEV * a, b)


def _disassemble(piece, full):
    a, b = PIECE_SHAPE[piece]
    if piece == 'wq':
        return full.reshape(a, 2, 4, 2, SLAB).transpose(3, 2, 0, 1, 4).reshape(2, 4, a, b)
    if piece in PIECE_COLS:
        return full.reshape(a, 4, 2, b).transpose(2, 1, 0, 3)
    return full.reshape(4, 2, a, b).transpose(1, 0, 2, 3)


def _piece_rows(piece):
    a, b = PIECE_SHAPE[piece]
    return a * b // PACK_C


def _prep_layer(pieces, Ws, l):
    P = dict(pieces)
    ri, ci = jnp.arange(SLAB)[:, None], jnp.arange(2 * SLAB)[None, :]
    sel = ((ri < ROPE) & (ci == ri + NOPE)).astype(P['wkv'].dtype)
    P['wkv'] = jnp.concatenate([P['wkv'], jnp.tile(sel, (1, HEADS))], axis=0)
    wout = P['wout']
    mla_rows = jnp.pad(wout[:HEADS * VDIM].reshape(HEADS, VDIM, D), ((0, 0), (0, SLAB - VDIM), (0, 0)))
    P['wout'] = jnp.concatenate([mla_rows.reshape(HEADS * SLAB, D), wout[HEADS * VDIM:]], axis=0)
    W = Ws
    row = lambda n: W[n][l].astype(F32).reshape(1, -1)
    for n in ('mix_norm_g', 'mla_q_norm_g', 'mla_kv_norm_g', 'ssm_norm_g', 'lru_lambda', 'lru_out_g', 'xattn_norm_g',
              'mem_norm_g', 'mlp_norm_g'):
        P[n] = row(n)
    P['mla_out_g'] = jnp.pad(W['mla_out_g'][l].astype(F32).reshape(HEADS, VDIM), ((0, 0), (0, SLAB - VDIM))).reshape(1, -1)
    for n in ('ssm_dt_bias', 'ssm_a_log', 'ssm_d'):
        P[n] = _pad_lanes(W[n][l])
    P['conv_w'] = jnp.pad(jnp.concatenate([W['ssm_conv_w'][l], W['lru_conv_w'][l]], axis=1).astype(F32), ((0, 4), (0, 0)))
    P['conv_b'] = jnp.concatenate([W['ssm_conv_b'][l], W['lru_conv_b'][l]]).astype(F32).reshape(1, -1)
    for n in ('lru_w_a', 'lru_w_i'):
        P[n] = jnp.concatenate([jnp.pad(W[n][l, k].astype(F32), ((0, 0), (64 * k, 192 - 64 * k))) for k in range(4)], axis=0)
    for n in ('lru_b_a', 'lru_b_i'):
        P[n] = W[n][l].astype(F32).reshape(1, -1)
    return P


def _unprep_pieces(G):
    o = {n: G[n] for n in PIECES}
    o['wkv'] = G['wkv'][:KV_RANK]
    wo = G['wout']
    o['wout'] = jnp.concatenate([wo[:HEADS * SLAB].reshape(HEADS, SLAB, D)[:, :VDIM].reshape(HEADS * VDIM, D),
                                 wo[HEADS * SLAB:]], axis=0)
    return o


def _unprep_small(G):
    o = {}
    for n in ('mix_norm_g', 'mla_q_norm_g', 'mla_kv_norm_g', 'ssm_norm_g', 'lru_lambda', 'lru_out_g', 'xattn_norm_g',
              'mem_norm_g', 'mlp_norm_g', 'lru_b_a', 'lru_b_i'):
        o[n] = G[n].reshape(-1)
    o['lru_b_a'] = o['lru_b_a'].reshape(4, 64)
    o['lru_b_i'] = o['lru_b_i'].reshape(4, 64)
    o['mla_out_g'] = G['mla_out_g'].reshape(HEADS, SLAB)[:, :VDIM].reshape(-1)
    for n in ('ssm_dt_bias', 'ssm_a_log', 'ssm_d'):
        o[n] = G[n][0, :4]
    o['ssm_conv_w'], o['lru_conv_w'] = G['conv_w'][:4, :512], G['conv_w'][:4, 512:]
    o['ssm_conv_b'], o['lru_conv_b'] = G['conv_b'][0, :512], G['conv_b'][0, 512:]
    for n in ('lru_w_a', 'lru_w_i'):
        o[n] = jnp.stack([G[n][64 * k:64 * (k + 1), 64 * k:64 * (k + 1)] for k in range(4)])
    return o


def _rope_tables(positions):
    half = ROPE // 2
    inv_freq = ROPE_THETA ** (-jnp.arange(half, dtype=F32) * 2.0 / ROPE)
    ang = positions.astype(F32)[:, None] * inv_freq
    cos, sin = jnp.cos(ang), jnp.sin(ang)
    T = positions.shape[0]
    z = lambda k: jnp.zeros((T, k), F32)
    ck = jnp.concatenate([cos, cos, z(96)], axis=1)
    sk = jnp.concatenate([-sin, sin, z(96)], axis=1)
    cq = jnp.concatenate([jnp.ones((T, NOPE), F32), cos, cos, z(32)], axis=1)
    sq = jnp.concatenate([z(NOPE), -sin, sin, z(32)], axis=1)
    return ck, sk, cq * _ATT_SCALE2, sq * _ATT_SCALE2


def _add_epi(acc, res):
    return (acc + res,)


def _add_norm_epi(acc, res, g):
    x = acc + res
    return x, _rms(x, g, x.shape[-1])


def _norm_bwd_epi(acc, x, res, g):
    _, vjp = jax.vjp(lambda xv, gv: _rms(xv, gv, xv.shape[-1]), x, g)
    dx, dg = vjp(acc)
    dx = dx + res
    return dx, dx, dg


def _relu2_epi(acc):
    r = jnp.maximum(acc, 0.0)
    return r, r * r


def _drelu2_epi(acc, r):
    return (acc * (2.0 * r.astype(F32)),)


def _norm(x, g, name):
    return _rows_fwd(_f_norm, [x], [g], [(x.shape[1], BF16)], name=name)[0]


def _norm_bwd(x, g, ct, add, name):
    (dx, dx16), (dg,) = _rows_vjp(_f_norm, [x], [g], [ct], name=name, drows=[0], dparams=[0], drow_dtypes=[F32],
                                  add=add, twin=True)
    return dx, dx16, dg


def _layer_fwd(x0, h1, mem, P, tabs, g_next=None, send=None):
    ck, sk, cq, sq = tabs
    S = {'x0': x0}
    if h1 is None:
        h1 = _norm(x0, P['mix_norm_g'], "norm_mix")
    S['h1'] = h1
    win = P['win']
    u_mla = S['u_mla'] = _mm(h1, win[:, 0:U_MLA], name="in_mla")
    u_gate = S['u_gate'] = _mm(h1, win[:, U_MLA:U_MLA + U_GATE], name="in_gate")
    u_conv = S['u_conv'] = _mm(h1, win[:, U_MLA + U_GATE:], name="in_conv")
    cqn, akv = _rows_fwd(_f_mla_prep, [u_mla, ck, sk], [P['mla_q_norm_g'], P['mla_kv_norm_g']],
                         [(Q_RANK, BF16), (2 * SLAB, BF16)], name="mla_prep")
    S['cqn'], S['akv'] = cqn, akv
    yq = _mm(cqn, P['wq'], name="q_proj")
    q = S['q'] = _rows_fwd(_f_qrope, [yq, cq, sq], [], [(HEADS * SLAB, BF16)], name="q_rope")[0]
    kv = S['kv'] = _mm(akv, P['wkv'], name="kv_proj", out_dtypes=(BF16,))
    o, lse, *got = _attn_fwd(q, kv, name="attn_fwd" if send is None else "attn_fwd_gather", send=send)
    S['o'], S['lse'] = o, lse
    c_ssm, c_lru = _conv_fwd(u_conv, P['conv_w'], P['conv_b'], name="conv_fwd")
    S['c_ssm'], S['c_lru'] = c_ssm, c_lru
    ys, sall = _ssd_fwd(c_ssm, u_gate, P['ssm_dt_bias'], P['ssm_a_log'], P['ssm_d'], P['ssm_norm_g'], name="ssd_fwd")
    S['ys'], S['sall'] = ys, sall
    a, b = _rows_fwd(_f_lru_gates, [c_lru], [P['lru_w_a'], P['lru_b_a'], P['lru_w_i'], P['lru_b_i'], P['lru_lambda']],
                     [(256, F32), (256, F32)], name="lru_gates", tm=_MM_ROWS)
    h, hprev = _lru_scan_fwd(a, b, name="lru_scan")
    S['a'], S['h'], S['hprev'] = a, h, hprev
    ymix = S['ymix'] = _rows_fwd(_f_mix, [o, ys, h, u_gate], [P['mla_out_g'], P['lru_out_g']],
                                 [(HEADS * SLAB + 512, BF16)], name="mix")[0]
    x1, hx = _mm(ymix, P['wout'], name="out_proj", epi=_add_norm_epi, extras=(x0, P['xattn_norm_g']), out_dtypes=(F32, BF16))
    S['x1'], S['hx'] = x1, hx
    qx = S['qx'] = _mm(hx, P['w_mq'], name="mem_q", out_dtypes=(BF16,))
    mn = S['mn'] = _norm(mem, P['mem_norm_g'], "norm_mem")
    kx = S['kx'] = _mm(mn, P['w_mk'], name="mem_k", out_dtypes=(BF16,))
    vx = S['vx'] = _mm(mn, P['w_mv'], name="mem_v", out_dtypes=(BF16,))
    ox = S['ox'] = _rows_fwd(_f_xattn, [qx], [kx, vx], [(D, BF16)], name="xattn", tm=_MM_ROWS)[0]
    x2, hm = _mm(ox, P['w_mo'], name="mem_o", epi=_add_norm_epi, extras=(x1, P['mlp_norm_g']), out_dtypes=(F32, BF16))
    S['x2'], S['hm'] = x2, hm
    r, s = _mm(hm, P['w_mlp1'], "nt", name="mlp_up", epi=_relu2_epi, out_dtypes=(BF16, BF16))
    S['r'], S['s'] = r, s
    if g_next is None:
        x3, h_next = _mm(s, P['w_mlp2'], name="mlp_down_last", epi=_add_epi, extras=(x2,)), None
    else:
        x3, h_next = _mm(s, P['w_mlp2'], name="mlp_down", epi=_add_norm_epi, extras=(x2, g_next), out_dtypes=(F32, BF16))
    return x3, h_next, S, (got[0] if got else None)


def _layer_bwd(dx3, dx3h, mem, S, P, tabs, send=None):
    ck, sk, cq, sq = tabs
    G = {}
    da = _mm(dx3h, P['w_mlp2'], "nt", name="mlp_down_dx", epi=_drelu2_epi, extras=(S['r'],), out_dtypes=(BF16,))
    G['w_mlp2'] = _mm(S['s'], dx3h, "tn", name="mlp_down_dw")
    G['w_mlp1'] = _mm(da, S['hm'], "tn", name="mlp_up_dw")
    norm_out = dict(epi=_norm_bwd_epi, out_dtypes=(F32, BF16), col_sums=1)
    dx2, dx2h, G['mlp_norm_g'] = _mm(da, P['w_mlp1'], "nn", name="mlp_up_dx", extras=(S['x2'], dx3, P['mlp_norm_g']),
                                     **norm_out)
    dox = _mm(dx2h, P['w_mo'], "nt", name="mem_o_dx")
    G['w_mo'] = _mm(S['ox'], dx2h, "tn", name="mem_o_dw")
    (dqx,), (dkx, dvx) = _rows_vjp(_f_xattn, [S['qx']], [S['kx'], S['vx']], [dox], name="xattn_bwd", drows=[0],
                                   dparams=[0, 1], drow_dtypes=[BF16], tm=_MM_ROWS)
    G['w_mq'] = _mm(S['hx'], dqx, "tn", name="mem_q_dw")
    dx1, dx1h, G['xattn_norm_g'] = _mm(dqx, P['w_mq'], "nt", name="mem_q_dx", extras=(S['x1'], dx2, P['xattn_norm_g']),
                                       **norm_out)
    G['w_mk'] = _mm(S['mn'], dkx, "tn", name="mem_k_dw")
    G['w_mv'] = _mm(S['mn'], dvx, "tn", name="mem_v_dw")
    dmn = _mm(dkx, P['w_mk'], "nt", name="mem_k_dx", epi=_add_epi, extras=(_mm(dvx, P['w_mv'], "nt", name="mem_v_dx"),))
    _, _, G['mem_norm_g'] = _norm_bwd(mem, P['mem_norm_g'], dmn, None, "norm_mem_bwd")
    dymix = _mm(dx1h, P['wout'], "nt", name="out_proj_dx")
    G['wout'] = _mm(S['ymix'], dx1h, "tn", name="out_proj_dw")
    (do, dys, dh, dug_mix), (G['mla_out_g'], G['lru_out_g']) = _rows_vjp(
        _f_mix, [S['o'], S['ys'], S['h'], S['u_gate']], [P['mla_out_g'], P['lru_out_g']], [dymix], name="mix_bwd",
        drows=[0, 1, 2, 3], dparams=[0, 1], drow_dtypes=[BF16, F32, F32, F32])
    g, da_lru = _lru_scan_bwd(S['a'], dh, S['hprev'], name="lru_scan_bwd")
    lru_par = [P['lru_w_a'], P['lru_b_a'], P['lru_w_i'], P['lru_b_i'], P['lru_lambda']]
    (dc_lru,), dpar = _rows_vjp(_f_lru_gates, [S['c_lru']], lru_par, [da_lru, g], name="lru_gates_bwd", drows=[0],
                                dparams=[0, 1, 2, 3, 4], drow_dtypes=[F32], tm=_MM_ROWS)
    G['lru_w_a'], G['lru_b_a'], G['lru_w_i'], G['lru_b_i'], G['lru_lambda'] = dpar
    dc_ssm, dug_ssd, G['ssm_dt_bias'], G['ssm_a_log'], G['ssm_d'], G['ssm_norm_g'] = _ssd_bwd(
        S['c_ssm'], S['u_gate'], S['sall'], dys, P['ssm_dt_bias'], P['ssm_a_log'], P['ssm_d'], P['ssm_norm_g'],
        name="ssd_bwd")
    du_conv, G['conv_w'], G['conv_b'] = _conv_bwd(S['u_conv'], dc_ssm, dc_lru, P['conv_w'], name="conv_bwd")
    delta = _attn_delta(do, S['o'], name="attn_delta")
    dqt, dkv, *got = _attn_bwd(S['q'], S['kv'], do, S['lse'], delta,
                               name="attn_bwd" if send is None else "attn_bwd_scatter", send=send)
    dyq = _qrope_bwd(dqt, cq, sq, name="q_rope_bwd")
    dcqn = _mm(dyq, P['wq'], "nt", name="q_proj_dx")
    G['wq'] = _mm(S['cqn'], dyq, "tn", name="q_proj_dw")
    dakv = _mm(dkv, P['wkv'], "nt", name="kv_proj_dx")
    G['wkv'] = _mm(S['akv'], dkv, "tn", name="kv_proj_dw")
    (du_mla,), (G['mla_q_norm_g'], G['mla_kv_norm_g']) = _rows_vjp(
        _f_mla_prep, [S['u_mla'], ck, sk], [P['mla_q_norm_g'], P['mla_kv_norm_g']], [dcqn, dakv], name="mla_prep_bwd",
        drows=[0], dparams=[0, 1], drow_dtypes=[BF16])
    du = jnp.concatenate([du_mla, (dug_mix + dug_ssd).astype(BF16), du_conv.astype(BF16)], axis=1)
    G['win'] = _mm(S['h1'], du, "tn", name="in_dw")
    dx0, dx0h, G['mix_norm_g'] = _mm(du, P['win'], "nt", name="in_dx", extras=(S['x0'], dx1, P['mix_norm_g']), **norm_out)
    return dx0, dx0h, G, (got[0] if got else None)


class _NoExchange:
    def __init__(self, layers):
        self.layers = layers

    def pieces(self, l):
        return self.layers[l]

    def fwd_send(self, l):
        return None

    def fwd_got(self, l, got):
        pass

    def bwd_send(self, l):
        return None

    def bwd_got(self, l, got):
        pass

    def grads_ready(self, l, pieces):
        pass


def _local_step(x, mem, positions, ex, Ws, tgt):
    tabs = _rope_tables(positions)
    saved, preps, h = [], [], None
    for l in range(DEPTH):
        P = _prep_layer(ex.pieces(l), Ws, l)
        send = ex.fwd_send(l)
        g_next = Ws['mix_norm_g'][l + 1].astype(F32).reshape(1, D) if l + 1 < DEPTH else None
        x, h, S, got = _layer_fwd(x, h, mem, P, tabs, g_next, send)
        if send is not None:
            ex.fwd_got(l, got)
        saved.append(S)
        preps.append(P)
    loss, dx, dxh, dg_final = _loss_head(x, tgt, Ws['final_norm_g'].astype(F32).reshape(1, D), name="loss_head")
    pieces, small = [None] * DEPTH, [None] * DEPTH
    for l in reversed(range(DEPTH)):
        send = ex.bwd_send(l)
        dx, dxh, G, got = _layer_bwd(dx, dxh, mem, saved[l], preps[l], tabs, send)
        if send is not None:
            ex.bwd_got(l, got)
        pieces[l], small[l] = _unprep_pieces(G), _unprep_small(G)
        ex.grads_ready(l, pieces[l])
    grads = {n: jnp.stack([small[l][n] for l in range(DEPTH)]) for n in SMALL if n != 'final_norm_g'}
    grads['final_norm_g'] = dg_final.reshape(D)
    return loss, dx, pieces, grads


def _pack_rows(pieces):
    return jnp.concatenate([pieces[n].reshape(pieces[n].shape[:-2] + (-1, PACK_C)) for n in PIECES], axis=-2)


def _unpack_rows(packed, lead=()):
    out, off = {}, 0
    for n in PIECES:
        rows = _piece_rows(n)
        out[n] = packed[..., off:off + rows, :].reshape(lead + PIECE_SHAPE[n])
        off += rows
    return out


class _StepExchange(_NoExchange):
    def __init__(self, shard):
        self.shard = {n: a.astype(BF16) for n, a in shard.items()}
        self.layers = {}
        self.sums, self.reduced = {}, {}
        first = _pack_rows({n: a[0] for n, a in self.shard.items()})
        self._take(*_finish_gather(first, _chip_exchange(first, True, name="gather_w0_chips"), name="gather_w0"), [0])

    def _take(self, mine, other, layers):
        mine, other = (g.reshape(4, len(layers), -1, PACK_C) for g in (mine, other))
        for k, l in enumerate(layers):
            a, b = _unpack_rows(mine[:, k], lead=(4,)), _unpack_rows(other[:, k], lead=(4,))
            self.layers[l] = {n: _assemble(n, _by_core(a[n], b[n])) for n in PIECES}

    def _rest(self):
        return _pack_rows({n: a[1:] for n, a in self.shard.items()}).reshape(-1, PACK_C)

    def fwd_send(self, l):
        return self._rest() if l == 0 else None

    def fwd_got(self, l, got):
        self._take(*_finish_gather(self._rest(), got, name="gather_w"), list(range(1, DEPTH)))

    def grads_ready(self, l, pieces):
        x = _pack_rows({n: _disassemble(n, pieces[n]) for n in PIECES})
        if l == 0:
            parts = _chip_exchange(_chip_sums(x, name="scatter_g0"), False, name="scatter_g0_chips")
            self.reduced[0] = _sum_fixed(parts, F32, name="scatter_g0_sum")
        else:
            self.sums[l] = x.transpose(1, 0, 2, 3).astype(BF16).reshape(4, -1, PACK_C)

    def bwd_send(self, l):
        return self.sums.pop(l + 1, None)

    def bwd_got(self, l, got):
        both = _sum_fixed(got, F32, name="scatter_g_sum").reshape(2, -1, PACK_C)
        other = _core_exchange(both, True, name="scatter_g_cores")
        self.reduced[l + 1] = _add_own_half(both, other, name="scatter_g_add", out_dtype=F32)


def _adamw_nd(w, g, m, v, name):
    shp = w.shape
    two = lambda a: a.reshape(-1, shp[-1])
    return [r.reshape(shp) for r in _adamw(two(w), two(g), two(m), two(v), name=name)]


def kernel(x, mem, positions, mix_norm_g, w_in, mla_q_norm_g, mla_kv_norm_g, mla_w_uq, mla_w_ukv, mla_out_g, ssm_conv_w, ssm_conv_b, ssm_dt_bias, ssm_a_log, ssm_d, ssm_norm_g, lru_conv_w, lru_conv_b, lru_w_a, lru_b_a, lru_w_i, lru_b_i, lru_lambda, lru_out_g, w_out, xattn_norm_g, mem_norm_g, w_mq, w_mk, w_mv, w_mo, mlp_norm_g, w_mlp1, w_mlp2, final_norm_g, loss_target, m_mix_norm_g, m_w_in, m_mla_q_norm_g, m_mla_kv_norm_g, m_mla_w_uq, m_mla_w_ukv, m_mla_out_g, m_ssm_conv_w, m_ssm_conv_b, m_ssm_dt_bias, m_ssm_a_log, m_ssm_d, m_ssm_norm_g, m_lru_conv_w, m_lru_conv_b, m_lru_w_a, m_lru_b_a, m_lru_w_i, m_lru_b_i, m_lru_lambda, m_lru_out_g, m_w_out, m_xattn_norm_g, m_mem_norm_g, m_w_mq, m_w_mk, m_w_mv, m_w_mo, m_mlp_norm_g, m_w_mlp1, m_w_mlp2, m_final_norm_g, v_mix_norm_g, v_w_in, v_mla_q_norm_g, v_mla_kv_norm_g, v_mla_w_uq, v_mla_w_ukv, v_mla_out_g, v_ssm_conv_w, v_ssm_conv_b, v_ssm_dt_bias, v_ssm_a_log, v_ssm_d, v_ssm_norm_g, v_lru_conv_w, v_lru_conv_b, v_lru_w_a, v_lru_b_a, v_lru_w_i, v_lru_b_i, v_lru_lambda, v_lru_out_g, v_w_out, v_xattn_norm_g, v_mem_norm_g, v_w_mq, v_w_mk, v_w_mv, v_w_mo, v_mlp_norm_g, v_w_mlp1, v_w_mlp2, v_final_norm_g):
    a = locals()
    w = {n: a[n] for n in WEIGHTS}
    m = {n: a['m_' + n] for n in WEIGHTS}
    v = {n: a['v_' + n] for n in WEIGHTS}
    me = 4 * lax.axis_index("x") + 2 * lax.axis_index("y") + lax.axis_index("c")

    ex = _StepExchange({n: _K_FWD.get(n, lambda a: a)(w[PIECE_SOURCE.get(n, n)]) for n in PIECES})
    Ws = {}
    conv_shapes = [w[n].shape for n in CONV_SHARDED]
    conv_g = _all_gather(_pack([w[n] for n in CONV_SHARDED], F32), name="gather_conv")
    conv_g = conv_g.transpose(1, 0, 2, 3).reshape((N_DEV,) + conv_g.shape[2:])
    for n, g in zip(CONV_SHARDED, _unpack(conv_g, conv_shapes, lead=(N_DEV,))):
        Ws[n] = g.transpose(1, 2, 0, 3).reshape(g.shape[1], g.shape[2], N_DEV * g.shape[3])
    for n in SMALL:
        if n not in CONV_SHARDED:
            Ws[n] = w[n]

    loss_share, dx, _, grads = _local_step(x[0], mem[0], positions[0], ex, Ws, loss_target[0])
    loss = lax.psum(loss_share[0, 0], ("x", "y", "c"))

    g_out = {}
    for n in PIECES:
        g = jnp.stack([_unpack_rows(ex.reduced[l])[n] for l in range(DEPTH)])
        g_out[PIECE_SOURCE.get(n, n)] = _K_INV[n](g) if n in _K_INV else g
    small_shapes = [grads[n].shape for n in SMALL]
    g_small = _all_reduce(_pack([grads[n] for n in SMALL], F32), name="reduce_g")
    for n, g in zip(SMALL, _unpack(g_small, small_shapes)):
        if n in CONV_SHARDED:
            cols = w[n].shape[-1]
            g = lax.dynamic_slice_in_dim(g, me * cols, cols, axis=2)
        g_out[n] = g

    delta, new_m, new_v = {}, {}, {}
    for n in BIG:
        delta[n], new_m[n], new_v[n] = _adamw_nd(w[n], g_out[n], m[n], v[n], "adamw_" + n)
    shapes = [w[n].shape for n in SMALL]
    packed = [_pack([d[n] for n in SMALL], F32) for d in (w, g_out, m, v)]
    for d, res in zip((delta, new_m, new_v), _adamw(*packed, name="adamw_small")):
        d.update(zip(SMALL, _unpack(res, shapes)))

    return (loss, dx[None], *[g_out[n] for n in WEIGHTS], *[delta[n] for n in WEIGHTS],
            *[new_m[n] for n in WEIGHTS], *[new_v[n] for n in WEIGHTS])
```

```python
import functools
import math

import jax
import jax.numpy as jnp
from jax import lax
from jax.experimental import pallas as pl
from jax.experimental.pallas import tpu as pltpu

F32, BF16 = jnp.float32, jnp.bfloat16

D = 1024
DEPTH = 4
N_MEM = 256
EPS = 1e-6
HEADS = 8
NOPE, ROPE, VDIM = 64, 32, 64
Q_RANK, KV_RANK = 256, 128
ROPE_THETA = 10000.0
SSM_CHUNK = 128
LRU_C = 8.0
MEM_HEADS = 4
D_FF = 4 * D
SLAB = 128
LR, B1, B2, AEPS, WD, STEP = 0.001, 0.9, 0.999, 1e-08, 0.01, 10

N_DEV = 8
MESH = pl.DeviceIdType.MESH

U_MLA = 640
U_GATE = 640
U_CONV = 768

_DN = {"nn": (((1,), (0,)), ((), ())), "nt": (((1,), (1,)), ((), ())), "tn": (((0,), (0,)), ((), ()))}


def _dot(a, b, kind):
    return lax.dot_general(a.astype(BF16), b.astype(BF16), _DN[kind], preferred_element_type=F32)


@functools.partial(jax.custom_vjp, nondiff_argnums=(2,))
def _bdot(a, b, kind):
    return _dot(a, b, kind)


def _bdot_fwd(a, b, kind):
    return _dot(a, b, kind), (a, b)


def _bdot_bwd(kind, res, g):
    a, b = res
    if kind == "nn":
        da, db = _dot(g, b, "nt"), _dot(a, g, "tn")
    elif kind == "nt":
        da, db = _dot(g, b, "nn"), _dot(g, a, "tn")
    else:
        da, db = _dot(b, g, "nt"), _dot(a, g, "nn")
    return da.astype(a.dtype), db.astype(b.dtype)


_bdot.defvjp(_bdot_fwd, _bdot_bwd)


def _tile(n, pref):
    if n <= pref:
        return n
    t = pref
    while n % t:
        t -= SLAB
    return t


def _cparams(sem, vmem_mb=48):
    return pltpu.CompilerParams(dimension_semantics=sem, vmem_limit_bytes=vmem_mb * 1024 * 1024)


def _mm(a, b, kind="nn", *, name, out_dtypes=(F32,), epi=None, extras=(), col_sums=0, tm=1024, tn=1024, tk=1024):
    if kind == "tn":
        K, M = a.shape
    else:
        M, K = a.shape
    N = b.shape[0] if kind == "nt" else b.shape[1]
    if a.dtype == F32 or b.dtype == F32:
        tk = tk // 2
    tm, tn, tk = _tile(M, tm), _tile(N, tn), _tile(K, tk)
    assert not col_sums or tn == N, "column sums need one tile across the columns"
    nk = K // tk
    a_spec = pl.BlockSpec((tk, tm), lambda i, j, k: (k, i)) if kind == "tn" else pl.BlockSpec((tm, tk), lambda i, j, k: (i, k))
    b_spec = pl.BlockSpec((tn, tk), lambda i, j, k: (j, k)) if kind == "nt" else pl.BlockSpec((tk, tn), lambda i, j, k: (k, j))
    o_spec = pl.BlockSpec((tm, tn), lambda i, j, k: (i, j))
    vec_spec = pl.BlockSpec((1, tn), lambda i, j, k: (0, j))
    ex_specs = [vec_spec if e.shape[0] == 1 else o_spec for e in extras]
    n_ex, n_out = len(extras), len(out_dtypes)

    def body(*refs):
        a_ref, b_ref = refs[:2]
        ex = refs[2:2 + n_ex]
        outs = refs[2 + n_ex:2 + n_ex + n_out]
        acc = refs[-1]
        k = pl.program_id(2)
        first_row_tile = pl.program_id(0) == 0

        def finish(r):
            res = epi(r, *[e[...] for e in ex]) if epi is not None else (r,)
            for o, v in zip(outs, res):
                o[...] = v.astype(o.dtype)
            sums = refs[2 + n_ex + n_out:2 + n_ex + n_out + col_sums]

            @pl.when(first_row_tile)
            def _():
                for o in sums:
                    o[...] = jnp.zeros_like(o)

            for o, v in zip(sums, res[n_out:]):
                o[...] += v

        if nk == 1:
            finish(_dot(a_ref[...], b_ref[...], kind))
            return

        @pl.when(k == 0)
        def _():
            acc[...] = _dot(a_ref[...], b_ref[...], kind)

        @pl.when(k > 0)
        def _():
            acc[...] += _dot(a_ref[...], b_ref[...], kind)

        @pl.when(k == nk - 1)
        def _():
            finish(acc[...])

    res = pl.pallas_call(
        body,
        name=name,
        grid=(M // tm, N // tn, nk),
        in_specs=[a_spec, b_spec] + ex_specs,
        out_specs=[o_spec] * n_out + [vec_spec] * col_sums,
        out_shape=[jax.ShapeDtypeStruct((M, N), dt) for dt in out_dtypes] + [jax.ShapeDtypeStruct((1, N), F32)] * col_sums,
        scratch_shapes=[pltpu.VMEM((tm, tn), F32)],
        compiler_params=_cparams(("arbitrary" if col_sums else "parallel", "parallel", "arbitrary")),
    )(a, b, *extras)
    return res[0] if n_out + col_sums == 1 else res


_MM_ROWS = 1024


def _row_spec(arr, tm):
    return pl.BlockSpec((tm, arr.shape[1]), lambda i: (i, 0))


def _full_spec(arr):
    nd = arr.ndim
    return pl.BlockSpec(arr.shape, lambda i: (0,) * nd)


def _rows_fwd(fn, rows, params, outs, *, name, tm=256):
    T = rows[0].shape[0]
    tm = min(tm, T)
    nr, npar = len(rows), len(params)

    def body(*refs):
        ins = [r[...] for r in refs[:nr + npar]]
        res = fn(*ins)
        for o, v in zip(refs[nr + npar:], res):
            o[...] = v.astype(o.dtype)

    res = pl.pallas_call(
        body,
        name=name,
        grid=(T // tm,),
        in_specs=[_row_spec(r, tm) for r in rows] + [_full_spec(p) for p in params],
        out_specs=[pl.BlockSpec((tm, c), lambda i: (i, 0)) for c, _ in outs],
        out_shape=[jax.ShapeDtypeStruct((T, c), dt) for c, dt in outs],
        compiler_params=_cparams(("parallel",)),
    )(*rows, *params)
    return res


def _rows_vjp(fn, rows, params, cts, *, name, drows, dparams, drow_dtypes, add=None, twin=False, tm=256):
    T = rows[0].shape[0]
    tm = min(tm, T)
    nr, npar, nct = len(rows), len(params), len(cts)
    n_add = 0 if add is None else 1
    n_dr, n_dp = len(drows), len(dparams)
    n_tw = 1 if twin else 0

    def body(*refs):
        row_t = [r[...] for r in refs[:nr]]
        par_t = [r[...] for r in refs[nr:nr + npar]]
        ct_t = [r[...].astype(F32) for r in refs[nr + npar:nr + npar + nct]]
        pos = nr + npar + nct
        add_t = refs[pos][...] if n_add else None
        pos += n_add
        drow_refs = refs[pos:pos + n_dr]
        dpar_refs = refs[pos + n_dr:pos + n_dr + n_dp]

        def g(*dargs):
            rr, pp = list(row_t), list(par_t)
            for idx, v in zip(drows, dargs[:n_dr]):
                rr[idx] = v
            for idx, v in zip(dparams, dargs[n_dr:]):
                pp[idx] = v
            return tuple(fn(*rr, *pp))

        prim = [row_t[i].astype(F32) for i in drows] + [par_t[i].astype(F32) for i in dparams]
        _, vjp = jax.vjp(g, *prim)
        grads = vjp(tuple(ct_t))
        for n, (o, v) in enumerate(zip(drow_refs, grads[:n_dr])):
            if n == 0 and n_add:
                v = v + add_t.astype(F32)
            o[...] = v.astype(o.dtype)
            if n == 0 and n_tw:
                refs[-1][...] = v.astype(BF16)

        @pl.when(pl.program_id(0) == 0)
        def _():
            for o in dpar_refs:
                o[...] = jnp.zeros_like(o)

        for o, v in zip(dpar_refs, grads[n_dr:]):
            o[...] += v

    res = pl.pallas_call(
        body,
        name=name,
        grid=(T // tm,),
        in_specs=[_row_spec(r, tm) for r in rows] + [_full_spec(p) for p in params] + [_row_spec(c, tm) for c in cts]
        + ([_row_spec(add, tm)] if n_add else []),
        out_specs=[_row_spec(rows[i], tm) for i in drows] + [_full_spec(params[i]) for i in dparams]
        + [_row_spec(rows[drows[0]], tm)] * n_tw,
        out_shape=[jax.ShapeDtypeStruct(rows[i].shape, dt) for i, dt in zip(drows, drow_dtypes)]
        + [jax.ShapeDtypeStruct(params[i].shape, F32) for i in dparams]
        + [jax.ShapeDtypeStruct(rows[drows[0]].shape, BF16)] * n_tw,
        compiler_params=_cparams(("arbitrary",)),
    )(*rows, *params, *cts, *([add] if n_add else []))
    return list(res[:n_dr]) + list(res[n_dr + n_dp:]), list(res[n_dr:n_dr + n_dp])


def _rms(x, g, n):
    return x * lax.rsqrt(jnp.sum(x * x, axis=-1, keepdims=True) * (1.0 / n) + EPS) * g


def _sigmoid(x):
    return 1.0 / (1.0 + jnp.exp(-x))


def _silu(x):
    return x * _sigmoid(x)


def _softplus(x):
    return jnp.maximum(x, 0.0) + jnp.log(1.0 + jnp.exp(-jnp.abs(x)))


def _gelu_tanh(x):
    return 0.5 * x * (1.0 + jnp.tanh(math.sqrt(2.0 / math.pi) * (x + 0.044715 * x * x * x)))


def _lane(shape):
    return lax.broadcasted_iota(jnp.int32, shape, len(shape) - 1)


def _col(x, h):
    return jnp.sum(jnp.where(_lane(x.shape) == h, x, 0.0), axis=-1, keepdims=True)


def _f_norm(x, g):
    return (_rms(x.astype(F32), g, x.shape[-1]),)


def _f_mla_prep(u, ck, sk, gq, gkv):
    u = u.astype(F32)
    cq = _rms(u[:, 0:256], gq, Q_RANK)
    ckv = _rms(u[:, 256:384], gkv, KV_RANK)
    kr = u[:, 384:512] * ck + u[:, 512:640] * sk
    return cq, jnp.concatenate([ckv, kr], axis=1)


def _f_qrope(y, cq, sq):
    y = y.astype(F32)
    c8, s8 = jnp.tile(cq, (1, HEADS)), jnp.tile(sq, (1, HEADS))
    return (y[:, :HEADS * SLAB] * c8 + y[:, HEADS * SLAB:] * s8,)


def _f_lru_gates(xc, wa, ba, wi, bi, lam):
    xc = xc.astype(F32)
    r = _sigmoid(_bdot(xc, wa, "nn") + ba)
    i = _sigmoid(_bdot(xc, wi, "nn") + bi)
    log_a = -LRU_C * r * _softplus(-lam)
    a = jnp.exp(log_a)
    x2 = 2.0 * log_a
    m1 = jnp.where(x2 > -0.02, -x2 * (1.0 + x2 * (0.5 + x2 * (1.0 / 6.0 + x2 * (1.0 / 24.0)))), 1.0 - jnp.exp(x2))
    return a, jnp.sqrt(m1) * (i * xc)


def _f_mix(o, ys, h, ug, g_mla, g_lru):
    o = o.astype(F32)
    y_mla = _rms(o, g_mla, HEADS * VDIM)
    y_lru = _rms(h.astype(F32) * _gelu_tanh(ug[:, 256:512].astype(F32)), g_lru, 256)
    return (jnp.concatenate([y_mla, ys.astype(F32), y_lru], axis=1),)


def _f_xattn(q, k, v):
    hd = D // MEM_HEADS
    outs = []
    for h in range(MEM_HEADS):
        sl = slice(h * hd, (h + 1) * hd)
        s = _bdot(q[:, sl], k[:, sl], "nt") * (1.0 / math.sqrt(hd))
        s = s - jnp.max(s, axis=-1, keepdims=True)
        p = jnp.exp(s)
        p = p / jnp.sum(p, axis=-1, keepdims=True)
        outs.append(_bdot(p, v[:, sl], "nn"))
    return (jnp.concatenate(outs, axis=1),)


def _split_dot(tri, a, kind):
    a_hi = a.astype(BF16)
    r1 = a - a_hi.astype(F32)
    a_mid = r1.astype(BF16)
    a_lo = (r1 - a_mid.astype(F32)).astype(BF16)
    return _dot(tri, a_hi, kind) + _dot(tri, a_mid, kind) + _dot(tri, a_lo, kind)


@jax.custom_vjp
def _tri_cumsum(tri, a):
    return _split_dot(tri, a, "nn")


def _tri_cumsum_fwd(tri, a):
    return _split_dot(tri, a, "nn"), tri


def _tri_cumsum_bwd(tri, g):
    return jnp.zeros_like(tri), _split_dot(tri, g, "tn")


_tri_cumsum.defvjp(_tri_cumsum_fwd, _tri_cumsum_bwd)


def _f_ssd_chunk(c, ug, s0, s1, dtb, alog, dsk, ng):
    L = c.shape[0]
    c = c.astype(F32)
    xbc = _silu(c)
    xs, bm, cm = xbc[:, 0:256], xbc[:, 256:384], xbc[:, 384:512]
    z = ug[:, 0:256].astype(F32)
    dt = _softplus(ug[:, 512:640].astype(F32) + dtb)
    a = dt * (-jnp.exp(alog))
    rowi = lax.broadcasted_iota(jnp.int32, (L, L), 0)
    coli = lax.broadcasted_iota(jnp.int32, (L, L), 1)
    tril = rowi >= coli
    acum = _tri_cumsum(tril.astype(BF16), a)
    acum_t = acum.T
    lane = _lane((1, SLAB))
    lo = lane < 64
    ys, new_s = [], []
    for g in range(2):
        gm = (lane >= 64 * g) & (lane < 64 * g + 64)
        bg, cg = jnp.where(gm, bm, 0.0), jnp.where(gm, cm, 0.0)
        cb = _bdot(cg, bg, "nt")
        x = xs[:, SLAB * g:SLAB * (g + 1)]
        h0, h1 = 2 * g, 2 * g + 1
        ac0, ac1 = _col(acum, h0), _col(acum, h1)
        xdt = x * jnp.where(lo, _col(dt, h0), _col(dt, h1))
        ac_l = jnp.where(lo, ac0, ac1)
        tot = acum[L - 1:L, :]
        tot_l = jnp.where(lo, _col(tot, h0), _col(tot, h1))
        yd = jnp.zeros((L, SLAB), F32)
        for hh, acc, hm in ((h0, ac0, lo), (h1, ac1, jnp.logical_not(lo))):
            seg = acc - acum_t[hh:hh + 1, :]
            lm = jnp.where(tril, jnp.exp(jnp.where(tril, seg, 0.0)), 0.0)
            yd = yd + _bdot(cb * lm, jnp.where(hm, xdt, 0.0), "nn")
        sg = (s0, s1)[g]
        y_off = _bdot(cg, sg, "nn") * jnp.exp(ac_l)
        st = _bdot(bg, xdt * jnp.exp(tot_l - ac_l), "tn")
        new_s.append(jnp.exp(tot_l) * sg + st)
        y = yd + y_off + jnp.where(lo, _col(dsk, h0), _col(dsk, h1)) * x
        y = y * _silu(z[:, SLAB * g:SLAB * (g + 1)])
        ys.append(_rms(y, ng[:, SLAB * g:SLAB * (g + 1)], SLAB))
    return jnp.concatenate(ys, axis=1), new_s[0], new_s[1]


_SSD_TILE = 512


def _ssd_fwd(c, ug, dtb, alog, dsk, ng, *, name):
    T = c.shape[0]
    tm = min(_SSD_TILE, T)
    ncs = tm // SSM_CHUNK
    nc = T // SSM_CHUNK

    def body(c_ref, ug_ref, dtb_ref, alog_ref, dsk_ref, ng_ref, y_ref, sall_ref, s_scr):
        @pl.when(pl.program_id(0) == 0)
        def _():
            s_scr[...] = jnp.zeros_like(s_scr)

        s0, s1 = s_scr[0], s_scr[1]
        for k in range(ncs):
            rows = slice(k * SSM_CHUNK, (k + 1) * SSM_CHUNK)
            sall_ref[k, 0] = s0
            sall_ref[k, 1] = s1
            y, s0, s1 = _f_ssd_chunk(c_ref[rows, :], ug_ref[rows, :], s0, s1, dtb_ref[...], alog_ref[...],
                                     dsk_ref[...], ng_ref[...])
            y_ref[rows, :] = y
        s_scr[0] = s0
        s_scr[1] = s1

    y, sall = pl.pallas_call(
        body,
        name=name,
        grid=(T // tm,),
        in_specs=[_row_spec(c, tm), _row_spec(ug, tm)] + [_full_spec(p) for p in (dtb, alog, dsk, ng)],
        out_specs=[pl.BlockSpec((tm, 256), lambda i: (i, 0)), pl.BlockSpec((ncs, 2, SLAB, SLAB), lambda i: (i, 0, 0, 0))],
        out_shape=[jax.ShapeDtypeStruct((T, 256), F32), jax.ShapeDtypeStruct((nc, 2, SLAB, SLAB), F32)],
        scratch_shapes=[pltpu.VMEM((2, SLAB, SLAB), F32)],
        compiler_params=_cparams(("arbitrary",)),
    )(c, ug, dtb, alog, dsk, ng)
    return y, sall


def _ssd_bwd(c, ug, sall, dy, dtb, alog, dsk, ng, *, name):
    T = c.shape[0]
    tm = min(_SSD_TILE, T)
    ncs = tm // SSM_CHUNK
    nt = T // tm

    def body(c_ref, ug_ref, sall_ref, dy_ref, dtb_ref, alog_ref, dsk_ref, ng_ref,
             dc_ref, dug_ref, ddtb_ref, dalog_ref, ddsk_ref, dng_ref, ds_scr):
        @pl.when(pl.program_id(0) == 0)
        def _():
            ds_scr[...] = jnp.zeros_like(ds_scr)
            for o in (ddtb_ref, dalog_ref, ddsk_ref, dng_ref):
                o[...] = jnp.zeros_like(o)

        ds0, ds1 = ds_scr[0], ds_scr[1]
        for k in reversed(range(ncs)):
            rows = slice(k * SSM_CHUNK, (k + 1) * SSM_CHUNK)
            prim = (c_ref[rows, :].astype(F32), ug_ref[rows, :].astype(F32), sall_ref[k, 0], sall_ref[k, 1],
                    dtb_ref[...], alog_ref[...], dsk_ref[...], ng_ref[...])
            _, vjp = jax.vjp(_f_ssd_chunk, *prim)
            dc, dug, ds0, ds1, g_dtb, g_alog, g_dsk, g_ng = vjp((dy_ref[rows, :].astype(F32), ds0, ds1))
            dc_ref[rows, :] = dc
            dug_ref[rows, :] = dug
            ddtb_ref[...] += g_dtb
            dalog_ref[...] += g_alog
            ddsk_ref[...] += g_dsk
            dng_ref[...] += g_ng
        ds_scr[0] = ds0
        ds_scr[1] = ds1

    rev = lambda i: (nt - 1 - i, 0)
    params = (dtb, alog, dsk, ng)
    res = pl.pallas_call(
        body,
        name=name,
        grid=(nt,),
        in_specs=[pl.BlockSpec((tm, c.shape[1]), rev), pl.BlockSpec((tm, ug.shape[1]), rev),
                  pl.BlockSpec((ncs, 2, SLAB, SLAB), lambda i: (nt - 1 - i, 0, 0, 0)), pl.BlockSpec((tm, 256), rev)]
        + [_full_spec(p) for p in params],
        out_specs=[pl.BlockSpec((tm, 512), rev), pl.BlockSpec((tm, U_GATE), rev)] + [_full_spec(p) for p in params],
        out_shape=[jax.ShapeDtypeStruct((T, 512), F32), jax.ShapeDtypeStruct((T, U_GATE), F32)]
        + [jax.ShapeDtypeStruct(p.shape, F32) for p in params],
        scratch_shapes=[pltpu.VMEM((2, SLAB, SLAB), F32)],
        compiler_params=_cparams(("arbitrary",)),
    )(c, ug, sall, dy, *params)
    return res


_CONV_TILE = 512
_HALO = 8
_CONV_W = 4


def _conv_fwd(u, w, b, *, name):
    T, C = u.shape
    tm = min(_CONV_TILE, T)
    hb = tm // _HALO

    def body(u_ref, prev_ref, w_ref, b_ref, y1_ref, y2_ref, ext):
        i = pl.program_id(0)
        ext[0:_HALO, :] = jnp.where(i > 0, prev_ref[...], 0.0)
        ext[_HALO:, :] = u_ref[...]
        y = jnp.broadcast_to(b_ref[...], (tm, C))
        for k in range(_CONV_W):
            y = y + ext[_HALO - (_CONV_W - 1) + k:_HALO - (_CONV_W - 1) + k + tm, :] * w_ref[k:k + 1, :]
        y1_ref[...] = y[:, 0:512]
        y2_ref[...] = y[:, 512:768]

    return pl.pallas_call(
        body,
        name=name,
        grid=(T // tm,),
        in_specs=[_row_spec(u, tm), pl.BlockSpec((_HALO, C), lambda i: (jnp.maximum(i * hb - 1, 0), 0)),
                  _full_spec(w), _full_spec(b)],
        out_specs=[pl.BlockSpec((tm, 512), lambda i: (i, 0)), pl.BlockSpec((tm, 256), lambda i: (i, 0))],
        out_shape=[jax.ShapeDtypeStruct((T, 512), F32), jax.ShapeDtypeStruct((T, 256), F32)],
        scratch_shapes=[pltpu.VMEM((tm + _HALO, C), F32)],
        compiler_params=_cparams(("parallel",)),
    )(u, u, w, b)


def _conv_bwd(u, dy1, dy2, w, *, name):
    T, C = u.shape
    tm = min(_CONV_TILE, T)
    hb = tm // _HALO
    nt = T // tm

    def body(u_ref, prev_ref, dy1_ref, next1_ref, dy2_ref, next2_ref, w_ref, du_ref, dw_ref, db_ref, ext, dext):
        i = pl.program_id(0)
        ext[0:_HALO, :] = jnp.where(i > 0, prev_ref[...], 0.0)
        ext[_HALO:, :] = u_ref[...]
        dext[0:tm, 0:512] = dy1_ref[...]
        dext[0:tm, 512:768] = dy2_ref[...]
        dext[tm:, 0:512] = jnp.where(i < nt - 1, next1_ref[...], 0.0)
        dext[tm:, 512:768] = jnp.where(i < nt - 1, next2_ref[...], 0.0)

        @pl.when(i == 0)
        def _():
            dw_ref[...] = jnp.zeros_like(dw_ref)
            db_ref[...] = jnp.zeros_like(db_ref)

        dy = dext[0:tm, :]
        du = jnp.zeros((tm, C), F32)
        for k in range(_CONV_W):
            du = du + dext[_CONV_W - 1 - k:_CONV_W - 1 - k + tm, :] * w_ref[k:k + 1, :]
            xk = ext[_HALO - (_CONV_W - 1) + k:_HALO - (_CONV_W - 1) + k + tm, :]
            dw_ref[k:k + 1, :] += jnp.sum(dy * xk, axis=0, keepdims=True)
        du_ref[...] = du
        db_ref[...] += jnp.sum(dy, axis=0, keepdims=True)

    nxt = lambda i: (jnp.minimum((i + 1) * hb, T // _HALO - 1), 0)
    return pl.pallas_call(
        body,
        name=name,
        grid=(nt,),
        in_specs=[_row_spec(u, tm), pl.BlockSpec((_HALO, C), lambda i: (jnp.maximum(i * hb - 1, 0), 0)),
                  _row_spec(dy1, tm), pl.BlockSpec((_HALO, 512), nxt), _row_spec(dy2, tm), pl.BlockSpec((_HALO, 256), nxt),
                  _full_spec(w)],
        out_specs=[pl.BlockSpec((tm, C), lambda i: (i, 0)), _full_spec(w), pl.BlockSpec((1, C), lambda i: (0, 0))],
        out_shape=[jax.ShapeDtypeStruct((T, C), F32), jax.ShapeDtypeStruct(w.shape, F32), jax.ShapeDtypeStruct((1, C), F32)],
        scratch_shapes=[pltpu.VMEM((tm + _HALO, C), F32), pltpu.VMEM((tm + _HALO, C), F32)],
        compiler_params=_cparams(("arbitrary",)),
    )(u, u, dy1, dy1, dy2, dy2, w)


_SCAN_TILE = 1024
_SUB = 8


def _shift_rows(x, d, fill, up):
    r = lax.broadcasted_iota(jnp.int32, x.shape, 0)
    if up:
        return jnp.where(r < _SUB - d, pltpu.roll(x, _SUB - d, 0), fill)
    return jnp.where(r >= d, pltpu.roll(x, d, 0), fill)


def _lru_scan_fwd(a, b, *, name):
    T, W = a.shape
    tr = min(_SCAN_TILE, T)

    def body(a_ref, b_ref, h_ref, hp_ref, carry):
        @pl.when(pl.program_id(0) == 0)
        def _():
            carry[...] = jnp.zeros_like(carry)

        def step(t, cr):
            rows = pl.ds(pl.multiple_of(t * _SUB, _SUB), _SUB)
            aa, bb = a_ref[rows, :], b_ref[rows, :]
            for d in (1, 2, 4):
                bb = bb + aa * _shift_rows(bb, d, 0.0, False)
                aa = aa * _shift_rows(aa, d, 1.0, False)
            h = bb + aa * cr
            h_ref[rows, :] = h
            r = lax.broadcasted_iota(jnp.int32, h.shape, 0)
            hp_ref[rows, :] = jnp.where(r >= 1, pltpu.roll(h, 1, 0), cr)
            return jnp.broadcast_to(h[_SUB - 1:_SUB, :], (_SUB, W))

        carry[...] = lax.fori_loop(0, tr // _SUB, step, carry[...])

    return pl.pallas_call(
        body,
        name=name,
        grid=(T // tr,),
        in_specs=[_row_spec(a, tr), _row_spec(b, tr)],
        out_specs=[pl.BlockSpec((tr, W), lambda i: (i, 0))] * 2,
        out_shape=[jax.ShapeDtypeStruct((T, W), F32)] * 2,
        scratch_shapes=[pltpu.VMEM((_SUB, W), F32)],
        compiler_params=_cparams(("arbitrary",)),
    )(a, b)


def _lru_scan_bwd(a, dh, hprev, *, name):
    T, W = a.shape
    tr = min(_SCAN_TILE, T)
    nt = T // tr

    def body(a_ref, dh_ref, hp_ref, g_ref, da_ref, carry):
        @pl.when(pl.program_id(0) == 0)
        def _():
            carry[...] = jnp.zeros_like(carry)

        nsub = tr // _SUB

        def step(s, cr):
            t = nsub - 1 - s
            rows = pl.ds(pl.multiple_of(t * _SUB, _SUB), _SUB)
            a_t = a_ref[rows, :]
            aa = _shift_rows(a_t, 1, 1.0, True)
            bb = dh_ref[rows, :]
            for d in (1, 2, 4):
                bb = bb + aa * _shift_rows(bb, d, 0.0, True)
                aa = aa * _shift_rows(aa, d, 1.0, True)
            g = bb + aa * cr
            g_ref[rows, :] = g
            da_ref[rows, :] = g * hp_ref[rows, :]
            return jnp.broadcast_to(a_t[0:1, :] * g[0:1, :], (_SUB, W))

        carry[...] = lax.fori_loop(0, nsub, step, carry[...])

    rev = lambda i: (nt - 1 - i, 0)
    return pl.pallas_call(
        body,
        name=name,
        grid=(nt,),
        in_specs=[pl.BlockSpec((tr, W), rev)] * 3,
        out_specs=[pl.BlockSpec((tr, W), rev)] * 2,
        out_shape=[jax.ShapeDtypeStruct((T, W), F32)] * 2,
        scratch_shapes=[pltpu.VMEM((_SUB, W), F32)],
        compiler_params=_cparams(("arbitrary",)),
    )(a, dh, hprev)


_ATT_BLK = 512
_ATT_QPARTS = 2
_ATT_BWD_TRIP = 4
_ATT_SCALE = 1.0 / math.sqrt(NOPE + ROPE)
_ATT_SCALE2 = _ATT_SCALE * math.log2(math.e)


def _call_with_exchange(body, send, gather, *, name, grid, in_specs, out_specs, out_shape, args):
    if send is None:
        return pl.pallas_call(body, name=name, grid=grid, in_specs=in_specs, out_specs=out_specs, out_shape=out_shape,
                              compiler_params=_cparams(("parallel", "arbitrary")))(*args)
    n_in, n_out = len(in_specs), len(out_specs)

    def riding(*refs):
        comm = (refs[n_in], refs[n_in + 1 + n_out]) + tuple(refs[n_in + 2 + n_out:])
        h, i = pl.program_id(0), pl.program_id(1)

        @pl.when((h == 0) & (i == 0))
        def _():
            _chip_start(*comm, gather=gather)

        body(*refs[:n_in], *refs[n_in + 1:n_in + 1 + n_out])

        @pl.when((h == grid[0] - 1) & (i == grid[1] - 1))
        def _():
            _chip_wait(*comm, gather=gather)

    shape = (4,) + send.shape if gather else send.shape
    return pl.pallas_call(
        riding, name=name, grid=grid, in_specs=in_specs + [_ANY], out_specs=out_specs + [_ANY],
        out_shape=out_shape + [jax.ShapeDtypeStruct(shape, send.dtype)], scratch_shapes=_CHIP_SEMS,
        compiler_params=_cparams(("arbitrary", "arbitrary")))(*args, send)


def _attn_fwd(q, kv, *, name, send=None):
    T = q.shape[0]
    blk = min(_ATT_BLK, T)
    nq = T // blk
    parts = _ATT_QPARTS if nq % _ATT_QPARTS == 0 else 1

    def body(q_ref, kv_ref, o_ref, lse_ref):
        i = pl.program_id(1)
        def one(part, j0, nblk, carry, masked):
            m, l, acc = carry
            rows = pl.ds(pl.multiple_of(j0 * blk, blk), nblk * blk)
            s = _dot(q_ref[part * blk:(part + 1) * blk, :], kv_ref[rows, 0:SLAB], "nt")
            if masked:
                col = lax.broadcasted_iota(jnp.int32, s.shape, 1)
                row = lax.broadcasted_iota(jnp.int32, s.shape, 0)
                s = jnp.where(col <= row + (nblk - 1) * blk, s, -jnp.inf)
            m_new = jnp.maximum(m, jnp.max(s, axis=-1, keepdims=True))
            alpha = jnp.exp2(m - m_new)
            p = jnp.exp2(s - m_new)
            l = alpha * l + jnp.sum(p, axis=-1, keepdims=True)
            acc = alpha * acc + _dot(p, kv_ref[rows, SLAB:2 * SLAB], "nn")
            return m_new, l, acc

        def step(t, carry):
            return tuple(one(part, t * parts, parts, carry[part], False) for part in range(parts))

        init = ((jnp.full((blk, 1), -jnp.inf, F32), jnp.zeros((blk, 1), F32), jnp.zeros((blk, SLAB), F32)),) * parts
        carry = lax.fori_loop(0, i // 2, lambda t, c: step(2 * t + 1, step(2 * t, c)), init)
        carry = lax.fori_loop((i // 2) * 2, i, step, carry)
        for part in range(parts):
            m, l, acc = one(part, i * parts, part + 1, carry[part], True)
            o_ref[part * blk:(part + 1) * blk, :] = acc / l
            lse_ref[part] = jnp.broadcast_to(m + jnp.log(l) * math.log2(math.e), (blk, SLAB)).T[0:_SUB, :]

    return _call_with_exchange(
        body, send, True,
        name=name,
        grid=(HEADS, nq // parts),
        in_specs=[pl.BlockSpec((parts * blk, SLAB), lambda h, i: (i, h)), pl.BlockSpec((T, 2 * SLAB), lambda h, i: (0, h))],
        out_specs=[pl.BlockSpec((parts * blk, SLAB), lambda h, i: (i, h)),
                   pl.BlockSpec((None, parts, _SUB, blk), lambda h, i: (h, i, 0, 0))],
        out_shape=[jax.ShapeDtypeStruct((T, HEADS * SLAB), F32), jax.ShapeDtypeStruct((HEADS, nq, _SUB, blk), F32)],
        args=(q, kv))


def _attn_delta(do, o, *, name):
    T = o.shape[0]
    blk = min(_ATT_BLK, T)
    nq = T // blk

    def body(do_ref, o_ref, d_ref):
        for h in range(HEADS):
            cols = slice(h * SLAB, (h + 1) * SLAB)
            dl = jnp.sum(do_ref[:, cols].astype(F32) * o_ref[:, cols], axis=-1, keepdims=True)
            d_ref[h] = jnp.broadcast_to(dl, (blk, SLAB)).T[0:_SUB, :]

    return pl.pallas_call(
        body,
        name=name,
        grid=(nq,),
        in_specs=[pl.BlockSpec((blk, HEADS * SLAB), lambda i: (i, 0))] * 2,
        out_specs=pl.BlockSpec((HEADS, None, _SUB, blk), lambda i: (0, i, 0, 0)),
        out_shape=jax.ShapeDtypeStruct((HEADS, nq, _SUB, blk), F32),
        compiler_params=_cparams(("parallel",)),
    )(do, o)


def _attn_bwd(q, kv, do, lse, delta, *, name, send=None):
    T = q.shape[0]
    blk = min(_ATT_BLK, T)
    nq = T // blk

    def body(q_ref, kv_ref, do_ref, lse_ref, dl_ref, dqt_ref, dkv_ref):
        j = pl.program_id(1)

        @pl.when(j == 0)
        def _():
            dqt_ref[...] = jnp.zeros_like(dqt_ref)

        kb, vb = kv_ref[:, 0:SLAB], kv_ref[:, SLAB:2 * SLAB]
        kbt = kb.astype(F32).T.astype(BF16)
        kpos = j * blk + lax.broadcasted_iota(jnp.int32, (blk, blk), 0)

        def step(i, carry, masked):
            dk, dv = carry
            rows = pl.ds(pl.multiple_of(i * blk, blk), blk)
            qb, dob = q_ref[rows, :], do_ref[rows, :]
            st = _dot(kb, qb, "nt")
            if masked:
                qpos = i * blk + lax.broadcasted_iota(jnp.int32, (blk, blk), 1)
                st = jnp.where(kpos <= qpos, st, -jnp.inf)
            pt = jnp.exp2(st - (lse_ref[i, 0:1, :] - math.log2(math.log(2.0))))
            dpt = _dot(vb, dob, "nt")
            dst = pt * (dpt - dl_ref[i, 0:1, :])
            dv = dv + _dot(pt, dob, "nn")
            dk = dk + _dot(dst, qb, "nn")
            dqt_ref[i] += _dot(kbt, dst, "nn")
            return dk, dv

        zero = jnp.zeros((blk, SLAB), F32)
        carry = step(j, (zero, zero), True)
        rest = nq - 1 - j

        def trip(t, c):
            for u in range(_ATT_BWD_TRIP):
                c = step(j + 1 + _ATT_BWD_TRIP * t + u, c, False)
            return c

        carry = lax.fori_loop(0, rest // _ATT_BWD_TRIP, trip, carry)
        tail = j + 1 + (rest // _ATT_BWD_TRIP) * _ATT_BWD_TRIP
        dk, dv = lax.fori_loop(tail, nq, functools.partial(step, masked=False), carry)
        dkv_ref[:, 0:SLAB] = dk.astype(BF16)
        dkv_ref[:, SLAB:2 * SLAB] = (dv * (1.0 / math.log(2.0))).astype(BF16)

    stat_spec = pl.BlockSpec((None, nq, _SUB, blk), lambda h, j: (h, 0, 0, 0))
    head_t_spec = pl.BlockSpec((None, nq, SLAB, blk), lambda h, j: (h, 0, 0, 0))
    return _call_with_exchange(
        body, send, False,
        name=name,
        grid=(HEADS, nq),
        in_specs=[pl.BlockSpec((T, SLAB), lambda h, j: (0, h)), pl.BlockSpec((blk, 2 * SLAB), lambda h, j: (j, h)),
                  pl.BlockSpec((T, SLAB), lambda h, j: (0, h)), stat_spec, stat_spec],
        out_specs=[head_t_spec, pl.BlockSpec((blk, 2 * SLAB), lambda h, j: (j, h))],
        out_shape=[jax.ShapeDtypeStruct((HEADS, nq, SLAB, blk), F32), jax.ShapeDtypeStruct((T, HEADS * 2 * SLAB), BF16)],
        args=(q, kv, do, lse, delta))


def _qrope_bwd(dqt, cq, sq, *, name):
    _, nq, _, blk = dqt.shape
    T = nq * blk

    def body(dqt_ref, c_ref, s_ref, dy_ref):
        c, s = c_ref[...], s_ref[...]
        for h in range(HEADS):
            dq = dqt_ref[h].T
            dy_ref[:, h * SLAB:(h + 1) * SLAB] = (dq * c).astype(BF16)
            dy_ref[:, (HEADS + h) * SLAB:(HEADS + h + 1) * SLAB] = (dq * s).astype(BF16)

    return pl.pallas_call(
        body,
        name=name,
        grid=(nq,),
        in_specs=[pl.BlockSpec((HEADS, None, SLAB, blk), lambda i: (0, i, 0, 0)), _row_spec(cq, blk), _row_spec(sq, blk)],
        out_specs=pl.BlockSpec((blk, 2 * HEADS * SLAB), lambda i: (i, 0)),
        out_shape=jax.ShapeDtypeStruct((T, 2 * HEADS * SLAB), BF16),
        compiler_params=_cparams(("parallel",)),
    )(dqt, cq, sq)


def _loss_head(x, tgt, g, *, name, tm=256):
    T = x.shape[0]
    tm = min(tm, T)

    def body(x_ref, t_ref, g_ref, loss_ref, dx_ref, dxh_ref, dg_ref):
        def f(xv, gv):
            e = _rms(xv, gv, D) - t_ref[...]
            row = jnp.sum(e * e, axis=1, keepdims=True)
            return jnp.sum(row, axis=0, keepdims=True) * (0.5 / D)

        val, vjp = jax.vjp(f, x_ref[...], g_ref[...])
        dxv, dgv = vjp(jnp.ones((1, 1), F32))

        @pl.when(pl.program_id(0) == 0)
        def _():
            loss_ref[...] = jnp.zeros_like(loss_ref)
            dg_ref[...] = jnp.zeros_like(dg_ref)

        dx_ref[...] = dxv
        dxh_ref[...] = dxv.astype(BF16)
        dg_ref[...] += dgv
        loss_ref[...] += jnp.broadcast_to(val, loss_ref.shape)

    return pl.pallas_call(
        body,
        name=name,
        grid=(T // tm,),
        in_specs=[_row_spec(x, tm), _row_spec(tgt, tm), _full_spec(g)],
        out_specs=[pl.BlockSpec((1, SLAB), lambda i: (0, 0)), _row_spec(x, tm), _row_spec(x, tm), _full_spec(g)],
        out_shape=[jax.ShapeDtypeStruct((1, SLAB), F32), jax.ShapeDtypeStruct(x.shape, F32),
                   jax.ShapeDtypeStruct(x.shape, BF16), jax.ShapeDtypeStruct(g.shape, F32)],
        compiler_params=_cparams(("arbitrary",)),
    )(x, tgt, g)


def _row_tile(rows, cols, budget=256 * 1024):
    best = None
    for t in range(16, rows + 1, 16):
        if rows % t == 0 and t * cols <= budget:
            best = t
    return best or rows


def _sum_fixed(x, out_dtype, *, name):
    n, R, C = x.shape
    tr = _row_tile(R, C)

    def body(x_ref, o_ref):
        acc = x_ref[0].astype(F32)
        for k in range(1, n):
            acc = acc + x_ref[k].astype(F32)
        o_ref[...] = acc.astype(o_ref.dtype)

    return pl.pallas_call(
        body,
        name=name,
        grid=(R // tr,),
        in_specs=[pl.BlockSpec((n, tr, C), lambda i: (0, i, 0))],
        out_specs=pl.BlockSpec((tr, C), lambda i: (i, 0)),
        out_shape=jax.ShapeDtypeStruct((R, C), out_dtype),
        compiler_params=_cparams(("parallel",)),
    )(x)


def _adamw(w, g, m, v, *, name):
    R, C = w.shape
    tr = _row_tile(R, C, 128 * 1024)

    def body(w_ref, g_ref, m_ref, v_ref, d_ref, nm_ref, nv_ref):
        gv = g_ref[...]
        mv = B1 * m_ref[...] + (1.0 - B1) * gv
        vv = B2 * v_ref[...] + (1.0 - B2) * (gv * gv)
        m_hat = mv / (1.0 - B1 ** STEP)
        v_hat = vv / (1.0 - B2 ** STEP)
        d_ref[...] = -LR * (m_hat / (jnp.sqrt(v_hat) + AEPS) + WD * w_ref[...])
        nm_ref[...] = mv
        nv_ref[...] = vv

    spec = pl.BlockSpec((tr, C), lambda i: (i, 0))
    return pl.pallas_call(
        body, name=name, grid=(R // tr,), in_specs=[spec] * 4, out_specs=[spec] * 3,
        out_shape=[jax.ShapeDtypeStruct((R, C), F32)] * 3, compiler_params=_cparams(("parallel",)),
    )(w, g, m, v)


_FLIPS = ((1, 0), (0, 1), (1, 1))
_ANY = pl.BlockSpec(memory_space=pl.ANY)


def _me():
    return lax.axis_index("x"), lax.axis_index("y"), lax.axis_index("c")


def _flip(mx, my, f):
    return (1 - mx if f[0] else mx), (1 - my if f[1] else my)


_CHIP_SEMS = [pltpu.SemaphoreType.DMA((3,)), pltpu.SemaphoreType.DMA((3,)), pltpu.SemaphoreType.DMA]


def _chip_copies(x_ref, out_ref, send_sems, recv_sems, local_sem, gather):
    mx, my, mc = _me()
    mine = 2 * mx + my
    outgoing, incoming = [], []
    for k, f in enumerate(_FLIPS):
        px, py = _flip(mx, my, f)
        peer = 2 * px + py
        src = x_ref if gather else x_ref.at[peer]
        for dst, to in ((out_ref.at[mine], outgoing), (out_ref.at[peer], incoming)):
            to.append(pltpu.make_async_remote_copy(src_ref=src, dst_ref=dst, send_sem=send_sems.at[k],
                                                   recv_sem=recv_sems.at[k], device_id=(px, py, mc), device_id_type=MESH))
    local = None if gather else pltpu.make_async_copy(x_ref.at[mine], out_ref.at[mine], local_sem)
    return outgoing, incoming, local


def _chip_start(*refs, gather):
    outgoing, _, local = _chip_copies(*refs, gather)
    if local is not None:
        local.start()
    for cp in outgoing:
        cp.start()


def _chip_wait(*refs, gather):
    outgoing, incoming, local = _chip_copies(*refs, gather)
    for cp in incoming:
        cp.wait_recv()
    for cp in outgoing:
        cp.wait_send()
    if local is not None:
        local.wait()


def _chip_exchange(x, gather, *, name):
    shape = x.shape if not gather else (4,) + x.shape

    def body(*refs):
        _chip_start(*refs, gather=gather)
        _chip_wait(*refs, gather=gather)

    return pl.pallas_call(
        body, name=name, in_specs=[_ANY], out_specs=_ANY, out_shape=jax.ShapeDtypeStruct(shape, x.dtype),
        scratch_shapes=_CHIP_SEMS,
    )(x)


def _core_exchange(x, half, *, name):
    shape = x.shape[1:] if half else x.shape

    def body(x_ref, out_ref, send_sem, recv_sem):
        mx, my, mc = _me()
        src = x_ref.at[1 - mc] if half else x_ref
        cp = pltpu.make_async_remote_copy(src_ref=src, dst_ref=out_ref, send_sem=send_sem, recv_sem=recv_sem,
                                          device_id=(mx, my, 1 - mc), device_id_type=MESH)
        cp.start()
        cp.wait()

    return pl.pallas_call(
        body, name=name, in_specs=[_ANY], out_specs=_ANY, out_shape=jax.ShapeDtypeStruct(shape, x.dtype),
        scratch_shapes=[pltpu.SemaphoreType.DMA, pltpu.SemaphoreType.DMA],
    )(x)


def _core_gather(x, g4, *, name):
    def body(x_ref, g_ref, out_ref, send_sems, recv_sems):
        mx, my, mc = _me()
        srcs = [(x_ref, 2 * mx + my)] + [(g_ref.at[2 * px + py], 2 * px + py) for px, py in (_flip(mx, my, f) for f in _FLIPS)]
        copies = [pltpu.make_async_remote_copy(src_ref=src, dst_ref=out_ref.at[slot], send_sem=send_sems.at[k],
                                               recv_sem=recv_sems.at[k], device_id=(mx, my, 1 - mc), device_id_type=MESH)
                  for k, (src, slot) in enumerate(srcs)]
        for cp in copies:
            cp.start()
        for cp in copies:
            cp.wait()

    return pl.pallas_call(
        body, name=name, in_specs=[_ANY, _ANY], out_specs=_ANY, out_shape=jax.ShapeDtypeStruct(g4.shape, g4.dtype),
        scratch_shapes=[pltpu.SemaphoreType.DMA((4,)), pltpu.SemaphoreType.DMA((4,))],
    )(x, g4)


def _finish_gather(x, g4, *, name):
    mx, my, _ = _me()
    own = (jnp.arange(4) == 2 * mx + my).reshape((4,) + (1,) * x.ndim)
    return jnp.where(own, x[None], g4), _core_gather(x, g4, name=name + "_cores")


def _by_core(mine, other):
    mc = lax.axis_index("c")
    return jnp.stack([jnp.where(mc == 0, mine, other), jnp.where(mc == 0, other, mine)])


def _all_gather(x, *, name):
    return _by_core(*_finish_gather(x, _chip_exchange(x, True, name=name + "_chips"), name=name))


def _add_own_half(x, got, *, name, out_dtype=BF16):
    _, R, C = x.shape
    tr = _row_tile(R, C)

    def body(c_ref, x_ref, g_ref, o_ref):
        o_ref[...] = (x_ref[...] + g_ref[...]).astype(o_ref.dtype)

    return pl.pallas_call(
        body,
        name=name,
        grid_spec=pltpu.PrefetchScalarGridSpec(
            num_scalar_prefetch=1, grid=(R // tr,),
            in_specs=[pl.BlockSpec((None, tr, C), lambda i, c: (c[0], i, 0)), pl.BlockSpec((tr, C), lambda i, c: (i, 0))],
            out_specs=pl.BlockSpec((tr, C), lambda i, c: (i, 0))),
        out_shape=jax.ShapeDtypeStruct((R, C), out_dtype),
        compiler_params=_cparams(("parallel",)),
    )(lax.axis_index("c").astype(jnp.int32).reshape(1), x, got)


def _chip_sums(x, *, name):
    _, _, R, C = x.shape
    got = _core_exchange(x, True, name=name + "_cores")
    return _add_own_half(x.reshape(2, 4 * R, C), got.reshape(4 * R, C), name=name + "_add").reshape(4, R, C)


def _all_reduce(x, *, name):
    g = _all_gather(x, name=name)
    return _sum_fixed(g.reshape((N_DEV,) + x.shape), F32, name=name + "_sum")


WEIGHTS = ['mix_norm_g', 'w_in', 'mla_q_norm_g', 'mla_kv_norm_g', 'mla_w_uq', 'mla_w_ukv', 'mla_out_g', 'ssm_conv_w',
           'ssm_conv_b', 'ssm_dt_bias', 'ssm_a_log', 'ssm_d', 'ssm_norm_g', 'lru_conv_w', 'lru_conv_b', 'lru_w_a',
           'lru_b_a', 'lru_w_i', 'lru_b_i', 'lru_lambda', 'lru_out_g', 'w_out', 'xattn_norm_g', 'mem_norm_g', 'w_mq',
           'w_mk', 'w_mv', 'w_mo', 'mlp_norm_g', 'w_mlp1', 'w_mlp2', 'final_norm_g']
ROW_SHARDED = ('w_in', 'w_out', 'w_mq', 'w_mk', 'w_mv', 'w_mo', 'w_mlp2')
COL_SHARDED = ('mla_w_uq', 'mla_w_ukv', 'w_mlp1')
BIG = tuple(n for n in WEIGHTS if n in ROW_SHARDED + COL_SHARDED)
CONV_SHARDED = ('ssm_conv_w', 'lru_conv_w')
SMALL = tuple(n for n in WEIGHTS if n not in BIG)
PACK_C = 1024


def _pack(arrs, dtype, lead=()):
    flat = jnp.concatenate([a.reshape(lead + (-1,)).astype(dtype) for a in arrs], axis=-1)
    n = flat.shape[-1]
    rows = -(-n // (16 * PACK_C)) * 16
    flat = jnp.pad(flat, [(0, 0)] * len(lead) + [(0, rows * PACK_C - n)])
    return flat.reshape(lead + (rows, PACK_C))


def _unpack(packed, shapes, lead=()):
    flat = packed.reshape(lead + (-1,))
    out, off = [], 0
    for s in shapes:
        n = math.prod(s)
        out.append(flat[..., off:off + n].reshape(lead + tuple(s)))
        off += n
    return out


def _pad_lanes(v, n=SLAB):
    return jnp.pad(v.astype(F32), (0, n - v.shape[0])).reshape(1, n)


PIECES = ('win', 'wq', 'wkv', 'wout', 'w_mq', 'w_mk', 'w_mv', 'w_mo', 'w_mlp1', 'w_mlp2')
PIECE_SOURCE = {'win': 'w_in', 'wq': 'mla_w_uq', 'wkv': 'mla_w_ukv', 'wout': 'w_out'}
PIECE_SHAPE = {'win': (128, 2048), 'wq': (Q_RANK, 2 * SLAB), 'wkv': (KV_RANK, 2 * SLAB), 'wout': (128, D),
               'w_mq': (128, D), 'w_mk': (128, D), 'w_mv': (128, D), 'w_mo': (128, D), 'w_mlp1': (512, D),
               'w_mlp2': (512, D)}
PIECE_COLS = ('wq', 'wkv')


def _k_win(w):
    kr = w[..., 384:416]
    zc = lambda k: jnp.zeros(w.shape[:-1] + (k,), w.dtype)
    return jnp.concatenate(
        [w[..., 0:384], kr, zc(96), kr[..., 16:32], kr[..., 0:16], zc(96),
         w[..., 416:672], w[..., 1444:1700], w[..., 1184:1188], zc(124),
         w[..., 672:1184], w[..., 1188:1444]], axis=-1)


def _k_win_inv(m):
    gk = m[..., 384:416] + jnp.concatenate([m[..., 528:544], m[..., 512:528]], axis=-1)
    return jnp.concatenate([m[..., 0:384], gk, m[..., 640:896], m[..., 1280:1792], m[..., 1152:1156], m[..., 1792:2048],
                            m[..., 896:1152]], axis=-1)


def _k_wq(w):
    nh = w.shape[-1] // (NOPE + ROPE)
    w = w.reshape(w.shape[:-1] + (nh, NOPE + ROPE))
    nope, r1, r2 = w[..., :NOPE], w[..., NOPE:NOPE + 16], w[..., NOPE + 16:]
    z = lambda k: jnp.zeros(w.shape[:-1] + (k,), w.dtype)
    both = jnp.stack([jnp.concatenate([nope, r1, r2, z(32)], -1), jnp.concatenate([z(64), r2, r1, z(32)], -1)], axis=-3)
    return both.reshape(w.shape[:-2] + (2 * nh * SLAB,))


def _k_wq_inv(m):
    nh = m.shape[-1] // (2 * SLAB)
    m = m.reshape(m.shape[:-1] + (2, nh, SLAB))
    q0, q1 = m[..., 0, :, :], m[..., 1, :, :]
    w = jnp.concatenate([q0[..., :64], q0[..., 64:80] + q1[..., 80:96], q0[..., 80:96] + q1[..., 64:80]], -1)
    return w.reshape(w.shape[:-2] + (nh * (NOPE + ROPE),))


def _k_wkv(w):
    nh = w.shape[-1] // (NOPE + VDIM)
    w = w.reshape(w.shape[:-1] + (nh, NOPE + VDIM))
    z = jnp.zeros(w.shape[:-1] + (64,), w.dtype)
    return jnp.concatenate([w[..., :NOPE], z, w[..., NOPE:], z], -1).reshape(w.shape[:-2] + (nh * 2 * SLAB,))


def _k_wkv_inv(m):
    nh = m.shape[-1] // (2 * SLAB)
    m = m.reshape(m.shape[:-1] + (nh, 2 * SLAB))
    return jnp.concatenate([m[..., :NOPE], m[..., SLAB:SLAB + VDIM]], -1).reshape(m.shape[:-2] + (nh * (NOPE + VDIM),))


def _k_swap(w):
    return jnp.swapaxes(w, -1, -2)


_K_FWD = {'win': _k_win, 'wq': _k_wq, 'wkv': _k_wkv, 'w_mlp1': _k_swap}
_K_INV = {'win': _k_win_inv, 'wq': _k_wq_inv, 'wkv': _k_wkv_inv, 'w_mlp1': _k_swap}


def _assemble(piece, g):
    _, _, a, b = g.shape
    if piece == 'wq':
        return g.reshape(2, 4, a, 2, SLAB).transpose(2, 3, 1, 0, 4).reshape(a, N_DEV * b)
    if piece in PIECE_COLS:
        return g.transpose(2, 1, 0, 3).reshape(a, N_DEV * b)
    return g.transpose(1, 0, 2, 3).reshape(N_DEV * a, b)


def _disassemble(piece, full):
    a, b = PIECE_SHAPE[piece]
    if piece == 'wq':
        return full.reshape(a, 2, 4, 2, SLAB).transpose(3, 2, 0, 1, 4).reshape(2, 4, a, b)
    if piece in PIECE_COLS:
        return full.reshape(a, 4, 2, b).transpose(2, 1, 0, 3)
    return full.reshape(4, 2, a, b).transpose(1, 0, 2, 3)


def _piece_rows(piece):
    a, b = PIECE_SHAPE[piece]
    return a * b // PACK_C


def _prep_late(pieces):
    if 'wout' in pieces:
        wout = pieces['wout']
        mla_rows = jnp.pad(wout[:HEADS * VDIM].reshape(HEADS, VDIM, D), ((0, 0), (0, SLAB - VDIM), (0, 0)))
        pieces['wout'] = jnp.concatenate([mla_rows.reshape(HEADS * SLAB, D), wout[HEADS * VDIM:]], axis=0)
    return pieces


def _prep_layer(pieces, Ws, l):
    P = _prep_late(dict(pieces))
    ri, ci = jnp.arange(SLAB)[:, None], jnp.arange(2 * SLAB)[None, :]
    sel = ((ri < ROPE) & (ci == ri + NOPE)).astype(P['wkv'].dtype)
    P['wkv'] = jnp.concatenate([P['wkv'], jnp.tile(sel, (1, HEADS))], axis=0)
    W = Ws
    row = lambda n: W[n][l].astype(F32).reshape(1, -1)
    for n in ('mix_norm_g', 'mla_q_norm_g', 'mla_kv_norm_g', 'ssm_norm_g', 'lru_lambda', 'lru_out_g', 'xattn_norm_g',
              'mem_norm_g', 'mlp_norm_g'):
        P[n] = row(n)
    P['mla_out_g'] = jnp.pad(W['mla_out_g'][l].astype(F32).reshape(HEADS, VDIM), ((0, 0), (0, SLAB - VDIM))).reshape(1, -1)
    for n in ('ssm_dt_bias', 'ssm_a_log', 'ssm_d'):
        P[n] = _pad_lanes(W[n][l])
    P['conv_w'] = jnp.pad(jnp.concatenate([W['ssm_conv_w'][l], W['lru_conv_w'][l]], axis=1).astype(F32), ((0, 4), (0, 0)))
    P['conv_b'] = jnp.concatenate([W['ssm_conv_b'][l], W['lru_conv_b'][l]]).astype(F32).reshape(1, -1)
    for n in ('lru_w_a', 'lru_w_i'):
        P[n] = jnp.concatenate([jnp.pad(W[n][l, k].astype(F32), ((0, 0), (64 * k, 192 - 64 * k))) for k in range(4)], axis=0)
    for n in ('lru_b_a', 'lru_b_i'):
        P[n] = W[n][l].astype(F32).reshape(1, -1)
    return P


def _unprep_pieces(G):
    o = {n: G[n] for n in PIECES}
    o['wkv'] = G['wkv'][:KV_RANK]
    wo = G['wout']
    o['wout'] = jnp.concatenate([wo[:HEADS * SLAB].reshape(HEADS, SLAB, D)[:, :VDIM].reshape(HEADS * VDIM, D),
                                 wo[HEADS * SLAB:]], axis=0)
    return o


def _unprep_small(G):
    o = {}
    for n in ('mix_norm_g', 'mla_q_norm_g', 'mla_kv_norm_g', 'ssm_norm_g', 'lru_lambda', 'lru_out_g', 'xattn_norm_g',
              'mem_norm_g', 'mlp_norm_g', 'lru_b_a', 'lru_b_i'):
        o[n] = G[n].reshape(-1)
    o['lru_b_a'] = o['lru_b_a'].reshape(4, 64)
    o['lru_b_i'] = o['lru_b_i'].reshape(4, 64)
    o['mla_out_g'] = G['mla_out_g'].reshape(HEADS, SLAB)[:, :VDIM].reshape(-1)
    for n in ('ssm_dt_bias', 'ssm_a_log', 'ssm_d'):
        o[n] = G[n][0, :4]
    o['ssm_conv_w'], o['lru_conv_w'] = G['conv_w'][:4, :512], G['conv_w'][:4, 512:]
    o['ssm_conv_b'], o['lru_conv_b'] = G['conv_b'][0, :512], G['conv_b'][0, 512:]
    for n in ('lru_w_a', 'lru_w_i'):
        o[n] = jnp.stack([G[n][64 * k:64 * (k + 1), 64 * k:64 * (k + 1)] for k in range(4)])
    return o


def _rope_tables(positions):
    half = ROPE // 2
    inv_freq = ROPE_THETA ** (-jnp.arange(half, dtype=F32) * 2.0 / ROPE)
    ang = positions.astype(F32)[:, None] * inv_freq
    cos, sin = jnp.cos(ang), jnp.sin(ang)
    T = positions.shape[0]
    z = lambda k: jnp.zeros((T, k), F32)
    ck = jnp.concatenate([cos, cos, z(96)], axis=1)
    sk = jnp.concatenate([-sin, sin, z(96)], axis=1)
    cq = jnp.concatenate([jnp.ones((T, NOPE), F32), cos, cos, z(32)], axis=1)
    sq = jnp.concatenate([z(NOPE), -sin, sin, z(32)], axis=1)
    return ck, sk, cq * _ATT_SCALE2, sq * _ATT_SCALE2


def _add_epi(acc, res):
    return (acc + res,)


def _add_norm_epi(acc, res, g):
    x = acc + res
    return x, _rms(x, g, x.shape[-1])


def _norm_bwd_epi(acc, x, res, g):
    _, vjp = jax.vjp(lambda xv, gv: _rms(xv, gv, xv.shape[-1]), x, g)
    dx, dg = vjp(acc)
    dx = dx + res
    return dx, dx, dg


def _relu2_epi(acc):
    r = jnp.maximum(acc, 0.0)
    return r, r * r


def _drelu2_epi(acc, r):
    return (acc * (2.0 * r.astype(F32)),)


def _norm(x, g, name):
    return _rows_fwd(_f_norm, [x], [g], [(x.shape[1], BF16)], name=name)[0]


def _norm_bwd(x, g, ct, add, name):
    (dx, dx16), (dg,) = _rows_vjp(_f_norm, [x], [g], [ct], name=name, drows=[0], dparams=[0], drow_dtypes=[F32],
                                  add=add, twin=True)
    return dx, dx16, dg


def _layer_fwd(x0, h1, mem, P, tabs, g_next=None, send=None, on_got=None):
    ck, sk, cq, sq = tabs
    S = {'x0': x0}
    if h1 is None:
        h1 = _norm(x0, P['mix_norm_g'], "norm_mix")
    S['h1'] = h1
    win = P['win']
    u_mla = S['u_mla'] = _mm(h1, win[:, 0:U_MLA], name="in_mla")
    u_gate = S['u_gate'] = _mm(h1, win[:, U_MLA:U_MLA + U_GATE], name="in_gate")
    u_conv = S['u_conv'] = _mm(h1, win[:, U_MLA + U_GATE:], name="in_conv")
    cqn, akv = _rows_fwd(_f_mla_prep, [u_mla, ck, sk], [P['mla_q_norm_g'], P['mla_kv_norm_g']],
                         [(Q_RANK, BF16), (2 * SLAB, BF16)], name="mla_prep")
    S['cqn'], S['akv'] = cqn, akv
    yq = _mm(cqn, P['wq'], name="q_proj")
    q = S['q'] = _rows_fwd(_f_qrope, [yq, cq, sq], [], [(HEADS * SLAB, BF16)], name="q_rope")[0]
    kv = S['kv'] = _mm(akv, P['wkv'], name="kv_proj", out_dtypes=(BF16,))
    o, lse, *got = _attn_fwd(q, kv, name="attn_fwd" if send is None else "attn_fwd_gather", send=send)
    S['o'], S['lse'] = o, lse
    if got:
        P.update(_prep_late(on_got(got[0]) or {}))
    c_ssm, c_lru = _conv_fwd(u_conv, P['conv_w'], P['conv_b'], name="conv_fwd")
    S['c_ssm'], S['c_lru'] = c_ssm, c_lru
    ys, sall = _ssd_fwd(c_ssm, u_gate, P['ssm_dt_bias'], P['ssm_a_log'], P['ssm_d'], P['ssm_norm_g'], name="ssd_fwd")
    S['ys'], S['sall'] = ys, sall
    a, b = _rows_fwd(_f_lru_gates, [c_lru], [P['lru_w_a'], P['lru_b_a'], P['lru_w_i'], P['lru_b_i'], P['lru_lambda']],
                     [(256, F32), (256, F32)], name="lru_gates", tm=_MM_ROWS)
    h, hprev = _lru_scan_fwd(a, b, name="lru_scan")
    S['a'], S['h'], S['hprev'] = a, h, hprev
    ymix = S['ymix'] = _rows_fwd(_f_mix, [o, ys, h, u_gate], [P['mla_out_g'], P['lru_out_g']],
                                 [(HEADS * SLAB + 512, BF16)], name="mix")[0]
    x1, hx = _mm(ymix, P['wout'], name="out_proj", epi=_add_norm_epi, extras=(x0, P['xattn_norm_g']), out_dtypes=(F32, BF16))
    S['x1'], S['hx'] = x1, hx
    qx = S['qx'] = _mm(hx, P['w_mq'], name="mem_q", out_dtypes=(BF16,))
    mn = S['mn'] = _norm(mem, P['mem_norm_g'], "norm_mem")
    kx = S['kx'] = _mm(mn, P['w_mk'], name="mem_k", out_dtypes=(BF16,))
    vx = S['vx'] = _mm(mn, P['w_mv'], name="mem_v", out_dtypes=(BF16,))
    ox = S['ox'] = _rows_fwd(_f_xattn, [qx], [kx, vx], [(D, BF16)], name="xattn", tm=_MM_ROWS)[0]
    x2, hm = _mm(ox, P['w_mo'], name="mem_o", epi=_add_norm_epi, extras=(x1, P['mlp_norm_g']), out_dtypes=(F32, BF16))
    S['x2'], S['hm'] = x2, hm
    r, s = _mm(hm, P['w_mlp1'], "nt", name="mlp_up", epi=_relu2_epi, out_dtypes=(BF16, BF16))
    S['r'], S['s'] = r, s
    if g_next is None:
        x3, h_next = _mm(s, P['w_mlp2'], name="mlp_down_last", epi=_add_epi, extras=(x2,)), None
    else:
        x3, h_next = _mm(s, P['w_mlp2'], name="mlp_down", epi=_add_norm_epi, extras=(x2, g_next), out_dtypes=(F32, BF16))
    return x3, h_next, S


def _layer_bwd(dx3, dx3h, mem, S, P, tabs, send=None):
    ck, sk, cq, sq = tabs
    G = {}
    da = _mm(dx3h, P['w_mlp2'], "nt", name="mlp_down_dx", epi=_drelu2_epi, extras=(S['r'],), out_dtypes=(BF16,))
    G['w_mlp2'] = _mm(S['s'], dx3h, "tn", name="mlp_down_dw")
    G['w_mlp1'] = _mm(da, S['hm'], "tn", name="mlp_up_dw")
    norm_out = dict(epi=_norm_bwd_epi, out_dtypes=(F32, BF16), col_sums=1)
    dx2, dx2h, G['mlp_norm_g'] = _mm(da, P['w_mlp1'], "nn", name="mlp_up_dx", extras=(S['x2'], dx3, P['mlp_norm_g']),
                                     **norm_out)
    dox = _mm(dx2h, P['w_mo'], "nt", name="mem_o_dx")
    G['w_mo'] = _mm(S['ox'], dx2h, "tn", name="mem_o_dw")
    (dqx,), (dkx, dvx) = _rows_vjp(_f_xattn, [S['qx']], [S['kx'], S['vx']], [dox], name="xattn_bwd", drows=[0],
                                   dparams=[0, 1], drow_dtypes=[BF16], tm=_MM_ROWS)
    G['w_mq'] = _mm(S['hx'], dqx, "tn", name="mem_q_dw")
    dx1, dx1h, G['xattn_norm_g'] = _mm(dqx, P['w_mq'], "nt", name="mem_q_dx", extras=(S['x1'], dx2, P['xattn_norm_g']),
                                       **norm_out)
    G['w_mk'] = _mm(S['mn'], dkx, "tn", name="mem_k_dw")
    G['w_mv'] = _mm(S['mn'], dvx, "tn", name="mem_v_dw")
    dmn = _mm(dkx, P['w_mk'], "nt", name="mem_k_dx", epi=_add_epi, extras=(_mm(dvx, P['w_mv'], "nt", name="mem_v_dx"),))
    _, _, G['mem_norm_g'] = _norm_bwd(mem, P['mem_norm_g'], dmn, None, "norm_mem_bwd")
    dymix = _mm(dx1h, P['wout'], "nt", name="out_proj_dx")
    G['wout'] = _mm(S['ymix'], dx1h, "tn", name="out_proj_dw")
    (do, dys, dh, dug_mix), (G['mla_out_g'], G['lru_out_g']) = _rows_vjp(
        _f_mix, [S['o'], S['ys'], S['h'], S['u_gate']], [P['mla_out_g'], P['lru_out_g']], [dymix], name="mix_bwd",
        drows=[0, 1, 2, 3], dparams=[0, 1], drow_dtypes=[BF16, F32, F32, F32])
    g, da_lru = _lru_scan_bwd(S['a'], dh, S['hprev'], name="lru_scan_bwd")
    lru_par = [P['lru_w_a'], P['lru_b_a'], P['lru_w_i'], P['lru_b_i'], P['lru_lambda']]
    (dc_lru,), dpar = _rows_vjp(_f_lru_gates, [S['c_lru']], lru_par, [da_lru, g], name="lru_gates_bwd", drows=[0],
                                dparams=[0, 1, 2, 3, 4], drow_dtypes=[F32], tm=_MM_ROWS)
    G['lru_w_a'], G['lru_b_a'], G['lru_w_i'], G['lru_b_i'], G['lru_lambda'] = dpar
    dc_ssm, dug_ssd, G['ssm_dt_bias'], G['ssm_a_log'], G['ssm_d'], G['ssm_norm_g'] = _ssd_bwd(
        S['c_ssm'], S['u_gate'], S['sall'], dys, P['ssm_dt_bias'], P['ssm_a_log'], P['ssm_d'], P['ssm_norm_g'],
        name="ssd_bwd")
    du_conv, G['conv_w'], G['conv_b'] = _conv_bwd(S['u_conv'], dc_ssm, dc_lru, P['conv_w'], name="conv_bwd")
    delta = _attn_delta(do, S['o'], name="attn_delta")
    dqt, dkv, *got = _attn_bwd(S['q'], S['kv'], do, S['lse'], delta,
                               name="attn_bwd" if send is None else "attn_bwd_scatter", send=send)
    dyq = _qrope_bwd(dqt, cq, sq, name="q_rope_bwd")
    dcqn = _mm(dyq, P['wq'], "nt", name="q_proj_dx")
    G['wq'] = _mm(S['cqn'], dyq, "tn", name="q_proj_dw")
    dakv = _mm(dkv, P['wkv'], "nt", name="kv_proj_dx")
    G['wkv'] = _mm(S['akv'], dkv, "tn", name="kv_proj_dw")
    (du_mla,), (G['mla_q_norm_g'], G['mla_kv_norm_g']) = _rows_vjp(
        _f_mla_prep, [S['u_mla'], ck, sk], [P['mla_q_norm_g'], P['mla_kv_norm_g']], [dcqn, dakv], name="mla_prep_bwd",
        drows=[0], dparams=[0, 1], drow_dtypes=[BF16])
    du = jnp.concatenate([du_mla, (dug_mix + dug_ssd).astype(BF16), du_conv.astype(BF16)], axis=1)
    G['win'] = _mm(S['h1'], du, "tn", name="in_dw")
    dx0, dx0h, G['mix_norm_g'] = _mm(du, P['win'], "nt", name="in_dx", extras=(S['x0'], dx1, P['mix_norm_g']), **norm_out)
    return dx0, dx0h, G, (got[0] if got else None)


class _NoExchange:
    def __init__(self, layers):
        self.layers = layers

    def pieces(self, l):
        return self.layers[l]

    def fwd_send(self, l):
        return None

    def fwd_got(self, l, got):
        pass

    def bwd_send(self, l):
        return None

    def bwd_got(self, l, got):
        pass

    def grads_ready(self, l, pieces):
        pass


def _local_step(x, mem, positions, ex, Ws, tgt):
    tabs = _rope_tables(positions)
    saved, preps, h = [], [], None
    for l in range(DEPTH):
        P = _prep_layer(ex.pieces(l), Ws, l)
        send = ex.fwd_send(l)
        g_next = Ws['mix_norm_g'][l + 1].astype(F32).reshape(1, D) if l + 1 < DEPTH else None
        x, h, S = _layer_fwd(x, h, mem, P, tabs, g_next, send, functools.partial(ex.fwd_got, l))
        saved.append(S)
        preps.append(P)
    loss, dx, dxh, dg_final = _loss_head(x, tgt, Ws['final_norm_g'].astype(F32).reshape(1, D), name="loss_head")
    pieces, small = [None] * DEPTH, [None] * DEPTH
    for l in reversed(range(DEPTH)):
        send = ex.bwd_send(l)
        dx, dxh, G, got = _layer_bwd(dx, dxh, mem, saved[l], preps[l], tabs, send)
        if send is not None:
            ex.bwd_got(l, got)
        pieces[l], small[l] = _unprep_pieces(G), _unprep_small(G)
        ex.grads_ready(l, pieces[l])
    grads = {n: jnp.stack([small[l][n] for l in range(DEPTH)]) for n in SMALL if n != 'final_norm_g'}
    grads['final_norm_g'] = dg_final.reshape(D)
    return loss, dx, pieces, grads


def _pack_rows(pieces, names=PIECES):
    return jnp.concatenate([pieces[n].reshape(pieces[n].shape[:-2] + (-1, PACK_C)) for n in names], axis=-2)


def _unpack_rows(packed, lead=(), names=PIECES):
    out, off = {}, 0
    for n in names:
        rows = _piece_rows(n)
        out[n] = packed[..., off:off + rows, :].reshape(lead + PIECE_SHAPE[n])
        off += rows
    return out


EARLY = ('win', 'wq', 'wkv')
LATE = tuple(n for n in PIECES if n not in EARLY)


class _StepExchange(_NoExchange):
    FIRST = [(0, EARLY)]
    RIDES = {0: [(0, LATE), (1, PIECES), (2, PIECES)], 1: [(3, PIECES)]}

    def __init__(self, shard):
        self.shard = {n: a.astype(BF16) for n, a in shard.items()}
        self.layers = {l: {} for l in range(DEPTH)}
        self.sums, self.reduced = {}, {}
        first = self._pack(self.FIRST)
        self._take(*_finish_gather(first, _chip_exchange(first, True, name="gather_w0_chips"), name="gather_w0"), self.FIRST)

    def _pack(self, spec):
        return jnp.concatenate([_pack_rows({n: self.shard[n][l] for n in names}, names) for l, names in spec], axis=0)

    def _take(self, mine, other, spec):
        off = 0
        for l, names in spec:
            rows = sum(_piece_rows(n) for n in names)
            a, b = (_unpack_rows(g[:, off:off + rows], (4,), names) for g in (mine, other))
            self.layers[l].update({n: _assemble(n, _by_core(a[n], b[n])) for n in names})
            off += rows

    def fwd_send(self, l):
        return self._pack(self.RIDES[l]) if l in self.RIDES else None

    def fwd_got(self, l, got):
        self._take(*_finish_gather(self._pack(self.RIDES[l]), got, name="gather_w"), self.RIDES[l])
        return {n: self.layers[l][n] for n in LATE} if (l, LATE) in self.RIDES[l] else {}

    def grads_ready(self, l, pieces):
        x = _pack_rows({n: _disassemble(n, pieces[n]) for n in PIECES})
        if l == 0:
            parts = _chip_exchange(_chip_sums(x, name="scatter_g0"), False, name="scatter_g0_chips")
            self.reduced[0] = _sum_fixed(parts, F32, name="scatter_g0_sum")
        else:
            self.sums[l] = x.transpose(1, 0, 2, 3).astype(BF16).reshape(4, -1, PACK_C)

    def bwd_send(self, l):
        return self.sums.pop(l + 1, None)

    def bwd_got(self, l, got):
        both = _sum_fixed(got, F32, name="scatter_g_sum").reshape(2, -1, PACK_C)
        other = _core_exchange(both, True, name="scatter_g_cores")
        self.reduced[l + 1] = _add_own_half(both, other, name="scatter_g_add", out_dtype=F32)


def _adamw_nd(w, g, m, v, name):
    shp = w.shape
    two = lambda a: a.reshape(-1, shp[-1])
    return [r.reshape(shp) for r in _adamw(two(w), two(g), two(m), two(v), name=name)]


def kernel(x, mem, positions, mix_norm_g, w_in, mla_q_norm_g, mla_kv_norm_g, mla_w_uq, mla_w_ukv, mla_out_g, ssm_conv_w, ssm_conv_b, ssm_dt_bias, ssm_a_log, ssm_d, ssm_norm_g, lru_conv_w, lru_conv_b, lru_w_a, lru_b_a, lru_w_i, lru_b_i, lru_lambda, lru_out_g, w_out, xattn_norm_g, mem_norm_g, w_mq, w_mk, w_mv, w_mo, mlp_norm_g, w_mlp1, w_mlp2, final_norm_g, loss_target, m_mix_norm_g, m_w_in, m_mla_q_norm_g, m_mla_kv_norm_g, m_mla_w_uq, m_mla_w_ukv, m_mla_out_g, m_ssm_conv_w, m_ssm_conv_b, m_ssm_dt_bias, m_ssm_a_log, m_ssm_d, m_ssm_norm_g, m_lru_conv_w, m_lru_conv_b, m_lru_w_a, m_lru_b_a, m_lru_w_i, m_lru_b_i, m_lru_lambda, m_lru_out_g, m_w_out, m_xattn_norm_g, m_mem_norm_g, m_w_mq, m_w_mk, m_w_mv, m_w_mo, m_mlp_norm_g, m_w_mlp1, m_w_mlp2, m_final_norm_g, v_mix_norm_g, v_w_in, v_mla_q_norm_g, v_mla_kv_norm_g, v_mla_w_uq, v_mla_w_ukv, v_mla_out_g, v_ssm_conv_w, v_ssm_conv_b, v_ssm_dt_bias, v_ssm_a_log, v_ssm_d, v_ssm_norm_g, v_lru_conv_w, v_lru_conv_b, v_lru_w_a, v_lru_b_a, v_lru_w_i, v_lru_b_i, v_lru_lambda, v_lru_out_g, v_w_out, v_xattn_norm_g, v_mem_norm_g, v_w_mq, v_w_mk, v_w_mv, v_w_mo, v_mlp_norm_g, v_w_mlp1, v_w_mlp2, v_final_norm_g):
    a = locals()
    w = {n: a[n] for n in WEIGHTS}
    m = {n: a['m_' + n] for n in WEIGHTS}
    v = {n: a['v_' + n] for n in WEIGHTS}
    me = 4 * lax.axis_index("x") + 2 * lax.axis_index("y") + lax.axis_index("c")

    ex = _StepExchange({n: _K_FWD.get(n, lambda a: a)(w[PIECE_SOURCE.get(n, n)]) for n in PIECES})
    Ws = {}
    conv_shapes = [w[n].shape for n in CONV_SHARDED]
    conv_g = _all_gather(_pack([w[n] for n in CONV_SHARDED], F32), name="gather_conv")
    conv_g = conv_g.transpose(1, 0, 2, 3).reshape((N_DEV,) + conv_g.shape[2:])
    for n, g in zip(CONV_SHARDED, _unpack(conv_g, conv_shapes, lead=(N_DEV,))):
        Ws[n] = g.transpose(1, 2, 0, 3).reshape(g.shape[1], g.shape[2], N_DEV * g.shape[3])
    for n in SMALL:
        if n not in CONV_SHARDED:
            Ws[n] = w[n]

    loss_share, dx, _, grads = _local_step(x[0], mem[0], positions[0], ex, Ws, loss_target[0])
    loss = lax.psum(loss_share[0, 0], ("x", "y", "c"))

    g_out = {}
    for n in PIECES:
        g = jnp.stack([_unpack_rows(ex.reduced[l])[n] for l in range(DEPTH)])
        g_out[PIECE_SOURCE.get(n, n)] = _K_INV[n](g) if n in _K_INV else g
    small_shapes = [grads[n].shape for n in SMALL]
    g_small = _all_reduce(_pack([grads[n] for n in SMALL], F32), name="reduce_g")
    for n, g in zip(SMALL, _unpack(g_small, small_shapes)):
        if n in CONV_SHARDED:
            cols = w[n].shape[-1]
            g = lax.dynamic_slice_in_dim(g, me * cols, cols, axis=2)
        g_out[n] = g

    delta, new_m, new_v = {}, {}, {}
    for n in BIG:
        delta[n], new_m[n], new_v[n] = _adamw_nd(w[n], g_out[n], m[n], v[n], "adamw_" + n)
    shapes = [w[n].shape for n in SMALL]
    packed = [_pack([d[n] for n in SMALL], F32) for d in (w, g_out, m, v)]
    for d, res in zip((delta, new_m, new_v), _adamw(*packed, name="adamw_small")):
        d.update(zip(SMALL, _unpack(res, shapes)))

    return (loss, dx[None], *[g_out[n] for n in WEIGHTS], *[delta[n] for n in WEIGHTS],
            *[new_m[n] for n in WEIGHTS], *[new_v[n] for n in WEIGHTS])
```

```python
import functools
import math

import jax
import jax.numpy as jnp
from jax import lax
from jax.experimental import pallas as pl
from jax.experimental.pallas import tpu as pltpu

F32, BF16 = jnp.float32, jnp.bfloat16

D = 1024
DEPTH = 4
N_MEM = 256
EPS = 1e-6
HEADS = 8
NOPE, ROPE, VDIM = 64, 32, 64
Q_RANK, KV_RANK = 256, 128
ROPE_THETA = 10000.0
SSM_CHUNK = 128
LRU_C = 8.0
MEM_HEADS = 4
D_FF = 4 * D
SLAB = 128
LR, B1, B2, AEPS, WD, STEP = 0.001, 0.9, 0.999, 1e-08, 0.01, 10

N_DEV = 8
MESH = pl.DeviceIdType.MESH

U_MLA = 640
U_GATE = 640
U_CONV = 768

_DN = {"nn": (((1,), (0,)), ((), ())), "nt": (((1,), (1,)), ((), ())), "tn": (((0,), (0,)), ((), ()))}


def _dot(a, b, kind):
    return lax.dot_general(a.astype(BF16), b.astype(BF16), _DN[kind], preferred_element_type=F32)


@functools.partial(jax.custom_vjp, nondiff_argnums=(2,))
def _bdot(a, b, kind):
    return _dot(a, b, kind)


def _bdot_fwd(a, b, kind):
    return _dot(a, b, kind), (a, b)


def _bdot_bwd(kind, res, g):
    a, b = res
    if kind == "nn":
        da, db = _dot(g, b, "nt"), _dot(a, g, "tn")
    elif kind == "nt":
        da, db = _dot(g, b, "nn"), _dot(g, a, "tn")
    else:
        da, db = _dot(b, g, "nt"), _dot(a, g, "nn")
    return da.astype(a.dtype), db.astype(b.dtype)


_bdot.defvjp(_bdot_fwd, _bdot_bwd)


def _tile(n, pref):
    if n <= pref:
        return n
    t = pref
    while n % t:
        t -= SLAB
    return t


def _cparams(sem, vmem_mb=48):
    return pltpu.CompilerParams(dimension_semantics=sem, vmem_limit_bytes=vmem_mb * 1024 * 1024)


def _mm(a, b, kind="nn", *, name, out_dtypes=(F32,), epi=None, extras=(), col_sums=0, tm=1024, tn=1024, tk=1024):
    if kind == "tn":
        K, M = a.shape
    else:
        M, K = a.shape
    N = b.shape[0] if kind == "nt" else b.shape[1]
    if a.dtype == F32 or b.dtype == F32:
        tk = tk // 2
    tm, tn, tk = _tile(M, tm), _tile(N, tn), _tile(K, tk)
    assert not col_sums or tn == N, "column sums need one tile across the columns"
    nk = K // tk
    a_spec = pl.BlockSpec((tk, tm), lambda i, j, k: (k, i)) if kind == "tn" else pl.BlockSpec((tm, tk), lambda i, j, k: (i, k))
    b_spec = pl.BlockSpec((tn, tk), lambda i, j, k: (j, k)) if kind == "nt" else pl.BlockSpec((tk, tn), lambda i, j, k: (k, j))
    o_spec = pl.BlockSpec((tm, tn), lambda i, j, k: (i, j))
    vec_spec = pl.BlockSpec((1, tn), lambda i, j, k: (0, j))
    ex_specs = [vec_spec if e.shape[0] == 1 else o_spec for e in extras]
    n_ex, n_out = len(extras), len(out_dtypes)

    def body(*refs):
        a_ref, b_ref = refs[:2]
        ex = refs[2:2 + n_ex]
        outs = refs[2 + n_ex:2 + n_ex + n_out]
        acc = refs[-1]
        k = pl.program_id(2)
        first_row_tile = pl.program_id(0) == 0

        def finish(r):
            res = epi(r, *[e[...] for e in ex]) if epi is not None else (r,)
            for o, v in zip(outs, res):
                o[...] = v.astype(o.dtype)
            sums = refs[2 + n_ex + n_out:2 + n_ex + n_out + col_sums]

            @pl.when(first_row_tile)
            def _():
                for o in sums:
                    o[...] = jnp.zeros_like(o)

            for o, v in zip(sums, res[n_out:]):
                o[...] += v

        if nk == 1:
            finish(_dot(a_ref[...], b_ref[...], kind))
            return

        @pl.when(k == 0)
        def _():
            acc[...] = _dot(a_ref[...], b_ref[...], kind)

        @pl.when(k > 0)
        def _():
            acc[...] += _dot(a_ref[...], b_ref[...], kind)

        @pl.when(k == nk - 1)
        def _():
            finish(acc[...])

    res = pl.pallas_call(
        body,
        name=name,
        grid=(M // tm, N // tn, nk),
        in_specs=[a_spec, b_spec] + ex_specs,
        out_specs=[o_spec] * n_out + [vec_spec] * col_sums,
        out_shape=[jax.ShapeDtypeStruct((M, N), dt) for dt in out_dtypes] + [jax.ShapeDtypeStruct((1, N), F32)] * col_sums,
        scratch_shapes=[pltpu.VMEM((tm, tn), F32)],
        compiler_params=_cparams(("arbitrary" if col_sums else "parallel", "parallel", "arbitrary")),
    )(a, b, *extras)
    return res[0] if n_out + col_sums == 1 else res


_MM_ROWS = 1024


def _row_spec(arr, tm):
    return pl.BlockSpec((tm, arr.shape[1]), lambda i: (i, 0))


def _full_spec(arr):
    nd = arr.ndim
    return pl.BlockSpec(arr.shape, lambda i: (0,) * nd)


def _rows_fwd(fn, rows, params, outs, *, name, tm=256):
    T = rows[0].shape[0]
    tm = min(tm, T)
    nr, npar = len(rows), len(params)

    def body(*refs):
        ins = [r[...] for r in refs[:nr + npar]]
        res = fn(*ins)
        for o, v in zip(refs[nr + npar:], res):
            o[...] = v.astype(o.dtype)

    res = pl.pallas_call(
        body,
        name=name,
        grid=(T // tm,),
        in_specs=[_row_spec(r, tm) for r in rows] + [_full_spec(p) for p in params],
        out_specs=[pl.BlockSpec((tm, c), lambda i: (i, 0)) for c, _ in outs],
        out_shape=[jax.ShapeDtypeStruct((T, c), dt) for c, dt in outs],
        compiler_params=_cparams(("parallel",)),
    )(*rows, *params)
    return res


def _rows_vjp(fn, rows, params, cts, *, name, drows, dparams, drow_dtypes, add=None, twin=False, tm=256):
    T = rows[0].shape[0]
    tm = min(tm, T)
    nr, npar, nct = len(rows), len(params), len(cts)
    n_add = 0 if add is None else 1
    n_dr, n_dp = len(drows), len(dparams)
    n_tw = 1 if twin else 0

    def body(*refs):
        row_t = [r[...] for r in refs[:nr]]
        par_t = [r[...] for r in refs[nr:nr + npar]]
        ct_t = [r[...].astype(F32) for r in refs[nr + npar:nr + npar + nct]]
        pos = nr + npar + nct
        add_t = refs[pos][...] if n_add else None
        pos += n_add
        drow_refs = refs[pos:pos + n_dr]
        dpar_refs = refs[pos + n_dr:pos + n_dr + n_dp]

        def g(*dargs):
            rr, pp = list(row_t), list(par_t)
            for idx, v in zip(drows, dargs[:n_dr]):
                rr[idx] = v
            for idx, v in zip(dparams, dargs[n_dr:]):
                pp[idx] = v
            return tuple(fn(*rr, *pp))

        prim = [row_t[i].astype(F32) for i in drows] + [par_t[i].astype(F32) for i in dparams]
        _, vjp = jax.vjp(g, *prim)
        grads = vjp(tuple(ct_t))
        for n, (o, v) in enumerate(zip(drow_refs, grads[:n_dr])):
            if n == 0 and n_add:
                v = v + add_t.astype(F32)
            o[...] = v.astype(o.dtype)
            if n == 0 and n_tw:
                refs[-1][...] = v.astype(BF16)

        @pl.when(pl.program_id(0) == 0)
        def _():
            for o in dpar_refs:
                o[...] = jnp.zeros_like(o)

        for o, v in zip(dpar_refs, grads[n_dr:]):
            o[...] += v

    res = pl.pallas_call(
        body,
        name=name,
        grid=(T // tm,),
        in_specs=[_row_spec(r, tm) for r in rows] + [_full_spec(p) for p in params] + [_row_spec(c, tm) for c in cts]
        + ([_row_spec(add, tm)] if n_add else []),
        out_specs=[_row_spec(rows[i], tm) for i in drows] + [_full_spec(params[i]) for i in dparams]
        + [_row_spec(rows[drows[0]], tm)] * n_tw,
        out_shape=[jax.ShapeDtypeStruct(rows[i].shape, dt) for i, dt in zip(drows, drow_dtypes)]
        + [jax.ShapeDtypeStruct(params[i].shape, F32) for i in dparams]
        + [jax.ShapeDtypeStruct(rows[drows[0]].shape, BF16)] * n_tw,
        compiler_params=_cparams(("arbitrary",)),
    )(*rows, *params, *cts, *([add] if n_add else []))
    return list(res[:n_dr]) + list(res[n_dr + n_dp:]), list(res[n_dr:n_dr + n_dp])


def _rms(x, g, n):
    return x * lax.rsqrt(jnp.sum(x * x, axis=-1, keepdims=True) * (1.0 / n) + EPS) * g


def _sigmoid(x):
    return 1.0 / (1.0 + jnp.exp(-x))


def _silu(x):
    return x * _sigmoid(x)


def _softplus(x):
    return jnp.maximum(x, 0.0) + jnp.log(1.0 + jnp.exp(-jnp.abs(x)))


def _gelu_tanh(x):
    return 0.5 * x * (1.0 + jnp.tanh(math.sqrt(2.0 / math.pi) * (x + 0.044715 * x * x * x)))


def _lane(shape):
    return lax.broadcasted_iota(jnp.int32, shape, len(shape) - 1)


def _col(x, h):
    return jnp.sum(jnp.where(_lane(x.shape) == h, x, 0.0), axis=-1, keepdims=True)


def _f_norm(x, g):
    return (_rms(x.astype(F32), g, x.shape[-1]),)


def _f_mla_prep(u, ck, sk, gq, gkv):
    u = u.astype(F32)
    cq = _rms(u[:, 0:256], gq, Q_RANK)
    ckv = _rms(u[:, 256:384], gkv, KV_RANK)
    kr = u[:, 384:512] * ck + u[:, 512:640] * sk
    return cq, jnp.concatenate([ckv, kr], axis=1)


def _f_qrope(y, cq, sq):
    y = y.astype(F32)
    c8, s8 = jnp.tile(cq, (1, HEADS)), jnp.tile(sq, (1, HEADS))
    return (y[:, :HEADS * SLAB] * c8 + y[:, HEADS * SLAB:] * s8,)


def _f_lru_gates(xc, wa, ba, wi, bi, lam):
    xc = xc.astype(F32)
    r = _sigmoid(_bdot(xc, wa, "nn") + ba)
    i = _sigmoid(_bdot(xc, wi, "nn") + bi)
    log_a = -LRU_C * r * _softplus(-lam)
    a = jnp.exp(log_a)
    x2 = 2.0 * log_a
    m1 = jnp.where(x2 > -0.02, -x2 * (1.0 + x2 * (0.5 + x2 * (1.0 / 6.0 + x2 * (1.0 / 24.0)))), 1.0 - jnp.exp(x2))
    return a, jnp.sqrt(m1) * (i * xc)


def _f_mix(o, ys, h, ug, g_mla, g_lru):
    o = o.astype(F32)
    y_mla = _rms(o, g_mla, HEADS * VDIM)
    y_lru = _rms(h.astype(F32) * _gelu_tanh(ug[:, 256:512].astype(F32)), g_lru, 256)
    return (jnp.concatenate([y_mla, ys.astype(F32), y_lru], axis=1),)


def _f_xattn(q, k, v):
    hd = D // MEM_HEADS
    outs = []
    for h in range(MEM_HEADS):
        sl = slice(h * hd, (h + 1) * hd)
        s = _bdot(q[:, sl], k[:, sl], "nt") * (1.0 / math.sqrt(hd))
        s = s - jnp.max(s, axis=-1, keepdims=True)
        p = jnp.exp(s)
        p = p / jnp.sum(p, axis=-1, keepdims=True)
        outs.append(_bdot(p, v[:, sl], "nn"))
    return (jnp.concatenate(outs, axis=1),)


def _split_dot(tri, a, kind):
    a_hi = a.astype(BF16)
    r1 = a - a_hi.astype(F32)
    a_mid = r1.astype(BF16)
    a_lo = (r1 - a_mid.astype(F32)).astype(BF16)
    return _dot(tri, a_hi, kind) + _dot(tri, a_mid, kind) + _dot(tri, a_lo, kind)


@jax.custom_vjp
def _tri_cumsum(tri, a):
    return _split_dot(tri, a, "nn")


def _tri_cumsum_fwd(tri, a):
    return _split_dot(tri, a, "nn"), tri


def _tri_cumsum_bwd(tri, g):
    return jnp.zeros_like(tri), _split_dot(tri, g, "tn")


_tri_cumsum.defvjp(_tri_cumsum_fwd, _tri_cumsum_bwd)


def _f_ssd_chunk(c, ug, s0, s1, dtb, alog, dsk, ng):
    L = c.shape[0]
    c = c.astype(F32)
    xbc = _silu(c)
    xs, bm, cm = xbc[:, 0:256], xbc[:, 256:384], xbc[:, 384:512]
    z = ug[:, 0:256].astype(F32)
    dt = _softplus(ug[:, 512:640].astype(F32) + dtb)
    a = dt * (-jnp.exp(alog))
    rowi = lax.broadcasted_iota(jnp.int32, (L, L), 0)
    coli = lax.broadcasted_iota(jnp.int32, (L, L), 1)
    tril = rowi >= coli
    acum = _tri_cumsum(tril.astype(BF16), a)
    acum_t = acum.T
    lane = _lane((1, SLAB))
    lo = lane < 64
    ys, new_s = [], []
    for g in range(2):
        gm = (lane >= 64 * g) & (lane < 64 * g + 64)
        bg, cg = jnp.where(gm, bm, 0.0), jnp.where(gm, cm, 0.0)
        cb = _bdot(cg, bg, "nt")
        x = xs[:, SLAB * g:SLAB * (g + 1)]
        h0, h1 = 2 * g, 2 * g + 1
        ac0, ac1 = _col(acum, h0), _col(acum, h1)
        xdt = x * jnp.where(lo, _col(dt, h0), _col(dt, h1))
        ac_l = jnp.where(lo, ac0, ac1)
        tot = acum[L - 1:L, :]
        tot_l = jnp.where(lo, _col(tot, h0), _col(tot, h1))
        yd = jnp.zeros((L, SLAB), F32)
        for hh, acc, hm in ((h0, ac0, lo), (h1, ac1, jnp.logical_not(lo))):
            seg = acc - acum_t[hh:hh + 1, :]
            lm = jnp.where(tril, jnp.exp(jnp.where(tril, seg, 0.0)), 0.0)
            yd = yd + _bdot(cb * lm, jnp.where(hm, xdt, 0.0), "nn")
        sg = (s0, s1)[g]
        y_off = _bdot(cg, sg, "nn") * jnp.exp(ac_l)
        st = _bdot(bg, xdt * jnp.exp(tot_l - ac_l), "tn")
        new_s.append(jnp.exp(tot_l) * sg + st)
        y = yd + y_off + jnp.where(lo, _col(dsk, h0), _col(dsk, h1)) * x
        y = y * _silu(z[:, SLAB * g:SLAB * (g + 1)])
        ys.append(_rms(y, ng[:, SLAB * g:SLAB * (g + 1)], SLAB))
    return jnp.concatenate(ys, axis=1), new_s[0], new_s[1]


_SSD_TILE = 512


def _ssd_fwd(c, ug, dtb, alog, dsk, ng, *, name):
    T = c.shape[0]
    tm = min(_SSD_TILE, T)
    ncs = tm // SSM_CHUNK
    nc = T // SSM_CHUNK

    def body(c_ref, ug_ref, dtb_ref, alog_ref, dsk_ref, ng_ref, y_ref, sall_ref, s_scr):
        @pl.when(pl.program_id(0) == 0)
        def _():
            s_scr[...] = jnp.zeros_like(s_scr)

        s0, s1 = s_scr[0], s_scr[1]
        for k in range(ncs):
            rows = slice(k * SSM_CHUNK, (k + 1) * SSM_CHUNK)
            sall_ref[k, 0] = s0
            sall_ref[k, 1] = s1
            y, s0, s1 = _f_ssd_chunk(c_ref[rows, :], ug_ref[rows, :], s0, s1, dtb_ref[...], alog_ref[...],
                                     dsk_ref[...], ng_ref[...])
            y_ref[rows, :] = y
        s_scr[0] = s0
        s_scr[1] = s1

    y, sall = pl.pallas_call(
        body,
        name=name,
        grid=(T // tm,),
        in_specs=[_row_spec(c, tm), _row_spec(ug, tm)] + [_full_spec(p) for p in (dtb, alog, dsk, ng)],
        out_specs=[pl.BlockSpec((tm, 256), lambda i: (i, 0)), pl.BlockSpec((ncs, 2, SLAB, SLAB), lambda i: (i, 0, 0, 0))],
        out_shape=[jax.ShapeDtypeStruct((T, 256), F32), jax.ShapeDtypeStruct((nc, 2, SLAB, SLAB), F32)],
        scratch_shapes=[pltpu.VMEM((2, SLAB, SLAB), F32)],
        compiler_params=_cparams(("arbitrary",)),
    )(c, ug, dtb, alog, dsk, ng)
    return y, sall


def _ssd_bwd(c, ug, sall, dy, dtb, alog, dsk, ng, *, name):
    T = c.shape[0]
    tm = min(_SSD_TILE, T)
    ncs = tm // SSM_CHUNK
    nt = T // tm

    def body(c_ref, ug_ref, sall_ref, dy_ref, dtb_ref, alog_ref, dsk_ref, ng_ref,
             dc_ref, dug_ref, ddtb_ref, dalog_ref, ddsk_ref, dng_ref, ds_scr):
        @pl.when(pl.program_id(0) == 0)
        def _():
            ds_scr[...] = jnp.zeros_like(ds_scr)
            for o in (ddtb_ref, dalog_ref, ddsk_ref, dng_ref):
                o[...] = jnp.zeros_like(o)

        ds0, ds1 = ds_scr[0], ds_scr[1]
        for k in reversed(range(ncs)):
            rows = slice(k * SSM_CHUNK, (k + 1) * SSM_CHUNK)
            prim = (c_ref[rows, :].astype(F32), ug_ref[rows, :].astype(F32), sall_ref[k, 0], sall_ref[k, 1],
                    dtb_ref[...], alog_ref[...], dsk_ref[...], ng_ref[...])
            _, vjp = jax.vjp(_f_ssd_chunk, *prim)
            dc, dug, ds0, ds1, g_dtb, g_alog, g_dsk, g_ng = vjp((dy_ref[rows, :].astype(F32), ds0, ds1))
            dc_ref[rows, :] = dc
            dug_ref[rows, :] = dug
            ddtb_ref[...] += g_dtb
            dalog_ref[...] += g_alog
            ddsk_ref[...] += g_dsk
            dng_ref[...] += g_ng
        ds_scr[0] = ds0
        ds_scr[1] = ds1

    rev = lambda i: (nt - 1 - i, 0)
    params = (dtb, alog, dsk, ng)
    res = pl.pallas_call(
        body,
        name=name,
        grid=(nt,),
        in_specs=[pl.BlockSpec((tm, c.shape[1]), rev), pl.BlockSpec((tm, ug.shape[1]), rev),
                  pl.BlockSpec((ncs, 2, SLAB, SLAB), lambda i: (nt - 1 - i, 0, 0, 0)), pl.BlockSpec((tm, 256), rev)]
        + [_full_spec(p) for p in params],
        out_specs=[pl.BlockSpec((tm, 512), rev), pl.BlockSpec((tm, U_GATE), rev)] + [_full_spec(p) for p in params],
        out_shape=[jax.ShapeDtypeStruct((T, 512), F32), jax.ShapeDtypeStruct((T, U_GATE), F32)]
        + [jax.ShapeDtypeStruct(p.shape, F32) for p in params],
        scratch_shapes=[pltpu.VMEM((2, SLAB, SLAB), F32)],
        compiler_params=_cparams(("arbitrary",)),
    )(c, ug, sall, dy, *params)
    return res


_CONV_TILE = 512
_HALO = 8
_CONV_W = 4


def _conv_fwd(u, w, b, *, name):
    T, C = u.shape
    tm = min(_CONV_TILE, T)
    hb = tm // _HALO

    def body(u_ref, prev_ref, w_ref, b_ref, y1_ref, y2_ref, ext):
        i = pl.program_id(0)
        ext[0:_HALO, :] = jnp.where(i > 0, prev_ref[...], 0.0)
        ext[_HALO:, :] = u_ref[...]
        y = jnp.broadcast_to(b_ref[...], (tm, C))
        for k in range(_CONV_W):
            y = y + ext[_HALO - (_CONV_W - 1) + k:_HALO - (_CONV_W - 1) + k + tm, :] * w_ref[k:k + 1, :]
        y1_ref[...] = y[:, 0:512]
        y2_ref[...] = y[:, 512:768]

    return pl.pallas_call(
        body,
        name=name,
        grid=(T // tm,),
        in_specs=[_row_spec(u, tm), pl.BlockSpec((_HALO, C), lambda i: (jnp.maximum(i * hb - 1, 0), 0)),
                  _full_spec(w), _full_spec(b)],
        out_specs=[pl.BlockSpec((tm, 512), lambda i: (i, 0)), pl.BlockSpec((tm, 256), lambda i: (i, 0))],
        out_shape=[jax.ShapeDtypeStruct((T, 512), F32), jax.ShapeDtypeStruct((T, 256), F32)],
        scratch_shapes=[pltpu.VMEM((tm + _HALO, C), F32)],
        compiler_params=_cparams(("parallel",)),
    )(u, u, w, b)


def _conv_bwd(u, dy1, dy2, w, *, name):
    T, C = u.shape
    tm = min(_CONV_TILE, T)
    hb = tm // _HALO
    nt = T // tm

    def body(u_ref, prev_ref, dy1_ref, next1_ref, dy2_ref, next2_ref, w_ref, du_ref, dw_ref, db_ref, ext, dext):
        i = pl.program_id(0)
        ext[0:_HALO, :] = jnp.where(i > 0, prev_ref[...], 0.0)
        ext[_HALO:, :] = u_ref[...]
        dext[0:tm, 0:512] = dy1_ref[...]
        dext[0:tm, 512:768] = dy2_ref[...]
        dext[tm:, 0:512] = jnp.where(i < nt - 1, next1_ref[...], 0.0)
        dext[tm:, 512:768] = jnp.where(i < nt - 1, next2_ref[...], 0.0)

        @pl.when(i == 0)
        def _():
            dw_ref[...] = jnp.zeros_like(dw_ref)
            db_ref[...] = jnp.zeros_like(db_ref)

        dy = dext[0:tm, :]
        du = jnp.zeros((tm, C), F32)
        for k in range(_CONV_W):
            du = du + dext[_CONV_W - 1 - k:_CONV_W - 1 - k + tm, :] * w_ref[k:k + 1, :]
            xk = ext[_HALO - (_CONV_W - 1) + k:_HALO - (_CONV_W - 1) + k + tm, :]
            dw_ref[k:k + 1, :] += jnp.sum(dy * xk, axis=0, keepdims=True)
        du_ref[...] = du
        db_ref[...] += jnp.sum(dy, axis=0, keepdims=True)

    nxt = lambda i: (jnp.minimum((i + 1) * hb, T // _HALO - 1), 0)
    return pl.pallas_call(
        body,
        name=name,
        grid=(nt,),
        in_specs=[_row_spec(u, tm), pl.BlockSpec((_HALO, C), lambda i: (jnp.maximum(i * hb - 1, 0), 0)),
                  _row_spec(dy1, tm), pl.BlockSpec((_HALO, 512), nxt), _row_spec(dy2, tm), pl.BlockSpec((_HALO, 256), nxt),
                  _full_spec(w)],
        out_specs=[pl.BlockSpec((tm, C), lambda i: (i, 0)), _full_spec(w), pl.BlockSpec((1, C), lambda i: (0, 0))],
        out_shape=[jax.ShapeDtypeStruct((T, C), F32), jax.ShapeDtypeStruct(w.shape, F32), jax.ShapeDtypeStruct((1, C), F32)],
        scratch_shapes=[pltpu.VMEM((tm + _HALO, C), F32), pltpu.VMEM((tm + _HALO, C), F32)],
        compiler_params=_cparams(("arbitrary",)),
    )(u, u, dy1, dy1, dy2, dy2, w)


_SCAN_TILE = 1024
_SUB = 8


def _shift_rows(x, d, fill, up):
    r = lax.broadcasted_iota(jnp.int32, x.shape, 0)
    if up:
        return jnp.where(r < _SUB - d, pltpu.roll(x, _SUB - d, 0), fill)
    return jnp.where(r >= d, pltpu.roll(x, d, 0), fill)


def _lru_scan_fwd(a, b, *, name):
    T, W = a.shape
    tr = min(_SCAN_TILE, T)

    def body(a_ref, b_ref, h_ref, hp_ref, carry):
        @pl.when(pl.program_id(0) == 0)
        def _():
            carry[...] = jnp.zeros_like(carry)

        def step(t, cr):
            rows = pl.ds(pl.multiple_of(t * _SUB, _SUB), _SUB)
            aa, bb = a_ref[rows, :], b_ref[rows, :]
            for d in (1, 2, 4):
                bb = bb + aa * _shift_rows(bb, d, 0.0, False)
                aa = aa * _shift_rows(aa, d, 1.0, False)
            h = bb + aa * cr
            h_ref[rows, :] = h
            r = lax.broadcasted_iota(jnp.int32, h.shape, 0)
            hp_ref[rows, :] = jnp.where(r >= 1, pltpu.roll(h, 1, 0), cr)
            return jnp.broadcast_to(h[_SUB - 1:_SUB, :], (_SUB, W))

        carry[...] = lax.fori_loop(0, tr // _SUB, step, carry[...])

    return pl.pallas_call(
        body,
        name=name,
        grid=(T // tr,),
        in_specs=[_row_spec(a, tr), _row_spec(b, tr)],
        out_specs=[pl.BlockSpec((tr, W), lambda i: (i, 0))] * 2,
        out_shape=[jax.ShapeDtypeStruct((T, W), F32)] * 2,
        scratch_shapes=[pltpu.VMEM((_SUB, W), F32)],
        compiler_params=_cparams(("arbitrary",)),
    )(a, b)


def _lru_scan_bwd(a, dh, hprev, *, name):
    T, W = a.shape
    tr = min(_SCAN_TILE, T)
    nt = T // tr

    def body(a_ref, dh_ref, hp_ref, g_ref, da_ref, carry):
        @pl.when(pl.program_id(0) == 0)
        def _():
            carry[...] = jnp.zeros_like(carry)

        nsub = tr // _SUB

        def step(s, cr):
            t = nsub - 1 - s
            rows = pl.ds(pl.multiple_of(t * _SUB, _SUB), _SUB)
            a_t = a_ref[rows, :]
            aa = _shift_rows(a_t, 1, 1.0, True)
            bb = dh_ref[rows, :]
            for d in (1, 2, 4):
                bb = bb + aa * _shift_rows(bb, d, 0.0, True)
                aa = aa * _shift_rows(aa, d, 1.0, True)
            g = bb + aa * cr
            g_ref[rows, :] = g
            da_ref[rows, :] = g * hp_ref[rows, :]
            return jnp.broadcast_to(a_t[0:1, :] * g[0:1, :], (_SUB, W))

        carry[...] = lax.fori_loop(0, nsub, step, carry[...])

    rev = lambda i: (nt - 1 - i, 0)
    return pl.pallas_call(
        body,
        name=name,
        grid=(nt,),
        in_specs=[pl.BlockSpec((tr, W), rev)] * 3,
        out_specs=[pl.BlockSpec((tr, W), rev)] * 2,
        out_shape=[jax.ShapeDtypeStruct((T, W), F32)] * 2,
        scratch_shapes=[pltpu.VMEM((_SUB, W), F32)],
        compiler_params=_cparams(("arbitrary",)),
    )(a, dh, hprev)


_ATT_BLK = 512
_ATT_QPARTS = 2
_ATT_BWD_TRIP = 4
_ATT_SCALE = 1.0 / math.sqrt(NOPE + ROPE)
_ATT_SCALE2 = _ATT_SCALE * math.log2(math.e)


def _call_with_exchange(body, send, gather, *, name, grid, in_specs, out_specs, out_shape, args):
    if send is None:
        return pl.pallas_call(body, name=name, grid=grid, in_specs=in_specs, out_specs=out_specs, out_shape=out_shape,
                              compiler_params=_cparams(("parallel", "arbitrary")))(*args)
    n_in, n_out = len(in_specs), len(out_specs)

    def riding(*refs):
        comm = (refs[n_in], refs[n_in + 1 + n_out]) + tuple(refs[n_in + 2 + n_out:])
        h, i = pl.program_id(0), pl.program_id(1)

        @pl.when((h == 0) & (i == 0))
        def _():
            _chip_start(*comm, gather=gather)

        body(*refs[:n_in], *refs[n_in + 1:n_in + 1 + n_out])

        @pl.when((h == grid[0] - 1) & (i == grid[1] - 1))
        def _():
            _chip_wait(*comm, gather=gather)

    shape = (4,) + send.shape if gather else send.shape
    return pl.pallas_call(
        riding, name=name, grid=grid, in_specs=in_specs + [_ANY], out_specs=out_specs + [_ANY],
        out_shape=out_shape + [jax.ShapeDtypeStruct(shape, send.dtype)], scratch_shapes=_CHIP_SEMS,
        compiler_params=_cparams(("arbitrary", "arbitrary")))(*args, send)


def _attn_fwd(q, kv, *, name, send=None):
    T = q.shape[0]
    blk = min(_ATT_BLK, T)
    nq = T // blk
    parts = _ATT_QPARTS if nq % _ATT_QPARTS == 0 else 1

    def body(q_ref, kv_ref, o_ref, lse_ref):
        i = pl.program_id(1)
        def one(part, j0, nblk, carry, masked):
            m, l, acc = carry
            rows = pl.ds(pl.multiple_of(j0 * blk, blk), nblk * blk)
            s = _dot(q_ref[part * blk:(part + 1) * blk, :], kv_ref[rows, 0:SLAB], "nt")
            if masked:
                col = lax.broadcasted_iota(jnp.int32, s.shape, 1)
                row = lax.broadcasted_iota(jnp.int32, s.shape, 0)
                s = jnp.where(col <= row + (nblk - 1) * blk, s, -jnp.inf)
            m_new = jnp.maximum(m, jnp.max(s, axis=-1, keepdims=True))
            alpha = jnp.exp2(m - m_new)
            p = jnp.exp2(s - m_new)
            l = alpha * l + jnp.sum(p, axis=-1, keepdims=True)
            acc = alpha * acc + _dot(p, kv_ref[rows, SLAB:2 * SLAB], "nn")
            return m_new, l, acc

        def step(t, carry):
            return tuple(one(part, t * parts, parts, carry[part], False) for part in range(parts))

        init = ((jnp.full((blk, 1), -jnp.inf, F32), jnp.zeros((blk, 1), F32), jnp.zeros((blk, SLAB), F32)),) * parts
        carry = lax.fori_loop(0, i // 2, lambda t, c: step(2 * t + 1, step(2 * t, c)), init)
        carry = lax.fori_loop((i // 2) * 2, i, step, carry)
        for part in range(parts):
            m, l, acc = one(part, i * parts, part + 1, carry[part], True)
            o_ref[part * blk:(part + 1) * blk, :] = acc / l
            lse_ref[part] = jnp.broadcast_to(m + jnp.log(l) * math.log2(math.e), (blk, SLAB)).T[0:_SUB, :]

    return _call_with_exchange(
        body, send, True,
        name=name,
        grid=(HEADS, nq // parts),
        in_specs=[pl.BlockSpec((parts * blk, SLAB), lambda h, i: (i, h)), pl.BlockSpec((T, 2 * SLAB), lambda h, i: (0, h))],
        out_specs=[pl.BlockSpec((parts * blk, SLAB), lambda h, i: (i, h)),
                   pl.BlockSpec((None, parts, _SUB, blk), lambda h, i: (h, i, 0, 0))],
        out_shape=[jax.ShapeDtypeStruct((T, HEADS * SLAB), F32), jax.ShapeDtypeStruct((HEADS, nq, _SUB, blk), F32)],
        args=(q, kv))


def _attn_delta(do, o, *, name):
    T = o.shape[0]
    blk = min(_ATT_BLK, T)
    nq = T // blk

    def body(do_ref, o_ref, d_ref):
        for h in range(HEADS):
            cols = slice(h * SLAB, (h + 1) * SLAB)
            dl = jnp.sum(do_ref[:, cols].astype(F32) * o_ref[:, cols], axis=-1, keepdims=True)
            d_ref[h] = jnp.broadcast_to(dl, (blk, SLAB)).T[0:_SUB, :]

    return pl.pallas_call(
        body,
        name=name,
        grid=(nq,),
        in_specs=[pl.BlockSpec((blk, HEADS * SLAB), lambda i: (i, 0))] * 2,
        out_specs=pl.BlockSpec((HEADS, None, _SUB, blk), lambda i: (0, i, 0, 0)),
        out_shape=jax.ShapeDtypeStruct((HEADS, nq, _SUB, blk), F32),
        compiler_params=_cparams(("parallel",)),
    )(do, o)


def _attn_bwd(q, kv, do, lse, delta, *, name, send=None):
    T = q.shape[0]
    blk = min(_ATT_BLK, T)
    nq = T // blk

    def body(q_ref, kv_ref, do_ref, lse_ref, dl_ref, dqt_ref, dkv_ref):
        j = pl.program_id(1)

        @pl.when(j == 0)
        def _():
            dqt_ref[...] = jnp.zeros_like(dqt_ref)

        kb, vb = kv_ref[:, 0:SLAB], kv_ref[:, SLAB:2 * SLAB]
        kbt = kb.astype(F32).T.astype(BF16)
        kpos = j * blk + lax.broadcasted_iota(jnp.int32, (blk, blk), 0)

        def step(i, carry, masked):
            dk, dv = carry
            rows = pl.ds(pl.multiple_of(i * blk, blk), blk)
            qb, dob = q_ref[rows, :], do_ref[rows, :]
            st = _dot(kb, qb, "nt")
            if masked:
                qpos = i * blk + lax.broadcasted_iota(jnp.int32, (blk, blk), 1)
                st = jnp.where(kpos <= qpos, st, -jnp.inf)
            pt = jnp.exp2(st - (lse_ref[i, 0:1, :] - math.log2(math.log(2.0))))
            dpt = _dot(vb, dob, "nt")
            dst = pt * (dpt - dl_ref[i, 0:1, :])
            dv = dv + _dot(pt, dob, "nn")
            dk = dk + _dot(dst, qb, "nn")
            dqt_ref[i] += _dot(kbt, dst, "nn")
            return dk, dv

        zero = jnp.zeros((blk, SLAB), F32)
        carry = step(j, (zero, zero), True)
        rest = nq - 1 - j

        def trip(t, c):
            for u in range(_ATT_BWD_TRIP):
                c = step(j + 1 + _ATT_BWD_TRIP * t + u, c, False)
            return c

        carry = lax.fori_loop(0, rest // _ATT_BWD_TRIP, trip, carry)
        tail = j + 1 + (rest // _ATT_BWD_TRIP) * _ATT_BWD_TRIP
        dk, dv = lax.fori_loop(tail, nq, functools.partial(step, masked=False), carry)
        dkv_ref[:, 0:SLAB] = dk.astype(BF16)
        dkv_ref[:, SLAB:2 * SLAB] = (dv * (1.0 / math.log(2.0))).astype(BF16)

    stat_spec = pl.BlockSpec((None, nq, _SUB, blk), lambda h, j: (h, 0, 0, 0))
    head_t_spec = pl.BlockSpec((None, nq, SLAB, blk), lambda h, j: (h, 0, 0, 0))
    return _call_with_exchange(
        body, send, False,
        name=name,
        grid=(HEADS, nq),
        in_specs=[pl.BlockSpec((T, SLAB), lambda h, j: (0, h)), pl.BlockSpec((blk, 2 * SLAB), lambda h, j: (j, h)),
                  pl.BlockSpec((T, SLAB), lambda h, j: (0, h)), stat_spec, stat_spec],
        out_specs=[head_t_spec, pl.BlockSpec((blk, 2 * SLAB), lambda h, j: (j, h))],
        out_shape=[jax.ShapeDtypeStruct((HEADS, nq, SLAB, blk), F32), jax.ShapeDtypeStruct((T, HEADS * 2 * SLAB), BF16)],
        args=(q, kv, do, lse, delta))


def _qrope_bwd(dqt, cq, sq, *, name):
    _, nq, _, blk = dqt.shape
    T = nq * blk

    def body(dqt_ref, c_ref, s_ref, dy_ref):
        c, s = c_ref[...], s_ref[...]
        for h in range(HEADS):
            dq = dqt_ref[h].T
            dy_ref[:, h * SLAB:(h + 1) * SLAB] = (dq * c).astype(BF16)
            dy_ref[:, (HEADS + h) * SLAB:(HEADS + h + 1) * SLAB] = (dq * s).astype(BF16)

    return pl.pallas_call(
        body,
        name=name,
        grid=(nq,),
        in_specs=[pl.BlockSpec((HEADS, None, SLAB, blk), lambda i: (0, i, 0, 0)), _row_spec(cq, blk), _row_spec(sq, blk)],
        out_specs=pl.BlockSpec((blk, 2 * HEADS * SLAB), lambda i: (i, 0)),
        out_shape=jax.ShapeDtypeStruct((T, 2 * HEADS * SLAB), BF16),
        compiler_params=_cparams(("parallel",)),
    )(dqt, cq, sq)


def _loss_head(x, tgt, g, *, name, tm=256):
    T = x.shape[0]
    tm = min(tm, T)

    def body(x_ref, t_ref, g_ref, loss_ref, dx_ref, dxh_ref, dg_ref):
        def f(xv, gv):
            e = _rms(xv, gv, D) - t_ref[...]
            row = jnp.sum(e * e, axis=1, keepdims=True)
            return jnp.sum(row, axis=0, keepdims=True) * (0.5 / D)

        val, vjp = jax.vjp(f, x_ref[...], g_ref[...])
        dxv, dgv = vjp(jnp.ones((1, 1), F32))

        @pl.when(pl.program_id(0) == 0)
        def _():
            loss_ref[...] = jnp.zeros_like(loss_ref)
            dg_ref[...] = jnp.zeros_like(dg_ref)

        dx_ref[...] = dxv
        dxh_ref[...] = dxv.astype(BF16)
        dg_ref[...] += dgv
        loss_ref[...] += jnp.broadcast_to(val, loss_ref.shape)

    return pl.pallas_call(
        body,
        name=name,
        grid=(T // tm,),
        in_specs=[_row_spec(x, tm), _row_spec(tgt, tm), _full_spec(g)],
        out_specs=[pl.BlockSpec((1, SLAB), lambda i: (0, 0)), _row_spec(x, tm), _row_spec(x, tm), _full_spec(g)],
        out_shape=[jax.ShapeDtypeStruct((1, SLAB), F32), jax.ShapeDtypeStruct(x.shape, F32),
                   jax.ShapeDtypeStruct(x.shape, BF16), jax.ShapeDtypeStruct(g.shape, F32)],
        compiler_params=_cparams(("arbitrary",)),
    )(x, tgt, g)


def _row_tile(rows, cols, budget=256 * 1024):
    best = None
    for t in range(16, rows + 1, 16):
        if rows % t == 0 and t * cols <= budget:
            best = t
    return best or rows


def _sum_fixed(x, out_dtype, *, name):
    n, R, C = x.shape
    tr = _row_tile(R, C)

    def body(x_ref, o_ref):
        acc = x_ref[0].astype(F32)
        for k in range(1, n):
            acc = acc + x_ref[k].astype(F32)
        o_ref[...] = acc.astype(o_ref.dtype)

    return pl.pallas_call(
        body,
        name=name,
        grid=(R // tr,),
        in_specs=[pl.BlockSpec((n, tr, C), lambda i: (0, i, 0))],
        out_specs=pl.BlockSpec((tr, C), lambda i: (i, 0)),
        out_shape=jax.ShapeDtypeStruct((R, C), out_dtype),
        compiler_params=_cparams(("parallel",)),
    )(x)


def _adamw(w, g, m, v, *, name):
    R, C = w.shape
    tr = _row_tile(R, C, 128 * 1024)

    def body(w_ref, g_ref, m_ref, v_ref, d_ref, nm_ref, nv_ref):
        gv = g_ref[...]
        mv = B1 * m_ref[...] + (1.0 - B1) * gv
        vv = B2 * v_ref[...] + (1.0 - B2) * (gv * gv)
        m_hat = mv / (1.0 - B1 ** STEP)
        v_hat = vv / (1.0 - B2 ** STEP)
        d_ref[...] = -LR * (m_hat / (jnp.sqrt(v_hat) + AEPS) + WD * w_ref[...])
        nm_ref[...] = mv
        nv_ref[...] = vv

    spec = pl.BlockSpec((tr, C), lambda i: (i, 0))
    return pl.pallas_call(
        body, name=name, grid=(R // tr,), in_specs=[spec] * 4, out_specs=[spec] * 3,
        out_shape=[jax.ShapeDtypeStruct((R, C), F32)] * 3, compiler_params=_cparams(("parallel",)),
    )(w, g, m, v)


_FLIPS = ((1, 0), (0, 1), (1, 1))
_ANY = pl.BlockSpec(memory_space=pl.ANY)


def _me():
    return lax.axis_index("x"), lax.axis_index("y"), lax.axis_index("c")


def _flip(mx, my, f):
    return (1 - mx if f[0] else mx), (1 - my if f[1] else my)


_CHIP_SEMS = [pltpu.SemaphoreType.DMA((3,)), pltpu.SemaphoreType.DMA((3,)), pltpu.SemaphoreType.DMA]


def _chip_copies(x_ref, out_ref, send_sems, recv_sems, local_sem, gather):
    mx, my, mc = _me()
    mine = 2 * mx + my
    outgoing, incoming = [], []
    for k, f in enumerate(_FLIPS):
        px, py = _flip(mx, my, f)
        peer = 2 * px + py
        src = x_ref if gather else x_ref.at[peer]
        for dst, to in ((out_ref.at[mine], outgoing), (out_ref.at[peer], incoming)):
            to.append(pltpu.make_async_remote_copy(src_ref=src, dst_ref=dst, send_sem=send_sems.at[k],
                                                   recv_sem=recv_sems.at[k], device_id=(px, py, mc), device_id_type=MESH))
    local = None if gather else pltpu.make_async_copy(x_ref.at[mine], out_ref.at[mine], local_sem)
    return outgoing, incoming, local


def _chip_start(*refs, gather):
    outgoing, _, local = _chip_copies(*refs, gather)
    if local is not None:
        local.start()
    for cp in outgoing:
        cp.start()


def _chip_wait(*refs, gather):
    outgoing, incoming, local = _chip_copies(*refs, gather)
    for cp in incoming:
        cp.wait_recv()
    for cp in outgoing:
        cp.wait_send()
    if local is not None:
        local.wait()


def _chip_exchange(x, gather, *, name):
    shape = x.shape if not gather else (4,) + x.shape

    def body(*refs):
        _chip_start(*refs, gather=gather)
        _chip_wait(*refs, gather=gather)

    return pl.pallas_call(
        body, name=name, in_specs=[_ANY], out_specs=_ANY, out_shape=jax.ShapeDtypeStruct(shape, x.dtype),
        scratch_shapes=_CHIP_SEMS,
    )(x)


def _core_exchange(x, half, *, name):
    shape = x.shape[1:] if half else x.shape

    def body(x_ref, out_ref, send_sem, recv_sem):
        mx, my, mc = _me()
        src = x_ref.at[1 - mc] if half else x_ref
        cp = pltpu.make_async_remote_copy(src_ref=src, dst_ref=out_ref, send_sem=send_sem, recv_sem=recv_sem,
                                          device_id=(mx, my, 1 - mc), device_id_type=MESH)
        cp.start()
        cp.wait()

    return pl.pallas_call(
        body, name=name, in_specs=[_ANY], out_specs=_ANY, out_shape=jax.ShapeDtypeStruct(shape, x.dtype),
        scratch_shapes=[pltpu.SemaphoreType.DMA, pltpu.SemaphoreType.DMA],
    )(x)


def _core_gather(x, g4, *, name):
    def body(x_ref, g_ref, out_ref, send_sems, recv_sems):
        mx, my, mc = _me()
        srcs = [(x_ref, 2 * mx + my)] + [(g_ref.at[2 * px + py], 2 * px + py) for px, py in (_flip(mx, my, f) for f in _FLIPS)]
        copies = [pltpu.make_async_remote_copy(src_ref=src, dst_ref=out_ref.at[slot], send_sem=send_sems.at[k],
                                               recv_sem=recv_sems.at[k], device_id=(mx, my, 1 - mc), device_id_type=MESH)
                  for k, (src, slot) in enumerate(srcs)]
        for cp in copies:
            cp.start()
        for cp in copies:
            cp.wait()

    return pl.pallas_call(
        body, name=name, in_specs=[_ANY, _ANY], out_specs=_ANY, out_shape=jax.ShapeDtypeStruct(g4.shape, g4.dtype),
        scratch_shapes=[pltpu.SemaphoreType.DMA((4,)), pltpu.SemaphoreType.DMA((4,))],
    )(x, g4)


def _finish_gather(x, g4, *, name):
    mx, my, _ = _me()
    own = (jnp.arange(4) == 2 * mx + my).reshape((4,) + (1,) * x.ndim)
    return jnp.where(own, x[None], g4), _core_gather(x, g4, name=name + "_cores")


def _by_core(mine, other):
    mc = lax.axis_index("c")
    return jnp.stack([jnp.where(mc == 0, mine, other), jnp.where(mc == 0, other, mine)])


def _all_gather(x, *, name):
    return _by_core(*_finish_gather(x, _chip_exchange(x, True, name=name + "_chips"), name=name))


def _add_own_half(x, got, *, name, out_dtype=BF16):
    _, R, C = x.shape
    tr = _row_tile(R, C)

    def body(c_ref, x_ref, g_ref, o_ref):
        o_ref[...] = (x_ref[...] + g_ref[...]).astype(o_ref.dtype)

    return pl.pallas_call(
        body,
        name=name,
        grid_spec=pltpu.PrefetchScalarGridSpec(
            num_scalar_prefetch=1, grid=(R // tr,),
            in_specs=[pl.BlockSpec((None, tr, C), lambda i, c: (c[0], i, 0)), pl.BlockSpec((tr, C), lambda i, c: (i, 0))],
            out_specs=pl.BlockSpec((tr, C), lambda i, c: (i, 0))),
        out_shape=jax.ShapeDtypeStruct((R, C), out_dtype),
        compiler_params=_cparams(("parallel",)),
    )(lax.axis_index("c").astype(jnp.int32).reshape(1), x, got)


def _chip_sums(x, *, name):
    _, _, R, C = x.shape
    got = _core_exchange(x, True, name=name + "_cores")
    return _add_own_half(x.reshape(2, 4 * R, C), got.reshape(4 * R, C), name=name + "_add").reshape(4, R, C)


def _all_reduce(x, *, name):
    g = _all_gather(x, name=name)
    return _sum_fixed(g.reshape((N_DEV,) + x.shape), F32, name=name + "_sum")


WEIGHTS = ['mix_norm_g', 'w_in', 'mla_q_norm_g', 'mla_kv_norm_g', 'mla_w_uq', 'mla_w_ukv', 'mla_out_g', 'ssm_conv_w',
           'ssm_conv_b', 'ssm_dt_bias', 'ssm_a_log', 'ssm_d', 'ssm_norm_g', 'lru_conv_w', 'lru_conv_b', 'lru_w_a',
           'lru_b_a', 'lru_w_i', 'lru_b_i', 'lru_lambda', 'lru_out_g', 'w_out', 'xattn_norm_g', 'mem_norm_g', 'w_mq',
           'w_mk', 'w_mv', 'w_mo', 'mlp_norm_g', 'w_mlp1', 'w_mlp2', 'final_norm_g']
ROW_SHARDED = ('w_in', 'w_out', 'w_mq', 'w_mk', 'w_mv', 'w_mo', 'w_mlp2')
COL_SHARDED = ('mla_w_uq', 'mla_w_ukv', 'w_mlp1')
BIG = tuple(n for n in WEIGHTS if n in ROW_SHARDED + COL_SHARDED)
CONV_SHARDED = ('ssm_conv_w', 'lru_conv_w')
SMALL = tuple(n for n in WEIGHTS if n not in BIG)
PACK_C = 1024


def _pack(arrs, dtype, lead=()):
    flat = jnp.concatenate([a.reshape(lead + (-1,)).astype(dtype) for a in arrs], axis=-1)
    n = flat.shape[-1]
    rows = -(-n // (16 * PACK_C)) * 16
    flat = jnp.pad(flat, [(0, 0)] * len(lead) + [(0, rows * PACK_C - n)])
    return flat.reshape(lead + (rows, PACK_C))


def _unpack(packed, shapes, lead=()):
    flat = packed.reshape(lead + (-1,))
    out, off = [], 0
    for s in shapes:
        n = math.prod(s)
        out.append(flat[..., off:off + n].reshape(lead + tuple(s)))
        off += n
    return out


def _pad_lanes(v, n=SLAB):
    return jnp.pad(v.astype(F32), (0, n - v.shape[0])).reshape(1, n)


PIECES = ('win', 'wq', 'wkv', 'wout', 'w_mq', 'w_mk', 'w_mv', 'w_mo', 'w_mlp1', 'w_mlp2')
PIECE_SOURCE = {'win': 'w_in', 'wq': 'mla_w_uq', 'wkv': 'mla_w_ukv', 'wout': 'w_out'}
PIECE_SHAPE = {'win': (128, 2048), 'wq': (Q_RANK, 2 * SLAB), 'wkv': (KV_RANK, 2 * SLAB), 'wout': (128, D),
               'w_mq': (128, D), 'w_mk': (128, D), 'w_mv': (128, D), 'w_mo': (128, D), 'w_mlp1': (512, D),
               'w_mlp2': (512, D)}
PIECE_COLS = ('wq', 'wkv')


def _k_win(w):
    kr = w[..., 384:416]
    zc = lambda k: jnp.zeros(w.shape[:-1] + (k,), w.dtype)
    return jnp.concatenate(
        [w[..., 0:384], kr, zc(96), kr[..., 16:32], kr[..., 0:16], zc(96),
         w[..., 416:672], w[..., 1444:1700], w[..., 1184:1188], zc(124),
         w[..., 672:1184], w[..., 1188:1444]], axis=-1)


def _k_win_inv(m):
    gk = m[..., 384:416] + jnp.concatenate([m[..., 528:544], m[..., 512:528]], axis=-1)
    return jnp.concatenate([m[..., 0:384], gk, m[..., 640:896], m[..., 1280:1792], m[..., 1152:1156], m[..., 1792:2048],
                            m[..., 896:1152]], axis=-1)


def _k_wq(w):
    nh = w.shape[-1] // (NOPE + ROPE)
    w = w.reshape(w.shape[:-1] + (nh, NOPE + ROPE))
    nope, r1, r2 = w[..., :NOPE], w[..., NOPE:NOPE + 16], w[..., NOPE + 16:]
    z = lambda k: jnp.zeros(w.shape[:-1] + (k,), w.dtype)
    both = jnp.stack([jnp.concatenate([nope, r1, r2, z(32)], -1), jnp.concatenate([z(64), r2, r1, z(32)], -1)], axis=-3)
    return both.reshape(w.shape[:-2] + (2 * nh * SLAB,))


def _k_wq_inv(m):
    nh = m.shape[-1] // (2 * SLAB)
    m = m.reshape(m.shape[:-1] + (2, nh, SLAB))
    q0, q1 = m[..., 0, :, :], m[..., 1, :, :]
    w = jnp.concatenate([q0[..., :64], q0[..., 64:80] + q1[..., 80:96], q0[..., 80:96] + q1[..., 64:80]], -1)
    return w.reshape(w.shape[:-2] + (nh * (NOPE + ROPE),))


def _k_wkv(w):
    nh = w.shape[-1] // (NOPE + VDIM)
    w = w.reshape(w.shape[:-1] + (nh, NOPE + VDIM))
    z = jnp.zeros(w.shape[:-1] + (64,), w.dtype)
    return jnp.concatenate([w[..., :NOPE], z, w[..., NOPE:], z], -1).reshape(w.shape[:-2] + (nh * 2 * SLAB,))


def _k_wkv_inv(m):
    nh = m.shape[-1] // (2 * SLAB)
    m = m.reshape(m.shape[:-1] + (nh, 2 * SLAB))
    return jnp.concatenate([m[..., :NOPE], m[..., SLAB:SLAB + VDIM]], -1).reshape(m.shape[:-2] + (nh * (NOPE + VDIM),))


def _k_swap(w):
    return jnp.swapaxes(w, -1, -2)


_K_FWD = {'win': _k_win, 'wq': _k_wq, 'wkv': _k_wkv, 'w_mlp1': _k_swap}
_K_INV = {'win': _k_win_inv, 'wq': _k_wq_inv, 'wkv': _k_wkv_inv, 'w_mlp1': _k_swap}


def _assemble(piece, g):
    _, _, a, b = g.shape
    if piece == 'wq':
        return g.reshape(2, 4, a, 2, SLAB).transpose(2, 3, 1, 0, 4).reshape(a, N_DEV * b)
    if piece in PIECE_COLS:
        return g.transpose(2, 1, 0, 3).reshape(a, N_DEV * b)
    return g.transpose(1, 0, 2, 3).reshape(N_DEV * a, b)


def _disassemble(piece, full):
    a, b = PIECE_SHAPE[piece]
    if piece == 'wq':
        return full.reshape(a, 2, 4, 2, SLAB).transpose(3, 2, 0, 1, 4).reshape(2, 4, a, b)
    if piece in PIECE_COLS:
        return full.reshape(a, 4, 2, b).transpose(2, 1, 0, 3)
    return full.reshape(4, 2, a, b).transpose(1, 0, 2, 3)


def _piece_rows(piece):
    a, b = PIECE_SHAPE[piece]
    return a * b // PACK_C


def _prep_late(pieces):
    if 'wout' in pieces:
        wout = pieces['wout']
        mla_rows = jnp.pad(wout[:HEADS * VDIM].reshape(HEADS, VDIM, D), ((0, 0), (0, SLAB - VDIM), (0, 0)))
        pieces['wout'] = jnp.concatenate([mla_rows.reshape(HEADS * SLAB, D), wout[HEADS * VDIM:]], axis=0)
    return pieces


def _prep_layer(pieces, Ws, l):
    P = _prep_late(dict(pieces))
    ri, ci = jnp.arange(SLAB)[:, None], jnp.arange(2 * SLAB)[None, :]
    sel = ((ri < ROPE) & (ci == ri + NOPE)).astype(P['wkv'].dtype)
    P['wkv'] = jnp.concatenate([P['wkv'], jnp.tile(sel, (1, HEADS))], axis=0)
    W = Ws
    row = lambda n: W[n][l].astype(F32).reshape(1, -1)
    for n in ('mix_norm_g', 'mla_q_norm_g', 'mla_kv_norm_g', 'ssm_norm_g', 'lru_lambda', 'lru_out_g', 'xattn_norm_g',
              'mem_norm_g', 'mlp_norm_g'):
        P[n] = row(n)
    P['mla_out_g'] = jnp.pad(W['mla_out_g'][l].astype(F32).reshape(HEADS, VDIM), ((0, 0), (0, SLAB - VDIM))).reshape(1, -1)
    for n in ('ssm_dt_bias', 'ssm_a_log', 'ssm_d'):
        P[n] = _pad_lanes(W[n][l])
    P['conv_w'] = jnp.pad(jnp.concatenate([W['ssm_conv_w'][l], W['lru_conv_w'][l]], axis=1).astype(F32), ((0, 4), (0, 0)))
    P['conv_b'] = jnp.concatenate([W['ssm_conv_b'][l], W['lru_conv_b'][l]]).astype(F32).reshape(1, -1)
    for n in ('lru_w_a', 'lru_w_i'):
        P[n] = jnp.concatenate([jnp.pad(W[n][l, k].astype(F32), ((0, 0), (64 * k, 192 - 64 * k))) for k in range(4)], axis=0)
    for n in ('lru_b_a', 'lru_b_i'):
        P[n] = W[n][l].astype(F32).reshape(1, -1)
    return P


def _unprep_pieces(G, names=PIECES):
    o = {n: G[n] for n in names}
    if 'wkv' in o:
        o['wkv'] = G['wkv'][:KV_RANK]
    if 'wout' in o:
        wo = G['wout']
        o['wout'] = jnp.concatenate([wo[:HEADS * SLAB].reshape(HEADS, SLAB, D)[:, :VDIM].reshape(HEADS * VDIM, D),
                                     wo[HEADS * SLAB:]], axis=0)
    return o


def _unprep_small(G):
    o = {}
    for n in ('mix_norm_g', 'mla_q_norm_g', 'mla_kv_norm_g', 'ssm_norm_g', 'lru_lambda', 'lru_out_g', 'xattn_norm_g',
              'mem_norm_g', 'mlp_norm_g', 'lru_b_a', 'lru_b_i'):
        o[n] = G[n].reshape(-1)
    o['lru_b_a'] = o['lru_b_a'].reshape(4, 64)
    o['lru_b_i'] = o['lru_b_i'].reshape(4, 64)
    o['mla_out_g'] = G['mla_out_g'].reshape(HEADS, SLAB)[:, :VDIM].reshape(-1)
    for n in ('ssm_dt_bias', 'ssm_a_log', 'ssm_d'):
        o[n] = G[n][0, :4]
    o['ssm_conv_w'], o['lru_conv_w'] = G['conv_w'][:4, :512], G['conv_w'][:4, 512:]
    o['ssm_conv_b'], o['lru_conv_b'] = G['conv_b'][0, :512], G['conv_b'][0, 512:]
    for n in ('lru_w_a', 'lru_w_i'):
        o[n] = jnp.stack([G[n][64 * k:64 * (k + 1), 64 * k:64 * (k + 1)] for k in range(4)])
    return o


def _rope_tables(positions):
    half = ROPE // 2
    inv_freq = ROPE_THETA ** (-jnp.arange(half, dtype=F32) * 2.0 / ROPE)
    ang = positions.astype(F32)[:, None] * inv_freq
    cos, sin = jnp.cos(ang), jnp.sin(ang)
    T = positions.shape[0]
    z = lambda k: jnp.zeros((T, k), F32)
    ck = jnp.concatenate([cos, cos, z(96)], axis=1)
    sk = jnp.concatenate([-sin, sin, z(96)], axis=1)
    cq = jnp.concatenate([jnp.ones((T, NOPE), F32), cos, cos, z(32)], axis=1)
    sq = jnp.concatenate([z(NOPE), -sin, sin, z(32)], axis=1)
    return ck, sk, cq * _ATT_SCALE2, sq * _ATT_SCALE2


def _add_epi(acc, res):
    return (acc + res,)


def _add_norm_epi(acc, res, g):
    x = acc + res
    return x, _rms(x, g, x.shape[-1])


def _norm_bwd_epi(acc, x, res, g):
    _, vjp = jax.vjp(lambda xv, gv: _rms(xv, gv, xv.shape[-1]), x, g)
    dx, dg = vjp(acc)
    dx = dx + res
    return dx, dx, dg


def _relu2_epi(acc):
    r = jnp.maximum(acc, 0.0)
    return r, r * r


def _drelu2_epi(acc, r):
    return (acc * (2.0 * r.astype(F32)),)


def _norm(x, g, name):
    return _rows_fwd(_f_norm, [x], [g], [(x.shape[1], BF16)], name=name)[0]


def _norm_bwd(x, g, ct, add, name):
    (dx, dx16), (dg,) = _rows_vjp(_f_norm, [x], [g], [ct], name=name, drows=[0], dparams=[0], drow_dtypes=[F32],
                                  add=add, twin=True)
    return dx, dx16, dg


def _layer_fwd(x0, h1, mem, P, tabs, g_next=None, send=None, on_got=None):
    ck, sk, cq, sq = tabs
    S = {'x0': x0}
    if h1 is None:
        h1 = _norm(x0, P['mix_norm_g'], "norm_mix")
    S['h1'] = h1
    win = P['win']
    u_mla = S['u_mla'] = _mm(h1, win[:, 0:U_MLA], name="in_mla")
    u_gate = S['u_gate'] = _mm(h1, win[:, U_MLA:U_MLA + U_GATE], name="in_gate")
    u_conv = S['u_conv'] = _mm(h1, win[:, U_MLA + U_GATE:], name="in_conv")
    cqn, akv = _rows_fwd(_f_mla_prep, [u_mla, ck, sk], [P['mla_q_norm_g'], P['mla_kv_norm_g']],
                         [(Q_RANK, BF16), (2 * SLAB, BF16)], name="mla_prep")
    S['cqn'], S['akv'] = cqn, akv
    yq = _mm(cqn, P['wq'], name="q_proj")
    q = S['q'] = _rows_fwd(_f_qrope, [yq, cq, sq], [], [(HEADS * SLAB, BF16)], name="q_rope")[0]
    kv = S['kv'] = _mm(akv, P['wkv'], name="kv_proj", out_dtypes=(BF16,))
    o, lse, *got = _attn_fwd(q, kv, name="attn_fwd" if send is None else "attn_fwd_gather", send=send)
    S['o'], S['lse'] = o, lse
    if got:
        P.update(_prep_late(on_got(got[0]) or {}))
    c_ssm, c_lru = _conv_fwd(u_conv, P['conv_w'], P['conv_b'], name="conv_fwd")
    S['c_ssm'], S['c_lru'] = c_ssm, c_lru
    ys, sall = _ssd_fwd(c_ssm, u_gate, P['ssm_dt_bias'], P['ssm_a_log'], P['ssm_d'], P['ssm_norm_g'], name="ssd_fwd")
    S['ys'], S['sall'] = ys, sall
    a, b = _rows_fwd(_f_lru_gates, [c_lru], [P['lru_w_a'], P['lru_b_a'], P['lru_w_i'], P['lru_b_i'], P['lru_lambda']],
                     [(256, F32), (256, F32)], name="lru_gates", tm=_MM_ROWS)
    h, hprev = _lru_scan_fwd(a, b, name="lru_scan")
    S['a'], S['h'], S['hprev'] = a, h, hprev
    ymix = S['ymix'] = _rows_fwd(_f_mix, [o, ys, h, u_gate], [P['mla_out_g'], P['lru_out_g']],
                                 [(HEADS * SLAB + 512, BF16)], name="mix")[0]
    x1, hx = _mm(ymix, P['wout'], name="out_proj", epi=_add_norm_epi, extras=(x0, P['xattn_norm_g']), out_dtypes=(F32, BF16))
    S['x1'], S['hx'] = x1, hx
    qx = S['qx'] = _mm(hx, P['w_mq'], name="mem_q", out_dtypes=(BF16,))
    mn = S['mn'] = _norm(mem, P['mem_norm_g'], "norm_mem")
    kx = S['kx'] = _mm(mn, P['w_mk'], name="mem_k", out_dtypes=(BF16,))
    vx = S['vx'] = _mm(mn, P['w_mv'], name="mem_v", out_dtypes=(BF16,))
    ox = S['ox'] = _rows_fwd(_f_xattn, [qx], [kx, vx], [(D, BF16)], name="xattn", tm=_MM_ROWS)[0]
    x2, hm = _mm(ox, P['w_mo'], name="mem_o", epi=_add_norm_epi, extras=(x1, P['mlp_norm_g']), out_dtypes=(F32, BF16))
    S['x2'], S['hm'] = x2, hm
    r, s = _mm(hm, P['w_mlp1'], "nt", name="mlp_up", epi=_relu2_epi, out_dtypes=(BF16, BF16))
    S['r'], S['s'] = r, s
    if g_next is None:
        x3, h_next = _mm(s, P['w_mlp2'], name="mlp_down_last", epi=_add_epi, extras=(x2,)), None
    else:
        x3, h_next = _mm(s, P['w_mlp2'], name="mlp_down", epi=_add_norm_epi, extras=(x2, g_next), out_dtypes=(F32, BF16))
    return x3, h_next, S


def _layer_bwd(dx3, dx3h, mem, S, P, tabs, send=None):
    ck, sk, cq, sq = tabs
    G = {}
    da = _mm(dx3h, P['w_mlp2'], "nt", name="mlp_down_dx", epi=_drelu2_epi, extras=(S['r'],), out_dtypes=(BF16,))
    G['w_mlp2'] = _mm(S['s'], dx3h, "tn", name="mlp_down_dw")
    G['w_mlp1'] = _mm(da, S['hm'], "tn", name="mlp_up_dw")
    norm_out = dict(epi=_norm_bwd_epi, out_dtypes=(F32, BF16), col_sums=1)
    dx2, dx2h, G['mlp_norm_g'] = _mm(da, P['w_mlp1'], "nn", name="mlp_up_dx", extras=(S['x2'], dx3, P['mlp_norm_g']),
                                     **norm_out)
    dox = _mm(dx2h, P['w_mo'], "nt", name="mem_o_dx")
    G['w_mo'] = _mm(S['ox'], dx2h, "tn", name="mem_o_dw")
    (dqx,), (dkx, dvx) = _rows_vjp(_f_xattn, [S['qx']], [S['kx'], S['vx']], [dox], name="xattn_bwd", drows=[0],
                                   dparams=[0, 1], drow_dtypes=[BF16], tm=_MM_ROWS)
    G['w_mq'] = _mm(S['hx'], dqx, "tn", name="mem_q_dw")
    dx1, dx1h, G['xattn_norm_g'] = _mm(dqx, P['w_mq'], "nt", name="mem_q_dx", extras=(S['x1'], dx2, P['xattn_norm_g']),
                                       **norm_out)
    G['w_mk'] = _mm(S['mn'], dkx, "tn", name="mem_k_dw")
    G['w_mv'] = _mm(S['mn'], dvx, "tn", name="mem_v_dw")
    dmn = _mm(dkx, P['w_mk'], "nt", name="mem_k_dx", epi=_add_epi, extras=(_mm(dvx, P['w_mv'], "nt", name="mem_v_dx"),))
    _, _, G['mem_norm_g'] = _norm_bwd(mem, P['mem_norm_g'], dmn, None, "norm_mem_bwd")
    dymix = _mm(dx1h, P['wout'], "nt", name="out_proj_dx")
    G['wout'] = _mm(S['ymix'], dx1h, "tn", name="out_proj_dw")
    (do, dys, dh, dug_mix), (G['mla_out_g'], G['lru_out_g']) = _rows_vjp(
        _f_mix, [S['o'], S['ys'], S['h'], S['u_gate']], [P['mla_out_g'], P['lru_out_g']], [dymix], name="mix_bwd",
        drows=[0, 1, 2, 3], dparams=[0, 1], drow_dtypes=[BF16, F32, F32, F32])
    g, da_lru = _lru_scan_bwd(S['a'], dh, S['hprev'], name="lru_scan_bwd")
    lru_par = [P['lru_w_a'], P['lru_b_a'], P['lru_w_i'], P['lru_b_i'], P['lru_lambda']]
    (dc_lru,), dpar = _rows_vjp(_f_lru_gates, [S['c_lru']], lru_par, [da_lru, g], name="lru_gates_bwd", drows=[0],
                                dparams=[0, 1, 2, 3, 4], drow_dtypes=[F32], tm=_MM_ROWS)
    G['lru_w_a'], G['lru_b_a'], G['lru_w_i'], G['lru_b_i'], G['lru_lambda'] = dpar
    dc_ssm, dug_ssd, G['ssm_dt_bias'], G['ssm_a_log'], G['ssm_d'], G['ssm_norm_g'] = _ssd_bwd(
        S['c_ssm'], S['u_gate'], S['sall'], dys, P['ssm_dt_bias'], P['ssm_a_log'], P['ssm_d'], P['ssm_norm_g'],
        name="ssd_bwd")
    du_conv, G['conv_w'], G['conv_b'] = _conv_bwd(S['u_conv'], dc_ssm, dc_lru, P['conv_w'], name="conv_bwd")
    delta = _attn_delta(do, S['o'], name="attn_delta")
    if callable(send):
        send = send(G)
    dqt, dkv, *got = _attn_bwd(S['q'], S['kv'], do, S['lse'], delta,
                               name="attn_bwd" if send is None else "attn_bwd_scatter", send=send)
    dyq = _qrope_bwd(dqt, cq, sq, name="q_rope_bwd")
    dcqn = _mm(dyq, P['wq'], "nt", name="q_proj_dx")
    G['wq'] = _mm(S['cqn'], dyq, "tn", name="q_proj_dw")
    dakv = _mm(dkv, P['wkv'], "nt", name="kv_proj_dx")
    G['wkv'] = _mm(S['akv'], dkv, "tn", name="kv_proj_dw")
    (du_mla,), (G['mla_q_norm_g'], G['mla_kv_norm_g']) = _rows_vjp(
        _f_mla_prep, [S['u_mla'], ck, sk], [P['mla_q_norm_g'], P['mla_kv_norm_g']], [dcqn, dakv], name="mla_prep_bwd",
        drows=[0], dparams=[0, 1], drow_dtypes=[BF16])
    du = jnp.concatenate([du_mla, (dug_mix + dug_ssd).astype(BF16), du_conv.astype(BF16)], axis=1)
    G['win'] = _mm(S['h1'], du, "tn", name="in_dw")
    dx0, dx0h, G['mix_norm_g'] = _mm(du, P['win'], "nt", name="in_dx", extras=(S['x0'], dx1, P['mix_norm_g']), **norm_out)
    return dx0, dx0h, G, (got[0] if got else None)


class _NoExchange:
    def __init__(self, layers):
        self.layers = layers

    def pieces(self, l):
        return self.layers[l]

    def fwd_send(self, l):
        return None

    def fwd_got(self, l, got):
        pass

    def bwd_send(self, l, G):
        return None

    def bwd_got(self, l, got):
        pass

    def grads_ready(self, l, pieces):
        pass


def _local_step(x, mem, positions, ex, Ws, tgt):
    tabs = _rope_tables(positions)
    saved, preps, h = [], [], None
    for l in range(DEPTH):
        P = _prep_layer(ex.pieces(l), Ws, l)
        send = ex.fwd_send(l)
        g_next = Ws['mix_norm_g'][l + 1].astype(F32).reshape(1, D) if l + 1 < DEPTH else None
        x, h, S = _layer_fwd(x, h, mem, P, tabs, g_next, send, functools.partial(ex.fwd_got, l))
        saved.append(S)
        preps.append(P)
    loss, dx, dxh, dg_final = _loss_head(x, tgt, Ws['final_norm_g'].astype(F32).reshape(1, D), name="loss_head")
    pieces, small = [None] * DEPTH, [None] * DEPTH
    for l in reversed(range(DEPTH)):
        dx, dxh, G, got = _layer_bwd(dx, dxh, mem, saved[l], preps[l], tabs, functools.partial(ex.bwd_send, l))
        if got is not None:
            ex.bwd_got(l, got)
        pieces[l], small[l] = _unprep_pieces(G), _unprep_small(G)
        ex.grads_ready(l, pieces[l])
    grads = {n: jnp.stack([small[l][n] for l in range(DEPTH)]) for n in SMALL if n != 'final_norm_g'}
    grads['final_norm_g'] = dg_final.reshape(D)
    return loss, dx, pieces, grads


def _pack_rows(pieces, names=PIECES):
    return jnp.concatenate([pieces[n].reshape(pieces[n].shape[:-2] + (-1, PACK_C)) for n in names], axis=-2)


def _unpack_rows(packed, lead=(), names=PIECES):
    out, off = {}, 0
    for n in names:
        rows = _piece_rows(n)
        out[n] = packed[..., off:off + rows, :].reshape(lead + PIECE_SHAPE[n])
        off += rows
    return out


EARLY = ('win', 'wq', 'wkv')
LATE = tuple(n for n in PIECES if n not in EARLY)


class _StepExchange(_NoExchange):
    FIRST = [(0, EARLY)]
    RIDES = {0: [(0, LATE), (1, PIECES), (2, PIECES)], 1: [(3, PIECES)]}

    def __init__(self, shard):
        self.shard = {n: a.astype(BF16) for n, a in shard.items()}
        self.layers = {l: {} for l in range(DEPTH)}
        self.sums, self.reduced = {}, {}
        first = self._pack(self.FIRST)
        self._take(*_finish_gather(first, _chip_exchange(first, True, name="gather_w0_chips"), name="gather_w0"), self.FIRST)

    def _pack(self, spec):
        return jnp.concatenate([_pack_rows({n: self.shard[n][l] for n in names}, names) for l, names in spec], axis=0)

    def _take(self, mine, other, spec):
        off = 0
        for l, names in spec:
            rows = sum(_piece_rows(n) for n in names)
            a, b = (_unpack_rows(g[:, off:off + rows], (4,), names) for g in (mine, other))
            self.layers[l].update({n: _assemble(n, _by_core(a[n], b[n])) for n in names})
            off += rows

    def fwd_send(self, l):
        return self._pack(self.RIDES[l]) if l in self.RIDES else None

    def fwd_got(self, l, got):
        self._take(*_finish_gather(self._pack(self.RIDES[l]), got, name="gather_w"), self.RIDES[l])
        return {n: self.layers[l][n] for n in LATE} if (l, LATE) in self.RIDES[l] else {}

    @staticmethod
    def _chips_first(pieces, names):
        x = _pack_rows({n: _disassemble(n, pieces[n]) for n in names}, names)
        return x.transpose(1, 0, 2, 3).astype(BF16).reshape(4, -1, PACK_C)

    @staticmethod
    def _cores_last(got):
        both = _sum_fixed(got, F32, name="scatter_g_sum").reshape(2, -1, PACK_C)
        other = _core_exchange(both, True, name="scatter_g_cores")
        return _add_own_half(both, other, name="scatter_g_add", out_dtype=F32)

    def grads_ready(self, l, pieces):
        if l > 0:
            self.sums[l] = self._chips_first(pieces, PIECES)
            return
        x = _pack_rows({n: _disassemble(n, pieces[n]) for n in EARLY}, EARLY)
        parts = _chip_exchange(_chip_sums(x, name="scatter_g0"), False, name="scatter_g0_chips")
        self.reduced[0] = jnp.concatenate([_sum_fixed(parts, F32, name="scatter_g0_sum"), self.late0], axis=0)

    def bwd_send(self, l, G):
        send = self.sums.pop(l + 1, None)
        if l == 0:
            send = jnp.concatenate([send, self._chips_first(_unprep_pieces(G, LATE), LATE)], axis=1)
        return send

    def bwd_got(self, l, got):
        rows = 2 * sum(_piece_rows(n) for n in PIECES)
        self.reduced[l + 1] = self._cores_last(got[:, :rows])
        if l == 0:
            self.late0 = self._cores_last(got[:, rows:])


def _adamw_nd(w, g, m, v, name):
    shp = w.shape
    two = lambda a: a.reshape(-1, shp[-1])
    return [r.reshape(shp) for r in _adamw(two(w), two(g), two(m), two(v), name=name)]


def kernel(x, mem, positions, mix_norm_g, w_in, mla_q_norm_g, mla_kv_norm_g, mla_w_uq, mla_w_ukv, mla_out_g, ssm_conv_w, ssm_conv_b, ssm_dt_bias, ssm_a_log, ssm_d, ssm_norm_g, lru_conv_w, lru_conv_b, lru_w_a, lru_b_a, lru_w_i, lru_b_i, lru_lambda, lru_out_g, w_out, xattn_norm_g, mem_norm_g, w_mq, w_mk, w_mv, w_mo, mlp_norm_g, w_mlp1, w_mlp2, final_norm_g, loss_target, m_mix_norm_g, m_w_in, m_mla_q_norm_g, m_mla_kv_norm_g, m_mla_w_uq, m_mla_w_ukv, m_mla_out_g, m_ssm_conv_w, m_ssm_conv_b, m_ssm_dt_bias, m_ssm_a_log, m_ssm_d, m_ssm_norm_g, m_lru_conv_w, m_lru_conv_b, m_lru_w_a, m_lru_b_a, m_lru_w_i, m_lru_b_i, m_lru_lambda, m_lru_out_g, m_w_out, m_xattn_norm_g, m_mem_norm_g, m_w_mq, m_w_mk, m_w_mv, m_w_mo, m_mlp_norm_g, m_w_mlp1, m_w_mlp2, m_final_norm_g, v_mix_norm_g, v_w_in, v_mla_q_norm_g, v_mla_kv_norm_g, v_mla_w_uq, v_mla_w_ukv, v_mla_out_g, v_ssm_conv_w, v_ssm_conv_b, v_ssm_dt_bias, v_ssm_a_log, v_ssm_d, v_ssm_norm_g, v_lru_conv_w, v_lru_conv_b, v_lru_w_a, v_lru_b_a, v_lru_w_i, v_lru_b_i, v_lru_lambda, v_lru_out_g, v_w_out, v_xattn_norm_g, v_mem_norm_g, v_w_mq, v_w_mk, v_w_mv, v_w_mo, v_mlp_norm_g, v_w_mlp1, v_w_mlp2, v_final_norm_g):
    a = locals()
    w = {n: a[n] for n in WEIGHTS}
    m = {n: a['m_' + n] for n in WEIGHTS}
    v = {n: a['v_' + n] for n in WEIGHTS}
    me = 4 * lax.axis_index("x") + 2 * lax.axis_index("y") + lax.axis_index("c")

    ex = _StepExchange({n: _K_FWD.get(n, lambda a: a)(w[PIECE_SOURCE.get(n, n)]) for n in PIECES})
    Ws = {}
    conv_shapes = [w[n].shape for n in CONV_SHARDED]
    conv_g = _all_gather(_pack([w[n] for n in CONV_SHARDED], F32), name="gather_conv")
    conv_g = conv_g.transpose(1, 0, 2, 3).reshape((N_DEV,) + conv_g.shape[2:])
    for n, g in zip(CONV_SHARDED, _unpack(conv_g, conv_shapes, lead=(N_DEV,))):
        Ws[n] = g.transpose(1, 2, 0, 3).reshape(g.shape[1], g.shape[2], N_DEV * g.shape[3])
    for n in SMALL:
        if n not in CONV_SHARDED:
            Ws[n] = w[n]

    loss_share, dx, _, grads = _local_step(x[0], mem[0], positions[0], ex, Ws, loss_target[0])
    loss = lax.psum(loss_share[0, 0], ("x", "y", "c"))

    g_out = {}
    for n in PIECES:
        g = jnp.stack([_unpack_rows(ex.reduced[l])[n] for l in range(DEPTH)])
        g_out[PIECE_SOURCE.get(n, n)] = _K_INV[n](g) if n in _K_INV else g
    small_shapes = [grads[n].shape for n in SMALL]
    g_small = _all_reduce(_pack([grads[n] for n in SMALL], F32), name="reduce_g")
    for n, g in zip(SMALL, _unpack(g_small, small_shapes)):
        if n in CONV_SHARDED:
            cols = w[n].shape[-1]
            g = lax.dynamic_slice_in_dim(g, me * cols, cols, axis=2)
        g_out[n] = g

    delta, new_m, new_v = {}, {}, {}
    for n in BIG:
        delta[n], new_m[n], new_v[n] = _adamw_nd(w[n], g_out[n], m[n], v[n], "adamw_" + n)
    shapes = [w[n].shape for n in SMALL]
    packed = [_pack([d[n] for n in SMALL], F32) for d in (w, g_out, m, v)]
    for d, res in zip((delta, new_m, new_v), _adamw(*packed, name="adamw_small")):
        d.update(zip(SMALL, _unpack(res, shapes)))

    return (loss, dx[None], *[g_out[n] for n in WEIGHTS], *[delta[n] for n in WEIGHTS],
            *[new_m[n] for n in WEIGHTS], *[new_v[n] for n in WEIGHTS])
```

```python
import functools
import math

import jax
import jax.numpy as jnp
from jax import lax
from jax.experimental import pallas as pl
from jax.experimental.pallas import tpu as pltpu

F32, BF16 = jnp.float32, jnp.bfloat16

D = 1024
DEPTH = 4
N_MEM = 256
EPS = 1e-6
HEADS = 8
NOPE, ROPE, VDIM = 64, 32, 64
Q_RANK, KV_RANK = 256, 128
ROPE_THETA = 10000.0
SSM_CHUNK = 128
LRU_C = 8.0
MEM_HEADS = 4
D_FF = 4 * D
SLAB = 128
LR, B1, B2, AEPS, WD, STEP = 0.001, 0.9, 0.999, 1e-08, 0.01, 10

N_DEV = 8
MESH = pl.DeviceIdType.MESH

U_MLA = 640
U_GATE = 640
U_CONV = 768

_DN = {"nn": (((1,), (0,)), ((), ())), "nt": (((1,), (1,)), ((), ())), "tn": (((0,), (0,)), ((), ()))}


def _dot(a, b, kind):
    return lax.dot_general(a.astype(BF16), b.astype(BF16), _DN[kind], preferred_element_type=F32)


@functools.partial(jax.custom_vjp, nondiff_argnums=(2,))
def _bdot(a, b, kind):
    return _dot(a, b, kind)


def _bdot_fwd(a, b, kind):
    return _dot(a, b, kind), (a, b)


def _bdot_bwd(kind, res, g):
    a, b = res
    if kind == "nn":
        da, db = _dot(g, b, "nt"), _dot(a, g, "tn")
    elif kind == "nt":
        da, db = _dot(g, b, "nn"), _dot(g, a, "tn")
    else:
        da, db = _dot(b, g, "nt"), _dot(a, g, "nn")
    return da.astype(a.dtype), db.astype(b.dtype)


_bdot.defvjp(_bdot_fwd, _bdot_bwd)


def _tile(n, pref):
    if n <= pref:
        return n
    t = pref
    while n % t:
        t -= SLAB
    return t


def _cparams(sem, vmem_mb=48):
    return pltpu.CompilerParams(dimension_semantics=sem, vmem_limit_bytes=vmem_mb * 1024 * 1024)


def _mm(a, b, kind="nn", *, name, out_dtypes=(F32,), epi=None, extras=(), col_sums=0, out_cols=None, tm=1024, tn=1024,
        tk=1024):
    if kind == "tn":
        K, M = a.shape
    else:
        M, K = a.shape
    N = b.shape[0] if kind == "nt" else b.shape[1]
    if a.dtype == F32 or b.dtype == F32:
        tk = tk // 2
    tm, tn, tk = _tile(M, tm), _tile(N, tn), _tile(K, tk)
    assert not col_sums or tn == N, "column sums need one tile across the columns"
    assert out_cols is None or tn == N, "outputs of their own widths need one tile across the columns"
    nk = K // tk
    a_spec = pl.BlockSpec((tk, tm), lambda i, j, k: (k, i)) if kind == "tn" else pl.BlockSpec((tm, tk), lambda i, j, k: (i, k))
    b_spec = pl.BlockSpec((tn, tk), lambda i, j, k: (j, k)) if kind == "nt" else pl.BlockSpec((tk, tn), lambda i, j, k: (k, j))
    o_spec = pl.BlockSpec((tm, tn), lambda i, j, k: (i, j))
    vec_spec = pl.BlockSpec((1, tn), lambda i, j, k: (0, j))
    ex_specs = [vec_spec if e.shape[0] == 1 else o_spec for e in extras]
    n_ex, n_out = len(extras), len(out_dtypes)

    def body(*refs):
        a_ref, b_ref = refs[:2]
        ex = refs[2:2 + n_ex]
        outs = refs[2 + n_ex:2 + n_ex + n_out]
        acc = refs[-1]
        k = pl.program_id(2)
        first_row_tile = pl.program_id(0) == 0

        def finish(r):
            res = epi(r, *[e[...] for e in ex]) if epi is not None else (r,)
            for o, v in zip(outs, res):
                o[...] = v.astype(o.dtype)
            sums = refs[2 + n_ex + n_out:2 + n_ex + n_out + col_sums]

            @pl.when(first_row_tile)
            def _():
                for o in sums:
                    o[...] = jnp.zeros_like(o)

            for o, v in zip(sums, res[n_out:]):
                o[...] += v

        if nk == 1:
            finish(_dot(a_ref[...], b_ref[...], kind))
            return

        @pl.when(k == 0)
        def _():
            acc[...] = _dot(a_ref[...], b_ref[...], kind)

        @pl.when(k > 0)
        def _():
            acc[...] += _dot(a_ref[...], b_ref[...], kind)

        @pl.when(k == nk - 1)
        def _():
            finish(acc[...])

    res = pl.pallas_call(
        body,
        name=name,
        grid=(M // tm, N // tn, nk),
        in_specs=[a_spec, b_spec] + ex_specs,
        out_specs=([o_spec] * n_out if out_cols is None else [pl.BlockSpec((tm, w), lambda i, j, k: (i, 0)) for w in out_cols])
        + [vec_spec] * col_sums,
        out_shape=[jax.ShapeDtypeStruct((M, N if out_cols is None else out_cols[n]), dt) for n, dt in enumerate(out_dtypes)]
        + [jax.ShapeDtypeStruct((1, N), F32)] * col_sums,
        scratch_shapes=[pltpu.VMEM((tm, tn), F32)],
        compiler_params=_cparams(("arbitrary" if col_sums else "parallel", "parallel", "arbitrary")),
    )(a, b, *extras)
    return res[0] if n_out + col_sums == 1 else res


_MM_ROWS = 1024


def _row_spec(arr, tm):
    return pl.BlockSpec((tm, arr.shape[1]), lambda i: (i, 0))


def _full_spec(arr):
    nd = arr.ndim
    return pl.BlockSpec(arr.shape, lambda i: (0,) * nd)


def _rows_fwd(fn, rows, params, outs, *, name, tm=256):
    T = rows[0].shape[0]
    tm = min(tm, T)
    nr, npar = len(rows), len(params)

    def body(*refs):
        ins = [r[...] for r in refs[:nr + npar]]
        res = fn(*ins)
        for o, v in zip(refs[nr + npar:], res):
            o[...] = v.astype(o.dtype)

    res = pl.pallas_call(
        body,
        name=name,
        grid=(T // tm,),
        in_specs=[_row_spec(r, tm) for r in rows] + [_full_spec(p) for p in params],
        out_specs=[pl.BlockSpec((tm, c), lambda i: (i, 0)) for c, _ in outs],
        out_shape=[jax.ShapeDtypeStruct((T, c), dt) for c, dt in outs],
        compiler_params=_cparams(("parallel",)),
    )(*rows, *params)
    return res


def _rows_vjp(fn, rows, params, cts, *, name, drows, dparams, drow_dtypes, add=None, twin=False, tm=256):
    T = rows[0].shape[0]
    tm = min(tm, T)
    nr, npar, nct = len(rows), len(params), len(cts)
    n_add = 0 if add is None else 1
    n_dr, n_dp = len(drows), len(dparams)
    n_tw = 1 if twin else 0

    def body(*refs):
        row_t = [r[...] for r in refs[:nr]]
        par_t = [r[...] for r in refs[nr:nr + npar]]
        ct_t = [r[...].astype(F32) for r in refs[nr + npar:nr + npar + nct]]
        pos = nr + npar + nct
        add_t = refs[pos][...] if n_add else None
        pos += n_add
        drow_refs = refs[pos:pos + n_dr]
        dpar_refs = refs[pos + n_dr:pos + n_dr + n_dp]

        def g(*dargs):
            rr, pp = list(row_t), list(par_t)
            for idx, v in zip(drows, dargs[:n_dr]):
                rr[idx] = v
            for idx, v in zip(dparams, dargs[n_dr:]):
                pp[idx] = v
            return tuple(fn(*rr, *pp))

        prim = [row_t[i].astype(F32) for i in drows] + [par_t[i].astype(F32) for i in dparams]
        _, vjp = jax.vjp(g, *prim)
        grads = vjp(tuple(ct_t))
        for n, (o, v) in enumerate(zip(drow_refs, grads[:n_dr])):
            if n == 0 and n_add:
                v = v + add_t.astype(F32)
            o[...] = v.astype(o.dtype)
            if n == 0 and n_tw:
                refs[-1][...] = v.astype(BF16)

        @pl.when(pl.program_id(0) == 0)
        def _():
            for o in dpar_refs:
                o[...] = jnp.zeros_like(o)

        for o, v in zip(dpar_refs, grads[n_dr:]):
            o[...] += v

    res = pl.pallas_call(
        body,
        name=name,
        grid=(T // tm,),
        in_specs=[_row_spec(r, tm) for r in rows] + [_full_spec(p) for p in params] + [_row_spec(c, tm) for c in cts]
        + ([_row_spec(add, tm)] if n_add else []),
        out_specs=[_row_spec(rows[i], tm) for i in drows] + [_full_spec(params[i]) for i in dparams]
        + [_row_spec(rows[drows[0]], tm)] * n_tw,
        out_shape=[jax.ShapeDtypeStruct(rows[i].shape, dt) for i, dt in zip(drows, drow_dtypes)]
        + [jax.ShapeDtypeStruct(params[i].shape, F32) for i in dparams]
        + [jax.ShapeDtypeStruct(rows[drows[0]].shape, BF16)] * n_tw,
        compiler_params=_cparams(("arbitrary",)),
    )(*rows, *params, *cts, *([add] if n_add else []))
    return list(res[:n_dr]) + list(res[n_dr + n_dp:]), list(res[n_dr:n_dr + n_dp])


def _rms(x, g, n):
    return x * lax.rsqrt(jnp.sum(x * x, axis=-1, keepdims=True) * (1.0 / n) + EPS) * g


def _sigmoid(x):
    return 1.0 / (1.0 + jnp.exp(-x))


def _silu(x):
    return x * _sigmoid(x)


def _softplus(x):
    return jnp.maximum(x, 0.0) + jnp.log(1.0 + jnp.exp(-jnp.abs(x)))


def _gelu_tanh(x):
    return 0.5 * x * (1.0 + jnp.tanh(math.sqrt(2.0 / math.pi) * (x + 0.044715 * x * x * x)))


def _lane(shape):
    return lax.broadcasted_iota(jnp.int32, shape, len(shape) - 1)


def _col(x, h):
    return jnp.sum(jnp.where(_lane(x.shape) == h, x, 0.0), axis=-1, keepdims=True)


def _f_norm(x, g):
    return (_rms(x.astype(F32), g, x.shape[-1]),)


def _f_mla_prep(u, ck, sk, gq, gkv):
    u = u.astype(F32)
    cq = _rms(u[:, 0:256], gq, Q_RANK)
    ckv = _rms(u[:, 256:384], gkv, KV_RANK)
    kr = u[:, 384:512] * ck + u[:, 512:640] * sk
    return cq, jnp.concatenate([ckv, kr], axis=1)


def _f_qrope(y, cq, sq):
    y = y.astype(F32)
    c8, s8 = jnp.tile(cq, (1, HEADS)), jnp.tile(sq, (1, HEADS))
    return (y[:, :HEADS * SLAB] * c8 + y[:, HEADS * SLAB:] * s8,)


def _f_lru_gates(xc, wa, ba, wi, bi, lam):
    xc = xc.astype(F32)
    r = _sigmoid(_bdot(xc, wa, "nn") + ba)
    i = _sigmoid(_bdot(xc, wi, "nn") + bi)
    log_a = -LRU_C * r * _softplus(-lam)
    a = jnp.exp(log_a)
    x2 = 2.0 * log_a
    m1 = jnp.where(x2 > -0.02, -x2 * (1.0 + x2 * (0.5 + x2 * (1.0 / 6.0 + x2 * (1.0 / 24.0)))), 1.0 - jnp.exp(x2))
    return a, jnp.sqrt(m1) * (i * xc)


def _f_mix(o, ys, h, ug, g_mla, g_lru):
    o = o.astype(F32)
    y_mla = _rms(o, g_mla, HEADS * VDIM)
    y_lru = _rms(h.astype(F32) * _gelu_tanh(ug[:, 256:512].astype(F32)), g_lru, 256)
    return (jnp.concatenate([y_mla, ys.astype(F32), y_lru], axis=1),)


def _f_xattn(q, k, v):
    hd = D // MEM_HEADS
    outs = []
    for h in range(MEM_HEADS):
        sl = slice(h * hd, (h + 1) * hd)
        s = _bdot(q[:, sl], k[:, sl], "nt") * (1.0 / math.sqrt(hd))
        s = s - jnp.max(s, axis=-1, keepdims=True)
        p = jnp.exp(s)
        p = p / jnp.sum(p, axis=-1, keepdims=True)
        outs.append(_bdot(p, v[:, sl], "nn"))
    return (jnp.concatenate(outs, axis=1),)


def _split_dot(tri, a, kind):
    a_hi = a.astype(BF16)
    r1 = a - a_hi.astype(F32)
    a_mid = r1.astype(BF16)
    a_lo = (r1 - a_mid.astype(F32)).astype(BF16)
    return _dot(tri, a_hi, kind) + _dot(tri, a_mid, kind) + _dot(tri, a_lo, kind)


@jax.custom_vjp
def _tri_cumsum(tri, a):
    return _split_dot(tri, a, "nn")


def _tri_cumsum_fwd(tri, a):
    return _split_dot(tri, a, "nn"), tri


def _tri_cumsum_bwd(tri, g):
    return jnp.zeros_like(tri), _split_dot(tri, g, "tn")


_tri_cumsum.defvjp(_tri_cumsum_fwd, _tri_cumsum_bwd)


def _f_ssd_chunk(c, ug, s0, s1, dtb, alog, dsk, ng):
    L = c.shape[0]
    c = c.astype(F32)
    xbc = _silu(c)
    xs, bm, cm = xbc[:, 0:256], xbc[:, 256:384], xbc[:, 384:512]
    z = ug[:, 0:256].astype(F32)
    dt = _softplus(ug[:, 512:640].astype(F32) + dtb)
    a = dt * (-jnp.exp(alog))
    rowi = lax.broadcasted_iota(jnp.int32, (L, L), 0)
    coli = lax.broadcasted_iota(jnp.int32, (L, L), 1)
    tril = rowi >= coli
    acum = _tri_cumsum(tril.astype(BF16), a)
    acum_t = acum.T
    lane = _lane((1, SLAB))
    lo = lane < 64
    ys, new_s = [], []
    for g in range(2):
        gm = (lane >= 64 * g) & (lane < 64 * g + 64)
        bg, cg = jnp.where(gm, bm, 0.0), jnp.where(gm, cm, 0.0)
        cb = _bdot(cg, bg, "nt")
        x = xs[:, SLAB * g:SLAB * (g + 1)]
        h0, h1 = 2 * g, 2 * g + 1
        ac0, ac1 = _col(acum, h0), _col(acum, h1)
        xdt = x * jnp.where(lo, _col(dt, h0), _col(dt, h1))
        ac_l = jnp.where(lo, ac0, ac1)
        tot = acum[L - 1:L, :]
        tot_l = jnp.where(lo, _col(tot, h0), _col(tot, h1))
        yd = jnp.zeros((L, SLAB), F32)
        for hh, acc, hm in ((h0, ac0, lo), (h1, ac1, jnp.logical_not(lo))):
            seg = acc - acum_t[hh:hh + 1, :]
            lm = jnp.where(tril, jnp.exp(jnp.where(tril, seg, 0.0)), 0.0)
            yd = yd + _bdot(cb * lm, jnp.where(hm, xdt, 0.0), "nn")
        sg = (s0, s1)[g]
        y_off = _bdot(cg, sg, "nn") * jnp.exp(ac_l)
        st = _bdot(bg, xdt * jnp.exp(tot_l - ac_l), "tn")
        new_s.append(jnp.exp(tot_l) * sg + st)
        y = yd + y_off + jnp.where(lo, _col(dsk, h0), _col(dsk, h1)) * x
        y = y * _silu(z[:, SLAB * g:SLAB * (g + 1)])
        ys.append(_rms(y, ng[:, SLAB * g:SLAB * (g + 1)], SLAB))
    return jnp.concatenate(ys, axis=1), new_s[0], new_s[1]


_SSD_TILE = 512


def _ssd_fwd(c, ug, dtb, alog, dsk, ng, *, name):
    T = c.shape[0]
    tm = min(_SSD_TILE, T)
    ncs = tm // SSM_CHUNK
    nc = T // SSM_CHUNK

    def body(c_ref, ug_ref, dtb_ref, alog_ref, dsk_ref, ng_ref, y_ref, sall_ref, s_scr):
        @pl.when(pl.program_id(0) == 0)
        def _():
            s_scr[...] = jnp.zeros_like(s_scr)

        s0, s1 = s_scr[0], s_scr[1]
        for k in range(ncs):
            rows = slice(k * SSM_CHUNK, (k + 1) * SSM_CHUNK)
            sall_ref[k, 0] = s0
            sall_ref[k, 1] = s1
            y, s0, s1 = _f_ssd_chunk(c_ref[rows, :], ug_ref[rows, :], s0, s1, dtb_ref[...], alog_ref[...],
                                     dsk_ref[...], ng_ref[...])
            y_ref[rows, :] = y
        s_scr[0] = s0
        s_scr[1] = s1

    y, sall = pl.pallas_call(
        body,
        name=name,
        grid=(T // tm,),
        in_specs=[_row_spec(c, tm), _row_spec(ug, tm)] + [_full_spec(p) for p in (dtb, alog, dsk, ng)],
        out_specs=[pl.BlockSpec((tm, 256), lambda i: (i, 0)), pl.BlockSpec((ncs, 2, SLAB, SLAB), lambda i: (i, 0, 0, 0))],
        out_shape=[jax.ShapeDtypeStruct((T, 256), F32), jax.ShapeDtypeStruct((nc, 2, SLAB, SLAB), F32)],
        scratch_shapes=[pltpu.VMEM((2, SLAB, SLAB), F32)],
        compiler_params=_cparams(("arbitrary",)),
    )(c, ug, dtb, alog, dsk, ng)
    return y, sall


def _ssd_bwd(c, ug, sall, dy, dtb, alog, dsk, ng, *, name):
    T = c.shape[0]
    tm = min(_SSD_TILE, T)
    ncs = tm // SSM_CHUNK
    nt = T // tm

    def body(c_ref, ug_ref, sall_ref, dy_ref, dtb_ref, alog_ref, dsk_ref, ng_ref,
             dc_ref, dug_ref, ddtb_ref, dalog_ref, ddsk_ref, dng_ref, ds_scr):
        @pl.when(pl.program_id(0) == 0)
        def _():
            ds_scr[...] = jnp.zeros_like(ds_scr)
            for o in (ddtb_ref, dalog_ref, ddsk_ref, dng_ref):
                o[...] = jnp.zeros_like(o)

        ds0, ds1 = ds_scr[0], ds_scr[1]
        for k in reversed(range(ncs)):
            rows = slice(k * SSM_CHUNK, (k + 1) * SSM_CHUNK)
            prim = (c_ref[rows, :].astype(F32), ug_ref[rows, :].astype(F32), sall_ref[k, 0], sall_ref[k, 1],
                    dtb_ref[...], alog_ref[...], dsk_ref[...], ng_ref[...])
            _, vjp = jax.vjp(_f_ssd_chunk, *prim)
            dc, dug, ds0, ds1, g_dtb, g_alog, g_dsk, g_ng = vjp((dy_ref[rows, :].astype(F32), ds0, ds1))
            dc_ref[rows, :] = dc
            dug_ref[rows, :] = dug
            ddtb_ref[...] += g_dtb
            dalog_ref[...] += g_alog
            ddsk_ref[...] += g_dsk
            dng_ref[...] += g_ng
        ds_scr[0] = ds0
        ds_scr[1] = ds1

    rev = lambda i: (nt - 1 - i, 0)
    params = (dtb, alog, dsk, ng)
    res = pl.pallas_call(
        body,
        name=name,
        grid=(nt,),
        in_specs=[pl.BlockSpec((tm, c.shape[1]), rev), pl.BlockSpec((tm, ug.shape[1]), rev),
                  pl.BlockSpec((ncs, 2, SLAB, SLAB), lambda i: (nt - 1 - i, 0, 0, 0)), pl.BlockSpec((tm, 256), rev)]
        + [_full_spec(p) for p in params],
        out_specs=[pl.BlockSpec((tm, 512), rev), pl.BlockSpec((tm, U_GATE), rev)] + [_full_spec(p) for p in params],
        out_shape=[jax.ShapeDtypeStruct((T, 512), F32), jax.ShapeDtypeStruct((T, U_GATE), F32)]
        + [jax.ShapeDtypeStruct(p.shape, F32) for p in params],
        scratch_shapes=[pltpu.VMEM((2, SLAB, SLAB), F32)],
        compiler_params=_cparams(("arbitrary",)),
    )(c, ug, sall, dy, *params)
    return res


_CONV_TILE = 512
_HALO = 8
_CONV_W = 4


def _conv_fwd(u, w, b, *, name):
    T, C = u.shape
    tm = min(_CONV_TILE, T)
    hb = tm // _HALO

    def body(u_ref, prev_ref, w_ref, b_ref, y1_ref, y2_ref, ext):
        i = pl.program_id(0)
        ext[0:_HALO, :] = jnp.where(i > 0, prev_ref[...], 0.0)
        ext[_HALO:, :] = u_ref[...]
        y = jnp.broadcast_to(b_ref[...], (tm, C))
        for k in range(_CONV_W):
            y = y + ext[_HALO - (_CONV_W - 1) + k:_HALO - (_CONV_W - 1) + k + tm, :] * w_ref[k:k + 1, :]
        y1_ref[...] = y[:, 0:512]
        y2_ref[...] = y[:, 512:768]

    return pl.pallas_call(
        body,
        name=name,
        grid=(T // tm,),
        in_specs=[_row_spec(u, tm), pl.BlockSpec((_HALO, C), lambda i: (jnp.maximum(i * hb - 1, 0), 0)),
                  _full_spec(w), _full_spec(b)],
        out_specs=[pl.BlockSpec((tm, 512), lambda i: (i, 0)), pl.BlockSpec((tm, 256), lambda i: (i, 0))],
        out_shape=[jax.ShapeDtypeStruct((T, 512), F32), jax.ShapeDtypeStruct((T, 256), F32)],
        scratch_shapes=[pltpu.VMEM((tm + _HALO, C), F32)],
        compiler_params=_cparams(("parallel",)),
    )(u, u, w, b)


def _conv_bwd(u, dy1, dy2, w, *, name):
    T, C = u.shape
    tm = min(_CONV_TILE, T)
    hb = tm // _HALO
    nt = T // tm

    def body(u_ref, prev_ref, dy1_ref, next1_ref, dy2_ref, next2_ref, w_ref, du_ref, dw_ref, db_ref, ext, dext):
        i = pl.program_id(0)
        ext[0:_HALO, :] = jnp.where(i > 0, prev_ref[...], 0.0)
        ext[_HALO:, :] = u_ref[...]
        dext[0:tm, 0:512] = dy1_ref[...]
        dext[0:tm, 512:768] = dy2_ref[...]
        dext[tm:, 0:512] = jnp.where(i < nt - 1, next1_ref[...], 0.0)
        dext[tm:, 512:768] = jnp.where(i < nt - 1, next2_ref[...], 0.0)

        @pl.when(i == 0)
        def _():
            dw_ref[...] = jnp.zeros_like(dw_ref)
            db_ref[...] = jnp.zeros_like(db_ref)

        dy = dext[0:tm, :]
        du = jnp.zeros((tm, C), F32)
        for k in range(_CONV_W):
            du = du + dext[_CONV_W - 1 - k:_CONV_W - 1 - k + tm, :] * w_ref[k:k + 1, :]
            xk = ext[_HALO - (_CONV_W - 1) + k:_HALO - (_CONV_W - 1) + k + tm, :]
            dw_ref[k:k + 1, :] += jnp.sum(dy * xk, axis=0, keepdims=True)
        du_ref[...] = du
        db_ref[...] += jnp.sum(dy, axis=0, keepdims=True)

    nxt = lambda i: (jnp.minimum((i + 1) * hb, T // _HALO - 1), 0)
    return pl.pallas_call(
        body,
        name=name,
        grid=(nt,),
        in_specs=[_row_spec(u, tm), pl.BlockSpec((_HALO, C), lambda i: (jnp.maximum(i * hb - 1, 0), 0)),
                  _row_spec(dy1, tm), pl.BlockSpec((_HALO, 512), nxt), _row_spec(dy2, tm), pl.BlockSpec((_HALO, 256), nxt),
                  _full_spec(w)],
        out_specs=[pl.BlockSpec((tm, C), lambda i: (i, 0)), _full_spec(w), pl.BlockSpec((1, C), lambda i: (0, 0))],
        out_shape=[jax.ShapeDtypeStruct((T, C), F32), jax.ShapeDtypeStruct(w.shape, F32), jax.ShapeDtypeStruct((1, C), F32)],
        scratch_shapes=[pltpu.VMEM((tm + _HALO, C), F32), pltpu.VMEM((tm + _HALO, C), F32)],
        compiler_params=_cparams(("arbitrary",)),
    )(u, u, dy1, dy1, dy2, dy2, w)


_SCAN_TILE = 1024
_SUB = 8


def _shift_rows(x, d, fill, up):
    r = lax.broadcasted_iota(jnp.int32, x.shape, 0)
    if up:
        return jnp.where(r < _SUB - d, pltpu.roll(x, _SUB - d, 0), fill)
    return jnp.where(r >= d, pltpu.roll(x, d, 0), fill)


def _lru_scan_fwd(a, b, *, name):
    T, W = a.shape
    tr = min(_SCAN_TILE, T)

    def body(a_ref, b_ref, h_ref, hp_ref, carry):
        @pl.when(pl.program_id(0) == 0)
        def _():
            carry[...] = jnp.zeros_like(carry)

        def step(t, cr):
            rows = pl.ds(pl.multiple_of(t * _SUB, _SUB), _SUB)
            aa, bb = a_ref[rows, :], b_ref[rows, :]
            for d in (1, 2, 4):
                bb = bb + aa * _shift_rows(bb, d, 0.0, False)
                aa = aa * _shift_rows(aa, d, 1.0, False)
            h = bb + aa * cr
            h_ref[rows, :] = h
            r = lax.broadcasted_iota(jnp.int32, h.shape, 0)
            hp_ref[rows, :] = jnp.where(r >= 1, pltpu.roll(h, 1, 0), cr)
            return jnp.broadcast_to(h[_SUB - 1:_SUB, :], (_SUB, W))

        carry[...] = lax.fori_loop(0, tr // _SUB, step, carry[...])

    return pl.pallas_call(
        body,
        name=name,
        grid=(T // tr,),
        in_specs=[_row_spec(a, tr), _row_spec(b, tr)],
        out_specs=[pl.BlockSpec((tr, W), lambda i: (i, 0))] * 2,
        out_shape=[jax.ShapeDtypeStruct((T, W), F32)] * 2,
        scratch_shapes=[pltpu.VMEM((_SUB, W), F32)],
        compiler_params=_cparams(("arbitrary",)),
    )(a, b)


def _lru_scan_bwd(a, dh, hprev, *, name):
    T, W = a.shape
    tr = min(_SCAN_TILE, T)
    nt = T // tr

    def body(a_ref, dh_ref, hp_ref, g_ref, da_ref, carry):
        @pl.when(pl.program_id(0) == 0)
        def _():
            carry[...] = jnp.zeros_like(carry)

        nsub = tr // _SUB

        def step(s, cr):
            t = nsub - 1 - s
            rows = pl.ds(pl.multiple_of(t * _SUB, _SUB), _SUB)
            a_t = a_ref[rows, :]
            aa = _shift_rows(a_t, 1, 1.0, True)
            bb = dh_ref[rows, :]
            for d in (1, 2, 4):
                bb = bb + aa * _shift_rows(bb, d, 0.0, True)
                aa = aa * _shift_rows(aa, d, 1.0, True)
            g = bb + aa * cr
            g_ref[rows, :] = g
            da_ref[rows, :] = g * hp_ref[rows, :]
            return jnp.broadcast_to(a_t[0:1, :] * g[0:1, :], (_SUB, W))

        carry[...] = lax.fori_loop(0, nsub, step, carry[...])

    rev = lambda i: (nt - 1 - i, 0)
    return pl.pallas_call(
        body,
        name=name,
        grid=(nt,),
        in_specs=[pl.BlockSpec((tr, W), rev)] * 3,
        out_specs=[pl.BlockSpec((tr, W), rev)] * 2,
        out_shape=[jax.ShapeDtypeStruct((T, W), F32)] * 2,
        scratch_shapes=[pltpu.VMEM((_SUB, W), F32)],
        compiler_params=_cparams(("arbitrary",)),
    )(a, dh, hprev)


_ATT_BLK = 512
_ATT_QPARTS = 2
_ATT_BWD_TRIP = 4
_ATT_SCALE = 1.0 / math.sqrt(NOPE + ROPE)
_ATT_SCALE2 = _ATT_SCALE * math.log2(math.e)


def _call_with_exchange(body, send, gather, *, name, grid, in_specs, out_specs, out_shape, args):
    if send is None:
        return pl.pallas_call(body, name=name, grid=grid, in_specs=in_specs, out_specs=out_specs, out_shape=out_shape,
                              compiler_params=_cparams(("parallel", "arbitrary")))(*args)
    n_in, n_out = len(in_specs), len(out_specs)

    def riding(*refs):
        comm = (refs[n_in], refs[n_in + 1 + n_out]) + tuple(refs[n_in + 2 + n_out:])
        h, i = pl.program_id(0), pl.program_id(1)

        @pl.when((h == 0) & (i == 0))
        def _():
            _chip_start(*comm, gather=gather)

        body(*refs[:n_in], *refs[n_in + 1:n_in + 1 + n_out])

        @pl.when((h == grid[0] - 1) & (i == grid[1] - 1))
        def _():
            _chip_wait(*comm, gather=gather)

    shape = (4,) + send.shape if gather else send.shape
    return pl.pallas_call(
        riding, name=name, grid=grid, in_specs=in_specs + [_ANY], out_specs=out_specs + [_ANY],
        out_shape=out_shape + [jax.ShapeDtypeStruct(shape, send.dtype)], scratch_shapes=_CHIP_SEMS,
        compiler_params=_cparams(("arbitrary", "arbitrary")))(*args, send)


def _attn_fwd(q, kv, *, name, send=None):
    T = q.shape[0]
    blk = min(_ATT_BLK, T)
    nq = T // blk
    parts = _ATT_QPARTS if nq % _ATT_QPARTS == 0 else 1

    def body(q_ref, kv_ref, o_ref, lse_ref):
        i = pl.program_id(1)
        def one(part, j0, nblk, carry, masked):
            m, l, acc = carry
            rows = pl.ds(pl.multiple_of(j0 * blk, blk), nblk * blk)
            s = _dot(q_ref[part * blk:(part + 1) * blk, :], kv_ref[rows, 0:SLAB], "nt")
            if masked:
                col = lax.broadcasted_iota(jnp.int32, s.shape, 1)
                row = lax.broadcasted_iota(jnp.int32, s.shape, 0)
                s = jnp.where(col <= row + (nblk - 1) * blk, s, -jnp.inf)
            m_new = jnp.maximum(m, jnp.max(s, axis=-1, keepdims=True))
            alpha = jnp.exp2(m - m_new)
            p = jnp.exp2(s - m_new)
            l = alpha * l + jnp.sum(p, axis=-1, keepdims=True)
            acc = alpha * acc + _dot(p, kv_ref[rows, SLAB:2 * SLAB], "nn")
            return m_new, l, acc

        def step(t, carry):
            return tuple(one(part, t * parts, parts, carry[part], False) for part in range(parts))

        init = ((jnp.full((blk, 1), -jnp.inf, F32), jnp.zeros((blk, 1), F32), jnp.zeros((blk, SLAB), F32)),) * parts
        carry = lax.fori_loop(0, i // 2, lambda t, c: step(2 * t + 1, step(2 * t, c)), init)
        carry = lax.fori_loop((i // 2) * 2, i, step, carry)
        for part in range(parts):
            m, l, acc = one(part, i * parts, part + 1, carry[part], True)
            o_ref[part * blk:(part + 1) * blk, :] = acc / l
            lse_ref[part] = jnp.broadcast_to(m + jnp.log(l) * math.log2(math.e), (blk, SLAB)).T[0:_SUB, :]

    return _call_with_exchange(
        body, send, True,
        name=name,
        grid=(HEADS, nq // parts),
        in_specs=[pl.BlockSpec((parts * blk, SLAB), lambda h, i: (i, h)), pl.BlockSpec((T, 2 * SLAB), lambda h, i: (0, h))],
        out_specs=[pl.BlockSpec((parts * blk, SLAB), lambda h, i: (i, h)),
                   pl.BlockSpec((None, parts, _SUB, blk), lambda h, i: (h, i, 0, 0))],
        out_shape=[jax.ShapeDtypeStruct((T, HEADS * SLAB), F32), jax.ShapeDtypeStruct((HEADS, nq, _SUB, blk), F32)],
        args=(q, kv))


def _attn_delta(do, o, *, name):
    T = o.shape[0]
    blk = min(_ATT_BLK, T)
    nq = T // blk

    def body(do_ref, o_ref, d_ref):
        for h in range(HEADS):
            cols = slice(h * SLAB, (h + 1) * SLAB)
            dl = jnp.sum(do_ref[:, cols].astype(F32) * o_ref[:, cols], axis=-1, keepdims=True)
            d_ref[h] = jnp.broadcast_to(dl, (blk, SLAB)).T[0:_SUB, :]

    return pl.pallas_call(
        body,
        name=name,
        grid=(nq,),
        in_specs=[pl.BlockSpec((blk, HEADS * SLAB), lambda i: (i, 0))] * 2,
        out_specs=pl.BlockSpec((HEADS, None, _SUB, blk), lambda i: (0, i, 0, 0)),
        out_shape=jax.ShapeDtypeStruct((HEADS, nq, _SUB, blk), F32),
        compiler_params=_cparams(("parallel",)),
    )(do, o)


def _attn_bwd(q, kv, do, lse, delta, *, name, send=None):
    T = q.shape[0]
    blk = min(_ATT_BLK, T)
    nq = T // blk

    def body(q_ref, kv_ref, do_ref, lse_ref, dl_ref, dqt_ref, dkv_ref):
        j = pl.program_id(1)

        @pl.when(j == 0)
        def _():
            dqt_ref[...] = jnp.zeros_like(dqt_ref)

        kb, vb = kv_ref[:, 0:SLAB], kv_ref[:, SLAB:2 * SLAB]
        kbt = kb.astype(F32).T.astype(BF16)
        kpos = j * blk + lax.broadcasted_iota(jnp.int32, (blk, blk), 0)

        def step(i, carry, masked):
            dk, dv = carry
            rows = pl.ds(pl.multiple_of(i * blk, blk), blk)
            qb, dob = q_ref[rows, :], do_ref[rows, :]
            st = _dot(kb, qb, "nt")
            if masked:
                qpos = i * blk + lax.broadcasted_iota(jnp.int32, (blk, blk), 1)
                st = jnp.where(kpos <= qpos, st, -jnp.inf)
            pt = jnp.exp2(st - (lse_ref[i, 0:1, :] - math.log2(math.log(2.0))))
            dpt = _dot(vb, dob, "nt")
            dst = pt * (dpt - dl_ref[i, 0:1, :])
            dv = dv + _dot(pt, dob, "nn")
            dk = dk + _dot(dst, qb, "nn")
            dqt_ref[i] += _dot(kbt, dst, "nn")
            return dk, dv

        zero = jnp.zeros((blk, SLAB), F32)
        carry = step(j, (zero, zero), True)
        rest = nq - 1 - j

        def trip(t, c):
            for u in range(_ATT_BWD_TRIP):
                c = step(j + 1 + _ATT_BWD_TRIP * t + u, c, False)
            return c

        carry = lax.fori_loop(0, rest // _ATT_BWD_TRIP, trip, carry)
        tail = j + 1 + (rest // _ATT_BWD_TRIP) * _ATT_BWD_TRIP
        dk, dv = lax.fori_loop(tail, nq, functools.partial(step, masked=False), carry)
        dkv_ref[:, 0:SLAB] = dk.astype(BF16)
        dkv_ref[:, SLAB:2 * SLAB] = (dv * (1.0 / math.log(2.0))).astype(BF16)

    stat_spec = pl.BlockSpec((None, nq, _SUB, blk), lambda h, j: (h, 0, 0, 0))
    head_t_spec = pl.BlockSpec((None, nq, SLAB, blk), lambda h, j: (h, 0, 0, 0))
    return _call_with_exchange(
        body, send, False,
        name=name,
        grid=(HEADS, nq),
        in_specs=[pl.BlockSpec((T, SLAB), lambda h, j: (0, h)), pl.BlockSpec((blk, 2 * SLAB), lambda h, j: (j, h)),
                  pl.BlockSpec((T, SLAB), lambda h, j: (0, h)), stat_spec, stat_spec],
        out_specs=[head_t_spec, pl.BlockSpec((blk, 2 * SLAB), lambda h, j: (j, h))],
        out_shape=[jax.ShapeDtypeStruct((HEADS, nq, SLAB, blk), F32), jax.ShapeDtypeStruct((T, HEADS * 2 * SLAB), BF16)],
        args=(q, kv, do, lse, delta))


def _qrope_bwd(dqt, cq, sq, *, name):
    _, nq, _, blk = dqt.shape
    T = nq * blk

    def body(dqt_ref, c_ref, s_ref, dy_ref):
        c, s = c_ref[...], s_ref[...]
        for h in range(HEADS):
            dq = dqt_ref[h].T
            dy_ref[:, h * SLAB:(h + 1) * SLAB] = (dq * c).astype(BF16)
            dy_ref[:, (HEADS + h) * SLAB:(HEADS + h + 1) * SLAB] = (dq * s).astype(BF16)

    return pl.pallas_call(
        body,
        name=name,
        grid=(nq,),
        in_specs=[pl.BlockSpec((HEADS, None, SLAB, blk), lambda i: (0, i, 0, 0)), _row_spec(cq, blk), _row_spec(sq, blk)],
        out_specs=pl.BlockSpec((blk, 2 * HEADS * SLAB), lambda i: (i, 0)),
        out_shape=jax.ShapeDtypeStruct((T, 2 * HEADS * SLAB), BF16),
        compiler_params=_cparams(("parallel",)),
    )(dqt, cq, sq)


def _loss_head(x, tgt, g, *, name, tm=256):
    T = x.shape[0]
    tm = min(tm, T)

    def body(x_ref, t_ref, g_ref, loss_ref, dx_ref, dxh_ref, dg_ref):
        def f(xv, gv):
            e = _rms(xv, gv, D) - t_ref[...]
            row = jnp.sum(e * e, axis=1, keepdims=True)
            return jnp.sum(row, axis=0, keepdims=True) * (0.5 / D)

        val, vjp = jax.vjp(f, x_ref[...], g_ref[...])
        dxv, dgv = vjp(jnp.ones((1, 1), F32))

        @pl.when(pl.program_id(0) == 0)
        def _():
            loss_ref[...] = jnp.zeros_like(loss_ref)
            dg_ref[...] = jnp.zeros_like(dg_ref)

        dx_ref[...] = dxv
        dxh_ref[...] = dxv.astype(BF16)
        dg_ref[...] += dgv
        loss_ref[...] += jnp.broadcast_to(val, loss_ref.shape)

    return pl.pallas_call(
        body,
        name=name,
        grid=(T // tm,),
        in_specs=[_row_spec(x, tm), _row_spec(tgt, tm), _full_spec(g)],
        out_specs=[pl.BlockSpec((1, SLAB), lambda i: (0, 0)), _row_spec(x, tm), _row_spec(x, tm), _full_spec(g)],
        out_shape=[jax.ShapeDtypeStruct((1, SLAB), F32), jax.ShapeDtypeStruct(x.shape, F32),
                   jax.ShapeDtypeStruct(x.shape, BF16), jax.ShapeDtypeStruct(g.shape, F32)],
        compiler_params=_cparams(("arbitrary",)),
    )(x, tgt, g)


def _row_tile(rows, cols, budget=256 * 1024):
    best = None
    for t in range(16, rows + 1, 16):
        if rows % t == 0 and t * cols <= budget:
            best = t
    return best or rows


def _sum_fixed(x, out_dtype, *, name):
    n, R, C = x.shape
    tr = _row_tile(R, C)

    def body(x_ref, o_ref):
        acc = x_ref[0].astype(F32)
        for k in range(1, n):
            acc = acc + x_ref[k].astype(F32)
        o_ref[...] = acc.astype(o_ref.dtype)

    return pl.pallas_call(
        body,
        name=name,
        grid=(R // tr,),
        in_specs=[pl.BlockSpec((n, tr, C), lambda i: (0, i, 0))],
        out_specs=pl.BlockSpec((tr, C), lambda i: (i, 0)),
        out_shape=jax.ShapeDtypeStruct((R, C), out_dtype),
        compiler_params=_cparams(("parallel",)),
    )(x)


def _adamw(w, g, m, v, *, name):
    R, C = w.shape
    tr = _row_tile(R, C, 128 * 1024)

    def body(w_ref, g_ref, m_ref, v_ref, d_ref, nm_ref, nv_ref):
        gv = g_ref[...]
        mv = B1 * m_ref[...] + (1.0 - B1) * gv
        vv = B2 * v_ref[...] + (1.0 - B2) * (gv * gv)
        m_hat = mv / (1.0 - B1 ** STEP)
        v_hat = vv / (1.0 - B2 ** STEP)
        d_ref[...] = -LR * (m_hat / (jnp.sqrt(v_hat) + AEPS) + WD * w_ref[...])
        nm_ref[...] = mv
        nv_ref[...] = vv

    spec = pl.BlockSpec((tr, C), lambda i: (i, 0))
    return pl.pallas_call(
        body, name=name, grid=(R // tr,), in_specs=[spec] * 4, out_specs=[spec] * 3,
        out_shape=[jax.ShapeDtypeStruct((R, C), F32)] * 3, compiler_params=_cparams(("parallel",)),
    )(w, g, m, v)


_FLIPS = ((1, 0), (0, 1), (1, 1))
_ANY = pl.BlockSpec(memory_space=pl.ANY)


def _me():
    return lax.axis_index("x"), lax.axis_index("y"), lax.axis_index("c")


def _flip(mx, my, f):
    return (1 - mx if f[0] else mx), (1 - my if f[1] else my)


_CHIP_SEMS = [pltpu.SemaphoreType.DMA((3,)), pltpu.SemaphoreType.DMA((3,)), pltpu.SemaphoreType.DMA]


def _chip_copies(x_ref, out_ref, send_sems, recv_sems, local_sem, gather):
    mx, my, mc = _me()
    mine = 2 * mx + my
    outgoing, incoming = [], []
    for k, f in enumerate(_FLIPS):
        px, py = _flip(mx, my, f)
        peer = 2 * px + py
        src = x_ref if gather else x_ref.at[peer]
        for dst, to in ((out_ref.at[mine], outgoing), (out_ref.at[peer], incoming)):
            to.append(pltpu.make_async_remote_copy(src_ref=src, dst_ref=dst, send_sem=send_sems.at[k],
                                                   recv_sem=recv_sems.at[k], device_id=(px, py, mc), device_id_type=MESH))
    local = None if gather else pltpu.make_async_copy(x_ref.at[mine], out_ref.at[mine], local_sem)
    return outgoing, incoming, local


def _chip_start(*refs, gather):
    outgoing, _, local = _chip_copies(*refs, gather)
    if local is not None:
        local.start()
    for cp in outgoing:
        cp.start()


def _chip_wait(*refs, gather):
    outgoing, incoming, local = _chip_copies(*refs, gather)
    for cp in incoming:
        cp.wait_recv()
    for cp in outgoing:
        cp.wait_send()
    if local is not None:
        local.wait()


def _chip_exchange(x, gather, *, name):
    shape = x.shape if not gather else (4,) + x.shape

    def body(*refs):
        _chip_start(*refs, gather=gather)
        _chip_wait(*refs, gather=gather)

    return pl.pallas_call(
        body, name=name, in_specs=[_ANY], out_specs=_ANY, out_shape=jax.ShapeDtypeStruct(shape, x.dtype),
        scratch_shapes=_CHIP_SEMS,
    )(x)


def _core_exchange(x, half, *, name):
    shape = x.shape[1:] if half else x.shape

    def body(x_ref, out_ref, send_sem, recv_sem):
        mx, my, mc = _me()
        src = x_ref.at[1 - mc] if half else x_ref
        cp = pltpu.make_async_remote_copy(src_ref=src, dst_ref=out_ref, send_sem=send_sem, recv_sem=recv_sem,
                                          device_id=(mx, my, 1 - mc), device_id_type=MESH)
        cp.start()
        cp.wait()

    return pl.pallas_call(
        body, name=name, in_specs=[_ANY], out_specs=_ANY, out_shape=jax.ShapeDtypeStruct(shape, x.dtype),
        scratch_shapes=[pltpu.SemaphoreType.DMA, pltpu.SemaphoreType.DMA],
    )(x)


def _core_gather(x, g4, *, name):
    def body(x_ref, g_ref, out_ref, send_sems, recv_sems):
        mx, my, mc = _me()
        srcs = [(x_ref, 2 * mx + my)] + [(g_ref.at[2 * px + py], 2 * px + py) for px, py in (_flip(mx, my, f) for f in _FLIPS)]
        copies = [pltpu.make_async_remote_copy(src_ref=src, dst_ref=out_ref.at[slot], send_sem=send_sems.at[k],
                                               recv_sem=recv_sems.at[k], device_id=(mx, my, 1 - mc), device_id_type=MESH)
                  for k, (src, slot) in enumerate(srcs)]
        for cp in copies:
            cp.start()
        for cp in copies:
            cp.wait()

    return pl.pallas_call(
        body, name=name, in_specs=[_ANY, _ANY], out_specs=_ANY, out_shape=jax.ShapeDtypeStruct(g4.shape, g4.dtype),
        scratch_shapes=[pltpu.SemaphoreType.DMA((4,)), pltpu.SemaphoreType.DMA((4,))],
    )(x, g4)


def _finish_gather(x, g4, *, name):
    mx, my, _ = _me()
    own = (jnp.arange(4) == 2 * mx + my).reshape((4,) + (1,) * x.ndim)
    return jnp.where(own, x[None], g4), _core_gather(x, g4, name=name + "_cores")


def _by_core(mine, other):
    mc = lax.axis_index("c")
    return jnp.stack([jnp.where(mc == 0, mine, other), jnp.where(mc == 0, other, mine)])


def _all_gather(x, *, name):
    return _by_core(*_finish_gather(x, _chip_exchange(x, True, name=name + "_chips"), name=name))


def _add_own_half(x, got, *, name, out_dtype=BF16):
    _, R, C = x.shape
    tr = _row_tile(R, C)

    def body(c_ref, x_ref, g_ref, o_ref):
        o_ref[...] = (x_ref[...] + g_ref[...]).astype(o_ref.dtype)

    return pl.pallas_call(
        body,
        name=name,
        grid_spec=pltpu.PrefetchScalarGridSpec(
            num_scalar_prefetch=1, grid=(R // tr,),
            in_specs=[pl.BlockSpec((None, tr, C), lambda i, c: (c[0], i, 0)), pl.BlockSpec((tr, C), lambda i, c: (i, 0))],
            out_specs=pl.BlockSpec((tr, C), lambda i, c: (i, 0))),
        out_shape=jax.ShapeDtypeStruct((R, C), out_dtype),
        compiler_params=_cparams(("parallel",)),
    )(lax.axis_index("c").astype(jnp.int32).reshape(1), x, got)


def _chip_sums(x, *, name):
    _, _, R, C = x.shape
    got = _core_exchange(x, True, name=name + "_cores")
    return _add_own_half(x.reshape(2, 4 * R, C), got.reshape(4 * R, C), name=name + "_add").reshape(4, R, C)


def _all_reduce(x, *, name):
    g = _all_gather(x, name=name)
    return _sum_fixed(g.reshape((N_DEV,) + x.shape), F32, name=name + "_sum")


WEIGHTS = ['mix_norm_g', 'w_in', 'mla_q_norm_g', 'mla_kv_norm_g', 'mla_w_uq', 'mla_w_ukv', 'mla_out_g', 'ssm_conv_w',
           'ssm_conv_b', 'ssm_dt_bias', 'ssm_a_log', 'ssm_d', 'ssm_norm_g', 'lru_conv_w', 'lru_conv_b', 'lru_w_a',
           'lru_b_a', 'lru_w_i', 'lru_b_i', 'lru_lambda', 'lru_out_g', 'w_out', 'xattn_norm_g', 'mem_norm_g', 'w_mq',
           'w_mk', 'w_mv', 'w_mo', 'mlp_norm_g', 'w_mlp1', 'w_mlp2', 'final_norm_g']
ROW_SHARDED = ('w_in', 'w_out', 'w_mq', 'w_mk', 'w_mv', 'w_mo', 'w_mlp2')
COL_SHARDED = ('mla_w_uq', 'mla_w_ukv', 'w_mlp1')
BIG = tuple(n for n in WEIGHTS if n in ROW_SHARDED + COL_SHARDED)
CONV_SHARDED = ('ssm_conv_w', 'lru_conv_w')
SMALL = tuple(n for n in WEIGHTS if n not in BIG)
PACK_C = 1024


def _pack(arrs, dtype, lead=()):
    flat = jnp.concatenate([a.reshape(lead + (-1,)).astype(dtype) for a in arrs], axis=-1)
    n = flat.shape[-1]
    rows = -(-n // (16 * PACK_C)) * 16
    flat = jnp.pad(flat, [(0, 0)] * len(lead) + [(0, rows * PACK_C - n)])
    return flat.reshape(lead + (rows, PACK_C))


def _unpack(packed, shapes, lead=()):
    flat = packed.reshape(lead + (-1,))
    out, off = [], 0
    for s in shapes:
        n = math.prod(s)
        out.append(flat[..., off:off + n].reshape(lead + tuple(s)))
        off += n
    return out


def _pad_lanes(v, n=SLAB):
    return jnp.pad(v.astype(F32), (0, n - v.shape[0])).reshape(1, n)


PIECES = ('win', 'wq', 'wkv', 'wout', 'w_mq', 'w_mk', 'w_mv', 'w_mo', 'w_mlp1', 'w_mlp2')
PIECE_SOURCE = {'win': 'w_in', 'wq': 'mla_w_uq', 'wkv': 'mla_w_ukv', 'wout': 'w_out'}
PIECE_SHAPE = {'win': (128, 2048), 'wq': (Q_RANK, 2 * SLAB), 'wkv': (KV_RANK, 2 * SLAB), 'wout': (128, D),
               'w_mq': (128, D), 'w_mk': (128, D), 'w_mv': (128, D), 'w_mo': (128, D), 'w_mlp1': (512, D),
               'w_mlp2': (512, D)}
PIECE_COLS = ('wq', 'wkv')


def _k_win(w):
    kr = w[..., 384:416]
    zc = lambda k: jnp.zeros(w.shape[:-1] + (k,), w.dtype)
    return jnp.concatenate(
        [w[..., 0:384], kr, zc(96), kr[..., 16:32], kr[..., 0:16], zc(96),
         w[..., 416:672], w[..., 1444:1700], w[..., 1184:1188], zc(124),
         w[..., 672:1184], w[..., 1188:1444]], axis=-1)


def _k_win_inv(m):
    gk = m[..., 384:416] + jnp.concatenate([m[..., 528:544], m[..., 512:528]], axis=-1)
    return jnp.concatenate([m[..., 0:384], gk, m[..., 640:896], m[..., 1280:1792], m[..., 1152:1156], m[..., 1792:2048],
                            m[..., 896:1152]], axis=-1)


def _k_wq(w):
    nh = w.shape[-1] // (NOPE + ROPE)
    w = w.reshape(w.shape[:-1] + (nh, NOPE + ROPE))
    nope, r1, r2 = w[..., :NOPE], w[..., NOPE:NOPE + 16], w[..., NOPE + 16:]
    z = lambda k: jnp.zeros(w.shape[:-1] + (k,), w.dtype)
    both = jnp.stack([jnp.concatenate([nope, r1, r2, z(32)], -1), jnp.concatenate([z(64), r2, r1, z(32)], -1)], axis=-3)
    return both.reshape(w.shape[:-2] + (2 * nh * SLAB,))


def _k_wq_inv(m):
    nh = m.shape[-1] // (2 * SLAB)
    m = m.reshape(m.shape[:-1] + (2, nh, SLAB))
    q0, q1 = m[..., 0, :, :], m[..., 1, :, :]
    w = jnp.concatenate([q0[..., :64], q0[..., 64:80] + q1[..., 80:96], q0[..., 80:96] + q1[..., 64:80]], -1)
    return w.reshape(w.shape[:-2] + (nh * (NOPE + ROPE),))


def _k_wkv(w):
    nh = w.shape[-1] // (NOPE + VDIM)
    w = w.reshape(w.shape[:-1] + (nh, NOPE + VDIM))
    z = jnp.zeros(w.shape[:-1] + (64,), w.dtype)
    return jnp.concatenate([w[..., :NOPE], z, w[..., NOPE:], z], -1).reshape(w.shape[:-2] + (nh * 2 * SLAB,))


def _k_wkv_inv(m):
    nh = m.shape[-1] // (2 * SLAB)
    m = m.reshape(m.shape[:-1] + (nh, 2 * SLAB))
    return jnp.concatenate([m[..., :NOPE], m[..., SLAB:SLAB + VDIM]], -1).reshape(m.shape[:-2] + (nh * (NOPE + VDIM),))


def _k_swap(w):
    return jnp.swapaxes(w, -1, -2)


_K_FWD = {'win': _k_win, 'wq': _k_wq, 'wkv': _k_wkv, 'w_mlp1': _k_swap}
_K_INV = {'win': _k_win_inv, 'wq': _k_wq_inv, 'wkv': _k_wkv_inv, 'w_mlp1': _k_swap}


def _assemble(piece, g):
    _, _, a, b = g.shape
    if piece == 'wq':
        return g.reshape(2, 4, a, 2, SLAB).transpose(2, 3, 1, 0, 4).reshape(a, N_DEV * b)
    if piece in PIECE_COLS:
        return g.transpose(2, 1, 0, 3).reshape(a, N_DEV * b)
    return g.transpose(1, 0, 2, 3).reshape(N_DEV * a, b)


def _disassemble(piece, full):
    a, b = PIECE_SHAPE[piece]
    if piece == 'wq':
        return full.reshape(a, 2, 4, 2, SLAB).transpose(3, 2, 0, 1, 4).reshape(2, 4, a, b)
    if piece in PIECE_COLS:
        return full.reshape(a, 4, 2, b).transpose(2, 1, 0, 3)
    return full.reshape(4, 2, a, b).transpose(1, 0, 2, 3)


def _piece_rows(piece):
    a, b = PIECE_SHAPE[piece]
    return a * b // PACK_C


def _prep_late(pieces):
    if 'wout' in pieces:
        wout = pieces['wout']
        mla_rows = jnp.pad(wout[:HEADS * VDIM].reshape(HEADS, VDIM, D), ((0, 0), (0, SLAB - VDIM), (0, 0)))
        pieces['wout'] = jnp.concatenate([mla_rows.reshape(HEADS * SLAB, D), wout[HEADS * VDIM:]], axis=0)
    return pieces


def _prep_layer(pieces, Ws, l):
    P = _prep_late(dict(pieces))
    ri, ci = jnp.arange(SLAB)[:, None], jnp.arange(2 * SLAB)[None, :]
    sel = ((ri < ROPE) & (ci == ri + NOPE)).astype(P['wkv'].dtype)
    P['wkv'] = jnp.concatenate([P['wkv'], jnp.tile(sel, (1, HEADS))], axis=0)
    W = Ws
    row = lambda n: W[n][l].astype(F32).reshape(1, -1)
    for n in ('mix_norm_g', 'mla_q_norm_g', 'mla_kv_norm_g', 'ssm_norm_g', 'lru_lambda', 'lru_out_g', 'xattn_norm_g',
              'mem_norm_g', 'mlp_norm_g'):
        P[n] = row(n)
    P['mla_out_g'] = jnp.pad(W['mla_out_g'][l].astype(F32).reshape(HEADS, VDIM), ((0, 0), (0, SLAB - VDIM))).reshape(1, -1)
    for n in ('ssm_dt_bias', 'ssm_a_log', 'ssm_d'):
        P[n] = _pad_lanes(W[n][l])
    P['conv_w'] = jnp.pad(jnp.concatenate([W['ssm_conv_w'][l], W['lru_conv_w'][l]], axis=1).astype(F32), ((0, 4), (0, 0)))
    P['conv_b'] = jnp.concatenate([W['ssm_conv_b'][l], W['lru_conv_b'][l]]).astype(F32).reshape(1, -1)
    for n in ('lru_w_a', 'lru_w_i'):
        P[n] = jnp.concatenate([jnp.pad(W[n][l, k].astype(F32), ((0, 0), (64 * k, 192 - 64 * k))) for k in range(4)], axis=0)
    for n in ('lru_b_a', 'lru_b_i'):
        P[n] = W[n][l].astype(F32).reshape(1, -1)
    return P


def _unprep_pieces(G, names=PIECES):
    o = {n: G[n] for n in names}
    if 'wkv' in o:
        o['wkv'] = G['wkv'][:KV_RANK]
    if 'wout' in o:
        wo = G['wout']
        o['wout'] = jnp.concatenate([wo[:HEADS * SLAB].reshape(HEADS, SLAB, D)[:, :VDIM].reshape(HEADS * VDIM, D),
                                     wo[HEADS * SLAB:]], axis=0)
    return o


def _unprep_small(G):
    o = {}
    for n in ('mix_norm_g', 'mla_q_norm_g', 'mla_kv_norm_g', 'ssm_norm_g', 'lru_lambda', 'lru_out_g', 'xattn_norm_g',
              'mem_norm_g', 'mlp_norm_g', 'lru_b_a', 'lru_b_i'):
        o[n] = G[n].reshape(-1)
    o['lru_b_a'] = o['lru_b_a'].reshape(4, 64)
    o['lru_b_i'] = o['lru_b_i'].reshape(4, 64)
    o['mla_out_g'] = G['mla_out_g'].reshape(HEADS, SLAB)[:, :VDIM].reshape(-1)
    for n in ('ssm_dt_bias', 'ssm_a_log', 'ssm_d'):
        o[n] = G[n][0, :4]
    o['ssm_conv_w'], o['lru_conv_w'] = G['conv_w'][:4, :512], G['conv_w'][:4, 512:]
    o['ssm_conv_b'], o['lru_conv_b'] = G['conv_b'][0, :512], G['conv_b'][0, 512:]
    for n in ('lru_w_a', 'lru_w_i'):
        o[n] = jnp.stack([G[n][64 * k:64 * (k + 1), 64 * k:64 * (k + 1)] for k in range(4)])
    return o


def _rope_tables(positions):
    half = ROPE // 2
    inv_freq = ROPE_THETA ** (-jnp.arange(half, dtype=F32) * 2.0 / ROPE)
    ang = positions.astype(F32)[:, None] * inv_freq
    cos, sin = jnp.cos(ang), jnp.sin(ang)
    T = positions.shape[0]
    z = lambda k: jnp.zeros((T, k), F32)
    ck = jnp.concatenate([cos, cos, z(96)], axis=1)
    sk = jnp.concatenate([-sin, sin, z(96)], axis=1)
    cq = jnp.concatenate([jnp.ones((T, NOPE), F32), cos, cos, z(32)], axis=1)
    sq = jnp.concatenate([z(NOPE), -sin, sin, z(32)], axis=1)
    return ck, sk, cq * _ATT_SCALE2, sq * _ATT_SCALE2


def _add_epi(acc, res):
    return (acc + res,)


def _add_norm_epi(acc, res, g):
    x = acc + res
    return x, _rms(x, g, x.shape[-1])


def _norm_bwd_epi(acc, x, res, g):
    _, vjp = jax.vjp(lambda xv, gv: _rms(xv, gv, xv.shape[-1]), x, g)
    dx, dg = vjp(acc)
    dx = dx + res
    return dx, dx, dg


def _split_u_epi(acc):
    return acc[:, 0:U_MLA], acc[:, U_MLA:U_MLA + U_GATE], acc[:, U_MLA + U_GATE:]


def _relu2_epi(acc):
    r = jnp.maximum(acc, 0.0)
    return r, r * r


def _drelu2_epi(acc, r):
    return (acc * (2.0 * r.astype(F32)),)


def _norm(x, g, name):
    return _rows_fwd(_f_norm, [x], [g], [(x.shape[1], BF16)], name=name)[0]


def _norm_bwd(x, g, ct, add, name):
    (dx, dx16), (dg,) = _rows_vjp(_f_norm, [x], [g], [ct], name=name, drows=[0], dparams=[0], drow_dtypes=[F32],
                                  add=add, twin=True)
    return dx, dx16, dg


def _layer_fwd(x0, h1, mem, P, tabs, g_next=None, send=None, on_got=None):
    ck, sk, cq, sq = tabs
    S = {'x0': x0}
    if h1 is None:
        h1 = _norm(x0, P['mix_norm_g'], "norm_mix")
    S['h1'] = h1
    win = P['win']
    u_mla, u_gate, u_conv = _mm(h1, win, name="in_proj", epi=_split_u_epi, out_dtypes=(F32, F32, F32),
                                out_cols=(U_MLA, U_GATE, U_CONV), tm=512, tn=U_MLA + U_GATE + U_CONV)
    S['u_mla'], S['u_gate'], S['u_conv'] = u_mla, u_gate, u_conv
    cqn, akv = _rows_fwd(_f_mla_prep, [u_mla, ck, sk], [P['mla_q_norm_g'], P['mla_kv_norm_g']],
                         [(Q_RANK, BF16), (2 * SLAB, BF16)], name="mla_prep")
    S['cqn'], S['akv'] = cqn, akv
    yq = _mm(cqn, P['wq'], name="q_proj")
    q = S['q'] = _rows_fwd(_f_qrope, [yq, cq, sq], [], [(HEADS * SLAB, BF16)], name="q_rope")[0]
    kv = S['kv'] = _mm(akv, P['wkv'], name="kv_proj", out_dtypes=(BF16,))
    o, lse, *got = _attn_fwd(q, kv, name="attn_fwd" if send is None else "attn_fwd_gather", send=send)
    S['o'], S['lse'] = o, lse
    if got:
        P.update(_prep_late(on_got(got[0]) or {}))
    c_ssm, c_lru = _conv_fwd(u_conv, P['conv_w'], P['conv_b'], name="conv_fwd")
    S['c_ssm'], S['c_lru'] = c_ssm, c_lru
    ys, sall = _ssd_fwd(c_ssm, u_gate, P['ssm_dt_bias'], P['ssm_a_log'], P['ssm_d'], P['ssm_norm_g'], name="ssd_fwd")
    S['ys'], S['sall'] = ys, sall
    a, b = _rows_fwd(_f_lru_gates, [c_lru], [P['lru_w_a'], P['lru_b_a'], P['lru_w_i'], P['lru_b_i'], P['lru_lambda']],
                     [(256, F32), (256, F32)], name="lru_gates", tm=_MM_ROWS)
    h, hprev = _lru_scan_fwd(a, b, name="lru_scan")
    S['a'], S['h'], S['hprev'] = a, h, hprev
    ymix = S['ymix'] = _rows_fwd(_f_mix, [o, ys, h, u_gate], [P['mla_out_g'], P['lru_out_g']],
                                 [(HEADS * SLAB + 512, BF16)], name="mix")[0]
    x1, hx = _mm(ymix, P['wout'], name="out_proj", epi=_add_norm_epi, extras=(x0, P['xattn_norm_g']), out_dtypes=(F32, BF16))
    S['x1'], S['hx'] = x1, hx
    qx = S['qx'] = _mm(hx, P['w_mq'], name="mem_q", out_dtypes=(BF16,))
    mn = S['mn'] = _norm(mem, P['mem_norm_g'], "norm_mem")
    kx = S['kx'] = _mm(mn, P['w_mk'], name="mem_k", out_dtypes=(BF16,))
    vx = S['vx'] = _mm(mn, P['w_mv'], name="mem_v", out_dtypes=(BF16,))
    ox = S['ox'] = _rows_fwd(_f_xattn, [qx], [kx, vx], [(D, BF16)], name="xattn", tm=_MM_ROWS)[0]
    x2, hm = _mm(ox, P['w_mo'], name="mem_o", epi=_add_norm_epi, extras=(x1, P['mlp_norm_g']), out_dtypes=(F32, BF16))
    S['x2'], S['hm'] = x2, hm
    r, s = _mm(hm, P['w_mlp1'], "nt", name="mlp_up", epi=_relu2_epi, out_dtypes=(BF16, BF16))
    S['r'], S['s'] = r, s
    if g_next is None:
        x3, h_next = _mm(s, P['w_mlp2'], name="mlp_down_last", epi=_add_epi, extras=(x2,)), None
    else:
        x3, h_next = _mm(s, P['w_mlp2'], name="mlp_down", epi=_add_norm_epi, extras=(x2, g_next), out_dtypes=(F32, BF16))
    return x3, h_next, S


def _layer_bwd(dx3, dx3h, mem, S, P, tabs, send=None):
    ck, sk, cq, sq = tabs
    G = {}
    da = _mm(dx3h, P['w_mlp2'], "nt", name="mlp_down_dx", epi=_drelu2_epi, extras=(S['r'],), out_dtypes=(BF16,))
    G['w_mlp2'] = _mm(S['s'], dx3h, "tn", name="mlp_down_dw")
    G['w_mlp1'] = _mm(da, S['hm'], "tn", name="mlp_up_dw")
    norm_out = dict(epi=_norm_bwd_epi, out_dtypes=(F32, BF16), col_sums=1)
    dx2, dx2h, G['mlp_norm_g'] = _mm(da, P['w_mlp1'], "nn", name="mlp_up_dx", extras=(S['x2'], dx3, P['mlp_norm_g']),
                                     **norm_out)
    dox = _mm(dx2h, P['w_mo'], "nt", name="mem_o_dx")
    G['w_mo'] = _mm(S['ox'], dx2h, "tn", name="mem_o_dw")
    (dqx,), (dkx, dvx) = _rows_vjp(_f_xattn, [S['qx']], [S['kx'], S['vx']], [dox], name="xattn_bwd", drows=[0],
                                   dparams=[0, 1], drow_dtypes=[BF16], tm=_MM_ROWS)
    G['w_mq'] = _mm(S['hx'], dqx, "tn", name="mem_q_dw")
    dx1, dx1h, G['xattn_norm_g'] = _mm(dqx, P['w_mq'], "nt", name="mem_q_dx", extras=(S['x1'], dx2, P['xattn_norm_g']),
                                       **norm_out)
    G['w_mk'] = _mm(S['mn'], dkx, "tn", name="mem_k_dw")
    G['w_mv'] = _mm(S['mn'], dvx, "tn", name="mem_v_dw")
    dmn = _mm(dkx, P['w_mk'], "nt", name="mem_k_dx", epi=_add_epi, extras=(_mm(dvx, P['w_mv'], "nt", name="mem_v_dx"),))
    _, _, G['mem_norm_g'] = _norm_bwd(mem, P['mem_norm_g'], dmn, None, "norm_mem_bwd")
    dymix = _mm(dx1h, P['wout'], "nt", name="out_proj_dx")
    G['wout'] = _mm(S['ymix'], dx1h, "tn", name="out_proj_dw")
    (do, dys, dh, dug_mix), (G['mla_out_g'], G['lru_out_g']) = _rows_vjp(
        _f_mix, [S['o'], S['ys'], S['h'], S['u_gate']], [P['mla_out_g'], P['lru_out_g']], [dymix], name="mix_bwd",
        drows=[0, 1, 2, 3], dparams=[0, 1], drow_dtypes=[BF16, F32, F32, F32])
    g, da_lru = _lru_scan_bwd(S['a'], dh, S['hprev'], name="lru_scan_bwd")
    lru_par = [P['lru_w_a'], P['lru_b_a'], P['lru_w_i'], P['lru_b_i'], P['lru_lambda']]
    (dc_lru,), dpar = _rows_vjp(_f_lru_gates, [S['c_lru']], lru_par, [da_lru, g], name="lru_gates_bwd", drows=[0],
                                dparams=[0, 1, 2, 3, 4], drow_dtypes=[F32], tm=_MM_ROWS)
    G['lru_w_a'], G['lru_b_a'], G['lru_w_i'], G['lru_b_i'], G['lru_lambda'] = dpar
    dc_ssm, dug_ssd, G['ssm_dt_bias'], G['ssm_a_log'], G['ssm_d'], G['ssm_norm_g'] = _ssd_bwd(
        S['c_ssm'], S['u_gate'], S['sall'], dys, P['ssm_dt_bias'], P['ssm_a_log'], P['ssm_d'], P['ssm_norm_g'],
        name="ssd_bwd")
    du_conv, G['conv_w'], G['conv_b'] = _conv_bwd(S['u_conv'], dc_ssm, dc_lru, P['conv_w'], name="conv_bwd")
    delta = _attn_delta(do, S['o'], name="attn_delta")
    if callable(send):
        send = send(G)
    dqt, dkv, *got = _attn_bwd(S['q'], S['kv'], do, S['lse'], delta,
                               name="attn_bwd" if send is None else "attn_bwd_scatter", send=send)
    dyq = _qrope_bwd(dqt, cq, sq, name="q_rope_bwd")
    dcqn = _mm(dyq, P['wq'], "nt", name="q_proj_dx")
    G['wq'] = _mm(S['cqn'], dyq, "tn", name="q_proj_dw")
    dakv = _mm(dkv, P['wkv'], "nt", name="kv_proj_dx")
    G['wkv'] = _mm(S['akv'], dkv, "tn", name="kv_proj_dw")
    (du_mla,), (G['mla_q_norm_g'], G['mla_kv_norm_g']) = _rows_vjp(
        _f_mla_prep, [S['u_mla'], ck, sk], [P['mla_q_norm_g'], P['mla_kv_norm_g']], [dcqn, dakv], name="mla_prep_bwd",
        drows=[0], dparams=[0, 1], drow_dtypes=[BF16])
    du = jnp.concatenate([du_mla, (dug_mix + dug_ssd).astype(BF16), du_conv.astype(BF16)], axis=1)
    G['win'] = _mm(S['h1'], du, "tn", name="in_dw")
    dx0, dx0h, G['mix_norm_g'] = _mm(du, P['win'], "nt", name="in_dx", extras=(S['x0'], dx1, P['mix_norm_g']), **norm_out)
    return dx0, dx0h, G, (got[0] if got else None)


class _NoExchange:
    def __init__(self, layers):
        self.layers = layers

    def pieces(self, l):
        return self.layers[l]

    def fwd_send(self, l):
        return None

    def fwd_got(self, l, got):
        pass

    def bwd_send(self, l, G):
        return None

    def bwd_got(self, l, got):
        pass

    def grads_ready(self, l, pieces):
        pass


def _local_step(x, mem, positions, ex, Ws, tgt):
    tabs = _rope_tables(positions)
    saved, preps, h = [], [], None
    for l in range(DEPTH):
        P = _prep_layer(ex.pieces(l), Ws, l)
        send = ex.fwd_send(l)
        g_next = Ws['mix_norm_g'][l + 1].astype(F32).reshape(1, D) if l + 1 < DEPTH else None
        x, h, S = _layer_fwd(x, h, mem, P, tabs, g_next, send, functools.partial(ex.fwd_got, l))
        saved.append(S)
        preps.append(P)
    loss, dx, dxh, dg_final = _loss_head(x, tgt, Ws['final_norm_g'].astype(F32).reshape(1, D), name="loss_head")
    pieces, small = [None] * DEPTH, [None] * DEPTH
    for l in reversed(range(DEPTH)):
        dx, dxh, G, got = _layer_bwd(dx, dxh, mem, saved[l], preps[l], tabs, functools.partial(ex.bwd_send, l))
        if got is not None:
            ex.bwd_got(l, got)
        pieces[l], small[l] = _unprep_pieces(G), _unprep_small(G)
        ex.grads_ready(l, pieces[l])
    grads = {n: jnp.stack([small[l][n] for l in range(DEPTH)]) for n in SMALL if n != 'final_norm_g'}
    grads['final_norm_g'] = dg_final.reshape(D)
    return loss, dx, pieces, grads


def _pack_rows(pieces, names=PIECES):
    return jnp.concatenate([pieces[n].reshape(pieces[n].shape[:-2] + (-1, PACK_C)) for n in names], axis=-2)


def _unpack_rows(packed, lead=(), names=PIECES):
    out, off = {}, 0
    for n in names:
        rows = _piece_rows(n)
        out[n] = packed[..., off:off + rows, :].reshape(lead + PIECE_SHAPE[n])
        off += rows
    return out


EARLY = ('win', 'wq', 'wkv')
LATE = tuple(n for n in PIECES if n not in EARLY)


class _StepExchange(_NoExchange):
    FIRST = [(0, EARLY)]
    RIDES = {0: [(0, LATE), (1, PIECES), (2, PIECES)], 1: [(3, PIECES)]}

    def __init__(self, shard):
        self.shard = {n: a.astype(BF16) for n, a in shard.items()}
        self.layers = {l: {} for l in range(DEPTH)}
        self.sums, self.reduced = {}, {}
        first = self._pack(self.FIRST)
        self._take(*_finish_gather(first, _chip_exchange(first, True, name="gather_w0_chips"), name="gather_w0"), self.FIRST)

    def _pack(self, spec):
        return jnp.concatenate([_pack_rows({n: self.shard[n][l] for n in names}, names) for l, names in spec], axis=0)

    def _take(self, mine, other, spec):
        off = 0
        for l, names in spec:
            rows = sum(_piece_rows(n) for n in names)
            a, b = (_unpack_rows(g[:, off:off + rows], (4,), names) for g in (mine, other))
            self.layers[l].update({n: _assemble(n, _by_core(a[n], b[n])) for n in names})
            off += rows

    def fwd_send(self, l):
        return self._pack(self.RIDES[l]) if l in self.RIDES else None

    def fwd_got(self, l, got):
        self._take(*_finish_gather(self._pack(self.RIDES[l]), got, name="gather_w"), self.RIDES[l])
        return {n: self.layers[l][n] for n in LATE} if (l, LATE) in self.RIDES[l] else {}

    @staticmethod
    def _chips_first(pieces, names):
        x = _pack_rows({n: _disassemble(n, pieces[n]) for n in names}, names)
        return x.transpose(1, 0, 2, 3).astype(BF16).reshape(4, -1, PACK_C)

    @staticmethod
    def _cores_last(got):
        both = _sum_fixed(got, F32, name="scatter_g_sum").reshape(2, -1, PACK_C)
        other = _core_exchange(both, True, name="scatter_g_cores")
        return _add_own_half(both, other, name="scatter_g_add", out_dtype=F32)

    def grads_ready(self, l, pieces):
        if l > 0:
            self.sums[l] = self._chips_first(pieces, PIECES)
            return
        x = _pack_rows({n: _disassemble(n, pieces[n]) for n in EARLY}, EARLY)
        parts = _chip_exchange(_chip_sums(x, name="scatter_g0"), False, name="scatter_g0_chips")
        self.reduced[0] = jnp.concatenate([_sum_fixed(parts, F32, name="scatter_g0_sum"), self.late0], axis=0)

    def bwd_send(self, l, G):
        send = self.sums.pop(l + 1, None)
        if l == 0:
            send = jnp.concatenate([send, self._chips_first(_unprep_pieces(G, LATE), LATE)], axis=1)
        return send

    def bwd_got(self, l, got):
        rows = 2 * sum(_piece_rows(n) for n in PIECES)
        self.reduced[l + 1] = self._cores_last(got[:, :rows])
        if l == 0:
            self.late0 = self._cores_last(got[:, rows:])


def _adamw_nd(w, g, m, v, name):
    shp = w.shape
    two = lambda a: a.reshape(-1, shp[-1])
    return [r.reshape(shp) for r in _adamw(two(w), two(g), two(m), two(v), name=name)]


def kernel(x, mem, positions, mix_norm_g, w_in, mla_q_norm_g, mla_kv_norm_g, mla_w_uq, mla_w_ukv, mla_out_g, ssm_conv_w, ssm_conv_b, ssm_dt_bias, ssm_a_log, ssm_d, ssm_norm_g, lru_conv_w, lru_conv_b, lru_w_a, lru_b_a, lru_w_i, lru_b_i, lru_lambda, lru_out_g, w_out, xattn_norm_g, mem_norm_g, w_mq, w_mk, w_mv, w_mo, mlp_norm_g, w_mlp1, w_mlp2, final_norm_g, loss_target, m_mix_norm_g, m_w_in, m_mla_q_norm_g, m_mla_kv_norm_g, m_mla_w_uq, m_mla_w_ukv, m_mla_out_g, m_ssm_conv_w, m_ssm_conv_b, m_ssm_dt_bias, m_ssm_a_log, m_ssm_d, m_ssm_norm_g, m_lru_conv_w, m_lru_conv_b, m_lru_w_a, m_lru_b_a, m_lru_w_i, m_lru_b_i, m_lru_lambda, m_lru_out_g, m_w_out, m_xattn_norm_g, m_mem_norm_g, m_w_mq, m_w_mk, m_w_mv, m_w_mo, m_mlp_norm_g, m_w_mlp1, m_w_mlp2, m_final_norm_g, v_mix_norm_g, v_w_in, v_mla_q_norm_g, v_mla_kv_norm_g, v_mla_w_uq, v_mla_w_ukv, v_mla_out_g, v_ssm_conv_w, v_ssm_conv_b, v_ssm_dt_bias, v_ssm_a_log, v_ssm_d, v_ssm_norm_g, v_lru_conv_w, v_lru_conv_b, v_lru_w_a, v_lru_b_a, v_lru_w_i, v_lru_b_i, v_lru_lambda, v_lru_out_g, v_w_out, v_xattn_norm_g, v_mem_norm_g, v_w_mq, v_w_mk, v_w_mv, v_w_mo, v_mlp_norm_g, v_w_mlp1, v_w_mlp2, v_final_norm_g):
    a = locals()
    w = {n: a[n] for n in WEIGHTS}
    m = {n: a['m_' + n] for n in WEIGHTS}
    v = {n: a['v_' + n] for n in WEIGHTS}
    me = 4 * lax.axis_index("x") + 2 * lax.axis_index("y") + lax.axis_index("c")

    ex = _StepExchange({n: _K_FWD.get(n, lambda a: a)(w[PIECE_SOURCE.get(n, n)]) for n in PIECES})
    Ws = {}
    conv_shapes = [w[n].shape for n in CONV_SHARDED]
    conv_g = _all_gather(_pack([w[n] for n in CONV_SHARDED], F32), name="gather_conv")
    conv_g = conv_g.transpose(1, 0, 2, 3).reshape((N_DEV,) + conv_g.shape[2:])
    for n, g in zip(CONV_SHARDED, _unpack(conv_g, conv_shapes, lead=(N_DEV,))):
        Ws[n] = g.transpose(1, 2, 0, 3).reshape(g.shape[1], g.shape[2], N_DEV * g.shape[3])
    for n in SMALL:
        if n not in CONV_SHARDED:
            Ws[n] = w[n]

    loss_share, dx, _, grads = _local_step(x[0], mem[0], positions[0], ex, Ws, loss_target[0])
    loss = lax.psum(loss_share[0, 0], ("x", "y", "c"))

    g_out = {}
    for n in PIECES:
        g = jnp.stack([_unpack_rows(ex.reduced[l])[n] for l in range(DEPTH)])
        g_out[PIECE_SOURCE.get(n, n)] = _K_INV[n](g) if n in _K_INV else g
    small_shapes = [grads[n].shape for n in SMALL]
    g_small = _all_reduce(_pack([grads[n] for n in SMALL], F32), name="reduce_g")
    for n, g in zip(SMALL, _unpack(g_small, small_shapes)):
        if n in CONV_SHARDED:
            cols = w[n].shape[-1]
            g = lax.dynamic_slice_in_dim(g, me * cols, cols, axis=2)
        g_out[n] = g

    delta, new_m, new_v = {}, {}, {}
    for n in BIG:
        delta[n], new_m[n], new_v[n] = _adamw_nd(w[n], g_out[n], m[n], v[n], "adamw_" + n)
    shapes = [w[n].shape for n in SMALL]
    packed = [_pack([d[n] for n in SMALL], F32) for d in (w, g_out, m, v)]
    for d, res in zip((delta, new_m, new_v), _adamw(*packed, name="adamw_small")):
        d.update(zip(SMALL, _unpack(res, shapes)))

    return (loss, dx[None], *[g_out[n] for n in WEIGHTS], *[delta[n] for n in WEIGHTS],
            *[new_m[n] for n in WEIGHTS], *[new_v[n] for n in WEIGHTS])
```
